```python
import jax, jax.numpy as jnp
from jax import lax
import numpy as np

D_MODEL = 1024
BATCH = 8
SEQ = 4096
DEPTH = 4

PLE_DIM = 256
D_MIX = D_MODEL
W_GRP = D_MIX // 4
N_HEADS_GRP = 4
HEAD_DIM = W_GRP // N_HEADS_GRP
GMLP_CHUNK = 128
RGLRU_CONV = 4
RGLRU_C = 8.0
HGRN_CHUNK = 64
POOL_WINDOWS = (2, 4, 8, 16)
D_FF = 2816
FFN_CONV = 3
EPS = 1e-6
COLS_A = 2 * W_GRP
COLS_B = 2 * W_GRP
COLS_C = 4 * W_GRP
COLS_D = W_GRP
OFF_B = COLS_A
OFF_C = OFF_B + COLS_B
OFF_D = OFF_C + COLS_C
D_PROJ = OFF_D + COLS_D

kernel_name = "hymba_style_gmlp_rglru_hgrn2_pool_hybrid"


def rms_norm(x, g):
    xf = x.astype(jnp.float32)
    y = xf * lax.rsqrt(jnp.mean(xf * xf, axis=-1, keepdims=True) + EPS)
    return (y * g.astype(jnp.float32)).astype(x.dtype)


def causal_dwconv(x, w, b):
    k_width = w.shape[0]
    s = x.shape[1]
    xp = jnp.pad(x, ((0, 0), (k_width - 1, 0), (0, 0)))
    y = b
    for k in range(k_width):
        y = y + xp[:, k:k + s] * w[k]
    return y


def gmlp_mixer(ab, ln_g, ln_b, ws, bs):
    bsz, s, _ = ab.shape
    ab = jax.nn.gelu(ab)
    u, v = jnp.split(ab, 2, axis=-1)
    vf = v.astype(jnp.float32)
    mu = jnp.mean(vf, axis=-1, keepdims=True)
    var = jnp.mean(jnp.square(vf - mu), axis=-1, keepdims=True)
    vn = ((vf - mu) * lax.rsqrt(var + EPS) * ln_g.astype(jnp.float32) + ln_b.astype(jnp.float32)).astype(v.dtype)
    vn = vn.reshape(bsz, s // GMLP_CHUNK, GMLP_CHUNK, N_HEADS_GRP, HEAD_DIM)
    mask = jnp.tril(jnp.ones((GMLP_CHUNK, GMLP_CHUNK), dtype=bool))
    wm = jnp.where(mask, ws, jnp.zeros_like(ws))
    sv = jnp.einsum('hts,bnshd->bnthd', wm, vn) + bs.T[:, :, None]
    return u * sv.reshape(bsz, s, W_GRP)


def rglru_mixer(xb, gb, conv_w, conv_b, wa, ba, wx, bx, lam):
    bsz, s, _ = xb.shape
    xc = causal_dwconv(xb, conv_w, conv_b)
    xh = xc.reshape(bsz, s, N_HEADS_GRP, HEAD_DIM)
    r = jax.nn.sigmoid(jnp.einsum('bshd,hde->bshe', xh, wa).reshape(bsz, s, W_GRP) + ba)
    i = jax.nn.sigmoid(jnp.einsum('bshd,hde->bshe', xh, wx).reshape(bsz, s, W_GRP) + bx)
    log_a = -RGLRU_C * r.astype(jnp.float32) * jax.nn.softplus(-lam.astype(jnp.float32))
    a = jnp.exp(log_a)
    mult = jnp.sqrt(-jnp.expm1(2.0 * log_a))
    bterm = mult * (i * xc).astype(jnp.float32)

    def combine(c1, c2):
        a1, b1 = c1
        a2, b2 = c2
        return a1 * a2, a2 * b1 + b2

    _, h = lax.associative_scan(combine, (a, bterm), axis=1)
    return h.astype(xb.dtype) * jax.nn.gelu(gb)


def hgrn2_mixer(q, f, i, g, lb, norm_g):
    bsz, s, _ = q.shape
    n_chunks = s // HGRN_CHUNK
    qf = jax.nn.silu(q.astype(jnp.float32))
    fgate = lb + (1.0 - lb) * jax.nn.sigmoid(f.astype(jnp.float32))
    log_f = jnp.log(fgate)
    kf = 1.0 - fgate
    vf = i.astype(jnp.float32)

    def to_chunks(t):
        return t.reshape(bsz, n_chunks, HGRN_CHUNK, N_HEADS_GRP, HEAD_DIM).transpose(1, 0, 3, 2, 4)

    qc, kc, vc = to_chunks(qf), to_chunks(kf), to_chunks(vf)
    bc = jnp.cumsum(to_chunks(log_f), axis=3)
    mask = jnp.tril(jnp.ones((HGRN_CHUNK, HGRN_CHUNK), dtype=bool))[:, :, None]

    def step(state, xs):
        qq, kk, vv, bb = xs
        diff = bb[:, :, :, None, :] - bb[:, :, None, :, :]
        decay = jnp.exp(jnp.where(mask, diff, -jnp.inf))
        att = jnp.einsum('bhtd,bhsd,bhtsd->bhts', qq, kk, decay)
        o = jnp.einsum('bhts,bhsv->bhtv', att, vv) + jnp.einsum('bhtd,bhdv->bhtv', qq * jnp.exp(bb), state)
        bl = bb[:, :, -1:, :]
        new_state = jnp.exp(bl[:, :, 0, :])[..., None] * state + jnp.einsum('bhsd,bhsv->bhdv', kk * jnp.exp(bl - bb), vv)
        return new_state, o

    s0 = jnp.zeros((bsz, N_HEADS_GRP, HEAD_DIM, HEAD_DIM), jnp.float32)
    _, o = lax.scan(step, s0, (qc, kc, vc, bc))
    o = o.transpose(1, 0, 3, 2, 4).reshape(bsz, s, N_HEADS_GRP, HEAD_DIM)
    o = o * lax.rsqrt(jnp.mean(o * o, axis=-1, keepdims=True) + EPS) * norm_g.astype(jnp.float32)
    o = o.reshape(bsz, s, W_GRP) * jax.nn.silu(g.astype(jnp.float32))
    return o.astype(q.dtype)


def pool_mixer(xd, wd, scale):
    bsz, s, _ = xd.shape
    xf = xd.astype(jnp.float32)
    cs = jnp.cumsum(xf, axis=1)
    pos = jnp.arange(1, s + 1, dtype=jnp.float32)[None, :, None]
    outs = []
    for j, w in enumerate(POOL_WINDOWS):
        c = cs[..., j * HEAD_DIM:(j + 1) * HEAD_DIM]
        shifted = jnp.pad(c, ((0, 0), (w, 0), (0, 0)))[:, :s]
        mean = (c - shifted) / jnp.minimum(pos, float(w))
        outs.append(mean - xf[..., j * HEAD_DIM:(j + 1) * HEAD_DIM])
    pooled = jnp.stack(outs, axis=2)
    y = jnp.einsum('bsgd,gde->bsge', pooled, wd.astype(jnp.float32)).reshape(bsz, s, W_GRP)
    return (y * scale.astype(jnp.float32)).astype(xd.dtype)


def _fwd_setup_inputs(seed: int = 0) -> dict:
    key = jax.random.key(seed)
    ks = jax.random.split(key, 32)

    def nrm(k, shape, scale):
        return jax.random.normal(k, shape, jnp.float32) * scale

    u = jax.random.uniform(ks[14], (DEPTH, W_GRP), jnp.float32, 0.9, 0.999)
    a_base = u ** (1.0 / RGLRU_C)
    b_lam = jnp.log(a_base) - jnp.log1p(-a_base)
    return {
        "x": nrm(ks[0], (BATCH, SEQ, D_MODEL), 1.0),
        "p": nrm(ks[1], (DEPTH, BATCH, SEQ, PLE_DIM), 1.0),
        "norm1_g": 1.0 + nrm(ks[2], (DEPTH, D_MODEL), 0.02),
        "w_in": nrm(ks[3], (DEPTH, D_MODEL, D_PROJ), D_MODEL ** -0.5),
        "a_ln_g": 1.0 + nrm(ks[4], (DEPTH, W_GRP), 0.02),
        "a_ln_b": nrm(ks[5], (DEPTH, W_GRP), 0.02),
        "a_ws": nrm(ks[6], (DEPTH, N_HEADS_GRP, GMLP_CHUNK, GMLP_CHUNK), GMLP_CHUNK ** -0.5),
        "a_bs": 1.0 + nrm(ks[7], (DEPTH, N_HEADS_GRP, GMLP_CHUNK), 0.1),
        "b_conv_w": nrm(ks[8], (DEPTH, RGLRU_CONV, W_GRP), RGLRU_CONV ** -0.5),
        "b_conv_b": nrm(ks[9], (DEPTH, W_GRP), 0.02),
        "b_wa": nrm(ks[10], (DEPTH, N_HEADS_GRP, HEAD_DIM, HEAD_DIM), HEAD_DIM ** -0.5),
        "b_ba": nrm(ks[11], (DEPTH, W_GRP), 0.02),
        "b_wx": nrm(ks[12], (DEPTH, N_HEADS_GRP, HEAD_DIM, HEAD_DIM), HEAD_DIM ** -0.5),
        "b_bx": nrm(ks[13], (DEPTH, W_GRP), 0.02),
        "b_lam": b_lam,
        "c_lb": nrm(ks[15], (DEPTH, W_GRP), 0.5),
        "c_norm_g": 1.0 + nrm(ks[16], (DEPTH, HEAD_DIM), 0.02),
        "d_w": nrm(ks[17], (DEPTH, N_HEADS_GRP, HEAD_DIM, HEAD_DIM), HEAD_DIM ** -0.5),
        "d_scale": 1.0 + nrm(ks[18], (DEPTH, W_GRP), 0.1),
        "w_out": nrm(ks[19], (DEPTH, D_MIX, D_MODEL), D_MIX ** -0.5),
        "norm2_g": 1.0 + nrm(ks[20], (DEPTH, D_MODEL), 0.02),
        "w_up": nrm(ks[21], (DEPTH, D_MODEL, 2 * D_FF), D_MODEL ** -0.5),
        "ffn_conv_w": nrm(ks[22], (DEPTH, FFN_CONV, 2 * D_FF), FFN_CONV ** -0.5),
        "ffn_conv_b": nrm(ks[23], (DEPTH, 2 * D_FF), 0.02),
        "w_down": nrm(ks[24], (DEPTH, D_FF, D_MODEL), D_FF ** -0.5),
        "norm3_g": 1.0 + nrm(ks[25], (DEPTH, D_MODEL), 0.02),
        "w_pe": nrm(ks[26], (DEPTH, PLE_DIM, D_MODEL), PLE_DIM ** -0.5),
        "w_pg": nrm(ks[27], (DEPTH, D_MODEL, D_MODEL), D_MODEL ** -0.5),
        "final_g": 1.0 + nrm(ks[28], (D_MODEL,), 0.02),
    }


def _fwd_reference(x, p, norm1_g, w_in, a_ln_g, a_ln_b, a_ws, a_bs, b_conv_w, b_conv_b, b_wa, b_ba, b_wx, b_bx, b_lam, c_lb, c_norm_g, d_w, d_scale, w_out, norm2_g, w_up, ffn_conv_w, ffn_conv_b, w_down, norm3_g, w_pe, w_pg, final_g):
    lbs = jnp.cumsum(jax.nn.softmax(c_lb.astype(jnp.float32), axis=0), axis=0)
    lbs = lbs - lbs[0:1]
    for l in range(DEPTH):
        h = rms_norm(x, norm1_g[l])
        z = h @ w_in[l]
        y_a = gmlp_mixer(z[..., :OFF_B], a_ln_g[l], a_ln_b[l], a_ws[l], a_bs[l])
        y_b = rglru_mixer(z[..., OFF_B:OFF_B + W_GRP], z[..., OFF_B + W_GRP:OFF_C],
                          b_conv_w[l], b_conv_b[l], b_wa[l], b_ba[l], b_wx[l], b_bx[l], b_lam[l])
        zc = z[..., OFF_C:OFF_D]
        y_c = hgrn2_mixer(zc[..., :W_GRP], zc[..., W_GRP:2 * W_GRP], zc[..., 2 * W_GRP:3 * W_GRP],
                          zc[..., 3 * W_GRP:], lbs[l], c_norm_g[l])
        y_d = pool_mixer(z[..., OFF_D:], d_w[l], d_scale[l])
        mix = jnp.concatenate([y_a, y_b, y_c, y_d], axis=-1)
        x = x + mix @ w_out[l]
        hf = rms_norm(x, norm2_g[l]) @ w_up[l]
        hf = causal_dwconv(hf, ffn_conv_w[l], ffn_conv_b[l])
        gt, val = jnp.split(hf, 2, axis=-1)
        x = x + (jax.nn.gelu(gt) * val) @ w_down[l]
        gate = jax.nn.sigmoid(rms_norm(x, norm3_g[l]) @ w_pg[l])
        x = x + (p[l] @ w_pe[l]) * gate
    return rms_norm(x, final_g)


import jax as _jax
import jax.numpy as _jnp

TWIN_FORMAT = 'train_step'
FWD_PARAMS = ['x', 'p', 'norm1_g', 'w_in', 'a_ln_g', 'a_ln_b', 'a_ws', 'a_bs', 'b_conv_w', 'b_conv_b', 'b_wa', 'b_ba', 'b_wx', 'b_bx', 'b_lam', 'c_lb', 'c_norm_g', 'd_w', 'd_scale', 'w_out', 'norm2_g', 'w_up', 'ffn_conv_w', 'ffn_conv_b', 'w_down', 'norm3_g', 'w_pe', 'w_pg', 'final_g']
TWIN_WEIGHTS = ['norm1_g', 'w_in', 'a_ln_g', 'a_ln_b', 'a_ws', 'a_bs', 'b_conv_w', 'b_conv_b', 'b_wa', 'b_ba', 'b_wx', 'b_bx', 'b_lam', 'c_lb', 'c_norm_g', 'd_w', 'd_scale', 'w_out', 'norm2_g', 'w_up', 'ffn_conv_w', 'ffn_conv_b', 'w_down', 'norm3_g', 'w_pe', 'w_pg', 'final_g']
TWIN_DIFF_INPUT = 'x'
TWIN_INPUTS = ['x', 'p', 'norm1_g', 'w_in', 'a_ln_g', 'a_ln_b', 'a_ws', 'a_bs', 'b_conv_w', 'b_conv_b', 'b_wa', 'b_ba', 'b_wx', 'b_bx', 'b_lam', 'c_lb', 'c_norm_g', 'd_w', 'd_scale', 'w_out', 'norm2_g', 'w_up', 'ffn_conv_w', 'ffn_conv_b', 'w_down', 'norm3_g', 'w_pe', 'w_pg', 'final_g', 'loss_target', 'm_norm1_g', 'm_w_in', 'm_a_ln_g', 'm_a_ln_b', 'm_a_ws', 'm_a_bs', 'm_b_conv_w', 'm_b_conv_b', 'm_b_wa', 'm_b_ba', 'm_b_wx', 'm_b_bx', 'm_b_lam', 'm_c_lb', 'm_c_norm_g', 'm_d_w', 'm_d_scale', 'm_w_out', 'm_norm2_g', 'm_w_up', 'm_ffn_conv_w', 'm_ffn_conv_b', 'm_w_down', 'm_norm3_g', 'm_w_pe', 'm_w_pg', 'm_final_g', 'v_norm1_g', 'v_w_in', 'v_a_ln_g', 'v_a_ln_b', 'v_a_ws', 'v_a_bs', 'v_b_conv_w', 'v_b_conv_b', 'v_b_wa', 'v_b_ba', 'v_b_wx', 'v_b_bx', 'v_b_lam', 'v_c_lb', 'v_c_norm_g', 'v_d_w', 'v_d_scale', 'v_w_out', 'v_norm2_g', 'v_w_up', 'v_ffn_conv_w', 'v_ffn_conv_b', 'v_w_down', 'v_norm3_g', 'v_w_pe', 'v_w_pg', 'v_final_g']
TWIN_OUTPUTS = ['loss', 'grad_x', 'grad_norm1_g', 'grad_w_in', 'grad_a_ln_g', 'grad_a_ln_b', 'grad_a_ws', 'grad_a_bs', 'grad_b_conv_w', 'grad_b_conv_b', 'grad_b_wa', 'grad_b_ba', 'grad_b_wx', 'grad_b_bx', 'grad_b_lam', 'grad_c_lb', 'grad_c_norm_g', 'grad_d_w', 'grad_d_scale', 'grad_w_out', 'grad_norm2_g', 'grad_w_up', 'grad_ffn_conv_w', 'grad_ffn_conv_b', 'grad_w_down', 'grad_norm3_g', 'grad_w_pe', 'grad_w_pg', 'grad_final_g', 'delta_norm1_g', 'delta_w_in', 'delta_a_ln_g', 'delta_a_ln_b', 'delta_a_ws', 'delta_a_bs', 'delta_b_conv_w', 'delta_b_conv_b', 'delta_b_wa', 'delta_b_ba', 'delta_b_wx', 'delta_b_bx', 'delta_b_lam', 'delta_c_lb', 'delta_c_norm_g', 'delta_d_w', 'delta_d_scale', 'delta_w_out', 'delta_norm2_g', 'delta_w_up', 'delta_ffn_conv_w', 'delta_ffn_conv_b', 'delta_w_down', 'delta_norm3_g', 'delta_w_pe', 'delta_w_pg', 'delta_final_g', 'new_m_norm1_g', 'new_m_w_in', 'new_m_a_ln_g', 'new_m_a_ln_b', 'new_m_a_ws', 'new_m_a_bs', 'new_m_b_conv_w', 'new_m_b_conv_b', 'new_m_b_wa', 'new_m_b_ba', 'new_m_b_wx', 'new_m_b_bx', 'new_m_b_lam', 'new_m_c_lb', 'new_m_c_norm_g', 'new_m_d_w', 'new_m_d_scale', 'new_m_w_out', 'new_m_norm2_g', 'new_m_w_up', 'new_m_ffn_conv_w', 'new_m_ffn_conv_b', 'new_m_w_down', 'new_m_norm3_g', 'new_m_w_pe', 'new_m_w_pg', 'new_m_final_g', 'new_v_norm1_g', 'new_v_w_in', 'new_v_a_ln_g', 'new_v_a_ln_b', 'new_v_a_ws', 'new_v_a_bs', 'new_v_b_conv_w', 'new_v_b_conv_b', 'new_v_b_wa', 'new_v_b_ba', 'new_v_b_wx', 'new_v_b_bx', 'new_v_b_lam', 'new_v_c_lb', 'new_v_c_norm_g', 'new_v_d_w', 'new_v_d_scale', 'new_v_w_out', 'new_v_norm2_g', 'new_v_w_up', 'new_v_ffn_conv_w', 'new_v_ffn_conv_b', 'new_v_w_down', 'new_v_norm3_g', 'new_v_w_pe', 'new_v_w_pg', 'new_v_final_g']
TWIN_LEAF_KINDS = {'loss': 'loss', 'grad_x': 'grad_x', 'grad_norm1_g': 'grad_w', 'grad_w_in': 'grad_w', 'grad_a_ln_g': 'grad_w', 'grad_a_ln_b': 'grad_w', 'grad_a_ws': 'grad_w', 'grad_a_bs': 'grad_w', 'grad_b_conv_w': 'grad_w', 'grad_b_conv_b': 'grad_w', 'grad_b_wa': 'grad_w', 'grad_b_ba': 'grad_w', 'grad_b_wx': 'grad_w', 'grad_b_bx': 'grad_w', 'grad_b_lam': 'grad_w', 'grad_c_lb': 'grad_w', 'grad_c_norm_g': 'grad_w', 'grad_d_w': 'grad_w', 'grad_d_scale': 'grad_w', 'grad_w_out': 'grad_w', 'grad_norm2_g': 'grad_w', 'grad_w_up': 'grad_w', 'grad_ffn_conv_w': 'grad_w', 'grad_ffn_conv_b': 'grad_w', 'grad_w_down': 'grad_w', 'grad_norm3_g': 'grad_w', 'grad_w_pe': 'grad_w', 'grad_w_pg': 'grad_w', 'grad_final_g': 'grad_w', 'delta_norm1_g': 'delta_w', 'delta_w_in': 'delta_w', 'delta_a_ln_g': 'delta_w', 'delta_a_ln_b': 'delta_w', 'delta_a_ws': 'delta_w', 'delta_a_bs': 'delta_w', 'delta_b_conv_w': 'delta_w', 'delta_b_conv_b': 'delta_w', 'delta_b_wa': 'delta_w', 'delta_b_ba': 'delta_w', 'delta_b_wx': 'delta_w', 'delta_b_bx': 'delta_w', 'delta_b_lam': 'delta_w', 'delta_c_lb': 'delta_w', 'delta_c_norm_g': 'delta_w', 'delta_d_w': 'delta_w', 'delta_d_scale': 'delta_w', 'delta_w_out': 'delta_w', 'delta_norm2_g': 'delta_w', 'delta_w_up': 'delta_w', 'delta_ffn_conv_w': 'delta_w', 'delta_ffn_conv_b': 'delta_w', 'delta_w_down': 'delta_w', 'delta_norm3_g': 'delta_w', 'delta_w_pe': 'delta_w', 'delta_w_pg': 'delta_w', 'delta_final_g': 'delta_w', 'new_m_norm1_g': 'new_m', 'new_m_w_in': 'new_m', 'new_m_a_ln_g': 'new_m', 'new_m_a_ln_b': 'new_m', 'new_m_a_ws': 'new_m', 'new_m_a_bs': 'new_m', 'new_m_b_conv_w': 'new_m', 'new_m_b_conv_b': 'new_m', 'new_m_b_wa': 'new_m', 'new_m_b_ba': 'new_m', 'new_m_b_wx': 'new_m', 'new_m_b_bx': 'new_m', 'new_m_b_lam': 'new_m', 'new_m_c_lb': 'new_m', 'new_m_c_norm_g': 'new_m', 'new_m_d_w': 'new_m', 'new_m_d_scale': 'new_m', 'new_m_w_out': 'new_m', 'new_m_norm2_g': 'new_m', 'new_m_w_up': 'new_m', 'new_m_ffn_conv_w': 'new_m', 'new_m_ffn_conv_b': 'new_m', 'new_m_w_down': 'new_m', 'new_m_norm3_g': 'new_m', 'new_m_w_pe': 'new_m', 'new_m_w_pg': 'new_m', 'new_m_final_g': 'new_m', 'new_v_norm1_g': 'new_v', 'new_v_w_in': 'new_v', 'new_v_a_ln_g': 'new_v', 'new_v_a_ln_b': 'new_v', 'new_v_a_ws': 'new_v', 'new_v_a_bs': 'new_v', 'new_v_b_conv_w': 'new_v', 'new_v_b_conv_b': 'new_v', 'new_v_b_wa': 'new_v', 'new_v_b_ba': 'new_v', 'new_v_b_wx': 'new_v', 'new_v_b_bx': 'new_v', 'new_v_b_lam': 'new_v', 'new_v_c_lb': 'new_v', 'new_v_c_norm_g': 'new_v', 'new_v_d_w': 'new_v', 'new_v_d_scale': 'new_v', 'new_v_w_out': 'new_v', 'new_v_norm2_g': 'new_v', 'new_v_w_up': 'new_v', 'new_v_ffn_conv_w': 'new_v', 'new_v_ffn_conv_b': 'new_v', 'new_v_w_down': 'new_v', 'new_v_norm3_g': 'new_v', 'new_v_w_pe': 'new_v', 'new_v_w_pg': 'new_v', 'new_v_final_g': 'new_v'}


def _forward(args):
    return _fwd_reference(*[args[k] for k in FWD_PARAMS])


def _output_shape():
    def fwd():
        inp = _fwd_setup_inputs(0)
        return _fwd_reference(*[inp[k] for k in FWD_PARAMS])
    out = _jax.eval_shape(fwd)
    return out.shape, out.dtype

N_MICROBATCH = 1
ADAM_LR = 0.001
ADAM_B1 = 0.9
ADAM_B2 = 0.999
ADAM_EPS = 1e-08
ADAM_WD = 0.01
ADAM_STEP = 10
PER_EXAMPLE_BATCH_AXIS = {'x': 0, 'p': 1, 'loss_target': 0}
SHARED_INPUTS = []
_WEIGHT_DTYPES = {'norm1_g': _jnp.float32, 'w_in': _jnp.float32, 'a_ln_g': _jnp.float32, 'a_ln_b': _jnp.float32, 'a_ws': _jnp.float32, 'a_bs': _jnp.float32, 'b_conv_w': _jnp.float32, 'b_conv_b': _jnp.float32, 'b_wa': _jnp.float32, 'b_ba': _jnp.float32, 'b_wx': _jnp.float32, 'b_bx': _jnp.float32, 'b_lam': _jnp.float32, 'c_lb': _jnp.float32, 'c_norm_g': _jnp.float32, 'd_w': _jnp.float32, 'd_scale': _jnp.float32, 'w_out': _jnp.float32, 'norm2_g': _jnp.float32, 'w_up': _jnp.float32, 'ffn_conv_w': _jnp.float32, 'ffn_conv_b': _jnp.float32, 'w_down': _jnp.float32, 'norm3_g': _jnp.float32, 'w_pe': _jnp.float32, 'w_pg': _jnp.float32, 'final_g': _jnp.float32}
MOMENT_SCALE = {'norm1_g': 1.056979e-01, 'w_in': 7.360082e-02, 'a_ln_g': 5.636915e-02, 'a_ln_b': 5.987925e-02, 'a_ws': 4.015947e-02, 'a_bs': 5.835450e-02, 'b_conv_w': 8.209069e-02, 'b_conv_b': 6.137968e-01, 'b_wa': 2.812811e-02, 'b_ba': 2.102941e-02, 'b_wx': 5.124710e-02, 'b_bx': 2.863443e-02, 'b_lam': 4.271353e-02, 'c_lb': 7.579220e-03, 'c_norm_g': 1.555674e-01, 'd_w': 1.091126e-01, 'd_scale': 1.158539e-01, 'w_out': 9.309179e-02, 'norm2_g': 9.894356e-02, 'w_up': 4.265025e-02, 'ffn_conv_w': 4.255828e-02, 'ffn_conv_b': 4.554780e-02, 'w_down': 6.962595e-02, 'norm3_g': 2.244089e-02, 'w_pe': 5.637338e-02, 'w_pg': 2.208236e-02, 'final_g': 3.201966e+01}


def _to_microbatches(a, axis):
    t = _jnp.moveaxis(a, axis, 0)
    t = t.reshape((N_MICROBATCH, t.shape[0] // N_MICROBATCH) + t.shape[1:])
    return _jnp.moveaxis(t, 1, axis + 1)


def setup_inputs(seed: int = 0) -> dict:
    inp = _fwd_setup_inputs(seed)
    key = _jax.random.fold_in(_jax.random.key(seed), 7919)
    shape, _ = _output_shape()
    out = dict(inp)
    out["loss_target"] = _jax.random.normal(_jax.random.fold_in(key, 0), shape, _jnp.float32)
    for i, name in enumerate(TWIN_WEIGHTS):
        w = inp[name].astype(_jnp.float32)
        if MOMENT_SCALE is None:
            s = _jnp.sqrt(_jnp.mean(_jnp.square(w)) + 1e-30)
        else:
            s = MOMENT_SCALE[name]
        km, kv = _jax.random.split(_jax.random.fold_in(key, i + 1))
        out[name] = w
        out["m_" + name] = s * _jax.random.normal(km, w.shape, _jnp.float32)
        out["v_" + name] = (s * s) * _jax.random.uniform(kv, w.shape, _jnp.float32, 0.5, 1.5)
    if N_MICROBATCH > 1:
        for name, axis in PER_EXAMPLE_BATCH_AXIS.items():
            out[name] = _to_microbatches(out[name], axis)
    return {'x': out['x'], 'p': out['p'], 'norm1_g': out['norm1_g'], 'w_in': out['w_in'], 'a_ln_g': out['a_ln_g'], 'a_ln_b': out['a_ln_b'], 'a_ws': out['a_ws'], 'a_bs': out['a_bs'], 'b_conv_w': out['b_conv_w'], 'b_conv_b': out['b_conv_b'], 'b_wa': out['b_wa'], 'b_ba': out['b_ba'], 'b_wx': out['b_wx'], 'b_bx': out['b_bx'], 'b_lam': out['b_lam'], 'c_lb': out['c_lb'], 'c_norm_g': out['c_norm_g'], 'd_w': out['d_w'], 'd_scale': out['d_scale'], 'w_out': out['w_out'], 'norm2_g': out['norm2_g'], 'w_up': out['w_up'], 'ffn_conv_w': out['ffn_conv_w'], 'ffn_conv_b': out['ffn_conv_b'], 'w_down': out['w_down'], 'norm3_g': out['norm3_g'], 'w_pe': out['w_pe'], 'w_pg': out['w_pg'], 'final_g': out['final_g'], 'loss_target': out['loss_target'], 'm_norm1_g': out['m_norm1_g'], 'm_w_in': out['m_w_in'], 'm_a_ln_g': out['m_a_ln_g'], 'm_a_ln_b': out['m_a_ln_b'], 'm_a_ws': out['m_a_ws'], 'm_a_bs': out['m_a_bs'], 'm_b_conv_w': out['m_b_conv_w'], 'm_b_conv_b': out['m_b_conv_b'], 'm_b_wa': out['m_b_wa'], 'm_b_ba': out['m_b_ba'], 'm_b_wx': out['m_b_wx'], 'm_b_bx': out['m_b_bx'], 'm_b_lam': out['m_b_lam'], 'm_c_lb': out['m_c_lb'], 'm_c_norm_g': out['m_c_norm_g'], 'm_d_w': out['m_d_w'], 'm_d_scale': out['m_d_scale'], 'm_w_out': out['m_w_out'], 'm_norm2_g': out['m_norm2_g'], 'm_w_up': out['m_w_up'], 'm_ffn_conv_w': out['m_ffn_conv_w'], 'm_ffn_conv_b': out['m_ffn_conv_b'], 'm_w_down': out['m_w_down'], 'm_norm3_g': out['m_norm3_g'], 'm_w_pe': out['m_w_pe'], 'm_w_pg': out['m_w_pg'], 'm_final_g': out['m_final_g'], 'v_norm1_g': out['v_norm1_g'], 'v_w_in': out['v_w_in'], 'v_a_ln_g': out['v_a_ln_g'], 'v_a_ln_b': out['v_a_ln_b'], 'v_a_ws': out['v_a_ws'], 'v_a_bs': out['v_a_bs'], 'v_b_conv_w': out['v_b_conv_w'], 'v_b_conv_b': out['v_b_conv_b'], 'v_b_wa': out['v_b_wa'], 'v_b_ba': out['v_b_ba'], 'v_b_wx': out['v_b_wx'], 'v_b_bx': out['v_b_bx'], 'v_b_lam': out['v_b_lam'], 'v_c_lb': out['v_c_lb'], 'v_c_norm_g': out['v_c_norm_g'], 'v_d_w': out['v_d_w'], 'v_d_scale': out['v_d_scale'], 'v_w_out': out['v_w_out'], 'v_norm2_g': out['v_norm2_g'], 'v_w_up': out['v_w_up'], 'v_ffn_conv_w': out['v_ffn_conv_w'], 'v_ffn_conv_b': out['v_ffn_conv_b'], 'v_w_down': out['v_w_down'], 'v_norm3_g': out['v_norm3_g'], 'v_w_pe': out['v_w_pe'], 'v_w_pg': out['v_w_pg'], 'v_final_g': out['v_final_g']}


def _loss(weights, diff, rest, loss_target):
    with _jax.named_scope("forward"):
        args = {**rest, TWIN_DIFF_INPUT: diff, **{k: w.astype(_WEIGHT_DTYPES[k]) for k, w in weights.items()}}
        y = _forward(args)
    with _jax.named_scope("loss_head"):
        err = _jnp.square(y.astype(_jnp.float32) - loss_target)
        return 0.5 * _jnp.sum(_jnp.mean(err, axis=-1)) if err.ndim else 0.5 * err


def _adamw(w, g, m, v):
    m = ADAM_B1 * m + (1.0 - ADAM_B1) * g
    v = ADAM_B2 * v + (1.0 - ADAM_B2) * _jnp.square(g)
    m_hat = m / (1.0 - ADAM_B1 ** ADAM_STEP)
    v_hat = v / (1.0 - ADAM_B2 ** ADAM_STEP)
    delta = -ADAM_LR * (m_hat / (_jnp.sqrt(v_hat) + ADAM_EPS) + ADAM_WD * w)
    return delta, m, v


def reference(x, p, norm1_g, w_in, a_ln_g, a_ln_b, a_ws, a_bs, b_conv_w, b_conv_b, b_wa, b_ba, b_wx, b_bx, b_lam, c_lb, c_norm_g, d_w, d_scale, w_out, norm2_g, w_up, ffn_conv_w, ffn_conv_b, w_down, norm3_g, w_pe, w_pg, final_g, loss_target, m_norm1_g, m_w_in, m_a_ln_g, m_a_ln_b, m_a_ws, m_a_bs, m_b_conv_w, m_b_conv_b, m_b_wa, m_b_ba, m_b_wx, m_b_bx, m_b_lam, m_c_lb, m_c_norm_g, m_d_w, m_d_scale, m_w_out, m_norm2_g, m_w_up, m_ffn_conv_w, m_ffn_conv_b, m_w_down, m_norm3_g, m_w_pe, m_w_pg, m_final_g, v_norm1_g, v_w_in, v_a_ln_g, v_a_ln_b, v_a_ws, v_a_bs, v_b_conv_w, v_b_conv_b, v_b_wa, v_b_ba, v_b_wx, v_b_bx, v_b_lam, v_c_lb, v_c_norm_g, v_d_w, v_d_scale, v_w_out, v_norm2_g, v_w_up, v_ffn_conv_w, v_ffn_conv_b, v_w_down, v_norm3_g, v_w_pe, v_w_pg, v_final_g):
    given = dict(x=x, p=p, norm1_g=norm1_g, w_in=w_in, a_ln_g=a_ln_g, a_ln_b=a_ln_b, a_ws=a_ws, a_bs=a_bs, b_conv_w=b_conv_w, b_conv_b=b_conv_b, b_wa=b_wa, b_ba=b_ba, b_wx=b_wx, b_bx=b_bx, b_lam=b_lam, c_lb=c_lb, c_norm_g=c_norm_g, d_w=d_w, d_scale=d_scale, w_out=w_out, norm2_g=norm2_g, w_up=w_up, ffn_conv_w=ffn_conv_w, ffn_conv_b=ffn_conv_b, w_down=w_down, norm3_g=norm3_g, w_pe=w_pe, w_pg=w_pg, final_g=final_g, loss_target=loss_target, m_norm1_g=m_norm1_g, m_w_in=m_w_in, m_a_ln_g=m_a_ln_g, m_a_ln_b=m_a_ln_b, m_a_ws=m_a_ws, m_a_bs=m_a_bs, m_b_conv_w=m_b_conv_w, m_b_conv_b=m_b_conv_b, m_b_wa=m_b_wa, m_b_ba=m_b_ba, m_b_wx=m_b_wx, m_b_bx=m_b_bx, m_b_lam=m_b_lam, m_c_lb=m_c_lb, m_c_norm_g=m_c_norm_g, m_d_w=m_d_w, m_d_scale=m_d_scale, m_w_out=m_w_out, m_norm2_g=m_norm2_g, m_w_up=m_w_up, m_ffn_conv_w=m_ffn_conv_w, m_ffn_conv_b=m_ffn_conv_b, m_w_down=m_w_down, m_norm3_g=m_norm3_g, m_w_pe=m_w_pe, m_w_pg=m_w_pg, m_final_g=m_final_g, v_norm1_g=v_norm1_g, v_w_in=v_w_in, v_a_ln_g=v_a_ln_g, v_a_ln_b=v_a_ln_b, v_a_ws=v_a_ws, v_a_bs=v_a_bs, v_b_conv_w=v_b_conv_w, v_b_conv_b=v_b_conv_b, v_b_wa=v_b_wa, v_b_ba=v_b_ba, v_b_wx=v_b_wx, v_b_bx=v_b_bx, v_b_lam=v_b_lam, v_c_lb=v_c_lb, v_c_norm_g=v_c_norm_g, v_d_w=v_d_w, v_d_scale=v_d_scale, v_w_out=v_w_out, v_norm2_g=v_norm2_g, v_w_up=v_w_up, v_ffn_conv_w=v_ffn_conv_w, v_ffn_conv_b=v_ffn_conv_b, v_w_down=v_w_down, v_norm3_g=v_norm3_g, v_w_pe=v_w_pe, v_w_pg=v_w_pg, v_final_g=v_final_g)
    weights = {n: given[n] for n in TWIN_WEIGHTS}
    shared = {n: given[n] for n in SHARED_INPUTS}
    per_example = {n: given[n] for n in ['x', 'p']}
    grad_fn = _jax.value_and_grad(_loss, argnums=(0, 1))

    def one_microbatch(ex, loss_target):
        ex = dict(ex)
        diff = ex.pop(TWIN_DIFF_INPUT)
        return grad_fn(weights, diff, {**shared, **ex}, loss_target)

    if N_MICROBATCH == 1:
        loss, (grad_w, grad_x) = one_microbatch(per_example, given["loss_target"])
    else:
        def body(carry, xs):
            loss_sum, grad_sum = carry
            l_k, (gw_k, gx_k) = one_microbatch(xs[0], xs[1])
            with _jax.named_scope("update"):
                return (loss_sum + l_k, _jax.tree.map(_jnp.add, grad_sum, gw_k)), gx_k

        init = (_jnp.zeros((), _jnp.float32), _jax.tree.map(_jnp.zeros_like, weights))
        (loss, grad_w), grad_x = _jax.lax.scan(body, init, (per_example, given["loss_target"]))
    with _jax.named_scope("update"):
        delta_w, new_m, new_v = {}, {}, {}
        for n in TWIN_WEIGHTS:
            delta_w[n], new_m[n], new_v[n] = _adamw(weights[n], grad_w[n], given["m_" + n], given["v_" + n])
    return (loss, grad_x, *[grad_w[n] for n in TWIN_WEIGHTS], *[delta_w[n] for n in TWIN_WEIGHTS],
            *[new_m[n] for n in TWIN_WEIGHTS], *[new_v[n] for n in TWIN_WEIGHTS])
```

```python
import functools

import jax
import jax.numpy as jnp
from jax import lax
from jax.experimental import pallas as pl
from jax.experimental.pallas import tpu as pltpu

F32 = jnp.float32
BF16 = jnp.bfloat16
EPS = 1e-6
HEAD_DIM = 64
N_HEADS = 4
W_GRP = HEAD_DIM * N_HEADS
GMLP_CHUNK = 128
HGRN_CHUNK = 64
RGLRU_C = 8.0
POOL_HALO = 16
EXP_CLAMP = 80.0
ADAM_LR, ADAM_B1, ADAM_B2, ADAM_EPS, ADAM_WD, ADAM_STEP = 0.001, 0.9, 0.999, 1e-08, 0.01, 10
VMEM_LIMIT_BYTES = 56 * 1024 * 1024
TILE_PREFS = (1024, 1408, 768, 512, 256, 128)
ROW_TILE_PREFS = (512, 256, 128, 64, 32, 16, 8)
MESH_ID = pl.DeviceIdType.MESH
N_DEV = 8


def _pick(n, prefs=TILE_PREFS):
    for p in prefs:
        if n % p == 0:
            return p
    return n


def _cp(*sem):
    return pltpu.CompilerParams(dimension_semantics=sem if sem else None, vmem_limit_bytes=VMEM_LIMIT_BYTES)


def _sds(shape, dtype):
    return jax.ShapeDtypeStruct(tuple(shape), dtype)


_GELU_C = 0.7978845608028654
_GELU_A = 0.044715


def _gelu(x):
    return 0.5 * x * (1.0 + jnp.tanh(_GELU_C * (x + _GELU_A * x * x * x)))


def _gelu_and_grad(x):
    t = jnp.tanh(_GELU_C * (x + _GELU_A * x * x * x))
    g = 0.5 * x * (1.0 + t)
    dg = 0.5 * (1.0 + t) + 0.5 * x * (1.0 - t * t) * _GELU_C * (1.0 + 3.0 * _GELU_A * x * x)
    return g, dg


def _sigmoid(x):
    return 1.0 / (1.0 + jnp.exp(-x))


def _dot(a, b):
    return jnp.dot(a, b, preferred_element_type=F32)


def _dot_nt(a, b):
    return lax.dot_general(a, b, (((1,), (1,)), ((), ())), preferred_element_type=F32)


def _dot_tn(a, b):
    return lax.dot_general(a, b, (((0,), (0,)), ((), ())), preferred_element_type=F32)


def _split3(x):
    hi = x.astype(BF16)
    r1 = x - hi.astype(F32)
    mid = r1.astype(BF16)
    lo = (r1 - mid.astype(F32)).astype(BF16)
    return hi, mid, lo


def _dot_f32_rhs_exact(x, m_bf16):
    hi, mid, lo = _split3(x)
    return _dot(hi, m_bf16) + _dot(mid, m_bf16) + _dot(lo, m_bf16)


def _dot_f32_lhs_exact(m_bf16, x):
    hi, mid, lo = _split3(x)
    return _dot(m_bf16, hi) + _dot(m_bf16, mid) + _dot(m_bf16, lo)


def _head_masks(width=W_GRP):
    lane = lax.broadcasted_iota(jnp.int32, (1, width), 1)
    return [(lane >= h * HEAD_DIM) & (lane < (h + 1) * HEAD_DIM) for h in range(N_HEADS)]


def _block_mask(n=W_GRP):
    r = lax.broadcasted_iota(jnp.int32, (n, n), 0)
    c = lax.broadcasted_iota(jnp.int32, (n, n), 1)
    m = None
    for h in range(N_HEADS):
        mh = (r >= h * HEAD_DIM) & (r < (h + 1) * HEAD_DIM) & (c >= h * HEAD_DIM) & (c < (h + 1) * HEAD_DIM)
        m = mh if m is None else (m | mh)
    return m


def _mm(a, b, mode, *, out_dtype, name, res=None, n=None, b_layer=None, b_noff=0, b_koff=0,
        out_buf=None, out_n=None, o_noff=0, layer=None, nlayers=None):
    if mode == "nn":
        m_dim, k_dim = a.shape
        n_dim = n if n is not None else b.shape[-1]
    elif mode == "nt":
        m_dim, k_dim = a.shape
        n_dim = b.shape[-2]
    else:
        k_dim, m_dim = a.shape
        n_dim = n if n is not None else b.shape[-1]
    tm, tn, tk = _pick(m_dim), _pick(n_dim), _pick(k_dim)
    nk = k_dim // tk
    assert b_noff % tn == 0 and b_koff % tk == 0 and o_noff % tn == 0
    bn0, bk0, on0 = b_noff // tn, b_koff // tk, o_noff // tn
    dims = {"nn": (((1,), (0,)), ((), ())), "nt": (((1,), (1,)), ((), ())), "tn": (((0,), (0,)), ((), ()))}[mode]

    if mode == "tn":
        a_spec = pl.BlockSpec((tk, tm), lambda i, j, k: (k, i))
    else:
        a_spec = pl.BlockSpec((tm, tk), lambda i, j, k: (i, k))
    lead = () if b_layer is None else (None,)
    lidx = () if b_layer is None else (b_layer,)
    if mode == "nt":
        b_spec = pl.BlockSpec(lead + (tn, tk), lambda i, j, k: lidx + (j + bn0, k + bk0))
    else:
        b_spec = pl.BlockSpec(lead + (tk, tn), lambda i, j, k: lidx + (k + bk0, j + bn0))
    in_specs = [a_spec, b_spec]
    args = [a, b]
    if res is not None:
        in_specs.append(pl.BlockSpec((tm, tn), lambda i, j, k: (i, j)))
        args.append(res)
    n_total = out_n if out_n is not None else n_dim
    if layer is None:
        out_shape = _sds((m_dim, n_total), out_dtype)
        out_spec = pl.BlockSpec((tm, tn), lambda i, j, k: (i, j + on0))
    else:
        out_shape = _sds((nlayers, m_dim, n_total), out_dtype)
        out_spec = pl.BlockSpec((None, tm, tn), lambda i, j, k: (layer, i, j + on0))
    aliases = {}
    if out_buf is not None:
        in_specs.append(pl.BlockSpec(memory_space=pl.ANY))
        args.append(out_buf)
        aliases = {len(args) - 1: 0}
    has_res = res is not None
    has_buf = out_buf is not None

    def body(*refs):
        a_ref, b_ref = refs[0], refs[1]
        res_ref = refs[2] if has_res else None
        o_ref = refs[2 + int(has_res) + int(has_buf)]
        acc_ref = refs[-1] if nk > 1 else None
        part = lax.dot_general(a_ref[...].astype(BF16), b_ref[...].astype(BF16), dims, preferred_element_type=F32)

        def finish(v):
            if has_res:
                v = v + res_ref[...]
            o_ref[...] = v.astype(o_ref.dtype)

        if nk == 1:
            finish(part)
        else:
            kk = pl.program_id(2)

            @pl.when(kk == 0)
            def _():
                acc_ref[...] = part

            @pl.when(kk > 0)
            def _():
                acc_ref[...] += part

            @pl.when(kk == nk - 1)
            def _():
                finish(acc_ref[...])

    return pl.pallas_call(
        body, grid=(m_dim // tm, n_dim // tn, nk), in_specs=in_specs, out_specs=out_spec, out_shape=out_shape,
        scratch_shapes=[pltpu.VMEM((tm, tn), F32)] if nk > 1 else [],
        input_output_aliases=aliases, name=name, compiler_params=_cp("parallel", "parallel", "arbitrary"),
    )(*args)


def _rms_fwd(x, g, name):
    t, d = x.shape
    tm = _pick(t, ROW_TILE_PREFS)

    def body(x_ref, g_ref, o_ref):
        xv = x_ref[...]
        r = lax.rsqrt(jnp.mean(xv * xv, axis=-1, keepdims=True) + EPS)
        o_ref[...] = (xv * r * g_ref[...]).astype(o_ref.dtype)

    return pl.pallas_call(
        body, grid=(t // tm,),
        in_specs=[pl.BlockSpec((tm, d), lambda i: (i, 0)), pl.BlockSpec((1, d), lambda i: (0, 0))],
        out_specs=pl.BlockSpec((tm, d), lambda i: (i, 0)), out_shape=_sds((t, d), BF16),
        name=name, compiler_params=_cp("parallel"),
    )(x, g.reshape(1, d))


def _rms_bwd(dh, x, g, dres, name):
    t, d = x.shape
    tm = _pick(t, ROW_TILE_PREFS)

    def body(dh_ref, x_ref, g_ref, dres_ref, dx_ref, dg_ref):
        i = pl.program_id(0)
        xv = x_ref[...]
        dy = dh_ref[...].astype(F32)
        r = lax.rsqrt(jnp.mean(xv * xv, axis=-1, keepdims=True) + EPS)
        dyg = dy * g_ref[...]
        dot = jnp.mean(dyg * xv, axis=-1, keepdims=True)
        dx_ref[...] = dres_ref[...] + r * dyg - xv * (r * r * r) * dot
        part = jnp.sum(dy * xv * r, axis=0, keepdims=True)

        @pl.when(i == 0)
        def _():
            dg_ref[...] = part

        @pl.when(i > 0)
        def _():
            dg_ref[...] += part

    row = pl.BlockSpec((tm, d), lambda i: (i, 0))
    vec = pl.BlockSpec((1, d), lambda i: (0, 0))
    return pl.pallas_call(
        body, grid=(t // tm,), in_specs=[row, row, vec, row], out_specs=[row, vec],
        out_shape=[_sds((t, d), F32), _sds((1, d), F32)], name=name, compiler_params=_cp("arbitrary"),
    )(dh, x, g.reshape(1, d), dres)


def _final_loss(x, g, target):
    t, d = x.shape
    tm = _pick(t, ROW_TILE_PREFS)

    def body(x_ref, g_ref, t_ref, dx_ref, dg_ref, loss_ref):
        i = pl.program_id(0)
        xv = x_ref[...]
        gv = g_ref[...]
        r = lax.rsqrt(jnp.mean(xv * xv, axis=-1, keepdims=True) + EPS)
        err = xv * r * gv - t_ref[...]
        lpart = (0.5 / d) * jnp.sum(jnp.sum(err * err, axis=1, keepdims=True), axis=0, keepdims=True)
        dy = err * (1.0 / d)
        dyg = dy * gv
        dot = jnp.mean(dyg * xv, axis=-1, keepdims=True)
        dx_ref[...] = r * dyg - xv * (r * r * r) * dot
        part = jnp.sum(dy * xv * r, axis=0, keepdims=True)

        @pl.when(i == 0)
        def _():
            dg_ref[...] = part
            loss_ref[...] = lpart

        @pl.when(i > 0)
        def _():
            dg_ref[...] += part
            loss_ref[...] += lpart

    row = pl.BlockSpec((tm, d), lambda i: (i, 0))
    vec = pl.BlockSpec((1, d), lambda i: (0, 0))
    return pl.pallas_call(
        body, grid=(t // tm,), in_specs=[row, vec, row], out_specs=[row, vec, pl.BlockSpec((1, 1), lambda i: (0, 0))],
        out_shape=[_sds((t, d), F32), _sds((1, d), F32), _sds((1, 1), F32)], name="final_loss",
        compiler_params=_cp("arbitrary"),
    )(x, g.reshape(1, d), target)


def _shift_down(ext, k, halo):
    return pltpu.roll(ext, k, 0)[halo:]


def _shift_up(ext, k, tm):
    return pltpu.roll(ext, ext.shape[0] - k, 0)[:tm]


def _ffn_tiles(t, f):
    return _pick(t, ROW_TILE_PREFS), _pick(f, (256, 128))


def _ffn_act_fwd(hf_g, hf_v, conv_w, conv_b):
    t, f = hf_g.shape
    tm, cn = _ffn_tiles(t, f)
    nf = f // cn

    def body(g_ref, v_ref, wg_ref, wv_ref, bg_ref, bv_ref, o_ref, ext_ref, hg_ref, hv_ref):
        i = pl.program_id(1)

        @pl.when(i == 0)
        def _():
            hg_ref[...] = jnp.zeros_like(hg_ref)
            hv_ref[...] = jnp.zeros_like(hv_ref)

        def conv(x_ref, halo_ref, w_ref, b_ref):
            ext_ref[0:8, :] = halo_ref[...]
            ext_ref[8:, :] = x_ref[...]
            halo_ref[...] = x_ref[tm - 8:tm, :]
            ext = ext_ref[...]
            w = w_ref[...]
            return b_ref[...] + w[2:3, :] * ext[8:] + w[1:2, :] * _shift_down(ext, 1, 8) + w[0:1, :] * _shift_down(ext, 2, 8)

        gc = conv(g_ref, hg_ref, wg_ref, bg_ref)
        vc = conv(v_ref, hv_ref, wv_ref, bv_ref)
        o_ref[...] = (_gelu(gc) * vc).astype(o_ref.dtype)

    blk = pl.BlockSpec((tm, cn), lambda j, i: (i, j))
    return pl.pallas_call(
        body, grid=(nf, t // tm),
        in_specs=[blk, blk, pl.BlockSpec((3, cn), lambda j, i: (0, j)), pl.BlockSpec((3, cn), lambda j, i: (0, j + nf)),
                  pl.BlockSpec((1, cn), lambda j, i: (0, j)), pl.BlockSpec((1, cn), lambda j, i: (0, j + nf))],
        out_specs=blk, out_shape=_sds((t, f), BF16),
        scratch_shapes=[pltpu.VMEM((tm + 8, cn), F32), pltpu.VMEM((8, cn), F32), pltpu.VMEM((8, cn), F32)],
        name="ffn_act_fwd", compiler_params=_cp("parallel", "arbitrary"),
    )(hf_g, hf_v, conv_w, conv_w, conv_b, conv_b)


def _ffn_act_bwd(dact, hf_g, hf_v, conv_w, conv_b):
    t, f = hf_g.shape
    tm, cn = _ffn_tiles(t, f)
    nf, nt = f // cn, t // tm
    hb = tm // 8

    def body(da_ref, g_ref, v_ref, gh_ref, vh_ref, wg_ref, wv_ref, bg_ref, bv_ref,
             dg_ref, dv_ref, dwg_ref, dwv_ref, dbg_ref, dbv_ref, ext_ref, cg_ref, cv_ref):
        i = pl.program_id(1)
        first_tile = i == nt - 1

        @pl.when(i == 0)
        def _():
            cg_ref[...] = jnp.zeros_like(cg_ref)
            cv_ref[...] = jnp.zeros_like(cv_ref)
            dwg_ref[...] = jnp.zeros_like(dwg_ref)
            dwv_ref[...] = jnp.zeros_like(dwv_ref)
            dbg_ref[...] = jnp.zeros_like(dbg_ref)
            dbv_ref[...] = jnp.zeros_like(dbv_ref)

        def shifted(x_ref, halo_ref):
            ext_ref[0:8, :] = jnp.where(first_tile, 0.0, halo_ref[...])
            ext_ref[8:, :] = x_ref[...]
            ext = ext_ref[0:tm + 8, :]
            return ext[8:], _shift_down(ext, 1, 8), _shift_down(ext, 2, 8)

        wg, wv = wg_ref[...], wv_ref[...]
        g0, g1, g2 = shifted(g_ref, gh_ref)
        gc = bg_ref[...] + wg[2:3, :] * g0 + wg[1:2, :] * g1 + wg[0:1, :] * g2
        v0, v1, v2 = shifted(v_ref, vh_ref)
        vc = bv_ref[...] + wv[2:3, :] * v0 + wv[1:2, :] * v1 + wv[0:1, :] * v2
        da = da_ref[...].astype(F32)
        gel, dgel = _gelu_and_grad(gc)
        dgc = da * vc * dgel
        dvc = da * gel

        def back(dc, carry_ref, w, x0, x1, x2, dx_ref, dw_ref, db_ref):
            ext_ref[0:tm, :] = dc
            ext_ref[tm:tm + 8, :] = carry_ref[...]
            carry_ref[...] = dc[0:8, :]
            ext = ext_ref[0:tm + 8, :]
            dx = w[2:3, :] * dc + w[1:2, :] * _shift_up(ext, 1, tm) + w[0:1, :] * _shift_up(ext, 2, tm)
            dx_ref[...] = dx.astype(dx_ref.dtype)
            dw_ref[0:1, :] += jnp.sum(dc * x2, axis=0, keepdims=True)
            dw_ref[1:2, :] += jnp.sum(dc * x1, axis=0, keepdims=True)
            dw_ref[2:3, :] += jnp.sum(dc * x0, axis=0, keepdims=True)
            db_ref[...] += jnp.sum(dc, axis=0, keepdims=True)

        back(dgc, cg_ref, wg, g0, g1, g2, dg_ref, dwg_ref, dbg_ref)
        back(dvc, cv_ref, wv, v0, v1, v2, dv_ref, dwv_ref, dbv_ref)

    blk = pl.BlockSpec((tm, cn), lambda j, i: (nt - 1 - i, j))
    halo = pl.BlockSpec((8, cn), lambda j, i: (jnp.maximum((nt - 1 - i) * hb - 1, 0), j))
    w3g = pl.BlockSpec((3, cn), lambda j, i: (0, j))
    w3v = pl.BlockSpec((3, cn), lambda j, i: (0, j + nf))
    b1g = pl.BlockSpec((1, cn), lambda j, i: (0, j))
    b1v = pl.BlockSpec((1, cn), lambda j, i: (0, j + nf))
    acc3 = pl.BlockSpec((3, cn), lambda j, i: (0, j))
    acc1 = pl.BlockSpec((1, cn), lambda j, i: (0, j))
    return pl.pallas_call(
        body, grid=(nf, nt),
        in_specs=[blk, blk, blk, halo, halo, w3g, w3v, b1g, b1v],
        out_specs=[blk, blk, acc3, acc3, acc1, acc1],
        out_shape=[_sds((t, f), BF16), _sds((t, f), BF16), _sds((3, f), F32), _sds((3, f), F32),
                   _sds((1, f), F32), _sds((1, f), F32)],
        scratch_shapes=[pltpu.VMEM((tm + 8, cn), F32), pltpu.VMEM((8, cn), F32), pltpu.VMEM((8, cn), F32)],
        name="ffn_act_bwd", compiler_params=_cp("parallel", "arbitrary"),
    )(dact, hf_g, hf_v, hf_g, hf_v, conv_w, conv_w, conv_b, conv_b)


def _ple_fwd(x2, pe, pre):
    t, d = x2.shape
    tm = _pick(t, ROW_TILE_PREFS)

    def body(x_ref, pe_ref, pre_ref, o_ref):
        o_ref[...] = x_ref[...] + pe_ref[...] * _sigmoid(pre_ref[...])

    row = pl.BlockSpec((tm, d), lambda i: (i, 0))
    return pl.pallas_call(body, grid=(t // tm,), in_specs=[row, row, row], out_specs=row,
                          out_shape=_sds((t, d), F32), name="ple_fwd", compiler_params=_cp("parallel"))(x2, pe, pre)


def _ple_bwd(dx3, pe, pre):
    t, d = dx3.shape
    tm = _pick(t, ROW_TILE_PREFS)

    def body(dx_ref, pe_ref, pre_ref, dpe_ref, dpre_ref):
        gate = _sigmoid(pre_ref[...])
        dx = dx_ref[...]
        dpe_ref[...] = (dx * gate).astype(dpe_ref.dtype)
        dpre_ref[...] = (dx * pe_ref[...] * gate * (1.0 - gate)).astype(dpre_ref.dtype)

    row = pl.BlockSpec((tm, d), lambda i: (i, 0))
    return pl.pallas_call(body, grid=(t // tm,), in_specs=[row, row, row], out_specs=[row, row],
                          out_shape=[_sds((t, d), BF16), _sds((t, d), BF16)], name="ple_bwd",
                          compiler_params=_cp("parallel"))(dx3, pe, pre)


def _mix_tm(t):
    return _pick(t, (512, 256, 128))


def _zblk(tm, col, rev_nt=None):
    if rev_nt is None:
        return pl.BlockSpec((tm, W_GRP), lambda i: (i, col))
    return pl.BlockSpec((tm, W_GRP), lambda i: (rev_nt - 1 - i, col))


def _full(shape):
    nd = len(shape)
    return pl.BlockSpec(tuple(shape), lambda i: (0,) * nd)


def _gmlp_sv(wm_ref, vnc, bs, hm):
    sv = bs
    for h in range(N_HEADS):
        sv = sv + jnp.where(hm[h], _dot(wm_ref[h], vnc), 0.0)
    return sv


def _layernorm(v, g, b):
    mu = jnp.mean(v, axis=-1, keepdims=True)
    vc = v - mu
    rs = lax.rsqrt(jnp.mean(vc * vc, axis=-1, keepdims=True) + EPS)
    xhat = vc * rs
    return xhat, rs, xhat * g + b


def _mix_a_fwd(z, d_mix, ln_g, ln_b, wm, bs_t):
    t = z.shape[0]
    tm = _mix_tm(t)

    def body(u_ref, v_ref, g_ref, b_ref, wm_ref, bs_ref, o_ref):
        hm = _head_masks()
        ug = _gelu(u_ref[...])
        _, _, vn = _layernorm(_gelu(v_ref[...]), g_ref[...], b_ref[...])
        vnb = vn.astype(BF16)
        for n in range(tm // GMLP_CHUNK):
            sl = slice(n * GMLP_CHUNK, (n + 1) * GMLP_CHUNK)
            o_ref[sl, :] = ug[sl] * _gmlp_sv(wm_ref, vnb[sl], bs_ref[...], hm)

    return pl.pallas_call(
        body, grid=(t // tm,),
        in_specs=[_zblk(tm, 0), _zblk(tm, 1), _full((1, W_GRP)), _full((1, W_GRP)), _full(wm.shape), _full(bs_t.shape)],
        out_specs=_zblk(tm, 0), out_shape=_sds((t, d_mix), F32), name="mix_a_fwd", compiler_params=_cp("parallel"),
    )(z, z, ln_g, ln_b, wm, bs_t)


def _mix_a_bwd(z, dmix, ln_g, ln_b, wm, wm_t, bs_t):
    t, zc = z.shape
    tm = _mix_tm(t)

    def body(u_ref, v_ref, dy_ref, g_ref, b_ref, wm_ref, wmt_ref, bs_ref, dz_ref, dg_ref, db_ref, dws_ref, dbs_ref):
        i = pl.program_id(0)

        @pl.when(i == 0)
        def _():
            dg_ref[...] = jnp.zeros_like(dg_ref)
            db_ref[...] = jnp.zeros_like(db_ref)
            dws_ref[...] = jnp.zeros_like(dws_ref)
            dbs_ref[...] = jnp.zeros_like(dbs_ref)

        hm = _head_masks()
        ug, dug = _gelu_and_grad(u_ref[...])
        vg, dvg = _gelu_and_grad(v_ref[...])
        gv = g_ref[...]
        xhat, rs, vn = _layernorm(vg, gv, b_ref[...])
        vnb = vn.astype(BF16)
        dy = dy_ref[...]
        for n in range(tm // GMLP_CHUNK):
            sl = slice(n * GMLP_CHUNK, (n + 1) * GMLP_CHUNK)
            vnc = vnb[sl]
            sv = _gmlp_sv(wm_ref, vnc, bs_ref[...], hm)
            dsv = dy[sl] * ug[sl]
            dz_ref[sl, 0:W_GRP] = dy[sl] * sv * dug[sl]
            dbs_ref[...] += dsv
            dsvb = dsv.astype(BF16)
            dvn = jnp.zeros((GMLP_CHUNK, W_GRP), F32)
            for h in range(N_HEADS):
                dws_ref[h] += _dot_nt(jnp.where(hm[h], dsv, 0.0).astype(BF16), vnc)
                dvn = dvn + jnp.where(hm[h], _dot(wmt_ref[h], dsvb), 0.0)
            xh = xhat[sl]
            dg_ref[...] += jnp.sum(dvn * xh, axis=0, keepdims=True)
            db_ref[...] += jnp.sum(dvn, axis=0, keepdims=True)
            dxh = dvn * gv
            dvg_c = rs[sl] * (dxh - jnp.mean(dxh, axis=-1, keepdims=True) - xh * jnp.mean(dxh * xh, axis=-1, keepdims=True))
            dz_ref[sl, W_GRP:2 * W_GRP] = dvg_c * dvg[sl]

    return pl.pallas_call(
        body, grid=(t // tm,),
        in_specs=[_zblk(tm, 0), _zblk(tm, 1), _zblk(tm, 0), _full((1, W_GRP)), _full((1, W_GRP)), _full(wm.shape),
                  _full(wm_t.shape), _full(bs_t.shape)],
        out_specs=[pl.BlockSpec((tm, 2 * W_GRP), lambda i: (i, 0)), _full((1, W_GRP)), _full((1, W_GRP)),
                   _full(wm.shape), _full(bs_t.shape)],
        out_shape=[_sds((t, zc), F32), _sds((1, W_GRP), F32), _sds((1, W_GRP), F32), _sds(wm.shape, F32),
                   _sds(bs_t.shape, F32)],
        name="mix_a_bwd", compiler_params=_cp("arbitrary"),
    )(z, z, dmix, ln_g, ln_b, wm, wm_t, bs_t)


def _softplus(x):
    return jnp.maximum(x, 0.0) + jnp.log(1.0 + jnp.exp(-jnp.abs(x)))


def _neg_expm1(x):
    series = -x * (1.0 + x * 0.5 * (1.0 + x * (1.0 / 3.0) * (1.0 + x * 0.25 * (1.0 + x * 0.2))))
    return jnp.where(x > -0.1, series, 1.0 - jnp.exp(x))


def _rglru_gates(ext_ref, x_ref, halo, cw, cb, wa_ref, wx_ref, ba, bx, lam):
    ext_ref[0:8, :] = halo
    ext_ref[8:, :] = x_ref[...]
    ext = ext_ref[...]
    x0, x1, x2, x3 = ext[8:], _shift_down(ext, 1, 8), _shift_down(ext, 2, 8), _shift_down(ext, 3, 8)
    xc = cb + cw[3:4, :] * x0 + cw[2:3, :] * x1 + cw[1:2, :] * x2 + cw[0:1, :] * x3
    xcb = xc.astype(BF16)
    r = _sigmoid(_dot(xcb, wa_ref[...]) + ba)
    ig = _sigmoid(_dot(xcb, wx_ref[...]) + bx)
    sp = _softplus(-lam)
    la = -RGLRU_C * r * sp
    a = jnp.exp(la)
    mult = jnp.sqrt(_neg_expm1(2.0 * la))
    return (x0, x1, x2, x3), xc, r, ig, sp, a, mult


def _mix_b_fwd(z, mix, conv_w, conv_b, wa, wx, ba, bx, lam):
    t = z.shape[0]
    tm = _mix_tm(t)

    def body(x_ref, gb_ref, cw_ref, cb_ref, wa_ref, wx_ref, ba_ref, bx_ref, lam_ref, mix_in, o_ref, hs_ref,
             ext_ref, a_ref, b_ref, xh_ref, hc_ref):
        i = pl.program_id(0)

        @pl.when(i == 0)
        def _():
            xh_ref[...] = jnp.zeros_like(xh_ref)
            hc_ref[...] = jnp.zeros_like(hc_ref)

        _, xc, _, ig, _, a, mult = _rglru_gates(ext_ref, x_ref, xh_ref[...], cw_ref[...], cb_ref[...], wa_ref, wx_ref,
                                                ba_ref[...], bx_ref[...], lam_ref[...])
        xh_ref[...] = x_ref[tm - 8:tm, :]
        a_ref[...] = a
        b_ref[...] = mult * (ig * xc)
        rid = lax.broadcasted_iota(jnp.int32, (8, W_GRP), 0)

        def group(gi, hprev):
            base = pl.multiple_of(gi * 8, 8)
            ca = a_ref[pl.ds(base, 8), :]
            cb = b_ref[pl.ds(base, 8), :]
            for k in (1, 2, 4):
                m = rid >= k
                cb = jnp.where(m, ca * pltpu.roll(cb, k, 0) + cb, cb)
                ca = jnp.where(m, ca * pltpu.roll(ca, k, 0), ca)
            hh = cb + ca * hprev
            hs_ref[pl.ds(base, 8), :] = hh
            return hh[7:8, :]

        hlast = lax.fori_loop(0, tm // 8, group, hc_ref[0:1, :])
        hc_ref[...] = jnp.broadcast_to(hlast, hc_ref.shape)
        o_ref[...] = hs_ref[...] * _gelu(gb_ref[...])

    sq = _full((W_GRP, W_GRP))
    vec = _full((1, W_GRP))
    return pl.pallas_call(
        body, grid=(t // tm,),
        in_specs=[_zblk(tm, 2), _zblk(tm, 3), _full((4, W_GRP)), vec, sq, sq, vec, vec, vec, pl.BlockSpec(memory_space=pl.ANY)],
        out_specs=[_zblk(tm, 1), pl.BlockSpec((tm, W_GRP), lambda i: (i, 0))],
        out_shape=[_sds(mix.shape, F32), _sds((t, W_GRP), F32)],
        scratch_shapes=[pltpu.VMEM((tm + 8, W_GRP), F32), pltpu.VMEM((tm, W_GRP), F32), pltpu.VMEM((tm, W_GRP), F32),
                        pltpu.VMEM((8, W_GRP), F32), pltpu.VMEM((8, W_GRP), F32)],
        input_output_aliases={9: 0}, name="mix_b_fwd", compiler_params=_cp("arbitrary"),
    )(z, z, conv_w, conv_b, wa, wx, ba, bx, lam, mix)


def _mix_b_bwd(z, dz, dmix, hs, conv_w, conv_b, wa, wx, wa_t, wx_t, ba, bx, lam):
    t = z.shape[0]
    tm = _mix_tm(t)
    nt = t // tm
    hb = tm // 8

    def body(x_ref, gb_ref, xhalo_ref, hs_ref, hhalo_ref, dy_ref, cw_ref, cb_ref, wa_ref, wx_ref, wat_ref, wxt_ref,
             ba_ref, bx_ref, lam_ref, dz_in, dz_ref, dcw_ref, dcb_ref, dwa_ref, dwx_ref, dba_ref, dbx_ref, dlam_ref,
             ext_ref, c_ref, d_ref, g_ref, an_ref, gn_ref, dxn_ref):
        i = pl.program_id(0)
        first_tile = i == nt - 1

        @pl.when(i == 0)
        def _():
            for ref in (dcw_ref, dcb_ref, dwa_ref, dwx_ref, dba_ref, dbx_ref, dlam_ref, an_ref, gn_ref, dxn_ref):
                ref[...] = jnp.zeros_like(ref)

        cw, lam = cw_ref[...], lam_ref[...]
        xhalo = jnp.where(first_tile, 0.0, xhalo_ref[...])
        (x0, x1, x2, x3), xc, r, ig, sp, a, mult = _rglru_gates(
            ext_ref, x_ref, xhalo, cw, cb_ref[...], wa_ref, wx_ref, ba_ref[...], bx_ref[...], lam)
        hs = hs_ref[...]
        dy = dy_ref[...]
        gel, dgel = _gelu_and_grad(gb_ref[...])
        dz_ref[:, W_GRP:2 * W_GRP] = dy * hs * dgel

        ext_ref[0:tm, :] = a
        ext_ref[tm:tm + 8, :] = an_ref[...]
        an_ref[...] = a[0:8, :]
        c_ref[...] = _shift_up(ext_ref[...], 1, tm)
        d_ref[...] = dy * gel
        rid = lax.broadcasted_iota(jnp.int32, (8, W_GRP), 0)

        def group(j, gnext):
            base = pl.multiple_of((tm // 8 - 1 - j) * 8, 8)
            cc = c_ref[pl.ds(base, 8), :]
            cd = d_ref[pl.ds(base, 8), :]
            for k in (1, 2, 4):
                m = rid < 8 - k
                cd = jnp.where(m, cc * pltpu.roll(cd, 8 - k, 0) + cd, cd)
                cc = jnp.where(m, cc * pltpu.roll(cc, 8 - k, 0), cc)
            gg = cd + cc * gnext
            g_ref[pl.ds(base, 8), :] = gg
            return gg[0:1, :]

        gfirst = lax.fori_loop(0, tm // 8, group, gn_ref[0:1, :])
        gn_ref[...] = jnp.broadcast_to(gfirst, gn_ref.shape)
        g = g_ref[...]

        ext_ref[0:8, :] = jnp.where(first_tile, 0.0, hhalo_ref[...])
        ext_ref[8:, :] = hs
        hprev = _shift_down(ext_ref[...], 1, 8)
        da = g * hprev
        dmult = g * (ig * xc)
        di = g * mult * xc
        dxc = g * mult * ig
        dla = da * a - dmult * a * a / mult
        dr = dla * (-RGLRU_C * sp)
        dlam_ref[...] += jnp.sum(dla * (-RGLRU_C * r), axis=0, keepdims=True) * (-_sigmoid(-lam))
        dpr = dr * r * (1.0 - r)
        dpi = di * ig * (1.0 - ig)
        dprb, dpib, xcb = dpr.astype(BF16), dpi.astype(BF16), xc.astype(BF16)
        dba_ref[...] += jnp.sum(dpr, axis=0, keepdims=True)
        dbx_ref[...] += jnp.sum(dpi, axis=0, keepdims=True)
        dwa_ref[...] += _dot_tn(xcb, dprb)
        dwx_ref[...] += _dot_tn(xcb, dpib)
        dxc = dxc + _dot(dprb, wat_ref[...]) + _dot(dpib, wxt_ref[...])
        dcb_ref[...] += jnp.sum(dxc, axis=0, keepdims=True)
        dcw_ref[3:4, :] += jnp.sum(dxc * x0, axis=0, keepdims=True)
        dcw_ref[2:3, :] += jnp.sum(dxc * x1, axis=0, keepdims=True)
        dcw_ref[1:2, :] += jnp.sum(dxc * x2, axis=0, keepdims=True)
        dcw_ref[0:1, :] += jnp.sum(dxc * x3, axis=0, keepdims=True)
        ext_ref[0:tm, :] = dxc
        ext_ref[tm:tm + 8, :] = dxn_ref[...]
        dxn_ref[...] = dxc[0:8, :]
        ext = ext_ref[...]
        dz_ref[:, 0:W_GRP] = (cw[3:4, :] * dxc + cw[2:3, :] * _shift_up(ext, 1, tm) + cw[1:2, :] * _shift_up(ext, 2, tm)
                              + cw[0:1, :] * _shift_up(ext, 3, tm))

    sq = _full((W_GRP, W_GRP))
    vec = _full((1, W_GRP))
    halo = lambda col: pl.BlockSpec((8, W_GRP), lambda i: (jnp.maximum((nt - 1 - i) * hb - 1, 0), col))
    rev = lambda col: _zblk(tm, col, nt)
    return pl.pallas_call(
        body, grid=(nt,),
        in_specs=[rev(2), rev(3), halo(2), rev(0), halo(0), rev(1), _full((4, W_GRP)), vec, sq, sq, sq, sq, vec, vec, vec,
                  pl.BlockSpec(memory_space=pl.ANY)],
        out_specs=[pl.BlockSpec((tm, 2 * W_GRP), lambda i: (nt - 1 - i, 1)), _full((4, W_GRP)), vec, sq, sq, vec, vec, vec],
        out_shape=[_sds(dz.shape, F32), _sds((4, W_GRP), F32), _sds((1, W_GRP), F32), _sds((W_GRP, W_GRP), F32),
                   _sds((W_GRP, W_GRP), F32), _sds((1, W_GRP), F32), _sds((1, W_GRP), F32), _sds((1, W_GRP), F32)],
        scratch_shapes=[pltpu.VMEM((tm + 8, W_GRP), F32), pltpu.VMEM((tm, W_GRP), F32), pltpu.VMEM((tm, W_GRP), F32),
                        pltpu.VMEM((tm, W_GRP), F32), pltpu.VMEM((8, W_GRP), F32), pltpu.VMEM((8, W_GRP), F32),
                        pltpu.VMEM((8, W_GRP), F32)],
        input_output_aliases={15: 0}, name="mix_b_bwd", compiler_params=_cp("arbitrary"),
    )(z, z, z, hs, hs, dmix, conv_w, conv_b, wa, wx, wa_t, wx_t, ba, bx, lam, dz)


def _tri(n, lower):
    r = lax.broadcasted_iota(jnp.int32, (n, n), 0)
    c = lax.broadcasted_iota(jnp.int32, (n, n), 1)
    return jnp.where((r >= c) if lower else (r <= c), 1.0, 0.0).astype(BF16)


def _causal_stack():
    r = lax.broadcasted_iota(jnp.int32, (N_HEADS * HGRN_CHUNK, HGRN_CHUNK), 0)
    c = lax.broadcasted_iota(jnp.int32, (N_HEADS * HGRN_CHUNK, HGRN_CHUNK), 1)
    m = None
    for h in range(N_HEADS):
        mh = (r >= h * HGRN_CHUNK) & (r < (h + 1) * HGRN_CHUNK) & (r - h * HGRN_CHUNK >= c)
        m = mh if m is None else (m | mh)
    return m


def _stack_heads(x, hm):
    return jnp.concatenate([jnp.where(hm[h], x, 0.0) for h in range(N_HEADS)], axis=0)


def _unstack_heads(xs, hm):
    out = jnp.where(hm[0], xs[0:HGRN_CHUNK], 0.0)
    for h in range(1, N_HEADS):
        out = out + jnp.where(hm[h], xs[h * HGRN_CHUNK:(h + 1) * HGRN_CHUNK], 0.0)
    return out


def _hgrn_chunk(qv, fv, lb, tril):
    sq = _sigmoid(qv)
    qq = qv * sq
    sg = _sigmoid(fv)
    fg = lb + (1.0 - lb) * sg
    kk = 1.0 - fg
    bb = _dot_f32_lhs_exact(tril, jnp.log(fg))
    b_last = bb[HGRN_CHUNK - 1:HGRN_CHUNK, :]
    b_mid = bb[HGRN_CHUNK // 2 - 1:HGRN_CHUNK // 2, :]
    eq = jnp.exp(jnp.minimum(bb - b_mid, EXP_CLAMP))
    ek = jnp.exp(jnp.minimum(b_mid - bb, EXP_CLAMP))
    eb = jnp.exp(bb)
    el = jnp.exp(b_last - bb)
    return sq, qq, sg, fg, kk, b_last, eq, ek, eb, el


def _seg_mean(x, avg):
    return _dot_f32_rhs_exact(x, avg)


def _mix_c_fwd(z, mix, lb, ng):
    t = z.shape[0]
    tm = _mix_tm(t)
    nch = tm // HGRN_CHUNK

    def body(q_ref, f_ref, i_ref, g_ref, lb_ref, ng_ref, mix_in, y_ref, o_ref, ss_ref, s_ref):
        @pl.when(pl.program_id(0) == 0)
        def _():
            s_ref[...] = jnp.zeros_like(s_ref)

        hm = _head_masks()
        bmask = _block_mask()
        causal = _causal_stack()
        tril = _tri(HGRN_CHUNK, True)
        avg = jnp.where(bmask, 1.0 / HEAD_DIM, 0.0).astype(BF16)
        lb, ng = lb_ref[...], ng_ref[...]

        def chunk(c, carry):
            rows = pl.ds(pl.multiple_of(c * HGRN_CHUNK, HGRN_CHUNK), HGRN_CHUNK)
            vv = i_ref[rows, :]
            gv = g_ref[rows, :]
            _, qq, _, _, kk, b_last, eq, ek, eb, el = _hgrn_chunk(q_ref[rows, :], f_ref[rows, :], lb, tril)
            vb = vv.astype(BF16)
            qs = _stack_heads(qq * eq, hm).astype(BF16)
            att = jnp.where(causal, _dot_nt(qs, (kk * ek).astype(BF16)), 0.0)
            o = _unstack_heads(_dot(att.astype(BF16), vb), hm)
            s0 = s_ref[...]
            ss_ref[c] = s0
            o = o + _dot_nt((qq * eb).astype(BF16), s0.astype(BF16))
            s_ref[...] = s0 * jnp.exp(b_last) + jnp.where(bmask, _dot_tn(vb, (kk * el).astype(BF16)), 0.0)
            o_ref[rows, :] = o
            rstd = lax.rsqrt(_seg_mean(o * o, avg) + EPS)
            y_ref[rows, :] = o * rstd * ng * (gv * _sigmoid(gv))
            return carry

        lax.fori_loop(0, nch, chunk, 0)

    vec = _full((1, W_GRP))
    return pl.pallas_call(
        body, grid=(t // tm,),
        in_specs=[_zblk(tm, 4), _zblk(tm, 5), _zblk(tm, 6), _zblk(tm, 7), vec, vec, pl.BlockSpec(memory_space=pl.ANY)],
        out_specs=[_zblk(tm, 2), pl.BlockSpec((tm, W_GRP), lambda i: (i, 0)),
                   pl.BlockSpec((nch, W_GRP, W_GRP), lambda i: (i, 0, 0))],
        out_shape=[_sds(mix.shape, F32), _sds((t, W_GRP), F32), _sds((t // HGRN_CHUNK, W_GRP, W_GRP), F32)],
        scratch_shapes=[pltpu.VMEM((W_GRP, W_GRP), F32)],
        input_output_aliases={6: 0}, name="mix_c_fwd", compiler_params=_cp("arbitrary"),
    )(z, z, z, z, lb, ng, mix)


def _mix_c_bwd(z, dz, dmix, o_pre, states, lb, ng):
    t = z.shape[0]
    tm = _mix_tm(t)
    nt = t // tm
    nch = tm // HGRN_CHUNK

    def body(q_ref, f_ref, i_ref, g_ref, o_ref, ss_ref, dy_ref, lb_ref, ng_ref, dz_in, dz_ref, dlb_ref, dng_ref, ds_ref):
        @pl.when(pl.program_id(0) == 0)
        def _():
            ds_ref[...] = jnp.zeros_like(ds_ref)
            dlb_ref[...] = jnp.zeros_like(dlb_ref)
            dng_ref[...] = jnp.zeros_like(dng_ref)

        hm = _head_masks()
        bmask = _block_mask()
        causal = _causal_stack()
        tril = _tri(HGRN_CHUNK, True)
        triu = _tri(HGRN_CHUNK, False)
        avg = jnp.where(bmask, 1.0 / HEAD_DIM, 0.0).astype(BF16)
        lb, ng = lb_ref[...], ng_ref[...]
        last_row = lax.broadcasted_iota(jnp.int32, (HGRN_CHUNK, W_GRP), 0) == HGRN_CHUNK - 1

        def chunk(j, carry):
            c = nch - 1 - j
            rows = pl.ds(pl.multiple_of(c * HGRN_CHUNK, HGRN_CHUNK), HGRN_CHUNK)
            qv, gv, vv = q_ref[rows, :], g_ref[rows, :], i_ref[rows, :]
            sq, qq, sg, fg, kk, b_last, eq, ek, eb, el = _hgrn_chunk(qv, f_ref[rows, :], lb, tril)
            s0 = ss_ref[c]
            ds1 = ds_ref[...]
            o = o_ref[rows, :]
            dy = dy_ref[rows, :]
            rstd = lax.rsqrt(_seg_mean(o * o, avg) + EPS)
            oh = o * rstd
            sgg = _sigmoid(gv)
            dz_ref[rows, 3 * W_GRP:4 * W_GRP] = dy * oh * ng * (sgg * (1.0 + gv * (1.0 - sgg)))
            don = dy * gv * sgg
            dng_ref[...] += jnp.sum(don * oh, axis=0, keepdims=True)
            doh = don * ng
            do = rstd * (doh - oh * _seg_mean(doh * oh, avg))
            qt, kt, qh, kh = qq * eq, kk * ek, qq * eb, kk * el
            vb, dob = vv.astype(BF16), do.astype(BF16)
            ktb, khb = kt.astype(BF16), kh.astype(BF16)
            ds1b = ds1.astype(BF16)
            qs = _stack_heads(qt, hm).astype(BF16)
            dos = _stack_heads(do, hm).astype(BF16)
            att = jnp.where(causal, _dot_nt(qs, ktb), 0.0).astype(BF16)
            datt = jnp.where(causal, _dot_nt(dos, vb), 0.0).astype(BF16)
            dv = _dot_tn(att, dos) + _dot_nt(khb, ds1b)
            dqt = _unstack_heads(_dot(datt, ktb), hm)
            dkt = _dot_tn(datt, qs)
            dqh = _dot(dob, s0.astype(BF16))
            dkh = _dot(vb, ds1b)
            e_last = jnp.exp(b_last)
            ds_ref[...] = ds1 * e_last + jnp.where(bmask, _dot_tn(dob, qh.astype(BF16)), 0.0)
            dq = dqt * eq + dqh * eb
            dk = dkt * ek + dkh * el
            db = qt * dqt - kt * dkt + qh * dqh - kh * dkh
            db_last = jnp.sum(kh * dkh, axis=0, keepdims=True) + e_last * jnp.sum(ds1 * s0, axis=0, keepdims=True)
            db = db + jnp.where(last_row, db_last, 0.0)
            dlogf = _dot_f32_lhs_exact(triu, db)
            dfg = dlogf / fg - dk
            dz_ref[rows, W_GRP:2 * W_GRP] = dfg * (1.0 - lb) * sg * (1.0 - sg)
            dlb_ref[...] += jnp.sum(dfg * (1.0 - sg), axis=0, keepdims=True)
            dz_ref[rows, 0:W_GRP] = dq * (sq * (1.0 + qv * (1.0 - sq)))
            dz_ref[rows, 2 * W_GRP:3 * W_GRP] = dv
            return carry

        lax.fori_loop(0, nch, chunk, 0)

    vec = _full((1, W_GRP))
    rev = lambda col: _zblk(tm, col, nt)
    return pl.pallas_call(
        body, grid=(nt,),
        in_specs=[rev(4), rev(5), rev(6), rev(7), rev(0), pl.BlockSpec((nch, W_GRP, W_GRP), lambda i: (nt - 1 - i, 0, 0)),
                  rev(2), vec, vec, pl.BlockSpec(memory_space=pl.ANY)],
        out_specs=[pl.BlockSpec((tm, 4 * W_GRP), lambda i: (nt - 1 - i, 1)), vec, vec],
        out_shape=[_sds(dz.shape, F32), _sds((1, W_GRP), F32), _sds((1, W_GRP), F32)],
        scratch_shapes=[pltpu.VMEM((W_GRP, W_GRP), F32)],
        input_output_aliases={9: 0}, name="mix_c_bwd", compiler_params=_cp("arbitrary"),
    )(z, z, z, z, o_pre, states, dmix, lb, ng, dz)


def _pool_select(hm, s2, s4, s8, s16):
    return jnp.where(hm[0], s2, jnp.where(hm[1], s4, jnp.where(hm[2], s8, s16)))


def _pool_counts(hm, row0, tm):
    pos = (row0 + 1 + lax.broadcasted_iota(jnp.int32, (tm, W_GRP), 0)).astype(F32)
    win = _pool_select(hm, 2.0, 4.0, 8.0, 16.0)
    return jnp.minimum(pos, win)


def _pooled(ext_ref, x, halo, hm, cnt):
    ext_ref[0:POOL_HALO, :] = halo
    ext_ref[POOL_HALO:, :] = x
    e = ext_ref[...]
    s2 = e + pltpu.roll(e, 1, 0)
    s4 = s2 + pltpu.roll(s2, 2, 0)
    s8 = s4 + pltpu.roll(s4, 4, 0)
    s16 = s8 + pltpu.roll(s8, 8, 0)
    return _pool_select(hm, s2, s4, s8, s16)[POOL_HALO:] / cnt - x


def _mix_d_fwd(z, mix, wd, scale):
    t = z.shape[0]
    tm = _mix_tm(t)

    def body(x_ref, wd_ref, sc_ref, mix_in, o_ref, ext_ref, halo_ref):
        i = pl.program_id(0)

        @pl.when(i == 0)
        def _():
            halo_ref[...] = jnp.zeros_like(halo_ref)

        hm = _head_masks()
        x = x_ref[...]
        pooled = _pooled(ext_ref, x, halo_ref[...], hm, _pool_counts(hm, i * tm, tm))
        halo_ref[...] = x_ref[tm - POOL_HALO:tm, :]
        o_ref[...] = _dot(pooled.astype(BF16), wd_ref[...]) * sc_ref[...]

    return pl.pallas_call(
        body, grid=(t // tm,),
        in_specs=[_zblk(tm, 8), _full((W_GRP, W_GRP)), _full((1, W_GRP)), pl.BlockSpec(memory_space=pl.ANY)],
        out_specs=_zblk(tm, 3), out_shape=_sds(mix.shape, F32),
        scratch_shapes=[pltpu.VMEM((tm + POOL_HALO, W_GRP), F32), pltpu.VMEM((POOL_HALO, W_GRP), F32)],
        input_output_aliases={3: 0}, name="mix_d_fwd", compiler_params=_cp("arbitrary"),
    )(z, wd, scale, mix)


def _mix_d_bwd(z, dz, dmix, wd, wd_t, scale):
    t = z.shape[0]
    tm = _mix_tm(t)
    nt = t // tm
    hb = tm // POOL_HALO

    def body(x_ref, xhalo_ref, dy_ref, wd_ref, wdt_ref, sc_ref, dz_in, dz_ref, dwd_ref, dsc_ref, ext_ref, en_ref):
        i = pl.program_id(0)
        ri = nt - 1 - i

        @pl.when(i == 0)
        def _():
            en_ref[...] = jnp.zeros_like(en_ref)
            dwd_ref[...] = jnp.zeros_like(dwd_ref)
            dsc_ref[...] = jnp.zeros_like(dsc_ref)

        hm = _head_masks()
        cnt = _pool_counts(hm, ri * tm, tm)
        x = x_ref[...]
        pooled = _pooled(ext_ref, x, jnp.where(ri == 0, 0.0, xhalo_ref[...]), hm, cnt)
        pb = pooled.astype(BF16)
        dy = dy_ref[...]
        dsc_ref[...] += jnp.sum(dy * _dot(pb, wd_ref[...]), axis=0, keepdims=True)
        dyw = (dy * sc_ref[...]).astype(BF16)
        dwd_ref[...] += _dot_tn(pb, dyw)
        dpool = _dot(dyw, wdt_ref[...])
        e = dpool / cnt
        ext_ref[0:tm, :] = e
        ext_ref[tm:, :] = en_ref[...]
        en_ref[...] = e[0:POOL_HALO, :]
        ee = ext_ref[...]
        n = tm + POOL_HALO
        r2 = ee + pltpu.roll(ee, n - 1, 0)
        r4 = r2 + pltpu.roll(r2, n - 2, 0)
        r8 = r4 + pltpu.roll(r4, n - 4, 0)
        r16 = r8 + pltpu.roll(r8, n - 8, 0)
        dz_ref[...] = _pool_select(hm, r2, r4, r8, r16)[:tm] - dpool

    sq = _full((W_GRP, W_GRP))
    vec = _full((1, W_GRP))
    return pl.pallas_call(
        body, grid=(nt,),
        in_specs=[_zblk(tm, 8, nt), pl.BlockSpec((POOL_HALO, W_GRP), lambda i: (jnp.maximum((nt - 1 - i) * hb - 1, 0), 8)),
                  _zblk(tm, 3, nt), sq, sq, vec, pl.BlockSpec(memory_space=pl.ANY)],
        out_specs=[_zblk(tm, 8, nt), sq, vec],
        out_shape=[_sds(dz.shape, F32), _sds((W_GRP, W_GRP), F32), _sds((1, W_GRP), F32)],
        scratch_shapes=[pltpu.VMEM((tm + POOL_HALO, W_GRP), F32), pltpu.VMEM((POOL_HALO, W_GRP), F32)],
        input_output_aliases={6: 0}, name="mix_d_bwd", compiler_params=_cp("arbitrary"),
    )(z, z, dmix, wd, wd_t, scale, dz)


def _as2d(a):
    if a.ndim == 1:
        return a.reshape(1, a.shape[0])
    return a.reshape(-1, a.shape[-1])


def _adamw(w, g, m, v, name):
    shape = w.shape
    w2, g2, m2, v2 = _as2d(w), _as2d(g), _as2d(m), _as2d(v)
    rows, cols = w2.shape
    tr = _pick(rows, (1024, 512, 256, 128, 64, 32, 16, 8))
    if tr * cols * 4 * 14 > VMEM_LIMIT_BYTES:
        tr = _pick(rows, (256, 128, 64, 32, 16, 8))

    def body(w_ref, g_ref, m_ref, v_ref, d_ref, nm_ref, nv_ref):
        gv = g_ref[...]
        mn = ADAM_B1 * m_ref[...] + (1.0 - ADAM_B1) * gv
        vn = ADAM_B2 * v_ref[...] + (1.0 - ADAM_B2) * (gv * gv)
        m_hat = mn / (1.0 - ADAM_B1 ** ADAM_STEP)
        v_hat = vn / (1.0 - ADAM_B2 ** ADAM_STEP)
        d_ref[...] = -ADAM_LR * (m_hat / (jnp.sqrt(v_hat) + ADAM_EPS) + ADAM_WD * w_ref[...])
        nm_ref[...] = mn
        nv_ref[...] = vn

    blk = pl.BlockSpec((tr, cols), lambda i: (i, 0))
    outs = pl.pallas_call(
        body, grid=(rows // tr,), in_specs=[blk] * 4, out_specs=[blk] * 3, out_shape=[_sds((rows, cols), F32)] * 3,
        name=name, compiler_params=_cp("parallel"),
    )(w2, g2, m2, v2)
    return tuple(o.reshape(shape) for o in outs)


def _slot_sum(slots, name):
    _, rows, cols = slots.shape
    tr = _pick(rows, (512, 256, 128, 64, 32, 16, 8))

    def body(s_ref, o_ref):
        acc = s_ref[0].astype(F32)
        for k in range(1, N_DEV):
            acc = acc + s_ref[k].astype(F32)
        o_ref[...] = acc

    return pl.pallas_call(
        body, grid=(rows // tr,), in_specs=[pl.BlockSpec((N_DEV, tr, cols), lambda i: (0, i, 0))],
        out_specs=pl.BlockSpec((tr, cols), lambda i: (i, 0)), out_shape=_sds((rows, cols), F32),
        name=name, compiler_params=_cp("parallel"),
    )(slots)


def _me():
    return lax.axis_index("x"), lax.axis_index("y"), lax.axis_index("c")


def _other_chips(x, y):
    return [(1 - x, y), (x, 1 - y), (1 - x, 1 - y)]


HBM_SPEC = pl.BlockSpec(memory_space=pl.ANY)

ROWS, COLS, SLABS = "rows", "cols", "slabs"


def _aligned(v, m):
    return v if isinstance(v, int) else pl.multiple_of(v, m)


def _piece(ref, layout, shard, half, r, cs, halves=True):
    hr = r // 2 if halves else r
    h0 = half * hr if halves else 0
    if layout == ROWS:
        return ref.at[:, pl.ds(_aligned(shard * r + h0, 8), hr), :]
    if layout == COLS:
        return ref.at[:, pl.ds(h0, hr), pl.ds(_aligned(shard * cs, 128), cs)]
    return ref.at[shard, :, pl.ds(h0, hr), :]


def _gathered_shape(shard_shape, layout):
    nl, r, cs = shard_shape
    if layout == ROWS:
        return (nl, 4 * r, cs)
    if layout == COLS:
        return (nl, r, 4 * cs)
    return (4, nl, r, cs)


def _gather_weights(shards, layouts, split):
    n = len(shards)
    n_sem = 6 * n

    def body(*refs):
        src = refs[:n]
        dst = refs[n:2 * n]
        send_sems, recv_sems, local_sems = refs[2 * n:]
        x, y, c = _me()
        sibling = (x, y, 1 - c)
        chips = _other_chips(x, y)
        my_shard = 2 * x + y
        copies = []
        for w in range(n):
            _, r, cs = src[w].shape
            lay = layouts[w]
            mine = pltpu.make_async_copy(src[w], _piece(dst[w], lay, my_shard, 0, r, cs, halves=False), local_sems.at[w])
            mine.start()
            copies.append(mine)
        sends = []
        for w in range(n):
            _, r, cs = src[w].shape
            lay, sp = layouts[w], split[w]
            for j, chip in enumerate(chips):
                k = 6 * w + j
                cp = pltpu.make_async_remote_copy(
                    src_ref=_piece(src[w], ROWS, 0, c, r, cs, halves=sp),
                    dst_ref=_piece(dst[w], lay, my_shard, c, r, cs, halves=sp),
                    send_sem=send_sems.at[k], recv_sem=recv_sems.at[k], device_id=(*chip, c), device_id_type=MESH_ID)
                cp.start()
                sends.append(cp)
        for w in range(n):
            _, r, cs = src[w].shape
            lay, sp = layouts[w], split[w]
            for j, chip in enumerate(chips):
                k = 6 * w + j
                got = _piece(dst[w], lay, 2 * chip[0] + chip[1], c, r, cs, halves=sp)
                pltpu.make_async_remote_copy(src_ref=got, dst_ref=got, send_sem=send_sems.at[k], recv_sem=recv_sems.at[k],
                                             device_id=(*chip, c), device_id_type=MESH_ID).wait_recv()
                if sp:
                    fw = pltpu.make_async_remote_copy(src_ref=got, dst_ref=got, send_sem=send_sems.at[k + 3],
                                                      recv_sem=recv_sems.at[k + 3], device_id=sibling, device_id_type=MESH_ID)
                    fw.start()
                    sends.append(fw)
        for w in range(n):
            _, r, cs = src[w].shape
            if split[w]:
                for j, chip in enumerate(chips):
                    k = 6 * w + j + 3
                    got = _piece(dst[w], layouts[w], 2 * chip[0] + chip[1], 1 - c, r, cs)
                    pltpu.make_async_remote_copy(src_ref=got, dst_ref=got, send_sem=send_sems.at[k], recv_sem=recv_sems.at[k],
                                                 device_id=sibling, device_id_type=MESH_ID).wait_recv()
        for cp in sends:
            cp.wait_send()
        for cp in copies:
            cp.wait()

    return pl.pallas_call(
        body, in_specs=[HBM_SPEC] * n, out_specs=[HBM_SPEC] * n,
        out_shape=[_sds(_gathered_shape(s.shape, lay), s.dtype) for s, lay in zip(shards, layouts)],
        scratch_shapes=[pltpu.SemaphoreType.DMA((n_sem,)), pltpu.SemaphoreType.DMA((n_sem,)), pltpu.SemaphoreType.DMA((n,))],
        name="gather_weights",
    )(*shards)


def _peer(x, y, c, k):
    fx, fy, fc = (k >> 2) & 1, (k >> 1) & 1, k & 1
    px = 1 - x if fx else x
    py = 1 - y if fy else y
    pc = 1 - c if fc else c
    return px, py, pc


def _push_partials(grads, layouts, small):
    n = len(grads)
    shard_shapes = []
    for g, lay in zip(grads, layouts):
        if lay == ROWS:
            shard_shapes.append((g.shape[0], g.shape[1] // 4, g.shape[2]))
        elif lay == COLS:
            shard_shapes.append((g.shape[0], g.shape[1], g.shape[2] // 4))
        else:
            shard_shapes.append(tuple(g.shape[1:]))

    def body(*refs):
        src = refs[:n]
        small_ref = refs[n]
        dst = refs[n + 1:2 * n + 1]
        small_dst = refs[2 * n + 1]
        send_sems, recv_sems, local_sems = refs[2 * n + 2:]
        x, y, c = _me()
        my_slot = 4 * x + 2 * y + c
        locals_, sends = [], []
        for w in range(n + 1):
            if w < n:
                _, r, cs = shard_shapes[w]
                mine = pltpu.make_async_copy(_piece(src[w], layouts[w], 2 * x + y, c, r, cs), dst[w].at[my_slot], local_sems.at[w])
            else:
                mine = pltpu.make_async_copy(small_ref, small_dst.at[my_slot], local_sems.at[w])
            mine.start()
            locals_.append(mine)
        for w in range(n + 1):
            for k in range(1, N_DEV):
                px, py, pc = _peer(x, y, c, k)
                if w < n:
                    _, r, cs = shard_shapes[w]
                    s_ref, d_ref = _piece(src[w], layouts[w], 2 * px + py, pc, r, cs), dst[w].at[my_slot]
                else:
                    s_ref, d_ref = small_ref, small_dst.at[my_slot]
                cp = pltpu.make_async_remote_copy(src_ref=s_ref, dst_ref=d_ref, send_sem=send_sems.at[w, k - 1],
                                                  recv_sem=recv_sems.at[w, k - 1], device_id=(px, py, pc), device_id_type=MESH_ID)
                cp.start()
                sends.append(cp)
        for w in range(n + 1):
            for k in range(1, N_DEV):
                px, py, pc = _peer(x, y, c, k)
                slot = (dst[w] if w < n else small_dst).at[4 * px + 2 * py + pc]
                pltpu.make_async_remote_copy(src_ref=slot, dst_ref=slot, send_sem=send_sems.at[w, k - 1],
                                             recv_sem=recv_sems.at[w, k - 1], device_id=(px, py, pc),
                                             device_id_type=MESH_ID).wait_recv()
        for cp in sends:
            cp.wait_send()
        for cp in locals_:
            cp.wait()

    out_shape = [_sds((N_DEV, nl, r // 2, cs), BF16) for nl, r, cs in shard_shapes] + [_sds((N_DEV,) + small.shape, F32)]
    return pl.pallas_call(
        body, in_specs=[HBM_SPEC] * (n + 1), out_specs=[HBM_SPEC] * (n + 1), out_shape=out_shape,
        scratch_shapes=[pltpu.SemaphoreType.DMA((n + 1, N_DEV - 1)), pltpu.SemaphoreType.DMA((n + 1, N_DEV - 1)),
                        pltpu.SemaphoreType.DMA((n + 1,))],
        name="push_partials",
    )(*grads, small)


def _swap_halves(halves):
    n = len(halves)

    def body(*refs):
        src = refs[:n]
        dst = refs[n:2 * n]
        send_sems, recv_sems, local_sems = refs[2 * n:]
        x, y, c = _me()
        sibling = (x, y, 1 - c)
        started = []
        for w in range(n):
            hr = src[w].shape[1]
            mine = pltpu.make_async_copy(src[w], dst[w].at[:, pl.ds(pl.multiple_of(c * hr, 8), hr), :], local_sems.at[w])
            mine.start()
            cp = pltpu.make_async_remote_copy(
                src_ref=src[w], dst_ref=dst[w].at[:, pl.ds(pl.multiple_of(c * hr, 8), hr), :],
                send_sem=send_sems.at[w], recv_sem=recv_sems.at[w], device_id=sibling, device_id_type=MESH_ID)
            cp.start()
            started.append((mine, cp))
        for w in range(n):
            hr = src[w].shape[1]
            got = dst[w].at[:, pl.ds(pl.multiple_of((1 - c) * hr, 8), hr), :]
            pltpu.make_async_remote_copy(src_ref=got, dst_ref=got, send_sem=send_sems.at[w], recv_sem=recv_sems.at[w],
                                         device_id=sibling, device_id_type=MESH_ID).wait_recv()
        for mine, cp in started:
            cp.wait_send()
            mine.wait()

    return pl.pallas_call(
        body, in_specs=[HBM_SPEC] * n, out_specs=[HBM_SPEC] * n,
        out_shape=[_sds((h.shape[0], 2 * h.shape[1], h.shape[2]), F32) for h in halves],
        scratch_shapes=[pltpu.SemaphoreType.DMA((n,)), pltpu.SemaphoreType.DMA((n,)), pltpu.SemaphoreType.DMA((n,))],
        name="swap_halves",
    )(*halves)


BIG = ("w_in", "w_out", "w_up", "w_down", "w_pe", "w_pg")
BIG_LAYOUT = {"w_in": SLABS, "w_out": ROWS, "w_up": COLS, "w_down": ROWS, "w_pe": COLS, "w_pg": ROWS}
SMALL = ("norm1_g", "a_ln_g", "a_ln_b", "a_ws", "a_bs", "b_conv_w", "b_conv_b", "b_wa", "b_ba", "b_wx", "b_bx", "b_lam",
         "c_lb", "c_norm_g", "d_w", "d_scale", "norm2_g", "ffn_conv_w", "ffn_conv_b", "norm3_g", "final_g")
SMALL_SHARDED = ("b_conv_w", "ffn_conv_w")
WEIGHTS = ("norm1_g", "w_in", "a_ln_g", "a_ln_b", "a_ws", "a_bs", "b_conv_w", "b_conv_b", "b_wa", "b_ba", "b_wx", "b_bx",
           "b_lam", "c_lb", "c_norm_g", "d_w", "d_scale", "w_out", "norm2_g", "w_up", "ffn_conv_w", "ffn_conv_b", "w_down",
           "norm3_g", "w_pe", "w_pg", "final_g")
ARGS = ("x", "p") + WEIGHTS + ("loss_target",) + tuple("m_" + n for n in WEIGHTS) + tuple("v_" + n for n in WEIGHTS)


def _block_diag(w):
    eye = jnp.eye(N_HEADS, dtype=w.dtype)
    return (eye[:, None, :, None] * w[:, :, None, :]).reshape(W_GRP, W_GRP)


def _diag_blocks(m):
    m4 = m.reshape(N_HEADS, HEAD_DIM, N_HEADS, HEAD_DIM)
    return jnp.stack([m4[h, :, h, :] for h in range(N_HEADS)])


def _lower_bounds(c_lb):
    lbs = jnp.cumsum(jax.nn.softmax(c_lb, axis=0), axis=0)
    return lbs - lbs[0:1]


def kernel(x, p, norm1_g, w_in, a_ln_g, a_ln_b, a_ws, a_bs, b_conv_w, b_conv_b, b_wa, b_ba, b_wx, b_bx, b_lam, c_lb, c_norm_g, d_w, d_scale, w_out, norm2_g, w_up, ffn_conv_w, ffn_conv_b, w_down, norm3_g, w_pe, w_pg, final_g, loss_target, m_norm1_g, m_w_in, m_a_ln_g, m_a_ln_b, m_a_ws, m_a_bs, m_b_conv_w, m_b_conv_b, m_b_wa, m_b_ba, m_b_wx, m_b_bx, m_b_lam, m_c_lb, m_c_norm_g, m_d_w, m_d_scale, m_w_out, m_norm2_g, m_w_up, m_ffn_conv_w, m_ffn_conv_b, m_w_down, m_norm3_g, m_w_pe, m_w_pg, m_final_g, v_norm1_g, v_w_in, v_a_ln_g, v_a_ln_b, v_a_ws, v_a_bs, v_b_conv_w, v_b_conv_b, v_b_wa, v_b_ba, v_b_wx, v_b_bx, v_b_lam, v_c_lb, v_c_norm_g, v_d_w, v_d_scale, v_w_out, v_norm2_g, v_w_up, v_ffn_conv_w, v_ffn_conv_b, v_w_down, v_norm3_g, v_w_pe, v_w_pg, v_final_g):
    return _step((x, p, norm1_g, w_in, a_ln_g, a_ln_b, a_ws, a_bs, b_conv_w, b_conv_b, b_wa, b_ba, b_wx, b_bx, b_lam, c_lb, c_norm_g, d_w, d_scale, w_out, norm2_g, w_up, ffn_conv_w, ffn_conv_b, w_down, norm3_g, w_pe, w_pg, final_g, loss_target, m_norm1_g, m_w_in, m_a_ln_g, m_a_ln_b, m_a_ws, m_a_bs, m_b_conv_w, m_b_conv_b, m_b_wa, m_b_ba, m_b_wx, m_b_bx, m_b_lam, m_c_lb, m_c_norm_g, m_d_w, m_d_scale, m_w_out, m_norm2_g, m_w_up, m_ffn_conv_w, m_ffn_conv_b, m_w_down, m_norm3_g, m_w_pe, m_w_pg, m_final_g, v_norm1_g, v_w_in, v_a_ln_g, v_a_ln_b, v_a_ws, v_a_bs, v_b_conv_w, v_b_conv_b, v_b_wa, v_b_ba, v_b_wx, v_b_bx, v_b_lam, v_c_lb, v_c_norm_g, v_d_w, v_d_scale, v_w_out, v_norm2_g, v_w_up, v_ffn_conv_w, v_ffn_conv_b, v_w_down, v_norm3_g, v_w_pe, v_w_pg, v_final_g))


def _step(args):
    a = dict(zip(ARGS, args, strict=True))
    x0 = a["x"][0]
    target = a["loss_target"][0]
    nl = a["norm1_g"].shape[0]
    t, d = x0.shape
    f = a["w_down"].shape[1] * 4
    cx, cy, _ = _me()
    my_shard = 2 * cx + cy

    gathered = _gather_weights(
        [a[n].astype(BF16) for n in BIG] + [a[n] for n in SMALL_SHARDED],
        [BIG_LAYOUT[n] for n in BIG] + [SLABS, SLABS], [True] * len(BIG) + [False, False])
    wfull = dict(zip(BIG + SMALL_SHARDED, gathered))
    w_in = jnp.moveaxis(wfull["w_in"], 0, 2).reshape(nl, d, -1)
    zc = w_in.shape[-1]
    b_conv_w = jnp.moveaxis(wfull["b_conv_w"], 0, 2).reshape(nl, 4, W_GRP)
    ffn_conv_w = jnp.moveaxis(wfull["ffn_conv_w"], 0, 2).reshape(nl, 3, 2 * f)
    w_out, w_up, w_down, w_pe, w_pg = (wfull[n] for n in ("w_out", "w_up", "w_down", "w_pe", "w_pg"))

    lbs, lbs_vjp = jax.vjp(_lower_bounds, a["c_lb"])
    tril = jnp.tril(jnp.ones((GMLP_CHUNK, GMLP_CHUNK), F32))

    def layer_params(l):
        q = {}
        q["wm"] = (a["a_ws"][l] * tril).astype(BF16)
        q["wm_t"] = jnp.swapaxes(q["wm"], 1, 2)
        q["bs_t"] = jnp.repeat(a["a_bs"][l].T, HEAD_DIM, axis=1)
        for nm in ("b_wa", "b_wx", "d_w"):
            bd = _block_diag(a[nm][l]).astype(BF16)
            q[nm], q[nm + "_t"] = bd, bd.T
        for nm in ("a_ln_g", "a_ln_b", "b_conv_b", "b_ba", "b_bx", "b_lam", "d_scale"):
            q[nm] = a[nm][l].reshape(1, W_GRP)
        q["lb"] = lbs[l].reshape(1, W_GRP)
        q["ng"] = jnp.tile(a["c_norm_g"][l], N_HEADS).reshape(1, W_GRP)
        q["b_conv_w"] = b_conv_w[l]
        q["ffn_conv_w"] = ffn_conv_w[l]
        q["ffn_conv_b"] = a["ffn_conv_b"][l].reshape(1, 2 * f)
        return q

    saved = []
    xl = x0
    for l in range(nl):
        q = layer_params(l)
        s = {"x0": xl}
        s["h1"] = _rms_fwd(xl, a["norm1_g"][l], "rms1_fwd")
        s["z"] = _mm(s["h1"], w_in, "nn", b_layer=l, out_dtype=F32, name="mm_z")
        mix = _mix_a_fwd(s["z"], d, q["a_ln_g"], q["a_ln_b"], q["wm"], q["bs_t"])
        mix, s["hs"] = _mix_b_fwd(s["z"], mix, q["b_conv_w"], q["b_conv_b"], q["b_wa"], q["b_wx"], q["b_ba"], q["b_bx"],
                                  q["b_lam"])
        mix, s["o_pre"], s["states"] = _mix_c_fwd(s["z"], mix, q["lb"], q["ng"])
        s["mix"] = _mix_d_fwd(s["z"], mix, q["d_w"], q["d_scale"])
        s["x1"] = _mm(s["mix"], w_out, "nn", b_layer=l, res=xl, out_dtype=F32, name="mm_out")
        s["h2"] = _rms_fwd(s["x1"], a["norm2_g"][l], "rms2_fwd")
        s["hf_g"] = _mm(s["h2"], w_up, "nn", b_layer=l, n=f, out_dtype=F32, name="mm_up_g")
        s["hf_v"] = _mm(s["h2"], w_up, "nn", b_layer=l, n=f, b_noff=f, out_dtype=F32, name="mm_up_v")
        s["act"] = _ffn_act_fwd(s["hf_g"], s["hf_v"], q["ffn_conv_w"], q["ffn_conv_b"])
        s["x2"] = _mm(s["act"], w_down, "nn", b_layer=l, res=s["x1"], out_dtype=F32, name="mm_down")
        s["h3"] = _rms_fwd(s["x2"], a["norm3_g"][l], "rms3_fwd")
        s["pre"] = _mm(s["h3"], w_pg, "nn", b_layer=l, out_dtype=F32, name="mm_pg")
        s["pe"] = _mm(a["p"][l, 0], w_pe, "nn", b_layer=l, out_dtype=F32, name="mm_pe")
        xl = _ple_fwd(s["x2"], s["pe"], s["pre"])
        saved.append(s)

    dx, g_final, loss = _final_loss(xl, a["final_g"], target)
    loss = lax.psum(loss[0, 0], ("x", "y", "c"))

    gbig = {n: None for n in BIG}
    gsm = {n: [None] * nl for n in SMALL if n != "final_g"}
    for l in reversed(range(nl)):
        q, s = layer_params(l), saved[l]
        dpe, dpre = _ple_bwd(dx, s["pe"], s["pre"])
        gbig["w_pe"] = _mm(a["p"][l, 0], dpe, "tn", out_dtype=BF16, name="mm_dwpe", layer=l, nlayers=nl, out_buf=gbig["w_pe"])
        gbig["w_pg"] = _mm(s["h3"], dpre, "tn", out_dtype=BF16, name="mm_dwpg", layer=l, nlayers=nl, out_buf=gbig["w_pg"])
        dh3 = _mm(dpre, w_pg, "nt", b_layer=l, out_dtype=BF16, name="mm_dh3")
        dx2, gsm["norm3_g"][l] = _rms_bwd(dh3, s["x2"], a["norm3_g"][l], dx, "rms3_bwd")
        gbig["w_down"] = _mm(s["act"], dx2, "tn", out_dtype=BF16, name="mm_dwdown", layer=l, nlayers=nl, out_buf=gbig["w_down"])
        dact = _mm(dx2, w_down, "nt", b_layer=l, out_dtype=BF16, name="mm_dact")
        dhf_g, dhf_v, dcw_g, dcw_v, dcb_g, dcb_v = _ffn_act_bwd(dact, s["hf_g"], s["hf_v"], q["ffn_conv_w"], q["ffn_conv_b"])
        gsm["ffn_conv_w"][l] = jnp.concatenate([dcw_g, dcw_v], axis=1)
        gsm["ffn_conv_b"][l] = jnp.concatenate([dcb_g, dcb_v], axis=1)
        gbig["w_up"] = _mm(s["h2"], dhf_g, "tn", out_dtype=BF16, name="mm_dwup_g", layer=l, nlayers=nl, out_n=2 * f,
                           out_buf=gbig["w_up"])
        gbig["w_up"] = _mm(s["h2"], dhf_v, "tn", out_dtype=BF16, name="mm_dwup_v", layer=l, nlayers=nl, out_n=2 * f, o_noff=f,
                           out_buf=gbig["w_up"])
        dh2 = _mm(dhf_g, w_up, "nt", b_layer=l, out_dtype=F32, name="mm_dh2_g")
        dh2 = _mm(dhf_v, w_up, "nt", b_layer=l, b_koff=f, res=dh2, out_dtype=F32, name="mm_dh2_v")
        dx1, gsm["norm2_g"][l] = _rms_bwd(dh2, s["x1"], a["norm2_g"][l], dx2, "rms2_bwd")
        gbig["w_out"] = _mm(s["mix"], dx1, "tn", out_dtype=BF16, name="mm_dwout", layer=l, nlayers=nl, out_buf=gbig["w_out"])
        dmix = _mm(dx1, w_out, "nt", b_layer=l, out_dtype=F32, name="mm_dmix")
        dz, gsm["a_ln_g"][l], gsm["a_ln_b"][l], dws, dbs_t = _mix_a_bwd(s["z"], dmix, q["a_ln_g"], q["a_ln_b"], q["wm"],
                                                                       q["wm_t"], q["bs_t"])
        gsm["a_ws"][l] = dws * tril
        gsm["a_bs"][l] = dbs_t.reshape(GMLP_CHUNK, N_HEADS, HEAD_DIM).sum(-1).T
        (dz, gsm["b_conv_w"][l], gsm["b_conv_b"][l], dwa, dwx, gsm["b_ba"][l], gsm["b_bx"][l],
         gsm["b_lam"][l]) = _mix_b_bwd(s["z"], dz, dmix, s["hs"], q["b_conv_w"], q["b_conv_b"], q["b_wa"], q["b_wx"],
                                       q["b_wa_t"], q["b_wx_t"], q["b_ba"], q["b_bx"], q["b_lam"])
        gsm["b_wa"][l], gsm["b_wx"][l] = _diag_blocks(dwa), _diag_blocks(dwx)
        dz, gsm["c_lb"][l], dng = _mix_c_bwd(s["z"], dz, dmix, s["o_pre"], s["states"], q["lb"], q["ng"])
        gsm["c_norm_g"][l] = dng.reshape(N_HEADS, HEAD_DIM).sum(0)
        dz, dwd, gsm["d_scale"][l] = _mix_d_bwd(s["z"], dz, dmix, q["d_w"], q["d_w_t"], q["d_scale"])
        gsm["d_w"][l] = _diag_blocks(dwd)
        gbig["w_in"] = _mm(s["h1"], dz, "tn", out_dtype=BF16, name="mm_dwin", layer=l, nlayers=nl, out_buf=gbig["w_in"])
        dh1 = _mm(dz, w_in, "nt", b_layer=l, out_dtype=BF16, name="mm_dh1")
        dx, gsm["norm1_g"][l] = _rms_bwd(dh1, s["x0"], a["norm1_g"][l], dx1, "rms1_bwd")

    grad_x = dx[None]

    gsmall = {n: jnp.stack([g.reshape(a[n].shape[1:]) if n not in SMALL_SHARDED else g for g in gsm[n]]) for n in gsm}
    gsmall["c_lb"] = lbs_vjp(gsmall["c_lb"])[0]
    gsmall["final_g"] = g_final.reshape(-1)
    flat = jnp.concatenate([gsmall[n].reshape(-1) for n in SMALL])
    n_small = flat.shape[0]
    pad = (-n_small) % 1024
    small_slab = jnp.pad(flat, (0, pad)).reshape(-1, 128)

    gbig["w_in"] = jnp.moveaxis(gbig["w_in"].reshape(nl, d, 4, zc // 4), 2, 0)
    pushed = _push_partials([gbig[n] for n in BIG], [BIG_LAYOUT[n] for n in BIG], small_slab)
    halves = [_slot_sum(sl.reshape(N_DEV, -1, sl.shape[-1]), "sum_" + n).reshape(sl.shape[1:]) for n, sl in zip(BIG, pushed)]
    shard_grads = dict(zip(BIG, _swap_halves(halves)))
    small_sum = _slot_sum(pushed[-1], "sum_small").reshape(-1)[:n_small]

    grads, off = {}, 0
    for n in SMALL:
        shape = gsmall[n].shape
        size = 1
        for s_ in shape:
            size *= s_
        g = small_sum[off:off + size].reshape(shape)
        off += size
        if n in SMALL_SHARDED:
            cs = a[n].shape[-1]
            g = lax.dynamic_slice_in_dim(g, my_shard * cs, cs, axis=2)
        grads[n] = g
    grads.update(shard_grads)

    outs_g, outs_d, outs_m, outs_v = [], [], [], []
    for n in WEIGHTS:
        dlt, nm, nv = _adamw(a[n], grads[n], a["m_" + n], a["v_" + n], "adamw_" + n)
        outs_g.append(grads[n])
        outs_d.append(dlt)
        outs_m.append(nm)
        outs_v.append(nv)
    return (loss, grad_x, *outs_g, *outs_d, *outs_m, *outs_v)
```

```python
import functools

import jax
import jax.numpy as jnp
from jax import lax
from jax.experimental import pallas as pl
from jax.experimental.pallas import tpu as pltpu

F32 = jnp.float32
BF16 = jnp.bfloat16
EPS = 1e-6
HEAD_DIM = 64
N_HEADS = 4
W_GRP = HEAD_DIM * N_HEADS
GMLP_CHUNK = 128
HGRN_CHUNK = 64
RGLRU_C = 8.0
POOL_HALO = 16
EXP_CLAMP = 80.0
ADAM_LR, ADAM_B1, ADAM_B2, ADAM_EPS, ADAM_WD, ADAM_STEP = 0.001, 0.9, 0.999, 1e-08, 0.01, 10
VMEM_LIMIT_BYTES = 56 * 1024 * 1024
TILE_PREFS = (1024, 1408, 768, 512, 256, 128)
ROW_TILE_PREFS = (512, 256, 128, 64, 32, 16, 8)
MESH_ID = pl.DeviceIdType.MESH
N_DEV = 8


def _pick(n, prefs=TILE_PREFS):
    for p in prefs:
        if n % p == 0:
            return p
    return n


def _cp(*sem):
    return pltpu.CompilerParams(dimension_semantics=sem if sem else None, vmem_limit_bytes=VMEM_LIMIT_BYTES)


def _sds(shape, dtype):
    return jax.ShapeDtypeStruct(tuple(shape), dtype)


_GELU_C = 0.7978845608028654
_GELU_A = 0.044715


def _gelu(x):
    return 0.5 * x * (1.0 + jnp.tanh(_GELU_C * (x + _GELU_A * x * x * x)))


def _gelu_and_grad(x):
    t = jnp.tanh(_GELU_C * (x + _GELU_A * x * x * x))
    g = 0.5 * x * (1.0 + t)
    dg = 0.5 * (1.0 + t) + 0.5 * x * (1.0 - t * t) * _GELU_C * (1.0 + 3.0 * _GELU_A * x * x)
    return g, dg


def _sigmoid(x):
    return 1.0 / (1.0 + jnp.exp(-x))


def _dot(a, b):
    return jnp.dot(a, b, preferred_element_type=F32)


def _dot_nt(a, b):
    return lax.dot_general(a, b, (((1,), (1,)), ((), ())), preferred_element_type=F32)


def _dot_tn(a, b):
    return lax.dot_general(a, b, (((0,), (0,)), ((), ())), preferred_element_type=F32)


def _split3(x):
    hi = x.astype(BF16)
    r1 = x - hi.astype(F32)
    mid = r1.astype(BF16)
    lo = (r1 - mid.astype(F32)).astype(BF16)
    return hi, mid, lo


def _dot_f32_rhs_exact(x, m_bf16):
    hi, mid, lo = _split3(x)
    return _dot(hi, m_bf16) + _dot(mid, m_bf16) + _dot(lo, m_bf16)


def _dot_f32_lhs_exact(m_bf16, x):
    hi, mid, lo = _split3(x)
    return _dot(m_bf16, hi) + _dot(m_bf16, mid) + _dot(m_bf16, lo)


def _head_masks(width=W_GRP):
    lane = lax.broadcasted_iota(jnp.int32, (1, width), 1)
    return [(lane >= h * HEAD_DIM) & (lane < (h + 1) * HEAD_DIM) for h in range(N_HEADS)]


def _block_mask(n=W_GRP):
    r = lax.broadcasted_iota(jnp.int32, (n, n), 0)
    c = lax.broadcasted_iota(jnp.int32, (n, n), 1)
    m = None
    for h in range(N_HEADS):
        mh = (r >= h * HEAD_DIM) & (r < (h + 1) * HEAD_DIM) & (c >= h * HEAD_DIM) & (c < (h + 1) * HEAD_DIM)
        m = mh if m is None else (m | mh)
    return m


def _mm(a, b, mode, *, out_dtype, name, res=None, b_slabs=False, n=None, b_noff=0, b_koff=0,
        out_slabs=0, out_buf=None, out_n=None, o_noff=0):
    if mode == "tn":
        k_dim, m_dim = a.shape
    else:
        m_dim, k_dim = a.shape
    if mode == "nt":
        n_dim = b.shape[-2]
    else:
        n_dim = n if n is not None else (b.shape[0] * b.shape[2] if b_slabs else b.shape[1])
    n_total = out_n if out_n is not None else n_dim
    tm, tn, tk = _pick(m_dim), _pick(n_dim), _pick(k_dim)
    if b_slabs and mode == "nt":
        tk = _pick(b.shape[2])
    elif b_slabs:
        tn = _pick(b.shape[2])
    elif out_slabs:
        tn = _pick(n_total // out_slabs)
    nk = k_dim // tk
    assert b_noff % tn == 0 and b_koff % tk == 0 and o_noff % tn == 0 and n_dim % tn == 0 and k_dim % tk == 0
    bn0, bk0, on0 = b_noff // tn, b_koff // tk, o_noff // tn
    dims = {"nn": (((1,), (0,)), ((), ())), "nt": (((1,), (1,)), ((), ())), "tn": (((0,), (0,)), ((), ()))}[mode]

    if mode == "tn":
        a_spec = pl.BlockSpec((tk, tm), lambda i, j, k: (k, i))
    else:
        a_spec = pl.BlockSpec((tm, tk), lambda i, j, k: (i, k))
    if not b_slabs:
        if mode == "nt":
            b_spec = pl.BlockSpec((tn, tk), lambda i, j, k: (j + bn0, k + bk0))
        else:
            b_spec = pl.BlockSpec((tk, tn), lambda i, j, k: (k + bk0, j + bn0))
    elif mode == "nt":
        bper = b.shape[2] // tk
        b_spec = pl.BlockSpec((None, tn, tk), lambda i, j, k: ((k + bk0) // bper, j, (k + bk0) % bper))
    else:
        bper = b.shape[2] // tn
        b_spec = pl.BlockSpec((None, tk, tn), lambda i, j, k: ((j + bn0) // bper, k, (j + bn0) % bper))
    in_specs = [a_spec, b_spec]
    args = [a, b]
    if res is not None:
        in_specs.append(pl.BlockSpec((tm, tn), lambda i, j, k: (i, j)))
        args.append(res)
    if out_slabs:
        oper = n_total // out_slabs // tn
        out_shape = _sds((out_slabs, m_dim, n_total // out_slabs), out_dtype)
        out_spec = pl.BlockSpec((None, tm, tn), lambda i, j, k: ((j + on0) // oper, i, (j + on0) % oper))
    else:
        out_shape = _sds((m_dim, n_total), out_dtype)
        out_spec = pl.BlockSpec((tm, tn), lambda i, j, k: (i, j + on0))
    aliases = {}
    if out_buf is not None:
        in_specs.append(pl.BlockSpec(memory_space=pl.ANY))
        args.append(out_buf)
        aliases = {len(args) - 1: 0}
    has_res = res is not None
    has_buf = out_buf is not None

    def body(*refs):
        a_ref, b_ref = refs[0], refs[1]
        res_ref = refs[2] if has_res else None
        o_ref = refs[2 + int(has_res) + int(has_buf)]
        acc_ref = refs[-1] if nk > 1 else None
        part = lax.dot_general(a_ref[...].astype(BF16), b_ref[...].astype(BF16), dims, preferred_element_type=F32)

        def finish(v):
            if has_res:
                v = v + res_ref[...]
            o_ref[...] = v.astype(o_ref.dtype)

        if nk == 1:
            finish(part)
        else:
            kk = pl.program_id(2)

            @pl.when(kk == 0)
            def _():
                acc_ref[...] = part

            @pl.when(kk > 0)
            def _():
                acc_ref[...] += part

            @pl.when(kk == nk - 1)
            def _():
                finish(acc_ref[...])

    return pl.pallas_call(
        body, grid=(m_dim // tm, n_dim // tn, nk), in_specs=in_specs, out_specs=out_spec, out_shape=out_shape,
        scratch_shapes=[pltpu.VMEM((tm, tn), F32)] if nk > 1 else [],
        input_output_aliases=aliases, name=name, compiler_params=_cp("parallel", "parallel", "arbitrary"),
    )(*args)


def _rms_fwd(x, g, name):
    t, d = x.shape
    tm = _pick(t, ROW_TILE_PREFS)

    def body(x_ref, g_ref, o_ref):
        xv = x_ref[...]
        r = lax.rsqrt(jnp.mean(xv * xv, axis=-1, keepdims=True) + EPS)
        o_ref[...] = (xv * r * g_ref[...]).astype(o_ref.dtype)

    return pl.pallas_call(
        body, grid=(t // tm,),
        in_specs=[pl.BlockSpec((tm, d), lambda i: (i, 0)), pl.BlockSpec((1, d), lambda i: (0, 0))],
        out_specs=pl.BlockSpec((tm, d), lambda i: (i, 0)), out_shape=_sds((t, d), BF16),
        name=name, compiler_params=_cp("parallel"),
    )(x, g.reshape(1, d))


def _rms_bwd(dh, x, g, dres, name):
    t, d = x.shape
    tm = _pick(t, ROW_TILE_PREFS)

    def body(dh_ref, x_ref, g_ref, dres_ref, dx_ref, dg_ref):
        i = pl.program_id(0)
        xv = x_ref[...]
        dy = dh_ref[...].astype(F32)
        r = lax.rsqrt(jnp.mean(xv * xv, axis=-1, keepdims=True) + EPS)
        dyg = dy * g_ref[...]
        dot = jnp.mean(dyg * xv, axis=-1, keepdims=True)
        dx_ref[...] = dres_ref[...] + r * dyg - xv * (r * r * r) * dot
        part = jnp.sum(dy * xv * r, axis=0, keepdims=True)

        @pl.when(i == 0)
        def _():
            dg_ref[...] = part

        @pl.when(i > 0)
        def _():
            dg_ref[...] += part

    row = pl.BlockSpec((tm, d), lambda i: (i, 0))
    vec = pl.BlockSpec((1, d), lambda i: (0, 0))
    return pl.pallas_call(
        body, grid=(t // tm,), in_specs=[row, row, vec, row], out_specs=[row, vec],
        out_shape=[_sds((t, d), F32), _sds((1, d), F32)], name=name, compiler_params=_cp("arbitrary"),
    )(dh, x, g.reshape(1, d), dres)


def _final_loss(x, g, target):
    t, d = x.shape
    tm = _pick(t, ROW_TILE_PREFS)

    def body(x_ref, g_ref, t_ref, dx_ref, dg_ref, loss_ref):
        i = pl.program_id(0)
        xv = x_ref[...]
        gv = g_ref[...]
        r = lax.rsqrt(jnp.mean(xv * xv, axis=-1, keepdims=True) + EPS)
        err = xv * r * gv - t_ref[...]
        lpart = (0.5 / d) * jnp.sum(jnp.sum(err * err, axis=1, keepdims=True), axis=0, keepdims=True)
        dy = err * (1.0 / d)
        dyg = dy * gv
        dot = jnp.mean(dyg * xv, axis=-1, keepdims=True)
        dx_ref[...] = r * dyg - xv * (r * r * r) * dot
        part = jnp.sum(dy * xv * r, axis=0, keepdims=True)

        @pl.when(i == 0)
        def _():
            dg_ref[...] = part
            loss_ref[...] = lpart

        @pl.when(i > 0)
        def _():
            dg_ref[...] += part
            loss_ref[...] += lpart

    row = pl.BlockSpec((tm, d), lambda i: (i, 0))
    vec = pl.BlockSpec((1, d), lambda i: (0, 0))
    return pl.pallas_call(
        body, grid=(t // tm,), in_specs=[row, vec, row], out_specs=[row, vec, pl.BlockSpec((1, 1), lambda i: (0, 0))],
        out_shape=[_sds((t, d), F32), _sds((1, d), F32), _sds((1, 1), F32)], name="final_loss",
        compiler_params=_cp("arbitrary"),
    )(x, g.reshape(1, d), target)


def _shift_down(ext, k, halo):
    return pltpu.roll(ext, k, 0)[halo:]


def _shift_up(ext, k, tm):
    return pltpu.roll(ext, ext.shape[0] - k, 0)[:tm]


def _ffn_tiles(t, f):
    return _pick(t, ROW_TILE_PREFS), _pick(f, (256, 128))


def _ffn_act_fwd(hf_g, hf_v, conv_w, conv_b):
    t, f = hf_g.shape
    tm, cn = _ffn_tiles(t, f)
    nf = f // cn

    def body(g_ref, v_ref, wg_ref, wv_ref, bg_ref, bv_ref, o_ref, ext_ref, hg_ref, hv_ref):
        i = pl.program_id(1)

        @pl.when(i == 0)
        def _():
            hg_ref[...] = jnp.zeros_like(hg_ref)
            hv_ref[...] = jnp.zeros_like(hv_ref)

        def conv(x_ref, halo_ref, w_ref, b_ref):
            ext_ref[0:8, :] = halo_ref[...]
            ext_ref[8:, :] = x_ref[...]
            halo_ref[...] = x_ref[tm - 8:tm, :]
            ext = ext_ref[...]
            w = w_ref[...]
            return b_ref[...] + w[2:3, :] * ext[8:] + w[1:2, :] * _shift_down(ext, 1, 8) + w[0:1, :] * _shift_down(ext, 2, 8)

        gc = conv(g_ref, hg_ref, wg_ref, bg_ref)
        vc = conv(v_ref, hv_ref, wv_ref, bv_ref)
        o_ref[...] = (_gelu(gc) * vc).astype(o_ref.dtype)

    blk = pl.BlockSpec((tm, cn), lambda j, i: (i, j))
    return pl.pallas_call(
        body, grid=(nf, t // tm),
        in_specs=[blk, blk, pl.BlockSpec((3, cn), lambda j, i: (0, j)), pl.BlockSpec((3, cn), lambda j, i: (0, j + nf)),
                  pl.BlockSpec((1, cn), lambda j, i: (0, j)), pl.BlockSpec((1, cn), lambda j, i: (0, j + nf))],
        out_specs=blk, out_shape=_sds((t, f), BF16),
        scratch_shapes=[pltpu.VMEM((tm + 8, cn), F32), pltpu.VMEM((8, cn), F32), pltpu.VMEM((8, cn), F32)],
        name="ffn_act_fwd", compiler_params=_cp("parallel", "arbitrary"),
    )(hf_g, hf_v, conv_w, conv_w, conv_b, conv_b)


def _ffn_act_bwd(dact, hf_g, hf_v, conv_w, conv_b):
    t, f = hf_g.shape
    tm, cn = _ffn_tiles(t, f)
    nf, nt = f // cn, t // tm
    hb = tm // 8

    def body(da_ref, g_ref, v_ref, gh_ref, vh_ref, wg_ref, wv_ref, bg_ref, bv_ref,
             dg_ref, dv_ref, dwg_ref, dwv_ref, dbg_ref, dbv_ref, ext_ref, cg_ref, cv_ref):
        i = pl.program_id(1)
        first_tile = i == nt - 1

        @pl.when(i == 0)
        def _():
            cg_ref[...] = jnp.zeros_like(cg_ref)
            cv_ref[...] = jnp.zeros_like(cv_ref)
            dwg_ref[...] = jnp.zeros_like(dwg_ref)
            dwv_ref[...] = jnp.zeros_like(dwv_ref)
            dbg_ref[...] = jnp.zeros_like(dbg_ref)
            dbv_ref[...] = jnp.zeros_like(dbv_ref)

        def shifted(x_ref, halo_ref):
            ext_ref[0:8, :] = jnp.where(first_tile, 0.0, halo_ref[...])
            ext_ref[8:, :] = x_ref[...]
            ext = ext_ref[0:tm + 8, :]
            return ext[8:], _shift_down(ext, 1, 8), _shift_down(ext, 2, 8)

        wg, wv = wg_ref[...], wv_ref[...]
        g0, g1, g2 = shifted(g_ref, gh_ref)
        gc = bg_ref[...] + wg[2:3, :] * g0 + wg[1:2, :] * g1 + wg[0:1, :] * g2
        v0, v1, v2 = shifted(v_ref, vh_ref)
        vc = bv_ref[...] + wv[2:3, :] * v0 + wv[1:2, :] * v1 + wv[0:1, :] * v2
        da = da_ref[...].astype(F32)
        gel, dgel = _gelu_and_grad(gc)
        dgc = da * vc * dgel
        dvc = da * gel

        def back(dc, carry_ref, w, x0, x1, x2, dx_ref, dw_ref, db_ref):
            ext_ref[0:tm, :] = dc
            ext_ref[tm:tm + 8, :] = carry_ref[...]
            carry_ref[...] = dc[0:8, :]
            ext = ext_ref[0:tm + 8, :]
            dx = w[2:3, :] * dc + w[1:2, :] * _shift_up(ext, 1, tm) + w[0:1, :] * _shift_up(ext, 2, tm)
            dx_ref[...] = dx.astype(dx_ref.dtype)
            dw_ref[0:1, :] += jnp.sum(dc * x2, axis=0, keepdims=True)
            dw_ref[1:2, :] += jnp.sum(dc * x1, axis=0, keepdims=True)
            dw_ref[2:3, :] += jnp.sum(dc * x0, axis=0, keepdims=True)
            db_ref[...] += jnp.sum(dc, axis=0, keepdims=True)

        back(dgc, cg_ref, wg, g0, g1, g2, dg_ref, dwg_ref, dbg_ref)
        back(dvc, cv_ref, wv, v0, v1, v2, dv_ref, dwv_ref, dbv_ref)

    blk = pl.BlockSpec((tm, cn), lambda j, i: (nt - 1 - i, j))
    halo = pl.BlockSpec((8, cn), lambda j, i: (jnp.maximum((nt - 1 - i) * hb - 1, 0), j))
    w3g = pl.BlockSpec((3, cn), lambda j, i: (0, j))
    w3v = pl.BlockSpec((3, cn), lambda j, i: (0, j + nf))
    b1g = pl.BlockSpec((1, cn), lambda j, i: (0, j))
    b1v = pl.BlockSpec((1, cn), lambda j, i: (0, j + nf))
    acc3 = pl.BlockSpec((3, cn), lambda j, i: (0, j))
    acc1 = pl.BlockSpec((1, cn), lambda j, i: (0, j))
    return pl.pallas_call(
        body, grid=(nf, nt),
        in_specs=[blk, blk, blk, halo, halo, w3g, w3v, b1g, b1v],
        out_specs=[blk, blk, acc3, acc3, acc1, acc1],
        out_shape=[_sds((t, f), BF16), _sds((t, f), BF16), _sds((3, f), F32), _sds((3, f), F32),
                   _sds((1, f), F32), _sds((1, f), F32)],
        scratch_shapes=[pltpu.VMEM((tm + 8, cn), F32), pltpu.VMEM((8, cn), F32), pltpu.VMEM((8, cn), F32)],
        name="ffn_act_bwd", compiler_params=_cp("parallel", "arbitrary"),
    )(dact, hf_g, hf_v, hf_g, hf_v, conv_w, conv_w, conv_b, conv_b)


def _ple_fwd(x2, pe, pre):
    t, d = x2.shape
    tm = _pick(t, ROW_TILE_PREFS)

    def body(x_ref, pe_ref, pre_ref, o_ref):
        o_ref[...] = x_ref[...] + pe_ref[...] * _sigmoid(pre_ref[...])

    row = pl.BlockSpec((tm, d), lambda i: (i, 0))
    return pl.pallas_call(body, grid=(t // tm,), in_specs=[row, row, row], out_specs=row,
                          out_shape=_sds((t, d), F32), name="ple_fwd", compiler_params=_cp("parallel"))(x2, pe, pre)


def _ple_bwd(dx3, pe, pre):
    t, d = dx3.shape
    tm = _pick(t, ROW_TILE_PREFS)

    def body(dx_ref, pe_ref, pre_ref, dpe_ref, dpre_ref):
        gate = _sigmoid(pre_ref[...])
        dx = dx_ref[...]
        dpe_ref[...] = (dx * gate).astype(dpe_ref.dtype)
        dpre_ref[...] = (dx * pe_ref[...] * gate * (1.0 - gate)).astype(dpre_ref.dtype)

    row = pl.BlockSpec((tm, d), lambda i: (i, 0))
    return pl.pallas_call(body, grid=(t // tm,), in_specs=[row, row, row], out_specs=[row, row],
                          out_shape=[_sds((t, d), BF16), _sds((t, d), BF16)], name="ple_bwd",
                          compiler_params=_cp("parallel"))(dx3, pe, pre)


def _mix_tm(t):
    return _pick(t, (512, 256, 128))


def _zblk(tm, col, rev_nt=None):
    if rev_nt is None:
        return pl.BlockSpec((tm, W_GRP), lambda i: (i, col))
    return pl.BlockSpec((tm, W_GRP), lambda i: (rev_nt - 1 - i, col))


def _full(shape):
    nd = len(shape)
    return pl.BlockSpec(tuple(shape), lambda i: (0,) * nd)


def _gmlp_sv(wm_ref, vnc, bs, hm):
    sv = bs
    for h in range(N_HEADS):
        sv = sv + jnp.where(hm[h], _dot(wm_ref[h], vnc), 0.0)
    return sv


def _layernorm(v, g, b):
    mu = jnp.mean(v, axis=-1, keepdims=True)
    vc = v - mu
    rs = lax.rsqrt(jnp.mean(vc * vc, axis=-1, keepdims=True) + EPS)
    xhat = vc * rs
    return xhat, rs, xhat * g + b


def _mix_a_fwd(z, d_mix, ln_g, ln_b, wm, bs_t):
    t = z.shape[0]
    tm = _mix_tm(t)

    def body(u_ref, v_ref, g_ref, b_ref, wm_ref, bs_ref, o_ref):
        hm = _head_masks()
        ug = _gelu(u_ref[...])
        _, _, vn = _layernorm(_gelu(v_ref[...]), g_ref[...], b_ref[...])
        vnb = vn.astype(BF16)
        for n in range(tm // GMLP_CHUNK):
            sl = slice(n * GMLP_CHUNK, (n + 1) * GMLP_CHUNK)
            o_ref[sl, :] = ug[sl] * _gmlp_sv(wm_ref, vnb[sl], bs_ref[...], hm)

    return pl.pallas_call(
        body, grid=(t // tm,),
        in_specs=[_zblk(tm, 0), _zblk(tm, 1), _full((1, W_GRP)), _full((1, W_GRP)), _full(wm.shape), _full(bs_t.shape)],
        out_specs=_zblk(tm, 0), out_shape=_sds((t, d_mix), F32), name="mix_a_fwd", compiler_params=_cp("parallel"),
    )(z, z, ln_g, ln_b, wm, bs_t)


def _mix_a_bwd(z, dmix, ln_g, ln_b, wm, wm_t, bs_t):
    t, zc = z.shape
    tm = _mix_tm(t)

    def body(u_ref, v_ref, dy_ref, g_ref, b_ref, wm_ref, wmt_ref, bs_ref, dz_ref, dg_ref, db_ref, dws_ref, dbs_ref):
        i = pl.program_id(0)

        @pl.when(i == 0)
        def _():
            dg_ref[...] = jnp.zeros_like(dg_ref)
            db_ref[...] = jnp.zeros_like(db_ref)
            dws_ref[...] = jnp.zeros_like(dws_ref)
            dbs_ref[...] = jnp.zeros_like(dbs_ref)

        hm = _head_masks()
        ug, dug = _gelu_and_grad(u_ref[...])
        vg, dvg = _gelu_and_grad(v_ref[...])
        gv = g_ref[...]
        xhat, rs, vn = _layernorm(vg, gv, b_ref[...])
        vnb = vn.astype(BF16)
        dy = dy_ref[...]
        for n in range(tm // GMLP_CHUNK):
            sl = slice(n * GMLP_CHUNK, (n + 1) * GMLP_CHUNK)
            vnc = vnb[sl]
            sv = _gmlp_sv(wm_ref, vnc, bs_ref[...], hm)
            dsv = dy[sl] * ug[sl]
            dz_ref[sl, 0:W_GRP] = dy[sl] * sv * dug[sl]
            dbs_ref[...] += dsv
            dsvb = dsv.astype(BF16)
            dvn = jnp.zeros((GMLP_CHUNK, W_GRP), F32)
            for h in range(N_HEADS):
                dws_ref[h] += _dot_nt(jnp.where(hm[h], dsv, 0.0).astype(BF16), vnc)
                dvn = dvn + jnp.where(hm[h], _dot(wmt_ref[h], dsvb), 0.0)
            xh = xhat[sl]
            dg_ref[...] += jnp.sum(dvn * xh, axis=0, keepdims=True)
            db_ref[...] += jnp.sum(dvn, axis=0, keepdims=True)
            dxh = dvn * gv
            dvg_c = rs[sl] * (dxh - jnp.mean(dxh, axis=-1, keepdims=True) - xh * jnp.mean(dxh * xh, axis=-1, keepdims=True))
            dz_ref[sl, W_GRP:2 * W_GRP] = dvg_c * dvg[sl]

    return pl.pallas_call(
        body, grid=(t // tm,),
        in_specs=[_zblk(tm, 0), _zblk(tm, 1), _zblk(tm, 0), _full((1, W_GRP)), _full((1, W_GRP)), _full(wm.shape),
                  _full(wm_t.shape), _full(bs_t.shape)],
        out_specs=[pl.BlockSpec((tm, 2 * W_GRP), lambda i: (i, 0)), _full((1, W_GRP)), _full((1, W_GRP)),
                   _full(wm.shape), _full(bs_t.shape)],
        out_shape=[_sds((t, zc), F32), _sds((1, W_GRP), F32), _sds((1, W_GRP), F32), _sds(wm.shape, F32),
                   _sds(bs_t.shape, F32)],
        name="mix_a_bwd", compiler_params=_cp("arbitrary"),
    )(z, z, dmix, ln_g, ln_b, wm, wm_t, bs_t)


def _softplus(x):
    return jnp.maximum(x, 0.0) + jnp.log(1.0 + jnp.exp(-jnp.abs(x)))


def _neg_expm1(x):
    series = -x * (1.0 + x * 0.5 * (1.0 + x * (1.0 / 3.0) * (1.0 + x * 0.25 * (1.0 + x * 0.2))))
    return jnp.where(x > -0.1, series, 1.0 - jnp.exp(x))


def _rglru_gates(ext_ref, x_ref, halo, cw, cb, wa_ref, wx_ref, ba, bx, lam):
    ext_ref[0:8, :] = halo
    ext_ref[8:, :] = x_ref[...]
    ext = ext_ref[...]
    x0, x1, x2, x3 = ext[8:], _shift_down(ext, 1, 8), _shift_down(ext, 2, 8), _shift_down(ext, 3, 8)
    xc = cb + cw[3:4, :] * x0 + cw[2:3, :] * x1 + cw[1:2, :] * x2 + cw[0:1, :] * x3
    xcb = xc.astype(BF16)
    r = _sigmoid(_dot(xcb, wa_ref[...]) + ba)
    ig = _sigmoid(_dot(xcb, wx_ref[...]) + bx)
    sp = _softplus(-lam)
    la = -RGLRU_C * r * sp
    a = jnp.exp(la)
    mult = jnp.sqrt(_neg_expm1(2.0 * la))
    return (x0, x1, x2, x3), xc, r, ig, sp, a, mult


def _mix_b_fwd(z, mix, conv_w, conv_b, wa, wx, ba, bx, lam):
    t = z.shape[0]
    tm = _mix_tm(t)

    def body(x_ref, gb_ref, cw_ref, cb_ref, wa_ref, wx_ref, ba_ref, bx_ref, lam_ref, mix_in, o_ref, hs_ref,
             ext_ref, a_ref, b_ref, xh_ref, hc_ref):
        i = pl.program_id(0)

        @pl.when(i == 0)
        def _():
            xh_ref[...] = jnp.zeros_like(xh_ref)
            hc_ref[...] = jnp.zeros_like(hc_ref)

        _, xc, _, ig, _, a, mult = _rglru_gates(ext_ref, x_ref, xh_ref[...], cw_ref[...], cb_ref[...], wa_ref, wx_ref,
                                                ba_ref[...], bx_ref[...], lam_ref[...])
        xh_ref[...] = x_ref[tm - 8:tm, :]
        a_ref[...] = a
        b_ref[...] = mult * (ig * xc)
        rid = lax.broadcasted_iota(jnp.int32, (8, W_GRP), 0)

        def group(gi, hprev):
            base = pl.multiple_of(gi * 8, 8)
            ca = a_ref[pl.ds(base, 8), :]
            cb = b_ref[pl.ds(base, 8), :]
            for k in (1, 2, 4):
                m = rid >= k
                cb = jnp.where(m, ca * pltpu.roll(cb, k, 0) + cb, cb)
                ca = jnp.where(m, ca * pltpu.roll(ca, k, 0), ca)
            hh = cb + ca * hprev
            hs_ref[pl.ds(base, 8), :] = hh
            return hh[7:8, :]

        hlast = lax.fori_loop(0, tm // 8, group, hc_ref[0:1, :])
        hc_ref[...] = jnp.broadcast_to(hlast, hc_ref.shape)
        o_ref[...] = hs_ref[...] * _gelu(gb_ref[...])

    sq = _full((W_GRP, W_GRP))
    vec = _full((1, W_GRP))
    return pl.pallas_call(
        body, grid=(t // tm,),
        in_specs=[_zblk(tm, 2), _zblk(tm, 3), _full((4, W_GRP)), vec, sq, sq, vec, vec, vec, pl.BlockSpec(memory_space=pl.ANY)],
        out_specs=[_zblk(tm, 1), pl.BlockSpec((tm, W_GRP), lambda i: (i, 0))],
        out_shape=[_sds(mix.shape, F32), _sds((t, W_GRP), F32)],
        scratch_shapes=[pltpu.VMEM((tm + 8, W_GRP), F32), pltpu.VMEM((tm, W_GRP), F32), pltpu.VMEM((tm, W_GRP), F32),
                        pltpu.VMEM((8, W_GRP), F32), pltpu.VMEM((8, W_GRP), F32)],
        input_output_aliases={9: 0}, name="mix_b_fwd", compiler_params=_cp("arbitrary"),
    )(z, z, conv_w, conv_b, wa, wx, ba, bx, lam, mix)


def _mix_b_bwd(z, dz, dmix, hs, conv_w, conv_b, wa, wx, wa_t, wx_t, ba, bx, lam):
    t = z.shape[0]
    tm = _mix_tm(t)
    nt = t // tm
    hb = tm // 8

    def body(x_ref, gb_ref, xhalo_ref, hs_ref, hhalo_ref, dy_ref, cw_ref, cb_ref, wa_ref, wx_ref, wat_ref, wxt_ref,
             ba_ref, bx_ref, lam_ref, dz_in, dz_ref, dcw_ref, dcb_ref, dwa_ref, dwx_ref, dba_ref, dbx_ref, dlam_ref,
             ext_ref, c_ref, d_ref, g_ref, an_ref, gn_ref, dxn_ref):
        i = pl.program_id(0)
        first_tile = i == nt - 1

        @pl.when(i == 0)
        def _():
            for ref in (dcw_ref, dcb_ref, dwa_ref, dwx_ref, dba_ref, dbx_ref, dlam_ref, an_ref, gn_ref, dxn_ref):
                ref[...] = jnp.zeros_like(ref)

        cw, lam = cw_ref[...], lam_ref[...]
        xhalo = jnp.where(first_tile, 0.0, xhalo_ref[...])
        (x0, x1, x2, x3), xc, r, ig, sp, a, mult = _rglru_gates(
            ext_ref, x_ref, xhalo, cw, cb_ref[...], wa_ref, wx_ref, ba_ref[...], bx_ref[...], lam)
        hs = hs_ref[...]
        dy = dy_ref[...]
        gel, dgel = _gelu_and_grad(gb_ref[...])
        dz_ref[:, W_GRP:2 * W_GRP] = dy * hs * dgel

        ext_ref[0:tm, :] = a
        ext_ref[tm:tm + 8, :] = an_ref[...]
        an_ref[...] = a[0:8, :]
        c_ref[...] = _shift_up(ext_ref[...], 1, tm)
        d_ref[...] = dy * gel
        rid = lax.broadcasted_iota(jnp.int32, (8, W_GRP), 0)

        def group(j, gnext):
            base = pl.multiple_of((tm // 8 - 1 - j) * 8, 8)
            cc = c_ref[pl.ds(base, 8), :]
            cd = d_ref[pl.ds(base, 8), :]
            for k in (1, 2, 4):
                m = rid < 8 - k
                cd = jnp.where(m, cc * pltpu.roll(cd, 8 - k, 0) + cd, cd)
                cc = jnp.where(m, cc * pltpu.roll(cc, 8 - k, 0), cc)
            gg = cd + cc * gnext
            g_ref[pl.ds(base, 8), :] = gg
            return gg[0:1, :]

        gfirst = lax.fori_loop(0, tm // 8, group, gn_ref[0:1, :])
        gn_ref[...] = jnp.broadcast_to(gfirst, gn_ref.shape)
        g = g_ref[...]

        ext_ref[0:8, :] = jnp.where(first_tile, 0.0, hhalo_ref[...])
        ext_ref[8:, :] = hs
        hprev = _shift_down(ext_ref[...], 1, 8)
        da = g * hprev
        dmult = g * (ig * xc)
        di = g * mult * xc
        dxc = g * mult * ig
        dla = da * a - dmult * a * a / mult
        dr = dla * (-RGLRU_C * sp)
        dlam_ref[...] += jnp.sum(dla * (-RGLRU_C * r), axis=0, keepdims=True) * (-_sigmoid(-lam))
        dpr = dr * r * (1.0 - r)
        dpi = di * ig * (1.0 - ig)
        dprb, dpib, xcb = dpr.astype(BF16), dpi.astype(BF16), xc.astype(BF16)
        dba_ref[...] += jnp.sum(dpr, axis=0, keepdims=True)
        dbx_ref[...] += jnp.sum(dpi, axis=0, keepdims=True)
        dwa_ref[...] += _dot_tn(xcb, dprb)
        dwx_ref[...] += _dot_tn(xcb, dpib)
        dxc = dxc + _dot(dprb, wat_ref[...]) + _dot(dpib, wxt_ref[...])
        dcb_ref[...] += jnp.sum(dxc, axis=0, keepdims=True)
        dcw_ref[3:4, :] += jnp.sum(dxc * x0, axis=0, keepdims=True)
        dcw_ref[2:3, :] += jnp.sum(dxc * x1, axis=0, keepdims=True)
        dcw_ref[1:2, :] += jnp.sum(dxc * x2, axis=0, keepdims=True)
        dcw_ref[0:1, :] += jnp.sum(dxc * x3, axis=0, keepdims=True)
        ext_ref[0:tm, :] = dxc
        ext_ref[tm:tm + 8, :] = dxn_ref[...]
        dxn_ref[...] = dxc[0:8, :]
        ext = ext_ref[...]
        dz_ref[:, 0:W_GRP] = (cw[3:4, :] * dxc + cw[2:3, :] * _shift_up(ext, 1, tm) + cw[1:2, :] * _shift_up(ext, 2, tm)
                              + cw[0:1, :] * _shift_up(ext, 3, tm))

    sq = _full((W_GRP, W_GRP))
    vec = _full((1, W_GRP))
    halo = lambda col: pl.BlockSpec((8, W_GRP), lambda i: (jnp.maximum((nt - 1 - i) * hb - 1, 0), col))
    rev = lambda col: _zblk(tm, col, nt)
    return pl.pallas_call(
        body, grid=(nt,),
        in_specs=[rev(2), rev(3), halo(2), rev(0), halo(0), rev(1), _full((4, W_GRP)), vec, sq, sq, sq, sq, vec, vec, vec,
                  pl.BlockSpec(memory_space=pl.ANY)],
        out_specs=[pl.BlockSpec((tm, 2 * W_GRP), lambda i: (nt - 1 - i, 1)), _full((4, W_GRP)), vec, sq, sq, vec, vec, vec],
        out_shape=[_sds(dz.shape, F32), _sds((4, W_GRP), F32), _sds((1, W_GRP), F32), _sds((W_GRP, W_GRP), F32),
                   _sds((W_GRP, W_GRP), F32), _sds((1, W_GRP), F32), _sds((1, W_GRP), F32), _sds((1, W_GRP), F32)],
        scratch_shapes=[pltpu.VMEM((tm + 8, W_GRP), F32), pltpu.VMEM((tm, W_GRP), F32), pltpu.VMEM((tm, W_GRP), F32),
                        pltpu.VMEM((tm, W_GRP), F32), pltpu.VMEM((8, W_GRP), F32), pltpu.VMEM((8, W_GRP), F32),
                        pltpu.VMEM((8, W_GRP), F32)],
        input_output_aliases={15: 0}, name="mix_b_bwd", compiler_params=_cp("arbitrary"),
    )(z, z, z, hs, hs, dmix, conv_w, conv_b, wa, wx, wa_t, wx_t, ba, bx, lam, dz)


def _tri(n, lower):
    r = lax.broadcasted_iota(jnp.int32, (n, n), 0)
    c = lax.broadcasted_iota(jnp.int32, (n, n), 1)
    return jnp.where((r >= c) if lower else (r <= c), 1.0, 0.0).astype(BF16)


def _causal_stack():
    r = lax.broadcasted_iota(jnp.int32, (N_HEADS * HGRN_CHUNK, HGRN_CHUNK), 0)
    c = lax.broadcasted_iota(jnp.int32, (N_HEADS * HGRN_CHUNK, HGRN_CHUNK), 1)
    m = None
    for h in range(N_HEADS):
        mh = (r >= h * HGRN_CHUNK) & (r < (h + 1) * HGRN_CHUNK) & (r - h * HGRN_CHUNK >= c)
        m = mh if m is None else (m | mh)
    return m


def _stack_heads(x, hm):
    return jnp.concatenate([jnp.where(hm[h], x, 0.0) for h in range(N_HEADS)], axis=0)


def _unstack_heads(xs, hm):
    out = jnp.where(hm[0], xs[0:HGRN_CHUNK], 0.0)
    for h in range(1, N_HEADS):
        out = out + jnp.where(hm[h], xs[h * HGRN_CHUNK:(h + 1) * HGRN_CHUNK], 0.0)
    return out


def _hgrn_chunk(qv, fv, lb, tril):
    sq = _sigmoid(qv)
    qq = qv * sq
    sg = _sigmoid(fv)
    fg = lb + (1.0 - lb) * sg
    kk = 1.0 - fg
    bb = _dot_f32_lhs_exact(tril, jnp.log(fg))
    b_last = bb[HGRN_CHUNK - 1:HGRN_CHUNK, :]
    b_mid = bb[HGRN_CHUNK // 2 - 1:HGRN_CHUNK // 2, :]
    eq = jnp.exp(jnp.minimum(bb - b_mid, EXP_CLAMP))
    ek = jnp.exp(jnp.minimum(b_mid - bb, EXP_CLAMP))
    eb = jnp.exp(bb)
    el = jnp.exp(b_last - bb)
    return sq, qq, sg, fg, kk, b_last, eq, ek, eb, el


def _seg_mean(x, avg):
    return _dot_f32_rhs_exact(x, avg)


def _mix_c_fwd(z, mix, lb, ng):
    t = z.shape[0]
    tm = _mix_tm(t)
    nch = tm // HGRN_CHUNK

    def body(q_ref, f_ref, i_ref, g_ref, lb_ref, ng_ref, mix_in, y_ref, o_ref, ss_ref, s_ref):
        @pl.when(pl.program_id(0) == 0)
        def _():
            s_ref[...] = jnp.zeros_like(s_ref)

        hm = _head_masks()
        bmask = _block_mask()
        causal = _causal_stack()
        tril = _tri(HGRN_CHUNK, True)
        avg = jnp.where(bmask, 1.0 / HEAD_DIM, 0.0).astype(BF16)
        lb, ng = lb_ref[...], ng_ref[...]

        def chunk(c, carry):
            rows = pl.ds(pl.multiple_of(c * HGRN_CHUNK, HGRN_CHUNK), HGRN_CHUNK)
            vv = i_ref[rows, :]
            gv = g_ref[rows, :]
            _, qq, _, _, kk, b_last, eq, ek, eb, el = _hgrn_chunk(q_ref[rows, :], f_ref[rows, :], lb, tril)
            vb = vv.astype(BF16)
            qs = _stack_heads(qq * eq, hm).astype(BF16)
            att = jnp.where(causal, _dot_nt(qs, (kk * ek).astype(BF16)), 0.0)
            o = _unstack_heads(_dot(att.astype(BF16), vb), hm)
            s0 = s_ref[...]
            ss_ref[c] = s0
            o = o + _dot_nt((qq * eb).astype(BF16), s0.astype(BF16))
            s_ref[...] = s0 * jnp.exp(b_last) + jnp.where(bmask, _dot_tn(vb, (kk * el).astype(BF16)), 0.0)
            o_ref[rows, :] = o
            rstd = lax.rsqrt(_seg_mean(o * o, avg) + EPS)
            y_ref[rows, :] = o * rstd * ng * (gv * _sigmoid(gv))
            return carry

        lax.fori_loop(0, nch, chunk, 0)

    vec = _full((1, W_GRP))
    return pl.pallas_call(
        body, grid=(t // tm,),
        in_specs=[_zblk(tm, 4), _zblk(tm, 5), _zblk(tm, 6), _zblk(tm, 7), vec, vec, pl.BlockSpec(memory_space=pl.ANY)],
        out_specs=[_zblk(tm, 2), pl.BlockSpec((tm, W_GRP), lambda i: (i, 0)),
                   pl.BlockSpec((nch, W_GRP, W_GRP), lambda i: (i, 0, 0))],
        out_shape=[_sds(mix.shape, F32), _sds((t, W_GRP), F32), _sds((t // HGRN_CHUNK, W_GRP, W_GRP), F32)],
        scratch_shapes=[pltpu.VMEM((W_GRP, W_GRP), F32)],
        input_output_aliases={6: 0}, name="mix_c_fwd", compiler_params=_cp("arbitrary"),
    )(z, z, z, z, lb, ng, mix)


def _mix_c_bwd(z, dz, dmix, o_pre, states, lb, ng):
    t = z.shape[0]
    tm = _mix_tm(t)
    nt = t // tm
    nch = tm // HGRN_CHUNK

    def body(q_ref, f_ref, i_ref, g_ref, o_ref, ss_ref, dy_ref, lb_ref, ng_ref, dz_in, dz_ref, dlb_ref, dng_ref, ds_ref):
        @pl.when(pl.program_id(0) == 0)
        def _():
            ds_ref[...] = jnp.zeros_like(ds_ref)
            dlb_ref[...] = jnp.zeros_like(dlb_ref)
            dng_ref[...] = jnp.zeros_like(dng_ref)

        hm = _head_masks()
        bmask = _block_mask()
        causal = _causal_stack()
        tril = _tri(HGRN_CHUNK, True)
        triu = _tri(HGRN_CHUNK, False)
        avg = jnp.where(bmask, 1.0 / HEAD_DIM, 0.0).astype(BF16)
        lb, ng = lb_ref[...], ng_ref[...]
        last_row = lax.broadcasted_iota(jnp.int32, (HGRN_CHUNK, W_GRP), 0) == HGRN_CHUNK - 1

        def chunk(j, carry):
            c = nch - 1 - j
            rows = pl.ds(pl.multiple_of(c * HGRN_CHUNK, HGRN_CHUNK), HGRN_CHUNK)
            qv, gv, vv = q_ref[rows, :], g_ref[rows, :], i_ref[rows, :]
            sq, qq, sg, fg, kk, b_last, eq, ek, eb, el = _hgrn_chunk(qv, f_ref[rows, :], lb, tril)
            s0 = ss_ref[c]
            ds1 = ds_ref[...]
            o = o_ref[rows, :]
            dy = dy_ref[rows, :]
            rstd = lax.rsqrt(_seg_mean(o * o, avg) + EPS)
            oh = o * rstd
            sgg = _sigmoid(gv)
            dz_ref[rows, 3 * W_GRP:4 * W_GRP] = dy * oh * ng * (sgg * (1.0 + gv * (1.0 - sgg)))
            don = dy * gv * sgg
            dng_ref[...] += jnp.sum(don * oh, axis=0, keepdims=True)
            doh = don * ng
            do = rstd * (doh - oh * _seg_mean(doh * oh, avg))
            qt, kt, qh, kh = qq * eq, kk * ek, qq * eb, kk * el
            vb, dob = vv.astype(BF16), do.astype(BF16)
            ktb, khb = kt.astype(BF16), kh.astype(BF16)
            ds1b = ds1.astype(BF16)
            qs = _stack_heads(qt, hm).astype(BF16)
            dos = _stack_heads(do, hm).astype(BF16)
            att = jnp.where(causal, _dot_nt(qs, ktb), 0.0).astype(BF16)
            datt = jnp.where(causal, _dot_nt(dos, vb), 0.0).astype(BF16)
            dv = _dot_tn(att, dos) + _dot_nt(khb, ds1b)
            dqt = _unstack_heads(_dot(datt, ktb), hm)
            dkt = _dot_tn(datt, qs)
            dqh = _dot(dob, s0.astype(BF16))
            dkh = _dot(vb, ds1b)
            e_last = jnp.exp(b_last)
            ds_ref[...] = ds1 * e_last + jnp.where(bmask, _dot_tn(dob, qh.astype(BF16)), 0.0)
            dq = dqt * eq + dqh * eb
            dk = dkt * ek + dkh * el
            db = qt * dqt - kt * dkt + qh * dqh - kh * dkh
            db_last = jnp.sum(kh * dkh, axis=0, keepdims=True) + e_last * jnp.sum(ds1 * s0, axis=0, keepdims=True)
            db = db + jnp.where(last_row, db_last, 0.0)
            dlogf = _dot_f32_lhs_exact(triu, db)
            dfg = dlogf / fg - dk
            dz_ref[rows, W_GRP:2 * W_GRP] = dfg * (1.0 - lb) * sg * (1.0 - sg)
            dlb_ref[...] += jnp.sum(dfg * (1.0 - sg), axis=0, keepdims=True)
            dz_ref[rows, 0:W_GRP] = dq * (sq * (1.0 + qv * (1.0 - sq)))
            dz_ref[rows, 2 * W_GRP:3 * W_GRP] = dv
            return carry

        lax.fori_loop(0, nch, chunk, 0)

    vec = _full((1, W_GRP))
    rev = lambda col: _zblk(tm, col, nt)
    return pl.pallas_call(
        body, grid=(nt,),
        in_specs=[rev(4), rev(5), rev(6), rev(7), rev(0), pl.BlockSpec((nch, W_GRP, W_GRP), lambda i: (nt - 1 - i, 0, 0)),
                  rev(2), vec, vec, pl.BlockSpec(memory_space=pl.ANY)],
        out_specs=[pl.BlockSpec((tm, 4 * W_GRP), lambda i: (nt - 1 - i, 1)), vec, vec],
        out_shape=[_sds(dz.shape, F32), _sds((1, W_GRP), F32), _sds((1, W_GRP), F32)],
        scratch_shapes=[pltpu.VMEM((W_GRP, W_GRP), F32)],
        input_output_aliases={9: 0}, name="mix_c_bwd", compiler_params=_cp("arbitrary"),
    )(z, z, z, z, o_pre, states, dmix, lb, ng, dz)


def _pool_select(hm, s2, s4, s8, s16):
    return jnp.where(hm[0], s2, jnp.where(hm[1], s4, jnp.where(hm[2], s8, s16)))


def _pool_counts(hm, row0, tm):
    pos = (row0 + 1 + lax.broadcasted_iota(jnp.int32, (tm, W_GRP), 0)).astype(F32)
    win = _pool_select(hm, 2.0, 4.0, 8.0, 16.0)
    return jnp.minimum(pos, win)


def _pooled(ext_ref, x, halo, hm, cnt):
    ext_ref[0:POOL_HALO, :] = halo
    ext_ref[POOL_HALO:, :] = x
    e = ext_ref[...]
    s2 = e + pltpu.roll(e, 1, 0)
    s4 = s2 + pltpu.roll(s2, 2, 0)
    s8 = s4 + pltpu.roll(s4, 4, 0)
    s16 = s8 + pltpu.roll(s8, 8, 0)
    return _pool_select(hm, s2, s4, s8, s16)[POOL_HALO:] / cnt - x


def _mix_d_fwd(z, mix, wd, scale):
    t = z.shape[0]
    tm = _mix_tm(t)

    def body(x_ref, wd_ref, sc_ref, mix_in, o_ref, ext_ref, halo_ref):
        i = pl.program_id(0)

        @pl.when(i == 0)
        def _():
            halo_ref[...] = jnp.zeros_like(halo_ref)

        hm = _head_masks()
        x = x_ref[...]
        pooled = _pooled(ext_ref, x, halo_ref[...], hm, _pool_counts(hm, i * tm, tm))
        halo_ref[...] = x_ref[tm - POOL_HALO:tm, :]
        o_ref[...] = _dot(pooled.astype(BF16), wd_ref[...]) * sc_ref[...]

    return pl.pallas_call(
        body, grid=(t // tm,),
        in_specs=[_zblk(tm, 8), _full((W_GRP, W_GRP)), _full((1, W_GRP)), pl.BlockSpec(memory_space=pl.ANY)],
        out_specs=_zblk(tm, 3), out_shape=_sds(mix.shape, F32),
        scratch_shapes=[pltpu.VMEM((tm + POOL_HALO, W_GRP), F32), pltpu.VMEM((POOL_HALO, W_GRP), F32)],
        input_output_aliases={3: 0}, name="mix_d_fwd", compiler_params=_cp("arbitrary"),
    )(z, wd, scale, mix)


def _mix_d_bwd(z, dz, dmix, wd, wd_t, scale):
    t = z.shape[0]
    tm = _mix_tm(t)
    nt = t // tm
    hb = tm // POOL_HALO

    def body(x_ref, xhalo_ref, dy_ref, wd_ref, wdt_ref, sc_ref, dz_in, dz_ref, dwd_ref, dsc_ref, ext_ref, en_ref):
        i = pl.program_id(0)
        ri = nt - 1 - i

        @pl.when(i == 0)
        def _():
            en_ref[...] = jnp.zeros_like(en_ref)
            dwd_ref[...] = jnp.zeros_like(dwd_ref)
            dsc_ref[...] = jnp.zeros_like(dsc_ref)

        hm = _head_masks()
        cnt = _pool_counts(hm, ri * tm, tm)
        x = x_ref[...]
        pooled = _pooled(ext_ref, x, jnp.where(ri == 0, 0.0, xhalo_ref[...]), hm, cnt)
        pb = pooled.astype(BF16)
        dy = dy_ref[...]
        dsc_ref[...] += jnp.sum(dy * _dot(pb, wd_ref[...]), axis=0, keepdims=True)
        dyw = (dy * sc_ref[...]).astype(BF16)
        dwd_ref[...] += _dot_tn(pb, dyw)
        dpool = _dot(dyw, wdt_ref[...])
        e = dpool / cnt
        ext_ref[0:tm, :] = e
        ext_ref[tm:, :] = en_ref[...]
        en_ref[...] = e[0:POOL_HALO, :]
        ee = ext_ref[...]
        n = tm + POOL_HALO
        r2 = ee + pltpu.roll(ee, n - 1, 0)
        r4 = r2 + pltpu.roll(r2, n - 2, 0)
        r8 = r4 + pltpu.roll(r4, n - 4, 0)
        r16 = r8 + pltpu.roll(r8, n - 8, 0)
        dz_ref[...] = _pool_select(hm, r2, r4, r8, r16)[:tm] - dpool

    sq = _full((W_GRP, W_GRP))
    vec = _full((1, W_GRP))
    return pl.pallas_call(
        body, grid=(nt,),
        in_specs=[_zblk(tm, 8, nt), pl.BlockSpec((POOL_HALO, W_GRP), lambda i: (jnp.maximum((nt - 1 - i) * hb - 1, 0), 8)),
                  _zblk(tm, 3, nt), sq, sq, vec, pl.BlockSpec(memory_space=pl.ANY)],
        out_specs=[_zblk(tm, 8, nt), sq, vec],
        out_shape=[_sds(dz.shape, F32), _sds((W_GRP, W_GRP), F32), _sds((1, W_GRP), F32)],
        scratch_shapes=[pltpu.VMEM((tm + POOL_HALO, W_GRP), F32), pltpu.VMEM((POOL_HALO, W_GRP), F32)],
        input_output_aliases={6: 0}, name="mix_d_bwd", compiler_params=_cp("arbitrary"),
    )(z, z, dmix, wd, wd_t, scale, dz)


def _as2d(a):
    if a.ndim == 1:
        return a.reshape(1, a.shape[0])
    return a.reshape(-1, a.shape[-1])


def _adamw(w, g, m, v, name):
    shape = w.shape
    w2, g2, m2, v2 = _as2d(w), _as2d(g), _as2d(m), _as2d(v)
    rows, cols = w2.shape
    tr = _pick(rows, (1024, 512, 256, 128, 64, 32, 16, 8))
    if tr * cols * 4 * 14 > VMEM_LIMIT_BYTES:
        tr = _pick(rows, (256, 128, 64, 32, 16, 8))

    def body(w_ref, g_ref, m_ref, v_ref, d_ref, nm_ref, nv_ref):
        gv = g_ref[...]
        mn = ADAM_B1 * m_ref[...] + (1.0 - ADAM_B1) * gv
        vn = ADAM_B2 * v_ref[...] + (1.0 - ADAM_B2) * (gv * gv)
        m_hat = mn / (1.0 - ADAM_B1 ** ADAM_STEP)
        v_hat = vn / (1.0 - ADAM_B2 ** ADAM_STEP)
        d_ref[...] = -ADAM_LR * (m_hat / (jnp.sqrt(v_hat) + ADAM_EPS) + ADAM_WD * w_ref[...])
        nm_ref[...] = mn
        nv_ref[...] = vn

    blk = pl.BlockSpec((tr, cols), lambda i: (i, 0))
    outs = pl.pallas_call(
        body, grid=(rows // tr,), in_specs=[blk] * 4, out_specs=[blk] * 3, out_shape=[_sds((rows, cols), F32)] * 3,
        name=name, compiler_params=_cp("parallel"),
    )(w2, g2, m2, v2)
    return tuple(o.reshape(shape) for o in outs)


def _adamw_layer(w, g, m, v, layer, bufs, name):
    nl, r, cs = w.shape
    tr = _pick(r, (256, 128, 64, 32, 16, 8))

    def body(w_ref, g_ref, m_ref, v_ref, *rest):
        go_ref, d_ref, nm_ref, nv_ref = rest[-4:]
        gv = g_ref[...]
        mn = ADAM_B1 * m_ref[...] + (1.0 - ADAM_B1) * gv
        vn = ADAM_B2 * v_ref[...] + (1.0 - ADAM_B2) * (gv * gv)
        m_hat = mn / (1.0 - ADAM_B1 ** ADAM_STEP)
        v_hat = vn / (1.0 - ADAM_B2 ** ADAM_STEP)
        go_ref[...] = gv
        d_ref[...] = -ADAM_LR * (m_hat / (jnp.sqrt(v_hat) + ADAM_EPS) + ADAM_WD * w_ref[...])
        nm_ref[...] = mn
        nv_ref[...] = vn

    lay = pl.BlockSpec((None, tr, cs), lambda i: (layer, i, 0))
    in_specs = [lay, pl.BlockSpec((tr, cs), lambda i: (i, 0)), lay, lay]
    args = [w, g, m, v]
    aliases = {}
    if bufs is not None:
        in_specs += [pl.BlockSpec(memory_space=pl.ANY)] * 4
        args += list(bufs)
        aliases = {4 + i: i for i in range(4)}
    return pl.pallas_call(
        body, grid=(r // tr,), in_specs=in_specs, out_specs=[lay] * 4, out_shape=[_sds((nl, r, cs), F32)] * 4,
        input_output_aliases=aliases, name=name, compiler_params=_cp("parallel"),
    )(*args)


def _slot_sum(own, slots, name):
    n_slots, rows, cols = slots.shape
    tr = _pick(rows, (512, 352, 256, 128, 64, 32, 16, 8))

    def body(*refs):
        s_ref, o_ref = refs[-2], refs[-1]
        acc = s_ref[0].astype(F32) if own is None else refs[0][...].astype(F32) + s_ref[0].astype(F32)
        for k in range(1, n_slots):
            acc = acc + s_ref[k].astype(F32)
        o_ref[...] = acc

    row = pl.BlockSpec((tr, cols), lambda i: (i, 0))
    return pl.pallas_call(
        body, grid=(rows // tr,),
        in_specs=([] if own is None else [row]) + [pl.BlockSpec((n_slots, tr, cols), lambda i: (0, i, 0))],
        out_specs=row, out_shape=_sds((rows, cols), F32), name=name, compiler_params=_cp("parallel"),
    )(*(() if own is None else (own,)), slots)


def _me():
    return lax.axis_index("x"), lax.axis_index("y"), lax.axis_index("c")


def _other_chips(x, y):
    return [(1 - x, y), (x, 1 - y), (1 - x, 1 - y)]


ANY_SPEC = pl.BlockSpec(memory_space=pl.ANY)
HBM_SPEC = pl.BlockSpec(memory_space=pltpu.HBM)
SEM_SPEC = pl.BlockSpec(memory_space=pltpu.SEMAPHORE)
SPLIT_COPY_PARAMS = pltpu.CompilerParams(has_side_effects=pltpu.SideEffectType.DATAFLOW_SIDE_EFFECTING)
N_CHIPS = 4


def _aligned(v, m):
    return v if isinstance(v, int) else pl.multiple_of(v, m)


def _in_hbm(arr):
    return pltpu.with_memory_space_constraint(arr, pltpu.HBM)


def _peer(x, y, c, k):
    fx, fy, fc = (k >> 2) & 1, (k >> 1) & 1, k & 1
    px = 1 - x if fx else x
    py = 1 - y if fy else y
    pc = 1 - c if fc else c
    return px, py, pc


def _gather_start(shards, after, name):
    n = len(shards)

    def body(*refs):
        src, land = refs[:n], refs[n:2 * n]
        send_sems, recv_sems = refs[2 * n + 1], refs[2 * n + 2]
        token = refs[-1]
        x, y, c = _me()
        for w in range(n):
            for chip in _other_chips(x, y):
                pltpu.make_async_remote_copy(
                    src_ref=src[w], dst_ref=land[w].at[2 * x + y], send_sem=send_sems.at[w], recv_sem=recv_sems.at[w],
                    device_id=(*chip, c), device_id_type=MESH_ID).start()
        token[...] = jnp.zeros_like(token)

    lands = [lax.empty((N_CHIPS,) + s.shape, s.dtype) for s in shards]
    thru = [pltpu.HBM(s.shape, s.dtype) for s in shards] + [pltpu.HBM(z.shape, z.dtype) for z in lands]
    outs = pl.pallas_call(
        body, name=name,
        out_shape=(pltpu.SemaphoreType.DMA((n,)), pltpu.SemaphoreType.DMA((n,)), *thru, _sds((8, 128), F32)),
        in_specs=[HBM_SPEC] * (2 * n) + [ANY_SPEC],
        out_specs=(SEM_SPEC, SEM_SPEC, *[HBM_SPEC] * (2 * n), pl.BlockSpec(memory_space=pltpu.VMEM)),
        input_output_aliases={i: 2 + i for i in range(2 * n)}, compiler_params=SPLIT_COPY_PARAMS,
    )(*[_in_hbm(s) for s in shards], *[_in_hbm(z) for z in lands], after)
    return outs[0], outs[1], outs[2:2 + n], outs[2 + n:2 + 2 * n]


def _gather_wait(send_sems, recv_sems, srcs, lands, after, name):
    n = len(srcs)

    def body(*refs):
        land = refs[n:2 * n]
        send_sems, recv_sems = refs[2 * n], refs[2 * n + 1]
        x, y, c = _me()
        for w in range(n):
            three = land[w].at[pl.ds(0, N_CHIPS - 1)]
            cp = pltpu.make_async_remote_copy(src_ref=three, dst_ref=three, send_sem=send_sems.at[w], recv_sem=recv_sems.at[w],
                                              device_id=(x, y, c), device_id_type=MESH_ID)
            cp.wait_send()
            cp.wait_recv()

    both = list(srcs) + list(lands)
    outs = pl.pallas_call(
        body, name=name, out_shape=tuple(pltpu.HBM(b.shape, b.dtype) for b in both),
        in_specs=[HBM_SPEC] * (2 * n) + [SEM_SPEC, SEM_SPEC, ANY_SPEC], out_specs=[HBM_SPEC] * (2 * n),
        input_output_aliases={i: i for i in range(2 * n)}, compiler_params=SPLIT_COPY_PARAMS,
    )(*both, send_sems, recv_sems, after)
    return outs[n:2 * n]


def _push_start(grads, small, name):
    n = len(grads)
    srcs = list(grads) + [small]

    def body(*refs):
        src, slots = refs[:n + 1], refs[n + 1:2 * n + 2]
        send_sems, recv_sems = refs[2 * n + 2], refs[2 * n + 3]
        token = refs[-1]
        x, y, c = _me()
        for w in range(n + 1):
            for k in range(1, N_DEV):
                px, py, pc = _peer(x, y, c, k)
                if w < n:
                    hr = src[w].shape[1] // 2
                    piece = src[w].at[2 * px + py, pl.ds(_aligned(pc * hr, 16), hr), :]
                    slot = slots[w].at[k - 1]
                else:
                    piece = src[w]
                    slot = slots[w].at[4 * x + 2 * y + c]
                pltpu.make_async_remote_copy(
                    src_ref=piece, dst_ref=slot, send_sem=send_sems.at[w], recv_sem=recv_sems.at[w],
                    device_id=(px, py, pc), device_id_type=MESH_ID).start()
        token[...] = jnp.zeros_like(token)

    slots = [lax.empty((N_DEV - 1, g.shape[1] // 2, g.shape[2]), g.dtype) for g in grads]
    slots.append(lax.empty((N_DEV,) + small.shape, small.dtype))
    both = srcs + slots
    outs = pl.pallas_call(
        body, name=name,
        out_shape=(pltpu.SemaphoreType.DMA((n + 1,)), pltpu.SemaphoreType.DMA((n + 1,)),
                   *[pltpu.HBM(b.shape, b.dtype) for b in both], _sds((8, 128), F32)),
        in_specs=[HBM_SPEC] * len(both),
        out_specs=(SEM_SPEC, SEM_SPEC, *[HBM_SPEC] * len(both), pl.BlockSpec(memory_space=pltpu.VMEM)),
        input_output_aliases={i: 2 + i for i in range(len(both))}, compiler_params=SPLIT_COPY_PARAMS,
    )(*[_in_hbm(b) for b in both])
    return outs[0], outs[1], outs[2:3 + n], outs[3 + n:4 + 2 * n]


def _push_wait(send_sems, recv_sems, srcs, slots, after, name):
    n = len(srcs)

    def body(*refs):
        slot = refs[n:2 * n]
        send_sems, recv_sems = refs[2 * n], refs[2 * n + 1]
        x, y, c = _me()
        for w in range(n):
            seven = slot[w].at[pl.ds(0, N_DEV - 1)]
            cp = pltpu.make_async_remote_copy(src_ref=seven, dst_ref=seven, send_sem=send_sems.at[w],
                                              recv_sem=recv_sems.at[w], device_id=(x, y, c), device_id_type=MESH_ID)
            cp.wait_send()
            cp.wait_recv()

    both = list(srcs) + list(slots)
    outs = pl.pallas_call(
        body, name=name, out_shape=tuple(pltpu.HBM(b.shape, b.dtype) for b in both),
        in_specs=[HBM_SPEC] * (2 * n) + [SEM_SPEC, SEM_SPEC, ANY_SPEC], out_specs=[HBM_SPEC] * (2 * n),
        input_output_aliases={i: i for i in range(2 * n)}, compiler_params=SPLIT_COPY_PARAMS,
    )(*both, send_sems, recv_sems, after)
    return outs[:n], outs[n:]


SWAP_CHUNK_BYTES = 2 * 1024 * 1024


def _swap_chunk_rows(hr, cs):
    ch = hr
    while ch * cs * 4 > SWAP_CHUNK_BYTES and ch % 16 == 0:
        ch //= 2
    return ch


def _swap_halves(halves, name):
    n = len(halves)
    chunk = [_swap_chunk_rows(*h.shape) for h in halves]
    rounds = max(h.shape[0] // ch for h, ch in zip(halves, chunk))

    def body(*refs):
        src, dst, buf = refs[:n], refs[n:2 * n], refs[2 * n:3 * n]
        load_sems, put_sems, send_sems, recv_sems = refs[3 * n:]
        x, y, c = _me()
        sibling = (x, y, 1 - c)
        for j in range(rounds):
            live = [w for w in range(n) if j < src[w].shape[0] // chunk[w]]
            loads = [pltpu.make_async_copy(src[w].at[pl.ds(j * chunk[w], chunk[w])], buf[w], load_sems.at[w]) for w in live]
            for ld in loads:
                ld.start()
            moves = []
            for ld, w in zip(loads, live):
                ld.wait()
                rows = pl.ds(_aligned(c * src[w].shape[0] + j * chunk[w], 8), chunk[w])
                put = pltpu.make_async_copy(buf[w], dst[w].at[rows], put_sems.at[w])
                send = pltpu.make_async_remote_copy(src_ref=buf[w], dst_ref=dst[w].at[rows], send_sem=send_sems.at[w],
                                                    recv_sem=recv_sems.at[w], device_id=sibling, device_id_type=MESH_ID)
                put.start()
                send.start()
                moves.append((put, send))
            for put, send in moves:
                put.wait()
                send.wait_send()
        for w in range(n):
            hr = src[w].shape[0]
            got = dst[w].at[pl.ds(_aligned((1 - c) * hr, 8), hr)]
            pltpu.make_async_remote_copy(src_ref=got, dst_ref=got, send_sem=send_sems.at[w], recv_sem=recv_sems.at[w],
                                         device_id=sibling, device_id_type=MESH_ID).wait_recv()

    return pl.pallas_call(
        body, in_specs=[ANY_SPEC] * n, out_specs=[ANY_SPEC] * n,
        out_shape=[_sds((2 * h.shape[0], h.shape[1]), F32) for h in halves],
        scratch_shapes=[pltpu.VMEM((ch, h.shape[1]), F32) for h, ch in zip(halves, chunk)]
        + [pltpu.SemaphoreType.DMA((n,))] * 4,
        name=name,
    )(*halves)


BIG = ("w_in", "w_out", "w_up", "w_down", "w_pe", "w_pg")
ROW_SHARDED = ("w_out", "w_down", "w_pg")
SMALL = ("norm1_g", "a_ln_g", "a_ln_b", "a_ws", "a_bs", "b_conv_w", "b_conv_b", "b_wa", "b_ba", "b_wx", "b_bx", "b_lam",
         "c_lb", "c_norm_g", "d_w", "d_scale", "norm2_g", "ffn_conv_w", "ffn_conv_b", "norm3_g", "final_g")
SMALL_SHARDED = ("b_conv_w", "ffn_conv_w")
WEIGHTS = ("norm1_g", "w_in", "a_ln_g", "a_ln_b", "a_ws", "a_bs", "b_conv_w", "b_conv_b", "b_wa", "b_ba", "b_wx", "b_bx",
           "b_lam", "c_lb", "c_norm_g", "d_w", "d_scale", "w_out", "norm2_g", "w_up", "ffn_conv_w", "ffn_conv_b", "w_down",
           "norm3_g", "w_pe", "w_pg", "final_g")
ARGS = ("x", "p") + WEIGHTS + ("loss_target",) + tuple("m_" + n for n in WEIGHTS) + tuple("v_" + n for n in WEIGHTS)


def _block_diag(w):
    eye = jnp.eye(N_HEADS, dtype=w.dtype)
    return (eye[:, None, :, None] * w[:, :, None, :]).reshape(W_GRP, W_GRP)


def _diag_blocks(m):
    m4 = m.reshape(N_HEADS, HEAD_DIM, N_HEADS, HEAD_DIM)
    return jnp.stack([m4[h, :, h, :] for h in range(N_HEADS)])


def _lower_bounds(c_lb):
    lbs = jnp.cumsum(jax.nn.softmax(c_lb, axis=0), axis=0)
    return lbs - lbs[0:1]


def kernel(x, p, norm1_g, w_in, a_ln_g, a_ln_b, a_ws, a_bs, b_conv_w, b_conv_b, b_wa, b_ba, b_wx, b_bx, b_lam, c_lb, c_norm_g, d_w, d_scale, w_out, norm2_g, w_up, ffn_conv_w, ffn_conv_b, w_down, norm3_g, w_pe, w_pg, final_g, loss_target, m_norm1_g, m_w_in, m_a_ln_g, m_a_ln_b, m_a_ws, m_a_bs, m_b_conv_w, m_b_conv_b, m_b_wa, m_b_ba, m_b_wx, m_b_bx, m_b_lam, m_c_lb, m_c_norm_g, m_d_w, m_d_scale, m_w_out, m_norm2_g, m_w_up, m_ffn_conv_w, m_ffn_conv_b, m_w_down, m_norm3_g, m_w_pe, m_w_pg, m_final_g, v_norm1_g, v_w_in, v_a_ln_g, v_a_ln_b, v_a_ws, v_a_bs, v_b_conv_w, v_b_conv_b, v_b_wa, v_b_ba, v_b_wx, v_b_bx, v_b_lam, v_c_lb, v_c_norm_g, v_d_w, v_d_scale, v_w_out, v_norm2_g, v_w_up, v_ffn_conv_w, v_ffn_conv_b, v_w_down, v_norm3_g, v_w_pe, v_w_pg, v_final_g):
    return _step((x, p, norm1_g, w_in, a_ln_g, a_ln_b, a_ws, a_bs, b_conv_w, b_conv_b, b_wa, b_ba, b_wx, b_bx, b_lam, c_lb, c_norm_g, d_w, d_scale, w_out, norm2_g, w_up, ffn_conv_w, ffn_conv_b, w_down, norm3_g, w_pe, w_pg, final_g, loss_target, m_norm1_g, m_w_in, m_a_ln_g, m_a_ln_b, m_a_ws, m_a_bs, m_b_conv_w, m_b_conv_b, m_b_wa, m_b_ba, m_b_wx, m_b_bx, m_b_lam, m_c_lb, m_c_norm_g, m_d_w, m_d_scale, m_w_out, m_norm2_g, m_w_up, m_ffn_conv_w, m_ffn_conv_b, m_w_down, m_norm3_g, m_w_pe, m_w_pg, m_final_g, v_norm1_g, v_w_in, v_a_ln_g, v_a_ln_b, v_a_ws, v_a_bs, v_b_conv_w, v_b_conv_b, v_b_wa, v_b_ba, v_b_wx, v_b_bx, v_b_lam, v_c_lb, v_c_norm_g, v_d_w, v_d_scale, v_w_out, v_norm2_g, v_w_up, v_ffn_conv_w, v_ffn_conv_b, v_w_down, v_norm3_g, v_w_pe, v_w_pg, v_final_g))


SMALL_PER_LAYER = tuple(n for n in SMALL if n != "final_g")
GATHERED = BIG + SMALL_SHARDED


def _cols_to_slabs(m):
    r, c4 = m.shape
    return jnp.moveaxis(m.reshape(r, N_CHIPS, c4 // N_CHIPS), 1, 0)


def _slabs_to_cols(s):
    return jnp.moveaxis(s, 0, 1).reshape(s.shape[1], -1)


def _pack_small(parts):
    flat = jnp.concatenate([p.reshape(-1) for p in parts])
    return jnp.pad(flat, (0, (-flat.shape[0]) % 1024)).reshape(-1, 128)


def _step(args):
    a = dict(zip(ARGS, args, strict=True))
    x0 = a["x"][0]
    target = a["loss_target"][0]
    nl = a["norm1_g"].shape[0]
    t, d = x0.shape
    f = a["w_down"].shape[1] * N_CHIPS
    cx, cy, cc = _me()
    my_shard = 2 * cx + cy
    shards = {n: a[n].astype(BF16) for n in BIG}
    shards.update({n: a[n] for n in SMALL_SHARDED})

    def start_gather(l, after):
        return _gather_start([shards[n][l] for n in GATHERED], after, f"gather_start_{l}")

    def finish_gather(l, handle, after):
        send, recv, srcs, lands = handle
        lands = _gather_wait(send, recv, srcs, lands, after, f"gather_wait_{l}")
        full = {n: lax.dynamic_update_slice(land, shards[n][l][None], (my_shard, 0, 0)) for n, land in zip(GATHERED, lands)}
        w = {n: full[n].reshape(-1, full[n].shape[-1]) for n in ROW_SHARDED}
        w.update(w_in=_slabs_to_cols(full["w_in"]), w_up=full["w_up"], w_pe=full["w_pe"],
                 b_conv_w=_slabs_to_cols(full["b_conv_w"]), ffn_conv_w=_slabs_to_cols(full["ffn_conv_w"]))
        return w

    lbs, lbs_vjp = jax.vjp(_lower_bounds, a["c_lb"])
    tril = jnp.tril(jnp.ones((GMLP_CHUNK, GMLP_CHUNK), F32))

    def layer_params(l, w):
        q = {}
        q["wm"] = (a["a_ws"][l] * tril).astype(BF16)
        q["wm_t"] = jnp.swapaxes(q["wm"], 1, 2)
        q["bs_t"] = jnp.repeat(a["a_bs"][l].T, HEAD_DIM, axis=1)
        for nm in ("b_wa", "b_wx", "d_w"):
            bd = _block_diag(a[nm][l]).astype(BF16)
            q[nm], q[nm + "_t"] = bd, bd.T
        for nm in ("a_ln_g", "a_ln_b", "b_conv_b", "b_ba", "b_bx", "b_lam", "d_scale"):
            q[nm] = a[nm][l].reshape(1, W_GRP)
        q["lb"] = lbs[l].reshape(1, W_GRP)
        q["ng"] = jnp.tile(a["c_norm_g"][l], N_HEADS).reshape(1, W_GRP)
        q["b_conv_w"] = w["b_conv_w"]
        q["ffn_conv_w"] = w["ffn_conv_w"]
        q["ffn_conv_b"] = a["ffn_conv_b"][l].reshape(1, 2 * f)
        return q

    saved, weights, params = [], [], []
    handle = start_gather(0, x0)
    xl = x0
    for l in range(nl):
        w = finish_gather(l, handle, xl)
        q = layer_params(l, w)
        s = {"x0": xl}
        s["h1"] = _rms_fwd(xl, a["norm1_g"][l], "rms1_fwd")
        if l + 1 < nl:
            handle = start_gather(l + 1, s["h1"])
        s["z"] = _mm(s["h1"], w["w_in"], "nn", out_dtype=F32, name="mm_z")
        mix = _mix_a_fwd(s["z"], d, q["a_ln_g"], q["a_ln_b"], q["wm"], q["bs_t"])
        mix, s["hs"] = _mix_b_fwd(s["z"], mix, q["b_conv_w"], q["b_conv_b"], q["b_wa"], q["b_wx"], q["b_ba"], q["b_bx"],
                                  q["b_lam"])
        mix, s["o_pre"], s["states"] = _mix_c_fwd(s["z"], mix, q["lb"], q["ng"])
        s["mix"] = _mix_d_fwd(s["z"], mix, q["d_w"], q["d_scale"])
        s["x1"] = _mm(s["mix"], w["w_out"], "nn", res=xl, out_dtype=F32, name="mm_out")
        s["h2"] = _rms_fwd(s["x1"], a["norm2_g"][l], "rms2_fwd")
        s["hf_g"] = _mm(s["h2"], w["w_up"], "nn", b_slabs=True, n=f, out_dtype=F32, name="mm_up_g")
        s["hf_v"] = _mm(s["h2"], w["w_up"], "nn", b_slabs=True, n=f, b_noff=f, out_dtype=F32, name="mm_up_v")
        s["act"] = _ffn_act_fwd(s["hf_g"], s["hf_v"], q["ffn_conv_w"], q["ffn_conv_b"])
        s["x2"] = _mm(s["act"], w["w_down"], "nn", res=s["x1"], out_dtype=F32, name="mm_down")
        s["h3"] = _rms_fwd(s["x2"], a["norm3_g"][l], "rms3_fwd")
        s["pre"] = _mm(s["h3"], w["w_pg"], "nn", out_dtype=F32, name="mm_pg")
        s["pe"] = _mm(a["p"][l, 0], w["w_pe"], "nn", b_slabs=True, out_dtype=F32, name="mm_pe")
        xl = _ple_fwd(s["x2"], s["pe"], s["pre"])
        saved.append(s)
        weights.append(w)
        params.append(q)

    dx, g_final, loss = _final_loss(xl, a["final_g"], target)
    loss = lax.psum(loss[0, 0], ("x", "y", "c"))

    stacked = {n: None for n in BIG}
    small_sums = [None] * nl

    def finish_push(l, handle, after):
        send, recv, srcs, slots = handle
        srcs, slots = _push_wait(send, recv, srcs, slots, after, f"push_wait_{l}")
        halves = []
        for n, g, sl in zip(BIG, srcs[:-1], slots[:-1]):
            hr = g.shape[1] // 2
            own = lax.dynamic_slice(g, (my_shard, cc * hr, 0), (1, hr, g.shape[2]))[0]
            halves.append(_slot_sum(own, sl, "sum_" + n))
        for n, g in zip(BIG, _swap_halves(halves, "swap_halves")):
            stacked[n] = _adamw_layer(a[n], g, a["m_" + n], a["v_" + n], l, stacked[n], "adamw_" + n)
        by_sender = lax.dynamic_update_slice(slots[-1], srcs[-1][None], (2 * my_shard + cc, 0, 0))
        small_sums[l] = _slot_sum(None, by_sender, "sum_small")

    pending = None
    for l in reversed(range(nl)):
        q, s, w = params[l], saved[l], weights[l]
        gs = {}
        dpe, dpre = _ple_bwd(dx, s["pe"], s["pre"])
        g_pe = _mm(a["p"][l, 0], dpe, "tn", out_dtype=BF16, name="mm_dwpe", out_slabs=N_CHIPS)
        g_pg = _mm(s["h3"], dpre, "tn", out_dtype=BF16, name="mm_dwpg")
        dh3 = _mm(dpre, w["w_pg"], "nt", out_dtype=BF16, name="mm_dh3")
        dx2, gs["norm3_g"] = _rms_bwd(dh3, s["x2"], a["norm3_g"][l], dx, "rms3_bwd")
        g_down = _mm(s["act"], dx2, "tn", out_dtype=BF16, name="mm_dwdown")
        dact = _mm(dx2, w["w_down"], "nt", out_dtype=BF16, name="mm_dact")
        dhf_g, dhf_v, dcw_g, dcw_v, dcb_g, dcb_v = _ffn_act_bwd(dact, s["hf_g"], s["hf_v"], q["ffn_conv_w"], q["ffn_conv_b"])
        gs["ffn_conv_w"] = jnp.concatenate([dcw_g, dcw_v], axis=1)
        gs["ffn_conv_b"] = jnp.concatenate([dcb_g, dcb_v], axis=1)
        g_up = _mm(s["h2"], dhf_g, "tn", out_dtype=BF16, name="mm_dwup_g", out_slabs=N_CHIPS, out_n=2 * f)
        g_up = _mm(s["h2"], dhf_v, "tn", out_dtype=BF16, name="mm_dwup_v", out_slabs=N_CHIPS, out_n=2 * f, o_noff=f, out_buf=g_up)
        dh2 = _mm(dhf_g, w["w_up"], "nt", b_slabs=True, out_dtype=F32, name="mm_dh2_g")
        dh2 = _mm(dhf_v, w["w_up"], "nt", b_slabs=True, b_koff=f, res=dh2, out_dtype=F32, name="mm_dh2_v")
        dx1, gs["norm2_g"] = _rms_bwd(dh2, s["x1"], a["norm2_g"][l], dx2, "rms2_bwd")
        g_out = _mm(s["mix"], dx1, "tn", out_dtype=BF16, name="mm_dwout")
        dmix = _mm(dx1, w["w_out"], "nt", out_dtype=F32, name="mm_dmix")
        dz, gs["a_ln_g"], gs["a_ln_b"], dws, dbs_t = _mix_a_bwd(s["z"], dmix, q["a_ln_g"], q["a_ln_b"], q["wm"], q["wm_t"],
                                                               q["bs_t"])
        gs["a_ws"] = dws * tril
        gs["a_bs"] = dbs_t.reshape(GMLP_CHUNK, N_HEADS, HEAD_DIM).sum(-1).T
        dz, gs["b_conv_w"], gs["b_conv_b"], dwa, dwx, gs["b_ba"], gs["b_bx"], gs["b_lam"] = _mix_b_bwd(
            s["z"], dz, dmix, s["hs"], q["b_conv_w"], q["b_conv_b"], q["b_wa"], q["b_wx"], q["b_wa_t"], q["b_wx_t"],
            q["b_ba"], q["b_bx"], q["b_lam"])
        gs["b_wa"], gs["b_wx"] = _diag_blocks(dwa), _diag_blocks(dwx)
        dz, gs["c_lb"], dng = _mix_c_bwd(s["z"], dz, dmix, s["o_pre"], s["states"], q["lb"], q["ng"])
        gs["c_norm_g"] = dng.reshape(N_HEADS, HEAD_DIM).sum(0)
        dz, dwd, gs["d_scale"] = _mix_d_bwd(s["z"], dz, dmix, q["d_w"], q["d_w_t"], q["d_scale"])
        gs["d_w"] = _diag_blocks(dwd)
        g_in = _mm(s["h1"], dz, "tn", out_dtype=BF16, name="mm_dwin")
        dh1 = _mm(dz, w["w_in"], "nt", out_dtype=BF16, name="mm_dh1")
        dx, gs["norm1_g"] = _rms_bwd(dh1, s["x0"], a["norm1_g"][l], dx1, "rms1_bwd")

        slabs = {"w_in": _cols_to_slabs(g_in), "w_up": g_up, "w_pe": g_pe}
        for n, g in (("w_out", g_out), ("w_down", g_down), ("w_pg", g_pg)):
            slabs[n] = g.reshape(N_CHIPS, -1, g.shape[-1])
        small = [gs[n] for n in SMALL_PER_LAYER] + ([g_final] if l == nl - 1 else [])
        handle = _push_start([slabs[n] for n in BIG], _pack_small(small), f"push_start_{l}")
        if pending is not None:
            finish_push(*pending, dx)
        pending = (l, handle)
    finish_push(*pending, dx)
    grad_x = dx[None]

    def small_shape(n):
        return a[n].shape[1:-1] + (a[n].shape[-1] * N_CHIPS,) if n in SMALL_SHARDED else a[n].shape[1:]

    per_layer = {n: [] for n in SMALL_PER_LAYER}
    for l in range(nl):
        vec, off = small_sums[l].reshape(-1), 0
        for n in SMALL_PER_LAYER:
            shape = small_shape(n)
            size = 1
            for dim in shape:
                size *= dim
            per_layer[n].append(vec[off:off + size].reshape(shape))
            off += size
        if l == nl - 1:
            grad_final = vec[off:off + d]
    grads = {n: jnp.stack(per_layer[n]) for n in SMALL_PER_LAYER}
    grads["c_lb"] = lbs_vjp(grads["c_lb"])[0]
    grads["final_g"] = grad_final
    for n in SMALL_SHARDED:
        cs = a[n].shape[-1]
        grads[n] = lax.dynamic_slice_in_dim(grads[n], my_shard * cs, cs, axis=2)

    outs = {}
    for n in WEIGHTS:
        if n in BIG:
            outs[n] = stacked[n]
        else:
            outs[n] = (grads[n],) + _adamw(a[n], grads[n], a["m_" + n], a["v_" + n], "adamw_" + n)
    return (loss, grad_x, *[outs[n][0] for n in WEIGHTS], *[outs[n][1] for n in WEIGHTS], *[outs[n][2] for n in WEIGHTS],
            *[outs[n][3] for n in WEIGHTS])


def _step_v1(args):
    a = dict(zip(ARGS, args, strict=True))
    x0 = a["x"][0]
    target = a["loss_target"][0]
    nl = a["norm1_g"].shape[0]
    t, d = x0.shape
    f = a["w_down"].shape[1] * 4
    cx, cy, _ = _me()
    my_shard = 2 * cx + cy

    gathered = _gather_weights(
        [a[n].astype(BF16) for n in BIG] + [a[n] for n in SMALL_SHARDED],
        [BIG_LAYOUT[n] for n in BIG] + [SLABS, SLABS], [True] * len(BIG) + [False, False])
    wfull = dict(zip(BIG + SMALL_SHARDED, gathered))
    w_in = jnp.moveaxis(wfull["w_in"], 0, 2).reshape(nl, d, -1)
    zc = w_in.shape[-1]
    b_conv_w = jnp.moveaxis(wfull["b_conv_w"], 0, 2).reshape(nl, 4, W_GRP)
    ffn_conv_w = jnp.moveaxis(wfull["ffn_conv_w"], 0, 2).reshape(nl, 3, 2 * f)
    w_out, w_up, w_down, w_pe, w_pg = (wfull[n] for n in ("w_out", "w_up", "w_down", "w_pe", "w_pg"))

    lbs, lbs_vjp = jax.vjp(_lower_bounds, a["c_lb"])
    tril = jnp.tril(jnp.ones((GMLP_CHUNK, GMLP_CHUNK), F32))

    def layer_params(l):
        q = {}
        q["wm"] = (a["a_ws"][l] * tril).astype(BF16)
        q["wm_t"] = jnp.swapaxes(q["wm"], 1, 2)
        q["bs_t"] = jnp.repeat(a["a_bs"][l].T, HEAD_DIM, axis=1)
        for nm in ("b_wa", "b_wx", "d_w"):
            bd = _block_diag(a[nm][l]).astype(BF16)
            q[nm], q[nm + "_t"] = bd, bd.T
        for nm in ("a_ln_g", "a_ln_b", "b_conv_b", "b_ba", "b_bx", "b_lam", "d_scale"):
            q[nm] = a[nm][l].reshape(1, W_GRP)
        q["lb"] = lbs[l].reshape(1, W_GRP)
        q["ng"] = jnp.tile(a["c_norm_g"][l], N_HEADS).reshape(1, W_GRP)
        q["b_conv_w"] = b_conv_w[l]
        q["ffn_conv_w"] = ffn_conv_w[l]
        q["ffn_conv_b"] = a["ffn_conv_b"][l].reshape(1, 2 * f)
        return q

    saved = []
    xl = x0
    for l in range(nl):
        q = layer_params(l)
        s = {"x0": xl}
        s["h1"] = _rms_fwd(xl, a["norm1_g"][l], "rms1_fwd")
        s["z"] = _mm(s["h1"], w_in, "nn", b_layer=l, out_dtype=F32, name="mm_z")
        mix = _mix_a_fwd(s["z"], d, q["a_ln_g"], q["a_ln_b"], q["wm"], q["bs_t"])
        mix, s["hs"] = _mix_b_fwd(s["z"], mix, q["b_conv_w"], q["b_conv_b"], q["b_wa"], q["b_wx"], q["b_ba"], q["b_bx"],
                                  q["b_lam"])
        mix, s["o_pre"], s["states"] = _mix_c_fwd(s["z"], mix, q["lb"], q["ng"])
        s["mix"] = _mix_d_fwd(s["z"], mix, q["d_w"], q["d_scale"])
        s["x1"] = _mm(s["mix"], w_out, "nn", b_layer=l, res=xl, out_dtype=F32, name="mm_out")
        s["h2"] = _rms_fwd(s["x1"], a["norm2_g"][l], "rms2_fwd")
        s["hf_g"] = _mm(s["h2"], w_up, "nn", b_layer=l, n=f, out_dtype=F32, name="mm_up_g")
        s["hf_v"] = _mm(s["h2"], w_up, "nn", b_layer=l, n=f, b_noff=f, out_dtype=F32, name="mm_up_v")
        s["act"] = _ffn_act_fwd(s["hf_g"], s["hf_v"], q["ffn_conv_w"], q["ffn_conv_b"])
        s["x2"] = _mm(s["act"], w_down, "nn", b_layer=l, res=s["x1"], out_dtype=F32, name="mm_down")
        s["h3"] = _rms_fwd(s["x2"], a["norm3_g"][l], "rms3_fwd")
        s["pre"] = _mm(s["h3"], w_pg, "nn", b_layer=l, out_dtype=F32, name="mm_pg")
        s["pe"] = _mm(a["p"][l, 0], w_pe, "nn", b_layer=l, out_dtype=F32, name="mm_pe")
        xl = _ple_fwd(s["x2"], s["pe"], s["pre"])
        saved.append(s)

    dx, g_final, loss = _final_loss(xl, a["final_g"], target)
    loss = lax.psum(loss[0, 0], ("x", "y", "c"))

    gbig = {n: None for n in BIG}
    gsm = {n: [None] * nl for n in SMALL if n != "final_g"}
    for l in reversed(range(nl)):
        q, s = layer_params(l), saved[l]
        dpe, dpre = _ple_bwd(dx, s["pe"], s["pre"])
        gbig["w_pe"] = _mm(a["p"][l, 0], dpe, "tn", out_dtype=BF16, name="mm_dwpe", layer=l, nlayers=nl, out_buf=gbig["w_pe"])
        gbig["w_pg"] = _mm(s["h3"], dpre, "tn", out_dtype=BF16, name="mm_dwpg", layer=l, nlayers=nl, out_buf=gbig["w_pg"])
        dh3 = _mm(dpre, w_pg, "nt", b_layer=l, out_dtype=BF16, name="mm_dh3")
        dx2, gsm["norm3_g"][l] = _rms_bwd(dh3, s["x2"], a["norm3_g"][l], dx, "rms3_bwd")
        gbig["w_down"] = _mm(s["act"], dx2, "tn", out_dtype=BF16, name="mm_dwdown", layer=l, nlayers=nl, out_buf=gbig["w_down"])
        dact = _mm(dx2, w_down, "nt", b_layer=l, out_dtype=BF16, name="mm_dact")
        dhf_g, dhf_v, dcw_g, dcw_v, dcb_g, dcb_v = _ffn_act_bwd(dact, s["hf_g"], s["hf_v"], q["ffn_conv_w"], q["ffn_conv_b"])
        gsm["ffn_conv_w"][l] = jnp.concatenate([dcw_g, dcw_v], axis=1)
        gsm["ffn_conv_b"][l] = jnp.concatenate([dcb_g, dcb_v], axis=1)
        gbig["w_up"] = _mm(s["h2"], dhf_g, "tn", out_dtype=BF16, name="mm_dwup_g", layer=l, nlayers=nl, out_n=2 * f,
                           out_buf=gbig["w_up"])
        gbig["w_up"] = _mm(s["h2"], dhf_v, "tn", out_dtype=BF16, name="mm_dwup_v", layer=l, nlayers=nl, out_n=2 * f, o_noff=f,
                           out_buf=gbig["w_up"])
        dh2 = _mm(dhf_g, w_up, "nt", b_layer=l, out_dtype=F32, name="mm_dh2_g")
        dh2 = _mm(dhf_v, w_up, "nt", b_layer=l, b_koff=f, res=dh2, out_dtype=F32, name="mm_dh2_v")
        dx1, gsm["norm2_g"][l] = _rms_bwd(dh2, s["x1"], a["norm2_g"][l], dx2, "rms2_bwd")
        gbig["w_out"] = _mm(s["mix"], dx1, "tn", out_dtype=BF16, name="mm_dwout", layer=l, nlayers=nl, out_buf=gbig["w_out"])
        dmix = _mm(dx1, w_out, "nt", b_layer=l, out_dtype=F32, name="mm_dmix")
        dz, gsm["a_ln_g"][l], gsm["a_ln_b"][l], dws, dbs_t = _mix_a_bwd(s["z"], dmix, q["a_ln_g"], q["a_ln_b"], q["wm"],
                                                                       q["wm_t"], q["bs_t"])
        gsm["a_ws"][l] = dws * tril
        gsm["a_bs"][l] = dbs_t.reshape(GMLP_CHUNK, N_HEADS, HEAD_DIM).sum(-1).T
        (dz, gsm["b_conv_w"][l], gsm["b_conv_b"][l], dwa, dwx, gsm["b_ba"][l], gsm["b_bx"][l],
         gsm["b_lam"][l]) = _mix_b_bwd(s["z"], dz, dmix, s["hs"], q["b_conv_w"], q["b_conv_b"], q["b_wa"], q["b_wx"],
                                       q["b_wa_t"], q["b_wx_t"], q["b_ba"], q["b_bx"], q["b_lam"])
        gsm["b_wa"][l], gsm["b_wx"][l] = _diag_blocks(dwa), _diag_blocks(dwx)
        dz, gsm["c_lb"][l], dng = _mix_c_bwd(s["z"], dz, dmix, s["o_pre"], s["states"], q["lb"], q["ng"])
        gsm["c_norm_g"][l] = dng.reshape(N_HEADS, HEAD_DIM).sum(0)
        dz, dwd, gsm["d_scale"][l] = _mix_d_bwd(s["z"], dz, dmix, q["d_w"], q["d_w_t"], q["d_scale"])
        gsm["d_w"][l] = _diag_blocks(dwd)
        gbig["w_in"] = _mm(s["h1"], dz, "tn", out_dtype=BF16, name="mm_dwin", layer=l, nlayers=nl, out_buf=gbig["w_in"])
        dh1 = _mm(dz, w_in, "nt", b_layer=l, out_dtype=BF16, name="mm_dh1")
        dx, gsm["norm1_g"][l] = _rms_bwd(dh1, s["x0"], a["norm1_g"][l], dx1, "rms1_bwd")

    grad_x = dx[None]

    gsmall = {n: jnp.stack([g.reshape(a[n].shape[1:]) if n not in SMALL_SHARDED else g for g in gsm[n]]) for n in gsm}
    gsmall["c_lb"] = lbs_vjp(gsmall["c_lb"])[0]
    gsmall["final_g"] = g_final.reshape(-1)
    flat = jnp.concatenate([gsmall[n].reshape(-1) for n in SMALL])
    n_small = flat.shape[0]
    pad = (-n_small) % 1024
    small_slab = jnp.pad(flat, (0, pad)).reshape(-1, 128)

    gbig["w_in"] = jnp.moveaxis(gbig["w_in"].reshape(nl, d, 4, zc // 4), 2, 0)
    pushed = _push_partials([gbig[n] for n in BIG], [BIG_LAYOUT[n] for n in BIG], small_slab)
    halves = [_slot_sum(sl.reshape(N_DEV, -1, sl.shape[-1]), "sum_" + n).reshape(sl.shape[1:]) for n, sl in zip(BIG, pushed)]
    shard_grads = dict(zip(BIG, _swap_halves(halves)))
    small_sum = _slot_sum(pushed[-1], "sum_small").reshape(-1)[:n_small]

    grads, off = {}, 0
    for n in SMALL:
        shape = gsmall[n].shape
        size = 1
        for s_ in shape:
            size *= s_
        g = small_sum[off:off + size].reshape(shape)
        off += size
        if n in SMALL_SHARDED:
            cs = a[n].shape[-1]
            g = lax.dynamic_slice_in_dim(g, my_shard * cs, cs, axis=2)
        grads[n] = g
    grads.update(shard_grads)

    outs_g, outs_d, outs_m, outs_v = [], [], [], []
    for n in WEIGHTS:
        dlt, nm, nv = _adamw(a[n], grads[n], a["m_" + n], a["v_" + n], "adamw_" + n)
        outs_g.append(grads[n])
        outs_d.append(dlt)
        outs_m.append(nm)
        outs_v.append(nv)
    return (loss, grad_x, *outs_g, *outs_d, *outs_m, *outs_v)
```

```python
import functools

import jax
import jax.numpy as jnp
from jax import lax
from jax.experimental import pallas as pl
from jax.experimental.pallas import tpu as pltpu

F32 = jnp.float32
BF16 = jnp.bfloat16
EPS = 1e-6
HEAD_DIM = 64
N_HEADS = 4
W_GRP = HEAD_DIM * N_HEADS
GMLP_CHUNK = 128
HGRN_CHUNK = 64
RGLRU_C = 8.0
POOL_HALO = 16
EXP_CLAMP = 80.0
ADAM_LR, ADAM_B1, ADAM_B2, ADAM_EPS, ADAM_WD, ADAM_STEP = 0.001, 0.9, 0.999, 1e-08, 0.01, 10
VMEM_LIMIT_BYTES = 56 * 1024 * 1024
TILE_PREFS = (1024, 1408, 768, 512, 256, 128)
ROW_TILE_PREFS = (512, 256, 128, 64, 32, 16, 8)
MESH_ID = pl.DeviceIdType.MESH
N_DEV = 8


def _pick(n, prefs=TILE_PREFS):
    for p in prefs:
        if n % p == 0:
            return p
    return n


def _cp(*sem):
    return pltpu.CompilerParams(dimension_semantics=sem if sem else None, vmem_limit_bytes=VMEM_LIMIT_BYTES)


def _sds(shape, dtype):
    return jax.ShapeDtypeStruct(tuple(shape), dtype)


_GELU_C = 0.7978845608028654
_GELU_A = 0.044715


def _gelu(x):
    return 0.5 * x * (1.0 + jnp.tanh(_GELU_C * (x + _GELU_A * x * x * x)))


def _gelu_and_grad(x):
    t = jnp.tanh(_GELU_C * (x + _GELU_A * x * x * x))
    g = 0.5 * x * (1.0 + t)
    dg = 0.5 * (1.0 + t) + 0.5 * x * (1.0 - t * t) * _GELU_C * (1.0 + 3.0 * _GELU_A * x * x)
    return g, dg


def _sigmoid(x):
    return 1.0 / (1.0 + jnp.exp(-x))


def _dot(a, b):
    return jnp.dot(a, b, preferred_element_type=F32)


def _dot_nt(a, b):
    return lax.dot_general(a, b, (((1,), (1,)), ((), ())), preferred_element_type=F32)


def _dot_tn(a, b):
    return lax.dot_general(a, b, (((0,), (0,)), ((), ())), preferred_element_type=F32)


def _split3(x):
    hi = x.astype(BF16)
    r1 = x - hi.astype(F32)
    mid = r1.astype(BF16)
    lo = (r1 - mid.astype(F32)).astype(BF16)
    return hi, mid, lo


def _dot_f32_rhs_exact(x, m_bf16):
    hi, mid, lo = _split3(x)
    return _dot(hi, m_bf16) + _dot(mid, m_bf16) + _dot(lo, m_bf16)


def _dot_f32_lhs_exact(m_bf16, x):
    hi, mid, lo = _split3(x)
    return _dot(m_bf16, hi) + _dot(m_bf16, mid) + _dot(m_bf16, lo)


def _head_masks(width=W_GRP):
    lane = lax.broadcasted_iota(jnp.int32, (1, width), 1)
    return [(lane >= h * HEAD_DIM) & (lane < (h + 1) * HEAD_DIM) for h in range(N_HEADS)]


def _block_mask(n=W_GRP):
    r = lax.broadcasted_iota(jnp.int32, (n, n), 0)
    c = lax.broadcasted_iota(jnp.int32, (n, n), 1)
    m = None
    for h in range(N_HEADS):
        mh = (r >= h * HEAD_DIM) & (r < (h + 1) * HEAD_DIM) & (c >= h * HEAD_DIM) & (c < (h + 1) * HEAD_DIM)
        m = mh if m is None else (m | mh)
    return m


def _mm(a, b, mode, *, out_dtype, name, res=None, b_slabs=False, n=None, b_noff=0, b_koff=0,
        out_slabs=0, out_buf=None, out_n=None, o_noff=0, after=None):
    if mode == "tn":
        k_dim, m_dim = a.shape
    else:
        m_dim, k_dim = a.shape
    if mode == "nt":
        n_dim = b.shape[-2]
    else:
        n_dim = n if n is not None else (b.shape[0] * b.shape[2] if b_slabs else b.shape[1])
    n_total = out_n if out_n is not None else n_dim
    tm, tn, tk = _pick(m_dim), _pick(n_dim), _pick(k_dim)
    if b_slabs and mode == "nt":
        tk = _pick(b.shape[2])
    elif b_slabs:
        tn = _pick(b.shape[2])
    elif out_slabs:
        tn = _pick(n_total // out_slabs)
    nk = k_dim // tk
    assert b_noff % tn == 0 and b_koff % tk == 0 and o_noff % tn == 0 and n_dim % tn == 0 and k_dim % tk == 0
    bn0, bk0, on0 = b_noff // tn, b_koff // tk, o_noff // tn
    dims = {"nn": (((1,), (0,)), ((), ())), "nt": (((1,), (1,)), ((), ())), "tn": (((0,), (0,)), ((), ()))}[mode]

    if mode == "tn":
        a_spec = pl.BlockSpec((tk, tm), lambda i, j, k: (k, i))
    else:
        a_spec = pl.BlockSpec((tm, tk), lambda i, j, k: (i, k))
    if not b_slabs:
        if mode == "nt":
            b_spec = pl.BlockSpec((tn, tk), lambda i, j, k: (j + bn0, k + bk0))
        else:
            b_spec = pl.BlockSpec((tk, tn), lambda i, j, k: (k + bk0, j + bn0))
    elif mode == "nt":
        bper = b.shape[2] // tk
        b_spec = pl.BlockSpec((None, tn, tk), lambda i, j, k: ((k + bk0) // bper, j, (k + bk0) % bper))
    else:
        bper = b.shape[2] // tn
        b_spec = pl.BlockSpec((None, tk, tn), lambda i, j, k: ((j + bn0) // bper, k, (j + bn0) % bper))
    in_specs = [a_spec, b_spec]
    args = [a, b]
    if res is not None:
        in_specs.append(pl.BlockSpec((tm, tn), lambda i, j, k: (i, j)))
        args.append(res)
    if out_slabs:
        oper = n_total // out_slabs // tn
        out_shape = _sds((out_slabs, m_dim, n_total // out_slabs), out_dtype)
        out_spec = pl.BlockSpec((None, tm, tn), lambda i, j, k: ((j + on0) // oper, i, (j + on0) % oper))
    else:
        out_shape = _sds((m_dim, n_total), out_dtype)
        out_spec = pl.BlockSpec((tm, tn), lambda i, j, k: (i, j + on0))
    aliases = {}
    if out_buf is not None:
        in_specs.append(pl.BlockSpec(memory_space=pl.ANY))
        args.append(out_buf)
        aliases = {len(args) - 1: 0}
    if after is not None:
        in_specs.append(pl.BlockSpec(memory_space=pl.ANY))
        args.append(after)
    has_res = res is not None
    n_in = len(args)

    def body(*refs):
        a_ref, b_ref = refs[0], refs[1]
        res_ref = refs[2] if has_res else None
        o_ref = refs[n_in]
        acc_ref = refs[-1] if nk > 1 else None
        part = lax.dot_general(a_ref[...].astype(BF16), b_ref[...].astype(BF16), dims, preferred_element_type=F32)

        def finish(v):
            if has_res:
                v = v + res_ref[...]
            o_ref[...] = v.astype(o_ref.dtype)

        if nk == 1:
            finish(part)
        else:
            kk = pl.program_id(2)

            @pl.when(kk == 0)
            def _():
                acc_ref[...] = part

            @pl.when(kk > 0)
            def _():
                acc_ref[...] += part

            @pl.when(kk == nk - 1)
            def _():
                finish(acc_ref[...])

    return pl.pallas_call(
        body, grid=(m_dim // tm, n_dim // tn, nk), in_specs=in_specs, out_specs=out_spec, out_shape=out_shape,
        scratch_shapes=[pltpu.VMEM((tm, tn), F32)] if nk > 1 else [],
        input_output_aliases=aliases, name=name, compiler_params=_cp("parallel", "parallel", "arbitrary"),
    )(*args)


def _rms_fwd(x, g, name):
    t, d = x.shape
    tm = _pick(t, ROW_TILE_PREFS)

    def body(x_ref, g_ref, o_ref):
        xv = x_ref[...]
        r = lax.rsqrt(jnp.mean(xv * xv, axis=-1, keepdims=True) + EPS)
        o_ref[...] = (xv * r * g_ref[...]).astype(o_ref.dtype)

    return pl.pallas_call(
        body, grid=(t // tm,),
        in_specs=[pl.BlockSpec((tm, d), lambda i: (i, 0)), pl.BlockSpec((1, d), lambda i: (0, 0))],
        out_specs=pl.BlockSpec((tm, d), lambda i: (i, 0)), out_shape=_sds((t, d), BF16),
        name=name, compiler_params=_cp("parallel"),
    )(x, g.reshape(1, d))


def _rms_bwd(dh, x, g, dres, name):
    t, d = x.shape
    tm = _pick(t, ROW_TILE_PREFS)

    def body(dh_ref, x_ref, g_ref, dres_ref, dx_ref, dg_ref):
        i = pl.program_id(0)
        xv = x_ref[...]
        dy = dh_ref[...].astype(F32)
        r = lax.rsqrt(jnp.mean(xv * xv, axis=-1, keepdims=True) + EPS)
        dyg = dy * g_ref[...]
        dot = jnp.mean(dyg * xv, axis=-1, keepdims=True)
        dx_ref[...] = dres_ref[...] + r * dyg - xv * (r * r * r) * dot
        part = jnp.sum(dy * xv * r, axis=0, keepdims=True)

        @pl.when(i == 0)
        def _():
            dg_ref[...] = part

        @pl.when(i > 0)
        def _():
            dg_ref[...] += part

    row = pl.BlockSpec((tm, d), lambda i: (i, 0))
    vec = pl.BlockSpec((1, d), lambda i: (0, 0))
    return pl.pallas_call(
        body, grid=(t // tm,), in_specs=[row, row, vec, row], out_specs=[row, vec],
        out_shape=[_sds((t, d), F32), _sds((1, d), F32)], name=name, compiler_params=_cp("arbitrary"),
    )(dh, x, g.reshape(1, d), dres)


def _final_loss(x, g, target):
    t, d = x.shape
    tm = _pick(t, ROW_TILE_PREFS)

    def body(x_ref, g_ref, t_ref, dx_ref, dg_ref, loss_ref):
        i = pl.program_id(0)
        xv = x_ref[...]
        gv = g_ref[...]
        r = lax.rsqrt(jnp.mean(xv * xv, axis=-1, keepdims=True) + EPS)
        err = xv * r * gv - t_ref[...]
        lpart = (0.5 / d) * jnp.sum(jnp.sum(err * err, axis=1, keepdims=True), axis=0, keepdims=True)
        dy = err * (1.0 / d)
        dyg = dy * gv
        dot = jnp.mean(dyg * xv, axis=-1, keepdims=True)
        dx_ref[...] = r * dyg - xv * (r * r * r) * dot
        part = jnp.sum(dy * xv * r, axis=0, keepdims=True)

        @pl.when(i == 0)
        def _():
            dg_ref[...] = part
            loss_ref[...] = lpart

        @pl.when(i > 0)
        def _():
            dg_ref[...] += part
            loss_ref[...] += lpart

    row = pl.BlockSpec((tm, d), lambda i: (i, 0))
    vec = pl.BlockSpec((1, d), lambda i: (0, 0))
    return pl.pallas_call(
        body, grid=(t // tm,), in_specs=[row, vec, row], out_specs=[row, vec, pl.BlockSpec((1, 1), lambda i: (0, 0))],
        out_shape=[_sds((t, d), F32), _sds((1, d), F32), _sds((1, 1), F32)], name="final_loss",
        compiler_params=_cp("arbitrary"),
    )(x, g.reshape(1, d), target)


def _shift_down(ext, k, halo):
    return pltpu.roll(ext, k, 0)[halo:]


def _shift_up(ext, k, tm):
    return pltpu.roll(ext, ext.shape[0] - k, 0)[:tm]


def _ffn_tiles(t, f):
    return _pick(t, ROW_TILE_PREFS), _pick(f, (256, 128))


def _ffn_act_fwd(hf_g, hf_v, conv_w, conv_b):
    t, f = hf_g.shape
    tm, cn = _ffn_tiles(t, f)
    nf = f // cn

    def body(g_ref, v_ref, wg_ref, wv_ref, bg_ref, bv_ref, o_ref, ext_ref, hg_ref, hv_ref):
        i = pl.program_id(1)

        @pl.when(i == 0)
        def _():
            hg_ref[...] = jnp.zeros_like(hg_ref)
            hv_ref[...] = jnp.zeros_like(hv_ref)

        def conv(x_ref, halo_ref, w_ref, b_ref):
            ext_ref[0:8, :] = halo_ref[...]
            ext_ref[8:, :] = x_ref[...]
            halo_ref[...] = x_ref[tm - 8:tm, :]
            ext = ext_ref[...]
            w = w_ref[...]
            return b_ref[...] + w[2:3, :] * ext[8:] + w[1:2, :] * _shift_down(ext, 1, 8) + w[0:1, :] * _shift_down(ext, 2, 8)

        gc = conv(g_ref, hg_ref, wg_ref, bg_ref)
        vc = conv(v_ref, hv_ref, wv_ref, bv_ref)
        o_ref[...] = (_gelu(gc) * vc).astype(o_ref.dtype)

    blk = pl.BlockSpec((tm, cn), lambda j, i: (i, j))
    return pl.pallas_call(
        body, grid=(nf, t // tm),
        in_specs=[blk, blk, pl.BlockSpec((3, cn), lambda j, i: (0, j)), pl.BlockSpec((3, cn), lambda j, i: (0, j + nf)),
                  pl.BlockSpec((1, cn), lambda j, i: (0, j)), pl.BlockSpec((1, cn), lambda j, i: (0, j + nf))],
        out_specs=blk, out_shape=_sds((t, f), BF16),
        scratch_shapes=[pltpu.VMEM((tm + 8, cn), F32), pltpu.VMEM((8, cn), F32), pltpu.VMEM((8, cn), F32)],
        name="ffn_act_fwd", compiler_params=_cp("parallel", "arbitrary"),
    )(hf_g, hf_v, conv_w, conv_w, conv_b, conv_b)


def _ffn_act_bwd(dact, hf_g, hf_v, conv_w, conv_b):
    t, f = hf_g.shape
    tm, cn = _ffn_tiles(t, f)
    nf, nt = f // cn, t // tm
    hb = tm // 8

    def body(da_ref, g_ref, v_ref, gh_ref, vh_ref, wg_ref, wv_ref, bg_ref, bv_ref,
             dg_ref, dv_ref, dwg_ref, dwv_ref, dbg_ref, dbv_ref, ext_ref, cg_ref, cv_ref):
        i = pl.program_id(1)
        first_tile = i == nt - 1

        @pl.when(i == 0)
        def _():
            cg_ref[...] = jnp.zeros_like(cg_ref)
            cv_ref[...] = jnp.zeros_like(cv_ref)
            dwg_ref[...] = jnp.zeros_like(dwg_ref)
            dwv_ref[...] = jnp.zeros_like(dwv_ref)
            dbg_ref[...] = jnp.zeros_like(dbg_ref)
            dbv_ref[...] = jnp.zeros_like(dbv_ref)

        def shifted(x_ref, halo_ref):
            ext_ref[0:8, :] = jnp.where(first_tile, 0.0, halo_ref[...])
            ext_ref[8:, :] = x_ref[...]
            ext = ext_ref[0:tm + 8, :]
            return ext[8:], _shift_down(ext, 1, 8), _shift_down(ext, 2, 8)

        wg, wv = wg_ref[...], wv_ref[...]
        g0, g1, g2 = shifted(g_ref, gh_ref)
        gc = bg_ref[...] + wg[2:3, :] * g0 + wg[1:2, :] * g1 + wg[0:1, :] * g2
        v0, v1, v2 = shifted(v_ref, vh_ref)
        vc = bv_ref[...] + wv[2:3, :] * v0 + wv[1:2, :] * v1 + wv[0:1, :] * v2
        da = da_ref[...].astype(F32)
        gel, dgel = _gelu_and_grad(gc)
        dgc = da * vc * dgel
        dvc = da * gel

        def back(dc, carry_ref, w, x0, x1, x2, dx_ref, dw_ref, db_ref):
            ext_ref[0:tm, :] = dc
            ext_ref[tm:tm + 8, :] = carry_ref[...]
            carry_ref[...] = dc[0:8, :]
            ext = ext_ref[0:tm + 8, :]
            dx = w[2:3, :] * dc + w[1:2, :] * _shift_up(ext, 1, tm) + w[0:1, :] * _shift_up(ext, 2, tm)
            dx_ref[...] = dx.astype(dx_ref.dtype)
            dw_ref[0:1, :] += jnp.sum(dc * x2, axis=0, keepdims=True)
            dw_ref[1:2, :] += jnp.sum(dc * x1, axis=0, keepdims=True)
            dw_ref[2:3, :] += jnp.sum(dc * x0, axis=0, keepdims=True)
            db_ref[...] += jnp.sum(dc, axis=0, keepdims=True)

        back(dgc, cg_ref, wg, g0, g1, g2, dg_ref, dwg_ref, dbg_ref)
        back(dvc, cv_ref, wv, v0, v1, v2, dv_ref, dwv_ref, dbv_ref)

    blk = pl.BlockSpec((tm, cn), lambda j, i: (nt - 1 - i, j))
    halo = pl.BlockSpec((8, cn), lambda j, i: (jnp.maximum((nt - 1 - i) * hb - 1, 0), j))
    w3g = pl.BlockSpec((3, cn), lambda j, i: (0, j))
    w3v = pl.BlockSpec((3, cn), lambda j, i: (0, j + nf))
    b1g = pl.BlockSpec((1, cn), lambda j, i: (0, j))
    b1v = pl.BlockSpec((1, cn), lambda j, i: (0, j + nf))
    acc3 = pl.BlockSpec((3, cn), lambda j, i: (0, j))
    acc1 = pl.BlockSpec((1, cn), lambda j, i: (0, j))
    return pl.pallas_call(
        body, grid=(nf, nt),
        in_specs=[blk, blk, blk, halo, halo, w3g, w3v, b1g, b1v],
        out_specs=[blk, blk, acc3, acc3, acc1, acc1],
        out_shape=[_sds((t, f), BF16), _sds((t, f), BF16), _sds((3, f), F32), _sds((3, f), F32),
                   _sds((1, f), F32), _sds((1, f), F32)],
        scratch_shapes=[pltpu.VMEM((tm + 8, cn), F32), pltpu.VMEM((8, cn), F32), pltpu.VMEM((8, cn), F32)],
        name="ffn_act_bwd", compiler_params=_cp("parallel", "arbitrary"),
    )(dact, hf_g, hf_v, hf_g, hf_v, conv_w, conv_w, conv_b, conv_b)


def _ple_fwd(x2, pe, pre):
    t, d = x2.shape
    tm = _pick(t, ROW_TILE_PREFS)

    def body(x_ref, pe_ref, pre_ref, o_ref):
        o_ref[...] = x_ref[...] + pe_ref[...] * _sigmoid(pre_ref[...])

    row = pl.BlockSpec((tm, d), lambda i: (i, 0))
    return pl.pallas_call(body, grid=(t // tm,), in_specs=[row, row, row], out_specs=row,
                          out_shape=_sds((t, d), F32), name="ple_fwd", compiler_params=_cp("parallel"))(x2, pe, pre)


def _ple_bwd(dx3, pe, pre, after=None):
    t, d = dx3.shape
    tm = _pick(t, ROW_TILE_PREFS)

    def body(dx_ref, pe_ref, pre_ref, *rest):
        dpe_ref, dpre_ref = rest[-2:]
        gate = _sigmoid(pre_ref[...])
        dx = dx_ref[...]
        dpe_ref[...] = (dx * gate).astype(dpe_ref.dtype)
        dpre_ref[...] = (dx * pe_ref[...] * gate * (1.0 - gate)).astype(dpre_ref.dtype)

    row = pl.BlockSpec((tm, d), lambda i: (i, 0))
    extra = [] if after is None else [after]
    return pl.pallas_call(body, grid=(t // tm,), in_specs=[row, row, row] + [pl.BlockSpec(memory_space=pl.ANY)] * len(extra),
                          out_specs=[row, row], out_shape=[_sds((t, d), BF16), _sds((t, d), BF16)], name="ple_bwd",
                          compiler_params=_cp("parallel"))(dx3, pe, pre, *extra)


def _mix_tm(t):
    return _pick(t, (512, 256, 128))


def _zblk(tm, col, rev_nt=None):
    if rev_nt is None:
        return pl.BlockSpec((tm, W_GRP), lambda i: (i, col))
    return pl.BlockSpec((tm, W_GRP), lambda i: (rev_nt - 1 - i, col))


def _full(shape):
    nd = len(shape)
    return pl.BlockSpec(tuple(shape), lambda i: (0,) * nd)


def _gmlp_sv(wm_ref, vnc, bs, hm):
    sv = bs
    for h in range(N_HEADS):
        sv = sv + jnp.where(hm[h], _dot(wm_ref[h], vnc), 0.0)
    return sv


def _layernorm(v, g, b):
    mu = jnp.mean(v, axis=-1, keepdims=True)
    vc = v - mu
    rs = lax.rsqrt(jnp.mean(vc * vc, axis=-1, keepdims=True) + EPS)
    xhat = vc * rs
    return xhat, rs, xhat * g + b


def _mix_a_fwd(z, d_mix, ln_g, ln_b, wm, bs_t):
    t = z.shape[0]
    tm = _mix_tm(t)

    def body(u_ref, v_ref, g_ref, b_ref, wm_ref, bs_ref, o_ref):
        hm = _head_masks()
        ug = _gelu(u_ref[...])
        _, _, vn = _layernorm(_gelu(v_ref[...]), g_ref[...], b_ref[...])
        vnb = vn.astype(BF16)
        for n in range(tm // GMLP_CHUNK):
            sl = slice(n * GMLP_CHUNK, (n + 1) * GMLP_CHUNK)
            o_ref[sl, :] = ug[sl] * _gmlp_sv(wm_ref, vnb[sl], bs_ref[...], hm)

    return pl.pallas_call(
        body, grid=(t // tm,),
        in_specs=[_zblk(tm, 0), _zblk(tm, 1), _full((1, W_GRP)), _full((1, W_GRP)), _full(wm.shape), _full(bs_t.shape)],
        out_specs=_zblk(tm, 0), out_shape=_sds((t, d_mix), F32), name="mix_a_fwd", compiler_params=_cp("parallel"),
    )(z, z, ln_g, ln_b, wm, bs_t)


def _mix_a_bwd(z, dmix, ln_g, ln_b, wm, wm_t, bs_t):
    t, zc = z.shape
    tm = _mix_tm(t)

    def body(u_ref, v_ref, dy_ref, g_ref, b_ref, wm_ref, wmt_ref, bs_ref, dz_ref, dg_ref, db_ref, dws_ref, dbs_ref):
        i = pl.program_id(0)

        @pl.when(i == 0)
        def _():
            dg_ref[...] = jnp.zeros_like(dg_ref)
            db_ref[...] = jnp.zeros_like(db_ref)
            dws_ref[...] = jnp.zeros_like(dws_ref)
            dbs_ref[...] = jnp.zeros_like(dbs_ref)

        hm = _head_masks()
        ug, dug = _gelu_and_grad(u_ref[...])
        vg, dvg = _gelu_and_grad(v_ref[...])
        gv = g_ref[...]
        xhat, rs, vn = _layernorm(vg, gv, b_ref[...])
        vnb = vn.astype(BF16)
        dy = dy_ref[...]
        for n in range(tm // GMLP_CHUNK):
            sl = slice(n * GMLP_CHUNK, (n + 1) * GMLP_CHUNK)
            vnc = vnb[sl]
            sv = _gmlp_sv(wm_ref, vnc, bs_ref[...], hm)
            dsv = dy[sl] * ug[sl]
            dz_ref[sl, 0:W_GRP] = dy[sl] * sv * dug[sl]
            dbs_ref[...] += dsv
            dsvb = dsv.astype(BF16)
            dvn = jnp.zeros((GMLP_CHUNK, W_GRP), F32)
            for h in range(N_HEADS):
                dws_ref[h] += _dot_nt(jnp.where(hm[h], dsv, 0.0).astype(BF16), vnc)
                dvn = dvn + jnp.where(hm[h], _dot(wmt_ref[h], dsvb), 0.0)
            xh = xhat[sl]
            dg_ref[...] += jnp.sum(dvn * xh, axis=0, keepdims=True)
            db_ref[...] += jnp.sum(dvn, axis=0, keepdims=True)
            dxh = dvn * gv
            dvg_c = rs[sl] * (dxh - jnp.mean(dxh, axis=-1, keepdims=True) - xh * jnp.mean(dxh * xh, axis=-1, keepdims=True))
            dz_ref[sl, W_GRP:2 * W_GRP] = dvg_c * dvg[sl]

    return pl.pallas_call(
        body, grid=(t // tm,),
        in_specs=[_zblk(tm, 0), _zblk(tm, 1), _zblk(tm, 0), _full((1, W_GRP)), _full((1, W_GRP)), _full(wm.shape),
                  _full(wm_t.shape), _full(bs_t.shape)],
        out_specs=[pl.BlockSpec((tm, 2 * W_GRP), lambda i: (i, 0)), _full((1, W_GRP)), _full((1, W_GRP)),
                   _full(wm.shape), _full(bs_t.shape)],
        out_shape=[_sds((t, zc), F32), _sds((1, W_GRP), F32), _sds((1, W_GRP), F32), _sds(wm.shape, F32),
                   _sds(bs_t.shape, F32)],
        name="mix_a_bwd", compiler_params=_cp("arbitrary"),
    )(z, z, dmix, ln_g, ln_b, wm, wm_t, bs_t)


def _softplus(x):
    return jnp.maximum(x, 0.0) + jnp.log(1.0 + jnp.exp(-jnp.abs(x)))


def _neg_expm1(x):
    series = -x * (1.0 + x * 0.5 * (1.0 + x * (1.0 / 3.0) * (1.0 + x * 0.25 * (1.0 + x * 0.2))))
    return jnp.where(x > -0.1, series, 1.0 - jnp.exp(x))


def _rglru_gates(ext_ref, x_ref, halo, cw, cb, wa_ref, wx_ref, ba, bx, lam):
    ext_ref[0:8, :] = halo
    ext_ref[8:, :] = x_ref[...]
    ext = ext_ref[...]
    x0, x1, x2, x3 = ext[8:], _shift_down(ext, 1, 8), _shift_down(ext, 2, 8), _shift_down(ext, 3, 8)
    xc = cb + cw[3:4, :] * x0 + cw[2:3, :] * x1 + cw[1:2, :] * x2 + cw[0:1, :] * x3
    xcb = xc.astype(BF16)
    r = _sigmoid(_dot(xcb, wa_ref[...]) + ba)
    ig = _sigmoid(_dot(xcb, wx_ref[...]) + bx)
    sp = _softplus(-lam)
    la = -RGLRU_C * r * sp
    a = jnp.exp(la)
    mult = jnp.sqrt(_neg_expm1(2.0 * la))
    return (x0, x1, x2, x3), xc, r, ig, sp, a, mult


def _mix_b_fwd(z, mix, conv_w, conv_b, wa, wx, ba, bx, lam):
    t = z.shape[0]
    tm = _mix_tm(t)

    def body(x_ref, gb_ref, cw_ref, cb_ref, wa_ref, wx_ref, ba_ref, bx_ref, lam_ref, mix_in, o_ref, hs_ref,
             ext_ref, a_ref, b_ref, xh_ref, hc_ref):
        i = pl.program_id(0)

        @pl.when(i == 0)
        def _():
            xh_ref[...] = jnp.zeros_like(xh_ref)
            hc_ref[...] = jnp.zeros_like(hc_ref)

        _, xc, _, ig, _, a, mult = _rglru_gates(ext_ref, x_ref, xh_ref[...], cw_ref[...], cb_ref[...], wa_ref, wx_ref,
                                                ba_ref[...], bx_ref[...], lam_ref[...])
        xh_ref[...] = x_ref[tm - 8:tm, :]
        a_ref[...] = a
        b_ref[...] = mult * (ig * xc)
        rid = lax.broadcasted_iota(jnp.int32, (8, W_GRP), 0)

        def group(gi, hprev):
            base = pl.multiple_of(gi * 8, 8)
            ca = a_ref[pl.ds(base, 8), :]
            cb = b_ref[pl.ds(base, 8), :]
            for k in (1, 2, 4):
                m = rid >= k
                cb = jnp.where(m, ca * pltpu.roll(cb, k, 0) + cb, cb)
                ca = jnp.where(m, ca * pltpu.roll(ca, k, 0), ca)
            hh = cb + ca * hprev
            hs_ref[pl.ds(base, 8), :] = hh
            return hh[7:8, :]

        hlast = lax.fori_loop(0, tm // 8, group, hc_ref[0:1, :])
        hc_ref[...] = jnp.broadcast_to(hlast, hc_ref.shape)
        o_ref[...] = hs_ref[...] * _gelu(gb_ref[...])

    sq = _full((W_GRP, W_GRP))
    vec = _full((1, W_GRP))
    return pl.pallas_call(
        body, grid=(t // tm,),
        in_specs=[_zblk(tm, 2), _zblk(tm, 3), _full((4, W_GRP)), vec, sq, sq, vec, vec, vec, pl.BlockSpec(memory_space=pl.ANY)],
        out_specs=[_zblk(tm, 1), pl.BlockSpec((tm, W_GRP), lambda i: (i, 0))],
        out_shape=[_sds(mix.shape, F32), _sds((t, W_GRP), F32)],
        scratch_shapes=[pltpu.VMEM((tm + 8, W_GRP), F32), pltpu.VMEM((tm, W_GRP), F32), pltpu.VMEM((tm, W_GRP), F32),
                        pltpu.VMEM((8, W_GRP), F32), pltpu.VMEM((8, W_GRP), F32)],
        input_output_aliases={9: 0}, name="mix_b_fwd", compiler_params=_cp("arbitrary"),
    )(z, z, conv_w, conv_b, wa, wx, ba, bx, lam, mix)


def _mix_b_bwd(z, dz, dmix, hs, conv_w, conv_b, wa, wx, wa_t, wx_t, ba, bx, lam):
    t = z.shape[0]
    tm = _mix_tm(t)
    nt = t // tm
    hb = tm // 8

    def body(x_ref, gb_ref, xhalo_ref, hs_ref, hhalo_ref, dy_ref, cw_ref, cb_ref, wa_ref, wx_ref, wat_ref, wxt_ref,
             ba_ref, bx_ref, lam_ref, dz_in, dz_ref, dcw_ref, dcb_ref, dwa_ref, dwx_ref, dba_ref, dbx_ref, dlam_ref,
             ext_ref, c_ref, d_ref, g_ref, an_ref, gn_ref, dxn_ref):
        i = pl.program_id(0)
        first_tile = i == nt - 1

        @pl.when(i == 0)
        def _():
            for ref in (dcw_ref, dcb_ref, dwa_ref, dwx_ref, dba_ref, dbx_ref, dlam_ref, an_ref, gn_ref, dxn_ref):
                ref[...] = jnp.zeros_like(ref)

        cw, lam = cw_ref[...], lam_ref[...]
        xhalo = jnp.where(first_tile, 0.0, xhalo_ref[...])
        (x0, x1, x2, x3), xc, r, ig, sp, a, mult = _rglru_gates(
            ext_ref, x_ref, xhalo, cw, cb_ref[...], wa_ref, wx_ref, ba_ref[...], bx_ref[...], lam)
        hs = hs_ref[...]
        dy = dy_ref[...]
        gel, dgel = _gelu_and_grad(gb_ref[...])
        dz_ref[:, W_GRP:2 * W_GRP] = dy * hs * dgel

        ext_ref[0:tm, :] = a
        ext_ref[tm:tm + 8, :] = an_ref[...]
        an_ref[...] = a[0:8, :]
        c_ref[...] = _shift_up(ext_ref[...], 1, tm)
        d_ref[...] = dy * gel
        rid = lax.broadcasted_iota(jnp.int32, (8, W_GRP), 0)

        def group(j, gnext):
            base = pl.multiple_of((tm // 8 - 1 - j) * 8, 8)
            cc = c_ref[pl.ds(base, 8), :]
            cd = d_ref[pl.ds(base, 8), :]
            for k in (1, 2, 4):
                m = rid < 8 - k
                cd = jnp.where(m, cc * pltpu.roll(cd, 8 - k, 0) + cd, cd)
                cc = jnp.where(m, cc * pltpu.roll(cc, 8 - k, 0), cc)
            gg = cd + cc * gnext
            g_ref[pl.ds(base, 8), :] = gg
            return gg[0:1, :]

        gfirst = lax.fori_loop(0, tm // 8, group, gn_ref[0:1, :])
        gn_ref[...] = jnp.broadcast_to(gfirst, gn_ref.shape)
        g = g_ref[...]

        ext_ref[0:8, :] = jnp.where(first_tile, 0.0, hhalo_ref[...])
        ext_ref[8:, :] = hs
        hprev = _shift_down(ext_ref[...], 1, 8)
        da = g * hprev
        dmult = g * (ig * xc)
        di = g * mult * xc
        dxc = g * mult * ig
        dla = da * a - dmult * a * a / mult
        dr = dla * (-RGLRU_C * sp)
        dlam_ref[...] += jnp.sum(dla * (-RGLRU_C * r), axis=0, keepdims=True) * (-_sigmoid(-lam))
        dpr = dr * r * (1.0 - r)
        dpi = di * ig * (1.0 - ig)
        dprb, dpib, xcb = dpr.astype(BF16), dpi.astype(BF16), xc.astype(BF16)
        dba_ref[...] += jnp.sum(dpr, axis=0, keepdims=True)
        dbx_ref[...] += jnp.sum(dpi, axis=0, keepdims=True)
        dwa_ref[...] += _dot_tn(xcb, dprb)
        dwx_ref[...] += _dot_tn(xcb, dpib)
        dxc = dxc + _dot(dprb, wat_ref[...]) + _dot(dpib, wxt_ref[...])
        dcb_ref[...] += jnp.sum(dxc, axis=0, keepdims=True)
        dcw_ref[3:4, :] += jnp.sum(dxc * x0, axis=0, keepdims=True)
        dcw_ref[2:3, :] += jnp.sum(dxc * x1, axis=0, keepdims=True)
        dcw_ref[1:2, :] += jnp.sum(dxc * x2, axis=0, keepdims=True)
        dcw_ref[0:1, :] += jnp.sum(dxc * x3, axis=0, keepdims=True)
        ext_ref[0:tm, :] = dxc
        ext_ref[tm:tm + 8, :] = dxn_ref[...]
        dxn_ref[...] = dxc[0:8, :]
        ext = ext_ref[...]
        dz_ref[:, 0:W_GRP] = (cw[3:4, :] * dxc + cw[2:3, :] * _shift_up(ext, 1, tm) + cw[1:2, :] * _shift_up(ext, 2, tm)
                              + cw[0:1, :] * _shift_up(ext, 3, tm))

    sq = _full((W_GRP, W_GRP))
    vec = _full((1, W_GRP))
    halo = lambda col: pl.BlockSpec((8, W_GRP), lambda i: (jnp.maximum((nt - 1 - i) * hb - 1, 0), col))
    rev = lambda col: _zblk(tm, col, nt)
    return pl.pallas_call(
        body, grid=(nt,),
        in_specs=[rev(2), rev(3), halo(2), rev(0), halo(0), rev(1), _full((4, W_GRP)), vec, sq, sq, sq, sq, vec, vec, vec,
                  pl.BlockSpec(memory_space=pl.ANY)],
        out_specs=[pl.BlockSpec((tm, 2 * W_GRP), lambda i: (nt - 1 - i, 1)), _full((4, W_GRP)), vec, sq, sq, vec, vec, vec],
        out_shape=[_sds(dz.shape, F32), _sds((4, W_GRP), F32), _sds((1, W_GRP), F32), _sds((W_GRP, W_GRP), F32),
                   _sds((W_GRP, W_GRP), F32), _sds((1, W_GRP), F32), _sds((1, W_GRP), F32), _sds((1, W_GRP), F32)],
        scratch_shapes=[pltpu.VMEM((tm + 8, W_GRP), F32), pltpu.VMEM((tm, W_GRP), F32), pltpu.VMEM((tm, W_GRP), F32),
                        pltpu.VMEM((tm, W_GRP), F32), pltpu.VMEM((8, W_GRP), F32), pltpu.VMEM((8, W_GRP), F32),
                        pltpu.VMEM((8, W_GRP), F32)],
        input_output_aliases={15: 0}, name="mix_b_bwd", compiler_params=_cp("arbitrary"),
    )(z, z, z, hs, hs, dmix, conv_w, conv_b, wa, wx, wa_t, wx_t, ba, bx, lam, dz)


def _tri(n, lower):
    r = lax.broadcasted_iota(jnp.int32, (n, n), 0)
    c = lax.broadcasted_iota(jnp.int32, (n, n), 1)
    return jnp.where((r >= c) if lower else (r <= c), 1.0, 0.0).astype(BF16)


def _causal_stack():
    r = lax.broadcasted_iota(jnp.int32, (N_HEADS * HGRN_CHUNK, HGRN_CHUNK), 0)
    c = lax.broadcasted_iota(jnp.int32, (N_HEADS * HGRN_CHUNK, HGRN_CHUNK), 1)
    m = None
    for h in range(N_HEADS):
        mh = (r >= h * HGRN_CHUNK) & (r < (h + 1) * HGRN_CHUNK) & (r - h * HGRN_CHUNK >= c)
        m = mh if m is None else (m | mh)
    return m


def _stack_heads(x, hm):
    return jnp.concatenate([jnp.where(hm[h], x, 0.0) for h in range(N_HEADS)], axis=0)


def _unstack_heads(xs, hm):
    out = jnp.where(hm[0], xs[0:HGRN_CHUNK], 0.0)
    for h in range(1, N_HEADS):
        out = out + jnp.where(hm[h], xs[h * HGRN_CHUNK:(h + 1) * HGRN_CHUNK], 0.0)
    return out


def _hgrn_chunk(qv, fv, lb, tril):
    sq = _sigmoid(qv)
    qq = qv * sq
    sg = _sigmoid(fv)
    fg = lb + (1.0 - lb) * sg
    kk = 1.0 - fg
    bb = _dot_f32_lhs_exact(tril, jnp.log(fg))
    b_last = bb[HGRN_CHUNK - 1:HGRN_CHUNK, :]
    b_mid = bb[HGRN_CHUNK // 2 - 1:HGRN_CHUNK // 2, :]
    eq = jnp.exp(jnp.minimum(bb - b_mid, EXP_CLAMP))
    ek = jnp.exp(jnp.minimum(b_mid - bb, EXP_CLAMP))
    eb = jnp.exp(bb)
    el = jnp.exp(b_last - bb)
    return sq, qq, sg, fg, kk, b_last, eq, ek, eb, el


def _seg_mean(x, avg):
    return _dot_f32_rhs_exact(x, avg)


def _mix_c_fwd(z, mix, lb, ng):
    t = z.shape[0]
    tm = _mix_tm(t)
    nch = tm // HGRN_CHUNK

    def body(q_ref, f_ref, i_ref, g_ref, lb_ref, ng_ref, mix_in, y_ref, o_ref, ss_ref, s_ref):
        @pl.when(pl.program_id(0) == 0)
        def _():
            s_ref[...] = jnp.zeros_like(s_ref)

        hm = _head_masks()
        bmask = _block_mask()
        causal = _causal_stack()
        tril = _tri(HGRN_CHUNK, True)
        avg = jnp.where(bmask, 1.0 / HEAD_DIM, 0.0).astype(BF16)
        lb, ng = lb_ref[...], ng_ref[...]

        def chunk(c, carry):
            rows = pl.ds(pl.multiple_of(c * HGRN_CHUNK, HGRN_CHUNK), HGRN_CHUNK)
            vv = i_ref[rows, :]
            gv = g_ref[rows, :]
            _, qq, _, _, kk, b_last, eq, ek, eb, el = _hgrn_chunk(q_ref[rows, :], f_ref[rows, :], lb, tril)
            vb = vv.astype(BF16)
            qs = _stack_heads(qq * eq, hm).astype(BF16)
            att = jnp.where(causal, _dot_nt(qs, (kk * ek).astype(BF16)), 0.0)
            o = _unstack_heads(_dot(att.astype(BF16), vb), hm)
            s0 = s_ref[...]
            ss_ref[c] = s0
            o = o + _dot_nt((qq * eb).astype(BF16), s0.astype(BF16))
            s_ref[...] = s0 * jnp.exp(b_last) + jnp.where(bmask, _dot_tn(vb, (kk * el).astype(BF16)), 0.0)
            o_ref[rows, :] = o
            rstd = lax.rsqrt(_seg_mean(o * o, avg) + EPS)
            y_ref[rows, :] = o * rstd * ng * (gv * _sigmoid(gv))
            return carry

        lax.fori_loop(0, nch, chunk, 0)

    vec = _full((1, W_GRP))
    return pl.pallas_call(
        body, grid=(t // tm,),
        in_specs=[_zblk(tm, 4), _zblk(tm, 5), _zblk(tm, 6), _zblk(tm, 7), vec, vec, pl.BlockSpec(memory_space=pl.ANY)],
        out_specs=[_zblk(tm, 2), pl.BlockSpec((tm, W_GRP), lambda i: (i, 0)),
                   pl.BlockSpec((nch, W_GRP, W_GRP), lambda i: (i, 0, 0))],
        out_shape=[_sds(mix.shape, F32), _sds((t, W_GRP), F32), _sds((t // HGRN_CHUNK, W_GRP, W_GRP), F32)],
        scratch_shapes=[pltpu.VMEM((W_GRP, W_GRP), F32)],
        input_output_aliases={6: 0}, name="mix_c_fwd", compiler_params=_cp("arbitrary"),
    )(z, z, z, z, lb, ng, mix)


def _mix_c_bwd(z, dz, dmix, o_pre, states, lb, ng):
    t = z.shape[0]
    tm = _mix_tm(t)
    nt = t // tm
    nch = tm // HGRN_CHUNK

    def body(q_ref, f_ref, i_ref, g_ref, o_ref, ss_ref, dy_ref, lb_ref, ng_ref, dz_in, dz_ref, dlb_ref, dng_ref, ds_ref):
        @pl.when(pl.program_id(0) == 0)
        def _():
            ds_ref[...] = jnp.zeros_like(ds_ref)
            dlb_ref[...] = jnp.zeros_like(dlb_ref)
            dng_ref[...] = jnp.zeros_like(dng_ref)

        hm = _head_masks()
        bmask = _block_mask()
        causal = _causal_stack()
        tril = _tri(HGRN_CHUNK, True)
        triu = _tri(HGRN_CHUNK, False)
        avg = jnp.where(bmask, 1.0 / HEAD_DIM, 0.0).astype(BF16)
        lb, ng = lb_ref[...], ng_ref[...]
        last_row = lax.broadcasted_iota(jnp.int32, (HGRN_CHUNK, W_GRP), 0) == HGRN_CHUNK - 1

        def chunk(j, carry):
            c = nch - 1 - j
            rows = pl.ds(pl.multiple_of(c * HGRN_CHUNK, HGRN_CHUNK), HGRN_CHUNK)
            qv, gv, vv = q_ref[rows, :], g_ref[rows, :], i_ref[rows, :]
            sq, qq, sg, fg, kk, b_last, eq, ek, eb, el = _hgrn_chunk(qv, f_ref[rows, :], lb, tril)
            s0 = ss_ref[c]
            ds1 = ds_ref[...]
            o = o_ref[rows, :]
            dy = dy_ref[rows, :]
            rstd = lax.rsqrt(_seg_mean(o * o, avg) + EPS)
            oh = o * rstd
            sgg = _sigmoid(gv)
            dz_ref[rows, 3 * W_GRP:4 * W_GRP] = dy * oh * ng * (sgg * (1.0 + gv * (1.0 - sgg)))
            don = dy * gv * sgg
            dng_ref[...] += jnp.sum(don * oh, axis=0, keepdims=True)
            doh = don * ng
            do = rstd * (doh - oh * _seg_mean(doh * oh, avg))
            qt, kt, qh, kh = qq * eq, kk * ek, qq * eb, kk * el
            vb, dob = vv.astype(BF16), do.astype(BF16)
            ktb, khb = kt.astype(BF16), kh.astype(BF16)
            ds1b = ds1.astype(BF16)
            qs = _stack_heads(qt, hm).astype(BF16)
            dos = _stack_heads(do, hm).astype(BF16)
            att = jnp.where(causal, _dot_nt(qs, ktb), 0.0).astype(BF16)
            datt = jnp.where(causal, _dot_nt(dos, vb), 0.0).astype(BF16)
            dv = _dot_tn(att, dos) + _dot_nt(khb, ds1b)
            dqt = _unstack_heads(_dot(datt, ktb), hm)
            dkt = _dot_tn(datt, qs)
            dqh = _dot(dob, s0.astype(BF16))
            dkh = _dot(vb, ds1b)
            e_last = jnp.exp(b_last)
            ds_ref[...] = ds1 * e_last + jnp.where(bmask, _dot_tn(dob, qh.astype(BF16)), 0.0)
            dq = dqt * eq + dqh * eb
            dk = dkt * ek + dkh * el
            db = qt * dqt - kt * dkt + qh * dqh - kh * dkh
            db_last = jnp.sum(kh * dkh, axis=0, keepdims=True) + e_last * jnp.sum(ds1 * s0, axis=0, keepdims=True)
            db = db + jnp.where(last_row, db_last, 0.0)
            dlogf = _dot_f32_lhs_exact(triu, db)
            dfg = dlogf / fg - dk
            dz_ref[rows, W_GRP:2 * W_GRP] = dfg * (1.0 - lb) * sg * (1.0 - sg)
            dlb_ref[...] += jnp.sum(dfg * (1.0 - sg), axis=0, keepdims=True)
            dz_ref[rows, 0:W_GRP] = dq * (sq * (1.0 + qv * (1.0 - sq)))
            dz_ref[rows, 2 * W_GRP:3 * W_GRP] = dv
            return carry

        lax.fori_loop(0, nch, chunk, 0)

    vec = _full((1, W_GRP))
    rev = lambda col: _zblk(tm, col, nt)
    return pl.pallas_call(
        body, grid=(nt,),
        in_specs=[rev(4), rev(5), rev(6), rev(7), rev(0), pl.BlockSpec((nch, W_GRP, W_GRP), lambda i: (nt - 1 - i, 0, 0)),
                  rev(2), vec, vec, pl.BlockSpec(memory_space=pl.ANY)],
        out_specs=[pl.BlockSpec((tm, 4 * W_GRP), lambda i: (nt - 1 - i, 1)), vec, vec],
        out_shape=[_sds(dz.shape, F32), _sds((1, W_GRP), F32), _sds((1, W_GRP), F32)],
        scratch_shapes=[pltpu.VMEM((W_GRP, W_GRP), F32)],
        input_output_aliases={9: 0}, name="mix_c_bwd", compiler_params=_cp("arbitrary"),
    )(z, z, z, z, o_pre, states, dmix, lb, ng, dz)


def _pool_select(hm, s2, s4, s8, s16):
    return jnp.where(hm[0], s2, jnp.where(hm[1], s4, jnp.where(hm[2], s8, s16)))


def _pool_counts(hm, row0, tm):
    pos = (row0 + 1 + lax.broadcasted_iota(jnp.int32, (tm, W_GRP), 0)).astype(F32)
    win = _pool_select(hm, 2.0, 4.0, 8.0, 16.0)
    return jnp.minimum(pos, win)


def _pooled(ext_ref, x, halo, hm, cnt):
    ext_ref[0:POOL_HALO, :] = halo
    ext_ref[POOL_HALO:, :] = x
    e = ext_ref[...]
    s2 = e + pltpu.roll(e, 1, 0)
    s4 = s2 + pltpu.roll(s2, 2, 0)
    s8 = s4 + pltpu.roll(s4, 4, 0)
    s16 = s8 + pltpu.roll(s8, 8, 0)
    return _pool_select(hm, s2, s4, s8, s16)[POOL_HALO:] / cnt - x


def _mix_d_fwd(z, mix, wd, scale):
    t = z.shape[0]
    tm = _mix_tm(t)

    def body(x_ref, wd_ref, sc_ref, mix_in, o_ref, ext_ref, halo_ref):
        i = pl.program_id(0)

        @pl.when(i == 0)
        def _():
            halo_ref[...] = jnp.zeros_like(halo_ref)

        hm = _head_masks()
        x = x_ref[...]
        pooled = _pooled(ext_ref, x, halo_ref[...], hm, _pool_counts(hm, i * tm, tm))
        halo_ref[...] = x_ref[tm - POOL_HALO:tm, :]
        o_ref[...] = _dot(pooled.astype(BF16), wd_ref[...]) * sc_ref[...]

    return pl.pallas_call(
        body, grid=(t // tm,),
        in_specs=[_zblk(tm, 8), _full((W_GRP, W_GRP)), _full((1, W_GRP)), pl.BlockSpec(memory_space=pl.ANY)],
        out_specs=_zblk(tm, 3), out_shape=_sds(mix.shape, F32),
        scratch_shapes=[pltpu.VMEM((tm + POOL_HALO, W_GRP), F32), pltpu.VMEM((POOL_HALO, W_GRP), F32)],
        input_output_aliases={3: 0}, name="mix_d_fwd", compiler_params=_cp("arbitrary"),
    )(z, wd, scale, mix)


def _mix_d_bwd(z, dz, dmix, wd, wd_t, scale):
    t = z.shape[0]
    tm = _mix_tm(t)
    nt = t // tm
    hb = tm // POOL_HALO

    def body(x_ref, xhalo_ref, dy_ref, wd_ref, wdt_ref, sc_ref, dz_in, dz_ref, dwd_ref, dsc_ref, ext_ref, en_ref):
        i = pl.program_id(0)
        ri = nt - 1 - i

        @pl.when(i == 0)
        def _():
            en_ref[...] = jnp.zeros_like(en_ref)
            dwd_ref[...] = jnp.zeros_like(dwd_ref)
            dsc_ref[...] = jnp.zeros_like(dsc_ref)

        hm = _head_masks()
        cnt = _pool_counts(hm, ri * tm, tm)
        x = x_ref[...]
        pooled = _pooled(ext_ref, x, jnp.where(ri == 0, 0.0, xhalo_ref[...]), hm, cnt)
        pb = pooled.astype(BF16)
        dy = dy_ref[...]
        dsc_ref[...] += jnp.sum(dy * _dot(pb, wd_ref[...]), axis=0, keepdims=True)
        dyw = (dy * sc_ref[...]).astype(BF16)
        dwd_ref[...] += _dot_tn(pb, dyw)
        dpool = _dot(dyw, wdt_ref[...])
        e = dpool / cnt
        ext_ref[0:tm, :] = e
        ext_ref[tm:, :] = en_ref[...]
        en_ref[...] = e[0:POOL_HALO, :]
        ee = ext_ref[...]
        n = tm + POOL_HALO
        r2 = ee + pltpu.roll(ee, n - 1, 0)
        r4 = r2 + pltpu.roll(r2, n - 2, 0)
        r8 = r4 + pltpu.roll(r4, n - 4, 0)
        r16 = r8 + pltpu.roll(r8, n - 8, 0)
        dz_ref[...] = _pool_select(hm, r2, r4, r8, r16)[:tm] - dpool

    sq = _full((W_GRP, W_GRP))
    vec = _full((1, W_GRP))
    return pl.pallas_call(
        body, grid=(nt,),
        in_specs=[_zblk(tm, 8, nt), pl.BlockSpec((POOL_HALO, W_GRP), lambda i: (jnp.maximum((nt - 1 - i) * hb - 1, 0), 8)),
                  _zblk(tm, 3, nt), sq, sq, vec, pl.BlockSpec(memory_space=pl.ANY)],
        out_specs=[_zblk(tm, 8, nt), sq, vec],
        out_shape=[_sds(dz.shape, F32), _sds((W_GRP, W_GRP), F32), _sds((1, W_GRP), F32)],
        scratch_shapes=[pltpu.VMEM((tm + POOL_HALO, W_GRP), F32), pltpu.VMEM((POOL_HALO, W_GRP), F32)],
        input_output_aliases={6: 0}, name="mix_d_bwd", compiler_params=_cp("arbitrary"),
    )(z, z, dmix, wd, wd_t, scale, dz)


def _as2d(a):
    if a.ndim == 1:
        return a.reshape(1, a.shape[0])
    return a.reshape(-1, a.shape[-1])


def _adamw(w, g, m, v, name):
    shape = w.shape
    w2, g2, m2, v2 = _as2d(w), _as2d(g), _as2d(m), _as2d(v)
    rows, cols = w2.shape
    tr = _pick(rows, (1024, 512, 256, 128, 64, 32, 16, 8))
    if tr * cols * 4 * 14 > VMEM_LIMIT_BYTES:
        tr = _pick(rows, (256, 128, 64, 32, 16, 8))

    def body(w_ref, g_ref, m_ref, v_ref, d_ref, nm_ref, nv_ref):
        gv = g_ref[...]
        mn = ADAM_B1 * m_ref[...] + (1.0 - ADAM_B1) * gv
        vn = ADAM_B2 * v_ref[...] + (1.0 - ADAM_B2) * (gv * gv)
        m_hat = mn / (1.0 - ADAM_B1 ** ADAM_STEP)
        v_hat = vn / (1.0 - ADAM_B2 ** ADAM_STEP)
        d_ref[...] = -ADAM_LR * (m_hat / (jnp.sqrt(v_hat) + ADAM_EPS) + ADAM_WD * w_ref[...])
        nm_ref[...] = mn
        nv_ref[...] = vn

    blk = pl.BlockSpec((tr, cols), lambda i: (i, 0))
    outs = pl.pallas_call(
        body, grid=(rows // tr,), in_specs=[blk] * 4, out_specs=[blk] * 3, out_shape=[_sds((rows, cols), F32)] * 3,
        name=name, compiler_params=_cp("parallel"),
    )(w2, g2, m2, v2)
    return tuple(o.reshape(shape) for o in outs)


def _adamw_layer(w, g, m, v, layer, bufs, name):
    nl, r, cs = w.shape
    tr = _pick(r, (256, 128, 64, 32, 16, 8))

    def body(w_ref, g_ref, m_ref, v_ref, *rest):
        go_ref, d_ref, nm_ref, nv_ref = rest[-4:]
        gv = g_ref[...]
        mn = ADAM_B1 * m_ref[...] + (1.0 - ADAM_B1) * gv
        vn = ADAM_B2 * v_ref[...] + (1.0 - ADAM_B2) * (gv * gv)
        m_hat = mn / (1.0 - ADAM_B1 ** ADAM_STEP)
        v_hat = vn / (1.0 - ADAM_B2 ** ADAM_STEP)
        go_ref[...] = gv
        d_ref[...] = -ADAM_LR * (m_hat / (jnp.sqrt(v_hat) + ADAM_EPS) + ADAM_WD * w_ref[...])
        nm_ref[...] = mn
        nv_ref[...] = vn

    lay = pl.BlockSpec((None, tr, cs), lambda i: (layer, i, 0))
    in_specs = [lay, pl.BlockSpec((tr, cs), lambda i: (i, 0)), lay, lay]
    args = [w, g, m, v]
    aliases = {}
    if bufs is not None:
        in_specs += [pl.BlockSpec(memory_space=pl.ANY)] * 4
        args += list(bufs)
        aliases = {4 + i: i for i in range(4)}
    return pl.pallas_call(
        body, grid=(r // tr,), in_specs=in_specs, out_specs=[lay] * 4, out_shape=[_sds((nl, r, cs), F32)] * 4,
        input_output_aliases=aliases, name=name, compiler_params=_cp("parallel"),
    )(*args)


def _slot_sum(own, slots, name):
    n_slots, rows, cols = slots.shape
    whole_fits = rows * cols * 4 * (n_slots + 2) * 2 <= VMEM_LIMIT_BYTES // 2
    tr = rows if whole_fits else _pick(rows, (512, 352, 256, 128, 64, 32, 16, 8))

    def body(*refs):
        s_ref, o_ref = refs[-2], refs[-1]
        acc = s_ref[0].astype(F32) if own is None else refs[0][...].astype(F32) + s_ref[0].astype(F32)
        for k in range(1, n_slots):
            acc = acc + s_ref[k].astype(F32)
        o_ref[...] = acc

    row = pl.BlockSpec((tr, cols), lambda i: (i, 0))
    return pl.pallas_call(
        body, grid=(rows // tr,),
        in_specs=([] if own is None else [row]) + [pl.BlockSpec((n_slots, tr, cols), lambda i: (0, i, 0))],
        out_specs=row, out_shape=_sds((rows, cols), F32), name=name, compiler_params=_cp("parallel"),
    )(*(() if own is None else (own,)), slots)


def _me():
    return lax.axis_index("x"), lax.axis_index("y"), lax.axis_index("c")


def _other_chips(x, y):
    return [(1 - x, y), (x, 1 - y), (1 - x, 1 - y)]


ANY_SPEC = pl.BlockSpec(memory_space=pl.ANY)
HBM_SPEC = pl.BlockSpec(memory_space=pltpu.HBM)
SEM_SPEC = pl.BlockSpec(memory_space=pltpu.SEMAPHORE)
SPLIT_COPY_PARAMS = pltpu.CompilerParams(has_side_effects=pltpu.SideEffectType.DATAFLOW_SIDE_EFFECTING)
N_CHIPS = 4


def _aligned(v, m):
    return v if isinstance(v, int) else pl.multiple_of(v, m)


def _in_hbm(arr):
    return pltpu.with_memory_space_constraint(arr, pltpu.HBM)


def _peer(x, y, c, k):
    fx, fy, fc = (k >> 2) & 1, (k >> 1) & 1, k & 1
    px = 1 - x if fx else x
    py = 1 - y if fy else y
    pc = 1 - c if fc else c
    return px, py, pc


def _gather_start(shards, after, name):
    n = len(shards)

    def body(*refs):
        src, land = refs[:n], refs[n:2 * n]
        send_sems, recv_sems = refs[2 * n + 1], refs[2 * n + 2]
        token = refs[-1]
        x, y, c = _me()
        for w in range(n):
            for chip in _other_chips(x, y):
                pltpu.make_async_remote_copy(
                    src_ref=src[w], dst_ref=land[w].at[2 * x + y], send_sem=send_sems.at[w], recv_sem=recv_sems.at[w],
                    device_id=(*chip, c), device_id_type=MESH_ID).start()
        token[...] = jnp.zeros_like(token)

    lands = [lax.empty((N_CHIPS,) + s.shape, s.dtype) for s in shards]
    thru = [pltpu.HBM(s.shape, s.dtype) for s in shards] + [pltpu.HBM(z.shape, z.dtype) for z in lands]
    outs = pl.pallas_call(
        body, name=name,
        out_shape=(pltpu.SemaphoreType.DMA((n,)), pltpu.SemaphoreType.DMA((n,)), *thru, _sds((8, 128), F32)),
        in_specs=[HBM_SPEC] * (2 * n) + [ANY_SPEC],
        out_specs=(SEM_SPEC, SEM_SPEC, *[HBM_SPEC] * (2 * n), pl.BlockSpec(memory_space=pltpu.VMEM)),
        input_output_aliases={i: 2 + i for i in range(2 * n)}, compiler_params=SPLIT_COPY_PARAMS,
    )(*[_in_hbm(s) for s in shards], *[_in_hbm(z) for z in lands], after)
    return (outs[0], outs[1], outs[2:2 + n], outs[2 + n:2 + 2 * n]), outs[-1]


def _gather_wait(send_sems, recv_sems, srcs, lands, after, name):
    n = len(srcs)

    def body(*refs):
        land = refs[n:2 * n]
        send_sems, recv_sems = refs[2 * n], refs[2 * n + 1]
        x, y, c = _me()
        for w in range(n):
            three = land[w].at[pl.ds(0, N_CHIPS - 1)]
            cp = pltpu.make_async_remote_copy(src_ref=three, dst_ref=three, send_sem=send_sems.at[w], recv_sem=recv_sems.at[w],
                                              device_id=(x, y, c), device_id_type=MESH_ID)
            cp.wait_send()
            cp.wait_recv()

    both = list(srcs) + list(lands)
    outs = pl.pallas_call(
        body, name=name, out_shape=tuple(pltpu.HBM(b.shape, b.dtype) for b in both),
        in_specs=[HBM_SPEC] * (2 * n) + [SEM_SPEC, SEM_SPEC, ANY_SPEC], out_specs=[HBM_SPEC] * (2 * n),
        input_output_aliases={i: i for i in range(2 * n)}, compiler_params=SPLIT_COPY_PARAMS,
    )(*both, send_sems, recv_sems, after)
    return outs[n:2 * n]


def _push_start(grads, small, name):
    n = len(grads)
    srcs = list(grads) + ([] if small is None else [small])
    ns = len(srcs)

    def body(*refs):
        src, slots = refs[:ns], refs[ns:2 * ns]
        send_sems, recv_sems = refs[2 * ns], refs[2 * ns + 1]
        token = refs[-1]
        x, y, c = _me()
        for w in range(ns):
            for k in range(1, N_DEV):
                px, py, pc = _peer(x, y, c, k)
                if w < n:
                    hr = src[w].shape[1] // 2
                    piece = src[w].at[2 * px + py, pl.ds(_aligned(pc * hr, 16), hr), :]
                    slot = slots[w].at[k - 1]
                else:
                    piece = src[w]
                    slot = slots[w].at[4 * x + 2 * y + c]
                pltpu.make_async_remote_copy(
                    src_ref=piece, dst_ref=slot, send_sem=send_sems.at[w], recv_sem=recv_sems.at[w],
                    device_id=(px, py, pc), device_id_type=MESH_ID).start()
        token[...] = jnp.zeros_like(token)

    slots = [lax.empty((N_DEV - 1, g.shape[1] // 2, g.shape[2]), g.dtype) for g in grads]
    if small is not None:
        slots.append(lax.empty((N_DEV,) + small.shape, small.dtype))
    both = srcs + slots
    outs = pl.pallas_call(
        body, name=name,
        out_shape=(pltpu.SemaphoreType.DMA((ns,)), pltpu.SemaphoreType.DMA((ns,)),
                   *[pltpu.HBM(b.shape, b.dtype) for b in both], _sds((8, 128), F32)),
        in_specs=[HBM_SPEC] * len(both),
        out_specs=(SEM_SPEC, SEM_SPEC, *[HBM_SPEC] * len(both), pl.BlockSpec(memory_space=pltpu.VMEM)),
        input_output_aliases={i: 2 + i for i in range(len(both))}, compiler_params=SPLIT_COPY_PARAMS,
    )(*[_in_hbm(b) for b in both])
    return (outs[0], outs[1], outs[2:2 + ns], outs[2 + ns:2 + 2 * ns]), outs[-1]


def _push_wait(send_sems, recv_sems, srcs, slots, after, name):
    n = len(srcs)

    def body(*refs):
        slot = refs[n:2 * n]
        send_sems, recv_sems = refs[2 * n], refs[2 * n + 1]
        x, y, c = _me()
        for w in range(n):
            seven = slot[w].at[pl.ds(0, N_DEV - 1)]
            cp = pltpu.make_async_remote_copy(src_ref=seven, dst_ref=seven, send_sem=send_sems.at[w],
                                              recv_sem=recv_sems.at[w], device_id=(x, y, c), device_id_type=MESH_ID)
            cp.wait_send()
            cp.wait_recv()

    both = list(srcs) + list(slots)
    outs = pl.pallas_call(
        body, name=name, out_shape=tuple(pltpu.HBM(b.shape, b.dtype) for b in both),
        in_specs=[HBM_SPEC] * (2 * n) + [SEM_SPEC, SEM_SPEC, ANY_SPEC], out_specs=[HBM_SPEC] * (2 * n),
        input_output_aliases={i: i for i in range(2 * n)}, compiler_params=SPLIT_COPY_PARAMS,
    )(*both, send_sems, recv_sems, after)
    return outs[:n], outs[n:]


SWAP_CHUNK_BYTES = 2 * 1024 * 1024


def _swap_chunk_rows(hr, cs):
    ch = hr
    while ch * cs * 4 > SWAP_CHUNK_BYTES and ch % 16 == 0:
        ch //= 2
    return ch


def _swap_halves(halves, name):
    n = len(halves)
    chunk = [_swap_chunk_rows(*h.shape) for h in halves]
    rounds = max(h.shape[0] // ch for h, ch in zip(halves, chunk))

    def body(*refs):
        src, dst, buf = refs[:n], refs[n:2 * n], refs[2 * n:3 * n]
        load_sems, put_sems, send_sems, recv_sems = refs[3 * n:]
        x, y, c = _me()
        sibling = (x, y, 1 - c)
        for j in range(rounds):
            live = [w for w in range(n) if j < src[w].shape[0] // chunk[w]]
            loads = [pltpu.make_async_copy(src[w].at[pl.ds(j * chunk[w], chunk[w])], buf[w], load_sems.at[w]) for w in live]
            for ld in loads:
                ld.start()
            moves = []
            for ld, w in zip(loads, live):
                ld.wait()
                rows = pl.ds(_aligned(c * src[w].shape[0] + j * chunk[w], 8), chunk[w])
                put = pltpu.make_async_copy(buf[w], dst[w].at[rows], put_sems.at[w])
                send = pltpu.make_async_remote_copy(src_ref=buf[w], dst_ref=dst[w].at[rows], send_sem=send_sems.at[w],
                                                    recv_sem=recv_sems.at[w], device_id=sibling, device_id_type=MESH_ID)
                put.start()
                send.start()
                moves.append((put, send))
            for put, send in moves:
                put.wait()
                send.wait_send()
        for w in range(n):
            hr = src[w].shape[0]
            got = dst[w].at[pl.ds(_aligned((1 - c) * hr, 8), hr)]
            pltpu.make_async_remote_copy(src_ref=got, dst_ref=got, send_sem=send_sems.at[w], recv_sem=recv_sems.at[w],
                                         device_id=sibling, device_id_type=MESH_ID).wait_recv()

    return pl.pallas_call(
        body, in_specs=[ANY_SPEC] * n, out_specs=[ANY_SPEC] * n,
        out_shape=[_sds((2 * h.shape[0], h.shape[1]), F32) for h in halves],
        scratch_shapes=[pltpu.VMEM((ch, h.shape[1]), F32) for h, ch in zip(halves, chunk)]
        + [pltpu.SemaphoreType.DMA((n,))] * 4,
        name=name,
    )(*halves)


BIG = ("w_in", "w_out", "w_up", "w_down", "w_pe", "w_pg")
ROW_SHARDED = ("w_out", "w_down", "w_pg")
SMALL = ("norm1_g", "a_ln_g", "a_ln_b", "a_ws", "a_bs", "b_conv_w", "b_conv_b", "b_wa", "b_ba", "b_wx", "b_bx", "b_lam",
         "c_lb", "c_norm_g", "d_w", "d_scale", "norm2_g", "ffn_conv_w", "ffn_conv_b", "norm3_g", "final_g")
SMALL_SHARDED = ("b_conv_w", "ffn_conv_w")
WEIGHTS = ("norm1_g", "w_in", "a_ln_g", "a_ln_b", "a_ws", "a_bs", "b_conv_w", "b_conv_b", "b_wa", "b_ba", "b_wx", "b_bx",
           "b_lam", "c_lb", "c_norm_g", "d_w", "d_scale", "w_out", "norm2_g", "w_up", "ffn_conv_w", "ffn_conv_b", "w_down",
           "norm3_g", "w_pe", "w_pg", "final_g")
ARGS = ("x", "p") + WEIGHTS + ("loss_target",) + tuple("m_" + n for n in WEIGHTS) + tuple("v_" + n for n in WEIGHTS)


def _block_diag(w):
    eye = jnp.eye(N_HEADS, dtype=w.dtype)
    return (eye[:, None, :, None] * w[:, :, None, :]).reshape(W_GRP, W_GRP)


def _diag_blocks(m):
    m4 = m.reshape(N_HEADS, HEAD_DIM, N_HEADS, HEAD_DIM)
    return jnp.stack([m4[h, :, h, :] for h in range(N_HEADS)])


def _lower_bounds(c_lb):
    lbs = jnp.cumsum(jax.nn.softmax(c_lb, axis=0), axis=0)
    return lbs - lbs[0:1]


def kernel(x, p, norm1_g, w_in, a_ln_g, a_ln_b, a_ws, a_bs, b_conv_w, b_conv_b, b_wa, b_ba, b_wx, b_bx, b_lam, c_lb, c_norm_g, d_w, d_scale, w_out, norm2_g, w_up, ffn_conv_w, ffn_conv_b, w_down, norm3_g, w_pe, w_pg, final_g, loss_target, m_norm1_g, m_w_in, m_a_ln_g, m_a_ln_b, m_a_ws, m_a_bs, m_b_conv_w, m_b_conv_b, m_b_wa, m_b_ba, m_b_wx, m_b_bx, m_b_lam, m_c_lb, m_c_norm_g, m_d_w, m_d_scale, m_w_out, m_norm2_g, m_w_up, m_ffn_conv_w, m_ffn_conv_b, m_w_down, m_norm3_g, m_w_pe, m_w_pg, m_final_g, v_norm1_g, v_w_in, v_a_ln_g, v_a_ln_b, v_a_ws, v_a_bs, v_b_conv_w, v_b_conv_b, v_b_wa, v_b_ba, v_b_wx, v_b_bx, v_b_lam, v_c_lb, v_c_norm_g, v_d_w, v_d_scale, v_w_out, v_norm2_g, v_w_up, v_ffn_conv_w, v_ffn_conv_b, v_w_down, v_norm3_g, v_w_pe, v_w_pg, v_final_g):
    return _step((x, p, norm1_g, w_in, a_ln_g, a_ln_b, a_ws, a_bs, b_conv_w, b_conv_b, b_wa, b_ba, b_wx, b_bx, b_lam, c_lb, c_norm_g, d_w, d_scale, w_out, norm2_g, w_up, ffn_conv_w, ffn_conv_b, w_down, norm3_g, w_pe, w_pg, final_g, loss_target, m_norm1_g, m_w_in, m_a_ln_g, m_a_ln_b, m_a_ws, m_a_bs, m_b_conv_w, m_b_conv_b, m_b_wa, m_b_ba, m_b_wx, m_b_bx, m_b_lam, m_c_lb, m_c_norm_g, m_d_w, m_d_scale, m_w_out, m_norm2_g, m_w_up, m_ffn_conv_w, m_ffn_conv_b, m_w_down, m_norm3_g, m_w_pe, m_w_pg, m_final_g, v_norm1_g, v_w_in, v_a_ln_g, v_a_ln_b, v_a_ws, v_a_bs, v_b_conv_w, v_b_conv_b, v_b_wa, v_b_ba, v_b_wx, v_b_bx, v_b_lam, v_c_lb, v_c_norm_g, v_d_w, v_d_scale, v_w_out, v_norm2_g, v_w_up, v_ffn_conv_w, v_ffn_conv_b, v_w_down, v_norm3_g, v_w_pe, v_w_pg, v_final_g))


SMALL_PER_LAYER = tuple(n for n in SMALL if n != "final_g")
GATHERED = BIG + SMALL_SHARDED
GATHER_FIRST = ("w_in", "b_conv_w")
GATHER_REST = tuple(n for n in GATHERED if n not in GATHER_FIRST)
PUSH_EARLY = ("w_pe", "w_pg", "w_down", "w_up")
PUSH_LATE = ("w_out", "w_in")


def _cols_to_slabs(m):
    r, c4 = m.shape
    return jnp.moveaxis(m.reshape(r, N_CHIPS, c4 // N_CHIPS), 1, 0)


def _slabs_to_cols(s):
    return jnp.moveaxis(s, 0, 1).reshape(s.shape[1], -1)


def _pack_small(parts):
    flat = jnp.concatenate([p.reshape(-1) for p in parts])
    return jnp.pad(flat, (0, (-flat.shape[0]) % 1024)).reshape(-1, 128)


def _step(args):
    a = dict(zip(ARGS, args, strict=True))
    x0 = a["x"][0]
    target = a["loss_target"][0]
    nl = a["norm1_g"].shape[0]
    t, d = x0.shape
    f = a["w_down"].shape[1] * N_CHIPS
    cx, cy, cc = _me()
    my_shard = 2 * cx + cy
    shards = {n: a[n].astype(BF16) for n in BIG}
    shards.update({n: a[n] for n in SMALL_SHARDED})

    def start_gather(l, names, after, tag):
        return _gather_start([shards[n][l] for n in names], after, f"gather_start_{l}{tag}")

    def finish_gather(l, names, handle, after, tag):
        send, recv, srcs, lands = handle
        lands = _gather_wait(send, recv, srcs, lands, after, f"gather_wait_{l}{tag}")
        w = {}
        for n, land in zip(names, lands):
            full = lax.dynamic_update_slice(land, shards[n][l][None], (my_shard, 0, 0))
            if n in ROW_SHARDED:
                w[n] = full.reshape(-1, full.shape[-1])
            elif n in ("w_up", "w_pe"):
                w[n] = full
            else:
                w[n] = _slabs_to_cols(full)
        return w

    lbs, lbs_vjp = jax.vjp(_lower_bounds, a["c_lb"])
    tril = jnp.tril(jnp.ones((GMLP_CHUNK, GMLP_CHUNK), F32))

    def layer_params(l, w):
        q = {}
        q["wm"] = (a["a_ws"][l] * tril).astype(BF16)
        q["wm_t"] = jnp.swapaxes(q["wm"], 1, 2)
        q["bs_t"] = jnp.repeat(a["a_bs"][l].T, HEAD_DIM, axis=1)
        for nm in ("b_wa", "b_wx", "d_w"):
            bd = _block_diag(a[nm][l]).astype(BF16)
            q[nm], q[nm + "_t"] = bd, bd.T
        for nm in ("a_ln_g", "a_ln_b", "b_conv_b", "b_ba", "b_bx", "b_lam", "d_scale"):
            q[nm] = a[nm][l].reshape(1, W_GRP)
        q["lb"] = lbs[l].reshape(1, W_GRP)
        q["ng"] = jnp.tile(a["c_norm_g"][l], N_HEADS).reshape(1, W_GRP)
        q["ffn_conv_b"] = a["ffn_conv_b"][l].reshape(1, 2 * f)
        q.update(w)
        return q

    saved, weights, params = [], [], []
    first_handle, _ = start_gather(0, GATHER_FIRST, x0, "a")
    rest_handle, _ = start_gather(0, GATHER_REST, x0, "b")
    xl = x0
    for l in range(nl):
        if l == 0:
            w = finish_gather(0, GATHER_FIRST, first_handle, xl, "a")
        else:
            w = finish_gather(l, GATHERED, next_handle, xl, "")
        s = {"x0": xl}
        s["h1"] = _rms_fwd(xl, a["norm1_g"][l], "rms1_fwd")
        token = None
        if l + 1 < nl:
            next_handle, token = start_gather(l + 1, GATHERED, s["h1"], "")
        s["z"] = _mm(s["h1"], w["w_in"], "nn", out_dtype=F32, name="mm_z", after=token)
        q = layer_params(l, w)
        mix = _mix_a_fwd(s["z"], d, q["a_ln_g"], q["a_ln_b"], q["wm"], q["bs_t"])
        mix, s["hs"] = _mix_b_fwd(s["z"], mix, q["b_conv_w"], q["b_conv_b"], q["b_wa"], q["b_wx"], q["b_ba"], q["b_bx"],
                                  q["b_lam"])
        mix, s["o_pre"], s["states"] = _mix_c_fwd(s["z"], mix, q["lb"], q["ng"])
        s["mix"] = _mix_d_fwd(s["z"], mix, q["d_w"], q["d_scale"])
        if l == 0:
            w.update(finish_gather(0, GATHER_REST, rest_handle, s["mix"], "b"))
            q.update(w)
        s["x1"] = _mm(s["mix"], w["w_out"], "nn", res=xl, out_dtype=F32, name="mm_out")
        s["h2"] = _rms_fwd(s["x1"], a["norm2_g"][l], "rms2_fwd")
        s["hf_g"] = _mm(s["h2"], w["w_up"], "nn", b_slabs=True, n=f, out_dtype=F32, name="mm_up_g")
        s["hf_v"] = _mm(s["h2"], w["w_up"], "nn", b_slabs=True, n=f, b_noff=f, out_dtype=F32, name="mm_up_v")
        s["act"] = _ffn_act_fwd(s["hf_g"], s["hf_v"], q["ffn_conv_w"], q["ffn_conv_b"])
        s["x2"] = _mm(s["act"], w["w_down"], "nn", res=s["x1"], out_dtype=F32, name="mm_down")
        s["h3"] = _rms_fwd(s["x2"], a["norm3_g"][l], "rms3_fwd")
        s["pre"] = _mm(s["h3"], w["w_pg"], "nn", out_dtype=F32, name="mm_pg")
        s["pe"] = _mm(a["p"][l, 0], w["w_pe"], "nn", b_slabs=True, out_dtype=F32, name="mm_pe")
        xl = _ple_fwd(s["x2"], s["pe"], s["pre"])
        saved.append(s)
        weights.append(w)
        params.append(q)

    dx, g_final, loss = _final_loss(xl, a["final_g"], target)
    loss = lax.psum(loss[0, 0], ("x", "y", "c"))

    stacked = {n: None for n in BIG}
    small_sums = [None] * nl

    def finish_push(l, names, handle, tag, after):
        send, recv, srcs, slots = handle
        srcs, slots = _push_wait(send, recv, srcs, slots, after, f"push_wait_{l}{tag}")
        halves = []
        for n, g, sl in zip(names, srcs, slots):
            hr = g.shape[1] // 2
            own = lax.dynamic_slice(g, (my_shard, cc * hr, 0), (1, hr, g.shape[2]))[0]
            halves.append(_slot_sum(own, sl, "sum_" + n))
        for n, g in zip(names, _swap_halves(halves, "swap_halves_" + tag)):
            stacked[n] = _adamw_layer(a[n], g, a["m_" + n], a["v_" + n], l, stacked[n], "adamw_" + n)
        if len(srcs) > len(names):
            by_sender = lax.dynamic_update_slice(slots[-1], srcs[-1][None], (2 * my_shard + cc, 0, 0))
            small_sums[l] = _slot_sum(None, by_sender, "sum_small")

    pending = []
    token = None
    for l in reversed(range(nl)):
        q, s, w = params[l], saved[l], weights[l]
        gs = {}
        dpe, dpre = _ple_bwd(dx, s["pe"], s["pre"], after=token)
        g_pe = _mm(a["p"][l, 0], dpe, "tn", out_dtype=BF16, name="mm_dwpe", out_slabs=N_CHIPS)
        g_pg = _mm(s["h3"], dpre, "tn", out_dtype=BF16, name="mm_dwpg")
        dh3 = _mm(dpre, w["w_pg"], "nt", out_dtype=BF16, name="mm_dh3")
        dx2, gs["norm3_g"] = _rms_bwd(dh3, s["x2"], a["norm3_g"][l], dx, "rms3_bwd")
        g_down = _mm(s["act"], dx2, "tn", out_dtype=BF16, name="mm_dwdown")
        dact = _mm(dx2, w["w_down"], "nt", out_dtype=BF16, name="mm_dact")
        dhf_g, dhf_v, dcw_g, dcw_v, dcb_g, dcb_v = _ffn_act_bwd(dact, s["hf_g"], s["hf_v"], q["ffn_conv_w"], q["ffn_conv_b"])
        gs["ffn_conv_w"] = jnp.concatenate([dcw_g, dcw_v], axis=1)
        gs["ffn_conv_b"] = jnp.concatenate([dcb_g, dcb_v], axis=1)
        g_up = _mm(s["h2"], dhf_g, "tn", out_dtype=BF16, name="mm_dwup_g", out_slabs=N_CHIPS, out_n=2 * f)
        g_up = _mm(s["h2"], dhf_v, "tn", out_dtype=BF16, name="mm_dwup_v", out_slabs=N_CHIPS, out_n=2 * f, o_noff=f, out_buf=g_up)
        early = {"w_pe": g_pe, "w_up": g_up, "w_pg": g_pg.reshape(N_CHIPS, -1, g_pg.shape[-1]),
                 "w_down": g_down.reshape(N_CHIPS, -1, g_down.shape[-1])}
        early_handle, token = _push_start([early[n] for n in PUSH_EARLY], None, f"push_start_{l}a")
        dh2 = _mm(dhf_g, w["w_up"], "nt", b_slabs=True, out_dtype=F32, name="mm_dh2_g", after=token)
        dh2 = _mm(dhf_v, w["w_up"], "nt", b_slabs=True, b_koff=f, res=dh2, out_dtype=F32, name="mm_dh2_v")
        dx1, gs["norm2_g"] = _rms_bwd(dh2, s["x1"], a["norm2_g"][l], dx2, "rms2_bwd")
        g_out = _mm(s["mix"], dx1, "tn", out_dtype=BF16, name="mm_dwout")
        dmix = _mm(dx1, w["w_out"], "nt", out_dtype=F32, name="mm_dmix")
        dz, gs["a_ln_g"], gs["a_ln_b"], dws, dbs_t = _mix_a_bwd(s["z"], dmix, q["a_ln_g"], q["a_ln_b"], q["wm"], q["wm_t"],
                                                               q["bs_t"])
        gs["a_ws"] = dws * tril
        gs["a_bs"] = dbs_t.reshape(GMLP_CHUNK, N_HEADS, HEAD_DIM).sum(-1).T
        dz, gs["b_conv_w"], gs["b_conv_b"], dwa, dwx, gs["b_ba"], gs["b_bx"], gs["b_lam"] = _mix_b_bwd(
            s["z"], dz, dmix, s["hs"], q["b_conv_w"], q["b_conv_b"], q["b_wa"], q["b_wx"], q["b_wa_t"], q["b_wx_t"],
            q["b_ba"], q["b_bx"], q["b_lam"])
        gs["b_wa"], gs["b_wx"] = _diag_blocks(dwa), _diag_blocks(dwx)
        dz, gs["c_lb"], dng = _mix_c_bwd(s["z"], dz, dmix, s["o_pre"], s["states"], q["lb"], q["ng"])
        gs["c_norm_g"] = dng.reshape(N_HEADS, HEAD_DIM).sum(0)
        dz, dwd, gs["d_scale"] = _mix_d_bwd(s["z"], dz, dmix, q["d_w"], q["d_w_t"], q["d_scale"])
        gs["d_w"] = _diag_blocks(dwd)
        g_in = _mm(s["h1"], dz, "tn", out_dtype=BF16, name="mm_dwin")
        dh1 = _mm(dz, w["w_in"], "nt", out_dtype=BF16, name="mm_dh1")
        dx, gs["norm1_g"] = _rms_bwd(dh1, s["x0"], a["norm1_g"][l], dx1, "rms1_bwd")

        late = {"w_in": _cols_to_slabs(g_in), "w_out": g_out.reshape(N_CHIPS, -1, g_out.shape[-1])}
        small = [gs[n] for n in SMALL_PER_LAYER] + ([g_final] if l == nl - 1 else [])
        late_handle, token = _push_start([late[n] for n in PUSH_LATE], _pack_small(small), f"push_start_{l}b")
        for push in pending:
            finish_push(*push, dx)
        pending = [(l, PUSH_EARLY, early_handle, "a"), (l, PUSH_LATE, late_handle, "b")]
    for push in pending:
        finish_push(*push, dx)
    grad_x = dx[None]

    def small_shape(n):
        return a[n].shape[1:-1] + (a[n].shape[-1] * N_CHIPS,) if n in SMALL_SHARDED else a[n].shape[1:]

    per_layer = {n: [] for n in SMALL_PER_LAYER}
    for l in range(nl):
        vec, off = small_sums[l].reshape(-1), 0
        for n in SMALL_PER_LAYER:
            shape = small_shape(n)
            size = 1
            for dim in shape:
                size *= dim
            per_layer[n].append(vec[off:off + size].reshape(shape))
            off += size
        if l == nl - 1:
            grad_final = vec[off:off + d]
    grads = {n: jnp.stack(per_layer[n]) for n in SMALL_PER_LAYER}
    grads["c_lb"] = lbs_vjp(grads["c_lb"])[0]
    grads["final_g"] = grad_final
    for n in SMALL_SHARDED:
        cs = a[n].shape[-1]
        grads[n] = lax.dynamic_slice_in_dim(grads[n], my_shard * cs, cs, axis=2)

    outs = {}
    for n in WEIGHTS:
        if n in BIG:
            outs[n] = stacked[n]
        else:
            outs[n] = (grads[n],) + _adamw(a[n], grads[n], a["m_" + n], a["v_" + n], "adamw_" + n)
    return (loss, grad_x, *[outs[n][0] for n in WEIGHTS], *[outs[n][1] for n in WEIGHTS], *[outs[n][2] for n in WEIGHTS],
            *[outs[n][3] for n in WEIGHTS])


def _step_v1(args):
    a = dict(zip(ARGS, args, strict=True))
    x0 = a["x"][0]
    target = a["loss_target"][0]
    nl = a["norm1_g"].shape[0]
    t, d = x0.shape
    f = a["w_down"].shape[1] * 4
    cx, cy, _ = _me()
    my_shard = 2 * cx + cy

    gathered = _gather_weights(
        [a[n].astype(BF16) for n in BIG] + [a[n] for n in SMALL_SHARDED],
        [BIG_LAYOUT[n] for n in BIG] + [SLABS, SLABS], [True] * len(BIG) + [False, False])
    wfull = dict(zip(BIG + SMALL_SHARDED, gathered))
    w_in = jnp.moveaxis(wfull["w_in"], 0, 2).reshape(nl, d, -1)
    zc = w_in.shape[-1]
    b_conv_w = jnp.moveaxis(wfull["b_conv_w"], 0, 2).reshape(nl, 4, W_GRP)
    ffn_conv_w = jnp.moveaxis(wfull["ffn_conv_w"], 0, 2).reshape(nl, 3, 2 * f)
    w_out, w_up, w_down, w_pe, w_pg = (wfull[n] for n in ("w_out", "w_up", "w_down", "w_pe", "w_pg"))

    lbs, lbs_vjp = jax.vjp(_lower_bounds, a["c_lb"])
    tril = jnp.tril(jnp.ones((GMLP_CHUNK, GMLP_CHUNK), F32))

    def layer_params(l):
        q = {}
        q["wm"] = (a["a_ws"][l] * tril).astype(BF16)
        q["wm_t"] = jnp.swapaxes(q["wm"], 1, 2)
        q["bs_t"] = jnp.repeat(a["a_bs"][l].T, HEAD_DIM, axis=1)
        for nm in ("b_wa", "b_wx", "d_w"):
            bd = _block_diag(a[nm][l]).astype(BF16)
            q[nm], q[nm + "_t"] = bd, bd.T
        for nm in ("a_ln_g", "a_ln_b", "b_conv_b", "b_ba", "b_bx", "b_lam", "d_scale"):
            q[nm] = a[nm][l].reshape(1, W_GRP)
        q["lb"] = lbs[l].reshape(1, W_GRP)
        q["ng"] = jnp.tile(a["c_norm_g"][l], N_HEADS).reshape(1, W_GRP)
        q["b_conv_w"] = b_conv_w[l]
        q["ffn_conv_w"] = ffn_conv_w[l]
        q["ffn_conv_b"] = a["ffn_conv_b"][l].reshape(1, 2 * f)
        return q

    saved = []
    xl = x0
    for l in range(nl):
        q = layer_params(l)
        s = {"x0": xl}
        s["h1"] = _rms_fwd(xl, a["norm1_g"][l], "rms1_fwd")
        s["z"] = _mm(s["h1"], w_in, "nn", b_layer=l, out_dtype=F32, name="mm_z")
        mix = _mix_a_fwd(s["z"], d, q["a_ln_g"], q["a_ln_b"], q["wm"], q["bs_t"])
        mix, s["hs"] = _mix_b_fwd(s["z"], mix, q["b_conv_w"], q["b_conv_b"], q["b_wa"], q["b_wx"], q["b_ba"], q["b_bx"],
                                  q["b_lam"])
        mix, s["o_pre"], s["states"] = _mix_c_fwd(s["z"], mix, q["lb"], q["ng"])
        s["mix"] = _mix_d_fwd(s["z"], mix, q["d_w"], q["d_scale"])
        s["x1"] = _mm(s["mix"], w_out, "nn", b_layer=l, res=xl, out_dtype=F32, name="mm_out")
        s["h2"] = _rms_fwd(s["x1"], a["norm2_g"][l], "rms2_fwd")
        s["hf_g"] = _mm(s["h2"], w_up, "nn", b_layer=l, n=f, out_dtype=F32, name="mm_up_g")
        s["hf_v"] = _mm(s["h2"], w_up, "nn", b_layer=l, n=f, b_noff=f, out_dtype=F32, name="mm_up_v")
        s["act"] = _ffn_act_fwd(s["hf_g"], s["hf_v"], q["ffn_conv_w"], q["ffn_conv_b"])
        s["x2"] = _mm(s["act"], w_down, "nn", b_layer=l, res=s["x1"], out_dtype=F32, name="mm_down")
        s["h3"] = _rms_fwd(s["x2"], a["norm3_g"][l], "rms3_fwd")
        s["pre"] = _mm(s["h3"], w_pg, "nn", b_layer=l, out_dtype=F32, name="mm_pg")
        s["pe"] = _mm(a["p"][l, 0], w_pe, "nn", b_layer=l, out_dtype=F32, name="mm_pe")
        xl = _ple_fwd(s["x2"], s["pe"], s["pre"])
        saved.append(s)

    dx, g_final, loss = _final_loss(xl, a["final_g"], target)
    loss = lax.psum(loss[0, 0], ("x", "y", "c"))

    gbig = {n: None for n in BIG}
    gsm = {n: [None] * nl for n in SMALL if n != "final_g"}
    for l in reversed(range(nl)):
        q, s = layer_params(l), saved[l]
        dpe, dpre = _ple_bwd(dx, s["pe"], s["pre"])
        gbig["w_pe"] = _mm(a["p"][l, 0], dpe, "tn", out_dtype=BF16, name="mm_dwpe", layer=l, nlayers=nl, out_buf=gbig["w_pe"])
        gbig["w_pg"] = _mm(s["h3"], dpre, "tn", out_dtype=BF16, name="mm_dwpg", layer=l, nlayers=nl, out_buf=gbig["w_pg"])
        dh3 = _mm(dpre, w_pg, "nt", b_layer=l, out_dtype=BF16, name="mm_dh3")
        dx2, gsm["norm3_g"][l] = _rms_bwd(dh3, s["x2"], a["norm3_g"][l], dx, "rms3_bwd")
        gbig["w_down"] = _mm(s["act"], dx2, "tn", out_dtype=BF16, name="mm_dwdown", layer=l, nlayers=nl, out_buf=gbig["w_down"])
        dact = _mm(dx2, w_down, "nt", b_layer=l, out_dtype=BF16, name="mm_dact")
        dhf_g, dhf_v, dcw_g, dcw_v, dcb_g, dcb_v = _ffn_act_bwd(dact, s["hf_g"], s["hf_v"], q["ffn_conv_w"], q["ffn_conv_b"])
        gsm["ffn_conv_w"][l] = jnp.concatenate([dcw_g, dcw_v], axis=1)
        gsm["ffn_conv_b"][l] = jnp.concatenate([dcb_g, dcb_v], axis=1)
        gbig["w_up"] = _mm(s["h2"], dhf_g, "tn", out_dtype=BF16, name="mm_dwup_g", layer=l, nlayers=nl, out_n=2 * f,
                           out_buf=gbig["w_up"])
        gbig["w_up"] = _mm(s["h2"], dhf_v, "tn", out_dtype=BF16, name="mm_dwup_v", layer=l, nlayers=nl, out_n=2 * f, o_noff=f,
                           out_buf=gbig["w_up"])
        dh2 = _mm(dhf_g, w_up, "nt", b_layer=l, out_dtype=F32, name="mm_dh2_g")
        dh2 = _mm(dhf_v, w_up, "nt", b_layer=l, b_koff=f, res=dh2, out_dtype=F32, name="mm_dh2_v")
        dx1, gsm["norm2_g"][l] = _rms_bwd(dh2, s["x1"], a["norm2_g"][l], dx2, "rms2_bwd")
        gbig["w_out"] = _mm(s["mix"], dx1, "tn", out_dtype=BF16, name="mm_dwout", layer=l, nlayers=nl, out_buf=gbig["w_out"])
        dmix = _mm(dx1, w_out, "nt", b_layer=l, out_dtype=F32, name="mm_dmix")
        dz, gsm["a_ln_g"][l], gsm["a_ln_b"][l], dws, dbs_t = _mix_a_bwd(s["z"], dmix, q["a_ln_g"], q["a_ln_b"], q["wm"],
                                                                       q["wm_t"], q["bs_t"])
        gsm["a_ws"][l] = dws * tril
        gsm["a_bs"][l] = dbs_t.reshape(GMLP_CHUNK, N_HEADS, HEAD_DIM).sum(-1).T
        (dz, gsm["b_conv_w"][l], gsm["b_conv_b"][l], dwa, dwx, gsm["b_ba"][l], gsm["b_bx"][l],
         gsm["b_lam"][l]) = _mix_b_bwd(s["z"], dz, dmix, s["hs"], q["b_conv_w"], q["b_conv_b"], q["b_wa"], q["b_wx"],
                                       q["b_wa_t"], q["b_wx_t"], q["b_ba"], q["b_bx"], q["b_lam"])
        gsm["b_wa"][l], gsm["b_wx"][l] = _diag_blocks(dwa), _diag_blocks(dwx)
        dz, gsm["c_lb"][l], dng = _mix_c_bwd(s["z"], dz, dmix, s["o_pre"], s["states"], q["lb"], q["ng"])
        gsm["c_norm_g"][l] = dng.reshape(N_HEADS, HEAD_DIM).sum(0)
        dz, dwd, gsm["d_scale"][l] = _mix_d_bwd(s["z"], dz, dmix, q["d_w"], q["d_w_t"], q["d_scale"])
        gsm["d_w"][l] = _diag_blocks(dwd)
        gbig["w_in"] = _mm(s["h1"], dz, "tn", out_dtype=BF16, name="mm_dwin", layer=l, nlayers=nl, out_buf=gbig["w_in"])
        dh1 = _mm(dz, w_in, "nt", b_layer=l, out_dtype=BF16, name="mm_dh1")
        dx, gsm["norm1_g"][l] = _rms_bwd(dh1, s["x0"], a["norm1_g"][l], dx1, "rms1_bwd")

    grad_x = dx[None]

    gsmall = {n: jnp.stack([g.reshape(a[n].shape[1:]) if n not in SMALL_SHARDED else g for g in gsm[n]]) for n in gsm}
    gsmall["c_lb"] = lbs_vjp(gsmall["c_lb"])[0]
    gsmall["final_g"] = g_final.reshape(-1)
    flat = jnp.concatenate([gsmall[n].reshape(-1) for n in SMALL])
    n_small = flat.shape[0]
    pad = (-n_small) % 1024
    small_slab = jnp.pad(flat, (0, pad)).reshape(-1, 128)

    gbig["w_in"] = jnp.moveaxis(gbig["w_in"].reshape(nl, d, 4, zc // 4), 2, 0)
    pushed = _push_partials([gbig[n] for n in BIG], [BIG_LAYOUT[n] for n in BIG], small_slab)
    halves = [_slot_sum(sl.reshape(N_DEV, -1, sl.shape[-1]), "sum_" + n).reshape(sl.shape[1:]) for n, sl in zip(BIG, pushed)]
    shard_grads = dict(zip(BIG, _swap_halves(halves)))
    small_sum = _slot_sum(pushed[-1], "sum_small").reshape(-1)[:n_small]

    grads, off = {}, 0
    for n in SMALL:
        shape = gsmall[n].shape
        size = 1
        for s_ in shape:
            size *= s_
        g = small_sum[off:off + size].reshape(shape)
        off += size
        if n in SMALL_SHARDED:
            cs = a[n].shape[-1]
            g = lax.dynamic_slice_in_dim(g, my_shard * cs, cs, axis=2)
        grads[n] = g
    grads.update(shard_grads)

    outs_g, outs_d, outs_m, outs_v = [], [], [], []
    for n in WEIGHTS:
        dlt, nm, nv = _adamw(a[n], grads[n], a["m_" + n], a["v_" + n], "adamw_" + n)
        outs_g.append(grads[n])
        outs_d.append(dlt)
        outs_m.append(nm)
        outs_v.append(nv)
    return (loss, grad_x, *outs_g, *outs_d, *outs_m, *outs_v)
```

```python
import functools

import jax
import jax.numpy as jnp
from jax import lax
from jax.experimental import pallas as pl
from jax.experimental.pallas import tpu as pltpu

F32 = jnp.float32
BF16 = jnp.bfloat16
EPS = 1e-6
HEAD_DIM = 64
N_HEADS = 4
W_GRP = HEAD_DIM * N_HEADS
GMLP_CHUNK = 128
HGRN_CHUNK = 64
RGLRU_C = 8.0
POOL_HALO = 16
EXP_CLAMP = 80.0
ADAM_LR, ADAM_B1, ADAM_B2, ADAM_EPS, ADAM_WD, ADAM_STEP = 0.001, 0.9, 0.999, 1e-08, 0.01, 10
VMEM_LIMIT_BYTES = 56 * 1024 * 1024
TILE_PREFS = (1024, 1408, 768, 512, 256, 128)
ROW_TILE_PREFS = (512, 256, 128, 64, 32, 16, 8)
MESH_ID = pl.DeviceIdType.MESH
N_DEV = 8


def _pick(n, prefs=TILE_PREFS):
    for p in prefs:
        if n % p == 0:
            return p
    return n


def _cp(*sem):
    return pltpu.CompilerParams(dimension_semantics=sem if sem else None, vmem_limit_bytes=VMEM_LIMIT_BYTES)


def _sds(shape, dtype):
    return jax.ShapeDtypeStruct(tuple(shape), dtype)


_GELU_C = 0.7978845608028654
_GELU_A = 0.044715


def _gelu(x):
    return 0.5 * x * (1.0 + jnp.tanh(_GELU_C * (x + _GELU_A * x * x * x)))


def _gelu_and_grad(x):
    t = jnp.tanh(_GELU_C * (x + _GELU_A * x * x * x))
    g = 0.5 * x * (1.0 + t)
    dg = 0.5 * (1.0 + t) + 0.5 * x * (1.0 - t * t) * _GELU_C * (1.0 + 3.0 * _GELU_A * x * x)
    return g, dg


def _sigmoid(x):
    return 1.0 / (1.0 + jnp.exp(-x))


def _dot(a, b):
    return jnp.dot(a, b, preferred_element_type=F32)


def _dot_nt(a, b):
    return lax.dot_general(a, b, (((1,), (1,)), ((), ())), preferred_element_type=F32)


def _dot_tn(a, b):
    return lax.dot_general(a, b, (((0,), (0,)), ((), ())), preferred_element_type=F32)


def _split3(x):
    hi = x.astype(BF16)
    r1 = x - hi.astype(F32)
    mid = r1.astype(BF16)
    lo = (r1 - mid.astype(F32)).astype(BF16)
    return hi, mid, lo


def _dot_f32_rhs_exact(x, m_bf16):
    hi, mid, lo = _split3(x)
    return _dot(hi, m_bf16) + _dot(mid, m_bf16) + _dot(lo, m_bf16)


def _dot_f32_lhs_exact(m_bf16, x):
    hi, mid, lo = _split3(x)
    return _dot(m_bf16, hi) + _dot(m_bf16, mid) + _dot(m_bf16, lo)


def _head_masks(width=W_GRP):
    lane = lax.broadcasted_iota(jnp.int32, (1, width), 1)
    return [(lane >= h * HEAD_DIM) & (lane < (h + 1) * HEAD_DIM) for h in range(N_HEADS)]


def _block_mask(n=W_GRP):
    r = lax.broadcasted_iota(jnp.int32, (n, n), 0)
    c = lax.broadcasted_iota(jnp.int32, (n, n), 1)
    m = None
    for h in range(N_HEADS):
        mh = (r >= h * HEAD_DIM) & (r < (h + 1) * HEAD_DIM) & (c >= h * HEAD_DIM) & (c < (h + 1) * HEAD_DIM)
        m = mh if m is None else (m | mh)
    return m


def _mm(a, b, mode, *, out_dtype, name, res=None, b_slabs=False, n=None, b_noff=0, b_koff=0,
        out_slabs=0, out_buf=None, out_n=None, o_noff=0, after=None):
    if mode == "tn":
        k_dim, m_dim = a.shape
    else:
        m_dim, k_dim = a.shape
    if mode == "nt":
        n_dim = b.shape[-2]
    else:
        n_dim = n if n is not None else (b.shape[0] * b.shape[2] if b_slabs else b.shape[1])
    n_total = out_n if out_n is not None else n_dim
    tm, tn, tk = _pick(m_dim), _pick(n_dim), _pick(k_dim)
    if b_slabs and mode == "nt":
        tk = _pick(b.shape[2])
    elif b_slabs:
        tn = _pick(b.shape[2])
    elif out_slabs:
        tn = _pick(n_total // out_slabs)
    nk = k_dim // tk
    assert b_noff % tn == 0 and b_koff % tk == 0 and o_noff % tn == 0 and n_dim % tn == 0 and k_dim % tk == 0
    bn0, bk0, on0 = b_noff // tn, b_koff // tk, o_noff // tn
    dims = {"nn": (((1,), (0,)), ((), ())), "nt": (((1,), (1,)), ((), ())), "tn": (((0,), (0,)), ((), ()))}[mode]

    if mode == "tn":
        a_spec = pl.BlockSpec((tk, tm), lambda i, j, k: (k, i))
    else:
        a_spec = pl.BlockSpec((tm, tk), lambda i, j, k: (i, k))
    if not b_slabs:
        if mode == "nt":
            b_spec = pl.BlockSpec((tn, tk), lambda i, j, k: (j + bn0, k + bk0))
        else:
            b_spec = pl.BlockSpec((tk, tn), lambda i, j, k: (k + bk0, j + bn0))
    elif mode == "nt":
        bper = b.shape[2] // tk
        b_spec = pl.BlockSpec((None, tn, tk), lambda i, j, k: ((k + bk0) // bper, j, (k + bk0) % bper))
    else:
        bper = b.shape[2] // tn
        b_spec = pl.BlockSpec((None, tk, tn), lambda i, j, k: ((j + bn0) // bper, k, (j + bn0) % bper))
    in_specs = [a_spec, b_spec]
    args = [a, b]
    if res is not None:
        in_specs.append(pl.BlockSpec((tm, tn), lambda i, j, k: (i, j)))
        args.append(res)
    if out_slabs:
        oper = n_total // out_slabs // tn
        out_shape = _sds((out_slabs, m_dim, n_total // out_slabs), out_dtype)
        out_spec = pl.BlockSpec((None, tm, tn), lambda i, j, k: ((j + on0) // oper, i, (j + on0) % oper))
    else:
        out_shape = _sds((m_dim, n_total), out_dtype)
        out_spec = pl.BlockSpec((tm, tn), lambda i, j, k: (i, j + on0))
    aliases = {}
    if out_buf is not None:
        in_specs.append(pl.BlockSpec(memory_space=pl.ANY))
        args.append(out_buf)
        aliases = {len(args) - 1: 0}
    if after is not None:
        in_specs.append(pl.BlockSpec(memory_space=pl.ANY))
        args.append(after)
    has_res = res is not None
    n_in = len(args)

    def body(*refs):
        a_ref, b_ref = refs[0], refs[1]
        res_ref = refs[2] if has_res else None
        o_ref = refs[n_in]
        acc_ref = refs[-1] if nk > 1 else None
        part = lax.dot_general(a_ref[...].astype(BF16), b_ref[...].astype(BF16), dims, preferred_element_type=F32)

        def finish(v):
            if has_res:
                v = v + res_ref[...]
            o_ref[...] = v.astype(o_ref.dtype)

        if nk == 1:
            finish(part)
        else:
            kk = pl.program_id(2)

            @pl.when(kk == 0)
            def _():
                acc_ref[...] = part

            @pl.when(kk > 0)
            def _():
                acc_ref[...] += part

            @pl.when(kk == nk - 1)
            def _():
                finish(acc_ref[...])

    return pl.pallas_call(
        body, grid=(m_dim // tm, n_dim // tn, nk), in_specs=in_specs, out_specs=out_spec, out_shape=out_shape,
        scratch_shapes=[pltpu.VMEM((tm, tn), F32)] if nk > 1 else [],
        input_output_aliases=aliases, name=name, compiler_params=_cp("parallel", "parallel", "arbitrary"),
    )(*args)


def _rms_fwd(x, g, name):
    t, d = x.shape
    tm = _pick(t, ROW_TILE_PREFS)

    def body(x_ref, g_ref, o_ref):
        xv = x_ref[...]
        r = lax.rsqrt(jnp.mean(xv * xv, axis=-1, keepdims=True) + EPS)
        o_ref[...] = (xv * r * g_ref[...]).astype(o_ref.dtype)

    return pl.pallas_call(
        body, grid=(t // tm,),
        in_specs=[pl.BlockSpec((tm, d), lambda i: (i, 0)), pl.BlockSpec((1, d), lambda i: (0, 0))],
        out_specs=pl.BlockSpec((tm, d), lambda i: (i, 0)), out_shape=_sds((t, d), BF16),
        name=name, compiler_params=_cp("parallel"),
    )(x, g.reshape(1, d))


def _rms_bwd(dh, x, g, dres, name):
    t, d = x.shape
    tm = _pick(t, ROW_TILE_PREFS)

    def body(dh_ref, x_ref, g_ref, dres_ref, dx_ref, dg_ref):
        i = pl.program_id(0)
        xv = x_ref[...]
        dy = dh_ref[...].astype(F32)
        r = lax.rsqrt(jnp.mean(xv * xv, axis=-1, keepdims=True) + EPS)
        dyg = dy * g_ref[...]
        dot = jnp.mean(dyg * xv, axis=-1, keepdims=True)
        dx_ref[...] = dres_ref[...] + r * dyg - xv * (r * r * r) * dot
        part = jnp.sum(dy * xv * r, axis=0, keepdims=True)

        @pl.when(i == 0)
        def _():
            dg_ref[...] = part

        @pl.when(i > 0)
        def _():
            dg_ref[...] += part

    row = pl.BlockSpec((tm, d), lambda i: (i, 0))
    vec = pl.BlockSpec((1, d), lambda i: (0, 0))
    return pl.pallas_call(
        body, grid=(t // tm,), in_specs=[row, row, vec, row], out_specs=[row, vec],
        out_shape=[_sds((t, d), F32), _sds((1, d), F32)], name=name, compiler_params=_cp("arbitrary"),
    )(dh, x, g.reshape(1, d), dres)


def _final_loss(x, g, target):
    t, d = x.shape
    tm = _pick(t, ROW_TILE_PREFS)

    def body(x_ref, g_ref, t_ref, dx_ref, dg_ref, loss_ref):
        i = pl.program_id(0)
        xv = x_ref[...]
        gv = g_ref[...]
        r = lax.rsqrt(jnp.mean(xv * xv, axis=-1, keepdims=True) + EPS)
        err = xv * r * gv - t_ref[...]
        lpart = (0.5 / d) * jnp.sum(jnp.sum(err * err, axis=1, keepdims=True), axis=0, keepdims=True)
        dy = err * (1.0 / d)
        dyg = dy * gv
        dot = jnp.mean(dyg * xv, axis=-1, keepdims=True)
        dx_ref[...] = r * dyg - xv * (r * r * r) * dot
        part = jnp.sum(dy * xv * r, axis=0, keepdims=True)

        @pl.when(i == 0)
        def _():
            dg_ref[...] = part
            loss_ref[...] = lpart

        @pl.when(i > 0)
        def _():
            dg_ref[...] += part
            loss_ref[...] += lpart

    row = pl.BlockSpec((tm, d), lambda i: (i, 0))
    vec = pl.BlockSpec((1, d), lambda i: (0, 0))
    return pl.pallas_call(
        body, grid=(t // tm,), in_specs=[row, vec, row], out_specs=[row, vec, pl.BlockSpec((1, 1), lambda i: (0, 0))],
        out_shape=[_sds((t, d), F32), _sds((1, d), F32), _sds((1, 1), F32)], name="final_loss",
        compiler_params=_cp("arbitrary"),
    )(x, g.reshape(1, d), target)


def _shift_down(ext, k, halo):
    return pltpu.roll(ext, k, 0)[halo:]


def _shift_up(ext, k, tm):
    return pltpu.roll(ext, ext.shape[0] - k, 0)[:tm]


def _ffn_tiles(t, f):
    return _pick(t, ROW_TILE_PREFS), _pick(f, (256, 128))


def _ffn_act_fwd(hf_g, hf_v, conv_w, conv_b):
    t, f = hf_g.shape
    tm, cn = _ffn_tiles(t, f)
    nf = f // cn

    def body(g_ref, v_ref, wg_ref, wv_ref, bg_ref, bv_ref, o_ref, ext_ref, hg_ref, hv_ref):
        i = pl.program_id(1)

        @pl.when(i == 0)
        def _():
            hg_ref[...] = jnp.zeros_like(hg_ref)
            hv_ref[...] = jnp.zeros_like(hv_ref)

        def conv(x_ref, halo_ref, w_ref, b_ref):
            ext_ref[0:8, :] = halo_ref[...]
            ext_ref[8:, :] = x_ref[...]
            halo_ref[...] = x_ref[tm - 8:tm, :]
            ext = ext_ref[...]
            w = w_ref[...]
            return b_ref[...] + w[2:3, :] * ext[8:] + w[1:2, :] * _shift_down(ext, 1, 8) + w[0:1, :] * _shift_down(ext, 2, 8)

        gc = conv(g_ref, hg_ref, wg_ref, bg_ref)
        vc = conv(v_ref, hv_ref, wv_ref, bv_ref)
        o_ref[...] = (_gelu(gc) * vc).astype(o_ref.dtype)

    blk = pl.BlockSpec((tm, cn), lambda j, i: (i, j))
    return pl.pallas_call(
        body, grid=(nf, t // tm),
        in_specs=[blk, blk, pl.BlockSpec((3, cn), lambda j, i: (0, j)), pl.BlockSpec((3, cn), lambda j, i: (0, j + nf)),
                  pl.BlockSpec((1, cn), lambda j, i: (0, j)), pl.BlockSpec((1, cn), lambda j, i: (0, j + nf))],
        out_specs=blk, out_shape=_sds((t, f), BF16),
        scratch_shapes=[pltpu.VMEM((tm + 8, cn), F32), pltpu.VMEM((8, cn), F32), pltpu.VMEM((8, cn), F32)],
        name="ffn_act_fwd", compiler_params=_cp("parallel", "arbitrary"),
    )(hf_g, hf_v, conv_w, conv_w, conv_b, conv_b)


def _ffn_act_bwd(dact, hf_g, hf_v, conv_w, conv_b):
    t, f = hf_g.shape
    tm, cn = _ffn_tiles(t, f)
    nf, nt = f // cn, t // tm
    hb = tm // 8

    def body(da_ref, g_ref, v_ref, gh_ref, vh_ref, wg_ref, wv_ref, bg_ref, bv_ref,
             dg_ref, dv_ref, dwg_ref, dwv_ref, dbg_ref, dbv_ref, ext_ref, cg_ref, cv_ref):
        i = pl.program_id(1)
        first_tile = i == nt - 1

        @pl.when(i == 0)
        def _():
            cg_ref[...] = jnp.zeros_like(cg_ref)
            cv_ref[...] = jnp.zeros_like(cv_ref)
            dwg_ref[...] = jnp.zeros_like(dwg_ref)
            dwv_ref[...] = jnp.zeros_like(dwv_ref)
            dbg_ref[...] = jnp.zeros_like(dbg_ref)
            dbv_ref[...] = jnp.zeros_like(dbv_ref)

        def shifted(x_ref, halo_ref):
            ext_ref[0:8, :] = jnp.where(first_tile, 0.0, halo_ref[...])
            ext_ref[8:, :] = x_ref[...]
            ext = ext_ref[0:tm + 8, :]
            return ext[8:], _shift_down(ext, 1, 8), _shift_down(ext, 2, 8)

        wg, wv = wg_ref[...], wv_ref[...]
        g0, g1, g2 = shifted(g_ref, gh_ref)
        gc = bg_ref[...] + wg[2:3, :] * g0 + wg[1:2, :] * g1 + wg[0:1, :] * g2
        v0, v1, v2 = shifted(v_ref, vh_ref)
        vc = bv_ref[...] + wv[2:3, :] * v0 + wv[1:2, :] * v1 + wv[0:1, :] * v2
        da = da_ref[...].astype(F32)
        gel, dgel = _gelu_and_grad(gc)
        dgc = da * vc * dgel
        dvc = da * gel

        def back(dc, carry_ref, w, x0, x1, x2, dx_ref, dw_ref, db_ref):
            ext_ref[0:tm, :] = dc
            ext_ref[tm:tm + 8, :] = carry_ref[...]
            carry_ref[...] = dc[0:8, :]
            ext = ext_ref[0:tm + 8, :]
            dx = w[2:3, :] * dc + w[1:2, :] * _shift_up(ext, 1, tm) + w[0:1, :] * _shift_up(ext, 2, tm)
            dx_ref[...] = dx.astype(dx_ref.dtype)
            dw_ref[0:1, :] += jnp.sum(dc * x2, axis=0, keepdims=True)
            dw_ref[1:2, :] += jnp.sum(dc * x1, axis=0, keepdims=True)
            dw_ref[2:3, :] += jnp.sum(dc * x0, axis=0, keepdims=True)
            db_ref[...] += jnp.sum(dc, axis=0, keepdims=True)

        back(dgc, cg_ref, wg, g0, g1, g2, dg_ref, dwg_ref, dbg_ref)
        back(dvc, cv_ref, wv, v0, v1, v2, dv_ref, dwv_ref, dbv_ref)

    blk = pl.BlockSpec((tm, cn), lambda j, i: (nt - 1 - i, j))
    halo = pl.BlockSpec((8, cn), lambda j, i: (jnp.maximum((nt - 1 - i) * hb - 1, 0), j))
    w3g = pl.BlockSpec((3, cn), lambda j, i: (0, j))
    w3v = pl.BlockSpec((3, cn), lambda j, i: (0, j + nf))
    b1g = pl.BlockSpec((1, cn), lambda j, i: (0, j))
    b1v = pl.BlockSpec((1, cn), lambda j, i: (0, j + nf))
    acc3 = pl.BlockSpec((3, cn), lambda j, i: (0, j))
    acc1 = pl.BlockSpec((1, cn), lambda j, i: (0, j))
    return pl.pallas_call(
        body, grid=(nf, nt),
        in_specs=[blk, blk, blk, halo, halo, w3g, w3v, b1g, b1v],
        out_specs=[blk, blk, acc3, acc3, acc1, acc1],
        out_shape=[_sds((t, f), BF16), _sds((t, f), BF16), _sds((3, f), F32), _sds((3, f), F32),
                   _sds((1, f), F32), _sds((1, f), F32)],
        scratch_shapes=[pltpu.VMEM((tm + 8, cn), F32), pltpu.VMEM((8, cn), F32), pltpu.VMEM((8, cn), F32)],
        name="ffn_act_bwd", compiler_params=_cp("parallel", "arbitrary"),
    )(dact, hf_g, hf_v, hf_g, hf_v, conv_w, conv_w, conv_b, conv_b)


def _ple_fwd(x2, pe, pre):
    t, d = x2.shape
    tm = _pick(t, ROW_TILE_PREFS)

    def body(x_ref, pe_ref, pre_ref, o_ref):
        o_ref[...] = x_ref[...] + pe_ref[...] * _sigmoid(pre_ref[...])

    row = pl.BlockSpec((tm, d), lambda i: (i, 0))
    return pl.pallas_call(body, grid=(t // tm,), in_specs=[row, row, row], out_specs=row,
                          out_shape=_sds((t, d), F32), name="ple_fwd", compiler_params=_cp("parallel"))(x2, pe, pre)


def _ple_bwd(dx3, pe, pre, after=None):
    t, d = dx3.shape
    tm = _pick(t, ROW_TILE_PREFS)

    def body(dx_ref, pe_ref, pre_ref, *rest):
        dpe_ref, dpre_ref = rest[-2:]
        gate = _sigmoid(pre_ref[...])
        dx = dx_ref[...]
        dpe_ref[...] = (dx * gate).astype(dpe_ref.dtype)
        dpre_ref[...] = (dx * pe_ref[...] * gate * (1.0 - gate)).astype(dpre_ref.dtype)

    row = pl.BlockSpec((tm, d), lambda i: (i, 0))
    extra = [] if after is None else [after]
    return pl.pallas_call(body, grid=(t // tm,), in_specs=[row, row, row] + [pl.BlockSpec(memory_space=pl.ANY)] * len(extra),
                          out_specs=[row, row], out_shape=[_sds((t, d), BF16), _sds((t, d), BF16)], name="ple_bwd",
                          compiler_params=_cp("parallel"))(dx3, pe, pre, *extra)


def _mix_tm(t):
    return _pick(t, (512, 256, 128))


def _zblk(tm, col, rev_nt=None):
    if rev_nt is None:
        return pl.BlockSpec((tm, W_GRP), lambda i: (i, col))
    return pl.BlockSpec((tm, W_GRP), lambda i: (rev_nt - 1 - i, col))


def _full(shape):
    nd = len(shape)
    return pl.BlockSpec(tuple(shape), lambda i: (0,) * nd)


def _gmlp_sv(wm_ref, vnc, bs, hm):
    sv = bs
    for h in range(N_HEADS):
        sv = sv + jnp.where(hm[h], _dot(wm_ref[h], vnc), 0.0)
    return sv


def _layernorm(v, g, b):
    mu = jnp.mean(v, axis=-1, keepdims=True)
    vc = v - mu
    rs = lax.rsqrt(jnp.mean(vc * vc, axis=-1, keepdims=True) + EPS)
    xhat = vc * rs
    return xhat, rs, xhat * g + b


def _mix_a_fwd(z, d_mix, ln_g, ln_b, wm, bs_t):
    t = z.shape[0]
    tm = _mix_tm(t)

    def body(u_ref, v_ref, g_ref, b_ref, wm_ref, bs_ref, o_ref):
        hm = _head_masks()
        ug = _gelu(u_ref[...])
        _, _, vn = _layernorm(_gelu(v_ref[...]), g_ref[...], b_ref[...])
        vnb = vn.astype(BF16)
        for n in range(tm // GMLP_CHUNK):
            sl = slice(n * GMLP_CHUNK, (n + 1) * GMLP_CHUNK)
            o_ref[sl, :] = ug[sl] * _gmlp_sv(wm_ref, vnb[sl], bs_ref[...], hm)

    return pl.pallas_call(
        body, grid=(t // tm,),
        in_specs=[_zblk(tm, 0), _zblk(tm, 1), _full((1, W_GRP)), _full((1, W_GRP)), _full(wm.shape), _full(bs_t.shape)],
        out_specs=_zblk(tm, 0), out_shape=_sds((t, d_mix), F32), name="mix_a_fwd", compiler_params=_cp("parallel"),
    )(z, z, ln_g, ln_b, wm, bs_t)


def _mix_a_bwd(z, dmix, ln_g, ln_b, wm, wm_t, bs_t):
    t, zc = z.shape
    tm = _mix_tm(t)

    def body(u_ref, v_ref, dy_ref, g_ref, b_ref, wm_ref, wmt_ref, bs_ref, dz_ref, dg_ref, db_ref, dws_ref, dbs_ref):
        i = pl.program_id(0)

        @pl.when(i == 0)
        def _():
            dg_ref[...] = jnp.zeros_like(dg_ref)
            db_ref[...] = jnp.zeros_like(db_ref)
            dws_ref[...] = jnp.zeros_like(dws_ref)
            dbs_ref[...] = jnp.zeros_like(dbs_ref)

        hm = _head_masks()
        ug, dug = _gelu_and_grad(u_ref[...])
        vg, dvg = _gelu_and_grad(v_ref[...])
        gv = g_ref[...]
        xhat, rs, vn = _layernorm(vg, gv, b_ref[...])
        vnb = vn.astype(BF16)
        dy = dy_ref[...]
        for n in range(tm // GMLP_CHUNK):
            sl = slice(n * GMLP_CHUNK, (n + 1) * GMLP_CHUNK)
            vnc = vnb[sl]
            sv = _gmlp_sv(wm_ref, vnc, bs_ref[...], hm)
            dsv = dy[sl] * ug[sl]
            dz_ref[sl, 0:W_GRP] = dy[sl] * sv * dug[sl]
            dbs_ref[...] += dsv
            dsvb = dsv.astype(BF16)
            dvn = jnp.zeros((GMLP_CHUNK, W_GRP), F32)
            for h in range(N_HEADS):
                dws_ref[h] += _dot_nt(jnp.where(hm[h], dsv, 0.0).astype(BF16), vnc)
                dvn = dvn + jnp.where(hm[h], _dot(wmt_ref[h], dsvb), 0.0)
            xh = xhat[sl]
            dg_ref[...] += jnp.sum(dvn * xh, axis=0, keepdims=True)
            db_ref[...] += jnp.sum(dvn, axis=0, keepdims=True)
            dxh = dvn * gv
            dvg_c = rs[sl] * (dxh - jnp.mean(dxh, axis=-1, keepdims=True) - xh * jnp.mean(dxh * xh, axis=-1, keepdims=True))
            dz_ref[sl, W_GRP:2 * W_GRP] = dvg_c * dvg[sl]

    return pl.pallas_call(
        body, grid=(t // tm,),
        in_specs=[_zblk(tm, 0), _zblk(tm, 1), _zblk(tm, 0), _full((1, W_GRP)), _full((1, W_GRP)), _full(wm.shape),
                  _full(wm_t.shape), _full(bs_t.shape)],
        out_specs=[pl.BlockSpec((tm, 2 * W_GRP), lambda i: (i, 0)), _full((1, W_GRP)), _full((1, W_GRP)),
                   _full(wm.shape), _full(bs_t.shape)],
        out_shape=[_sds((t, zc), F32), _sds((1, W_GRP), F32), _sds((1, W_GRP), F32), _sds(wm.shape, F32),
                   _sds(bs_t.shape, F32)],
        name="mix_a_bwd", compiler_params=_cp("arbitrary"),
    )(z, z, dmix, ln_g, ln_b, wm, wm_t, bs_t)


def _softplus(x):
    return jnp.maximum(x, 0.0) + jnp.log(1.0 + jnp.exp(-jnp.abs(x)))


def _neg_expm1(x):
    series = -x * (1.0 + x * 0.5 * (1.0 + x * (1.0 / 3.0) * (1.0 + x * 0.25 * (1.0 + x * 0.2))))
    return jnp.where(x > -0.1, series, 1.0 - jnp.exp(x))


def _rglru_gates(ext_ref, x_ref, halo, cw, cb, wa_ref, wx_ref, ba, bx, lam):
    ext_ref[0:8, :] = halo
    ext_ref[8:, :] = x_ref[...]
    ext = ext_ref[...]
    x0, x1, x2, x3 = ext[8:], _shift_down(ext, 1, 8), _shift_down(ext, 2, 8), _shift_down(ext, 3, 8)
    xc = cb + cw[3:4, :] * x0 + cw[2:3, :] * x1 + cw[1:2, :] * x2 + cw[0:1, :] * x3
    xcb = xc.astype(BF16)
    r = _sigmoid(_dot(xcb, wa_ref[...]) + ba)
    ig = _sigmoid(_dot(xcb, wx_ref[...]) + bx)
    sp = _softplus(-lam)
    la = -RGLRU_C * r * sp
    a = jnp.exp(la)
    mult = jnp.sqrt(_neg_expm1(2.0 * la))
    return (x0, x1, x2, x3), xc, r, ig, sp, a, mult


def _mix_b_fwd(z, mix, conv_w, conv_b, wa, wx, ba, bx, lam):
    t = z.shape[0]
    tm = _mix_tm(t)

    def body(x_ref, gb_ref, cw_ref, cb_ref, wa_ref, wx_ref, ba_ref, bx_ref, lam_ref, mix_in, o_ref, hs_ref,
             ext_ref, a_ref, b_ref, xh_ref, hc_ref):
        i = pl.program_id(0)

        @pl.when(i == 0)
        def _():
            xh_ref[...] = jnp.zeros_like(xh_ref)
            hc_ref[...] = jnp.zeros_like(hc_ref)

        _, xc, _, ig, _, a, mult = _rglru_gates(ext_ref, x_ref, xh_ref[...], cw_ref[...], cb_ref[...], wa_ref, wx_ref,
                                                ba_ref[...], bx_ref[...], lam_ref[...])
        xh_ref[...] = x_ref[tm - 8:tm, :]
        a_ref[...] = a
        b_ref[...] = mult * (ig * xc)
        rid = lax.broadcasted_iota(jnp.int32, (8, W_GRP), 0)

        def group(gi, hprev):
            base = pl.multiple_of(gi * 8, 8)
            ca = a_ref[pl.ds(base, 8), :]
            cb = b_ref[pl.ds(base, 8), :]
            for k in (1, 2, 4):
                m = rid >= k
                cb = jnp.where(m, ca * pltpu.roll(cb, k, 0) + cb, cb)
                ca = jnp.where(m, ca * pltpu.roll(ca, k, 0), ca)
            hh = cb + ca * hprev
            hs_ref[pl.ds(base, 8), :] = hh
            return hh[7:8, :]

        hlast = lax.fori_loop(0, tm // 8, group, hc_ref[0:1, :])
        hc_ref[...] = jnp.broadcast_to(hlast, hc_ref.shape)
        o_ref[...] = hs_ref[...] * _gelu(gb_ref[...])

    sq = _full((W_GRP, W_GRP))
    vec = _full((1, W_GRP))
    return pl.pallas_call(
        body, grid=(t // tm,),
        in_specs=[_zblk(tm, 2), _zblk(tm, 3), _full((4, W_GRP)), vec, sq, sq, vec, vec, vec, pl.BlockSpec(memory_space=pl.ANY)],
        out_specs=[_zblk(tm, 1), pl.BlockSpec((tm, W_GRP), lambda i: (i, 0))],
        out_shape=[_sds(mix.shape, F32), _sds((t, W_GRP), F32)],
        scratch_shapes=[pltpu.VMEM((tm + 8, W_GRP), F32), pltpu.VMEM((tm, W_GRP), F32), pltpu.VMEM((tm, W_GRP), F32),
                        pltpu.VMEM((8, W_GRP), F32), pltpu.VMEM((8, W_GRP), F32)],
        input_output_aliases={9: 0}, name="mix_b_fwd", compiler_params=_cp("arbitrary"),
    )(z, z, conv_w, conv_b, wa, wx, ba, bx, lam, mix)


def _mix_b_bwd(z, dz, dmix, hs, conv_w, conv_b, wa, wx, wa_t, wx_t, ba, bx, lam):
    t = z.shape[0]
    tm = _mix_tm(t)
    nt = t // tm
    hb = tm // 8

    def body(x_ref, gb_ref, xhalo_ref, hs_ref, hhalo_ref, dy_ref, cw_ref, cb_ref, wa_ref, wx_ref, wat_ref, wxt_ref,
             ba_ref, bx_ref, lam_ref, dz_in, dz_ref, dcw_ref, dcb_ref, dwa_ref, dwx_ref, dba_ref, dbx_ref, dlam_ref,
             ext_ref, c_ref, d_ref, g_ref, an_ref, gn_ref, dxn_ref):
        i = pl.program_id(0)
        first_tile = i == nt - 1

        @pl.when(i == 0)
        def _():
            for ref in (dcw_ref, dcb_ref, dwa_ref, dwx_ref, dba_ref, dbx_ref, dlam_ref, an_ref, gn_ref, dxn_ref):
                ref[...] = jnp.zeros_like(ref)

        cw, lam = cw_ref[...], lam_ref[...]
        xhalo = jnp.where(first_tile, 0.0, xhalo_ref[...])
        (x0, x1, x2, x3), xc, r, ig, sp, a, mult = _rglru_gates(
            ext_ref, x_ref, xhalo, cw, cb_ref[...], wa_ref, wx_ref, ba_ref[...], bx_ref[...], lam)
        hs = hs_ref[...]
        dy = dy_ref[...]
        gel, dgel = _gelu_and_grad(gb_ref[...])
        dz_ref[:, W_GRP:2 * W_GRP] = dy * hs * dgel

        ext_ref[0:tm, :] = a
        ext_ref[tm:tm + 8, :] = an_ref[...]
        an_ref[...] = a[0:8, :]
        c_ref[...] = _shift_up(ext_ref[...], 1, tm)
        d_ref[...] = dy * gel
        rid = lax.broadcasted_iota(jnp.int32, (8, W_GRP), 0)

        def group(j, gnext):
            base = pl.multiple_of((tm // 8 - 1 - j) * 8, 8)
            cc = c_ref[pl.ds(base, 8), :]
            cd = d_ref[pl.ds(base, 8), :]
            for k in (1, 2, 4):
                m = rid < 8 - k
                cd = jnp.where(m, cc * pltpu.roll(cd, 8 - k, 0) + cd, cd)
                cc = jnp.where(m, cc * pltpu.roll(cc, 8 - k, 0), cc)
            gg = cd + cc * gnext
            g_ref[pl.ds(base, 8), :] = gg
            return gg[0:1, :]

        gfirst = lax.fori_loop(0, tm // 8, group, gn_ref[0:1, :])
        gn_ref[...] = jnp.broadcast_to(gfirst, gn_ref.shape)
        g = g_ref[...]

        ext_ref[0:8, :] = jnp.where(first_tile, 0.0, hhalo_ref[...])
        ext_ref[8:, :] = hs
        hprev = _shift_down(ext_ref[...], 1, 8)
        da = g * hprev
        dmult = g * (ig * xc)
        di = g * mult * xc
        dxc = g * mult * ig
        dla = da * a - dmult * a * a / mult
        dr = dla * (-RGLRU_C * sp)
        dlam_ref[...] += jnp.sum(dla * (-RGLRU_C * r), axis=0, keepdims=True) * (-_sigmoid(-lam))
        dpr = dr * r * (1.0 - r)
        dpi = di * ig * (1.0 - ig)
        dprb, dpib, xcb = dpr.astype(BF16), dpi.astype(BF16), xc.astype(BF16)
        dba_ref[...] += jnp.sum(dpr, axis=0, keepdims=True)
        dbx_ref[...] += jnp.sum(dpi, axis=0, keepdims=True)
        dwa_ref[...] += _dot_tn(xcb, dprb)
        dwx_ref[...] += _dot_tn(xcb, dpib)
        dxc = dxc + _dot(dprb, wat_ref[...]) + _dot(dpib, wxt_ref[...])
        dcb_ref[...] += jnp.sum(dxc, axis=0, keepdims=True)
        dcw_ref[3:4, :] += jnp.sum(dxc * x0, axis=0, keepdims=True)
        dcw_ref[2:3, :] += jnp.sum(dxc * x1, axis=0, keepdims=True)
        dcw_ref[1:2, :] += jnp.sum(dxc * x2, axis=0, keepdims=True)
        dcw_ref[0:1, :] += jnp.sum(dxc * x3, axis=0, keepdims=True)
        ext_ref[0:tm, :] = dxc
        ext_ref[tm:tm + 8, :] = dxn_ref[...]
        dxn_ref[...] = dxc[0:8, :]
        ext = ext_ref[...]
        dz_ref[:, 0:W_GRP] = (cw[3:4, :] * dxc + cw[2:3, :] * _shift_up(ext, 1, tm) + cw[1:2, :] * _shift_up(ext, 2, tm)
                              + cw[0:1, :] * _shift_up(ext, 3, tm))

    sq = _full((W_GRP, W_GRP))
    vec = _full((1, W_GRP))
    halo = lambda col: pl.BlockSpec((8, W_GRP), lambda i: (jnp.maximum((nt - 1 - i) * hb - 1, 0), col))
    rev = lambda col: _zblk(tm, col, nt)
    return pl.pallas_call(
        body, grid=(nt,),
        in_specs=[rev(2), rev(3), halo(2), rev(0), halo(0), rev(1), _full((4, W_GRP)), vec, sq, sq, sq, sq, vec, vec, vec,
                  pl.BlockSpec(memory_space=pl.ANY)],
        out_specs=[pl.BlockSpec((tm, 2 * W_GRP), lambda i: (nt - 1 - i, 1)), _full((4, W_GRP)), vec, sq, sq, vec, vec, vec],
        out_shape=[_sds(dz.shape, F32), _sds((4, W_GRP), F32), _sds((1, W_GRP), F32), _sds((W_GRP, W_GRP), F32),
                   _sds((W_GRP, W_GRP), F32), _sds((1, W_GRP), F32), _sds((1, W_GRP), F32), _sds((1, W_GRP), F32)],
        scratch_shapes=[pltpu.VMEM((tm + 8, W_GRP), F32), pltpu.VMEM((tm, W_GRP), F32), pltpu.VMEM((tm, W_GRP), F32),
                        pltpu.VMEM((tm, W_GRP), F32), pltpu.VMEM((8, W_GRP), F32), pltpu.VMEM((8, W_GRP), F32),
                        pltpu.VMEM((8, W_GRP), F32)],
        input_output_aliases={15: 0}, name="mix_b_bwd", compiler_params=_cp("arbitrary"),
    )(z, z, z, hs, hs, dmix, conv_w, conv_b, wa, wx, wa_t, wx_t, ba, bx, lam, dz)


def _tri(n, lower):
    r = lax.broadcasted_iota(jnp.int32, (n, n), 0)
    c = lax.broadcasted_iota(jnp.int32, (n, n), 1)
    return jnp.where((r >= c) if lower else (r <= c), 1.0, 0.0).astype(BF16)


def _causal_stack():
    r = lax.broadcasted_iota(jnp.int32, (N_HEADS * HGRN_CHUNK, HGRN_CHUNK), 0)
    c = lax.broadcasted_iota(jnp.int32, (N_HEADS * HGRN_CHUNK, HGRN_CHUNK), 1)
    m = None
    for h in range(N_HEADS):
        mh = (r >= h * HGRN_CHUNK) & (r < (h + 1) * HGRN_CHUNK) & (r - h * HGRN_CHUNK >= c)
        m = mh if m is None else (m | mh)
    return m


def _stack_heads(x, hm):
    return jnp.concatenate([jnp.where(hm[h], x, 0.0) for h in range(N_HEADS)], axis=0)


def _unstack_heads(xs, hm):
    out = jnp.where(hm[0], xs[0:HGRN_CHUNK], 0.0)
    for h in range(1, N_HEADS):
        out = out + jnp.where(hm[h], xs[h * HGRN_CHUNK:(h + 1) * HGRN_CHUNK], 0.0)
    return out


def _hgrn_chunk(qv, fv, lb, tril):
    sq = _sigmoid(qv)
    qq = qv * sq
    sg = _sigmoid(fv)
    fg = lb + (1.0 - lb) * sg
    kk = 1.0 - fg
    bb = _dot_f32_lhs_exact(tril, jnp.log(fg))
    b_last = bb[HGRN_CHUNK - 1:HGRN_CHUNK, :]
    b_mid = bb[HGRN_CHUNK // 2 - 1:HGRN_CHUNK // 2, :]
    eq = jnp.exp(jnp.minimum(bb - b_mid, EXP_CLAMP))
    ek = jnp.exp(jnp.minimum(b_mid - bb, EXP_CLAMP))
    eb = jnp.exp(bb)
    el = jnp.exp(b_last - bb)
    return sq, qq, sg, fg, kk, b_last, eq, ek, eb, el


def _seg_mean(x, avg):
    return _dot_f32_rhs_exact(x, avg)


def _mix_c_fwd(z, mix, lb, ng):
    t = z.shape[0]
    tm = _mix_tm(t)
    nch = tm // HGRN_CHUNK

    def body(q_ref, f_ref, i_ref, g_ref, lb_ref, ng_ref, mix_in, y_ref, o_ref, ss_ref, s_ref):
        @pl.when(pl.program_id(0) == 0)
        def _():
            s_ref[...] = jnp.zeros_like(s_ref)

        hm = _head_masks()
        bmask = _block_mask()
        causal = _causal_stack()
        tril = _tri(HGRN_CHUNK, True)
        avg = jnp.where(bmask, 1.0 / HEAD_DIM, 0.0).astype(BF16)
        lb, ng = lb_ref[...], ng_ref[...]

        def chunk(c, carry):
            rows = pl.ds(pl.multiple_of(c * HGRN_CHUNK, HGRN_CHUNK), HGRN_CHUNK)
            vv = i_ref[rows, :]
            gv = g_ref[rows, :]
            _, qq, _, _, kk, b_last, eq, ek, eb, el = _hgrn_chunk(q_ref[rows, :], f_ref[rows, :], lb, tril)
            vb = vv.astype(BF16)
            qs = _stack_heads(qq * eq, hm).astype(BF16)
            att = jnp.where(causal, _dot_nt(qs, (kk * ek).astype(BF16)), 0.0)
            o = _unstack_heads(_dot(att.astype(BF16), vb), hm)
            s0 = s_ref[...]
            ss_ref[c] = s0
            o = o + _dot_nt((qq * eb).astype(BF16), s0.astype(BF16))
            s_ref[...] = s0 * jnp.exp(b_last) + jnp.where(bmask, _dot_tn(vb, (kk * el).astype(BF16)), 0.0)
            o_ref[rows, :] = o
            rstd = lax.rsqrt(_seg_mean(o * o, avg) + EPS)
            y_ref[rows, :] = o * rstd * ng * (gv * _sigmoid(gv))
            return carry

        lax.fori_loop(0, nch, chunk, 0)

    vec = _full((1, W_GRP))
    return pl.pallas_call(
        body, grid=(t // tm,),
        in_specs=[_zblk(tm, 4), _zblk(tm, 5), _zblk(tm, 6), _zblk(tm, 7), vec, vec, pl.BlockSpec(memory_space=pl.ANY)],
        out_specs=[_zblk(tm, 2), pl.BlockSpec((tm, W_GRP), lambda i: (i, 0)),
                   pl.BlockSpec((nch, W_GRP, W_GRP), lambda i: (i, 0, 0))],
        out_shape=[_sds(mix.shape, F32), _sds((t, W_GRP), F32), _sds((t // HGRN_CHUNK, W_GRP, W_GRP), F32)],
        scratch_shapes=[pltpu.VMEM((W_GRP, W_GRP), F32)],
        input_output_aliases={6: 0}, name="mix_c_fwd", compiler_params=_cp("arbitrary"),
    )(z, z, z, z, lb, ng, mix)


def _mix_c_bwd(z, dz, dmix, o_pre, states, lb, ng):
    t = z.shape[0]
    tm = _mix_tm(t)
    nt = t // tm
    nch = tm // HGRN_CHUNK

    def body(q_ref, f_ref, i_ref, g_ref, o_ref, ss_ref, dy_ref, lb_ref, ng_ref, dz_in, dz_ref, dlb_ref, dng_ref, ds_ref):
        @pl.when(pl.program_id(0) == 0)
        def _():
            ds_ref[...] = jnp.zeros_like(ds_ref)
            dlb_ref[...] = jnp.zeros_like(dlb_ref)
            dng_ref[...] = jnp.zeros_like(dng_ref)

        hm = _head_masks()
        bmask = _block_mask()
        causal = _causal_stack()
        tril = _tri(HGRN_CHUNK, True)
        triu = _tri(HGRN_CHUNK, False)
        avg = jnp.where(bmask, 1.0 / HEAD_DIM, 0.0).astype(BF16)
        lb, ng = lb_ref[...], ng_ref[...]
        last_row = lax.broadcasted_iota(jnp.int32, (HGRN_CHUNK, W_GRP), 0) == HGRN_CHUNK - 1

        def chunk(j, carry):
            c = nch - 1 - j
            rows = pl.ds(pl.multiple_of(c * HGRN_CHUNK, HGRN_CHUNK), HGRN_CHUNK)
            qv, gv, vv = q_ref[rows, :], g_ref[rows, :], i_ref[rows, :]
            sq, qq, sg, fg, kk, b_last, eq, ek, eb, el = _hgrn_chunk(qv, f_ref[rows, :], lb, tril)
            s0 = ss_ref[c]
            ds1 = ds_ref[...]
            o = o_ref[rows, :]
            dy = dy_ref[rows, :]
            rstd = lax.rsqrt(_seg_mean(o * o, avg) + EPS)
            oh = o * rstd
            sgg = _sigmoid(gv)
            dz_ref[rows, 3 * W_GRP:4 * W_GRP] = dy * oh * ng * (sgg * (1.0 + gv * (1.0 - sgg)))
            don = dy * gv * sgg
            dng_ref[...] += jnp.sum(don * oh, axis=0, keepdims=True)
            doh = don * ng
            do = rstd * (doh - oh * _seg_mean(doh * oh, avg))
            qt, kt, qh, kh = qq * eq, kk * ek, qq * eb, kk * el
            vb, dob = vv.astype(BF16), do.astype(BF16)
            ktb, khb = kt.astype(BF16), kh.astype(BF16)
            ds1b = ds1.astype(BF16)
            qs = _stack_heads(qt, hm).astype(BF16)
            dos = _stack_heads(do, hm).astype(BF16)
            att = jnp.where(causal, _dot_nt(qs, ktb), 0.0).astype(BF16)
            datt = jnp.where(causal, _dot_nt(dos, vb), 0.0).astype(BF16)
            dv = _dot_tn(att, dos) + _dot_nt(khb, ds1b)
            dqt = _unstack_heads(_dot(datt, ktb), hm)
            dkt = _dot_tn(datt, qs)
            dqh = _dot(dob, s0.astype(BF16))
            dkh = _dot(vb, ds1b)
            e_last = jnp.exp(b_last)
            ds_ref[...] = ds1 * e_last + jnp.where(bmask, _dot_tn(dob, qh.astype(BF16)), 0.0)
            dq = dqt * eq + dqh * eb
            dk = dkt * ek + dkh * el
            db = qt * dqt - kt * dkt + qh * dqh - kh * dkh
            db_last = jnp.sum(kh * dkh, axis=0, keepdims=True) + e_last * jnp.sum(ds1 * s0, axis=0, keepdims=True)
            db = db + jnp.where(last_row, db_last, 0.0)
            dlogf = _dot_f32_lhs_exact(triu, db)
            dfg = dlogf / fg - dk
            dz_ref[rows, W_GRP:2 * W_GRP] = dfg * (1.0 - lb) * sg * (1.0 - sg)
            dlb_ref[...] += jnp.sum(dfg * (1.0 - sg), axis=0, keepdims=True)
            dz_ref[rows, 0:W_GRP] = dq * (sq * (1.0 + qv * (1.0 - sq)))
            dz_ref[rows, 2 * W_GRP:3 * W_GRP] = dv
            return carry

        lax.fori_loop(0, nch, chunk, 0)

    vec = _full((1, W_GRP))
    rev = lambda col: _zblk(tm, col, nt)
    return pl.pallas_call(
        body, grid=(nt,),
        in_specs=[rev(4), rev(5), rev(6), rev(7), rev(0), pl.BlockSpec((nch, W_GRP, W_GRP), lambda i: (nt - 1 - i, 0, 0)),
                  rev(2), vec, vec, pl.BlockSpec(memory_space=pl.ANY)],
        out_specs=[pl.BlockSpec((tm, 4 * W_GRP), lambda i: (nt - 1 - i, 1)), vec, vec],
        out_shape=[_sds(dz.shape, F32), _sds((1, W_GRP), F32), _sds((1, W_GRP), F32)],
        scratch_shapes=[pltpu.VMEM((W_GRP, W_GRP), F32)],
        input_output_aliases={9: 0}, name="mix_c_bwd", compiler_params=_cp("arbitrary"),
    )(z, z, z, z, o_pre, states, dmix, lb, ng, dz)


def _pool_select(hm, s2, s4, s8, s16):
    return jnp.where(hm[0], s2, jnp.where(hm[1], s4, jnp.where(hm[2], s8, s16)))


def _pool_counts(hm, row0, tm):
    pos = (row0 + 1 + lax.broadcasted_iota(jnp.int32, (tm, W_GRP), 0)).astype(F32)
    win = _pool_select(hm, 2.0, 4.0, 8.0, 16.0)
    return jnp.minimum(pos, win)


def _pooled(ext_ref, x, halo, hm, cnt):
    ext_ref[0:POOL_HALO, :] = halo
    ext_ref[POOL_HALO:, :] = x
    e = ext_ref[...]
    s2 = e + pltpu.roll(e, 1, 0)
    s4 = s2 + pltpu.roll(s2, 2, 0)
    s8 = s4 + pltpu.roll(s4, 4, 0)
    s16 = s8 + pltpu.roll(s8, 8, 0)
    return _pool_select(hm, s2, s4, s8, s16)[POOL_HALO:] / cnt - x


def _mix_d_fwd(z, mix, wd, scale):
    t = z.shape[0]
    tm = _mix_tm(t)

    def body(x_ref, wd_ref, sc_ref, mix_in, o_ref, ext_ref, halo_ref):
        i = pl.program_id(0)

        @pl.when(i == 0)
        def _():
            halo_ref[...] = jnp.zeros_like(halo_ref)

        hm = _head_masks()
        x = x_ref[...]
        pooled = _pooled(ext_ref, x, halo_ref[...], hm, _pool_counts(hm, i * tm, tm))
        halo_ref[...] = x_ref[tm - POOL_HALO:tm, :]
        o_ref[...] = _dot(pooled.astype(BF16), wd_ref[...]) * sc_ref[...]

    return pl.pallas_call(
        body, grid=(t // tm,),
        in_specs=[_zblk(tm, 8), _full((W_GRP, W_GRP)), _full((1, W_GRP)), pl.BlockSpec(memory_space=pl.ANY)],
        out_specs=_zblk(tm, 3), out_shape=_sds(mix.shape, F32),
        scratch_shapes=[pltpu.VMEM((tm + POOL_HALO, W_GRP), F32), pltpu.VMEM((POOL_HALO, W_GRP), F32)],
        input_output_aliases={3: 0}, name="mix_d_fwd", compiler_params=_cp("arbitrary"),
    )(z, wd, scale, mix)


def _mix_d_bwd(z, dz, dmix, wd, wd_t, scale):
    t = z.shape[0]
    tm = _mix_tm(t)
    nt = t // tm
    hb = tm // POOL_HALO

    def body(x_ref, xhalo_ref, dy_ref, wd_ref, wdt_ref, sc_ref, dz_in, dz_ref, dwd_ref, dsc_ref, ext_ref, en_ref):
        i = pl.program_id(0)
        ri = nt - 1 - i

        @pl.when(i == 0)
        def _():
            en_ref[...] = jnp.zeros_like(en_ref)
            dwd_ref[...] = jnp.zeros_like(dwd_ref)
            dsc_ref[...] = jnp.zeros_like(dsc_ref)

        hm = _head_masks()
        cnt = _pool_counts(hm, ri * tm, tm)
        x = x_ref[...]
        pooled = _pooled(ext_ref, x, jnp.where(ri == 0, 0.0, xhalo_ref[...]), hm, cnt)
        pb = pooled.astype(BF16)
        dy = dy_ref[...]
        dsc_ref[...] += jnp.sum(dy * _dot(pb, wd_ref[...]), axis=0, keepdims=True)
        dyw = (dy * sc_ref[...]).astype(BF16)
        dwd_ref[...] += _dot_tn(pb, dyw)
        dpool = _dot(dyw, wdt_ref[...])
        e = dpool / cnt
        ext_ref[0:tm, :] = e
        ext_ref[tm:, :] = en_ref[...]
        en_ref[...] = e[0:POOL_HALO, :]
        ee = ext_ref[...]
        n = tm + POOL_HALO
        r2 = ee + pltpu.roll(ee, n - 1, 0)
        r4 = r2 + pltpu.roll(r2, n - 2, 0)
        r8 = r4 + pltpu.roll(r4, n - 4, 0)
        r16 = r8 + pltpu.roll(r8, n - 8, 0)
        dz_ref[...] = _pool_select(hm, r2, r4, r8, r16)[:tm] - dpool

    sq = _full((W_GRP, W_GRP))
    vec = _full((1, W_GRP))
    return pl.pallas_call(
        body, grid=(nt,),
        in_specs=[_zblk(tm, 8, nt), pl.BlockSpec((POOL_HALO, W_GRP), lambda i: (jnp.maximum((nt - 1 - i) * hb - 1, 0), 8)),
                  _zblk(tm, 3, nt), sq, sq, vec, pl.BlockSpec(memory_space=pl.ANY)],
        out_specs=[_zblk(tm, 8, nt), sq, vec],
        out_shape=[_sds(dz.shape, F32), _sds((W_GRP, W_GRP), F32), _sds((1, W_GRP), F32)],
        scratch_shapes=[pltpu.VMEM((tm + POOL_HALO, W_GRP), F32), pltpu.VMEM((POOL_HALO, W_GRP), F32)],
        input_output_aliases={6: 0}, name="mix_d_bwd", compiler_params=_cp("arbitrary"),
    )(z, z, dmix, wd, wd_t, scale, dz)


def _as2d(a):
    if a.ndim == 1:
        return a.reshape(1, a.shape[0])
    return a.reshape(-1, a.shape[-1])


def _adamw(w, g, m, v, name):
    shape = w.shape
    w2, g2, m2, v2 = _as2d(w), _as2d(g), _as2d(m), _as2d(v)
    rows, cols = w2.shape
    tr = _pick(rows, (1024, 512, 256, 128, 64, 32, 16, 8))
    if tr * cols * 4 * 14 > VMEM_LIMIT_BYTES:
        tr = _pick(rows, (256, 128, 64, 32, 16, 8))

    def body(w_ref, g_ref, m_ref, v_ref, d_ref, nm_ref, nv_ref):
        gv = g_ref[...]
        mn = ADAM_B1 * m_ref[...] + (1.0 - ADAM_B1) * gv
        vn = ADAM_B2 * v_ref[...] + (1.0 - ADAM_B2) * (gv * gv)
        m_hat = mn / (1.0 - ADAM_B1 ** ADAM_STEP)
        v_hat = vn / (1.0 - ADAM_B2 ** ADAM_STEP)
        d_ref[...] = -ADAM_LR * (m_hat / (jnp.sqrt(v_hat) + ADAM_EPS) + ADAM_WD * w_ref[...])
        nm_ref[...] = mn
        nv_ref[...] = vn

    blk = pl.BlockSpec((tr, cols), lambda i: (i, 0))
    outs = pl.pallas_call(
        body, grid=(rows // tr,), in_specs=[blk] * 4, out_specs=[blk] * 3, out_shape=[_sds((rows, cols), F32)] * 3,
        name=name, compiler_params=_cp("parallel"),
    )(w2, g2, m2, v2)
    return tuple(o.reshape(shape) for o in outs)


def _adamw_layer(w, g, m, v, layer, bufs, name):
    nl, r, cs = w.shape
    tr = _pick(r, (256, 128, 64, 32, 16, 8))

    def body(w_ref, g_ref, m_ref, v_ref, *rest):
        go_ref, d_ref, nm_ref, nv_ref = rest[-4:]
        gv = g_ref[...]
        mn = ADAM_B1 * m_ref[...] + (1.0 - ADAM_B1) * gv
        vn = ADAM_B2 * v_ref[...] + (1.0 - ADAM_B2) * (gv * gv)
        m_hat = mn / (1.0 - ADAM_B1 ** ADAM_STEP)
        v_hat = vn / (1.0 - ADAM_B2 ** ADAM_STEP)
        go_ref[...] = gv
        d_ref[...] = -ADAM_LR * (m_hat / (jnp.sqrt(v_hat) + ADAM_EPS) + ADAM_WD * w_ref[...])
        nm_ref[...] = mn
        nv_ref[...] = vn

    lay = pl.BlockSpec((None, tr, cs), lambda i: (layer, i, 0))
    in_specs = [lay, pl.BlockSpec((tr, cs), lambda i: (i, 0)), lay, lay]
    args = [w, g, m, v]
    aliases = {}
    if bufs is not None:
        in_specs += [pl.BlockSpec(memory_space=pl.ANY)] * 4
        args += list(bufs)
        aliases = {4 + i: i for i in range(4)}
    return pl.pallas_call(
        body, grid=(r // tr,), in_specs=in_specs, out_specs=[lay] * 4, out_shape=[_sds((nl, r, cs), F32)] * 4,
        input_output_aliases=aliases, name=name, compiler_params=_cp("parallel"),
    )(*args)


def _slot_sum(own, slots, name):
    n_slots, rows, cols = slots.shape
    whole_fits = rows * cols * 4 * (n_slots + 2) * 2 <= VMEM_LIMIT_BYTES // 2
    tr = rows if whole_fits else _pick(rows, (512, 352, 256, 128, 64, 32, 16, 8))

    def body(*refs):
        s_ref, o_ref = refs[-2], refs[-1]
        acc = s_ref[0].astype(F32) if own is None else refs[0][...].astype(F32) + s_ref[0].astype(F32)
        for k in range(1, n_slots):
            acc = acc + s_ref[k].astype(F32)
        o_ref[...] = acc

    row = pl.BlockSpec((tr, cols), lambda i: (i, 0))
    return pl.pallas_call(
        body, grid=(rows // tr,),
        in_specs=([] if own is None else [row]) + [pl.BlockSpec((n_slots, tr, cols), lambda i: (0, i, 0))],
        out_specs=row, out_shape=_sds((rows, cols), F32), name=name, compiler_params=_cp("parallel"),
    )(*(() if own is None else (own,)), slots)


def _me():
    return lax.axis_index("x"), lax.axis_index("y"), lax.axis_index("c")


def _other_chips(x, y):
    return [(1 - x, y), (x, 1 - y), (1 - x, 1 - y)]


ANY_SPEC = pl.BlockSpec(memory_space=pl.ANY)
HBM_SPEC = pl.BlockSpec(memory_space=pltpu.HBM)
SEM_SPEC = pl.BlockSpec(memory_space=pltpu.SEMAPHORE)
SPLIT_COPY_PARAMS = pltpu.CompilerParams(has_side_effects=pltpu.SideEffectType.DATAFLOW_SIDE_EFFECTING)
N_CHIPS = 4


def _aligned(v, m):
    return v if isinstance(v, int) else pl.multiple_of(v, m)


def _in_hbm(arr):
    return pltpu.with_memory_space_constraint(arr, pltpu.HBM)


def _peer(x, y, c, k):
    fx, fy, fc = (k >> 2) & 1, (k >> 1) & 1, k & 1
    px = 1 - x if fx else x
    py = 1 - y if fy else y
    pc = 1 - c if fc else c
    return px, py, pc


def _gather_start(shards, after, name):
    n = len(shards)

    def body(*refs):
        src, land = refs[:n], refs[n:2 * n]
        send_sems, recv_sems = refs[2 * n + 1], refs[2 * n + 2]
        token = refs[-1]
        x, y, c = _me()
        for w in range(n):
            for chip in _other_chips(x, y):
                pltpu.make_async_remote_copy(
                    src_ref=src[w], dst_ref=land[w].at[2 * x + y], send_sem=send_sems.at[w], recv_sem=recv_sems.at[w],
                    device_id=(*chip, c), device_id_type=MESH_ID).start()
        token[...] = jnp.zeros_like(token)

    lands = [lax.empty((N_CHIPS,) + s.shape, s.dtype) for s in shards]
    thru = [pltpu.HBM(s.shape, s.dtype) for s in shards] + [pltpu.HBM(z.shape, z.dtype) for z in lands]
    outs = pl.pallas_call(
        body, name=name,
        out_shape=(pltpu.SemaphoreType.DMA((n,)), pltpu.SemaphoreType.DMA((n,)), *thru, _sds((8, 128), F32)),
        in_specs=[HBM_SPEC] * (2 * n) + [ANY_SPEC],
        out_specs=(SEM_SPEC, SEM_SPEC, *[HBM_SPEC] * (2 * n), pl.BlockSpec(memory_space=pltpu.VMEM)),
        input_output_aliases={i: 2 + i for i in range(2 * n)}, compiler_params=SPLIT_COPY_PARAMS,
    )(*[_in_hbm(s) for s in shards], *[_in_hbm(z) for z in lands], after)
    return (outs[0], outs[1], outs[2:2 + n], outs[2 + n:2 + 2 * n]), outs[-1]


def _gather_wait(send_sems, recv_sems, srcs, lands, after, name):
    n = len(srcs)

    def body(*refs):
        land = refs[n:2 * n]
        send_sems, recv_sems = refs[2 * n], refs[2 * n + 1]
        x, y, c = _me()
        for w in range(n):
            three = land[w].at[pl.ds(0, N_CHIPS - 1)]
            cp = pltpu.make_async_remote_copy(src_ref=three, dst_ref=three, send_sem=send_sems.at[w], recv_sem=recv_sems.at[w],
                                              device_id=(x, y, c), device_id_type=MESH_ID)
            cp.wait_send()
            cp.wait_recv()

    both = list(srcs) + list(lands)
    outs = pl.pallas_call(
        body, name=name, out_shape=tuple(pltpu.HBM(b.shape, b.dtype) for b in both),
        in_specs=[HBM_SPEC] * (2 * n) + [SEM_SPEC, SEM_SPEC, ANY_SPEC], out_specs=[HBM_SPEC] * (2 * n),
        input_output_aliases={i: i for i in range(2 * n)}, compiler_params=SPLIT_COPY_PARAMS,
    )(*both, send_sems, recv_sems, after)
    return outs[n:2 * n]


def _push_start(grads, small, name):
    n = len(grads)
    srcs = list(grads) + ([] if small is None else [small])
    ns = len(srcs)

    def body(*refs):
        src, slots = refs[:ns], refs[ns:2 * ns]
        send_sems, recv_sems = refs[2 * ns], refs[2 * ns + 1]
        token = refs[-1]
        x, y, c = _me()
        for w in range(ns):
            for k in range(1, N_DEV):
                px, py, pc = _peer(x, y, c, k)
                if w < n:
                    hr = src[w].shape[1] // 2
                    piece = src[w].at[2 * px + py, pl.ds(_aligned(pc * hr, 16), hr), :]
                    slot = slots[w].at[k - 1]
                else:
                    piece = src[w]
                    slot = slots[w].at[4 * x + 2 * y + c]
                pltpu.make_async_remote_copy(
                    src_ref=piece, dst_ref=slot, send_sem=send_sems.at[w], recv_sem=recv_sems.at[w],
                    device_id=(px, py, pc), device_id_type=MESH_ID).start()
        token[...] = jnp.zeros_like(token)

    slots = [lax.empty((N_DEV - 1, g.shape[1] // 2, g.shape[2]), g.dtype) for g in grads]
    if small is not None:
        slots.append(lax.empty((N_DEV,) + small.shape, small.dtype))
    both = srcs + slots
    outs = pl.pallas_call(
        body, name=name,
        out_shape=(pltpu.SemaphoreType.DMA((ns,)), pltpu.SemaphoreType.DMA((ns,)),
                   *[pltpu.HBM(b.shape, b.dtype) for b in both], _sds((8, 128), F32)),
        in_specs=[HBM_SPEC] * len(both),
        out_specs=(SEM_SPEC, SEM_SPEC, *[HBM_SPEC] * len(both), pl.BlockSpec(memory_space=pltpu.VMEM)),
        input_output_aliases={i: 2 + i for i in range(len(both))}, compiler_params=SPLIT_COPY_PARAMS,
    )(*[_in_hbm(b) for b in both])
    return (outs[0], outs[1], outs[2:2 + ns], outs[2 + ns:2 + 2 * ns]), outs[-1]


def _push_wait(send_sems, recv_sems, srcs, slots, after, name):
    n = len(srcs)

    def body(*refs):
        slot = refs[n:2 * n]
        send_sems, recv_sems = refs[2 * n], refs[2 * n + 1]
        x, y, c = _me()
        for w in range(n):
            seven = slot[w].at[pl.ds(0, N_DEV - 1)]
            cp = pltpu.make_async_remote_copy(src_ref=seven, dst_ref=seven, send_sem=send_sems.at[w],
                                              recv_sem=recv_sems.at[w], device_id=(x, y, c), device_id_type=MESH_ID)
            cp.wait_send()
            cp.wait_recv()

    both = list(srcs) + list(slots)
    outs = pl.pallas_call(
        body, name=name, out_shape=tuple(pltpu.HBM(b.shape, b.dtype) for b in both),
        in_specs=[HBM_SPEC] * (2 * n) + [SEM_SPEC, SEM_SPEC, ANY_SPEC], out_specs=[HBM_SPEC] * (2 * n),
        input_output_aliases={i: i for i in range(2 * n)}, compiler_params=SPLIT_COPY_PARAMS,
    )(*both, send_sems, recv_sems, after)
    return outs[:n], outs[n:]


SWAP_CHUNK_BYTES = 2 * 1024 * 1024


def _swap_chunk_rows(hr, cs):
    ch = hr
    while ch * cs * 4 > SWAP_CHUNK_BYTES and ch % 16 == 0:
        ch //= 2
    return ch


def _swap_halves(halves, name):
    n = len(halves)
    chunk = [_swap_chunk_rows(*h.shape) for h in halves]
    rounds = max(h.shape[0] // ch for h, ch in zip(halves, chunk))

    def body(*refs):
        src, dst, buf = refs[:n], refs[n:2 * n], refs[2 * n:3 * n]
        load_sems, put_sems, send_sems, recv_sems = refs[3 * n:]
        x, y, c = _me()
        sibling = (x, y, 1 - c)
        for j in range(rounds):
            live = [w for w in range(n) if j < src[w].shape[0] // chunk[w]]
            loads = [pltpu.make_async_copy(src[w].at[pl.ds(j * chunk[w], chunk[w])], buf[w], load_sems.at[w]) for w in live]
            for ld in loads:
                ld.start()
            moves = []
            for ld, w in zip(loads, live):
                ld.wait()
                rows = pl.ds(_aligned(c * src[w].shape[0] + j * chunk[w], 8), chunk[w])
                put = pltpu.make_async_copy(buf[w], dst[w].at[rows], put_sems.at[w])
                send = pltpu.make_async_remote_copy(src_ref=buf[w], dst_ref=dst[w].at[rows], send_sem=send_sems.at[w],
                                                    recv_sem=recv_sems.at[w], device_id=sibling, device_id_type=MESH_ID)
                put.start()
                send.start()
                moves.append((put, send))
            for put, send in moves:
                put.wait()
                send.wait_send()
        for w in range(n):
            hr = src[w].shape[0]
            got = dst[w].at[pl.ds(_aligned((1 - c) * hr, 8), hr)]
            pltpu.make_async_remote_copy(src_ref=got, dst_ref=got, send_sem=send_sems.at[w], recv_sem=recv_sems.at[w],
                                         device_id=sibling, device_id_type=MESH_ID).wait_recv()

    return pl.pallas_call(
        body, in_specs=[ANY_SPEC] * n, out_specs=[ANY_SPEC] * n,
        out_shape=[_sds((2 * h.shape[0], h.shape[1]), F32) for h in halves],
        scratch_shapes=[pltpu.VMEM((ch, h.shape[1]), F32) for h, ch in zip(halves, chunk)]
        + [pltpu.SemaphoreType.DMA((n,))] * 4,
        name=name,
    )(*halves)


BIG = ("w_in", "w_out", "w_up", "w_down", "w_pe", "w_pg")
ROW_SHARDED = ("w_out", "w_down", "w_pg")
SMALL = ("norm1_g", "a_ln_g", "a_ln_b", "a_ws", "a_bs", "b_conv_w", "b_conv_b", "b_wa", "b_ba", "b_wx", "b_bx", "b_lam",
         "c_lb", "c_norm_g", "d_w", "d_scale", "norm2_g", "ffn_conv_w", "ffn_conv_b", "norm3_g", "final_g")
SMALL_SHARDED = ("b_conv_w", "ffn_conv_w")
WEIGHTS = ("norm1_g", "w_in", "a_ln_g", "a_ln_b", "a_ws", "a_bs", "b_conv_w", "b_conv_b", "b_wa", "b_ba", "b_wx", "b_bx",
           "b_lam", "c_lb", "c_norm_g", "d_w", "d_scale", "w_out", "norm2_g", "w_up", "ffn_conv_w", "ffn_conv_b", "w_down",
           "norm3_g", "w_pe", "w_pg", "final_g")
ARGS = ("x", "p") + WEIGHTS + ("loss_target",) + tuple("m_" + n for n in WEIGHTS) + tuple("v_" + n for n in WEIGHTS)


def _block_diag(w):
    eye = jnp.eye(N_HEADS, dtype=w.dtype)
    return (eye[:, None, :, None] * w[:, :, None, :]).reshape(W_GRP, W_GRP)


def _diag_blocks(m):
    m4 = m.reshape(N_HEADS, HEAD_DIM, N_HEADS, HEAD_DIM)
    return jnp.stack([m4[h, :, h, :] for h in range(N_HEADS)])


def _lower_bounds(c_lb):
    lbs = jnp.cumsum(jax.nn.softmax(c_lb, axis=0), axis=0)
    return lbs - lbs[0:1]


def kernel(x, p, norm1_g, w_in, a_ln_g, a_ln_b, a_ws, a_bs, b_conv_w, b_conv_b, b_wa, b_ba, b_wx, b_bx, b_lam, c_lb, c_norm_g, d_w, d_scale, w_out, norm2_g, w_up, ffn_conv_w, ffn_conv_b, w_down, norm3_g, w_pe, w_pg, final_g, loss_target, m_norm1_g, m_w_in, m_a_ln_g, m_a_ln_b, m_a_ws, m_a_bs, m_b_conv_w, m_b_conv_b, m_b_wa, m_b_ba, m_b_wx, m_b_bx, m_b_lam, m_c_lb, m_c_norm_g, m_d_w, m_d_scale, m_w_out, m_norm2_g, m_w_up, m_ffn_conv_w, m_ffn_conv_b, m_w_down, m_norm3_g, m_w_pe, m_w_pg, m_final_g, v_norm1_g, v_w_in, v_a_ln_g, v_a_ln_b, v_a_ws, v_a_bs, v_b_conv_w, v_b_conv_b, v_b_wa, v_b_ba, v_b_wx, v_b_bx, v_b_lam, v_c_lb, v_c_norm_g, v_d_w, v_d_scale, v_w_out, v_norm2_g, v_w_up, v_ffn_conv_w, v_ffn_conv_b, v_w_down, v_norm3_g, v_w_pe, v_w_pg, v_final_g):
    return _step((x, p, norm1_g, w_in, a_ln_g, a_ln_b, a_ws, a_bs, b_conv_w, b_conv_b, b_wa, b_ba, b_wx, b_bx, b_lam, c_lb, c_norm_g, d_w, d_scale, w_out, norm2_g, w_up, ffn_conv_w, ffn_conv_b, w_down, norm3_g, w_pe, w_pg, final_g, loss_target, m_norm1_g, m_w_in, m_a_ln_g, m_a_ln_b, m_a_ws, m_a_bs, m_b_conv_w, m_b_conv_b, m_b_wa, m_b_ba, m_b_wx, m_b_bx, m_b_lam, m_c_lb, m_c_norm_g, m_d_w, m_d_scale, m_w_out, m_norm2_g, m_w_up, m_ffn_conv_w, m_ffn_conv_b, m_w_down, m_norm3_g, m_w_pe, m_w_pg, m_final_g, v_norm1_g, v_w_in, v_a_ln_g, v_a_ln_b, v_a_ws, v_a_bs, v_b_conv_w, v_b_conv_b, v_b_wa, v_b_ba, v_b_wx, v_b_bx, v_b_lam, v_c_lb, v_c_norm_g, v_d_w, v_d_scale, v_w_out, v_norm2_g, v_w_up, v_ffn_conv_w, v_ffn_conv_b, v_w_down, v_norm3_g, v_w_pe, v_w_pg, v_final_g))


SMALL_PER_LAYER = tuple(n for n in SMALL if n != "final_g")
GATHERED = BIG + SMALL_SHARDED
GATHER_FIRST = ("w_in", "b_conv_w")
GATHER_REST = tuple(n for n in GATHERED if n not in GATHER_FIRST)
PUSH_EARLY = ("w_pe", "w_pg", "w_down", "w_up")
PUSH_MID = ("w_out",)
PUSH_LATE = ("w_in",)
SMALL_MID = tuple(n for n in SMALL_PER_LAYER if n != "norm1_g")


def _cols_to_slabs(m):
    r, c4 = m.shape
    return jnp.moveaxis(m.reshape(r, N_CHIPS, c4 // N_CHIPS), 1, 0)


def _slabs_to_cols(s):
    return jnp.moveaxis(s, 0, 1).reshape(s.shape[1], -1)


def _pack_small(parts):
    flat = jnp.concatenate([p.reshape(-1) for p in parts])
    return jnp.pad(flat, (0, (-flat.shape[0]) % 1024)).reshape(-1, 128)


def _step(args):
    a = dict(zip(ARGS, args, strict=True))
    x0 = a["x"][0]
    target = a["loss_target"][0]
    nl = a["norm1_g"].shape[0]
    t, d = x0.shape
    f = a["w_down"].shape[1] * N_CHIPS
    cx, cy, cc = _me()
    my_shard = 2 * cx + cy
    shards = {n: a[n].astype(BF16) for n in BIG}
    shards.update({n: a[n] for n in SMALL_SHARDED})

    def start_gather(l, names, after, tag):
        return _gather_start([shards[n][l] for n in names], after, f"gather_start_{l}{tag}")

    def finish_gather(l, names, handle, after, tag):
        send, recv, srcs, lands = handle
        lands = _gather_wait(send, recv, srcs, lands, after, f"gather_wait_{l}{tag}")
        w = {}
        for n, land in zip(names, lands):
            full = lax.dynamic_update_slice(land, shards[n][l][None], (my_shard, 0, 0))
            if n in ROW_SHARDED:
                w[n] = full.reshape(-1, full.shape[-1])
            elif n in ("w_up", "w_pe"):
                w[n] = full
            else:
                w[n] = _slabs_to_cols(full)
        return w

    lbs, lbs_vjp = jax.vjp(_lower_bounds, a["c_lb"])
    tril = jnp.tril(jnp.ones((GMLP_CHUNK, GMLP_CHUNK), F32))

    def layer_params(l, w):
        q = {}
        q["wm"] = (a["a_ws"][l] * tril).astype(BF16)
        q["wm_t"] = jnp.swapaxes(q["wm"], 1, 2)
        q["bs_t"] = jnp.repeat(a["a_bs"][l].T, HEAD_DIM, axis=1)
        for nm in ("b_wa", "b_wx", "d_w"):
            bd = _block_diag(a[nm][l]).astype(BF16)
            q[nm], q[nm + "_t"] = bd, bd.T
        for nm in ("a_ln_g", "a_ln_b", "b_conv_b", "b_ba", "b_bx", "b_lam", "d_scale"):
            q[nm] = a[nm][l].reshape(1, W_GRP)
        q["lb"] = lbs[l].reshape(1, W_GRP)
        q["ng"] = jnp.tile(a["c_norm_g"][l], N_HEADS).reshape(1, W_GRP)
        q["ffn_conv_b"] = a["ffn_conv_b"][l].reshape(1, 2 * f)
        q.update(w)
        return q

    saved, weights, params = [], [], []
    first_handle, _ = start_gather(0, GATHER_FIRST, x0, "a")
    rest_handle, _ = start_gather(0, GATHER_REST, x0, "b")
    xl = x0
    for l in range(nl):
        if l == 0:
            w = finish_gather(0, GATHER_FIRST, first_handle, xl, "a")
        else:
            w = finish_gather(l, GATHERED, next_handle, xl, "")
        s = {"x0": xl}
        s["h1"] = _rms_fwd(xl, a["norm1_g"][l], "rms1_fwd")
        token = None
        if 0 < l < nl - 1:
            next_handle, token = start_gather(l + 1, GATHERED, s["h1"], "")
        s["z"] = _mm(s["h1"], w["w_in"], "nn", out_dtype=F32, name="mm_z", after=token)
        q = layer_params(l, w)
        mix = _mix_a_fwd(s["z"], d, q["a_ln_g"], q["a_ln_b"], q["wm"], q["bs_t"])
        mix, s["hs"] = _mix_b_fwd(s["z"], mix, q["b_conv_w"], q["b_conv_b"], q["b_wa"], q["b_wx"], q["b_ba"], q["b_bx"],
                                  q["b_lam"])
        mix, s["o_pre"], s["states"] = _mix_c_fwd(s["z"], mix, q["lb"], q["ng"])
        s["mix"] = _mix_d_fwd(s["z"], mix, q["d_w"], q["d_scale"])
        token = None
        if l == 0:
            w.update(finish_gather(0, GATHER_REST, rest_handle, s["mix"], "b"))
            q.update(w)
            if nl > 1:
                next_handle, token = start_gather(1, GATHERED, w["w_out"], "")
        s["x1"] = _mm(s["mix"], w["w_out"], "nn", res=xl, out_dtype=F32, name="mm_out", after=token)
        s["h2"] = _rms_fwd(s["x1"], a["norm2_g"][l], "rms2_fwd")
        s["hf_g"] = _mm(s["h2"], w["w_up"], "nn", b_slabs=True, n=f, out_dtype=F32, name="mm_up_g")
        s["hf_v"] = _mm(s["h2"], w["w_up"], "nn", b_slabs=True, n=f, b_noff=f, out_dtype=F32, name="mm_up_v")
        s["act"] = _ffn_act_fwd(s["hf_g"], s["hf_v"], q["ffn_conv_w"], q["ffn_conv_b"])
        s["x2"] = _mm(s["act"], w["w_down"], "nn", res=s["x1"], out_dtype=F32, name="mm_down")
        s["h3"] = _rms_fwd(s["x2"], a["norm3_g"][l], "rms3_fwd")
        s["pre"] = _mm(s["h3"], w["w_pg"], "nn", out_dtype=F32, name="mm_pg")
        s["pe"] = _mm(a["p"][l, 0], w["w_pe"], "nn", b_slabs=True, out_dtype=F32, name="mm_pe")
        xl = _ple_fwd(s["x2"], s["pe"], s["pre"])
        saved.append(s)
        weights.append(w)
        params.append(q)

    dx, g_final, loss = _final_loss(xl, a["final_g"], target)
    loss = lax.psum(loss[0, 0], ("x", "y", "c"))

    stacked = {n: None for n in BIG}
    small_sums = {}

    def finish_push(l, names, handle, tag, after):
        send, recv, srcs, slots = handle
        srcs, slots = _push_wait(send, recv, srcs, slots, after, f"push_wait_{l}{tag}")
        halves = []
        for n, g, sl in zip(names, srcs, slots):
            hr = g.shape[1] // 2
            own = lax.dynamic_slice(g, (my_shard, cc * hr, 0), (1, hr, g.shape[2]))[0]
            halves.append(_slot_sum(own, sl, "sum_" + n))
        for n, g in zip(names, _swap_halves(halves, "swap_halves_" + tag)):
            stacked[n] = _adamw_layer(a[n], g, a["m_" + n], a["v_" + n], l, stacked[n], "adamw_" + n)
        if len(srcs) > len(names):
            by_sender = lax.dynamic_update_slice(slots[-1], srcs[-1][None], (2 * my_shard + cc, 0, 0))
            small_sums[l, tag] = _slot_sum(None, by_sender, "sum_small_" + tag)

    pending = []
    token = None
    for l in reversed(range(nl)):
        q, s, w = params[l], saved[l], weights[l]
        gs = {}
        dpe, dpre = _ple_bwd(dx, s["pe"], s["pre"], after=token)
        g_pe = _mm(a["p"][l, 0], dpe, "tn", out_dtype=BF16, name="mm_dwpe", out_slabs=N_CHIPS)
        g_pg = _mm(s["h3"], dpre, "tn", out_dtype=BF16, name="mm_dwpg")
        dh3 = _mm(dpre, w["w_pg"], "nt", out_dtype=BF16, name="mm_dh3")
        dx2, gs["norm3_g"] = _rms_bwd(dh3, s["x2"], a["norm3_g"][l], dx, "rms3_bwd")
        g_down = _mm(s["act"], dx2, "tn", out_dtype=BF16, name="mm_dwdown")
        dact = _mm(dx2, w["w_down"], "nt", out_dtype=BF16, name="mm_dact")
        dhf_g, dhf_v, dcw_g, dcw_v, dcb_g, dcb_v = _ffn_act_bwd(dact, s["hf_g"], s["hf_v"], q["ffn_conv_w"], q["ffn_conv_b"])
        gs["ffn_conv_w"] = jnp.concatenate([dcw_g, dcw_v], axis=1)
        gs["ffn_conv_b"] = jnp.concatenate([dcb_g, dcb_v], axis=1)
        g_up = _mm(s["h2"], dhf_g, "tn", out_dtype=BF16, name="mm_dwup_g", out_slabs=N_CHIPS, out_n=2 * f)
        g_up = _mm(s["h2"], dhf_v, "tn", out_dtype=BF16, name="mm_dwup_v", out_slabs=N_CHIPS, out_n=2 * f, o_noff=f, out_buf=g_up)
        early = {"w_pe": g_pe, "w_up": g_up, "w_pg": g_pg.reshape(N_CHIPS, -1, g_pg.shape[-1]),
                 "w_down": g_down.reshape(N_CHIPS, -1, g_down.shape[-1])}
        early_handle, token = _push_start([early[n] for n in PUSH_EARLY], None, f"push_start_{l}a")
        dh2 = _mm(dhf_g, w["w_up"], "nt", b_slabs=True, out_dtype=F32, name="mm_dh2_g", after=token)
        dh2 = _mm(dhf_v, w["w_up"], "nt", b_slabs=True, b_koff=f, res=dh2, out_dtype=F32, name="mm_dh2_v")
        dx1, gs["norm2_g"] = _rms_bwd(dh2, s["x1"], a["norm2_g"][l], dx2, "rms2_bwd")
        g_out = _mm(s["mix"], dx1, "tn", out_dtype=BF16, name="mm_dwout")
        dmix = _mm(dx1, w["w_out"], "nt", out_dtype=F32, name="mm_dmix")
        dz, gs["a_ln_g"], gs["a_ln_b"], dws, dbs_t = _mix_a_bwd(s["z"], dmix, q["a_ln_g"], q["a_ln_b"], q["wm"], q["wm_t"],
                                                               q["bs_t"])
        gs["a_ws"] = dws * tril
        gs["a_bs"] = dbs_t.reshape(GMLP_CHUNK, N_HEADS, HEAD_DIM).sum(-1).T
        dz, gs["b_conv_w"], gs["b_conv_b"], dwa, dwx, gs["b_ba"], gs["b_bx"], gs["b_lam"] = _mix_b_bwd(
            s["z"], dz, dmix, s["hs"], q["b_conv_w"], q["b_conv_b"], q["b_wa"], q["b_wx"], q["b_wa_t"], q["b_wx_t"],
            q["b_ba"], q["b_bx"], q["b_lam"])
        gs["b_wa"], gs["b_wx"] = _diag_blocks(dwa), _diag_blocks(dwx)
        dz, gs["c_lb"], dng = _mix_c_bwd(s["z"], dz, dmix, s["o_pre"], s["states"], q["lb"], q["ng"])
        gs["c_norm_g"] = dng.reshape(N_HEADS, HEAD_DIM).sum(0)
        dz, dwd, gs["d_scale"] = _mix_d_bwd(s["z"], dz, dmix, q["d_w"], q["d_w_t"], q["d_scale"])
        gs["d_w"] = _diag_blocks(dwd)
        small = [gs[n] for n in SMALL_MID] + ([g_final] if l == nl - 1 else [])
        mid_handle, token = _push_start([g_out.reshape(N_CHIPS, -1, g_out.shape[-1])], _pack_small(small), f"push_start_{l}b")
        g_in = _mm(s["h1"], dz, "tn", out_dtype=BF16, name="mm_dwin", after=token)
        dh1 = _mm(dz, w["w_in"], "nt", out_dtype=BF16, name="mm_dh1")
        dx, gs["norm1_g"] = _rms_bwd(dh1, s["x0"], a["norm1_g"][l], dx1, "rms1_bwd")

        late_handle, token = _push_start([_cols_to_slabs(g_in)], _pack_small([gs["norm1_g"]]), f"push_start_{l}c")
        for push in pending:
            finish_push(*push, dx)
        pending = [(l, PUSH_EARLY, early_handle, "a"), (l, PUSH_MID, mid_handle, "b"), (l, PUSH_LATE, late_handle, "c")]
    for push in pending:
        finish_push(*push, dx)
    grad_x = dx[None]

    def small_shape(n):
        return a[n].shape[1:-1] + (a[n].shape[-1] * N_CHIPS,) if n in SMALL_SHARDED else a[n].shape[1:]

    per_layer = {n: [] for n in SMALL_PER_LAYER}
    for l in range(nl):
        per_layer["norm1_g"].append(small_sums[l, "c"].reshape(-1)[:d])
        vec, off = small_sums[l, "b"].reshape(-1), 0
        for n in SMALL_MID:
            shape = small_shape(n)
            size = 1
            for dim in shape:
                size *= dim
            per_layer[n].append(vec[off:off + size].reshape(shape))
            off += size
        if l == nl - 1:
            grad_final = vec[off:off + d]
    grads = {n: jnp.stack(per_layer[n]) for n in SMALL_PER_LAYER}
    grads["c_lb"] = lbs_vjp(grads["c_lb"])[0]
    grads["final_g"] = grad_final
    for n in SMALL_SHARDED:
        cs = a[n].shape[-1]
        grads[n] = lax.dynamic_slice_in_dim(grads[n], my_shard * cs, cs, axis=2)

    outs = {}
    for n in WEIGHTS:
        if n in BIG:
            outs[n] = stacked[n]
        else:
            outs[n] = (grads[n],) + _adamw(a[n], grads[n], a["m_" + n], a["v_" + n], "adamw_" + n)
    return (loss, grad_x, *[outs[n][0] for n in WEIGHTS], *[outs[n][1] for n in WEIGHTS], *[outs[n][2] for n in WEIGHTS],
            *[outs[n][3] for n in WEIGHTS])
```

```python
import functools

import jax
import jax.numpy as jnp
from jax import lax
from jax.experimental import pallas as pl
from jax.experimental.pallas import tpu as pltpu

F32 = jnp.float32
BF16 = jnp.bfloat16
EPS = 1e-6
HEAD_DIM = 64
N_HEADS = 4
W_GRP = HEAD_DIM * N_HEADS
GMLP_CHUNK = 128
HGRN_CHUNK = 64
HGRN_UNROLL = 4
RGLRU_C = 8.0
POOL_HALO = 16
EXP_CLAMP = 80.0
ADAM_LR, ADAM_B1, ADAM_B2, ADAM_EPS, ADAM_WD, ADAM_STEP = 0.001, 0.9, 0.999, 1e-08, 0.01, 10
VMEM_LIMIT_BYTES = 56 * 1024 * 1024
TILE_PREFS = (1024, 1408, 768, 512, 256, 128)
ROW_TILE_PREFS = (512, 256, 128, 64, 32, 16, 8)
MESH_ID = pl.DeviceIdType.MESH
N_DEV = 8


def _pick(n, prefs=TILE_PREFS):
    for p in prefs:
        if n % p == 0:
            return p
    return n


def _cp(*sem):
    return pltpu.CompilerParams(dimension_semantics=sem if sem else None, vmem_limit_bytes=VMEM_LIMIT_BYTES)


def _sds(shape, dtype):
    return jax.ShapeDtypeStruct(tuple(shape), dtype)


_GELU_C = 0.7978845608028654
_GELU_A = 0.044715


def _gelu(x):
    return 0.5 * x * (1.0 + jnp.tanh(_GELU_C * (x + _GELU_A * x * x * x)))


def _gelu_and_grad(x):
    t = jnp.tanh(_GELU_C * (x + _GELU_A * x * x * x))
    g = 0.5 * x * (1.0 + t)
    dg = 0.5 * (1.0 + t) + 0.5 * x * (1.0 - t * t) * _GELU_C * (1.0 + 3.0 * _GELU_A * x * x)
    return g, dg


def _sigmoid(x):
    return 1.0 / (1.0 + jnp.exp(-x))


def _dot(a, b):
    return jnp.dot(a, b, preferred_element_type=F32)


def _dot_nt(a, b):
    return lax.dot_general(a, b, (((1,), (1,)), ((), ())), preferred_element_type=F32)


def _dot_tn(a, b):
    return lax.dot_general(a, b, (((0,), (0,)), ((), ())), preferred_element_type=F32)


def _split3(x):
    hi = x.astype(BF16)
    r1 = x - hi.astype(F32)
    mid = r1.astype(BF16)
    lo = (r1 - mid.astype(F32)).astype(BF16)
    return hi, mid, lo


def _dot_f32_rhs_exact(x, m_bf16):
    hi, mid, lo = _split3(x)
    return _dot(hi, m_bf16) + _dot(mid, m_bf16) + _dot(lo, m_bf16)


def _dot_f32_lhs_exact(m_bf16, x):
    hi, mid, lo = _split3(x)
    return _dot(m_bf16, hi) + _dot(m_bf16, mid) + _dot(m_bf16, lo)


def _head_masks(width=W_GRP):
    lane = lax.broadcasted_iota(jnp.int32, (1, width), 1)
    return [(lane >= h * HEAD_DIM) & (lane < (h + 1) * HEAD_DIM) for h in range(N_HEADS)]


def _block_mask(n=W_GRP):
    r = lax.broadcasted_iota(jnp.int32, (n, n), 0)
    c = lax.broadcasted_iota(jnp.int32, (n, n), 1)
    m = None
    for h in range(N_HEADS):
        mh = (r >= h * HEAD_DIM) & (r < (h + 1) * HEAD_DIM) & (c >= h * HEAD_DIM) & (c < (h + 1) * HEAD_DIM)
        m = mh if m is None else (m | mh)
    return m


def _mm(a, b, mode, *, out_dtype, name, res=None, b_slabs=False, n=None, b_noff=0, b_koff=0,
        out_slabs=0, out_buf=None, out_n=None, o_noff=0, after=None):
    if mode == "tn":
        k_dim, m_dim = a.shape
    else:
        m_dim, k_dim = a.shape
    if mode == "nt":
        n_dim = b.shape[-2]
    else:
        n_dim = n if n is not None else (b.shape[0] * b.shape[2] if b_slabs else b.shape[1])
    n_total = out_n if out_n is not None else n_dim
    tm, tn, tk = _pick(m_dim), _pick(n_dim), _pick(k_dim)
    if b_slabs and mode == "nt":
        tk = _pick(b.shape[2])
    elif b_slabs:
        tn = _pick(b.shape[2])
    elif out_slabs:
        tn = _pick(n_total // out_slabs)
    nk = k_dim // tk
    assert b_noff % tn == 0 and b_koff % tk == 0 and o_noff % tn == 0 and n_dim % tn == 0 and k_dim % tk == 0
    bn0, bk0, on0 = b_noff // tn, b_koff // tk, o_noff // tn
    dims = {"nn": (((1,), (0,)), ((), ())), "nt": (((1,), (1,)), ((), ())), "tn": (((0,), (0,)), ((), ()))}[mode]

    if mode == "tn":
        a_spec = pl.BlockSpec((tk, tm), lambda i, j, k: (k, i))
    else:
        a_spec = pl.BlockSpec((tm, tk), lambda i, j, k: (i, k))
    if not b_slabs:
        if mode == "nt":
            b_spec = pl.BlockSpec((tn, tk), lambda i, j, k: (j + bn0, k + bk0))
        else:
            b_spec = pl.BlockSpec((tk, tn), lambda i, j, k: (k + bk0, j + bn0))
    elif mode == "nt":
        bper = b.shape[2] // tk
        b_spec = pl.BlockSpec((None, tn, tk), lambda i, j, k: ((k + bk0) // bper, j, (k + bk0) % bper))
    else:
        bper = b.shape[2] // tn
        b_spec = pl.BlockSpec((None, tk, tn), lambda i, j, k: ((j + bn0) // bper, k, (j + bn0) % bper))
    in_specs = [a_spec, b_spec]
    args = [a, b]
    if res is not None:
        in_specs.append(pl.BlockSpec((tm, tn), lambda i, j, k: (i, j)))
        args.append(res)
    if out_slabs:
        oper = n_total // out_slabs // tn
        out_shape = _sds((out_slabs, m_dim, n_total // out_slabs), out_dtype)
        out_spec = pl.BlockSpec((None, tm, tn), lambda i, j, k: ((j + on0) // oper, i, (j + on0) % oper))
    else:
        out_shape = _sds((m_dim, n_total), out_dtype)
        out_spec = pl.BlockSpec((tm, tn), lambda i, j, k: (i, j + on0))
    aliases = {}
    if out_buf is not None:
        in_specs.append(pl.BlockSpec(memory_space=pl.ANY))
        args.append(out_buf)
        aliases = {len(args) - 1: 0}
    if after is not None:
        in_specs.append(pl.BlockSpec(memory_space=pl.ANY))
        args.append(after)
    has_res = res is not None
    n_in = len(args)

    def body(*refs):
        a_ref, b_ref = refs[0], refs[1]
        res_ref = refs[2] if has_res else None
        o_ref = refs[n_in]
        acc_ref = refs[-1] if nk > 1 else None
        part = lax.dot_general(a_ref[...].astype(BF16), b_ref[...].astype(BF16), dims, preferred_element_type=F32)

        def finish(v):
            if has_res:
                v = v + res_ref[...]
            o_ref[...] = v.astype(o_ref.dtype)

        if nk == 1:
            finish(part)
        else:
            kk = pl.program_id(2)

            @pl.when(kk == 0)
            def _():
                acc_ref[...] = part

            @pl.when(kk > 0)
            def _():
                acc_ref[...] += part

            @pl.when(kk == nk - 1)
            def _():
                finish(acc_ref[...])

    return pl.pallas_call(
        body, grid=(m_dim // tm, n_dim // tn, nk), in_specs=in_specs, out_specs=out_spec, out_shape=out_shape,
        scratch_shapes=[pltpu.VMEM((tm, tn), F32)] if nk > 1 else [],
        input_output_aliases=aliases, name=name, compiler_params=_cp("parallel", "parallel", "arbitrary"),
    )(*args)


def _rms_fwd(x, g, name):
    t, d = x.shape
    tm = _pick(t, ROW_TILE_PREFS)

    def body(x_ref, g_ref, o_ref):
        xv = x_ref[...]
        r = lax.rsqrt(jnp.mean(xv * xv, axis=-1, keepdims=True) + EPS)
        o_ref[...] = (xv * r * g_ref[...]).astype(o_ref.dtype)

    return pl.pallas_call(
        body, grid=(t // tm,),
        in_specs=[pl.BlockSpec((tm, d), lambda i: (i, 0)), pl.BlockSpec((1, d), lambda i: (0, 0))],
        out_specs=pl.BlockSpec((tm, d), lambda i: (i, 0)), out_shape=_sds((t, d), BF16),
        name=name, compiler_params=_cp("parallel"),
    )(x, g.reshape(1, d))


def _rms_bwd(dh, x, g, dres, name, after=None):
    t, d = x.shape
    tm = _pick(t, ROW_TILE_PREFS)
    extra = [] if after is None else [after]

    def body(dh_ref, x_ref, g_ref, dres_ref, *rest):
        dx_ref, dg_ref = rest[-2:]
        i = pl.program_id(0)
        xv = x_ref[...]
        dy = dh_ref[...].astype(F32)
        r = lax.rsqrt(jnp.mean(xv * xv, axis=-1, keepdims=True) + EPS)
        dyg = dy * g_ref[...]
        dot = jnp.mean(dyg * xv, axis=-1, keepdims=True)
        dx_ref[...] = dres_ref[...] + r * dyg - xv * (r * r * r) * dot
        part = jnp.sum(dy * xv * r, axis=0, keepdims=True)

        @pl.when(i == 0)
        def _():
            dg_ref[...] = part

        @pl.when(i > 0)
        def _():
            dg_ref[...] += part

    row = pl.BlockSpec((tm, d), lambda i: (i, 0))
    vec = pl.BlockSpec((1, d), lambda i: (0, 0))
    return pl.pallas_call(
        body, grid=(t // tm,), in_specs=[row, row, vec, row] + [pl.BlockSpec(memory_space=pl.ANY)] * len(extra),
        out_specs=[row, vec], out_shape=[_sds((t, d), F32), _sds((1, d), F32)], name=name, compiler_params=_cp("arbitrary"),
    )(dh, x, g.reshape(1, d), dres, *extra)


def _final_loss(x, g, target):
    t, d = x.shape
    tm = _pick(t, ROW_TILE_PREFS)

    def body(x_ref, g_ref, t_ref, dx_ref, dg_ref, loss_ref):
        i = pl.program_id(0)
        xv = x_ref[...]
        gv = g_ref[...]
        r = lax.rsqrt(jnp.mean(xv * xv, axis=-1, keepdims=True) + EPS)
        err = xv * r * gv - t_ref[...]
        lpart = (0.5 / d) * jnp.sum(jnp.sum(err * err, axis=1, keepdims=True), axis=0, keepdims=True)
        dy = err * (1.0 / d)
        dyg = dy * gv
        dot = jnp.mean(dyg * xv, axis=-1, keepdims=True)
        dx_ref[...] = r * dyg - xv * (r * r * r) * dot
        part = jnp.sum(dy * xv * r, axis=0, keepdims=True)

        @pl.when(i == 0)
        def _():
            dg_ref[...] = part
            loss_ref[...] = lpart

        @pl.when(i > 0)
        def _():
            dg_ref[...] += part
            loss_ref[...] += lpart

    row = pl.BlockSpec((tm, d), lambda i: (i, 0))
    vec = pl.BlockSpec((1, d), lambda i: (0, 0))
    return pl.pallas_call(
        body, grid=(t // tm,), in_specs=[row, vec, row], out_specs=[row, vec, pl.BlockSpec((1, 1), lambda i: (0, 0))],
        out_shape=[_sds((t, d), F32), _sds((1, d), F32), _sds((1, 1), F32)], name="final_loss",
        compiler_params=_cp("arbitrary"),
    )(x, g.reshape(1, d), target)


def _shift_down(ext, k, halo):
    return pltpu.roll(ext, k, 0)[halo:]


def _shift_up(ext, k, tm):
    return pltpu.roll(ext, ext.shape[0] - k, 0)[:tm]


def _ffn_tiles(t, f):
    return _pick(t, ROW_TILE_PREFS), _pick(f, (256, 128))


def _ffn_act_fwd(hf_g, hf_v, conv_w, conv_b):
    t, f = hf_g.shape
    tm, cn = _ffn_tiles(t, f)
    nf = f // cn

    def body(g_ref, v_ref, wg_ref, wv_ref, bg_ref, bv_ref, o_ref, ext_ref, hg_ref, hv_ref):
        i = pl.program_id(1)

        @pl.when(i == 0)
        def _():
            hg_ref[...] = jnp.zeros_like(hg_ref)
            hv_ref[...] = jnp.zeros_like(hv_ref)

        def conv(x_ref, halo_ref, w_ref, b_ref):
            ext_ref[0:8, :] = halo_ref[...]
            ext_ref[8:, :] = x_ref[...]
            halo_ref[...] = x_ref[tm - 8:tm, :]
            ext = ext_ref[...]
            w = w_ref[...]
            return b_ref[...] + w[2:3, :] * ext[8:] + w[1:2, :] * _shift_down(ext, 1, 8) + w[0:1, :] * _shift_down(ext, 2, 8)

        gc = conv(g_ref, hg_ref, wg_ref, bg_ref)
        vc = conv(v_ref, hv_ref, wv_ref, bv_ref)
        o_ref[...] = (_gelu(gc) * vc).astype(o_ref.dtype)

    blk = pl.BlockSpec((tm, cn), lambda j, i: (i, j))
    return pl.pallas_call(
        body, grid=(nf, t // tm),
        in_specs=[blk, blk, pl.BlockSpec((3, cn), lambda j, i: (0, j)), pl.BlockSpec((3, cn), lambda j, i: (0, j + nf)),
                  pl.BlockSpec((1, cn), lambda j, i: (0, j)), pl.BlockSpec((1, cn), lambda j, i: (0, j + nf))],
        out_specs=blk, out_shape=_sds((t, f), BF16),
        scratch_shapes=[pltpu.VMEM((tm + 8, cn), F32), pltpu.VMEM((8, cn), F32), pltpu.VMEM((8, cn), F32)],
        name="ffn_act_fwd", compiler_params=_cp("parallel", "arbitrary"),
    )(hf_g, hf_v, conv_w, conv_w, conv_b, conv_b)


def _ffn_act_bwd(dact, hf_g, hf_v, conv_w, conv_b):
    t, f = hf_g.shape
    tm, cn = _ffn_tiles(t, f)
    nf, nt = f // cn, t // tm
    hb = tm // 8

    def body(da_ref, g_ref, v_ref, gh_ref, vh_ref, wg_ref, wv_ref, bg_ref, bv_ref,
             dg_ref, dv_ref, dwg_ref, dwv_ref, dbg_ref, dbv_ref, ext_ref, cg_ref, cv_ref):
        i = pl.program_id(1)
        first_tile = i == nt - 1

        @pl.when(i == 0)
        def _():
            cg_ref[...] = jnp.zeros_like(cg_ref)
            cv_ref[...] = jnp.zeros_like(cv_ref)
            dwg_ref[...] = jnp.zeros_like(dwg_ref)
            dwv_ref[...] = jnp.zeros_like(dwv_ref)
            dbg_ref[...] = jnp.zeros_like(dbg_ref)
            dbv_ref[...] = jnp.zeros_like(dbv_ref)

        def shifted(x_ref, halo_ref):
            ext_ref[0:8, :] = jnp.where(first_tile, 0.0, halo_ref[...])
            ext_ref[8:, :] = x_ref[...]
            ext = ext_ref[0:tm + 8, :]
            return ext[8:], _shift_down(ext, 1, 8), _shift_down(ext, 2, 8)

        wg, wv = wg_ref[...], wv_ref[...]
        g0, g1, g2 = shifted(g_ref, gh_ref)
        gc = bg_ref[...] + wg[2:3, :] * g0 + wg[1:2, :] * g1 + wg[0:1, :] * g2
        v0, v1, v2 = shifted(v_ref, vh_ref)
        vc = bv_ref[...] + wv[2:3, :] * v0 + wv[1:2, :] * v1 + wv[0:1, :] * v2
        da = da_ref[...].astype(F32)
        gel, dgel = _gelu_and_grad(gc)
        dgc = da * vc * dgel
        dvc = da * gel

        def back(dc, carry_ref, w, x0, x1, x2, dx_ref, dw_ref, db_ref):
            ext_ref[0:tm, :] = dc
            ext_ref[tm:tm + 8, :] = carry_ref[...]
            carry_ref[...] = dc[0:8, :]
            ext = ext_ref[0:tm + 8, :]
            dx = w[2:3, :] * dc + w[1:2, :] * _shift_up(ext, 1, tm) + w[0:1, :] * _shift_up(ext, 2, tm)
            dx_ref[...] = dx.astype(dx_ref.dtype)
            dw_ref[0:1, :] += jnp.sum(dc * x2, axis=0, keepdims=True)
            dw_ref[1:2, :] += jnp.sum(dc * x1, axis=0, keepdims=True)
            dw_ref[2:3, :] += jnp.sum(dc * x0, axis=0, keepdims=True)
            db_ref[...] += jnp.sum(dc, axis=0, keepdims=True)

        back(dgc, cg_ref, wg, g0, g1, g2, dg_ref, dwg_ref, dbg_ref)
        back(dvc, cv_ref, wv, v0, v1, v2, dv_ref, dwv_ref, dbv_ref)

    blk = pl.BlockSpec((tm, cn), lambda j, i: (nt - 1 - i, j))
    halo = pl.BlockSpec((8, cn), lambda j, i: (jnp.maximum((nt - 1 - i) * hb - 1, 0), j))
    w3g = pl.BlockSpec((3, cn), lambda j, i: (0, j))
    w3v = pl.BlockSpec((3, cn), lambda j, i: (0, j + nf))
    b1g = pl.BlockSpec((1, cn), lambda j, i: (0, j))
    b1v = pl.BlockSpec((1, cn), lambda j, i: (0, j + nf))
    acc3 = pl.BlockSpec((3, cn), lambda j, i: (0, j))
    acc1 = pl.BlockSpec((1, cn), lambda j, i: (0, j))
    return pl.pallas_call(
        body, grid=(nf, nt),
        in_specs=[blk, blk, blk, halo, halo, w3g, w3v, b1g, b1v],
        out_specs=[blk, blk, acc3, acc3, acc1, acc1],
        out_shape=[_sds((t, f), BF16), _sds((t, f), BF16), _sds((3, f), F32), _sds((3, f), F32),
                   _sds((1, f), F32), _sds((1, f), F32)],
        scratch_shapes=[pltpu.VMEM((tm + 8, cn), F32), pltpu.VMEM((8, cn), F32), pltpu.VMEM((8, cn), F32)],
        name="ffn_act_bwd", compiler_params=_cp("parallel", "arbitrary"),
    )(dact, hf_g, hf_v, hf_g, hf_v, conv_w, conv_w, conv_b, conv_b)


def _ple_fwd(x2, pe, pre):
    t, d = x2.shape
    tm = _pick(t, ROW_TILE_PREFS)

    def body(x_ref, pe_ref, pre_ref, o_ref):
        o_ref[...] = x_ref[...] + pe_ref[...] * _sigmoid(pre_ref[...])

    row = pl.BlockSpec((tm, d), lambda i: (i, 0))
    return pl.pallas_call(body, grid=(t // tm,), in_specs=[row, row, row], out_specs=row,
                          out_shape=_sds((t, d), F32), name="ple_fwd", compiler_params=_cp("parallel"))(x2, pe, pre)


def _ple_bwd(dx3, pe, pre, after=None):
    t, d = dx3.shape
    tm = _pick(t, ROW_TILE_PREFS)

    def body(dx_ref, pe_ref, pre_ref, *rest):
        dpe_ref, dpre_ref = rest[-2:]
        gate = _sigmoid(pre_ref[...])
        dx = dx_ref[...]
        dpe_ref[...] = (dx * gate).astype(dpe_ref.dtype)
        dpre_ref[...] = (dx * pe_ref[...] * gate * (1.0 - gate)).astype(dpre_ref.dtype)

    row = pl.BlockSpec((tm, d), lambda i: (i, 0))
    extra = [] if after is None else [after]
    return pl.pallas_call(body, grid=(t // tm,), in_specs=[row, row, row] + [pl.BlockSpec(memory_space=pl.ANY)] * len(extra),
                          out_specs=[row, row], out_shape=[_sds((t, d), BF16), _sds((t, d), BF16)], name="ple_bwd",
                          compiler_params=_cp("parallel"))(dx3, pe, pre, *extra)


def _mix_tm(t):
    return _pick(t, (512, 256, 128))


def _zblk(tm, col, rev_nt=None):
    if rev_nt is None:
        return pl.BlockSpec((tm, W_GRP), lambda i: (i, col))
    return pl.BlockSpec((tm, W_GRP), lambda i: (rev_nt - 1 - i, col))


def _full(shape):
    nd = len(shape)
    return pl.BlockSpec(tuple(shape), lambda i: (0,) * nd)


def _gmlp_sv(wm_ref, vnc, bs, hm):
    sv = bs
    for h in range(N_HEADS):
        sv = sv + jnp.where(hm[h], _dot(wm_ref[h], vnc), 0.0)
    return sv


def _layernorm(v, g, b):
    mu = jnp.mean(v, axis=-1, keepdims=True)
    vc = v - mu
    rs = lax.rsqrt(jnp.mean(vc * vc, axis=-1, keepdims=True) + EPS)
    xhat = vc * rs
    return xhat, rs, xhat * g + b


def _mix_a_fwd(z, d_mix, ln_g, ln_b, wm, bs_t):
    t = z.shape[0]
    tm = _mix_tm(t)

    def body(u_ref, v_ref, g_ref, b_ref, wm_ref, bs_ref, o_ref):
        hm = _head_masks()
        ug = _gelu(u_ref[...])
        _, _, vn = _layernorm(_gelu(v_ref[...]), g_ref[...], b_ref[...])
        vnb = vn.astype(BF16)
        for n in range(tm // GMLP_CHUNK):
            sl = slice(n * GMLP_CHUNK, (n + 1) * GMLP_CHUNK)
            o_ref[sl, :] = ug[sl] * _gmlp_sv(wm_ref, vnb[sl], bs_ref[...], hm)

    return pl.pallas_call(
        body, grid=(t // tm,),
        in_specs=[_zblk(tm, 0), _zblk(tm, 1), _full((1, W_GRP)), _full((1, W_GRP)), _full(wm.shape), _full(bs_t.shape)],
        out_specs=_zblk(tm, 0), out_shape=_sds((t, d_mix), F32), name="mix_a_fwd", compiler_params=_cp("parallel"),
    )(z, z, ln_g, ln_b, wm, bs_t)


def _mix_a_bwd(z, dmix, ln_g, ln_b, wm, wm_t, bs_t):
    t, zc = z.shape
    tm = _mix_tm(t)

    def body(u_ref, v_ref, dy_ref, g_ref, b_ref, wm_ref, wmt_ref, bs_ref, dz_ref, dg_ref, db_ref, dws_ref, dbs_ref):
        i = pl.program_id(0)

        @pl.when(i == 0)
        def _():
            dg_ref[...] = jnp.zeros_like(dg_ref)
            db_ref[...] = jnp.zeros_like(db_ref)
            dws_ref[...] = jnp.zeros_like(dws_ref)
            dbs_ref[...] = jnp.zeros_like(dbs_ref)

        hm = _head_masks()
        ug, dug = _gelu_and_grad(u_ref[...])
        vg, dvg = _gelu_and_grad(v_ref[...])
        gv = g_ref[...]
        xhat, rs, vn = _layernorm(vg, gv, b_ref[...])
        vnb = vn.astype(BF16)
        dy = dy_ref[...]
        for n in range(tm // GMLP_CHUNK):
            sl = slice(n * GMLP_CHUNK, (n + 1) * GMLP_CHUNK)
            vnc = vnb[sl]
            sv = _gmlp_sv(wm_ref, vnc, bs_ref[...], hm)
            dsv = dy[sl] * ug[sl]
            dz_ref[sl, 0:W_GRP] = dy[sl] * sv * dug[sl]
            dbs_ref[...] += dsv
            dsvb = dsv.astype(BF16)
            dvn = jnp.zeros((GMLP_CHUNK, W_GRP), F32)
            for h in range(N_HEADS):
                dws_ref[h] += _dot_nt(jnp.where(hm[h], dsv, 0.0).astype(BF16), vnc)
                dvn = dvn + jnp.where(hm[h], _dot(wmt_ref[h], dsvb), 0.0)
            xh = xhat[sl]
            dg_ref[...] += jnp.sum(dvn * xh, axis=0, keepdims=True)
            db_ref[...] += jnp.sum(dvn, axis=0, keepdims=True)
            dxh = dvn * gv
            dvg_c = rs[sl] * (dxh - jnp.mean(dxh, axis=-1, keepdims=True) - xh * jnp.mean(dxh * xh, axis=-1, keepdims=True))
            dz_ref[sl, W_GRP:2 * W_GRP] = dvg_c * dvg[sl]

    return pl.pallas_call(
        body, grid=(t // tm,),
        in_specs=[_zblk(tm, 0), _zblk(tm, 1), _zblk(tm, 0), _full((1, W_GRP)), _full((1, W_GRP)), _full(wm.shape),
                  _full(wm_t.shape), _full(bs_t.shape)],
        out_specs=[pl.BlockSpec((tm, 2 * W_GRP), lambda i: (i, 0)), _full((1, W_GRP)), _full((1, W_GRP)),
                   _full(wm.shape), _full(bs_t.shape)],
        out_shape=[_sds((t, zc), F32), _sds((1, W_GRP), F32), _sds((1, W_GRP), F32), _sds(wm.shape, F32),
                   _sds(bs_t.shape, F32)],
        name="mix_a_bwd", compiler_params=_cp("arbitrary"),
    )(z, z, dmix, ln_g, ln_b, wm, wm_t, bs_t)


def _softplus(x):
    return jnp.maximum(x, 0.0) + jnp.log(1.0 + jnp.exp(-jnp.abs(x)))


def _neg_expm1(x):
    series = -x * (1.0 + x * 0.5 * (1.0 + x * (1.0 / 3.0) * (1.0 + x * 0.25 * (1.0 + x * 0.2))))
    return jnp.where(x > -0.1, series, 1.0 - jnp.exp(x))


def _rglru_gates(ext_ref, x_ref, halo, cw, cb, wa_ref, wx_ref, ba, bx, lam):
    ext_ref[0:8, :] = halo
    ext_ref[8:, :] = x_ref[...]
    ext = ext_ref[...]
    x0, x1, x2, x3 = ext[8:], _shift_down(ext, 1, 8), _shift_down(ext, 2, 8), _shift_down(ext, 3, 8)
    xc = cb + cw[3:4, :] * x0 + cw[2:3, :] * x1 + cw[1:2, :] * x2 + cw[0:1, :] * x3
    xcb = xc.astype(BF16)
    r = _sigmoid(_dot(xcb, wa_ref[...]) + ba)
    ig = _sigmoid(_dot(xcb, wx_ref[...]) + bx)
    sp = _softplus(-lam)
    la = -RGLRU_C * r * sp
    a = jnp.exp(la)
    mult = jnp.sqrt(_neg_expm1(2.0 * la))
    return (x0, x1, x2, x3), xc, r, ig, sp, a, mult


def _mix_b_fwd(z, mix, conv_w, conv_b, wa, wx, ba, bx, lam):
    t = z.shape[0]
    tm = _mix_tm(t)

    def body(x_ref, gb_ref, cw_ref, cb_ref, wa_ref, wx_ref, ba_ref, bx_ref, lam_ref, mix_in, o_ref, hs_ref,
             ext_ref, a_ref, b_ref, xh_ref, hc_ref):
        i = pl.program_id(0)

        @pl.when(i == 0)
        def _():
            xh_ref[...] = jnp.zeros_like(xh_ref)
            hc_ref[...] = jnp.zeros_like(hc_ref)

        _, xc, _, ig, _, a, mult = _rglru_gates(ext_ref, x_ref, xh_ref[...], cw_ref[...], cb_ref[...], wa_ref, wx_ref,
                                                ba_ref[...], bx_ref[...], lam_ref[...])
        xh_ref[...] = x_ref[tm - 8:tm, :]
        a_ref[...] = a
        b_ref[...] = mult * (ig * xc)
        rid = lax.broadcasted_iota(jnp.int32, (8, W_GRP), 0)

        def group(gi, hprev):
            base = pl.multiple_of(gi * 8, 8)
            ca = a_ref[pl.ds(base, 8), :]
            cb = b_ref[pl.ds(base, 8), :]
            for k in (1, 2, 4):
                m = rid >= k
                cb = jnp.where(m, ca * pltpu.roll(cb, k, 0) + cb, cb)
                ca = jnp.where(m, ca * pltpu.roll(ca, k, 0), ca)
            hh = cb + ca * hprev
            hs_ref[pl.ds(base, 8), :] = hh
            return hh[7:8, :]

        hlast = lax.fori_loop(0, tm // 8, group, hc_ref[0:1, :])
        hc_ref[...] = jnp.broadcast_to(hlast, hc_ref.shape)
        o_ref[...] = hs_ref[...] * _gelu(gb_ref[...])

    sq = _full((W_GRP, W_GRP))
    vec = _full((1, W_GRP))
    return pl.pallas_call(
        body, grid=(t // tm,),
        in_specs=[_zblk(tm, 2), _zblk(tm, 3), _full((4, W_GRP)), vec, sq, sq, vec, vec, vec, pl.BlockSpec(memory_space=pl.ANY)],
        out_specs=[_zblk(tm, 1), pl.BlockSpec((tm, W_GRP), lambda i: (i, 0))],
        out_shape=[_sds(mix.shape, F32), _sds((t, W_GRP), F32)],
        scratch_shapes=[pltpu.VMEM((tm + 8, W_GRP), F32), pltpu.VMEM((tm, W_GRP), F32), pltpu.VMEM((tm, W_GRP), F32),
                        pltpu.VMEM((8, W_GRP), F32), pltpu.VMEM((8, W_GRP), F32)],
        input_output_aliases={9: 0}, name="mix_b_fwd", compiler_params=_cp("arbitrary"),
    )(z, z, conv_w, conv_b, wa, wx, ba, bx, lam, mix)


def _mix_b_bwd(z, dz, dmix, hs, conv_w, conv_b, wa, wx, wa_t, wx_t, ba, bx, lam):
    t = z.shape[0]
    tm = _mix_tm(t)
    nt = t // tm
    hb = tm // 8

    def body(x_ref, gb_ref, xhalo_ref, hs_ref, hhalo_ref, dy_ref, cw_ref, cb_ref, wa_ref, wx_ref, wat_ref, wxt_ref,
             ba_ref, bx_ref, lam_ref, dz_in, dz_ref, dcw_ref, dcb_ref, dwa_ref, dwx_ref, dba_ref, dbx_ref, dlam_ref,
             ext_ref, c_ref, d_ref, g_ref, an_ref, gn_ref, dxn_ref):
        i = pl.program_id(0)
        first_tile = i == nt - 1

        @pl.when(i == 0)
        def _():
            for ref in (dcw_ref, dcb_ref, dwa_ref, dwx_ref, dba_ref, dbx_ref, dlam_ref, an_ref, gn_ref, dxn_ref):
                ref[...] = jnp.zeros_like(ref)

        cw, lam = cw_ref[...], lam_ref[...]
        xhalo = jnp.where(first_tile, 0.0, xhalo_ref[...])
        (x0, x1, x2, x3), xc, r, ig, sp, a, mult = _rglru_gates(
            ext_ref, x_ref, xhalo, cw, cb_ref[...], wa_ref, wx_ref, ba_ref[...], bx_ref[...], lam)
        hs = hs_ref[...]
        dy = dy_ref[...]
        gel, dgel = _gelu_and_grad(gb_ref[...])
        dz_ref[:, W_GRP:2 * W_GRP] = dy * hs * dgel

        ext_ref[0:tm, :] = a
        ext_ref[tm:tm + 8, :] = an_ref[...]
        an_ref[...] = a[0:8, :]
        c_ref[...] = _shift_up(ext_ref[...], 1, tm)
        d_ref[...] = dy * gel
        rid = lax.broadcasted_iota(jnp.int32, (8, W_GRP), 0)

        def group(j, gnext):
            base = pl.multiple_of((tm // 8 - 1 - j) * 8, 8)
            cc = c_ref[pl.ds(base, 8), :]
            cd = d_ref[pl.ds(base, 8), :]
            for k in (1, 2, 4):
                m = rid < 8 - k
                cd = jnp.where(m, cc * pltpu.roll(cd, 8 - k, 0) + cd, cd)
                cc = jnp.where(m, cc * pltpu.roll(cc, 8 - k, 0), cc)
            gg = cd + cc * gnext
            g_ref[pl.ds(base, 8), :] = gg
            return gg[0:1, :]

        gfirst = lax.fori_loop(0, tm // 8, group, gn_ref[0:1, :])
        gn_ref[...] = jnp.broadcast_to(gfirst, gn_ref.shape)
        g = g_ref[...]

        ext_ref[0:8, :] = jnp.where(first_tile, 0.0, hhalo_ref[...])
        ext_ref[8:, :] = hs
        hprev = _shift_down(ext_ref[...], 1, 8)
        da = g * hprev
        dmult = g * (ig * xc)
        di = g * mult * xc
        dxc = g * mult * ig
        dla = da * a - dmult * a * a / mult
        dr = dla * (-RGLRU_C * sp)
        dlam_ref[...] += jnp.sum(dla * (-RGLRU_C * r), axis=0, keepdims=True) * (-_sigmoid(-lam))
        dpr = dr * r * (1.0 - r)
        dpi = di * ig * (1.0 - ig)
        dprb, dpib, xcb = dpr.astype(BF16), dpi.astype(BF16), xc.astype(BF16)
        dba_ref[...] += jnp.sum(dpr, axis=0, keepdims=True)
        dbx_ref[...] += jnp.sum(dpi, axis=0, keepdims=True)
        dwa_ref[...] += _dot_tn(xcb, dprb)
        dwx_ref[...] += _dot_tn(xcb, dpib)
        dxc = dxc + _dot(dprb, wat_ref[...]) + _dot(dpib, wxt_ref[...])
        dcb_ref[...] += jnp.sum(dxc, axis=0, keepdims=True)
        dcw_ref[3:4, :] += jnp.sum(dxc * x0, axis=0, keepdims=True)
        dcw_ref[2:3, :] += jnp.sum(dxc * x1, axis=0, keepdims=True)
        dcw_ref[1:2, :] += jnp.sum(dxc * x2, axis=0, keepdims=True)
        dcw_ref[0:1, :] += jnp.sum(dxc * x3, axis=0, keepdims=True)
        ext_ref[0:tm, :] = dxc
        ext_ref[tm:tm + 8, :] = dxn_ref[...]
        dxn_ref[...] = dxc[0:8, :]
        ext = ext_ref[...]
        dz_ref[:, 0:W_GRP] = (cw[3:4, :] * dxc + cw[2:3, :] * _shift_up(ext, 1, tm) + cw[1:2, :] * _shift_up(ext, 2, tm)
                              + cw[0:1, :] * _shift_up(ext, 3, tm))

    sq = _full((W_GRP, W_GRP))
    vec = _full((1, W_GRP))
    halo = lambda col: pl.BlockSpec((8, W_GRP), lambda i: (jnp.maximum((nt - 1 - i) * hb - 1, 0), col))
    rev = lambda col: _zblk(tm, col, nt)
    return pl.pallas_call(
        body, grid=(nt,),
        in_specs=[rev(2), rev(3), halo(2), rev(0), halo(0), rev(1), _full((4, W_GRP)), vec, sq, sq, sq, sq, vec, vec, vec,
                  pl.BlockSpec(memory_space=pl.ANY)],
        out_specs=[pl.BlockSpec((tm, 2 * W_GRP), lambda i: (nt - 1 - i, 1)), _full((4, W_GRP)), vec, sq, sq, vec, vec, vec],
        out_shape=[_sds(dz.shape, F32), _sds((4, W_GRP), F32), _sds((1, W_GRP), F32), _sds((W_GRP, W_GRP), F32),
                   _sds((W_GRP, W_GRP), F32), _sds((1, W_GRP), F32), _sds((1, W_GRP), F32), _sds((1, W_GRP), F32)],
        scratch_shapes=[pltpu.VMEM((tm + 8, W_GRP), F32), pltpu.VMEM((tm, W_GRP), F32), pltpu.VMEM((tm, W_GRP), F32),
                        pltpu.VMEM((tm, W_GRP), F32), pltpu.VMEM((8, W_GRP), F32), pltpu.VMEM((8, W_GRP), F32),
                        pltpu.VMEM((8, W_GRP), F32)],
        input_output_aliases={15: 0}, name="mix_b_bwd", compiler_params=_cp("arbitrary"),
    )(z, z, z, hs, hs, dmix, conv_w, conv_b, wa, wx, wa_t, wx_t, ba, bx, lam, dz)


def _tri(n, lower):
    r = lax.broadcasted_iota(jnp.int32, (n, n), 0)
    c = lax.broadcasted_iota(jnp.int32, (n, n), 1)
    return jnp.where((r >= c) if lower else (r <= c), 1.0, 0.0).astype(BF16)


def _causal_stack():
    r = lax.broadcasted_iota(jnp.int32, (N_HEADS * HGRN_CHUNK, HGRN_CHUNK), 0)
    c = lax.broadcasted_iota(jnp.int32, (N_HEADS * HGRN_CHUNK, HGRN_CHUNK), 1)
    m = None
    for h in range(N_HEADS):
        mh = (r >= h * HGRN_CHUNK) & (r < (h + 1) * HGRN_CHUNK) & (r - h * HGRN_CHUNK >= c)
        m = mh if m is None else (m | mh)
    return m


def _stack_heads(x, hm):
    return jnp.concatenate([jnp.where(hm[h], x, 0.0) for h in range(N_HEADS)], axis=0)


def _unstack_heads(xs, hm):
    out = jnp.where(hm[0], xs[0:HGRN_CHUNK], 0.0)
    for h in range(1, N_HEADS):
        out = out + jnp.where(hm[h], xs[h * HGRN_CHUNK:(h + 1) * HGRN_CHUNK], 0.0)
    return out


def _hgrn_chunk(qv, fv, lb, tril):
    sq = _sigmoid(qv)
    qq = qv * sq
    sg = _sigmoid(fv)
    fg = lb + (1.0 - lb) * sg
    kk = 1.0 - fg
    bb = _dot_f32_lhs_exact(tril, jnp.log(fg))
    b_last = bb[HGRN_CHUNK - 1:HGRN_CHUNK, :]
    b_mid = bb[HGRN_CHUNK // 2 - 1:HGRN_CHUNK // 2, :]
    eq = jnp.exp(jnp.minimum(bb - b_mid, EXP_CLAMP))
    ek = jnp.exp(jnp.minimum(b_mid - bb, EXP_CLAMP))
    eb = jnp.exp(bb)
    el = jnp.exp(b_last - bb)
    return sq, qq, sg, fg, kk, b_last, eq, ek, eb, el


def _seg_mean(x, avg):
    return _dot_f32_rhs_exact(x, avg)


def _mix_c_fwd(z, mix, lb, ng):
    t = z.shape[0]
    tm = _mix_tm(t)
    nch = tm // HGRN_CHUNK

    def body(q_ref, f_ref, i_ref, g_ref, lb_ref, ng_ref, mix_in, y_ref, o_ref, ss_ref, s_ref):
        @pl.when(pl.program_id(0) == 0)
        def _():
            s_ref[...] = jnp.zeros_like(s_ref)

        hm = _head_masks()
        bmask = _block_mask()
        causal = _causal_stack()
        tril = _tri(HGRN_CHUNK, True)
        avg = jnp.where(bmask, 1.0 / HEAD_DIM, 0.0).astype(BF16)
        lb, ng = lb_ref[...], ng_ref[...]

        def chunk(c, carry):
            rows = pl.ds(pl.multiple_of(c * HGRN_CHUNK, HGRN_CHUNK), HGRN_CHUNK)
            vv = i_ref[rows, :]
            gv = g_ref[rows, :]
            _, qq, _, _, kk, b_last, eq, ek, eb, el = _hgrn_chunk(q_ref[rows, :], f_ref[rows, :], lb, tril)
            vb = vv.astype(BF16)
            qs = _stack_heads(qq * eq, hm).astype(BF16)
            att = jnp.where(causal, _dot_nt(qs, (kk * ek).astype(BF16)), 0.0)
            o = _unstack_heads(_dot(att.astype(BF16), vb), hm)
            s0 = s_ref[...]
            ss_ref[c] = s0
            o = o + _dot_nt((qq * eb).astype(BF16), s0.astype(BF16))
            s_ref[...] = s0 * jnp.exp(b_last) + jnp.where(bmask, _dot_tn(vb, (kk * el).astype(BF16)), 0.0)
            o_ref[rows, :] = o
            rstd = lax.rsqrt(_seg_mean(o * o, avg) + EPS)
            y_ref[rows, :] = o * rstd * ng * (gv * _sigmoid(gv))
            return carry

        lax.fori_loop(0, nch, chunk, 0, unroll=HGRN_UNROLL)

    vec = _full((1, W_GRP))
    return pl.pallas_call(
        body, grid=(t // tm,),
        in_specs=[_zblk(tm, 4), _zblk(tm, 5), _zblk(tm, 6), _zblk(tm, 7), vec, vec, pl.BlockSpec(memory_space=pl.ANY)],
        out_specs=[_zblk(tm, 2), pl.BlockSpec((tm, W_GRP), lambda i: (i, 0)),
                   pl.BlockSpec((nch, W_GRP, W_GRP), lambda i: (i, 0, 0))],
        out_shape=[_sds(mix.shape, F32), _sds((t, W_GRP), F32), _sds((t // HGRN_CHUNK, W_GRP, W_GRP), F32)],
        scratch_shapes=[pltpu.VMEM((W_GRP, W_GRP), F32)],
        input_output_aliases={6: 0}, name="mix_c_fwd", compiler_params=_cp("arbitrary"),
    )(z, z, z, z, lb, ng, mix)


def _mix_c_bwd(z, dz, dmix, o_pre, states, lb, ng):
    t = z.shape[0]
    tm = _mix_tm(t)
    nt = t // tm
    nch = tm // HGRN_CHUNK

    def body(q_ref, f_ref, i_ref, g_ref, o_ref, ss_ref, dy_ref, lb_ref, ng_ref, dz_in, dz_ref, dlb_ref, dng_ref, ds_ref):
        @pl.when(pl.program_id(0) == 0)
        def _():
            ds_ref[...] = jnp.zeros_like(ds_ref)
            dlb_ref[...] = jnp.zeros_like(dlb_ref)
            dng_ref[...] = jnp.zeros_like(dng_ref)

        hm = _head_masks()
        bmask = _block_mask()
        causal = _causal_stack()
        tril = _tri(HGRN_CHUNK, True)
        triu = _tri(HGRN_CHUNK, False)
        avg = jnp.where(bmask, 1.0 / HEAD_DIM, 0.0).astype(BF16)
        lb, ng = lb_ref[...], ng_ref[...]
        last_row = lax.broadcasted_iota(jnp.int32, (HGRN_CHUNK, W_GRP), 0) == HGRN_CHUNK - 1

        def chunk(j, carry):
            c = nch - 1 - j
            rows = pl.ds(pl.multiple_of(c * HGRN_CHUNK, HGRN_CHUNK), HGRN_CHUNK)
            qv, gv, vv = q_ref[rows, :], g_ref[rows, :], i_ref[rows, :]
            sq, qq, sg, fg, kk, b_last, eq, ek, eb, el = _hgrn_chunk(qv, f_ref[rows, :], lb, tril)
            s0 = ss_ref[c]
            ds1 = ds_ref[...]
            o = o_ref[rows, :]
            dy = dy_ref[rows, :]
            rstd = lax.rsqrt(_seg_mean(o * o, avg) + EPS)
            oh = o * rstd
            sgg = _sigmoid(gv)
            dz_ref[rows, 3 * W_GRP:4 * W_GRP] = dy * oh * ng * (sgg * (1.0 + gv * (1.0 - sgg)))
            don = dy * gv * sgg
            dng_ref[...] += jnp.sum(don * oh, axis=0, keepdims=True)
            doh = don * ng
            do = rstd * (doh - oh * _seg_mean(doh * oh, avg))
            qt, kt, qh, kh = qq * eq, kk * ek, qq * eb, kk * el
            vb, dob = vv.astype(BF16), do.astype(BF16)
            ktb, khb = kt.astype(BF16), kh.astype(BF16)
            ds1b = ds1.astype(BF16)
            qs = _stack_heads(qt, hm).astype(BF16)
            dos = _stack_heads(do, hm).astype(BF16)
            att = jnp.where(causal, _dot_nt(qs, ktb), 0.0).astype(BF16)
            datt = jnp.where(causal, _dot_nt(dos, vb), 0.0).astype(BF16)
            dv = _dot_tn(att, dos) + _dot_nt(khb, ds1b)
            dqt = _unstack_heads(_dot(datt, ktb), hm)
            dkt = _dot_tn(datt, qs)
            dqh = _dot(dob, s0.astype(BF16))
            dkh = _dot(vb, ds1b)
            e_last = jnp.exp(b_last)
            ds_ref[...] = ds1 * e_last + jnp.where(bmask, _dot_tn(dob, qh.astype(BF16)), 0.0)
            dq = dqt * eq + dqh * eb
            dk = dkt * ek + dkh * el
            db = qt * dqt - kt * dkt + qh * dqh - kh * dkh
            db_last = jnp.sum(kh * dkh, axis=0, keepdims=True) + e_last * jnp.sum(ds1 * s0, axis=0, keepdims=True)
            db = db + jnp.where(last_row, db_last, 0.0)
            dlogf = _dot_f32_lhs_exact(triu, db)
            dfg = dlogf / fg - dk
            dz_ref[rows, W_GRP:2 * W_GRP] = dfg * (1.0 - lb) * sg * (1.0 - sg)
            dlb_ref[...] += jnp.sum(dfg * (1.0 - sg), axis=0, keepdims=True)
            dz_ref[rows, 0:W_GRP] = dq * (sq * (1.0 + qv * (1.0 - sq)))
            dz_ref[rows, 2 * W_GRP:3 * W_GRP] = dv
            return carry

        lax.fori_loop(0, nch, chunk, 0, unroll=HGRN_UNROLL)

    vec = _full((1, W_GRP))
    rev = lambda col: _zblk(tm, col, nt)
    return pl.pallas_call(
        body, grid=(nt,),
        in_specs=[rev(4), rev(5), rev(6), rev(7), rev(0), pl.BlockSpec((nch, W_GRP, W_GRP), lambda i: (nt - 1 - i, 0, 0)),
                  rev(2), vec, vec, pl.BlockSpec(memory_space=pl.ANY)],
        out_specs=[pl.BlockSpec((tm, 4 * W_GRP), lambda i: (nt - 1 - i, 1)), vec, vec],
        out_shape=[_sds(dz.shape, F32), _sds((1, W_GRP), F32), _sds((1, W_GRP), F32)],
        scratch_shapes=[pltpu.VMEM((W_GRP, W_GRP), F32)],
        input_output_aliases={9: 0}, name="mix_c_bwd", compiler_params=_cp("arbitrary"),
    )(z, z, z, z, o_pre, states, dmix, lb, ng, dz)


def _pool_select(hm, s2, s4, s8, s16):
    return jnp.where(hm[0], s2, jnp.where(hm[1], s4, jnp.where(hm[2], s8, s16)))


def _pool_counts(hm, row0, tm):
    pos = (row0 + 1 + lax.broadcasted_iota(jnp.int32, (tm, W_GRP), 0)).astype(F32)
    win = _pool_select(hm, 2.0, 4.0, 8.0, 16.0)
    return jnp.minimum(pos, win)


def _pooled(ext_ref, x, halo, hm, cnt):
    ext_ref[0:POOL_HALO, :] = halo
    ext_ref[POOL_HALO:, :] = x
    e = ext_ref[...]
    s2 = e + pltpu.roll(e, 1, 0)
    s4 = s2 + pltpu.roll(s2, 2, 0)
    s8 = s4 + pltpu.roll(s4, 4, 0)
    s16 = s8 + pltpu.roll(s8, 8, 0)
    return _pool_select(hm, s2, s4, s8, s16)[POOL_HALO:] / cnt - x


def _mix_d_fwd(z, mix, wd, scale):
    t = z.shape[0]
    tm = _mix_tm(t)

    def body(x_ref, wd_ref, sc_ref, mix_in, o_ref, ext_ref, halo_ref):
        i = pl.program_id(0)

        @pl.when(i == 0)
        def _():
            halo_ref[...] = jnp.zeros_like(halo_ref)

        hm = _head_masks()
        x = x_ref[...]
        pooled = _pooled(ext_ref, x, halo_ref[...], hm, _pool_counts(hm, i * tm, tm))
        halo_ref[...] = x_ref[tm - POOL_HALO:tm, :]
        o_ref[...] = _dot(pooled.astype(BF16), wd_ref[...]) * sc_ref[...]

    return pl.pallas_call(
        body, grid=(t // tm,),
        in_specs=[_zblk(tm, 8), _full((W_GRP, W_GRP)), _full((1, W_GRP)), pl.BlockSpec(memory_space=pl.ANY)],
        out_specs=_zblk(tm, 3), out_shape=_sds(mix.shape, F32),
        scratch_shapes=[pltpu.VMEM((tm + POOL_HALO, W_GRP), F32), pltpu.VMEM((POOL_HALO, W_GRP), F32)],
        input_output_aliases={3: 0}, name="mix_d_fwd", compiler_params=_cp("arbitrary"),
    )(z, wd, scale, mix)


def _mix_d_bwd(z, dz, dmix, wd, wd_t, scale):
    t = z.shape[0]
    tm = _mix_tm(t)
    nt = t // tm
    hb = tm // POOL_HALO

    def body(x_ref, xhalo_ref, dy_ref, wd_ref, wdt_ref, sc_ref, dz_in, dz_ref, dwd_ref, dsc_ref, ext_ref, en_ref):
        i = pl.program_id(0)
        ri = nt - 1 - i

        @pl.when(i == 0)
        def _():
            en_ref[...] = jnp.zeros_like(en_ref)
            dwd_ref[...] = jnp.zeros_like(dwd_ref)
            dsc_ref[...] = jnp.zeros_like(dsc_ref)

        hm = _head_masks()
        cnt = _pool_counts(hm, ri * tm, tm)
        x = x_ref[...]
        pooled = _pooled(ext_ref, x, jnp.where(ri == 0, 0.0, xhalo_ref[...]), hm, cnt)
        pb = pooled.astype(BF16)
        dy = dy_ref[...]
        dsc_ref[...] += jnp.sum(dy * _dot(pb, wd_ref[...]), axis=0, keepdims=True)
        dyw = (dy * sc_ref[...]).astype(BF16)
        dwd_ref[...] += _dot_tn(pb, dyw)
        dpool = _dot(dyw, wdt_ref[...])
        e = dpool / cnt
        ext_ref[0:tm, :] = e
        ext_ref[tm:, :] = en_ref[...]
        en_ref[...] = e[0:POOL_HALO, :]
        ee = ext_ref[...]
        n = tm + POOL_HALO
        r2 = ee + pltpu.roll(ee, n - 1, 0)
        r4 = r2 + pltpu.roll(r2, n - 2, 0)
        r8 = r4 + pltpu.roll(r4, n - 4, 0)
        r16 = r8 + pltpu.roll(r8, n - 8, 0)
        dz_ref[...] = _pool_select(hm, r2, r4, r8, r16)[:tm] - dpool

    sq = _full((W_GRP, W_GRP))
    vec = _full((1, W_GRP))
    return pl.pallas_call(
        body, grid=(nt,),
        in_specs=[_zblk(tm, 8, nt), pl.BlockSpec((POOL_HALO, W_GRP), lambda i: (jnp.maximum((nt - 1 - i) * hb - 1, 0), 8)),
                  _zblk(tm, 3, nt), sq, sq, vec, pl.BlockSpec(memory_space=pl.ANY)],
        out_specs=[_zblk(tm, 8, nt), sq, vec],
        out_shape=[_sds(dz.shape, F32), _sds((W_GRP, W_GRP), F32), _sds((1, W_GRP), F32)],
        scratch_shapes=[pltpu.VMEM((tm + POOL_HALO, W_GRP), F32), pltpu.VMEM((POOL_HALO, W_GRP), F32)],
        input_output_aliases={6: 0}, name="mix_d_bwd", compiler_params=_cp("arbitrary"),
    )(z, z, dmix, wd, wd_t, scale, dz)


def _as2d(a):
    if a.ndim == 1:
        return a.reshape(1, a.shape[0])
    return a.reshape(-1, a.shape[-1])


def _adamw(w, g, m, v, name):
    shape = w.shape
    w2, g2, m2, v2 = _as2d(w), _as2d(g), _as2d(m), _as2d(v)
    rows, cols = w2.shape
    tr = _pick(rows, (1024, 512, 256, 128, 64, 32, 16, 8))
    if tr * cols * 4 * 14 > VMEM_LIMIT_BYTES:
        tr = _pick(rows, (256, 128, 64, 32, 16, 8))

    def body(w_ref, g_ref, m_ref, v_ref, d_ref, nm_ref, nv_ref):
        gv = g_ref[...]
        mn = ADAM_B1 * m_ref[...] + (1.0 - ADAM_B1) * gv
        vn = ADAM_B2 * v_ref[...] + (1.0 - ADAM_B2) * (gv * gv)
        m_hat = mn / (1.0 - ADAM_B1 ** ADAM_STEP)
        v_hat = vn / (1.0 - ADAM_B2 ** ADAM_STEP)
        d_ref[...] = -ADAM_LR * (m_hat / (jnp.sqrt(v_hat) + ADAM_EPS) + ADAM_WD * w_ref[...])
        nm_ref[...] = mn
        nv_ref[...] = vn

    blk = pl.BlockSpec((tr, cols), lambda i: (i, 0))
    outs = pl.pallas_call(
        body, grid=(rows // tr,), in_specs=[blk] * 4, out_specs=[blk] * 3, out_shape=[_sds((rows, cols), F32)] * 3,
        name=name, compiler_params=_cp("parallel"),
    )(w2, g2, m2, v2)
    return tuple(o.reshape(shape) for o in outs)


def _adamw_layer(w, g, m, v, layer, bufs, name):
    nl, r, cs = w.shape
    tr = _pick(r, (256, 128, 64, 32, 16, 8))

    def body(w_ref, g_ref, m_ref, v_ref, *rest):
        go_ref, d_ref, nm_ref, nv_ref = rest[-4:]
        gv = g_ref[...]
        mn = ADAM_B1 * m_ref[...] + (1.0 - ADAM_B1) * gv
        vn = ADAM_B2 * v_ref[...] + (1.0 - ADAM_B2) * (gv * gv)
        m_hat = mn / (1.0 - ADAM_B1 ** ADAM_STEP)
        v_hat = vn / (1.0 - ADAM_B2 ** ADAM_STEP)
        go_ref[...] = gv
        d_ref[...] = -ADAM_LR * (m_hat / (jnp.sqrt(v_hat) + ADAM_EPS) + ADAM_WD * w_ref[...])
        nm_ref[...] = mn
        nv_ref[...] = vn

    lay = pl.BlockSpec((None, tr, cs), lambda i: (layer, i, 0))
    in_specs = [lay, pl.BlockSpec((tr, cs), lambda i: (i, 0)), lay, lay]
    args = [w, g, m, v]
    aliases = {}
    if bufs is not None:
        in_specs += [pl.BlockSpec(memory_space=pl.ANY)] * 4
        args += list(bufs)
        aliases = {4 + i: i for i in range(4)}
    return pl.pallas_call(
        body, grid=(r // tr,), in_specs=in_specs, out_specs=[lay] * 4, out_shape=[_sds((nl, r, cs), F32)] * 4,
        input_output_aliases=aliases, name=name, compiler_params=_cp("parallel"),
    )(*args)


def _slot_sum(own, slots, name):
    n_slots, rows, cols = slots.shape
    whole_fits = rows * cols * 4 * (n_slots + 2) * 2 <= VMEM_LIMIT_BYTES // 2
    tr = rows if whole_fits else _pick(rows, (512, 352, 256, 128, 64, 32, 16, 8))

    def body(*refs):
        s_ref, o_ref = refs[-2], refs[-1]
        acc = s_ref[0].astype(F32) if own is None else refs[0][...].astype(F32) + s_ref[0].astype(F32)
        for k in range(1, n_slots):
            acc = acc + s_ref[k].astype(F32)
        o_ref[...] = acc

    row = pl.BlockSpec((tr, cols), lambda i: (i, 0))
    return pl.pallas_call(
        body, grid=(rows // tr,),
        in_specs=([] if own is None else [row]) + [pl.BlockSpec((n_slots, tr, cols), lambda i: (0, i, 0))],
        out_specs=row, out_shape=_sds((rows, cols), F32), name=name, compiler_params=_cp("parallel"),
    )(*(() if own is None else (own,)), slots)


def _me():
    return lax.axis_index("x"), lax.axis_index("y"), lax.axis_index("c")


def _other_chips(x, y):
    return [(1 - x, y), (x, 1 - y), (1 - x, 1 - y)]


ANY_SPEC = pl.BlockSpec(memory_space=pl.ANY)
HBM_SPEC = pl.BlockSpec(memory_space=pltpu.HBM)
SEM_SPEC = pl.BlockSpec(memory_space=pltpu.SEMAPHORE)
SPLIT_COPY_PARAMS = pltpu.CompilerParams(has_side_effects=pltpu.SideEffectType.DATAFLOW_SIDE_EFFECTING)
N_CHIPS = 4


def _aligned(v, m):
    return v if isinstance(v, int) else pl.multiple_of(v, m)


def _in_hbm(arr):
    return pltpu.with_memory_space_constraint(arr, pltpu.HBM)


def _peer(x, y, c, k):
    fx, fy, fc = (k >> 2) & 1, (k >> 1) & 1, k & 1
    px = 1 - x if fx else x
    py = 1 - y if fy else y
    pc = 1 - c if fc else c
    return px, py, pc


def _gather_start(shards, after, name):
    n = len(shards)

    def body(*refs):
        src, land = refs[:n], refs[n:2 * n]
        send_sems, recv_sems = refs[2 * n + 1], refs[2 * n + 2]
        token = refs[-1]
        x, y, c = _me()
        for w in range(n):
            for chip in _other_chips(x, y):
                pltpu.make_async_remote_copy(
                    src_ref=src[w], dst_ref=land[w].at[2 * x + y], send_sem=send_sems.at[w], recv_sem=recv_sems.at[w],
                    device_id=(*chip, c), device_id_type=MESH_ID).start()
        token[...] = jnp.zeros_like(token)

    lands = [lax.empty((N_CHIPS,) + s.shape, s.dtype) for s in shards]
    thru = [pltpu.HBM(s.shape, s.dtype) for s in shards] + [pltpu.HBM(z.shape, z.dtype) for z in lands]
    outs = pl.pallas_call(
        body, name=name,
        out_shape=(pltpu.SemaphoreType.DMA((n,)), pltpu.SemaphoreType.DMA((n,)), *thru, _sds((8, 128), F32)),
        in_specs=[HBM_SPEC] * (2 * n) + [ANY_SPEC],
        out_specs=(SEM_SPEC, SEM_SPEC, *[HBM_SPEC] * (2 * n), pl.BlockSpec(memory_space=pltpu.VMEM)),
        input_output_aliases={i: 2 + i for i in range(2 * n)}, compiler_params=SPLIT_COPY_PARAMS,
    )(*[_in_hbm(s) for s in shards], *[_in_hbm(z) for z in lands], after)
    return (outs[0], outs[1], outs[2:2 + n], outs[2 + n:2 + 2 * n]), outs[-1]


def _gather_wait(send_sems, recv_sems, srcs, lands, after, name):
    n = len(srcs)

    def body(*refs):
        land = refs[n:2 * n]
        send_sems, recv_sems = refs[2 * n], refs[2 * n + 1]
        x, y, c = _me()
        for w in range(n):
            three = land[w].at[pl.ds(0, N_CHIPS - 1)]
            cp = pltpu.make_async_remote_copy(src_ref=three, dst_ref=three, send_sem=send_sems.at[w], recv_sem=recv_sems.at[w],
                                              device_id=(x, y, c), device_id_type=MESH_ID)
            cp.wait_send()
            cp.wait_recv()

    both = list(srcs) + list(lands)
    outs = pl.pallas_call(
        body, name=name, out_shape=tuple(pltpu.HBM(b.shape, b.dtype) for b in both),
        in_specs=[HBM_SPEC] * (2 * n) + [SEM_SPEC, SEM_SPEC, ANY_SPEC], out_specs=[HBM_SPEC] * (2 * n),
        input_output_aliases={i: i for i in range(2 * n)}, compiler_params=SPLIT_COPY_PARAMS,
    )(*both, send_sems, recv_sems, after)
    return outs[n:2 * n]


def _push_start(grads, small, name):
    n = len(grads)
    srcs = list(grads) + ([] if small is None else [small])
    ns = len(srcs)

    def body(*refs):
        src, slots = refs[:ns], refs[ns:2 * ns]
        send_sems, recv_sems = refs[2 * ns], refs[2 * ns + 1]
        token = refs[-1]
        x, y, c = _me()
        for w in range(ns):
            for k in range(1, N_DEV):
                px, py, pc = _peer(x, y, c, k)
                if w < n:
                    hr = src[w].shape[1] // 2
                    piece = src[w].at[2 * px + py, pl.ds(_aligned(pc * hr, 16), hr), :]
                    slot = slots[w].at[k - 1]
                else:
                    piece = src[w]
                    slot = slots[w].at[4 * x + 2 * y + c]
                pltpu.make_async_remote_copy(
                    src_ref=piece, dst_ref=slot, send_sem=send_sems.at[w], recv_sem=recv_sems.at[w],
                    device_id=(px, py, pc), device_id_type=MESH_ID).start()
        token[...] = jnp.zeros_like(token)

    slots = [lax.empty((N_DEV - 1, g.shape[1] // 2, g.shape[2]), g.dtype) for g in grads]
    if small is not None:
        slots.append(lax.empty((N_DEV,) + small.shape, small.dtype))
    both = srcs + slots
    outs = pl.pallas_call(
        body, name=name,
        out_shape=(pltpu.SemaphoreType.DMA((ns,)), pltpu.SemaphoreType.DMA((ns,)),
                   *[pltpu.HBM(b.shape, b.dtype) for b in both], _sds((8, 128), F32)),
        in_specs=[HBM_SPEC] * len(both),
        out_specs=(SEM_SPEC, SEM_SPEC, *[HBM_SPEC] * len(both), pl.BlockSpec(memory_space=pltpu.VMEM)),
        input_output_aliases={i: 2 + i for i in range(len(both))}, compiler_params=SPLIT_COPY_PARAMS,
    )(*[_in_hbm(b) for b in both])
    return (outs[0], outs[1], outs[2:2 + ns], outs[2 + ns:2 + 2 * ns]), outs[-1]


def _push_wait(send_sems, recv_sems, srcs, slots, after, name):
    n = len(srcs)

    def body(*refs):
        slot = refs[n:2 * n]
        send_sems, recv_sems = refs[2 * n], refs[2 * n + 1]
        x, y, c = _me()
        for w in range(n):
            seven = slot[w].at[pl.ds(0, N_DEV - 1)]
            cp = pltpu.make_async_remote_copy(src_ref=seven, dst_ref=seven, send_sem=send_sems.at[w],
                                              recv_sem=recv_sems.at[w], device_id=(x, y, c), device_id_type=MESH_ID)
            cp.wait_send()
            cp.wait_recv()

    both = list(srcs) + list(slots)
    outs = pl.pallas_call(
        body, name=name, out_shape=tuple(pltpu.HBM(b.shape, b.dtype) for b in both),
        in_specs=[HBM_SPEC] * (2 * n) + [SEM_SPEC, SEM_SPEC, ANY_SPEC], out_specs=[HBM_SPEC] * (2 * n),
        input_output_aliases={i: i for i in range(2 * n)}, compiler_params=SPLIT_COPY_PARAMS,
    )(*both, send_sems, recv_sems, after)
    return outs[:n], outs[n:]


SWAP_CHUNK_BYTES = 2 * 1024 * 1024


def _swap_chunk_rows(hr, cs):
    ch = hr
    while ch * cs * 4 > SWAP_CHUNK_BYTES and ch % 16 == 0:
        ch //= 2
    return ch


def _swap_halves(halves, name):
    n = len(halves)
    chunk = [_swap_chunk_rows(*h.shape) for h in halves]
    rounds = max(h.shape[0] // ch for h, ch in zip(halves, chunk))

    def body(*refs):
        src, dst, buf = refs[:n], refs[n:2 * n], refs[2 * n:3 * n]
        load_sems, put_sems, send_sems, recv_sems = refs[3 * n:]
        x, y, c = _me()
        sibling = (x, y, 1 - c)
        for j in range(rounds):
            live = [w for w in range(n) if j < src[w].shape[0] // chunk[w]]
            loads = [pltpu.make_async_copy(src[w].at[pl.ds(j * chunk[w], chunk[w])], buf[w], load_sems.at[w]) for w in live]
            for ld in loads:
                ld.start()
            moves = []
            for ld, w in zip(loads, live):
                ld.wait()
                rows = pl.ds(_aligned(c * src[w].shape[0] + j * chunk[w], 8), chunk[w])
                put = pltpu.make_async_copy(buf[w], dst[w].at[rows], put_sems.at[w])
                send = pltpu.make_async_remote_copy(src_ref=buf[w], dst_ref=dst[w].at[rows], send_sem=send_sems.at[w],
                                                    recv_sem=recv_sems.at[w], device_id=sibling, device_id_type=MESH_ID)
                put.start()
                send.start()
                moves.append((put, send))
            for put, send in moves:
                put.wait()
                send.wait_send()
        for w in range(n):
            hr = src[w].shape[0]
            got = dst[w].at[pl.ds(_aligned((1 - c) * hr, 8), hr)]
            pltpu.make_async_remote_copy(src_ref=got, dst_ref=got, send_sem=send_sems.at[w], recv_sem=recv_sems.at[w],
                                         device_id=sibling, device_id_type=MESH_ID).wait_recv()

    return pl.pallas_call(
        body, in_specs=[ANY_SPEC] * n, out_specs=[ANY_SPEC] * n,
        out_shape=[_sds((2 * h.shape[0], h.shape[1]), F32) for h in halves],
        scratch_shapes=[pltpu.VMEM((ch, h.shape[1]), F32) for h, ch in zip(halves, chunk)]
        + [pltpu.SemaphoreType.DMA((n,))] * 4,
        name=name,
    )(*halves)


BIG = ("w_in", "w_out", "w_up", "w_down", "w_pe", "w_pg")
ROW_SHARDED = ("w_out", "w_down", "w_pg")
SMALL = ("norm1_g", "a_ln_g", "a_ln_b", "a_ws", "a_bs", "b_conv_w", "b_conv_b", "b_wa", "b_ba", "b_wx", "b_bx", "b_lam",
         "c_lb", "c_norm_g", "d_w", "d_scale", "norm2_g", "ffn_conv_w", "ffn_conv_b", "norm3_g", "final_g")
SMALL_SHARDED = ("b_conv_w", "ffn_conv_w")
WEIGHTS = ("norm1_g", "w_in", "a_ln_g", "a_ln_b", "a_ws", "a_bs", "b_conv_w", "b_conv_b", "b_wa", "b_ba", "b_wx", "b_bx",
           "b_lam", "c_lb", "c_norm_g", "d_w", "d_scale", "w_out", "norm2_g", "w_up", "ffn_conv_w", "ffn_conv_b", "w_down",
           "norm3_g", "w_pe", "w_pg", "final_g")
ARGS = ("x", "p") + WEIGHTS + ("loss_target",) + tuple("m_" + n for n in WEIGHTS) + tuple("v_" + n for n in WEIGHTS)


def _block_diag(w):
    eye = jnp.eye(N_HEADS, dtype=w.dtype)
    return (eye[:, None, :, None] * w[:, :, None, :]).reshape(W_GRP, W_GRP)


def _diag_blocks(m):
    m4 = m.reshape(N_HEADS, HEAD_DIM, N_HEADS, HEAD_DIM)
    return jnp.stack([m4[h, :, h, :] for h in range(N_HEADS)])


def _lower_bounds(c_lb):
    lbs = jnp.cumsum(jax.nn.softmax(c_lb, axis=0), axis=0)
    return lbs - lbs[0:1]


def kernel(x, p, norm1_g, w_in, a_ln_g, a_ln_b, a_ws, a_bs, b_conv_w, b_conv_b, b_wa, b_ba, b_wx, b_bx, b_lam, c_lb, c_norm_g, d_w, d_scale, w_out, norm2_g, w_up, ffn_conv_w, ffn_conv_b, w_down, norm3_g, w_pe, w_pg, final_g, loss_target, m_norm1_g, m_w_in, m_a_ln_g, m_a_ln_b, m_a_ws, m_a_bs, m_b_conv_w, m_b_conv_b, m_b_wa, m_b_ba, m_b_wx, m_b_bx, m_b_lam, m_c_lb, m_c_norm_g, m_d_w, m_d_scale, m_w_out, m_norm2_g, m_w_up, m_ffn_conv_w, m_ffn_conv_b, m_w_down, m_norm3_g, m_w_pe, m_w_pg, m_final_g, v_norm1_g, v_w_in, v_a_ln_g, v_a_ln_b, v_a_ws, v_a_bs, v_b_conv_w, v_b_conv_b, v_b_wa, v_b_ba, v_b_wx, v_b_bx, v_b_lam, v_c_lb, v_c_norm_g, v_d_w, v_d_scale, v_w_out, v_norm2_g, v_w_up, v_ffn_conv_w, v_ffn_conv_b, v_w_down, v_norm3_g, v_w_pe, v_w_pg, v_final_g):
    return _step((x, p, norm1_g, w_in, a_ln_g, a_ln_b, a_ws, a_bs, b_conv_w, b_conv_b, b_wa, b_ba, b_wx, b_bx, b_lam, c_lb, c_norm_g, d_w, d_scale, w_out, norm2_g, w_up, ffn_conv_w, ffn_conv_b, w_down, norm3_g, w_pe, w_pg, final_g, loss_target, m_norm1_g, m_w_in, m_a_ln_g, m_a_ln_b, m_a_ws, m_a_bs, m_b_conv_w, m_b_conv_b, m_b_wa, m_b_ba, m_b_wx, m_b_bx, m_b_lam, m_c_lb, m_c_norm_g, m_d_w, m_d_scale, m_w_out, m_norm2_g, m_w_up, m_ffn_conv_w, m_ffn_conv_b, m_w_down, m_norm3_g, m_w_pe, m_w_pg, m_final_g, v_norm1_g, v_w_in, v_a_ln_g, v_a_ln_b, v_a_ws, v_a_bs, v_b_conv_w, v_b_conv_b, v_b_wa, v_b_ba, v_b_wx, v_b_bx, v_b_lam, v_c_lb, v_c_norm_g, v_d_w, v_d_scale, v_w_out, v_norm2_g, v_w_up, v_ffn_conv_w, v_ffn_conv_b, v_w_down, v_norm3_g, v_w_pe, v_w_pg, v_final_g))


SMALL_PER_LAYER = tuple(n for n in SMALL if n != "final_g")
GATHERED = BIG + SMALL_SHARDED
GATHER_FIRST = ("w_in", "b_conv_w")
GATHER_REST = tuple(n for n in GATHERED if n not in GATHER_FIRST)
PUSH_EARLY = ("w_pe", "w_pg", "w_down", "w_up")
PUSH_MID = ("w_out",)
PUSH_LATE = ("w_in",)
SMALL_MID = tuple(n for n in SMALL_PER_LAYER if n != "norm1_g")


def _cols_to_slabs(m):
    r, c4 = m.shape
    return jnp.moveaxis(m.reshape(r, N_CHIPS, c4 // N_CHIPS), 1, 0)


def _slabs_to_cols(s):
    return jnp.moveaxis(s, 0, 1).reshape(s.shape[1], -1)


def _pack_small(parts):
    flat = jnp.concatenate([p.reshape(-1) for p in parts])
    return jnp.pad(flat, (0, (-flat.shape[0]) % 1024)).reshape(-1, 128)


def _step(args):
    a = dict(zip(ARGS, args, strict=True))
    x0 = a["x"][0]
    target = a["loss_target"][0]
    nl = a["norm1_g"].shape[0]
    t, d = x0.shape
    f = a["w_down"].shape[1] * N_CHIPS
    cx, cy, cc = _me()
    my_shard = 2 * cx + cy
    shards = {n: a[n].astype(BF16) for n in BIG}
    shards.update({n: a[n] for n in SMALL_SHARDED})

    def start_gather(l, names, after, tag):
        return _gather_start([shards[n][l] for n in names], after, f"gather_start_{l}{tag}")

    def finish_gather(l, names, handle, after, tag):
        send, recv, srcs, lands = handle
        lands = _gather_wait(send, recv, srcs, lands, after, f"gather_wait_{l}{tag}")
        w = {}
        for n, land in zip(names, lands):
            full = lax.dynamic_update_slice(land, shards[n][l][None], (my_shard, 0, 0))
            if n in ROW_SHARDED:
                w[n] = full.reshape(-1, full.shape[-1])
            elif n in ("w_up", "w_pe"):
                w[n] = full
            else:
                w[n] = _slabs_to_cols(full)
        return w

    lbs, lbs_vjp = jax.vjp(_lower_bounds, a["c_lb"])
    tril = jnp.tril(jnp.ones((GMLP_CHUNK, GMLP_CHUNK), F32))

    def layer_params(l, w):
        q = {}
        q["wm"] = (a["a_ws"][l] * tril).astype(BF16)
        q["wm_t"] = jnp.swapaxes(q["wm"], 1, 2)
        q["bs_t"] = jnp.repeat(a["a_bs"][l].T, HEAD_DIM, axis=1)
        for nm in ("b_wa", "b_wx", "d_w"):
            bd = _block_diag(a[nm][l]).astype(BF16)
            q[nm], q[nm + "_t"] = bd, bd.T
        for nm in ("a_ln_g", "a_ln_b", "b_conv_b", "b_ba", "b_bx", "b_lam", "d_scale"):
            q[nm] = a[nm][l].reshape(1, W_GRP)
        q["lb"] = lbs[l].reshape(1, W_GRP)
        q["ng"] = jnp.tile(a["c_norm_g"][l], N_HEADS).reshape(1, W_GRP)
        q["ffn_conv_b"] = a["ffn_conv_b"][l].reshape(1, 2 * f)
        q.update(w)
        return q

    saved, weights, params = [], [], []
    first_handle, _ = start_gather(0, GATHER_FIRST, x0, "a")
    rest_handle, _ = start_gather(0, GATHER_REST, x0, "b")
    xl = x0
    for l in range(nl):
        if l == 0:
            w = finish_gather(0, GATHER_FIRST, first_handle, xl, "a")
        else:
            w = finish_gather(l, GATHERED, next_handle, xl, "")
        s = {"x0": xl}
        s["h1"] = _rms_fwd(xl, a["norm1_g"][l], "rms1_fwd")
        token = None
        if 0 < l < nl - 1:
            next_handle, token = start_gather(l + 1, GATHERED, s["h1"], "")
        s["z"] = _mm(s["h1"], w["w_in"], "nn", out_dtype=F32, name="mm_z", after=token)
        q = layer_params(l, w)
        mix = _mix_a_fwd(s["z"], d, q["a_ln_g"], q["a_ln_b"], q["wm"], q["bs_t"])
        mix, s["hs"] = _mix_b_fwd(s["z"], mix, q["b_conv_w"], q["b_conv_b"], q["b_wa"], q["b_wx"], q["b_ba"], q["b_bx"],
                                  q["b_lam"])
        mix, s["o_pre"], s["states"] = _mix_c_fwd(s["z"], mix, q["lb"], q["ng"])
        s["mix"] = _mix_d_fwd(s["z"], mix, q["d_w"], q["d_scale"])
        token = None
        if l == 0:
            w.update(finish_gather(0, GATHER_REST, rest_handle, s["mix"], "b"))
            q.update(w)
            if nl > 1:
                next_handle, token = start_gather(1, GATHERED, w["w_out"], "")
        s["x1"] = _mm(s["mix"], w["w_out"], "nn", res=xl, out_dtype=F32, name="mm_out", after=token)
        s["h2"] = _rms_fwd(s["x1"], a["norm2_g"][l], "rms2_fwd")
        s["hf_g"] = _mm(s["h2"], w["w_up"], "nn", b_slabs=True, n=f, out_dtype=F32, name="mm_up_g")
        s["hf_v"] = _mm(s["h2"], w["w_up"], "nn", b_slabs=True, n=f, b_noff=f, out_dtype=F32, name="mm_up_v")
        s["act"] = _ffn_act_fwd(s["hf_g"], s["hf_v"], q["ffn_conv_w"], q["ffn_conv_b"])
        s["x2"] = _mm(s["act"], w["w_down"], "nn", res=s["x1"], out_dtype=F32, name="mm_down")
        s["h3"] = _rms_fwd(s["x2"], a["norm3_g"][l], "rms3_fwd")
        s["pre"] = _mm(s["h3"], w["w_pg"], "nn", out_dtype=F32, name="mm_pg")
        s["pe"] = _mm(a["p"][l, 0], w["w_pe"], "nn", b_slabs=True, out_dtype=F32, name="mm_pe")
        xl = _ple_fwd(s["x2"], s["pe"], s["pre"])
        saved.append(s)
        weights.append(w)
        params.append(q)

    dx, g_final, loss = _final_loss(xl, a["final_g"], target)
    loss = lax.psum(loss[0, 0], ("x", "y", "c"))

    stacked = {n: None for n in BIG}
    small_sums = {}

    def finish_push(l, names, handle, tag, after):
        send, recv, srcs, slots = handle
        srcs, slots = _push_wait(send, recv, srcs, slots, after, f"push_wait_{l}{tag}")
        halves = []
        for n, g, sl in zip(names, srcs, slots):
            hr = g.shape[1] // 2
            own = lax.dynamic_slice(g, (my_shard, cc * hr, 0), (1, hr, g.shape[2]))[0]
            halves.append(_slot_sum(own, sl, "sum_" + n))
        for n, g in zip(names, _swap_halves(halves, "swap_halves_" + tag)):
            stacked[n] = _adamw_layer(a[n], g, a["m_" + n], a["v_" + n], l, stacked[n], "adamw_" + n)
        if len(srcs) > len(names):
            by_sender = lax.dynamic_update_slice(slots[-1], srcs[-1][None], (2 * my_shard + cc, 0, 0))
            small_sums[l, tag] = _slot_sum(None, by_sender, "sum_small_" + tag)
        return stacked[names[-1]][1]

    pending = []
    token = None
    for l in reversed(range(nl)):
        q, s, w = params[l], saved[l], weights[l]
        gs = {}
        dpe, dpre = _ple_bwd(dx, s["pe"], s["pre"], after=token)
        g_pe = _mm(a["p"][l, 0], dpe, "tn", out_dtype=BF16, name="mm_dwpe", out_slabs=N_CHIPS)
        g_pg = _mm(s["h3"], dpre, "tn", out_dtype=BF16, name="mm_dwpg")
        dh3 = _mm(dpre, w["w_pg"], "nt", out_dtype=BF16, name="mm_dh3")
        dx2, gs["norm3_g"] = _rms_bwd(dh3, s["x2"], a["norm3_g"][l], dx, "rms3_bwd")
        g_down = _mm(s["act"], dx2, "tn", out_dtype=BF16, name="mm_dwdown")
        dact = _mm(dx2, w["w_down"], "nt", out_dtype=BF16, name="mm_dact")
        dhf_g, dhf_v, dcw_g, dcw_v, dcb_g, dcb_v = _ffn_act_bwd(dact, s["hf_g"], s["hf_v"], q["ffn_conv_w"], q["ffn_conv_b"])
        gs["ffn_conv_w"] = jnp.concatenate([dcw_g, dcw_v], axis=1)
        gs["ffn_conv_b"] = jnp.concatenate([dcb_g, dcb_v], axis=1)
        g_up = _mm(s["h2"], dhf_g, "tn", out_dtype=BF16, name="mm_dwup_g", out_slabs=N_CHIPS, out_n=2 * f)
        g_up = _mm(s["h2"], dhf_v, "tn", out_dtype=BF16, name="mm_dwup_v", out_slabs=N_CHIPS, out_n=2 * f, o_noff=f, out_buf=g_up)
        early = {"w_pe": g_pe, "w_up": g_up, "w_pg": g_pg.reshape(N_CHIPS, -1, g_pg.shape[-1]),
                 "w_down": g_down.reshape(N_CHIPS, -1, g_down.shape[-1])}
        early_handle, token = _push_start([early[n] for n in PUSH_EARLY], None, f"push_start_{l}a")
        dh2 = _mm(dhf_g, w["w_up"], "nt", b_slabs=True, out_dtype=F32, name="mm_dh2_g", after=token)
        dh2 = _mm(dhf_v, w["w_up"], "nt", b_slabs=True, b_koff=f, res=dh2, out_dtype=F32, name="mm_dh2_v")
        dx1, gs["norm2_g"] = _rms_bwd(dh2, s["x1"], a["norm2_g"][l], dx2, "rms2_bwd")
        g_out = _mm(s["mix"], dx1, "tn", out_dtype=BF16, name="mm_dwout")
        dmix = _mm(dx1, w["w_out"], "nt", out_dtype=F32, name="mm_dmix")
        dz, gs["a_ln_g"], gs["a_ln_b"], dws, dbs_t = _mix_a_bwd(s["z"], dmix, q["a_ln_g"], q["a_ln_b"], q["wm"], q["wm_t"],
                                                               q["bs_t"])
        gs["a_ws"] = dws * tril
        gs["a_bs"] = dbs_t.reshape(GMLP_CHUNK, N_HEADS, HEAD_DIM).sum(-1).T
        dz, gs["b_conv_w"], gs["b_conv_b"], dwa, dwx, gs["b_ba"], gs["b_bx"], gs["b_lam"] = _mix_b_bwd(
            s["z"], dz, dmix, s["hs"], q["b_conv_w"], q["b_conv_b"], q["b_wa"], q["b_wx"], q["b_wa_t"], q["b_wx_t"],
            q["b_ba"], q["b_bx"], q["b_lam"])
        gs["b_wa"], gs["b_wx"] = _diag_blocks(dwa), _diag_blocks(dwx)
        dz, gs["c_lb"], dng = _mix_c_bwd(s["z"], dz, dmix, s["o_pre"], s["states"], q["lb"], q["ng"])
        gs["c_norm_g"] = dng.reshape(N_HEADS, HEAD_DIM).sum(0)
        dz, dwd, gs["d_scale"] = _mix_d_bwd(s["z"], dz, dmix, q["d_w"], q["d_w_t"], q["d_scale"])
        gs["d_w"] = _diag_blocks(dwd)
        small = [gs[n] for n in SMALL_MID] + ([g_final] if l == nl - 1 else [])
        mid_handle, token = _push_start([g_out.reshape(N_CHIPS, -1, g_out.shape[-1])], _pack_small(small), f"push_start_{l}b")
        g_in = _mm(s["h1"], dz, "tn", out_dtype=BF16, name="mm_dwin")
        dh1 = _mm(dz, w["w_in"], "nt", out_dtype=BF16, name="mm_dh1", after=token)
        dx, gs["norm1_g"] = _rms_bwd(dh1, s["x0"], a["norm1_g"][l], dx1, "rms1_bwd", after=g_in)

        late_handle, token = _push_start([_cols_to_slabs(g_in)], _pack_small([gs["norm1_g"]]), f"push_start_{l}c")
        dep = token
        for push in pending:
            dep = finish_push(*push, dep)
        pending = [(l, PUSH_EARLY, early_handle, "a"), (l, PUSH_MID, mid_handle, "b"), (l, PUSH_LATE, late_handle, "c")]
    for push in pending:
        dep = finish_push(*push, dep)
    grad_x = dx[None]

    def small_shape(n):
        return a[n].shape[1:-1] + (a[n].shape[-1] * N_CHIPS,) if n in SMALL_SHARDED else a[n].shape[1:]

    per_layer = {n: [] for n in SMALL_PER_LAYER}
    for l in range(nl):
        per_layer["norm1_g"].append(small_sums[l, "c"].reshape(-1)[:d])
        vec, off = small_sums[l, "b"].reshape(-1), 0
        for n in SMALL_MID:
            shape = small_shape(n)
            size = 1
            for dim in shape:
                size *= dim
            per_layer[n].append(vec[off:off + size].reshape(shape))
            off += size
        if l == nl - 1:
            grad_final = vec[off:off + d]
    grads = {n: jnp.stack(per_layer[n]) for n in SMALL_PER_LAYER}
    grads["c_lb"] = lbs_vjp(grads["c_lb"])[0]
    grads["final_g"] = grad_final
    for n in SMALL_SHARDED:
        cs = a[n].shape[-1]
        grads[n] = lax.dynamic_slice_in_dim(grads[n], my_shard * cs, cs, axis=2)

    outs = {}
    for n in WEIGHTS:
        if n in BIG:
            outs[n] = stacked[n]
        else:
            outs[n] = (grads[n],) + _adamw(a[n], grads[n], a["m_" + n], a["v_" + n], "adamw_" + n)
    return (loss, grad_x, *[outs[n][0] for n in WEIGHTS], *[outs[n][1] for n in WEIGHTS], *[outs[n][2] for n in WEIGHTS],
            *[outs[n][3] for n in WEIGHTS])
```

```python
import functools

import jax
import jax.numpy as jnp
from jax import lax
from jax.experimental import pallas as pl
from jax.experimental.pallas import tpu as pltpu

F32 = jnp.float32
BF16 = jnp.bfloat16
EPS = 1e-6
HEAD_DIM = 64
N_HEADS = 4
W_GRP = HEAD_DIM * N_HEADS
GMLP_CHUNK = 128
HGRN_CHUNK = 64
HGRN_UNROLL = 4
RGLRU_C = 8.0
POOL_HALO = 16
EXP_CLAMP = 80.0
ADAM_LR, ADAM_B1, ADAM_B2, ADAM_EPS, ADAM_WD, ADAM_STEP = 0.001, 0.9, 0.999, 1e-08, 0.01, 10
VMEM_LIMIT_BYTES = 56 * 1024 * 1024
TILE_PREFS = (1024, 1408, 768, 512, 256, 128)
ROW_TILE_PREFS = (512, 256, 128, 64, 32, 16, 8)
MESH_ID = pl.DeviceIdType.MESH
N_DEV = 8


def _pick(n, prefs=TILE_PREFS):
    for p in prefs:
        if n % p == 0:
            return p
    return n


def _cp(*sem):
    return pltpu.CompilerParams(dimension_semantics=sem if sem else None, vmem_limit_bytes=VMEM_LIMIT_BYTES)


def _sds(shape, dtype):
    return jax.ShapeDtypeStruct(tuple(shape), dtype)


_GELU_C = 0.7978845608028654
_GELU_A = 0.044715


def _gelu(x):
    return 0.5 * x * (1.0 + jnp.tanh(_GELU_C * (x + _GELU_A * x * x * x)))


def _gelu_and_grad(x):
    t = jnp.tanh(_GELU_C * (x + _GELU_A * x * x * x))
    g = 0.5 * x * (1.0 + t)
    dg = 0.5 * (1.0 + t) + 0.5 * x * (1.0 - t * t) * _GELU_C * (1.0 + 3.0 * _GELU_A * x * x)
    return g, dg


def _sigmoid(x):
    return 1.0 / (1.0 + jnp.exp(-x))


def _dot(a, b):
    return jnp.dot(a, b, preferred_element_type=F32)


def _dot_nt(a, b):
    return lax.dot_general(a, b, (((1,), (1,)), ((), ())), preferred_element_type=F32)


def _dot_tn(a, b):
    return lax.dot_general(a, b, (((0,), (0,)), ((), ())), preferred_element_type=F32)


def _split3(x):
    hi = x.astype(BF16)
    r1 = x - hi.astype(F32)
    mid = r1.astype(BF16)
    lo = (r1 - mid.astype(F32)).astype(BF16)
    return hi, mid, lo


def _dot_f32_rhs_exact(x, m_bf16):
    hi, mid, lo = _split3(x)
    return _dot(hi, m_bf16) + _dot(mid, m_bf16) + _dot(lo, m_bf16)


def _dot_f32_lhs_exact(m_bf16, x):
    hi, mid, lo = _split3(x)
    return _dot(m_bf16, hi) + _dot(m_bf16, mid) + _dot(m_bf16, lo)


def _head_masks(width=W_GRP):
    lane = lax.broadcasted_iota(jnp.int32, (1, width), 1)
    return [(lane >= h * HEAD_DIM) & (lane < (h + 1) * HEAD_DIM) for h in range(N_HEADS)]


def _block_mask(n=W_GRP):
    r = lax.broadcasted_iota(jnp.int32, (n, n), 0)
    c = lax.broadcasted_iota(jnp.int32, (n, n), 1)
    m = None
    for h in range(N_HEADS):
        mh = (r >= h * HEAD_DIM) & (r < (h + 1) * HEAD_DIM) & (c >= h * HEAD_DIM) & (c < (h + 1) * HEAD_DIM)
        m = mh if m is None else (m | mh)
    return m


def _mm(a, b, mode, *, out_dtype, name, res=None, b_slabs=False, n=None, b_noff=0, b_koff=0,
        out_slabs=0, out_buf=None, out_n=None, o_noff=0, after=None):
    if mode == "tn":
        k_dim, m_dim = a.shape
    else:
        m_dim, k_dim = a.shape
    if mode == "nt":
        n_dim = b.shape[-2]
    else:
        n_dim = n if n is not None else (b.shape[0] * b.shape[2] if b_slabs else b.shape[1])
    n_total = out_n if out_n is not None else n_dim
    tm, tn, tk = _pick(m_dim), _pick(n_dim), _pick(k_dim)
    if b_slabs and mode == "nt":
        tk = _pick(b.shape[2])
    elif b_slabs:
        tn = _pick(b.shape[2])
    elif out_slabs:
        tn = _pick(n_total // out_slabs)
    nk = k_dim // tk
    assert b_noff % tn == 0 and b_koff % tk == 0 and o_noff % tn == 0 and n_dim % tn == 0 and k_dim % tk == 0
    bn0, bk0, on0 = b_noff // tn, b_koff // tk, o_noff // tn
    dims = {"nn": (((1,), (0,)), ((), ())), "nt": (((1,), (1,)), ((), ())), "tn": (((0,), (0,)), ((), ()))}[mode]

    if mode == "tn":
        a_spec = pl.BlockSpec((tk, tm), lambda i, j, k: (k, i))
    else:
        a_spec = pl.BlockSpec((tm, tk), lambda i, j, k: (i, k))
    if not b_slabs:
        if mode == "nt":
            b_spec = pl.BlockSpec((tn, tk), lambda i, j, k: (j + bn0, k + bk0))
        else:
            b_spec = pl.BlockSpec((tk, tn), lambda i, j, k: (k + bk0, j + bn0))
    elif mode == "nt":
        bper = b.shape[2] // tk
        b_spec = pl.BlockSpec((None, tn, tk), lambda i, j, k: ((k + bk0) // bper, j, (k + bk0) % bper))
    else:
        bper = b.shape[2] // tn
        b_spec = pl.BlockSpec((None, tk, tn), lambda i, j, k: ((j + bn0) // bper, k, (j + bn0) % bper))
    in_specs = [a_spec, b_spec]
    args = [a, b]
    if res is not None:
        in_specs.append(pl.BlockSpec((tm, tn), lambda i, j, k: (i, j)))
        args.append(res)
    if out_slabs:
        oper = n_total // out_slabs // tn
        out_shape = _sds((out_slabs, m_dim, n_total // out_slabs), out_dtype)
        out_spec = pl.BlockSpec((None, tm, tn), lambda i, j, k: ((j + on0) // oper, i, (j + on0) % oper))
    else:
        out_shape = _sds((m_dim, n_total), out_dtype)
        out_spec = pl.BlockSpec((tm, tn), lambda i, j, k: (i, j + on0))
    aliases = {}
    if out_buf is not None:
        in_specs.append(pl.BlockSpec(memory_space=pl.ANY))
        args.append(out_buf)
        aliases = {len(args) - 1: 0}
    if after is not None:
        in_specs.append(pl.BlockSpec(memory_space=pl.ANY))
        args.append(after)
    has_res = res is not None
    n_in = len(args)

    def body(*refs):
        a_ref, b_ref = refs[0], refs[1]
        res_ref = refs[2] if has_res else None
        o_ref = refs[n_in]
        acc_ref = refs[-1] if nk > 1 else None
        part = lax.dot_general(a_ref[...].astype(BF16), b_ref[...].astype(BF16), dims, preferred_element_type=F32)

        def finish(v):
            if has_res:
                v = v + res_ref[...]
            o_ref[...] = v.astype(o_ref.dtype)

        if nk == 1:
            finish(part)
        else:
            kk = pl.program_id(2)

            @pl.when(kk == 0)
            def _():
                acc_ref[...] = part

            @pl.when(kk > 0)
            def _():
                acc_ref[...] += part

            @pl.when(kk == nk - 1)
            def _():
                finish(acc_ref[...])

    return pl.pallas_call(
        body, grid=(m_dim // tm, n_dim // tn, nk), in_specs=in_specs, out_specs=out_spec, out_shape=out_shape,
        scratch_shapes=[pltpu.VMEM((tm, tn), F32)] if nk > 1 else [],
        input_output_aliases=aliases, name=name, compiler_params=_cp("parallel", "parallel", "arbitrary"),
    )(*args)


def _rms_fwd(x, g, name):
    t, d = x.shape
    tm = _pick(t, ROW_TILE_PREFS)

    def body(x_ref, g_ref, o_ref):
        xv = x_ref[...]
        r = lax.rsqrt(jnp.mean(xv * xv, axis=-1, keepdims=True) + EPS)
        o_ref[...] = (xv * r * g_ref[...]).astype(o_ref.dtype)

    return pl.pallas_call(
        body, grid=(t // tm,),
        in_specs=[pl.BlockSpec((tm, d), lambda i: (i, 0)), pl.BlockSpec((1, d), lambda i: (0, 0))],
        out_specs=pl.BlockSpec((tm, d), lambda i: (i, 0)), out_shape=_sds((t, d), BF16),
        name=name, compiler_params=_cp("parallel"),
    )(x, g.reshape(1, d))


def _rms_bwd(dh, x, g, dres, name, after=None):
    t, d = x.shape
    tm = _pick(t, ROW_TILE_PREFS)
    extra = [] if after is None else [after]

    def body(dh_ref, x_ref, g_ref, dres_ref, *rest):
        dx_ref, dg_ref = rest[-2:]
        i = pl.program_id(0)
        xv = x_ref[...]
        dy = dh_ref[...].astype(F32)
        r = lax.rsqrt(jnp.mean(xv * xv, axis=-1, keepdims=True) + EPS)
        dyg = dy * g_ref[...]
        dot = jnp.mean(dyg * xv, axis=-1, keepdims=True)
        dx_ref[...] = dres_ref[...] + r * dyg - xv * (r * r * r) * dot
        part = jnp.sum(dy * xv * r, axis=0, keepdims=True)

        @pl.when(i == 0)
        def _():
            dg_ref[...] = part

        @pl.when(i > 0)
        def _():
            dg_ref[...] += part

    row = pl.BlockSpec((tm, d), lambda i: (i, 0))
    vec = pl.BlockSpec((1, d), lambda i: (0, 0))
    return pl.pallas_call(
        body, grid=(t // tm,), in_specs=[row, row, vec, row] + [pl.BlockSpec(memory_space=pl.ANY)] * len(extra),
        out_specs=[row, vec], out_shape=[_sds((t, d), F32), _sds((1, d), F32)], name=name, compiler_params=_cp("arbitrary"),
    )(dh, x, g.reshape(1, d), dres, *extra)


def _final_loss(x, g, target):
    t, d = x.shape
    tm = _pick(t, ROW_TILE_PREFS)

    def body(x_ref, g_ref, t_ref, dx_ref, dg_ref, loss_ref):
        i = pl.program_id(0)
        xv = x_ref[...]
        gv = g_ref[...]
        r = lax.rsqrt(jnp.mean(xv * xv, axis=-1, keepdims=True) + EPS)
        err = xv * r * gv - t_ref[...]
        lpart = (0.5 / d) * jnp.sum(jnp.sum(err * err, axis=1, keepdims=True), axis=0, keepdims=True)
        dy = err * (1.0 / d)
        dyg = dy * gv
        dot = jnp.mean(dyg * xv, axis=-1, keepdims=True)
        dx_ref[...] = r * dyg - xv * (r * r * r) * dot
        part = jnp.sum(dy * xv * r, axis=0, keepdims=True)

        @pl.when(i == 0)
        def _():
            dg_ref[...] = part
            loss_ref[...] = lpart

        @pl.when(i > 0)
        def _():
            dg_ref[...] += part
            loss_ref[...] += lpart

    row = pl.BlockSpec((tm, d), lambda i: (i, 0))
    vec = pl.BlockSpec((1, d), lambda i: (0, 0))
    return pl.pallas_call(
        body, grid=(t // tm,), in_specs=[row, vec, row], out_specs=[row, vec, pl.BlockSpec((1, 1), lambda i: (0, 0))],
        out_shape=[_sds((t, d), F32), _sds((1, d), F32), _sds((1, 1), F32)], name="final_loss",
        compiler_params=_cp("arbitrary"),
    )(x, g.reshape(1, d), target)


def _shift_down(ext, k, halo):
    return pltpu.roll(ext, k, 0)[halo:]


def _shift_up(ext, k, tm):
    return pltpu.roll(ext, ext.shape[0] - k, 0)[:tm]


def _ffn_tiles(t, f):
    return _pick(t, (256, 128, 64, 32, 16, 8)), _pick(f, (1408, 256, 128))


def _ffn_act_fwd(hf_g, hf_v, conv_w, conv_b):
    t, f = hf_g.shape
    tm, cn = _ffn_tiles(t, f)
    nf = f // cn

    def body(g_ref, v_ref, wg_ref, wv_ref, bg_ref, bv_ref, o_ref, gc_ref, vc_ref, ext_ref, hg_ref, hv_ref):
        i = pl.program_id(1)

        @pl.when(i == 0)
        def _():
            hg_ref[...] = jnp.zeros_like(hg_ref)
            hv_ref[...] = jnp.zeros_like(hv_ref)

        def conv(x_ref, halo_ref, w_ref, b_ref):
            ext_ref[0:8, :] = halo_ref[...]
            ext_ref[8:, :] = x_ref[...]
            halo_ref[...] = x_ref[tm - 8:tm, :]
            ext = ext_ref[...]
            w = w_ref[...]
            return b_ref[...] + w[2:3, :] * ext[8:] + w[1:2, :] * _shift_down(ext, 1, 8) + w[0:1, :] * _shift_down(ext, 2, 8)

        gc = conv(g_ref, hg_ref, wg_ref, bg_ref)
        vc = conv(v_ref, hv_ref, wv_ref, bv_ref)
        o_ref[...] = (_gelu(gc) * vc).astype(o_ref.dtype)
        gc_ref[...] = gc.astype(gc_ref.dtype)
        vc_ref[...] = vc.astype(vc_ref.dtype)

    blk = pl.BlockSpec((tm, cn), lambda j, i: (i, j))
    return pl.pallas_call(
        body, grid=(nf, t // tm),
        in_specs=[blk, blk, pl.BlockSpec((3, cn), lambda j, i: (0, j)), pl.BlockSpec((3, cn), lambda j, i: (0, j + nf)),
                  pl.BlockSpec((1, cn), lambda j, i: (0, j)), pl.BlockSpec((1, cn), lambda j, i: (0, j + nf))],
        out_specs=[blk, blk, blk], out_shape=[_sds((t, f), BF16)] * 3,
        scratch_shapes=[pltpu.VMEM((tm + 8, cn), F32), pltpu.VMEM((8, cn), F32), pltpu.VMEM((8, cn), F32)],
        name="ffn_act_fwd", compiler_params=_cp("parallel", "arbitrary"),
    )(hf_g, hf_v, conv_w, conv_w, conv_b, conv_b)


def _ffn_bwd(dx2, w_down, gc, vc, hf_g, hf_v, conv_w):
    t, f = hf_g.shape
    d = dx2.shape[1]
    tm, cn = _ffn_tiles(t, f)
    nf, nt = f // cn, t // tm

    def body(dx_ref, wd_ref, gc_ref, vc_ref, g_ref, v_ref, wg_ref, wv_ref, dg_ref, dv_ref, sg_ref, sv_ref,
             ext_ref, cg_ref, cv_ref):
        @pl.when(pl.program_id(1) == 0)
        def _():
            for ref in (cg_ref, cv_ref, sg_ref, sv_ref):
                ref[...] = jnp.zeros_like(ref)

        da = _dot_nt(dx_ref[...].astype(BF16), wd_ref[...])
        gel, dgel = _gelu_and_grad(gc_ref[...].astype(F32))
        dgc = da * vc_ref[...].astype(F32) * dgel
        dvc = da * gel

        def back(dc, carry_ref, w, x, out_ref, sums_ref):
            ext_ref[0:tm, :] = dc
            ext_ref[tm:tm + 8, :] = carry_ref[...]
            carry_ref[...] = dc[0:8, :]
            ext = ext_ref[...]
            up1, up2 = _shift_up(ext, 1, tm), _shift_up(ext, 2, tm)
            out_ref[...] = (w[2:3, :] * dc + w[1:2, :] * up1 + w[0:1, :] * up2).astype(out_ref.dtype)
            sums_ref[0:1, :] += jnp.sum(up2 * x, axis=0, keepdims=True)
            sums_ref[1:2, :] += jnp.sum(up1 * x, axis=0, keepdims=True)
            sums_ref[2:3, :] += jnp.sum(dc * x, axis=0, keepdims=True)
            sums_ref[3:4, :] += jnp.sum(dc, axis=0, keepdims=True)

        back(dgc, cg_ref, wg_ref[...], g_ref[...], dg_ref, sg_ref)
        back(dvc, cv_ref, wv_ref[...], v_ref[...], dv_ref, sv_ref)

    blk = pl.BlockSpec((tm, cn), lambda j, i: (nt - 1 - i, j))
    sums = pl.BlockSpec((8, cn), lambda j, i: (0, j))
    return pl.pallas_call(
        body, grid=(nf, nt),
        in_specs=[pl.BlockSpec((tm, d), lambda j, i: (nt - 1 - i, 0)), pl.BlockSpec((cn, d), lambda j, i: (j, 0)),
                  blk, blk, blk, blk, pl.BlockSpec((3, cn), lambda j, i: (0, j)), pl.BlockSpec((3, cn), lambda j, i: (0, j + nf))],
        out_specs=[blk, blk, sums, sums],
        out_shape=[_sds((t, f), BF16), _sds((t, f), BF16), _sds((8, f), F32), _sds((8, f), F32)],
        scratch_shapes=[pltpu.VMEM((tm + 8, cn), F32), pltpu.VMEM((8, cn), F32), pltpu.VMEM((8, cn), F32)],
        name="ffn_bwd", compiler_params=_cp("parallel", "arbitrary"),
    )(dx2, w_down, gc, vc, hf_g, hf_v, conv_w, conv_w)


def _ple_fwd(x2, pe, pre):
    t, d = x2.shape
    tm = _pick(t, ROW_TILE_PREFS)

    def body(x_ref, pe_ref, pre_ref, o_ref):
        o_ref[...] = x_ref[...] + pe_ref[...] * _sigmoid(pre_ref[...])

    row = pl.BlockSpec((tm, d), lambda i: (i, 0))
    return pl.pallas_call(body, grid=(t // tm,), in_specs=[row, row, row], out_specs=row,
                          out_shape=_sds((t, d), F32), name="ple_fwd", compiler_params=_cp("parallel"))(x2, pe, pre)


def _ple_bwd(dx3, pe, pre, after=None):
    t, d = dx3.shape
    tm = _pick(t, ROW_TILE_PREFS)

    def body(dx_ref, pe_ref, pre_ref, *rest):
        dpe_ref, dpre_ref = rest[-2:]
        gate = _sigmoid(pre_ref[...])
        dx = dx_ref[...]
        dpe_ref[...] = (dx * gate).astype(dpe_ref.dtype)
        dpre_ref[...] = (dx * pe_ref[...] * gate * (1.0 - gate)).astype(dpre_ref.dtype)

    row = pl.BlockSpec((tm, d), lambda i: (i, 0))
    extra = [] if after is None else [after]
    return pl.pallas_call(body, grid=(t // tm,), in_specs=[row, row, row] + [pl.BlockSpec(memory_space=pl.ANY)] * len(extra),
                          out_specs=[row, row], out_shape=[_sds((t, d), BF16), _sds((t, d), BF16)], name="ple_bwd",
                          compiler_params=_cp("parallel"))(dx3, pe, pre, *extra)


def _mix_tm(t):
    return _pick(t, (512, 256, 128))


def _zblk(tm, col, rev_nt=None):
    if rev_nt is None:
        return pl.BlockSpec((tm, W_GRP), lambda i: (i, col))
    return pl.BlockSpec((tm, W_GRP), lambda i: (rev_nt - 1 - i, col))


def _full(shape):
    nd = len(shape)
    return pl.BlockSpec(tuple(shape), lambda i: (0,) * nd)


def _gmlp_sv(wm_ref, vnc, bs, hm):
    sv = bs
    for h in range(N_HEADS):
        sv = sv + jnp.where(hm[h], _dot(wm_ref[h], vnc), 0.0)
    return sv


def _layernorm(v, g, b):
    mu = jnp.mean(v, axis=-1, keepdims=True)
    vc = v - mu
    rs = lax.rsqrt(jnp.mean(vc * vc, axis=-1, keepdims=True) + EPS)
    xhat = vc * rs
    return xhat, rs, xhat * g + b


def _mix_a_fwd(z, d_mix, ln_g, ln_b, wm, bs_t):
    t = z.shape[0]
    tm = _mix_tm(t)

    def body(u_ref, v_ref, g_ref, b_ref, wm_ref, bs_ref, o_ref):
        hm = _head_masks()
        ug = _gelu(u_ref[...])
        _, _, vn = _layernorm(_gelu(v_ref[...]), g_ref[...], b_ref[...])
        vnb = vn.astype(BF16)
        for n in range(tm // GMLP_CHUNK):
            sl = slice(n * GMLP_CHUNK, (n + 1) * GMLP_CHUNK)
            o_ref[sl, :] = ug[sl] * _gmlp_sv(wm_ref, vnb[sl], bs_ref[...], hm)

    return pl.pallas_call(
        body, grid=(t // tm,),
        in_specs=[_zblk(tm, 0), _zblk(tm, 1), _full((1, W_GRP)), _full((1, W_GRP)), _full(wm.shape), _full(bs_t.shape)],
        out_specs=_zblk(tm, 0), out_shape=_sds((t, d_mix), F32), name="mix_a_fwd", compiler_params=_cp("parallel"),
    )(z, z, ln_g, ln_b, wm, bs_t)


def _mix_a_bwd(z, dmix, ln_g, ln_b, wm, wm_t, bs_t):
    t, zc = z.shape
    tm = _mix_tm(t)

    def body(u_ref, v_ref, dy_ref, g_ref, b_ref, wm_ref, wmt_ref, bs_ref, dz_ref, dg_ref, db_ref, dws_ref, dbs_ref):
        i = pl.program_id(0)

        @pl.when(i == 0)
        def _():
            dg_ref[...] = jnp.zeros_like(dg_ref)
            db_ref[...] = jnp.zeros_like(db_ref)
            dws_ref[...] = jnp.zeros_like(dws_ref)
            dbs_ref[...] = jnp.zeros_like(dbs_ref)

        hm = _head_masks()
        ug, dug = _gelu_and_grad(u_ref[...])
        vg, dvg = _gelu_and_grad(v_ref[...])
        gv = g_ref[...]
        xhat, rs, vn = _layernorm(vg, gv, b_ref[...])
        vnb = vn.astype(BF16)
        dy = dy_ref[...]
        for n in range(tm // GMLP_CHUNK):
            sl = slice(n * GMLP_CHUNK, (n + 1) * GMLP_CHUNK)
            vnc = vnb[sl]
            sv = _gmlp_sv(wm_ref, vnc, bs_ref[...], hm)
            dsv = dy[sl] * ug[sl]
            dz_ref[sl, 0:W_GRP] = dy[sl] * sv * dug[sl]
            dbs_ref[...] += dsv
            dsvb = dsv.astype(BF16)
            dvn = jnp.zeros((GMLP_CHUNK, W_GRP), F32)
            for h in range(N_HEADS):
                dws_ref[h] += _dot_nt(jnp.where(hm[h], dsv, 0.0).astype(BF16), vnc)
                dvn = dvn + jnp.where(hm[h], _dot(wmt_ref[h], dsvb), 0.0)
            xh = xhat[sl]
            dg_ref[...] += jnp.sum(dvn * xh, axis=0, keepdims=True)
            db_ref[...] += jnp.sum(dvn, axis=0, keepdims=True)
            dxh = dvn * gv
            dvg_c = rs[sl] * (dxh - jnp.mean(dxh, axis=-1, keepdims=True) - xh * jnp.mean(dxh * xh, axis=-1, keepdims=True))
            dz_ref[sl, W_GRP:2 * W_GRP] = dvg_c * dvg[sl]

    return pl.pallas_call(
        body, grid=(t // tm,),
        in_specs=[_zblk(tm, 0), _zblk(tm, 1), _zblk(tm, 0), _full((1, W_GRP)), _full((1, W_GRP)), _full(wm.shape),
                  _full(wm_t.shape), _full(bs_t.shape)],
        out_specs=[pl.BlockSpec((tm, 2 * W_GRP), lambda i: (i, 0)), _full((1, W_GRP)), _full((1, W_GRP)),
                   _full(wm.shape), _full(bs_t.shape)],
        out_shape=[_sds((t, zc), F32), _sds((1, W_GRP), F32), _sds((1, W_GRP), F32), _sds(wm.shape, F32),
                   _sds(bs_t.shape, F32)],
        name="mix_a_bwd", compiler_params=_cp("arbitrary"),
    )(z, z, dmix, ln_g, ln_b, wm, wm_t, bs_t)


def _softplus(x):
    return jnp.maximum(x, 0.0) + jnp.log(1.0 + jnp.exp(-jnp.abs(x)))


def _neg_expm1(x):
    series = -x * (1.0 + x * 0.5 * (1.0 + x * (1.0 / 3.0) * (1.0 + x * 0.25 * (1.0 + x * 0.2))))
    return jnp.where(x > -0.1, series, 1.0 - jnp.exp(x))


def _rglru_gates(ext_ref, x_ref, halo, cw, cb, wa_ref, wx_ref, ba, bx, lam):
    ext_ref[0:8, :] = halo
    ext_ref[8:, :] = x_ref[...]
    ext = ext_ref[...]
    x0, x1, x2, x3 = ext[8:], _shift_down(ext, 1, 8), _shift_down(ext, 2, 8), _shift_down(ext, 3, 8)
    xc = cb + cw[3:4, :] * x0 + cw[2:3, :] * x1 + cw[1:2, :] * x2 + cw[0:1, :] * x3
    xcb = xc.astype(BF16)
    r = _sigmoid(_dot(xcb, wa_ref[...]) + ba)
    ig = _sigmoid(_dot(xcb, wx_ref[...]) + bx)
    sp = _softplus(-lam)
    la = -RGLRU_C * r * sp
    a = jnp.exp(la)
    mult = jnp.sqrt(_neg_expm1(2.0 * la))
    return (x0, x1, x2, x3), xc, r, ig, sp, a, mult


def _mix_b_fwd(z, mix, conv_w, conv_b, wa, wx, ba, bx, lam):
    t = z.shape[0]
    tm = _mix_tm(t)

    def body(x_ref, gb_ref, cw_ref, cb_ref, wa_ref, wx_ref, ba_ref, bx_ref, lam_ref, mix_in, o_ref, hs_ref,
             ext_ref, a_ref, b_ref, xh_ref, hc_ref):
        i = pl.program_id(0)

        @pl.when(i == 0)
        def _():
            xh_ref[...] = jnp.zeros_like(xh_ref)
            hc_ref[...] = jnp.zeros_like(hc_ref)

        _, xc, _, ig, _, a, mult = _rglru_gates(ext_ref, x_ref, xh_ref[...], cw_ref[...], cb_ref[...], wa_ref, wx_ref,
                                                ba_ref[...], bx_ref[...], lam_ref[...])
        xh_ref[...] = x_ref[tm - 8:tm, :]
        a_ref[...] = a
        b_ref[...] = mult * (ig * xc)
        rid = lax.broadcasted_iota(jnp.int32, (8, W_GRP), 0)

        def group(gi, hprev):
            base = pl.multiple_of(gi * 8, 8)
            ca = a_ref[pl.ds(base, 8), :]
            cb = b_ref[pl.ds(base, 8), :]
            for k in (1, 2, 4):
                m = rid >= k
                cb = jnp.where(m, ca * pltpu.roll(cb, k, 0) + cb, cb)
                ca = jnp.where(m, ca * pltpu.roll(ca, k, 0), ca)
            hh = cb + ca * hprev
            hs_ref[pl.ds(base, 8), :] = hh
            return hh[7:8, :]

        hlast = lax.fori_loop(0, tm // 8, group, hc_ref[0:1, :])
        hc_ref[...] = jnp.broadcast_to(hlast, hc_ref.shape)
        o_ref[...] = hs_ref[...] * _gelu(gb_ref[...])

    sq = _full((W_GRP, W_GRP))
    vec = _full((1, W_GRP))
    return pl.pallas_call(
        body, grid=(t // tm,),
        in_specs=[_zblk(tm, 2), _zblk(tm, 3), _full((4, W_GRP)), vec, sq, sq, vec, vec, vec, pl.BlockSpec(memory_space=pl.ANY)],
        out_specs=[_zblk(tm, 1), pl.BlockSpec((tm, W_GRP), lambda i: (i, 0))],
        out_shape=[_sds(mix.shape, F32), _sds((t, W_GRP), F32)],
        scratch_shapes=[pltpu.VMEM((tm + 8, W_GRP), F32), pltpu.VMEM((tm, W_GRP), F32), pltpu.VMEM((tm, W_GRP), F32),
                        pltpu.VMEM((8, W_GRP), F32), pltpu.VMEM((8, W_GRP), F32)],
        input_output_aliases={9: 0}, name="mix_b_fwd", compiler_params=_cp("arbitrary"),
    )(z, z, conv_w, conv_b, wa, wx, ba, bx, lam, mix)


def _mix_b_bwd(z, dz, dmix, hs, conv_w, conv_b, wa, wx, wa_t, wx_t, ba, bx, lam):
    t = z.shape[0]
    tm = _mix_tm(t)
    nt = t // tm
    hb = tm // 8

    def body(x_ref, gb_ref, xhalo_ref, hs_ref, hhalo_ref, dy_ref, cw_ref, cb_ref, wa_ref, wx_ref, wat_ref, wxt_ref,
             ba_ref, bx_ref, lam_ref, dz_in, dz_ref, dcw_ref, dcb_ref, dwa_ref, dwx_ref, dba_ref, dbx_ref, dlam_ref,
             ext_ref, c_ref, d_ref, g_ref, an_ref, gn_ref, dxn_ref):
        i = pl.program_id(0)
        first_tile = i == nt - 1

        @pl.when(i == 0)
        def _():
            for ref in (dcw_ref, dcb_ref, dwa_ref, dwx_ref, dba_ref, dbx_ref, dlam_ref, an_ref, gn_ref, dxn_ref):
                ref[...] = jnp.zeros_like(ref)

        cw, lam = cw_ref[...], lam_ref[...]
        xhalo = jnp.where(first_tile, 0.0, xhalo_ref[...])
        (x0, x1, x2, x3), xc, r, ig, sp, a, mult = _rglru_gates(
            ext_ref, x_ref, xhalo, cw, cb_ref[...], wa_ref, wx_ref, ba_ref[...], bx_ref[...], lam)
        hs = hs_ref[...]
        dy = dy_ref[...]
        gel, dgel = _gelu_and_grad(gb_ref[...])
        dz_ref[:, W_GRP:2 * W_GRP] = dy * hs * dgel

        ext_ref[0:tm, :] = a
        ext_ref[tm:tm + 8, :] = an_ref[...]
        an_ref[...] = a[0:8, :]
        c_ref[...] = _shift_up(ext_ref[...], 1, tm)
        d_ref[...] = dy * gel
        rid = lax.broadcasted_iota(jnp.int32, (8, W_GRP), 0)

        def group(j, gnext):
            base = pl.multiple_of((tm // 8 - 1 - j) * 8, 8)
            cc = c_ref[pl.ds(base, 8), :]
            cd = d_ref[pl.ds(base, 8), :]
            for k in (1, 2, 4):
                m = rid < 8 - k
                cd = jnp.where(m, cc * pltpu.roll(cd, 8 - k, 0) + cd, cd)
                cc = jnp.where(m, cc * pltpu.roll(cc, 8 - k, 0), cc)
            gg = cd + cc * gnext
            g_ref[pl.ds(base, 8), :] = gg
            return gg[0:1, :]

        gfirst = lax.fori_loop(0, tm // 8, group, gn_ref[0:1, :])
        gn_ref[...] = jnp.broadcast_to(gfirst, gn_ref.shape)
        g = g_ref[...]

        ext_ref[0:8, :] = jnp.where(first_tile, 0.0, hhalo_ref[...])
        ext_ref[8:, :] = hs
        hprev = _shift_down(ext_ref[...], 1, 8)
        da = g * hprev
        dmult = g * (ig * xc)
        di = g * mult * xc
        dxc = g * mult * ig
        dla = da * a - dmult * a * a / mult
        dr = dla * (-RGLRU_C * sp)
        dlam_ref[...] += jnp.sum(dla * (-RGLRU_C * r), axis=0, keepdims=True) * (-_sigmoid(-lam))
        dpr = dr * r * (1.0 - r)
        dpi = di * ig * (1.0 - ig)
        dprb, dpib, xcb = dpr.astype(BF16), dpi.astype(BF16), xc.astype(BF16)
        dba_ref[...] += jnp.sum(dpr, axis=0, keepdims=True)
        dbx_ref[...] += jnp.sum(dpi, axis=0, keepdims=True)
        dwa_ref[...] += _dot_tn(xcb, dprb)
        dwx_ref[...] += _dot_tn(xcb, dpib)
        dxc = dxc + _dot(dprb, wat_ref[...]) + _dot(dpib, wxt_ref[...])
        dcb_ref[...] += jnp.sum(dxc, axis=0, keepdims=True)
        dcw_ref[3:4, :] += jnp.sum(dxc * x0, axis=0, keepdims=True)
        dcw_ref[2:3, :] += jnp.sum(dxc * x1, axis=0, keepdims=True)
        dcw_ref[1:2, :] += jnp.sum(dxc * x2, axis=0, keepdims=True)
        dcw_ref[0:1, :] += jnp.sum(dxc * x3, axis=0, keepdims=True)
        ext_ref[0:tm, :] = dxc
        ext_ref[tm:tm + 8, :] = dxn_ref[...]
        dxn_ref[...] = dxc[0:8, :]
        ext = ext_ref[...]
        dz_ref[:, 0:W_GRP] = (cw[3:4, :] * dxc + cw[2:3, :] * _shift_up(ext, 1, tm) + cw[1:2, :] * _shift_up(ext, 2, tm)
                              + cw[0:1, :] * _shift_up(ext, 3, tm))

    sq = _full((W_GRP, W_GRP))
    vec = _full((1, W_GRP))
    halo = lambda col: pl.BlockSpec((8, W_GRP), lambda i: (jnp.maximum((nt - 1 - i) * hb - 1, 0), col))
    rev = lambda col: _zblk(tm, col, nt)
    return pl.pallas_call(
        body, grid=(nt,),
        in_specs=[rev(2), rev(3), halo(2), rev(0), halo(0), rev(1), _full((4, W_GRP)), vec, sq, sq, sq, sq, vec, vec, vec,
                  pl.BlockSpec(memory_space=pl.ANY)],
        out_specs=[pl.BlockSpec((tm, 2 * W_GRP), lambda i: (nt - 1 - i, 1)), _full((4, W_GRP)), vec, sq, sq, vec, vec, vec],
        out_shape=[_sds(dz.shape, F32), _sds((4, W_GRP), F32), _sds((1, W_GRP), F32), _sds((W_GRP, W_GRP), F32),
                   _sds((W_GRP, W_GRP), F32), _sds((1, W_GRP), F32), _sds((1, W_GRP), F32), _sds((1, W_GRP), F32)],
        scratch_shapes=[pltpu.VMEM((tm + 8, W_GRP), F32), pltpu.VMEM((tm, W_GRP), F32), pltpu.VMEM((tm, W_GRP), F32),
                        pltpu.VMEM((tm, W_GRP), F32), pltpu.VMEM((8, W_GRP), F32), pltpu.VMEM((8, W_GRP), F32),
                        pltpu.VMEM((8, W_GRP), F32)],
        input_output_aliases={15: 0}, name="mix_b_bwd", compiler_params=_cp("arbitrary"),
    )(z, z, z, hs, hs, dmix, conv_w, conv_b, wa, wx, wa_t, wx_t, ba, bx, lam, dz)


def _tri(n, lower):
    r = lax.broadcasted_iota(jnp.int32, (n, n), 0)
    c = lax.broadcasted_iota(jnp.int32, (n, n), 1)
    return jnp.where((r >= c) if lower else (r <= c), 1.0, 0.0).astype(BF16)


def _causal_stack():
    r = lax.broadcasted_iota(jnp.int32, (N_HEADS * HGRN_CHUNK, HGRN_CHUNK), 0)
    c = lax.broadcasted_iota(jnp.int32, (N_HEADS * HGRN_CHUNK, HGRN_CHUNK), 1)
    m = None
    for h in range(N_HEADS):
        mh = (r >= h * HGRN_CHUNK) & (r < (h + 1) * HGRN_CHUNK) & (r - h * HGRN_CHUNK >= c)
        m = mh if m is None else (m | mh)
    return m


def _stack_heads(x, hm):
    return jnp.concatenate([jnp.where(hm[h], x, 0.0) for h in range(N_HEADS)], axis=0)


def _unstack_heads(xs, hm):
    out = jnp.where(hm[0], xs[0:HGRN_CHUNK], 0.0)
    for h in range(1, N_HEADS):
        out = out + jnp.where(hm[h], xs[h * HGRN_CHUNK:(h + 1) * HGRN_CHUNK], 0.0)
    return out


def _hgrn_chunk(qv, fv, lb, tril):
    sq = _sigmoid(qv)
    qq = qv * sq
    sg = _sigmoid(fv)
    fg = lb + (1.0 - lb) * sg
    kk = 1.0 - fg
    bb = _dot_f32_lhs_exact(tril, jnp.log(fg))
    b_last = bb[HGRN_CHUNK - 1:HGRN_CHUNK, :]
    b_mid = bb[HGRN_CHUNK // 2 - 1:HGRN_CHUNK // 2, :]
    eq = jnp.exp(jnp.minimum(bb - b_mid, EXP_CLAMP))
    ek = jnp.exp(jnp.minimum(b_mid - bb, EXP_CLAMP))
    eb = jnp.exp(bb)
    el = jnp.exp(b_last - bb)
    return sq, qq, sg, fg, kk, b_last, eq, ek, eb, el


def _seg_mean(x, avg):
    return _dot_f32_rhs_exact(x, avg)


def _mix_c_fwd(z, mix, lb, ng):
    t = z.shape[0]
    tm = _mix_tm(t)
    nch = tm // HGRN_CHUNK

    def body(q_ref, f_ref, i_ref, g_ref, lb_ref, ng_ref, mix_in, y_ref, o_ref, ss_ref, s_ref):
        @pl.when(pl.program_id(0) == 0)
        def _():
            s_ref[...] = jnp.zeros_like(s_ref)

        hm = _head_masks()
        bmask = _block_mask()
        causal = _causal_stack()
        tril = _tri(HGRN_CHUNK, True)
        avg = jnp.where(bmask, 1.0 / HEAD_DIM, 0.0).astype(BF16)
        lb, ng = lb_ref[...], ng_ref[...]

        def chunk(c, carry):
            rows = pl.ds(pl.multiple_of(c * HGRN_CHUNK, HGRN_CHUNK), HGRN_CHUNK)
            vv = i_ref[rows, :]
            gv = g_ref[rows, :]
            _, qq, _, _, kk, b_last, eq, ek, eb, el = _hgrn_chunk(q_ref[rows, :], f_ref[rows, :], lb, tril)
            vb = vv.astype(BF16)
            qs = _stack_heads(qq * eq, hm).astype(BF16)
            att = jnp.where(causal, _dot_nt(qs, (kk * ek).astype(BF16)), 0.0)
            o = _unstack_heads(_dot(att.astype(BF16), vb), hm)
            s0 = s_ref[...]
            ss_ref[c] = s0
            o = o + _dot_nt((qq * eb).astype(BF16), s0.astype(BF16))
            s_ref[...] = s0 * jnp.exp(b_last) + jnp.where(bmask, _dot_tn(vb, (kk * el).astype(BF16)), 0.0)
            o_ref[rows, :] = o
            rstd = lax.rsqrt(_seg_mean(o * o, avg) + EPS)
            y_ref[rows, :] = o * rstd * ng * (gv * _sigmoid(gv))
            return carry

        lax.fori_loop(0, nch, chunk, 0, unroll=HGRN_UNROLL)

    vec = _full((1, W_GRP))
    return pl.pallas_call(
        body, grid=(t // tm,),
        in_specs=[_zblk(tm, 4), _zblk(tm, 5), _zblk(tm, 6), _zblk(tm, 7), vec, vec, pl.BlockSpec(memory_space=pl.ANY)],
        out_specs=[_zblk(tm, 2), pl.BlockSpec((tm, W_GRP), lambda i: (i, 0)),
                   pl.BlockSpec((nch, W_GRP, W_GRP), lambda i: (i, 0, 0))],
        out_shape=[_sds(mix.shape, F32), _sds((t, W_GRP), F32), _sds((t // HGRN_CHUNK, W_GRP, W_GRP), F32)],
        scratch_shapes=[pltpu.VMEM((W_GRP, W_GRP), F32)],
        input_output_aliases={6: 0}, name="mix_c_fwd", compiler_params=_cp("arbitrary"),
    )(z, z, z, z, lb, ng, mix)


def _mix_c_bwd(z, dz, dmix, o_pre, states, lb, ng):
    t = z.shape[0]
    tm = _mix_tm(t)
    nt = t // tm
    nch = tm // HGRN_CHUNK

    def body(q_ref, f_ref, i_ref, g_ref, o_ref, ss_ref, dy_ref, lb_ref, ng_ref, dz_in, dz_ref, dlb_ref, dng_ref, ds_ref):
        @pl.when(pl.program_id(0) == 0)
        def _():
            ds_ref[...] = jnp.zeros_like(ds_ref)
            dlb_ref[...] = jnp.zeros_like(dlb_ref)
            dng_ref[...] = jnp.zeros_like(dng_ref)

        hm = _head_masks()
        bmask = _block_mask()
        causal = _causal_stack()
        tril = _tri(HGRN_CHUNK, True)
        triu = _tri(HGRN_CHUNK, False)
        avg = jnp.where(bmask, 1.0 / HEAD_DIM, 0.0).astype(BF16)
        lb, ng = lb_ref[...], ng_ref[...]
        last_row = lax.broadcasted_iota(jnp.int32, (HGRN_CHUNK, W_GRP), 0) == HGRN_CHUNK - 1

        def chunk(j, carry):
            c = nch - 1 - j
            rows = pl.ds(pl.multiple_of(c * HGRN_CHUNK, HGRN_CHUNK), HGRN_CHUNK)
            qv, gv, vv = q_ref[rows, :], g_ref[rows, :], i_ref[rows, :]
            sq, qq, sg, fg, kk, b_last, eq, ek, eb, el = _hgrn_chunk(qv, f_ref[rows, :], lb, tril)
            s0 = ss_ref[c]
            ds1 = ds_ref[...]
            o = o_ref[rows, :]
            dy = dy_ref[rows, :]
            rstd = lax.rsqrt(_seg_mean(o * o, avg) + EPS)
            oh = o * rstd
            sgg = _sigmoid(gv)
            dz_ref[rows, 3 * W_GRP:4 * W_GRP] = dy * oh * ng * (sgg * (1.0 + gv * (1.0 - sgg)))
            don = dy * gv * sgg
            dng_ref[...] += jnp.sum(don * oh, axis=0, keepdims=True)
            doh = don * ng
            do = rstd * (doh - oh * _seg_mean(doh * oh, avg))
            qt, kt, qh, kh = qq * eq, kk * ek, qq * eb, kk * el
            vb, dob = vv.astype(BF16), do.astype(BF16)
            ktb, khb = kt.astype(BF16), kh.astype(BF16)
            ds1b = ds1.astype(BF16)
            qs = _stack_heads(qt, hm).astype(BF16)
            dos = _stack_heads(do, hm).astype(BF16)
            att = jnp.where(causal, _dot_nt(qs, ktb), 0.0).astype(BF16)
            datt = jnp.where(causal, _dot_nt(dos, vb), 0.0).astype(BF16)
            dv = _dot_tn(att, dos) + _dot_nt(khb, ds1b)
            dqt = _unstack_heads(_dot(datt, ktb), hm)
            dkt = _dot_tn(datt, qs)
            dqh = _dot(dob, s0.astype(BF16))
            dkh = _dot(vb, ds1b)
            e_last = jnp.exp(b_last)
            ds_ref[...] = ds1 * e_last + jnp.where(bmask, _dot_tn(dob, qh.astype(BF16)), 0.0)
            dq = dqt * eq + dqh * eb
            dk = dkt * ek + dkh * el
            db = qt * dqt - kt * dkt + qh * dqh - kh * dkh
            db_last = jnp.sum(kh * dkh, axis=0, keepdims=True) + e_last * jnp.sum(ds1 * s0, axis=0, keepdims=True)
            db = db + jnp.where(last_row, db_last, 0.0)
            dlogf = _dot_f32_lhs_exact(triu, db)
            dfg = dlogf / fg - dk
            dz_ref[rows, W_GRP:2 * W_GRP] = dfg * (1.0 - lb) * sg * (1.0 - sg)
            dlb_ref[...] += jnp.sum(dfg * (1.0 - sg), axis=0, keepdims=True)
            dz_ref[rows, 0:W_GRP] = dq * (sq * (1.0 + qv * (1.0 - sq)))
            dz_ref[rows, 2 * W_GRP:3 * W_GRP] = dv
            return carry

        lax.fori_loop(0, nch, chunk, 0, unroll=HGRN_UNROLL)

    vec = _full((1, W_GRP))
    rev = lambda col: _zblk(tm, col, nt)
    return pl.pallas_call(
        body, grid=(nt,),
        in_specs=[rev(4), rev(5), rev(6), rev(7), rev(0), pl.BlockSpec((nch, W_GRP, W_GRP), lambda i: (nt - 1 - i, 0, 0)),
                  rev(2), vec, vec, pl.BlockSpec(memory_space=pl.ANY)],
        out_specs=[pl.BlockSpec((tm, 4 * W_GRP), lambda i: (nt - 1 - i, 1)), vec, vec],
        out_shape=[_sds(dz.shape, F32), _sds((1, W_GRP), F32), _sds((1, W_GRP), F32)],
        scratch_shapes=[pltpu.VMEM((W_GRP, W_GRP), F32)],
        input_output_aliases={9: 0}, name="mix_c_bwd", compiler_params=_cp("arbitrary"),
    )(z, z, z, z, o_pre, states, dmix, lb, ng, dz)


def _pool_select(hm, s2, s4, s8, s16):
    return jnp.where(hm[0], s2, jnp.where(hm[1], s4, jnp.where(hm[2], s8, s16)))


def _pool_counts(hm, row0, tm):
    pos = (row0 + 1 + lax.broadcasted_iota(jnp.int32, (tm, W_GRP), 0)).astype(F32)
    win = _pool_select(hm, 2.0, 4.0, 8.0, 16.0)
    return jnp.minimum(pos, win)


def _pooled(ext_ref, x, halo, hm, cnt):
    ext_ref[0:POOL_HALO, :] = halo
    ext_ref[POOL_HALO:, :] = x
    e = ext_ref[...]
    s2 = e + pltpu.roll(e, 1, 0)
    s4 = s2 + pltpu.roll(s2, 2, 0)
    s8 = s4 + pltpu.roll(s4, 4, 0)
    s16 = s8 + pltpu.roll(s8, 8, 0)
    return _pool_select(hm, s2, s4, s8, s16)[POOL_HALO:] / cnt - x


def _mix_d_fwd(z, mix, wd, scale):
    t = z.shape[0]
    tm = _mix_tm(t)

    def body(x_ref, wd_ref, sc_ref, mix_in, o_ref, ext_ref, halo_ref):
        i = pl.program_id(0)

        @pl.when(i == 0)
        def _():
            halo_ref[...] = jnp.zeros_like(halo_ref)

        hm = _head_masks()
        x = x_ref[...]
        pooled = _pooled(ext_ref, x, halo_ref[...], hm, _pool_counts(hm, i * tm, tm))
        halo_ref[...] = x_ref[tm - POOL_HALO:tm, :]
        o_ref[...] = _dot(pooled.astype(BF16), wd_ref[...]) * sc_ref[...]

    return pl.pallas_call(
        body, grid=(t // tm,),
        in_specs=[_zblk(tm, 8), _full((W_GRP, W_GRP)), _full((1, W_GRP)), pl.BlockSpec(memory_space=pl.ANY)],
        out_specs=_zblk(tm, 3), out_shape=_sds(mix.shape, F32),
        scratch_shapes=[pltpu.VMEM((tm + POOL_HALO, W_GRP), F32), pltpu.VMEM((POOL_HALO, W_GRP), F32)],
        input_output_aliases={3: 0}, name="mix_d_fwd", compiler_params=_cp("arbitrary"),
    )(z, wd, scale, mix)


def _mix_d_bwd(z, dz, dmix, wd, wd_t, scale):
    t = z.shape[0]
    tm = _mix_tm(t)
    nt = t // tm
    hb = tm // POOL_HALO

    def body(x_ref, xhalo_ref, dy_ref, wd_ref, wdt_ref, sc_ref, dz_in, dz_ref, dwd_ref, dsc_ref, ext_ref, en_ref):
        i = pl.program_id(0)
        ri = nt - 1 - i

        @pl.when(i == 0)
        def _():
            en_ref[...] = jnp.zeros_like(en_ref)
            dwd_ref[...] = jnp.zeros_like(dwd_ref)
            dsc_ref[...] = jnp.zeros_like(dsc_ref)

        hm = _head_masks()
        cnt = _pool_counts(hm, ri * tm, tm)
        x = x_ref[...]
        pooled = _pooled(ext_ref, x, jnp.where(ri == 0, 0.0, xhalo_ref[...]), hm, cnt)
        pb = pooled.astype(BF16)
        dy = dy_ref[...]
        dsc_ref[...] += jnp.sum(dy * _dot(pb, wd_ref[...]), axis=0, keepdims=True)
        dyw = (dy * sc_ref[...]).astype(BF16)
        dwd_ref[...] += _dot_tn(pb, dyw)
        dpool = _dot(dyw, wdt_ref[...])
        e = dpool / cnt
        ext_ref[0:tm, :] = e
        ext_ref[tm:, :] = en_ref[...]
        en_ref[...] = e[0:POOL_HALO, :]
        ee = ext_ref[...]
        n = tm + POOL_HALO
        r2 = ee + pltpu.roll(ee, n - 1, 0)
        r4 = r2 + pltpu.roll(r2, n - 2, 0)
        r8 = r4 + pltpu.roll(r4, n - 4, 0)
        r16 = r8 + pltpu.roll(r8, n - 8, 0)
        dz_ref[...] = _pool_select(hm, r2, r4, r8, r16)[:tm] - dpool

    sq = _full((W_GRP, W_GRP))
    vec = _full((1, W_GRP))
    return pl.pallas_call(
        body, grid=(nt,),
        in_specs=[_zblk(tm, 8, nt), pl.BlockSpec((POOL_HALO, W_GRP), lambda i: (jnp.maximum((nt - 1 - i) * hb - 1, 0), 8)),
                  _zblk(tm, 3, nt), sq, sq, vec, pl.BlockSpec(memory_space=pl.ANY)],
        out_specs=[_zblk(tm, 8, nt), sq, vec],
        out_shape=[_sds(dz.shape, F32), _sds((W_GRP, W_GRP), F32), _sds((1, W_GRP), F32)],
        scratch_shapes=[pltpu.VMEM((tm + POOL_HALO, W_GRP), F32), pltpu.VMEM((POOL_HALO, W_GRP), F32)],
        input_output_aliases={6: 0}, name="mix_d_bwd", compiler_params=_cp("arbitrary"),
    )(z, z, dmix, wd, wd_t, scale, dz)


def _as2d(a):
    if a.ndim == 1:
        return a.reshape(1, a.shape[0])
    return a.reshape(-1, a.shape[-1])


def _adamw(w, g, m, v, name):
    shape = w.shape
    w2, g2, m2, v2 = _as2d(w), _as2d(g), _as2d(m), _as2d(v)
    rows, cols = w2.shape
    tr = _pick(rows, (1024, 512, 256, 128, 64, 32, 16, 8))
    if tr * cols * 4 * 14 > VMEM_LIMIT_BYTES:
        tr = _pick(rows, (256, 128, 64, 32, 16, 8))

    def body(w_ref, g_ref, m_ref, v_ref, d_ref, nm_ref, nv_ref):
        gv = g_ref[...]
        mn = ADAM_B1 * m_ref[...] + (1.0 - ADAM_B1) * gv
        vn = ADAM_B2 * v_ref[...] + (1.0 - ADAM_B2) * (gv * gv)
        m_hat = mn / (1.0 - ADAM_B1 ** ADAM_STEP)
        v_hat = vn / (1.0 - ADAM_B2 ** ADAM_STEP)
        d_ref[...] = -ADAM_LR * (m_hat / (jnp.sqrt(v_hat) + ADAM_EPS) + ADAM_WD * w_ref[...])
        nm_ref[...] = mn
        nv_ref[...] = vn

    blk = pl.BlockSpec((tr, cols), lambda i: (i, 0))
    outs = pl.pallas_call(
        body, grid=(rows // tr,), in_specs=[blk] * 4, out_specs=[blk] * 3, out_shape=[_sds((rows, cols), F32)] * 3,
        name=name, compiler_params=_cp("parallel"),
    )(w2, g2, m2, v2)
    return tuple(o.reshape(shape) for o in outs)


def _adamw_layer(w, g, m, v, layer, bufs, name):
    nl, r, cs = w.shape
    tr = _pick(r, (256, 128, 64, 32, 16, 8))

    def body(w_ref, g_ref, m_ref, v_ref, *rest):
        go_ref, d_ref, nm_ref, nv_ref = rest[-4:]
        gv = g_ref[...]
        mn = ADAM_B1 * m_ref[...] + (1.0 - ADAM_B1) * gv
        vn = ADAM_B2 * v_ref[...] + (1.0 - ADAM_B2) * (gv * gv)
        m_hat = mn / (1.0 - ADAM_B1 ** ADAM_STEP)
        v_hat = vn / (1.0 - ADAM_B2 ** ADAM_STEP)
        go_ref[...] = gv
        d_ref[...] = -ADAM_LR * (m_hat / (jnp.sqrt(v_hat) + ADAM_EPS) + ADAM_WD * w_ref[...])
        nm_ref[...] = mn
        nv_ref[...] = vn

    lay = pl.BlockSpec((None, tr, cs), lambda i: (layer, i, 0))
    in_specs = [lay, pl.BlockSpec((tr, cs), lambda i: (i, 0)), lay, lay]
    args = [w, g, m, v]
    aliases = {}
    if bufs is not None:
        in_specs += [pl.BlockSpec(memory_space=pl.ANY)] * 4
        args += list(bufs)
        aliases = {4 + i: i for i in range(4)}
    return pl.pallas_call(
        body, grid=(r // tr,), in_specs=in_specs, out_specs=[lay] * 4, out_shape=[_sds((nl, r, cs), F32)] * 4,
        input_output_aliases=aliases, name=name, compiler_params=_cp("parallel"),
    )(*args)


def _slot_sum(own, slots, name):
    n_slots, rows, cols = slots.shape
    whole_fits = rows * cols * 4 * (n_slots + 2) * 2 <= VMEM_LIMIT_BYTES // 2
    tr = rows if whole_fits else _pick(rows, (512, 352, 256, 128, 64, 32, 16, 8))

    def body(*refs):
        s_ref, o_ref = refs[-2], refs[-1]
        acc = s_ref[0].astype(F32) if own is None else refs[0][...].astype(F32) + s_ref[0].astype(F32)
        for k in range(1, n_slots):
            acc = acc + s_ref[k].astype(F32)
        o_ref[...] = acc

    row = pl.BlockSpec((tr, cols), lambda i: (i, 0))
    return pl.pallas_call(
        body, grid=(rows // tr,),
        in_specs=([] if own is None else [row]) + [pl.BlockSpec((n_slots, tr, cols), lambda i: (0, i, 0))],
        out_specs=row, out_shape=_sds((rows, cols), F32), name=name, compiler_params=_cp("parallel"),
    )(*(() if own is None else (own,)), slots)


def _me():
    return lax.axis_index("x"), lax.axis_index("y"), lax.axis_index("c")


def _other_chips(x, y):
    return [(1 - x, y), (x, 1 - y), (1 - x, 1 - y)]


ANY_SPEC = pl.BlockSpec(memory_space=pl.ANY)
HBM_SPEC = pl.BlockSpec(memory_space=pltpu.HBM)
SEM_SPEC = pl.BlockSpec(memory_space=pltpu.SEMAPHORE)
SPLIT_COPY_PARAMS = pltpu.CompilerParams(has_side_effects=pltpu.SideEffectType.DATAFLOW_SIDE_EFFECTING)
N_CHIPS = 4


def _aligned(v, m):
    return v if isinstance(v, int) else pl.multiple_of(v, m)


def _in_hbm(arr):
    return pltpu.with_memory_space_constraint(arr, pltpu.HBM)


def _peer(x, y, c, k):
    fx, fy, fc = (k >> 2) & 1, (k >> 1) & 1, k & 1
    px = 1 - x if fx else x
    py = 1 - y if fy else y
    pc = 1 - c if fc else c
    return px, py, pc


def _gather_start(shards, after, name):
    n = len(shards)

    def body(*refs):
        src, land = refs[:n], refs[n:2 * n]
        send_sems, recv_sems = refs[2 * n + 1], refs[2 * n + 2]
        token = refs[-1]
        x, y, c = _me()
        for w in range(n):
            for chip in _other_chips(x, y):
                pltpu.make_async_remote_copy(
                    src_ref=src[w], dst_ref=land[w].at[2 * x + y], send_sem=send_sems.at[w], recv_sem=recv_sems.at[w],
                    device_id=(*chip, c), device_id_type=MESH_ID).start()
        token[...] = jnp.zeros_like(token)

    lands = [lax.empty((N_CHIPS,) + s.shape, s.dtype) for s in shards]
    thru = [pltpu.HBM(s.shape, s.dtype) for s in shards] + [pltpu.HBM(z.shape, z.dtype) for z in lands]
    outs = pl.pallas_call(
        body, name=name,
        out_shape=(pltpu.SemaphoreType.DMA((n,)), pltpu.SemaphoreType.DMA((n,)), *thru, _sds((8, 128), F32)),
        in_specs=[HBM_SPEC] * (2 * n) + [ANY_SPEC],
        out_specs=(SEM_SPEC, SEM_SPEC, *[HBM_SPEC] * (2 * n), pl.BlockSpec(memory_space=pltpu.VMEM)),
        input_output_aliases={i: 2 + i for i in range(2 * n)}, compiler_params=SPLIT_COPY_PARAMS,
    )(*[_in_hbm(s) for s in shards], *[_in_hbm(z) for z in lands], after)
    return (outs[0], outs[1], outs[2:2 + n], outs[2 + n:2 + 2 * n]), outs[-1]


def _gather_wait(send_sems, recv_sems, srcs, lands, after, name):
    n = len(srcs)

    def body(*refs):
        land = refs[n:2 * n]
        send_sems, recv_sems = refs[2 * n], refs[2 * n + 1]
        x, y, c = _me()
        for w in range(n):
            three = land[w].at[pl.ds(0, N_CHIPS - 1)]
            cp = pltpu.make_async_remote_copy(src_ref=three, dst_ref=three, send_sem=send_sems.at[w], recv_sem=recv_sems.at[w],
                                              device_id=(x, y, c), device_id_type=MESH_ID)
            cp.wait_send()
            cp.wait_recv()

    both = list(srcs) + list(lands)
    outs = pl.pallas_call(
        body, name=name, out_shape=tuple(pltpu.HBM(b.shape, b.dtype) for b in both),
        in_specs=[HBM_SPEC] * (2 * n) + [SEM_SPEC, SEM_SPEC, ANY_SPEC], out_specs=[HBM_SPEC] * (2 * n),
        input_output_aliases={i: i for i in range(2 * n)}, compiler_params=SPLIT_COPY_PARAMS,
    )(*both, send_sems, recv_sems, after)
    return outs[n:2 * n]


def _push_start(grads, small, name):
    n = len(grads)
    srcs = list(grads) + ([] if small is None else [small])
    ns = len(srcs)

    def body(*refs):
        src, slots = refs[:ns], refs[ns:2 * ns]
        send_sems, recv_sems = refs[2 * ns], refs[2 * ns + 1]
        token = refs[-1]
        x, y, c = _me()
        for w in range(ns):
            for k in range(1, N_DEV):
                px, py, pc = _peer(x, y, c, k)
                if w < n:
                    hr = src[w].shape[1] // 2
                    piece = src[w].at[2 * px + py, pl.ds(_aligned(pc * hr, 16), hr), :]
                    slot = slots[w].at[k - 1]
                else:
                    piece = src[w]
                    slot = slots[w].at[4 * x + 2 * y + c]
                pltpu.make_async_remote_copy(
                    src_ref=piece, dst_ref=slot, send_sem=send_sems.at[w], recv_sem=recv_sems.at[w],
                    device_id=(px, py, pc), device_id_type=MESH_ID).start()
        token[...] = jnp.zeros_like(token)

    slots = [lax.empty((N_DEV - 1, g.shape[1] // 2, g.shape[2]), g.dtype) for g in grads]
    if small is not None:
        slots.append(lax.empty((N_DEV,) + small.shape, small.dtype))
    both = srcs + slots
    outs = pl.pallas_call(
        body, name=name,
        out_shape=(pltpu.SemaphoreType.DMA((ns,)), pltpu.SemaphoreType.DMA((ns,)),
                   *[pltpu.HBM(b.shape, b.dtype) for b in both], _sds((8, 128), F32)),
        in_specs=[HBM_SPEC] * len(both),
        out_specs=(SEM_SPEC, SEM_SPEC, *[HBM_SPEC] * len(both), pl.BlockSpec(memory_space=pltpu.VMEM)),
        input_output_aliases={i: 2 + i for i in range(len(both))}, compiler_params=SPLIT_COPY_PARAMS,
    )(*[_in_hbm(b) for b in both])
    return (outs[0], outs[1], outs[2:2 + ns], outs[2 + ns:2 + 2 * ns]), outs[-1]


def _push_wait(send_sems, recv_sems, srcs, slots, after, name):
    n = len(srcs)

    def body(*refs):
        slot = refs[n:2 * n]
        send_sems, recv_sems = refs[2 * n], refs[2 * n + 1]
        x, y, c = _me()
        for w in range(n):
            seven = slot[w].at[pl.ds(0, N_DEV - 1)]
            cp = pltpu.make_async_remote_copy(src_ref=seven, dst_ref=seven, send_sem=send_sems.at[w],
                                              recv_sem=recv_sems.at[w], device_id=(x, y, c), device_id_type=MESH_ID)
            cp.wait_send()
            cp.wait_recv()

    both = list(srcs) + list(slots)
    outs = pl.pallas_call(
        body, name=name, out_shape=tuple(pltpu.HBM(b.shape, b.dtype) for b in both),
        in_specs=[HBM_SPEC] * (2 * n) + [SEM_SPEC, SEM_SPEC, ANY_SPEC], out_specs=[HBM_SPEC] * (2 * n),
        input_output_aliases={i: i for i in range(2 * n)}, compiler_params=SPLIT_COPY_PARAMS,
    )(*both, send_sems, recv_sems, after)
    return outs[:n], outs[n:]


SWAP_CHUNK_BYTES = 2 * 1024 * 1024


def _swap_chunk_rows(hr, cs):
    ch = hr
    while ch * cs * 4 > SWAP_CHUNK_BYTES and ch % 16 == 0:
        ch //= 2
    return ch


def _swap_halves(halves, name):
    n = len(halves)
    chunk = [_swap_chunk_rows(*h.shape) for h in halves]
    rounds = max(h.shape[0] // ch for h, ch in zip(halves, chunk))

    def body(*refs):
        src, dst, buf = refs[:n], refs[n:2 * n], refs[2 * n:3 * n]
        load_sems, put_sems, send_sems, recv_sems = refs[3 * n:]
        x, y, c = _me()
        sibling = (x, y, 1 - c)
        for j in range(rounds):
            live = [w for w in range(n) if j < src[w].shape[0] // chunk[w]]
            loads = [pltpu.make_async_copy(src[w].at[pl.ds(j * chunk[w], chunk[w])], buf[w], load_sems.at[w]) for w in live]
            for ld in loads:
                ld.start()
            moves = []
            for ld, w in zip(loads, live):
                ld.wait()
                rows = pl.ds(_aligned(c * src[w].shape[0] + j * chunk[w], 8), chunk[w])
                put = pltpu.make_async_copy(buf[w], dst[w].at[rows], put_sems.at[w])
                send = pltpu.make_async_remote_copy(src_ref=buf[w], dst_ref=dst[w].at[rows], send_sem=send_sems.at[w],
                                                    recv_sem=recv_sems.at[w], device_id=sibling, device_id_type=MESH_ID)
                put.start()
                send.start()
                moves.append((put, send))
            for put, send in moves:
                put.wait()
                send.wait_send()
        for w in range(n):
            hr = src[w].shape[0]
            got = dst[w].at[pl.ds(_aligned((1 - c) * hr, 8), hr)]
            pltpu.make_async_remote_copy(src_ref=got, dst_ref=got, send_sem=send_sems.at[w], recv_sem=recv_sems.at[w],
                                         device_id=sibling, device_id_type=MESH_ID).wait_recv()

    return pl.pallas_call(
        body, in_specs=[ANY_SPEC] * n, out_specs=[ANY_SPEC] * n,
        out_shape=[_sds((2 * h.shape[0], h.shape[1]), F32) for h in halves],
        scratch_shapes=[pltpu.VMEM((ch, h.shape[1]), F32) for h, ch in zip(halves, chunk)]
        + [pltpu.SemaphoreType.DMA((n,))] * 4,
        name=name,
    )(*halves)


BIG = ("w_in", "w_out", "w_up", "w_down", "w_pe", "w_pg")
ROW_SHARDED = ("w_out", "w_down", "w_pg")
SMALL = ("norm1_g", "a_ln_g", "a_ln_b", "a_ws", "a_bs", "b_conv_w", "b_conv_b", "b_wa", "b_ba", "b_wx", "b_bx", "b_lam",
         "c_lb", "c_norm_g", "d_w", "d_scale", "norm2_g", "ffn_conv_w", "ffn_conv_b", "norm3_g", "final_g")
SMALL_SHARDED = ("b_conv_w", "ffn_conv_w")
WEIGHTS = ("norm1_g", "w_in", "a_ln_g", "a_ln_b", "a_ws", "a_bs", "b_conv_w", "b_conv_b", "b_wa", "b_ba", "b_wx", "b_bx",
           "b_lam", "c_lb", "c_norm_g", "d_w", "d_scale", "w_out", "norm2_g", "w_up", "ffn_conv_w", "ffn_conv_b", "w_down",
           "norm3_g", "w_pe", "w_pg", "final_g")
ARGS = ("x", "p") + WEIGHTS + ("loss_target",) + tuple("m_" + n for n in WEIGHTS) + tuple("v_" + n for n in WEIGHTS)


def _block_diag(w):
    eye = jnp.eye(N_HEADS, dtype=w.dtype)
    return (eye[:, None, :, None] * w[:, :, None, :]).reshape(W_GRP, W_GRP)


def _diag_blocks(m):
    m4 = m.reshape(N_HEADS, HEAD_DIM, N_HEADS, HEAD_DIM)
    return jnp.stack([m4[h, :, h, :] for h in range(N_HEADS)])


def _lower_bounds(c_lb):
    lbs = jnp.cumsum(jax.nn.softmax(c_lb, axis=0), axis=0)
    return lbs - lbs[0:1]


def kernel(x, p, norm1_g, w_in, a_ln_g, a_ln_b, a_ws, a_bs, b_conv_w, b_conv_b, b_wa, b_ba, b_wx, b_bx, b_lam, c_lb, c_norm_g, d_w, d_scale, w_out, norm2_g, w_up, ffn_conv_w, ffn_conv_b, w_down, norm3_g, w_pe, w_pg, final_g, loss_target, m_norm1_g, m_w_in, m_a_ln_g, m_a_ln_b, m_a_ws, m_a_bs, m_b_conv_w, m_b_conv_b, m_b_wa, m_b_ba, m_b_wx, m_b_bx, m_b_lam, m_c_lb, m_c_norm_g, m_d_w, m_d_scale, m_w_out, m_norm2_g, m_w_up, m_ffn_conv_w, m_ffn_conv_b, m_w_down, m_norm3_g, m_w_pe, m_w_pg, m_final_g, v_norm1_g, v_w_in, v_a_ln_g, v_a_ln_b, v_a_ws, v_a_bs, v_b_conv_w, v_b_conv_b, v_b_wa, v_b_ba, v_b_wx, v_b_bx, v_b_lam, v_c_lb, v_c_norm_g, v_d_w, v_d_scale, v_w_out, v_norm2_g, v_w_up, v_ffn_conv_w, v_ffn_conv_b, v_w_down, v_norm3_g, v_w_pe, v_w_pg, v_final_g):
    return _step((x, p, norm1_g, w_in, a_ln_g, a_ln_b, a_ws, a_bs, b_conv_w, b_conv_b, b_wa, b_ba, b_wx, b_bx, b_lam, c_lb, c_norm_g, d_w, d_scale, w_out, norm2_g, w_up, ffn_conv_w, ffn_conv_b, w_down, norm3_g, w_pe, w_pg, final_g, loss_target, m_norm1_g, m_w_in, m_a_ln_g, m_a_ln_b, m_a_ws, m_a_bs, m_b_conv_w, m_b_conv_b, m_b_wa, m_b_ba, m_b_wx, m_b_bx, m_b_lam, m_c_lb, m_c_norm_g, m_d_w, m_d_scale, m_w_out, m_norm2_g, m_w_up, m_ffn_conv_w, m_ffn_conv_b, m_w_down, m_norm3_g, m_w_pe, m_w_pg, m_final_g, v_norm1_g, v_w_in, v_a_ln_g, v_a_ln_b, v_a_ws, v_a_bs, v_b_conv_w, v_b_conv_b, v_b_wa, v_b_ba, v_b_wx, v_b_bx, v_b_lam, v_c_lb, v_c_norm_g, v_d_w, v_d_scale, v_w_out, v_norm2_g, v_w_up, v_ffn_conv_w, v_ffn_conv_b, v_w_down, v_norm3_g, v_w_pe, v_w_pg, v_final_g))


SMALL_PER_LAYER = tuple(n for n in SMALL if n != "final_g")
GATHERED = BIG + SMALL_SHARDED
GATHER_FIRST = ("w_in", "b_conv_w")
GATHER_REST = tuple(n for n in GATHERED if n not in GATHER_FIRST)
PUSH_EARLY = ("w_pe", "w_pg", "w_down", "w_up")
PUSH_MID = ("w_out",)
PUSH_LATE = ("w_in",)
SMALL_MID = tuple(n for n in SMALL_PER_LAYER if n != "norm1_g")


def _cols_to_slabs(m):
    r, c4 = m.shape
    return jnp.moveaxis(m.reshape(r, N_CHIPS, c4 // N_CHIPS), 1, 0)


def _slabs_to_cols(s):
    return jnp.moveaxis(s, 0, 1).reshape(s.shape[1], -1)


def _pack_small(parts):
    flat = jnp.concatenate([p.reshape(-1) for p in parts])
    return jnp.pad(flat, (0, (-flat.shape[0]) % 1024)).reshape(-1, 128)


def _step(args):
    a = dict(zip(ARGS, args, strict=True))
    x0 = a["x"][0]
    target = a["loss_target"][0]
    nl = a["norm1_g"].shape[0]
    t, d = x0.shape
    f = a["w_down"].shape[1] * N_CHIPS
    cx, cy, cc = _me()
    my_shard = 2 * cx + cy
    shards = {n: a[n].astype(BF16) for n in BIG}
    shards.update({n: a[n] for n in SMALL_SHARDED})

    def start_gather(l, names, after, tag):
        return _gather_start([shards[n][l] for n in names], after, f"gather_start_{l}{tag}")

    def finish_gather(l, names, handle, after, tag):
        send, recv, srcs, lands = handle
        lands = _gather_wait(send, recv, srcs, lands, after, f"gather_wait_{l}{tag}")
        w = {}
        for n, land in zip(names, lands):
            full = lax.dynamic_update_slice(land, shards[n][l][None], (my_shard, 0, 0))
            if n in ROW_SHARDED:
                w[n] = full.reshape(-1, full.shape[-1])
            elif n in ("w_up", "w_pe"):
                w[n] = full
            else:
                w[n] = _slabs_to_cols(full)
        return w

    lbs, lbs_vjp = jax.vjp(_lower_bounds, a["c_lb"])
    tril = jnp.tril(jnp.ones((GMLP_CHUNK, GMLP_CHUNK), F32))

    def layer_params(l, w):
        q = {}
        q["wm"] = (a["a_ws"][l] * tril).astype(BF16)
        q["wm_t"] = jnp.swapaxes(q["wm"], 1, 2)
        q["bs_t"] = jnp.repeat(a["a_bs"][l].T, HEAD_DIM, axis=1)
        for nm in ("b_wa", "b_wx", "d_w"):
            bd = _block_diag(a[nm][l]).astype(BF16)
            q[nm], q[nm + "_t"] = bd, bd.T
        for nm in ("a_ln_g", "a_ln_b", "b_conv_b", "b_ba", "b_bx", "b_lam", "d_scale"):
            q[nm] = a[nm][l].reshape(1, W_GRP)
        q["lb"] = lbs[l].reshape(1, W_GRP)
        q["ng"] = jnp.tile(a["c_norm_g"][l], N_HEADS).reshape(1, W_GRP)
        q["ffn_conv_b"] = a["ffn_conv_b"][l].reshape(1, 2 * f)
        q.update(w)
        return q

    saved, weights, params = [], [], []
    first_handle, _ = start_gather(0, GATHER_FIRST, x0, "a")
    rest_handle, _ = start_gather(0, GATHER_REST, x0, "b")
    xl = x0
    for l in range(nl):
        if l == 0:
            w = finish_gather(0, GATHER_FIRST, first_handle, xl, "a")
        else:
            w = finish_gather(l, GATHERED, next_handle, xl, "")
        s = {"x0": xl}
        s["h1"] = _rms_fwd(xl, a["norm1_g"][l], "rms1_fwd")
        token = None
        if 0 < l < nl - 1:
            next_handle, token = start_gather(l + 1, GATHERED, s["h1"], "")
        s["z"] = _mm(s["h1"], w["w_in"], "nn", out_dtype=F32, name="mm_z", after=token)
        q = layer_params(l, w)
        mix = _mix_a_fwd(s["z"], d, q["a_ln_g"], q["a_ln_b"], q["wm"], q["bs_t"])
        mix, s["hs"] = _mix_b_fwd(s["z"], mix, q["b_conv_w"], q["b_conv_b"], q["b_wa"], q["b_wx"], q["b_ba"], q["b_bx"],
                                  q["b_lam"])
        mix, s["o_pre"], s["states"] = _mix_c_fwd(s["z"], mix, q["lb"], q["ng"])
        s["mix"] = _mix_d_fwd(s["z"], mix, q["d_w"], q["d_scale"])
        token = None
        if l == 0:
            w.update(finish_gather(0, GATHER_REST, rest_handle, s["mix"], "b"))
            q.update(w)
            if nl > 1:
                next_handle, token = start_gather(1, GATHERED, w["w_out"], "")
        s["x1"] = _mm(s["mix"], w["w_out"], "nn", res=xl, out_dtype=F32, name="mm_out", after=token)
        s["h2"] = _rms_fwd(s["x1"], a["norm2_g"][l], "rms2_fwd")
        s["hf_g"] = _mm(s["h2"], w["w_up"], "nn", b_slabs=True, n=f, out_dtype=F32, name="mm_up_g")
        s["hf_v"] = _mm(s["h2"], w["w_up"], "nn", b_slabs=True, n=f, b_noff=f, out_dtype=F32, name="mm_up_v")
        s["act"], s["gc"], s["vc"] = _ffn_act_fwd(s["hf_g"], s["hf_v"], q["ffn_conv_w"], q["ffn_conv_b"])
        s["x2"] = _mm(s["act"], w["w_down"], "nn", res=s["x1"], out_dtype=F32, name="mm_down")
        s["h3"] = _rms_fwd(s["x2"], a["norm3_g"][l], "rms3_fwd")
        s["pre"] = _mm(s["h3"], w["w_pg"], "nn", out_dtype=F32, name="mm_pg")
        s["pe"] = _mm(a["p"][l, 0], w["w_pe"], "nn", b_slabs=True, out_dtype=F32, name="mm_pe")
        xl = _ple_fwd(s["x2"], s["pe"], s["pre"])
        saved.append(s)
        weights.append(w)
        params.append(q)

    dx, g_final, loss = _final_loss(xl, a["final_g"], target)
    loss = lax.psum(loss[0, 0], ("x", "y", "c"))

    stacked = {n: None for n in BIG}
    small_sums = {}

    def finish_push(l, names, handle, tag, after):
        send, recv, srcs, slots = handle
        srcs, slots = _push_wait(send, recv, srcs, slots, after, f"push_wait_{l}{tag}")
        halves = []
        for n, g, sl in zip(names, srcs, slots):
            hr = g.shape[1] // 2
            own = lax.dynamic_slice(g, (my_shard, cc * hr, 0), (1, hr, g.shape[2]))[0]
            halves.append(_slot_sum(own, sl, "sum_" + n))
        for n, g in zip(names, _swap_halves(halves, "swap_halves_" + tag)):
            stacked[n] = _adamw_layer(a[n], g, a["m_" + n], a["v_" + n], l, stacked[n], "adamw_" + n)
        if len(srcs) > len(names):
            by_sender = lax.dynamic_update_slice(slots[-1], srcs[-1][None], (2 * my_shard + cc, 0, 0))
            small_sums[l, tag] = _slot_sum(None, by_sender, "sum_small_" + tag)
        return stacked[names[-1]][1]

    pending = []
    token = None
    for l in reversed(range(nl)):
        q, s, w = params[l], saved[l], weights[l]
        gs = {}
        dpe, dpre = _ple_bwd(dx, s["pe"], s["pre"], after=token)
        g_pe = _mm(a["p"][l, 0], dpe, "tn", out_dtype=BF16, name="mm_dwpe", out_slabs=N_CHIPS)
        g_pg = _mm(s["h3"], dpre, "tn", out_dtype=BF16, name="mm_dwpg")
        dh3 = _mm(dpre, w["w_pg"], "nt", out_dtype=BF16, name="mm_dh3")
        dx2, gs["norm3_g"] = _rms_bwd(dh3, s["x2"], a["norm3_g"][l], dx, "rms3_bwd")
        g_down = _mm(s["act"], dx2, "tn", out_dtype=BF16, name="mm_dwdown")
        dhf_g, dhf_v, sums_g, sums_v = _ffn_bwd(dx2, w["w_down"], s["gc"], s["vc"], s["hf_g"], s["hf_v"], q["ffn_conv_w"])
        gs["ffn_conv_w"] = jnp.concatenate([sums_g[0:3], sums_v[0:3]], axis=1)
        gs["ffn_conv_b"] = jnp.concatenate([sums_g[3:4], sums_v[3:4]], axis=1)
        g_up = _mm(s["h2"], dhf_g, "tn", out_dtype=BF16, name="mm_dwup_g", out_slabs=N_CHIPS, out_n=2 * f)
        g_up = _mm(s["h2"], dhf_v, "tn", out_dtype=BF16, name="mm_dwup_v", out_slabs=N_CHIPS, out_n=2 * f, o_noff=f, out_buf=g_up)
        early = {"w_pe": g_pe, "w_up": g_up, "w_pg": g_pg.reshape(N_CHIPS, -1, g_pg.shape[-1]),
                 "w_down": g_down.reshape(N_CHIPS, -1, g_down.shape[-1])}
        early_handle, token = _push_start([early[n] for n in PUSH_EARLY], None, f"push_start_{l}a")
        dh2 = _mm(dhf_g, w["w_up"], "nt", b_slabs=True, out_dtype=F32, name="mm_dh2_g", after=token)
        dh2 = _mm(dhf_v, w["w_up"], "nt", b_slabs=True, b_koff=f, res=dh2, out_dtype=F32, name="mm_dh2_v")
        dx1, gs["norm2_g"] = _rms_bwd(dh2, s["x1"], a["norm2_g"][l], dx2, "rms2_bwd")
        g_out = _mm(s["mix"], dx1, "tn", out_dtype=BF16, name="mm_dwout")
        dmix = _mm(dx1, w["w_out"], "nt", out_dtype=F32, name="mm_dmix")
        dz, gs["a_ln_g"], gs["a_ln_b"], dws, dbs_t = _mix_a_bwd(s["z"], dmix, q["a_ln_g"], q["a_ln_b"], q["wm"], q["wm_t"],
                                                               q["bs_t"])
        gs["a_ws"] = dws * tril
        gs["a_bs"] = dbs_t.reshape(GMLP_CHUNK, N_HEADS, HEAD_DIM).sum(-1).T
        dz, gs["b_conv_w"], gs["b_conv_b"], dwa, dwx, gs["b_ba"], gs["b_bx"], gs["b_lam"] = _mix_b_bwd(
            s["z"], dz, dmix, s["hs"], q["b_conv_w"], q["b_conv_b"], q["b_wa"], q["b_wx"], q["b_wa_t"], q["b_wx_t"],
            q["b_ba"], q["b_bx"], q["b_lam"])
        gs["b_wa"], gs["b_wx"] = _diag_blocks(dwa), _diag_blocks(dwx)
        dz, gs["c_lb"], dng = _mix_c_bwd(s["z"], dz, dmix, s["o_pre"], s["states"], q["lb"], q["ng"])
        gs["c_norm_g"] = dng.reshape(N_HEADS, HEAD_DIM).sum(0)
        dz, dwd, gs["d_scale"] = _mix_d_bwd(s["z"], dz, dmix, q["d_w"], q["d_w_t"], q["d_scale"])
        gs["d_w"] = _diag_blocks(dwd)
        small = [gs[n] for n in SMALL_MID] + ([g_final] if l == nl - 1 else [])
        mid_handle, token = _push_start([g_out.reshape(N_CHIPS, -1, g_out.shape[-1])], _pack_small(small), f"push_start_{l}b")
        g_in = _mm(s["h1"], dz, "tn", out_dtype=BF16, name="mm_dwin")
        dh1 = _mm(dz, w["w_in"], "nt", out_dtype=BF16, name="mm_dh1", after=token)
        dx, gs["norm1_g"] = _rms_bwd(dh1, s["x0"], a["norm1_g"][l], dx1, "rms1_bwd", after=g_in)

        late_handle, token = _push_start([_cols_to_slabs(g_in)], _pack_small([gs["norm1_g"]]), f"push_start_{l}c")
        dep = token
        for push in pending:
            dep = finish_push(*push, dep)
        pending = [(l, PUSH_EARLY, early_handle, "a"), (l, PUSH_MID, mid_handle, "b"), (l, PUSH_LATE, late_handle, "c")]
    for push in pending:
        dep = finish_push(*push, dep)
    grad_x = dx[None]

    def small_shape(n):
        return a[n].shape[1:-1] + (a[n].shape[-1] * N_CHIPS,) if n in SMALL_SHARDED else a[n].shape[1:]

    per_layer = {n: [] for n in SMALL_PER_LAYER}
    for l in range(nl):
        per_layer["norm1_g"].append(small_sums[l, "c"].reshape(-1)[:d])
        vec, off = small_sums[l, "b"].reshape(-1), 0
        for n in SMALL_MID:
            shape = small_shape(n)
            size = 1
            for dim in shape:
                size *= dim
            per_layer[n].append(vec[off:off + size].reshape(shape))
            off += size
        if l == nl - 1:
            grad_final = vec[off:off + d]
    grads = {n: jnp.stack(per_layer[n]) for n in SMALL_PER_LAYER}
    grads["c_lb"] = lbs_vjp(grads["c_lb"])[0]
    grads["final_g"] = grad_final
    for n in SMALL_SHARDED:
        cs = a[n].shape[-1]
        grads[n] = lax.dynamic_slice_in_dim(grads[n], my_shard * cs, cs, axis=2)

    outs = {}
    for n in WEIGHTS:
        if n in BIG:
            outs[n] = stacked[n]
        else:
            outs[n] = (grads[n],) + _adamw(a[n], grads[n], a["m_" + n], a["v_" + n], "adamw_" + n)
    return (loss, grad_x, *[outs[n][0] for n in WEIGHTS], *[outs[n][1] for n in WEIGHTS], *[outs[n][2] for n in WEIGHTS],
            *[outs[n][3] for n in WEIGHTS])
```

```python
import functools

import jax
import jax.numpy as jnp
from jax import lax
from jax.experimental import pallas as pl
from jax.experimental.pallas import tpu as pltpu

F32 = jnp.float32
BF16 = jnp.bfloat16
EPS = 1e-6
HEAD_DIM = 64
N_HEADS = 4
W_GRP = HEAD_DIM * N_HEADS
GMLP_CHUNK = 128
HGRN_CHUNK = 64
HGRN_UNROLL = 4
RGLRU_C = 8.0
POOL_HALO = 16
EXP_CLAMP = 80.0
ADAM_LR, ADAM_B1, ADAM_B2, ADAM_EPS, ADAM_WD, ADAM_STEP = 0.001, 0.9, 0.999, 1e-08, 0.01, 10
VMEM_LIMIT_BYTES = 56 * 1024 * 1024
TILE_PREFS = (1024, 1408, 768, 512, 256, 128)
ROW_TILE_PREFS = (512, 256, 128, 64, 32, 16, 8)
MESH_ID = pl.DeviceIdType.MESH
N_DEV = 8


def _pick(n, prefs=TILE_PREFS):
    for p in prefs:
        if n % p == 0:
            return p
    return n


def _cp(*sem):
    return pltpu.CompilerParams(dimension_semantics=sem if sem else None, vmem_limit_bytes=VMEM_LIMIT_BYTES)


def _sds(shape, dtype):
    return jax.ShapeDtypeStruct(tuple(shape), dtype)


_GELU_C = 0.7978845608028654
_GELU_A = 0.044715


def _gelu(x):
    return 0.5 * x * (1.0 + jnp.tanh(_GELU_C * (x + _GELU_A * x * x * x)))


def _gelu_and_grad(x):
    t = jnp.tanh(_GELU_C * (x + _GELU_A * x * x * x))
    g = 0.5 * x * (1.0 + t)
    dg = 0.5 * (1.0 + t) + 0.5 * x * (1.0 - t * t) * _GELU_C * (1.0 + 3.0 * _GELU_A * x * x)
    return g, dg


def _sigmoid(x):
    return 1.0 / (1.0 + jnp.exp(-x))


def _dot(a, b):
    return jnp.dot(a, b, preferred_element_type=F32)


def _dot_nt(a, b):
    return lax.dot_general(a, b, (((1,), (1,)), ((), ())), preferred_element_type=F32)


def _dot_tn(a, b):
    return lax.dot_general(a, b, (((0,), (0,)), ((), ())), preferred_element_type=F32)


def _split3(x):
    hi = x.astype(BF16)
    r1 = x - hi.astype(F32)
    mid = r1.astype(BF16)
    lo = (r1 - mid.astype(F32)).astype(BF16)
    return hi, mid, lo


def _dot_f32_rhs_exact(x, m_bf16):
    hi, mid, lo = _split3(x)
    return _dot(hi, m_bf16) + _dot(mid, m_bf16) + _dot(lo, m_bf16)


def _dot_f32_lhs_exact(m_bf16, x):
    hi, mid, lo = _split3(x)
    return _dot(m_bf16, hi) + _dot(m_bf16, mid) + _dot(m_bf16, lo)


def _head_masks(width=W_GRP):
    lane = lax.broadcasted_iota(jnp.int32, (1, width), 1)
    return [(lane >= h * HEAD_DIM) & (lane < (h + 1) * HEAD_DIM) for h in range(N_HEADS)]


def _block_mask(n=W_GRP):
    r = lax.broadcasted_iota(jnp.int32, (n, n), 0)
    c = lax.broadcasted_iota(jnp.int32, (n, n), 1)
    m = None
    for h in range(N_HEADS):
        mh = (r >= h * HEAD_DIM) & (r < (h + 1) * HEAD_DIM) & (c >= h * HEAD_DIM) & (c < (h + 1) * HEAD_DIM)
        m = mh if m is None else (m | mh)
    return m


def _mm(a, b, mode, *, out_dtype, name, res=None, b_slabs=False, n=None, b_noff=0, b_koff=0,
        out_slabs=0, out_buf=None, out_n=None, o_noff=0, after=(), norm_g=None, rms_bwd=None, tm_max=None):
    after = () if after is None else (tuple(after) if isinstance(after, (tuple, list)) else (after,))
    if mode == "tn":
        k_dim, m_dim = a.shape
    else:
        m_dim, k_dim = a.shape
    if mode == "nt":
        n_dim = b.shape[-2]
    else:
        n_dim = n if n is not None else (b.shape[0] * b.shape[2] if b_slabs else b.shape[1])
    n_total = out_n if out_n is not None else n_dim
    tm, tn, tk = _pick(m_dim), _pick(n_dim), _pick(k_dim)
    if tm_max is not None:
        tm = _pick(m_dim, tuple(p for p in TILE_PREFS if p <= tm_max))
    if b_slabs and mode == "nt":
        tk = _pick(b.shape[2])
    elif b_slabs:
        tn = _pick(b.shape[2])
    elif out_slabs:
        tn = _pick(n_total // out_slabs)
    nk = k_dim // tk
    assert b_noff % tn == 0 and b_koff % tk == 0 and o_noff % tn == 0 and n_dim % tn == 0 and k_dim % tk == 0
    bn0, bk0, on0 = b_noff // tn, b_koff // tk, o_noff // tn
    dims = {"nn": (((1,), (0,)), ((), ())), "nt": (((1,), (1,)), ((), ())), "tn": (((0,), (0,)), ((), ()))}[mode]

    if mode == "tn":
        a_spec = pl.BlockSpec((tk, tm), lambda i, j, k: (k, i))
    else:
        a_spec = pl.BlockSpec((tm, tk), lambda i, j, k: (i, k))
    if not b_slabs:
        if mode == "nt":
            b_spec = pl.BlockSpec((tn, tk), lambda i, j, k: (j + bn0, k + bk0))
        else:
            b_spec = pl.BlockSpec((tk, tn), lambda i, j, k: (k + bk0, j + bn0))
    elif mode == "nt":
        bper = b.shape[2] // tk
        b_spec = pl.BlockSpec((None, tn, tk), lambda i, j, k: ((k + bk0) // bper, j, (k + bk0) % bper))
    else:
        bper = b.shape[2] // tn
        b_spec = pl.BlockSpec((None, tk, tn), lambda i, j, k: ((j + bn0) // bper, k, (j + bn0) % bper))
    in_specs = [a_spec, b_spec]
    args = [a, b]
    if res is not None:
        in_specs.append(pl.BlockSpec((tm, tn), lambda i, j, k: (i, j)))
        args.append(res)
    if out_slabs:
        oper = n_total // out_slabs // tn
        out_shape = _sds((out_slabs, m_dim, n_total // out_slabs), out_dtype)
        out_spec = pl.BlockSpec((None, tm, tn), lambda i, j, k: ((j + on0) // oper, i, (j + on0) % oper))
    else:
        out_shape = _sds((m_dim, n_total), out_dtype)
        out_spec = pl.BlockSpec((tm, tn), lambda i, j, k: (i, j + on0))
    out_specs, out_shapes = [out_spec], [out_shape]
    row_spec = pl.BlockSpec((tm, tn), lambda i, j, k: (i, 0))
    vec_spec = pl.BlockSpec((1, tn), lambda i, j, k: (0, 0))
    norm_at = rms_at = None
    if norm_g is not None:
        assert tn == n_dim and not out_slabs
        norm_at = len(args)
        in_specs.append(vec_spec)
        args.append(norm_g.reshape(1, n_dim))
        out_specs.append(row_spec)
        out_shapes.append(_sds((m_dim, n_dim), BF16))
    if rms_bwd is not None:
        assert tn == n_dim and not out_slabs and norm_g is None
        x_in, gain, dres = rms_bwd
        rms_at = len(args)
        in_specs += [row_spec, vec_spec, row_spec]
        args += [x_in, gain.reshape(1, n_dim), dres]
        out_specs, out_shapes = [row_spec, vec_spec], [_sds((m_dim, n_dim), F32), _sds((1, n_dim), F32)]
    aliases = {}
    if out_buf is not None:
        in_specs.append(pl.BlockSpec(memory_space=pl.ANY))
        args.append(out_buf)
        aliases = {len(args) - 1: 0}
    for dep in after:
        in_specs.append(pl.BlockSpec(memory_space=pl.ANY))
        args.append(dep)
    has_res = res is not None
    n_in = len(args)

    def body(*refs):
        a_ref, b_ref = refs[0], refs[1]
        res_ref = refs[2] if has_res else None
        o_ref = refs[n_in]
        acc_ref = refs[-1] if nk > 1 else None
        part = lax.dot_general(a_ref[...].astype(BF16), b_ref[...].astype(BF16), dims, preferred_element_type=F32)

        def finish(v):
            if has_res:
                v = v + res_ref[...]
            if rms_at is not None:
                xv, gv = refs[rms_at][...], refs[rms_at + 1][...]
                r = lax.rsqrt(jnp.mean(xv * xv, axis=-1, keepdims=True) + EPS)
                dyg = v * gv
                dot = jnp.mean(dyg * xv, axis=-1, keepdims=True)
                o_ref[...] = refs[rms_at + 2][...] + r * dyg - xv * (r * r * r) * dot
                dg_ref = refs[n_in + 1]
                gpart = jnp.sum(v * xv * r, axis=0, keepdims=True)
                first = pl.program_id(0) == 0

                @pl.when(first)
                def _():
                    dg_ref[...] = gpart

                @pl.when(jnp.logical_not(first))
                def _():
                    dg_ref[...] += gpart

                return
            o_ref[...] = v.astype(o_ref.dtype)
            if norm_at is not None:
                r = lax.rsqrt(jnp.mean(v * v, axis=-1, keepdims=True) + EPS)
                refs[n_in + 1][...] = (v * r * refs[norm_at][...]).astype(BF16)

        if nk == 1:
            finish(part)
        else:
            kk = pl.program_id(2)

            @pl.when(kk == 0)
            def _():
                acc_ref[...] = part

            @pl.when(kk > 0)
            def _():
                acc_ref[...] += part

            @pl.when(kk == nk - 1)
            def _():
                finish(acc_ref[...])

    outs = pl.pallas_call(
        body, grid=(m_dim // tm, n_dim // tn, nk), in_specs=in_specs, out_specs=out_specs, out_shape=out_shapes,
        scratch_shapes=[pltpu.VMEM((tm, tn), F32)] if nk > 1 else [],
        input_output_aliases=aliases, name=name,
        compiler_params=_cp(*(("arbitrary",) * 3 if rms_bwd is not None else ("parallel", "parallel", "arbitrary"))),
    )(*args)
    return outs[0] if len(outs) == 1 else tuple(outs)


def _rms_fwd(x, g, name):
    t, d = x.shape
    tm = _pick(t, ROW_TILE_PREFS)

    def body(x_ref, g_ref, o_ref):
        xv = x_ref[...]
        r = lax.rsqrt(jnp.mean(xv * xv, axis=-1, keepdims=True) + EPS)
        o_ref[...] = (xv * r * g_ref[...]).astype(o_ref.dtype)

    return pl.pallas_call(
        body, grid=(t // tm,),
        in_specs=[pl.BlockSpec((tm, d), lambda i: (i, 0)), pl.BlockSpec((1, d), lambda i: (0, 0))],
        out_specs=pl.BlockSpec((tm, d), lambda i: (i, 0)), out_shape=_sds((t, d), BF16),
        name=name, compiler_params=_cp("parallel"),
    )(x, g.reshape(1, d))


def _final_loss(x, g, target):
    t, d = x.shape
    tm = _pick(t, ROW_TILE_PREFS)

    def body(x_ref, g_ref, t_ref, dx_ref, dg_ref, loss_ref):
        i = pl.program_id(0)
        xv = x_ref[...]
        gv = g_ref[...]
        r = lax.rsqrt(jnp.mean(xv * xv, axis=-1, keepdims=True) + EPS)
        err = xv * r * gv - t_ref[...]
        lpart = (0.5 / d) * jnp.sum(jnp.sum(err * err, axis=1, keepdims=True), axis=0, keepdims=True)
        dy = err * (1.0 / d)
        dyg = dy * gv
        dot = jnp.mean(dyg * xv, axis=-1, keepdims=True)
        dx_ref[...] = r * dyg - xv * (r * r * r) * dot
        part = jnp.sum(dy * xv * r, axis=0, keepdims=True)

        @pl.when(i == 0)
        def _():
            dg_ref[...] = part
            loss_ref[...] = lpart

        @pl.when(i > 0)
        def _():
            dg_ref[...] += part
            loss_ref[...] += lpart

    row = pl.BlockSpec((tm, d), lambda i: (i, 0))
    vec = pl.BlockSpec((1, d), lambda i: (0, 0))
    return pl.pallas_call(
        body, grid=(t // tm,), in_specs=[row, vec, row], out_specs=[row, vec, pl.BlockSpec((1, 1), lambda i: (0, 0))],
        out_shape=[_sds((t, d), F32), _sds((1, d), F32), _sds((1, 1), F32)], name="final_loss",
        compiler_params=_cp("arbitrary"),
    )(x, g.reshape(1, d), target)


def _shift_down(ext, k, halo):
    return pltpu.roll(ext, k, 0)[halo:]


def _shift_up(ext, k, tm):
    return pltpu.roll(ext, ext.shape[0] - k, 0)[:tm]


def _ffn_tiles(t, f):
    return _pick(t, (256, 128, 64, 32, 16, 8)), _pick(f, (1408, 256, 128))


def _ffn_act_fwd(hf_g, hf_v, conv_w, conv_b):
    t, f = hf_g.shape
    tm, cn = _ffn_tiles(t, f)
    nf = f // cn

    def body(g_ref, v_ref, wg_ref, wv_ref, bg_ref, bv_ref, o_ref, gc_ref, vc_ref, ext_ref, hg_ref, hv_ref):
        i = pl.program_id(1)

        @pl.when(i == 0)
        def _():
            hg_ref[...] = jnp.zeros_like(hg_ref)
            hv_ref[...] = jnp.zeros_like(hv_ref)

        def conv(x_ref, halo_ref, w_ref, b_ref):
            ext_ref[0:8, :] = halo_ref[...]
            ext_ref[8:, :] = x_ref[...]
            halo_ref[...] = x_ref[tm - 8:tm, :]
            ext = ext_ref[...]
            w = w_ref[...]
            return b_ref[...] + w[2:3, :] * ext[8:] + w[1:2, :] * _shift_down(ext, 1, 8) + w[0:1, :] * _shift_down(ext, 2, 8)

        gc = conv(g_ref, hg_ref, wg_ref, bg_ref)
        vc = conv(v_ref, hv_ref, wv_ref, bv_ref)
        o_ref[...] = (_gelu(gc) * vc).astype(o_ref.dtype)
        gc_ref[...] = gc.astype(gc_ref.dtype)
        vc_ref[...] = vc.astype(vc_ref.dtype)

    blk = pl.BlockSpec((tm, cn), lambda j, i: (i, j))
    return pl.pallas_call(
        body, grid=(nf, t // tm),
        in_specs=[blk, blk, pl.BlockSpec((3, cn), lambda j, i: (0, j)), pl.BlockSpec((3, cn), lambda j, i: (0, j + nf)),
                  pl.BlockSpec((1, cn), lambda j, i: (0, j)), pl.BlockSpec((1, cn), lambda j, i: (0, j + nf))],
        out_specs=[blk, blk, blk], out_shape=[_sds((t, f), BF16)] * 3,
        scratch_shapes=[pltpu.VMEM((tm + 8, cn), F32), pltpu.VMEM((8, cn), F32), pltpu.VMEM((8, cn), F32)],
        name="ffn_act_fwd", compiler_params=_cp("parallel", "arbitrary"),
    )(hf_g, hf_v, conv_w, conv_w, conv_b, conv_b)


def _ffn_bwd(dx2, w_down, gc, vc, hf_g, hf_v, conv_w):
    t, f = hf_g.shape
    d = dx2.shape[1]
    tm, cn = _ffn_tiles(t, f)
    nf, nt = f // cn, t // tm

    def body(dx_ref, wd_ref, gc_ref, vc_ref, g_ref, v_ref, wg_ref, wv_ref, dg_ref, dv_ref, sg_ref, sv_ref,
             ext_ref, cg_ref, cv_ref):
        @pl.when(pl.program_id(1) == 0)
        def _():
            for ref in (cg_ref, cv_ref, sg_ref, sv_ref):
                ref[...] = jnp.zeros_like(ref)

        da = _dot_nt(dx_ref[...].astype(BF16), wd_ref[...])
        gel, dgel = _gelu_and_grad(gc_ref[...].astype(F32))
        dgc = da * vc_ref[...].astype(F32) * dgel
        dvc = da * gel

        def back(dc, carry_ref, w, x, out_ref, sums_ref):
            ext_ref[0:tm, :] = dc
            ext_ref[tm:tm + 8, :] = carry_ref[...]
            carry_ref[...] = dc[0:8, :]
            ext = ext_ref[...]
            up1, up2 = _shift_up(ext, 1, tm), _shift_up(ext, 2, tm)
            out_ref[...] = (w[2:3, :] * dc + w[1:2, :] * up1 + w[0:1, :] * up2).astype(out_ref.dtype)
            sums_ref[0:1, :] += jnp.sum(up2 * x, axis=0, keepdims=True)
            sums_ref[1:2, :] += jnp.sum(up1 * x, axis=0, keepdims=True)
            sums_ref[2:3, :] += jnp.sum(dc * x, axis=0, keepdims=True)
            sums_ref[3:4, :] += jnp.sum(dc, axis=0, keepdims=True)

        back(dgc, cg_ref, wg_ref[...], g_ref[...], dg_ref, sg_ref)
        back(dvc, cv_ref, wv_ref[...], v_ref[...], dv_ref, sv_ref)

    blk = pl.BlockSpec((tm, cn), lambda j, i: (nt - 1 - i, j))
    sums = pl.BlockSpec((8, cn), lambda j, i: (0, j))
    return pl.pallas_call(
        body, grid=(nf, nt),
        in_specs=[pl.BlockSpec((tm, d), lambda j, i: (nt - 1 - i, 0)), pl.BlockSpec((cn, d), lambda j, i: (j, 0)),
                  blk, blk, blk, blk, pl.BlockSpec((3, cn), lambda j, i: (0, j)), pl.BlockSpec((3, cn), lambda j, i: (0, j + nf))],
        out_specs=[blk, blk, sums, sums],
        out_shape=[_sds((t, f), BF16), _sds((t, f), BF16), _sds((8, f), F32), _sds((8, f), F32)],
        scratch_shapes=[pltpu.VMEM((tm + 8, cn), F32), pltpu.VMEM((8, cn), F32), pltpu.VMEM((8, cn), F32)],
        name="ffn_bwd", compiler_params=_cp("parallel", "arbitrary"),
    )(dx2, w_down, gc, vc, hf_g, hf_v, conv_w, conv_w)


def _ple_fwd(x2, pe, pre, next_g=None):
    t, d = x2.shape
    tm = _pick(t, ROW_TILE_PREFS)

    def body(x_ref, pe_ref, pre_ref, *rest):
        x3 = x_ref[...] + pe_ref[...] * _sigmoid(pre_ref[...])
        if next_g is None:
            rest[0][...] = x3
        else:
            g_ref, o_ref, h_ref = rest
            o_ref[...] = x3
            r = lax.rsqrt(jnp.mean(x3 * x3, axis=-1, keepdims=True) + EPS)
            h_ref[...] = (x3 * r * g_ref[...]).astype(h_ref.dtype)

    row = pl.BlockSpec((tm, d), lambda i: (i, 0))
    if next_g is None:
        return pl.pallas_call(body, grid=(t // tm,), in_specs=[row, row, row], out_specs=row,
                              out_shape=_sds((t, d), F32), name="ple_fwd", compiler_params=_cp("parallel"))(x2, pe, pre), None
    return pl.pallas_call(body, grid=(t // tm,), in_specs=[row, row, row, pl.BlockSpec((1, d), lambda i: (0, 0))],
                          out_specs=[row, row], out_shape=[_sds((t, d), F32), _sds((t, d), BF16)], name="ple_norm_fwd",
                          compiler_params=_cp("parallel"))(x2, pe, pre, next_g.reshape(1, d))


def _ple_bwd(dx3, pe, pre, after=None):
    t, d = dx3.shape
    tm = _pick(t, ROW_TILE_PREFS)

    def body(dx_ref, pe_ref, pre_ref, *rest):
        dpe_ref, dpre_ref = rest[-2:]
        gate = _sigmoid(pre_ref[...])
        dx = dx_ref[...]
        dpe_ref[...] = (dx * gate).astype(dpe_ref.dtype)
        dpre_ref[...] = (dx * pe_ref[...] * gate * (1.0 - gate)).astype(dpre_ref.dtype)

    row = pl.BlockSpec((tm, d), lambda i: (i, 0))
    extra = [] if after is None else [after]
    return pl.pallas_call(body, grid=(t // tm,), in_specs=[row, row, row] + [pl.BlockSpec(memory_space=pl.ANY)] * len(extra),
                          out_specs=[row, row], out_shape=[_sds((t, d), BF16), _sds((t, d), BF16)], name="ple_bwd",
                          compiler_params=_cp("parallel"))(dx3, pe, pre, *extra)


def _mix_tm(t):
    return _pick(t, (512, 256, 128))


def _zblk(tm, col, rev_nt=None):
    if rev_nt is None:
        return pl.BlockSpec((tm, W_GRP), lambda i: (i, col))
    return pl.BlockSpec((tm, W_GRP), lambda i: (rev_nt - 1 - i, col))


def _full(shape):
    nd = len(shape)
    return pl.BlockSpec(tuple(shape), lambda i: (0,) * nd)


def _gmlp_sv(wm_ref, vnc, bs, hm):
    sv = bs
    for h in range(N_HEADS):
        sv = sv + jnp.where(hm[h], _dot(wm_ref[h], vnc), 0.0)
    return sv


def _layernorm(v, g, b):
    mu = jnp.mean(v, axis=-1, keepdims=True)
    vc = v - mu
    rs = lax.rsqrt(jnp.mean(vc * vc, axis=-1, keepdims=True) + EPS)
    xhat = vc * rs
    return xhat, rs, xhat * g + b


def _mix_a_fwd(z, d_mix, ln_g, ln_b, wm, bs_t):
    t = z.shape[0]
    tm = _mix_tm(t)

    def body(u_ref, v_ref, g_ref, b_ref, wm_ref, bs_ref, o_ref):
        hm = _head_masks()
        ug = _gelu(u_ref[...])
        _, _, vn = _layernorm(_gelu(v_ref[...]), g_ref[...], b_ref[...])
        vnb = vn.astype(BF16)
        for n in range(tm // GMLP_CHUNK):
            sl = slice(n * GMLP_CHUNK, (n + 1) * GMLP_CHUNK)
            o_ref[sl, :] = ug[sl] * _gmlp_sv(wm_ref, vnb[sl], bs_ref[...], hm)

    return pl.pallas_call(
        body, grid=(t // tm,),
        in_specs=[_zblk(tm, 0), _zblk(tm, 1), _full((1, W_GRP)), _full((1, W_GRP)), _full(wm.shape), _full(bs_t.shape)],
        out_specs=_zblk(tm, 0), out_shape=_sds((t, d_mix), F32), name="mix_a_fwd", compiler_params=_cp("parallel"),
    )(z, z, ln_g, ln_b, wm, bs_t)


def _mix_a_bwd(z, dmix, ln_g, ln_b, wm, wm_t, bs_t):
    t, zc = z.shape
    tm = _mix_tm(t)

    def body(u_ref, v_ref, dy_ref, g_ref, b_ref, wm_ref, wmt_ref, bs_ref, dz_ref, dg_ref, db_ref, dws_ref, dbs_ref):
        i = pl.program_id(0)

        @pl.when(i == 0)
        def _():
            dg_ref[...] = jnp.zeros_like(dg_ref)
            db_ref[...] = jnp.zeros_like(db_ref)
            dws_ref[...] = jnp.zeros_like(dws_ref)
            dbs_ref[...] = jnp.zeros_like(dbs_ref)

        hm = _head_masks()
        ug, dug = _gelu_and_grad(u_ref[...])
        vg, dvg = _gelu_and_grad(v_ref[...])
        gv = g_ref[...]
        xhat, rs, vn = _layernorm(vg, gv, b_ref[...])
        vnb = vn.astype(BF16)
        dy = dy_ref[...]
        for n in range(tm // GMLP_CHUNK):
            sl = slice(n * GMLP_CHUNK, (n + 1) * GMLP_CHUNK)
            vnc = vnb[sl]
            sv = _gmlp_sv(wm_ref, vnc, bs_ref[...], hm)
            dsv = dy[sl] * ug[sl]
            dz_ref[sl, 0:W_GRP] = dy[sl] * sv * dug[sl]
            dbs_ref[...] += dsv
            dsvb = dsv.astype(BF16)
            dvn = jnp.zeros((GMLP_CHUNK, W_GRP), F32)
            for h in range(N_HEADS):
                dws_ref[h] += _dot_nt(jnp.where(hm[h], dsv, 0.0).astype(BF16), vnc)
                dvn = dvn + jnp.where(hm[h], _dot(wmt_ref[h], dsvb), 0.0)
            xh = xhat[sl]
            dg_ref[...] += jnp.sum(dvn * xh, axis=0, keepdims=True)
            db_ref[...] += jnp.sum(dvn, axis=0, keepdims=True)
            dxh = dvn * gv
            dvg_c = rs[sl] * (dxh - jnp.mean(dxh, axis=-1, keepdims=True) - xh * jnp.mean(dxh * xh, axis=-1, keepdims=True))
            dz_ref[sl, W_GRP:2 * W_GRP] = dvg_c * dvg[sl]

    return pl.pallas_call(
        body, grid=(t // tm,),
        in_specs=[_zblk(tm, 0), _zblk(tm, 1), _zblk(tm, 0), _full((1, W_GRP)), _full((1, W_GRP)), _full(wm.shape),
                  _full(wm_t.shape), _full(bs_t.shape)],
        out_specs=[pl.BlockSpec((tm, 2 * W_GRP), lambda i: (i, 0)), _full((1, W_GRP)), _full((1, W_GRP)),
                   _full(wm.shape), _full(bs_t.shape)],
        out_shape=[_sds((t, zc), F32), _sds((1, W_GRP), F32), _sds((1, W_GRP), F32), _sds(wm.shape, F32),
                   _sds(bs_t.shape, F32)],
        name="mix_a_bwd", compiler_params=_cp("arbitrary"),
    )(z, z, dmix, ln_g, ln_b, wm, wm_t, bs_t)


def _softplus(x):
    return jnp.maximum(x, 0.0) + jnp.log(1.0 + jnp.exp(-jnp.abs(x)))


def _neg_expm1(x):
    series = -x * (1.0 + x * 0.5 * (1.0 + x * (1.0 / 3.0) * (1.0 + x * 0.25 * (1.0 + x * 0.2))))
    return jnp.where(x > -0.1, series, 1.0 - jnp.exp(x))


def _rglru_gates(ext_ref, x_ref, halo, cw, cb, wa_ref, wx_ref, ba, bx, lam):
    ext_ref[0:8, :] = halo
    ext_ref[8:, :] = x_ref[...]
    ext = ext_ref[...]
    x0, x1, x2, x3 = ext[8:], _shift_down(ext, 1, 8), _shift_down(ext, 2, 8), _shift_down(ext, 3, 8)
    xc = cb + cw[3:4, :] * x0 + cw[2:3, :] * x1 + cw[1:2, :] * x2 + cw[0:1, :] * x3
    xcb = xc.astype(BF16)
    r = _sigmoid(_dot(xcb, wa_ref[...]) + ba)
    ig = _sigmoid(_dot(xcb, wx_ref[...]) + bx)
    sp = _softplus(-lam)
    la = -RGLRU_C * r * sp
    a = jnp.exp(la)
    mult = jnp.sqrt(_neg_expm1(2.0 * la))
    return (x0, x1, x2, x3), xc, r, ig, sp, a, mult


def _mix_b_fwd(z, mix, conv_w, conv_b, wa, wx, ba, bx, lam):
    t = z.shape[0]
    tm = _mix_tm(t)

    def body(x_ref, gb_ref, cw_ref, cb_ref, wa_ref, wx_ref, ba_ref, bx_ref, lam_ref, mix_in, o_ref, hs_ref,
             ext_ref, a_ref, b_ref, xh_ref, hc_ref):
        i = pl.program_id(0)

        @pl.when(i == 0)
        def _():
            xh_ref[...] = jnp.zeros_like(xh_ref)
            hc_ref[...] = jnp.zeros_like(hc_ref)

        _, xc, _, ig, _, a, mult = _rglru_gates(ext_ref, x_ref, xh_ref[...], cw_ref[...], cb_ref[...], wa_ref, wx_ref,
                                                ba_ref[...], bx_ref[...], lam_ref[...])
        xh_ref[...] = x_ref[tm - 8:tm, :]
        a_ref[...] = a
        b_ref[...] = mult * (ig * xc)
        rid = lax.broadcasted_iota(jnp.int32, (8, W_GRP), 0)

        def group(gi, hprev):
            base = pl.multiple_of(gi * 8, 8)
            ca = a_ref[pl.ds(base, 8), :]
            cb = b_ref[pl.ds(base, 8), :]
            for k in (1, 2, 4):
                m = rid >= k
                cb = jnp.where(m, ca * pltpu.roll(cb, k, 0) + cb, cb)
                ca = jnp.where(m, ca * pltpu.roll(ca, k, 0), ca)
            hh = cb + ca * hprev
            hs_ref[pl.ds(base, 8), :] = hh
            return hh[7:8, :]

        hlast = lax.fori_loop(0, tm // 8, group, hc_ref[0:1, :])
        hc_ref[...] = jnp.broadcast_to(hlast, hc_ref.shape)
        o_ref[...] = hs_ref[...] * _gelu(gb_ref[...])

    sq = _full((W_GRP, W_GRP))
    vec = _full((1, W_GRP))
    return pl.pallas_call(
        body, grid=(t // tm,),
        in_specs=[_zblk(tm, 2), _zblk(tm, 3), _full((4, W_GRP)), vec, sq, sq, vec, vec, vec, pl.BlockSpec(memory_space=pl.ANY)],
        out_specs=[_zblk(tm, 1), pl.BlockSpec((tm, W_GRP), lambda i: (i, 0))],
        out_shape=[_sds(mix.shape, F32), _sds((t, W_GRP), F32)],
        scratch_shapes=[pltpu.VMEM((tm + 8, W_GRP), F32), pltpu.VMEM((tm, W_GRP), F32), pltpu.VMEM((tm, W_GRP), F32),
                        pltpu.VMEM((8, W_GRP), F32), pltpu.VMEM((8, W_GRP), F32)],
        input_output_aliases={9: 0}, name="mix_b_fwd", compiler_params=_cp("arbitrary"),
    )(z, z, conv_w, conv_b, wa, wx, ba, bx, lam, mix)


def _mix_b_bwd(z, dz, dmix, hs, conv_w, conv_b, wa, wx, wa_t, wx_t, ba, bx, lam):
    t = z.shape[0]
    tm = _mix_tm(t)
    nt = t // tm
    hb = tm // 8

    def body(x_ref, gb_ref, xhalo_ref, hs_ref, hhalo_ref, dy_ref, cw_ref, cb_ref, wa_ref, wx_ref, wat_ref, wxt_ref,
             ba_ref, bx_ref, lam_ref, dz_in, dz_ref, dcw_ref, dcb_ref, dwa_ref, dwx_ref, dba_ref, dbx_ref, dlam_ref,
             ext_ref, c_ref, d_ref, g_ref, an_ref, gn_ref, dxn_ref):
        i = pl.program_id(0)
        first_tile = i == nt - 1

        @pl.when(i == 0)
        def _():
            for ref in (dcw_ref, dcb_ref, dwa_ref, dwx_ref, dba_ref, dbx_ref, dlam_ref, an_ref, gn_ref, dxn_ref):
                ref[...] = jnp.zeros_like(ref)

        cw, lam = cw_ref[...], lam_ref[...]
        xhalo = jnp.where(first_tile, 0.0, xhalo_ref[...])
        (x0, x1, x2, x3), xc, r, ig, sp, a, mult = _rglru_gates(
            ext_ref, x_ref, xhalo, cw, cb_ref[...], wa_ref, wx_ref, ba_ref[...], bx_ref[...], lam)
        hs = hs_ref[...]
        dy = dy_ref[...]
        gel, dgel = _gelu_and_grad(gb_ref[...])
        dz_ref[:, W_GRP:2 * W_GRP] = dy * hs * dgel

        ext_ref[0:tm, :] = a
        ext_ref[tm:tm + 8, :] = an_ref[...]
        an_ref[...] = a[0:8, :]
        c_ref[...] = _shift_up(ext_ref[...], 1, tm)
        d_ref[...] = dy * gel
        rid = lax.broadcasted_iota(jnp.int32, (8, W_GRP), 0)

        def group(j, gnext):
            base = pl.multiple_of((tm // 8 - 1 - j) * 8, 8)
            cc = c_ref[pl.ds(base, 8), :]
            cd = d_ref[pl.ds(base, 8), :]
            for k in (1, 2, 4):
                m = rid < 8 - k
                cd = jnp.where(m, cc * pltpu.roll(cd, 8 - k, 0) + cd, cd)
                cc = jnp.where(m, cc * pltpu.roll(cc, 8 - k, 0), cc)
            gg = cd + cc * gnext
            g_ref[pl.ds(base, 8), :] = gg
            return gg[0:1, :]

        gfirst = lax.fori_loop(0, tm // 8, group, gn_ref[0:1, :])
        gn_ref[...] = jnp.broadcast_to(gfirst, gn_ref.shape)
        g = g_ref[...]

        ext_ref[0:8, :] = jnp.where(first_tile, 0.0, hhalo_ref[...])
        ext_ref[8:, :] = hs
        hprev = _shift_down(ext_ref[...], 1, 8)
        da = g * hprev
        dmult = g * (ig * xc)
        di = g * mult * xc
        dxc = g * mult * ig
        dla = da * a - dmult * a * a / mult
        dr = dla * (-RGLRU_C * sp)
        dlam_ref[...] += jnp.sum(dla * (-RGLRU_C * r), axis=0, keepdims=True) * (-_sigmoid(-lam))
        dpr = dr * r * (1.0 - r)
        dpi = di * ig * (1.0 - ig)
        dprb, dpib, xcb = dpr.astype(BF16), dpi.astype(BF16), xc.astype(BF16)
        dba_ref[...] += jnp.sum(dpr, axis=0, keepdims=True)
        dbx_ref[...] += jnp.sum(dpi, axis=0, keepdims=True)
        dwa_ref[...] += _dot_tn(xcb, dprb)
        dwx_ref[...] += _dot_tn(xcb, dpib)
        dxc = dxc + _dot(dprb, wat_ref[...]) + _dot(dpib, wxt_ref[...])
        dcb_ref[...] += jnp.sum(dxc, axis=0, keepdims=True)
        dcw_ref[3:4, :] += jnp.sum(dxc * x0, axis=0, keepdims=True)
        dcw_ref[2:3, :] += jnp.sum(dxc * x1, axis=0, keepdims=True)
        dcw_ref[1:2, :] += jnp.sum(dxc * x2, axis=0, keepdims=True)
        dcw_ref[0:1, :] += jnp.sum(dxc * x3, axis=0, keepdims=True)
        ext_ref[0:tm, :] = dxc
        ext_ref[tm:tm + 8, :] = dxn_ref[...]
        dxn_ref[...] = dxc[0:8, :]
        ext = ext_ref[...]
        dz_ref[:, 0:W_GRP] = (cw[3:4, :] * dxc + cw[2:3, :] * _shift_up(ext, 1, tm) + cw[1:2, :] * _shift_up(ext, 2, tm)
                              + cw[0:1, :] * _shift_up(ext, 3, tm))

    sq = _full((W_GRP, W_GRP))
    vec = _full((1, W_GRP))
    halo = lambda col: pl.BlockSpec((8, W_GRP), lambda i: (jnp.maximum((nt - 1 - i) * hb - 1, 0), col))
    rev = lambda col: _zblk(tm, col, nt)
    return pl.pallas_call(
        body, grid=(nt,),
        in_specs=[rev(2), rev(3), halo(2), rev(0), halo(0), rev(1), _full((4, W_GRP)), vec, sq, sq, sq, sq, vec, vec, vec,
                  pl.BlockSpec(memory_space=pl.ANY)],
        out_specs=[pl.BlockSpec((tm, 2 * W_GRP), lambda i: (nt - 1 - i, 1)), _full((4, W_GRP)), vec, sq, sq, vec, vec, vec],
        out_shape=[_sds(dz.shape, F32), _sds((4, W_GRP), F32), _sds((1, W_GRP), F32), _sds((W_GRP, W_GRP), F32),
                   _sds((W_GRP, W_GRP), F32), _sds((1, W_GRP), F32), _sds((1, W_GRP), F32), _sds((1, W_GRP), F32)],
        scratch_shapes=[pltpu.VMEM((tm + 8, W_GRP), F32), pltpu.VMEM((tm, W_GRP), F32), pltpu.VMEM((tm, W_GRP), F32),
                        pltpu.VMEM((tm, W_GRP), F32), pltpu.VMEM((8, W_GRP), F32), pltpu.VMEM((8, W_GRP), F32),
                        pltpu.VMEM((8, W_GRP), F32)],
        input_output_aliases={15: 0}, name="mix_b_bwd", compiler_params=_cp("arbitrary"),
    )(z, z, z, hs, hs, dmix, conv_w, conv_b, wa, wx, wa_t, wx_t, ba, bx, lam, dz)


def _tri(n, lower):
    r = lax.broadcasted_iota(jnp.int32, (n, n), 0)
    c = lax.broadcasted_iota(jnp.int32, (n, n), 1)
    return jnp.where((r >= c) if lower else (r <= c), 1.0, 0.0).astype(BF16)


def _causal_stack():
    r = lax.broadcasted_iota(jnp.int32, (N_HEADS * HGRN_CHUNK, HGRN_CHUNK), 0)
    c = lax.broadcasted_iota(jnp.int32, (N_HEADS * HGRN_CHUNK, HGRN_CHUNK), 1)
    m = None
    for h in range(N_HEADS):
        mh = (r >= h * HGRN_CHUNK) & (r < (h + 1) * HGRN_CHUNK) & (r - h * HGRN_CHUNK >= c)
        m = mh if m is None else (m | mh)
    return m


def _stack_heads(x, hm):
    return jnp.concatenate([jnp.where(hm[h], x, 0.0) for h in range(N_HEADS)], axis=0)


def _unstack_heads(xs, hm):
    out = jnp.where(hm[0], xs[0:HGRN_CHUNK], 0.0)
    for h in range(1, N_HEADS):
        out = out + jnp.where(hm[h], xs[h * HGRN_CHUNK:(h + 1) * HGRN_CHUNK], 0.0)
    return out


def _hgrn_chunk(qv, fv, lb, tril):
    sq = _sigmoid(qv)
    qq = qv * sq
    sg = _sigmoid(fv)
    fg = lb + (1.0 - lb) * sg
    kk = 1.0 - fg
    bb = _dot_f32_lhs_exact(tril, jnp.log(fg))
    b_last = bb[HGRN_CHUNK - 1:HGRN_CHUNK, :]
    b_mid = bb[HGRN_CHUNK // 2 - 1:HGRN_CHUNK // 2, :]
    eq = jnp.exp(jnp.minimum(bb - b_mid, EXP_CLAMP))
    ek = jnp.exp(jnp.minimum(b_mid - bb, EXP_CLAMP))
    eb = jnp.exp(bb)
    el = jnp.exp(b_last - bb)
    return sq, qq, sg, fg, kk, b_last, eq, ek, eb, el


def _seg_mean(x, avg):
    return _dot_f32_rhs_exact(x, avg)


def _mix_c_fwd(z, mix, lb, ng):
    t = z.shape[0]
    tm = _mix_tm(t)
    nch = tm // HGRN_CHUNK

    def body(q_ref, f_ref, i_ref, g_ref, lb_ref, ng_ref, mix_in, y_ref, o_ref, ss_ref, s_ref):
        @pl.when(pl.program_id(0) == 0)
        def _():
            s_ref[...] = jnp.zeros_like(s_ref)

        hm = _head_masks()
        bmask = _block_mask()
        causal = _causal_stack()
        tril = _tri(HGRN_CHUNK, True)
        avg = jnp.where(bmask, 1.0 / HEAD_DIM, 0.0).astype(BF16)
        lb, ng = lb_ref[...], ng_ref[...]

        def chunk(c, carry):
            rows = pl.ds(pl.multiple_of(c * HGRN_CHUNK, HGRN_CHUNK), HGRN_CHUNK)
            vv = i_ref[rows, :]
            gv = g_ref[rows, :]
            _, qq, _, _, kk, b_last, eq, ek, eb, el = _hgrn_chunk(q_ref[rows, :], f_ref[rows, :], lb, tril)
            vb = vv.astype(BF16)
            qs = _stack_heads(qq * eq, hm).astype(BF16)
            att = jnp.where(causal, _dot_nt(qs, (kk * ek).astype(BF16)), 0.0)
            o = _unstack_heads(_dot(att.astype(BF16), vb), hm)
            s0 = s_ref[...]
            ss_ref[c] = s0
            o = o + _dot_nt((qq * eb).astype(BF16), s0.astype(BF16))
            s_ref[...] = s0 * jnp.exp(b_last) + jnp.where(bmask, _dot_tn(vb, (kk * el).astype(BF16)), 0.0)
            o_ref[rows, :] = o
            rstd = lax.rsqrt(_seg_mean(o * o, avg) + EPS)
            y_ref[rows, :] = o * rstd * ng * (gv * _sigmoid(gv))
            return carry

        lax.fori_loop(0, nch, chunk, 0, unroll=HGRN_UNROLL)

    vec = _full((1, W_GRP))
    return pl.pallas_call(
        body, grid=(t // tm,),
        in_specs=[_zblk(tm, 4), _zblk(tm, 5), _zblk(tm, 6), _zblk(tm, 7), vec, vec, pl.BlockSpec(memory_space=pl.ANY)],
        out_specs=[_zblk(tm, 2), pl.BlockSpec((tm, W_GRP), lambda i: (i, 0)),
                   pl.BlockSpec((nch, W_GRP, W_GRP), lambda i: (i, 0, 0))],
        out_shape=[_sds(mix.shape, F32), _sds((t, W_GRP), F32), _sds((t // HGRN_CHUNK, W_GRP, W_GRP), F32)],
        scratch_shapes=[pltpu.VMEM((W_GRP, W_GRP), F32)],
        input_output_aliases={6: 0}, name="mix_c_fwd", compiler_params=_cp("arbitrary"),
    )(z, z, z, z, lb, ng, mix)


def _mix_c_bwd(z, dz, dmix, o_pre, states, lb, ng):
    t = z.shape[0]
    tm = _mix_tm(t)
    nt = t // tm
    nch = tm // HGRN_CHUNK

    def body(q_ref, f_ref, i_ref, g_ref, o_ref, ss_ref, dy_ref, lb_ref, ng_ref, dz_in, dz_ref, dlb_ref, dng_ref, ds_ref):
        @pl.when(pl.program_id(0) == 0)
        def _():
            ds_ref[...] = jnp.zeros_like(ds_ref)
            dlb_ref[...] = jnp.zeros_like(dlb_ref)
            dng_ref[...] = jnp.zeros_like(dng_ref)

        hm = _head_masks()
        bmask = _block_mask()
        causal = _causal_stack()
        tril = _tri(HGRN_CHUNK, True)
        triu = _tri(HGRN_CHUNK, False)
        avg = jnp.where(bmask, 1.0 / HEAD_DIM, 0.0).astype(BF16)
        lb, ng = lb_ref[...], ng_ref[...]
        last_row = lax.broadcasted_iota(jnp.int32, (HGRN_CHUNK, W_GRP), 0) == HGRN_CHUNK - 1

        def chunk(j, carry):
            c = nch - 1 - j
            rows = pl.ds(pl.multiple_of(c * HGRN_CHUNK, HGRN_CHUNK), HGRN_CHUNK)
            qv, gv, vv = q_ref[rows, :], g_ref[rows, :], i_ref[rows, :]
            sq, qq, sg, fg, kk, b_last, eq, ek, eb, el = _hgrn_chunk(qv, f_ref[rows, :], lb, tril)
            s0 = ss_ref[c]
            ds1 = ds_ref[...]
            o = o_ref[rows, :]
            dy = dy_ref[rows, :]
            rstd = lax.rsqrt(_seg_mean(o * o, avg) + EPS)
            oh = o * rstd
            sgg = _sigmoid(gv)
            dz_ref[rows, 3 * W_GRP:4 * W_GRP] = dy * oh * ng * (sgg * (1.0 + gv * (1.0 - sgg)))
            don = dy * gv * sgg
            dng_ref[...] += jnp.sum(don * oh, axis=0, keepdims=True)
            doh = don * ng
            do = rstd * (doh - oh * _seg_mean(doh * oh, avg))
            qt, kt, qh, kh = qq * eq, kk * ek, qq * eb, kk * el
            vb, dob = vv.astype(BF16), do.astype(BF16)
            ktb, khb = kt.astype(BF16), kh.astype(BF16)
            ds1b = ds1.astype(BF16)
            qs = _stack_heads(qt, hm).astype(BF16)
            dos = _stack_heads(do, hm).astype(BF16)
            att = jnp.where(causal, _dot_nt(qs, ktb), 0.0).astype(BF16)
            datt = jnp.where(causal, _dot_nt(dos, vb), 0.0).astype(BF16)
            dv = _dot_tn(att, dos) + _dot_nt(khb, ds1b)
            dqt = _unstack_heads(_dot(datt, ktb), hm)
            dkt = _dot_tn(datt, qs)
            dqh = _dot(dob, s0.astype(BF16))
            dkh = _dot(vb, ds1b)
            e_last = jnp.exp(b_last)
            ds_ref[...] = ds1 * e_last + jnp.where(bmask, _dot_tn(dob, qh.astype(BF16)), 0.0)
            dq = dqt * eq + dqh * eb
            dk = dkt * ek + dkh * el
            db = qt * dqt - kt * dkt + qh * dqh - kh * dkh
            db_last = jnp.sum(kh * dkh, axis=0, keepdims=True) + e_last * jnp.sum(ds1 * s0, axis=0, keepdims=True)
            db = db + jnp.where(last_row, db_last, 0.0)
            dlogf = _dot_f32_lhs_exact(triu, db)
            dfg = dlogf / fg - dk
            dz_ref[rows, W_GRP:2 * W_GRP] = dfg * (1.0 - lb) * sg * (1.0 - sg)
            dlb_ref[...] += jnp.sum(dfg * (1.0 - sg), axis=0, keepdims=True)
            dz_ref[rows, 0:W_GRP] = dq * (sq * (1.0 + qv * (1.0 - sq)))
            dz_ref[rows, 2 * W_GRP:3 * W_GRP] = dv
            return carry

        lax.fori_loop(0, nch, chunk, 0, unroll=HGRN_UNROLL)

    vec = _full((1, W_GRP))
    rev = lambda col: _zblk(tm, col, nt)
    return pl.pallas_call(
        body, grid=(nt,),
        in_specs=[rev(4), rev(5), rev(6), rev(7), rev(0), pl.BlockSpec((nch, W_GRP, W_GRP), lambda i: (nt - 1 - i, 0, 0)),
                  rev(2), vec, vec, pl.BlockSpec(memory_space=pl.ANY)],
        out_specs=[pl.BlockSpec((tm, 4 * W_GRP), lambda i: (nt - 1 - i, 1)), vec, vec],
        out_shape=[_sds(dz.shape, F32), _sds((1, W_GRP), F32), _sds((1, W_GRP), F32)],
        scratch_shapes=[pltpu.VMEM((W_GRP, W_GRP), F32)],
        input_output_aliases={9: 0}, name="mix_c_bwd", compiler_params=_cp("arbitrary"),
    )(z, z, z, z, o_pre, states, dmix, lb, ng, dz)


def _pool_select(hm, s2, s4, s8, s16):
    return jnp.where(hm[0], s2, jnp.where(hm[1], s4, jnp.where(hm[2], s8, s16)))


def _pool_counts(hm, row0, tm):
    pos = (row0 + 1 + lax.broadcasted_iota(jnp.int32, (tm, W_GRP), 0)).astype(F32)
    win = _pool_select(hm, 2.0, 4.0, 8.0, 16.0)
    return jnp.minimum(pos, win)


def _pooled(ext_ref, x, halo, hm, cnt):
    ext_ref[0:POOL_HALO, :] = halo
    ext_ref[POOL_HALO:, :] = x
    e = ext_ref[...]
    s2 = e + pltpu.roll(e, 1, 0)
    s4 = s2 + pltpu.roll(s2, 2, 0)
    s8 = s4 + pltpu.roll(s4, 4, 0)
    s16 = s8 + pltpu.roll(s8, 8, 0)
    return _pool_select(hm, s2, s4, s8, s16)[POOL_HALO:] / cnt - x


def _mix_d_fwd(z, mix, wd, scale):
    t = z.shape[0]
    tm = _mix_tm(t)

    def body(x_ref, wd_ref, sc_ref, mix_in, o_ref, ext_ref, halo_ref):
        i = pl.program_id(0)

        @pl.when(i == 0)
        def _():
            halo_ref[...] = jnp.zeros_like(halo_ref)

        hm = _head_masks()
        x = x_ref[...]
        pooled = _pooled(ext_ref, x, halo_ref[...], hm, _pool_counts(hm, i * tm, tm))
        halo_ref[...] = x_ref[tm - POOL_HALO:tm, :]
        o_ref[...] = _dot(pooled.astype(BF16), wd_ref[...]) * sc_ref[...]

    return pl.pallas_call(
        body, grid=(t // tm,),
        in_specs=[_zblk(tm, 8), _full((W_GRP, W_GRP)), _full((1, W_GRP)), pl.BlockSpec(memory_space=pl.ANY)],
        out_specs=_zblk(tm, 3), out_shape=_sds(mix.shape, F32),
        scratch_shapes=[pltpu.VMEM((tm + POOL_HALO, W_GRP), F32), pltpu.VMEM((POOL_HALO, W_GRP), F32)],
        input_output_aliases={3: 0}, name="mix_d_fwd", compiler_params=_cp("arbitrary"),
    )(z, wd, scale, mix)


def _mix_d_bwd(z, dz, dmix, wd, wd_t, scale):
    t = z.shape[0]
    tm = _mix_tm(t)
    nt = t // tm
    hb = tm // POOL_HALO

    def body(x_ref, xhalo_ref, dy_ref, wd_ref, wdt_ref, sc_ref, dz_in, dz_ref, dwd_ref, dsc_ref, ext_ref, en_ref):
        i = pl.program_id(0)
        ri = nt - 1 - i

        @pl.when(i == 0)
        def _():
            en_ref[...] = jnp.zeros_like(en_ref)
            dwd_ref[...] = jnp.zeros_like(dwd_ref)
            dsc_ref[...] = jnp.zeros_like(dsc_ref)

        hm = _head_masks()
        cnt = _pool_counts(hm, ri * tm, tm)
        x = x_ref[...]
        pooled = _pooled(ext_ref, x, jnp.where(ri == 0, 0.0, xhalo_ref[...]), hm, cnt)
        pb = pooled.astype(BF16)
        dy = dy_ref[...]
        dsc_ref[...] += jnp.sum(dy * _dot(pb, wd_ref[...]), axis=0, keepdims=True)
        dyw = (dy * sc_ref[...]).astype(BF16)
        dwd_ref[...] += _dot_tn(pb, dyw)
        dpool = _dot(dyw, wdt_ref[...])
        e = dpool / cnt
        ext_ref[0:tm, :] = e
        ext_ref[tm:, :] = en_ref[...]
        en_ref[...] = e[0:POOL_HALO, :]
        ee = ext_ref[...]
        n = tm + POOL_HALO
        r2 = ee + pltpu.roll(ee, n - 1, 0)
        r4 = r2 + pltpu.roll(r2, n - 2, 0)
        r8 = r4 + pltpu.roll(r4, n - 4, 0)
        r16 = r8 + pltpu.roll(r8, n - 8, 0)
        dz_ref[...] = _pool_select(hm, r2, r4, r8, r16)[:tm] - dpool

    sq = _full((W_GRP, W_GRP))
    vec = _full((1, W_GRP))
    return pl.pallas_call(
        body, grid=(nt,),
        in_specs=[_zblk(tm, 8, nt), pl.BlockSpec((POOL_HALO, W_GRP), lambda i: (jnp.maximum((nt - 1 - i) * hb - 1, 0), 8)),
                  _zblk(tm, 3, nt), sq, sq, vec, pl.BlockSpec(memory_space=pl.ANY)],
        out_specs=[_zblk(tm, 8, nt), sq, vec],
        out_shape=[_sds(dz.shape, F32), _sds((W_GRP, W_GRP), F32), _sds((1, W_GRP), F32)],
        scratch_shapes=[pltpu.VMEM((tm + POOL_HALO, W_GRP), F32), pltpu.VMEM((POOL_HALO, W_GRP), F32)],
        input_output_aliases={6: 0}, name="mix_d_bwd", compiler_params=_cp("arbitrary"),
    )(z, z, dmix, wd, wd_t, scale, dz)


def _as2d(a):
    if a.ndim == 1:
        return a.reshape(1, a.shape[0])
    return a.reshape(-1, a.shape[-1])


def _adamw(w, g, m, v, name):
    shape = w.shape
    w2, g2, m2, v2 = _as2d(w), _as2d(g), _as2d(m), _as2d(v)
    rows, cols = w2.shape
    tr = _pick(rows, (1024, 512, 256, 128, 64, 32, 16, 8))
    if tr * cols * 4 * 14 > VMEM_LIMIT_BYTES:
        tr = _pick(rows, (256, 128, 64, 32, 16, 8))

    def body(w_ref, g_ref, m_ref, v_ref, d_ref, nm_ref, nv_ref):
        gv = g_ref[...]
        mn = ADAM_B1 * m_ref[...] + (1.0 - ADAM_B1) * gv
        vn = ADAM_B2 * v_ref[...] + (1.0 - ADAM_B2) * (gv * gv)
        m_hat = mn / (1.0 - ADAM_B1 ** ADAM_STEP)
        v_hat = vn / (1.0 - ADAM_B2 ** ADAM_STEP)
        d_ref[...] = -ADAM_LR * (m_hat / (jnp.sqrt(v_hat) + ADAM_EPS) + ADAM_WD * w_ref[...])
        nm_ref[...] = mn
        nv_ref[...] = vn

    blk = pl.BlockSpec((tr, cols), lambda i: (i, 0))
    outs = pl.pallas_call(
        body, grid=(rows // tr,), in_specs=[blk] * 4, out_specs=[blk] * 3, out_shape=[_sds((rows, cols), F32)] * 3,
        name=name, compiler_params=_cp("parallel"),
    )(w2, g2, m2, v2)
    return tuple(o.reshape(shape) for o in outs)


def _adamw_layer(w, g, m, v, layer, bufs, name):
    nl, r, cs = w.shape
    tr = _pick(r, (256, 128, 64, 32, 16, 8))

    def body(w_ref, g_ref, m_ref, v_ref, *rest):
        go_ref, d_ref, nm_ref, nv_ref = rest[-4:]
        gv = g_ref[...]
        mn = ADAM_B1 * m_ref[...] + (1.0 - ADAM_B1) * gv
        vn = ADAM_B2 * v_ref[...] + (1.0 - ADAM_B2) * (gv * gv)
        m_hat = mn / (1.0 - ADAM_B1 ** ADAM_STEP)
        v_hat = vn / (1.0 - ADAM_B2 ** ADAM_STEP)
        go_ref[...] = gv
        d_ref[...] = -ADAM_LR * (m_hat / (jnp.sqrt(v_hat) + ADAM_EPS) + ADAM_WD * w_ref[...])
        nm_ref[...] = mn
        nv_ref[...] = vn

    lay = pl.BlockSpec((None, tr, cs), lambda i: (layer, i, 0))
    in_specs = [lay, pl.BlockSpec((tr, cs), lambda i: (i, 0)), lay, lay]
    args = [w, g, m, v]
    aliases = {}
    if bufs is not None:
        in_specs += [pl.BlockSpec(memory_space=pl.ANY)] * 4
        args += list(bufs)
        aliases = {4 + i: i for i in range(4)}
    return pl.pallas_call(
        body, grid=(r // tr,), in_specs=in_specs, out_specs=[lay] * 4, out_shape=[_sds((nl, r, cs), F32)] * 4,
        input_output_aliases=aliases, name=name, compiler_params=_cp("parallel"),
    )(*args)


def _slot_sum(own, slots, name):
    n_slots, rows, cols = slots.shape
    whole_fits = rows * cols * 4 * (n_slots + 2) * 2 <= VMEM_LIMIT_BYTES // 2
    tr = rows if whole_fits else _pick(rows, (512, 352, 256, 128, 64, 32, 16, 8))

    def body(*refs):
        s_ref, o_ref = refs[-2], refs[-1]
        acc = s_ref[0].astype(F32) if own is None else refs[0][...].astype(F32) + s_ref[0].astype(F32)
        for k in range(1, n_slots):
            acc = acc + s_ref[k].astype(F32)
        o_ref[...] = acc

    row = pl.BlockSpec((tr, cols), lambda i: (i, 0))
    return pl.pallas_call(
        body, grid=(rows // tr,),
        in_specs=([] if own is None else [row]) + [pl.BlockSpec((n_slots, tr, cols), lambda i: (0, i, 0))],
        out_specs=row, out_shape=_sds((rows, cols), F32), name=name, compiler_params=_cp("parallel"),
    )(*(() if own is None else (own,)), slots)


def _me():
    return lax.axis_index("x"), lax.axis_index("y"), lax.axis_index("c")


def _other_chips(x, y):
    return [(1 - x, y), (x, 1 - y), (1 - x, 1 - y)]


ANY_SPEC = pl.BlockSpec(memory_space=pl.ANY)
HBM_SPEC = pl.BlockSpec(memory_space=pltpu.HBM)
SEM_SPEC = pl.BlockSpec(memory_space=pltpu.SEMAPHORE)
SPLIT_COPY_PARAMS = pltpu.CompilerParams(has_side_effects=pltpu.SideEffectType.DATAFLOW_SIDE_EFFECTING)
N_CHIPS = 4


def _aligned(v, m):
    return v if isinstance(v, int) else pl.multiple_of(v, m)


def _in_hbm(arr):
    return pltpu.with_memory_space_constraint(arr, pltpu.HBM)


def _peer(x, y, c, k):
    fx, fy, fc = (k >> 2) & 1, (k >> 1) & 1, k & 1
    px = 1 - x if fx else x
    py = 1 - y if fy else y
    pc = 1 - c if fc else c
    return px, py, pc


def _gather_start(shards, after, name):
    n = len(shards)

    def body(*refs):
        src, land = refs[:n], refs[n:2 * n]
        send_sems, recv_sems = refs[2 * n + 1], refs[2 * n + 2]
        token = refs[-1]
        x, y, c = _me()
        for w in range(n):
            for chip in _other_chips(x, y):
                pltpu.make_async_remote_copy(
                    src_ref=src[w], dst_ref=land[w].at[2 * x + y], send_sem=send_sems.at[w], recv_sem=recv_sems.at[w],
                    device_id=(*chip, c), device_id_type=MESH_ID).start()
        token[...] = jnp.zeros_like(token)

    lands = [lax.empty((N_CHIPS,) + s.shape, s.dtype) for s in shards]
    thru = [pltpu.HBM(s.shape, s.dtype) for s in shards] + [pltpu.HBM(z.shape, z.dtype) for z in lands]
    outs = pl.pallas_call(
        body, name=name,
        out_shape=(pltpu.SemaphoreType.DMA((n,)), pltpu.SemaphoreType.DMA((n,)), *thru, _sds((8, 128), F32)),
        in_specs=[HBM_SPEC] * (2 * n) + [ANY_SPEC],
        out_specs=(SEM_SPEC, SEM_SPEC, *[HBM_SPEC] * (2 * n), pl.BlockSpec(memory_space=pltpu.VMEM)),
        input_output_aliases={i: 2 + i for i in range(2 * n)}, compiler_params=SPLIT_COPY_PARAMS,
    )(*[_in_hbm(s) for s in shards], *[_in_hbm(z) for z in lands], after)
    return (outs[0], outs[1], outs[2:2 + n], outs[2 + n:2 + 2 * n]), outs[-1]


def _gather_wait(send_sems, recv_sems, srcs, lands, after, name):
    n = len(srcs)

    def body(*refs):
        land = refs[n:2 * n]
        send_sems, recv_sems = refs[2 * n], refs[2 * n + 1]
        x, y, c = _me()
        for w in range(n):
            three = land[w].at[pl.ds(0, N_CHIPS - 1)]
            cp = pltpu.make_async_remote_copy(src_ref=three, dst_ref=three, send_sem=send_sems.at[w], recv_sem=recv_sems.at[w],
                                              device_id=(x, y, c), device_id_type=MESH_ID)
            cp.wait_send()
            cp.wait_recv()

    both = list(srcs) + list(lands)
    outs = pl.pallas_call(
        body, name=name, out_shape=tuple(pltpu.HBM(b.shape, b.dtype) for b in both),
        in_specs=[HBM_SPEC] * (2 * n) + [SEM_SPEC, SEM_SPEC, ANY_SPEC], out_specs=[HBM_SPEC] * (2 * n),
        input_output_aliases={i: i for i in range(2 * n)}, compiler_params=SPLIT_COPY_PARAMS,
    )(*both, send_sems, recv_sems, after)
    return outs[n:2 * n]


def _push_start(grads, small, name):
    n = len(grads)
    srcs = list(grads) + ([] if small is None else [small])
    ns = len(srcs)

    def body(*refs):
        src, slots = refs[:ns], refs[ns:2 * ns]
        send_sems, recv_sems = refs[2 * ns], refs[2 * ns + 1]
        token = refs[-1]
        x, y, c = _me()
        for w in range(ns):
            for k in range(1, N_DEV):
                px, py, pc = _peer(x, y, c, k)
                if w < n:
                    hr = src[w].shape[1] // 2
                    piece = src[w].at[2 * px + py, pl.ds(_aligned(pc * hr, 16), hr), :]
                    slot = slots[w].at[k - 1]
                else:
                    piece = src[w]
                    slot = slots[w].at[4 * x + 2 * y + c]
                pltpu.make_async_remote_copy(
                    src_ref=piece, dst_ref=slot, send_sem=send_sems.at[w], recv_sem=recv_sems.at[w],
                    device_id=(px, py, pc), device_id_type=MESH_ID).start()
        token[...] = jnp.zeros_like(token)

    slots = [lax.empty((N_DEV - 1, g.shape[1] // 2, g.shape[2]), g.dtype) for g in grads]
    if small is not None:
        slots.append(lax.empty((N_DEV,) + small.shape, small.dtype))
    both = srcs + slots
    outs = pl.pallas_call(
        body, name=name,
        out_shape=(pltpu.SemaphoreType.DMA((ns,)), pltpu.SemaphoreType.DMA((ns,)),
                   *[pltpu.HBM(b.shape, b.dtype) for b in both], _sds((8, 128), F32)),
        in_specs=[HBM_SPEC] * len(both),
        out_specs=(SEM_SPEC, SEM_SPEC, *[HBM_SPEC] * len(both), pl.BlockSpec(memory_space=pltpu.VMEM)),
        input_output_aliases={i: 2 + i for i in range(len(both))}, compiler_params=SPLIT_COPY_PARAMS,
    )(*[_in_hbm(b) for b in both])
    return (outs[0], outs[1], outs[2:2 + ns], outs[2 + ns:2 + 2 * ns]), outs[-1]


def _push_wait(send_sems, recv_sems, srcs, slots, after, name):
    n = len(srcs)

    def body(*refs):
        slot = refs[n:2 * n]
        send_sems, recv_sems = refs[2 * n], refs[2 * n + 1]
        x, y, c = _me()
        for w in range(n):
            seven = slot[w].at[pl.ds(0, N_DEV - 1)]
            cp = pltpu.make_async_remote_copy(src_ref=seven, dst_ref=seven, send_sem=send_sems.at[w],
                                              recv_sem=recv_sems.at[w], device_id=(x, y, c), device_id_type=MESH_ID)
            cp.wait_send()
            cp.wait_recv()

    both = list(srcs) + list(slots)
    outs = pl.pallas_call(
        body, name=name, out_shape=tuple(pltpu.HBM(b.shape, b.dtype) for b in both),
        in_specs=[HBM_SPEC] * (2 * n) + [SEM_SPEC, SEM_SPEC, ANY_SPEC], out_specs=[HBM_SPEC] * (2 * n),
        input_output_aliases={i: i for i in range(2 * n)}, compiler_params=SPLIT_COPY_PARAMS,
    )(*both, send_sems, recv_sems, after)
    return outs[:n], outs[n:]


SWAP_CHUNK_BYTES = 2 * 1024 * 1024


def _swap_chunk_rows(hr, cs):
    ch = hr
    while ch * cs * 4 > SWAP_CHUNK_BYTES and ch % 16 == 0:
        ch //= 2
    return ch


def _swap_halves(halves, name):
    n = len(halves)
    chunk = [_swap_chunk_rows(*h.shape) for h in halves]
    rounds = max(h.shape[0] // ch for h, ch in zip(halves, chunk))

    def body(*refs):
        src, dst, buf = refs[:n], refs[n:2 * n], refs[2 * n:3 * n]
        load_sems, put_sems, send_sems, recv_sems = refs[3 * n:]
        x, y, c = _me()
        sibling = (x, y, 1 - c)
        for j in range(rounds):
            live = [w for w in range(n) if j < src[w].shape[0] // chunk[w]]
            loads = [pltpu.make_async_copy(src[w].at[pl.ds(j * chunk[w], chunk[w])], buf[w], load_sems.at[w]) for w in live]
            for ld in loads:
                ld.start()
            moves = []
            for ld, w in zip(loads, live):
                ld.wait()
                rows = pl.ds(_aligned(c * src[w].shape[0] + j * chunk[w], 8), chunk[w])
                put = pltpu.make_async_copy(buf[w], dst[w].at[rows], put_sems.at[w])
                send = pltpu.make_async_remote_copy(src_ref=buf[w], dst_ref=dst[w].at[rows], send_sem=send_sems.at[w],
                                                    recv_sem=recv_sems.at[w], device_id=sibling, device_id_type=MESH_ID)
                put.start()
                send.start()
                moves.append((put, send))
            for put, send in moves:
                put.wait()
                send.wait_send()
        for w in range(n):
            hr = src[w].shape[0]
            got = dst[w].at[pl.ds(_aligned((1 - c) * hr, 8), hr)]
            pltpu.make_async_remote_copy(src_ref=got, dst_ref=got, send_sem=send_sems.at[w], recv_sem=recv_sems.at[w],
                                         device_id=sibling, device_id_type=MESH_ID).wait_recv()

    return pl.pallas_call(
        body, in_specs=[ANY_SPEC] * n, out_specs=[ANY_SPEC] * n,
        out_shape=[_sds((2 * h.shape[0], h.shape[1]), F32) for h in halves],
        scratch_shapes=[pltpu.VMEM((ch, h.shape[1]), F32) for h, ch in zip(halves, chunk)]
        + [pltpu.SemaphoreType.DMA((n,))] * 4,
        name=name,
    )(*halves)


BIG = ("w_in", "w_out", "w_up", "w_down", "w_pe", "w_pg")
ROW_SHARDED = ("w_out", "w_down", "w_pg")
SMALL = ("norm1_g", "a_ln_g", "a_ln_b", "a_ws", "a_bs", "b_conv_w", "b_conv_b", "b_wa", "b_ba", "b_wx", "b_bx", "b_lam",
         "c_lb", "c_norm_g", "d_w", "d_scale", "norm2_g", "ffn_conv_w", "ffn_conv_b", "norm3_g", "final_g")
SMALL_SHARDED = ("b_conv_w", "ffn_conv_w")
WEIGHTS = ("norm1_g", "w_in", "a_ln_g", "a_ln_b", "a_ws", "a_bs", "b_conv_w", "b_conv_b", "b_wa", "b_ba", "b_wx", "b_bx",
           "b_lam", "c_lb", "c_norm_g", "d_w", "d_scale", "w_out", "norm2_g", "w_up", "ffn_conv_w", "ffn_conv_b", "w_down",
           "norm3_g", "w_pe", "w_pg", "final_g")
ARGS = ("x", "p") + WEIGHTS + ("loss_target",) + tuple("m_" + n for n in WEIGHTS) + tuple("v_" + n for n in WEIGHTS)


def _block_diag(w):
    eye = jnp.eye(N_HEADS, dtype=w.dtype)
    return (eye[:, None, :, None] * w[:, :, None, :]).reshape(W_GRP, W_GRP)


def _diag_blocks(m):
    m4 = m.reshape(N_HEADS, HEAD_DIM, N_HEADS, HEAD_DIM)
    return jnp.stack([m4[h, :, h, :] for h in range(N_HEADS)])


def _lower_bounds(c_lb):
    lbs = jnp.cumsum(jax.nn.softmax(c_lb, axis=0), axis=0)
    return lbs - lbs[0:1]


def kernel(x, p, norm1_g, w_in, a_ln_g, a_ln_b, a_ws, a_bs, b_conv_w, b_conv_b, b_wa, b_ba, b_wx, b_bx, b_lam, c_lb, c_norm_g, d_w, d_scale, w_out, norm2_g, w_up, ffn_conv_w, ffn_conv_b, w_down, norm3_g, w_pe, w_pg, final_g, loss_target, m_norm1_g, m_w_in, m_a_ln_g, m_a_ln_b, m_a_ws, m_a_bs, m_b_conv_w, m_b_conv_b, m_b_wa, m_b_ba, m_b_wx, m_b_bx, m_b_lam, m_c_lb, m_c_norm_g, m_d_w, m_d_scale, m_w_out, m_norm2_g, m_w_up, m_ffn_conv_w, m_ffn_conv_b, m_w_down, m_norm3_g, m_w_pe, m_w_pg, m_final_g, v_norm1_g, v_w_in, v_a_ln_g, v_a_ln_b, v_a_ws, v_a_bs, v_b_conv_w, v_b_conv_b, v_b_wa, v_b_ba, v_b_wx, v_b_bx, v_b_lam, v_c_lb, v_c_norm_g, v_d_w, v_d_scale, v_w_out, v_norm2_g, v_w_up, v_ffn_conv_w, v_ffn_conv_b, v_w_down, v_norm3_g, v_w_pe, v_w_pg, v_final_g):
    return _step((x, p, norm1_g, w_in, a_ln_g, a_ln_b, a_ws, a_bs, b_conv_w, b_conv_b, b_wa, b_ba, b_wx, b_bx, b_lam, c_lb, c_norm_g, d_w, d_scale, w_out, norm2_g, w_up, ffn_conv_w, ffn_conv_b, w_down, norm3_g, w_pe, w_pg, final_g, loss_target, m_norm1_g, m_w_in, m_a_ln_g, m_a_ln_b, m_a_ws, m_a_bs, m_b_conv_w, m_b_conv_b, m_b_wa, m_b_ba, m_b_wx, m_b_bx, m_b_lam, m_c_lb, m_c_norm_g, m_d_w, m_d_scale, m_w_out, m_norm2_g, m_w_up, m_ffn_conv_w, m_ffn_conv_b, m_w_down, m_norm3_g, m_w_pe, m_w_pg, m_final_g, v_norm1_g, v_w_in, v_a_ln_g, v_a_ln_b, v_a_ws, v_a_bs, v_b_conv_w, v_b_conv_b, v_b_wa, v_b_ba, v_b_wx, v_b_bx, v_b_lam, v_c_lb, v_c_norm_g, v_d_w, v_d_scale, v_w_out, v_norm2_g, v_w_up, v_ffn_conv_w, v_ffn_conv_b, v_w_down, v_norm3_g, v_w_pe, v_w_pg, v_final_g))


SMALL_PER_LAYER = tuple(n for n in SMALL if n != "final_g")
GATHERED = BIG + SMALL_SHARDED
GATHER_FIRST = ("w_in", "b_conv_w")
GATHER_REST = tuple(n for n in GATHERED if n not in GATHER_FIRST)
PUSH_EARLY = ("w_pe", "w_pg", "w_down", "w_up")
PUSH_MID = ("w_out",)
PUSH_LATE = ("w_in",)
SMALL_MID = tuple(n for n in SMALL_PER_LAYER if n != "norm1_g")


def _cols_to_slabs(m):
    r, c4 = m.shape
    return jnp.moveaxis(m.reshape(r, N_CHIPS, c4 // N_CHIPS), 1, 0)


def _slabs_to_cols(s):
    return jnp.moveaxis(s, 0, 1).reshape(s.shape[1], -1)


def _pack_small(parts):
    flat = jnp.concatenate([p.reshape(-1) for p in parts])
    return jnp.pad(flat, (0, (-flat.shape[0]) % 1024)).reshape(-1, 128)


def _step(args):
    a = dict(zip(ARGS, args, strict=True))
    x0 = a["x"][0]
    target = a["loss_target"][0]
    nl = a["norm1_g"].shape[0]
    t, d = x0.shape
    f = a["w_down"].shape[1] * N_CHIPS
    cx, cy, cc = _me()
    my_shard = 2 * cx + cy
    shards = {n: a[n].astype(BF16) for n in BIG}
    shards.update({n: a[n] for n in SMALL_SHARDED})

    def start_gather(l, names, after, tag):
        return _gather_start([shards[n][l] for n in names], after, f"gather_start_{l}{tag}")

    def finish_gather(l, names, handle, after, tag):
        send, recv, srcs, lands = handle
        lands = _gather_wait(send, recv, srcs, lands, after, f"gather_wait_{l}{tag}")
        w = {}
        for n, land in zip(names, lands):
            full = lax.dynamic_update_slice(land, shards[n][l][None], (my_shard, 0, 0))
            if n in ROW_SHARDED:
                w[n] = full.reshape(-1, full.shape[-1])
            elif n in ("w_up", "w_pe"):
                w[n] = full
            else:
                w[n] = _slabs_to_cols(full)
        return w

    lbs, lbs_vjp = jax.vjp(_lower_bounds, a["c_lb"])
    tril = jnp.tril(jnp.ones((GMLP_CHUNK, GMLP_CHUNK), F32))

    def layer_params(l, w):
        q = {}
        q["wm"] = (a["a_ws"][l] * tril).astype(BF16)
        q["wm_t"] = jnp.swapaxes(q["wm"], 1, 2)
        q["bs_t"] = jnp.repeat(a["a_bs"][l].T, HEAD_DIM, axis=1)
        for nm in ("b_wa", "b_wx", "d_w"):
            bd = _block_diag(a[nm][l]).astype(BF16)
            q[nm], q[nm + "_t"] = bd, bd.T
        for nm in ("a_ln_g", "a_ln_b", "b_conv_b", "b_ba", "b_bx", "b_lam", "d_scale"):
            q[nm] = a[nm][l].reshape(1, W_GRP)
        q["lb"] = lbs[l].reshape(1, W_GRP)
        q["ng"] = jnp.tile(a["c_norm_g"][l], N_HEADS).reshape(1, W_GRP)
        q["ffn_conv_b"] = a["ffn_conv_b"][l].reshape(1, 2 * f)
        q.update(w)
        return q

    saved, weights, params = [], [], []
    first_handle, _ = start_gather(0, GATHER_FIRST, x0, "a")
    rest_handle, _ = start_gather(0, GATHER_REST, x0, "b")
    xl = x0
    for l in range(nl):
        if l == 0:
            w = finish_gather(0, GATHER_FIRST, first_handle, xl, "a")
        else:
            w = finish_gather(l, GATHERED, next_handle, xl, "")
        s = {"x0": xl}
        s["h1"] = _rms_fwd(xl, a["norm1_g"][0], "rms1_fwd") if l == 0 else h_next
        token = None
        if 0 < l < nl - 1:
            next_handle, token = start_gather(l + 1, GATHERED, s["h1"], "")
        s["z"] = _mm(s["h1"], w["w_in"], "nn", out_dtype=F32, name="mm_z", after=token)
        q = layer_params(l, w)
        mix = _mix_a_fwd(s["z"], d, q["a_ln_g"], q["a_ln_b"], q["wm"], q["bs_t"])
        mix, s["hs"] = _mix_b_fwd(s["z"], mix, q["b_conv_w"], q["b_conv_b"], q["b_wa"], q["b_wx"], q["b_ba"], q["b_bx"],
                                  q["b_lam"])
        mix, s["o_pre"], s["states"] = _mix_c_fwd(s["z"], mix, q["lb"], q["ng"])
        s["mix"] = _mix_d_fwd(s["z"], mix, q["d_w"], q["d_scale"])
        token = None
        if l == 0:
            w.update(finish_gather(0, GATHER_REST, rest_handle, s["mix"], "b"))
            q.update(w)
            if nl > 1:
                next_handle, token = start_gather(1, GATHERED, w["w_out"], "")
        s["x1"], s["h2"] = _mm(s["mix"], w["w_out"], "nn", res=xl, out_dtype=F32, name="mm_out", after=token,
                               norm_g=a["norm2_g"][l], tm_max=512)
        s["hf_g"] = _mm(s["h2"], w["w_up"], "nn", b_slabs=True, n=f, out_dtype=F32, name="mm_up_g")
        s["hf_v"] = _mm(s["h2"], w["w_up"], "nn", b_slabs=True, n=f, b_noff=f, out_dtype=F32, name="mm_up_v")
        s["act"], s["gc"], s["vc"] = _ffn_act_fwd(s["hf_g"], s["hf_v"], q["ffn_conv_w"], q["ffn_conv_b"])
        s["x2"], s["h3"] = _mm(s["act"], w["w_down"], "nn", res=s["x1"], out_dtype=F32, name="mm_down",
                               norm_g=a["norm3_g"][l], tm_max=512)
        s["pre"] = _mm(s["h3"], w["w_pg"], "nn", out_dtype=F32, name="mm_pg")
        s["pe"] = _mm(a["p"][l, 0], w["w_pe"], "nn", b_slabs=True, out_dtype=F32, name="mm_pe")
        xl, h_next = _ple_fwd(s["x2"], s["pe"], s["pre"], a["norm1_g"][l + 1] if l + 1 < nl else None)
        saved.append(s)
        weights.append(w)
        params.append(q)

    dx, g_final, loss = _final_loss(xl, a["final_g"], target)
    loss = lax.psum(loss[0, 0], ("x", "y", "c"))

    stacked = {n: None for n in BIG}
    small_sums = {}

    def finish_push(l, names, handle, tag, after):
        send, recv, srcs, slots = handle
        srcs, slots = _push_wait(send, recv, srcs, slots, after, f"push_wait_{l}{tag}")
        halves = []
        for n, g, sl in zip(names, srcs, slots):
            hr = g.shape[1] // 2
            own = lax.dynamic_slice(g, (my_shard, cc * hr, 0), (1, hr, g.shape[2]))[0]
            halves.append(_slot_sum(own, sl, "sum_" + n))
        for n, g in zip(names, _swap_halves(halves, "swap_halves_" + tag)):
            stacked[n] = _adamw_layer(a[n], g, a["m_" + n], a["v_" + n], l, stacked[n], "adamw_" + n)
        if len(srcs) > len(names):
            by_sender = lax.dynamic_update_slice(slots[-1], srcs[-1][None], (2 * my_shard + cc, 0, 0))
            small_sums[l, tag] = _slot_sum(None, by_sender, "sum_small_" + tag)
        return stacked[names[-1]][1]

    pending = []
    token = None
    for l in reversed(range(nl)):
        q, s, w = params[l], saved[l], weights[l]
        gs = {}
        dpe, dpre = _ple_bwd(dx, s["pe"], s["pre"], after=token)
        g_pe = _mm(a["p"][l, 0], dpe, "tn", out_dtype=BF16, name="mm_dwpe", out_slabs=N_CHIPS)
        g_pg = _mm(s["h3"], dpre, "tn", out_dtype=BF16, name="mm_dwpg")
        dx2, gs["norm3_g"] = _mm(dpre, w["w_pg"], "nt", out_dtype=F32, name="mm_dh3", tm_max=512,
                                 rms_bwd=(s["x2"], a["norm3_g"][l], dx))
        g_down = _mm(s["act"], dx2, "tn", out_dtype=BF16, name="mm_dwdown")
        dhf_g, dhf_v, sums_g, sums_v = _ffn_bwd(dx2, w["w_down"], s["gc"], s["vc"], s["hf_g"], s["hf_v"], q["ffn_conv_w"])
        gs["ffn_conv_w"] = jnp.concatenate([sums_g[0:3], sums_v[0:3]], axis=1)
        gs["ffn_conv_b"] = jnp.concatenate([sums_g[3:4], sums_v[3:4]], axis=1)
        g_up = _mm(s["h2"], dhf_g, "tn", out_dtype=BF16, name="mm_dwup_g", out_slabs=N_CHIPS, out_n=2 * f)
        g_up = _mm(s["h2"], dhf_v, "tn", out_dtype=BF16, name="mm_dwup_v", out_slabs=N_CHIPS, out_n=2 * f, o_noff=f, out_buf=g_up)
        early = {"w_pe": g_pe, "w_up": g_up, "w_pg": g_pg.reshape(N_CHIPS, -1, g_pg.shape[-1]),
                 "w_down": g_down.reshape(N_CHIPS, -1, g_down.shape[-1])}
        early_handle, token = _push_start([early[n] for n in PUSH_EARLY], None, f"push_start_{l}a")
        dh2 = _mm(dhf_g, w["w_up"], "nt", b_slabs=True, out_dtype=F32, name="mm_dh2_g", after=token)
        dx1, gs["norm2_g"] = _mm(dhf_v, w["w_up"], "nt", b_slabs=True, b_koff=f, res=dh2, out_dtype=F32, name="mm_dh2_v",
                                 tm_max=512, rms_bwd=(s["x1"], a["norm2_g"][l], dx2))
        g_out = _mm(s["mix"], dx1, "tn", out_dtype=BF16, name="mm_dwout")
        dmix = _mm(dx1, w["w_out"], "nt", out_dtype=F32, name="mm_dmix")
        dz, gs["a_ln_g"], gs["a_ln_b"], dws, dbs_t = _mix_a_bwd(s["z"], dmix, q["a_ln_g"], q["a_ln_b"], q["wm"], q["wm_t"],
                                                               q["bs_t"])
        gs["a_ws"] = dws * tril
        gs["a_bs"] = dbs_t.reshape(GMLP_CHUNK, N_HEADS, HEAD_DIM).sum(-1).T
        dz, gs["b_conv_w"], gs["b_conv_b"], dwa, dwx, gs["b_ba"], gs["b_bx"], gs["b_lam"] = _mix_b_bwd(
            s["z"], dz, dmix, s["hs"], q["b_conv_w"], q["b_conv_b"], q["b_wa"], q["b_wx"], q["b_wa_t"], q["b_wx_t"],
            q["b_ba"], q["b_bx"], q["b_lam"])
        gs["b_wa"], gs["b_wx"] = _diag_blocks(dwa), _diag_blocks(dwx)
        dz, gs["c_lb"], dng = _mix_c_bwd(s["z"], dz, dmix, s["o_pre"], s["states"], q["lb"], q["ng"])
        gs["c_norm_g"] = dng.reshape(N_HEADS, HEAD_DIM).sum(0)
        dz, dwd, gs["d_scale"] = _mix_d_bwd(s["z"], dz, dmix, q["d_w"], q["d_w_t"], q["d_scale"])
        gs["d_w"] = _diag_blocks(dwd)
        small = [gs[n] for n in SMALL_MID] + ([g_final] if l == nl - 1 else [])
        mid_handle, token = _push_start([g_out.reshape(N_CHIPS, -1, g_out.shape[-1])], _pack_small(small), f"push_start_{l}b")
        g_in = _mm(s["h1"], dz, "tn", out_dtype=BF16, name="mm_dwin")
        dx, gs["norm1_g"] = _mm(dz, w["w_in"], "nt", out_dtype=F32, name="mm_dh1", after=(token, g_in), tm_max=512,
                                rms_bwd=(s["x0"], a["norm1_g"][l], dx1))

        late_handle, token = _push_start([_cols_to_slabs(g_in)], _pack_small([gs["norm1_g"]]), f"push_start_{l}c")
        dep = token
        for push in pending:
            dep = finish_push(*push, dep)
        pending = [(l, PUSH_EARLY, early_handle, "a"), (l, PUSH_MID, mid_handle, "b"), (l, PUSH_LATE, late_handle, "c")]
    for push in pending:
        dep = finish_push(*push, dep)
    grad_x = dx[None]

    def small_shape(n):
        return a[n].shape[1:-1] + (a[n].shape[-1] * N_CHIPS,) if n in SMALL_SHARDED else a[n].shape[1:]

    per_layer = {n: [] for n in SMALL_PER_LAYER}
    for l in range(nl):
        per_layer["norm1_g"].append(small_sums[l, "c"].reshape(-1)[:d])
        vec, off = small_sums[l, "b"].reshape(-1), 0
        for n in SMALL_MID:
            shape = small_shape(n)
            size = 1
            for dim in shape:
                size *= dim
            per_layer[n].append(vec[off:off + size].reshape(shape))
            off += size
        if l == nl - 1:
            grad_final = vec[off:off + d]
    grads = {n: jnp.stack(per_layer[n]) for n in SMALL_PER_LAYER}
    grads["c_lb"] = lbs_vjp(grads["c_lb"])[0]
    grads["final_g"] = grad_final
    for n in SMALL_SHARDED:
        cs = a[n].shape[-1]
        grads[n] = lax.dynamic_slice_in_dim(grads[n], my_shard * cs, cs, axis=2)

    outs = {}
    for n in WEIGHTS:
        if n in BIG:
            outs[n] = stacked[n]
        else:
            outs[n] = (grads[n],) + _adamw(a[n], grads[n], a["m_" + n], a["v_" + n], "adamw_" + n)
    return (loss, grad_x, *[outs[n][0] for n in WEIGHTS], *[outs[n][1] for n in WEIGHTS], *[outs[n][2] for n in WEIGHTS],
            *[outs[n][3] for n in WEIGHTS])
```

```python
import functools

import jax
import jax.numpy as jnp
from jax import lax
from jax.experimental import pallas as pl
from jax.experimental.pallas import tpu as pltpu

F32 = jnp.float32
BF16 = jnp.bfloat16
EPS = 1e-6
HEAD_DIM = 64
N_HEADS = 4
W_GRP = HEAD_DIM * N_HEADS
GMLP_CHUNK = 128
HGRN_CHUNK = 64
HGRN_UNROLL = 8
RGLRU_C = 8.0
POOL_HALO = 16
EXP_CLAMP = 80.0
ADAM_LR, ADAM_B1, ADAM_B2, ADAM_EPS, ADAM_WD, ADAM_STEP = 0.001, 0.9, 0.999, 1e-08, 0.01, 10
VMEM_LIMIT_BYTES = 56 * 1024 * 1024
TILE_PREFS = (1024, 1408, 768, 512, 256, 128)
ROW_TILE_PREFS = (512, 256, 128, 64, 32, 16, 8)
MESH_ID = pl.DeviceIdType.MESH
N_DEV = 8


def _pick(n, prefs=TILE_PREFS):
    for p in prefs:
        if n % p == 0:
            return p
    return n


def _cp(*sem):
    return pltpu.CompilerParams(dimension_semantics=sem if sem else None, vmem_limit_bytes=VMEM_LIMIT_BYTES)


def _sds(shape, dtype):
    return jax.ShapeDtypeStruct(tuple(shape), dtype)


_GELU_C = 0.7978845608028654
_GELU_A = 0.044715


def _gelu(x):
    return 0.5 * x * (1.0 + jnp.tanh(_GELU_C * (x + _GELU_A * x * x * x)))


def _gelu_and_grad(x):
    t = jnp.tanh(_GELU_C * (x + _GELU_A * x * x * x))
    g = 0.5 * x * (1.0 + t)
    dg = 0.5 * (1.0 + t) + 0.5 * x * (1.0 - t * t) * _GELU_C * (1.0 + 3.0 * _GELU_A * x * x)
    return g, dg


def _sigmoid(x):
    return 1.0 / (1.0 + jnp.exp(-x))


def _dot(a, b):
    return jnp.dot(a, b, preferred_element_type=F32)


def _dot_nt(a, b):
    return lax.dot_general(a, b, (((1,), (1,)), ((), ())), preferred_element_type=F32)


def _dot_tn(a, b):
    return lax.dot_general(a, b, (((0,), (0,)), ((), ())), preferred_element_type=F32)


def _split3(x):
    hi = x.astype(BF16)
    r1 = x - hi.astype(F32)
    mid = r1.astype(BF16)
    lo = (r1 - mid.astype(F32)).astype(BF16)
    return hi, mid, lo


def _dot_f32_rhs_exact(x, m_bf16):
    hi, mid, lo = _split3(x)
    return _dot(hi, m_bf16) + _dot(mid, m_bf16) + _dot(lo, m_bf16)


def _dot_f32_lhs_exact(m_bf16, x):
    hi, mid, lo = _split3(x)
    return _dot(m_bf16, hi) + _dot(m_bf16, mid) + _dot(m_bf16, lo)


def _head_masks(width=W_GRP):
    lane = lax.broadcasted_iota(jnp.int32, (1, width), 1)
    return [(lane >= h * HEAD_DIM) & (lane < (h + 1) * HEAD_DIM) for h in range(N_HEADS)]


def _block_mask(n=W_GRP):
    r = lax.broadcasted_iota(jnp.int32, (n, n), 0)
    c = lax.broadcasted_iota(jnp.int32, (n, n), 1)
    m = None
    for h in range(N_HEADS):
        mh = (r >= h * HEAD_DIM) & (r < (h + 1) * HEAD_DIM) & (c >= h * HEAD_DIM) & (c < (h + 1) * HEAD_DIM)
        m = mh if m is None else (m | mh)
    return m


def _mm(a, b, mode, *, out_dtype, name, res=None, b_slabs=False, n=None, b_noff=0, b_koff=0,
        out_slabs=0, out_buf=None, out_n=None, o_noff=0, after=(), norm_g=None, rms_bwd=None, tm_max=None):
    after = () if after is None else (tuple(after) if isinstance(after, (tuple, list)) else (after,))
    if mode == "tn":
        k_dim, m_dim = a.shape
    else:
        m_dim, k_dim = a.shape
    if mode == "nt":
        n_dim = b.shape[-2]
    else:
        n_dim = n if n is not None else (b.shape[0] * b.shape[2] if b_slabs else b.shape[1])
    n_total = out_n if out_n is not None else n_dim
    tm, tn, tk = _pick(m_dim), _pick(n_dim), _pick(k_dim)
    if tm_max is not None:
        tm = _pick(m_dim, tuple(p for p in TILE_PREFS if p <= tm_max))
    if b_slabs and mode == "nt":
        tk = _pick(b.shape[2])
    elif b_slabs:
        tn = _pick(b.shape[2])
    elif out_slabs:
        tn = _pick(n_total // out_slabs)
    nk = k_dim // tk
    assert b_noff % tn == 0 and b_koff % tk == 0 and o_noff % tn == 0 and n_dim % tn == 0 and k_dim % tk == 0
    bn0, bk0, on0 = b_noff // tn, b_koff // tk, o_noff // tn
    dims = {"nn": (((1,), (0,)), ((), ())), "nt": (((1,), (1,)), ((), ())), "tn": (((0,), (0,)), ((), ()))}[mode]

    if mode == "tn":
        a_spec = pl.BlockSpec((tk, tm), lambda i, j, k: (k, i))
    else:
        a_spec = pl.BlockSpec((tm, tk), lambda i, j, k: (i, k))
    if not b_slabs:
        if mode == "nt":
            b_spec = pl.BlockSpec((tn, tk), lambda i, j, k: (j + bn0, k + bk0))
        else:
            b_spec = pl.BlockSpec((tk, tn), lambda i, j, k: (k + bk0, j + bn0))
    elif mode == "nt":
        bper = b.shape[2] // tk
        b_spec = pl.BlockSpec((None, tn, tk), lambda i, j, k: ((k + bk0) // bper, j, (k + bk0) % bper))
    else:
        bper = b.shape[2] // tn
        b_spec = pl.BlockSpec((None, tk, tn), lambda i, j, k: ((j + bn0) // bper, k, (j + bn0) % bper))
    in_specs = [a_spec, b_spec]
    args = [a, b]
    if res is not None:
        in_specs.append(pl.BlockSpec((tm, tn), lambda i, j, k: (i, j)))
        args.append(res)
    if out_slabs:
        oper = n_total // out_slabs // tn
        out_shape = _sds((out_slabs, m_dim, n_total // out_slabs), out_dtype)
        out_spec = pl.BlockSpec((None, tm, tn), lambda i, j, k: ((j + on0) // oper, i, (j + on0) % oper))
    else:
        out_shape = _sds((m_dim, n_total), out_dtype)
        out_spec = pl.BlockSpec((tm, tn), lambda i, j, k: (i, j + on0))
    out_specs, out_shapes = [out_spec], [out_shape]
    row_spec = pl.BlockSpec((tm, tn), lambda i, j, k: (i, 0))
    vec_spec = pl.BlockSpec((1, tn), lambda i, j, k: (0, 0))
    norm_at = rms_at = None
    if norm_g is not None:
        assert tn == n_dim and not out_slabs
        norm_at = len(args)
        in_specs.append(vec_spec)
        args.append(norm_g.reshape(1, n_dim))
        out_specs.append(row_spec)
        out_shapes.append(_sds((m_dim, n_dim), BF16))
    if rms_bwd is not None:
        assert tn == n_dim and not out_slabs and norm_g is None
        x_in, gain, dres = rms_bwd
        rms_at = len(args)
        in_specs += [row_spec, vec_spec, row_spec]
        args += [x_in, gain.reshape(1, n_dim), dres]
        out_specs, out_shapes = [row_spec, vec_spec], [_sds((m_dim, n_dim), F32), _sds((1, n_dim), F32)]
    aliases = {}
    if out_buf is not None:
        in_specs.append(pl.BlockSpec(memory_space=pl.ANY))
        args.append(out_buf)
        aliases = {len(args) - 1: 0}
    for dep in after:
        in_specs.append(pl.BlockSpec(memory_space=pl.ANY))
        args.append(dep)
    has_res = res is not None
    n_in = len(args)

    def body(*refs):
        a_ref, b_ref = refs[0], refs[1]
        res_ref = refs[2] if has_res else None
        o_ref = refs[n_in]
        acc_ref = refs[-1] if nk > 1 else None
        part = lax.dot_general(a_ref[...].astype(BF16), b_ref[...].astype(BF16), dims, preferred_element_type=F32)

        def finish(v):
            if has_res:
                v = v + res_ref[...]
            if rms_at is not None:
                xv, gv = refs[rms_at][...], refs[rms_at + 1][...]
                r = lax.rsqrt(jnp.mean(xv * xv, axis=-1, keepdims=True) + EPS)
                dyg = v * gv
                dot = jnp.mean(dyg * xv, axis=-1, keepdims=True)
                o_ref[...] = refs[rms_at + 2][...] + r * dyg - xv * (r * r * r) * dot
                dg_ref = refs[n_in + 1]
                gpart = jnp.sum(v * xv * r, axis=0, keepdims=True)
                first = pl.program_id(0) == 0

                @pl.when(first)
                def _():
                    dg_ref[...] = gpart

                @pl.when(jnp.logical_not(first))
                def _():
                    dg_ref[...] += gpart

                return
            o_ref[...] = v.astype(o_ref.dtype)
            if norm_at is not None:
                r = lax.rsqrt(jnp.mean(v * v, axis=-1, keepdims=True) + EPS)
                refs[n_in + 1][...] = (v * r * refs[norm_at][...]).astype(BF16)

        if nk == 1:
            finish(part)
        else:
            kk = pl.program_id(2)

            @pl.when(kk == 0)
            def _():
                acc_ref[...] = part

            @pl.when(kk > 0)
            def _():
                acc_ref[...] += part

            @pl.when(kk == nk - 1)
            def _():
                finish(acc_ref[...])

    outs = pl.pallas_call(
        body, grid=(m_dim // tm, n_dim // tn, nk), in_specs=in_specs, out_specs=out_specs, out_shape=out_shapes,
        scratch_shapes=[pltpu.VMEM((tm, tn), F32)] if nk > 1 else [],
        input_output_aliases=aliases, name=name,
        compiler_params=_cp(*(("arbitrary",) * 3 if rms_bwd is not None else ("parallel", "parallel", "arbitrary"))),
    )(*args)
    return outs[0] if len(outs) == 1 else tuple(outs)


def _rms_fwd(x, g, name):
    t, d = x.shape
    tm = _pick(t, ROW_TILE_PREFS)

    def body(x_ref, g_ref, o_ref):
        xv = x_ref[...]
        r = lax.rsqrt(jnp.mean(xv * xv, axis=-1, keepdims=True) + EPS)
        o_ref[...] = (xv * r * g_ref[...]).astype(o_ref.dtype)

    return pl.pallas_call(
        body, grid=(t // tm,),
        in_specs=[pl.BlockSpec((tm, d), lambda i: (i, 0)), pl.BlockSpec((1, d), lambda i: (0, 0))],
        out_specs=pl.BlockSpec((tm, d), lambda i: (i, 0)), out_shape=_sds((t, d), BF16),
        name=name, compiler_params=_cp("parallel"),
    )(x, g.reshape(1, d))


def _final_loss(x, g, target):
    t, d = x.shape
    tm = _pick(t, ROW_TILE_PREFS)

    def body(x_ref, g_ref, t_ref, dx_ref, dg_ref, loss_ref):
        i = pl.program_id(0)
        xv = x_ref[...]
        gv = g_ref[...]
        r = lax.rsqrt(jnp.mean(xv * xv, axis=-1, keepdims=True) + EPS)
        err = xv * r * gv - t_ref[...]
        lpart = (0.5 / d) * jnp.sum(jnp.sum(err * err, axis=1, keepdims=True), axis=0, keepdims=True)
        dy = err * (1.0 / d)
        dyg = dy * gv
        dot = jnp.mean(dyg * xv, axis=-1, keepdims=True)
        dx_ref[...] = r * dyg - xv * (r * r * r) * dot
        part = jnp.sum(dy * xv * r, axis=0, keepdims=True)

        @pl.when(i == 0)
        def _():
            dg_ref[...] = part
            loss_ref[...] = lpart

        @pl.when(i > 0)
        def _():
            dg_ref[...] += part
            loss_ref[...] += lpart

    row = pl.BlockSpec((tm, d), lambda i: (i, 0))
    vec = pl.BlockSpec((1, d), lambda i: (0, 0))
    return pl.pallas_call(
        body, grid=(t // tm,), in_specs=[row, vec, row], out_specs=[row, vec, pl.BlockSpec((1, 1), lambda i: (0, 0))],
        out_shape=[_sds((t, d), F32), _sds((1, d), F32), _sds((1, 1), F32)], name="final_loss",
        compiler_params=_cp("arbitrary"),
    )(x, g.reshape(1, d), target)


def _shift_down(ext, k, halo):
    return pltpu.roll(ext, k, 0)[halo:]


def _shift_up(ext, k, tm):
    return pltpu.roll(ext, ext.shape[0] - k, 0)[:tm]


def _ffn_tiles(t, f):
    return _pick(t, (256, 128, 64, 32, 16, 8)), _pick(f, (1408, 256, 128))


def _ffn_act_fwd(hf_g, hf_v, conv_w, conv_b):
    t, f = hf_g.shape
    tm, cn = _ffn_tiles(t, f)
    nf = f // cn

    def body(g_ref, v_ref, wg_ref, wv_ref, bg_ref, bv_ref, o_ref, gc_ref, vc_ref, ext_ref, hg_ref, hv_ref):
        i = pl.program_id(1)

        @pl.when(i == 0)
        def _():
            hg_ref[...] = jnp.zeros_like(hg_ref)
            hv_ref[...] = jnp.zeros_like(hv_ref)

        def conv(x_ref, halo_ref, w_ref, b_ref):
            ext_ref[0:8, :] = halo_ref[...]
            ext_ref[8:, :] = x_ref[...]
            halo_ref[...] = x_ref[tm - 8:tm, :]
            ext = ext_ref[...]
            w = w_ref[...]
            return b_ref[...] + w[2:3, :] * ext[8:] + w[1:2, :] * _shift_down(ext, 1, 8) + w[0:1, :] * _shift_down(ext, 2, 8)

        gc = conv(g_ref, hg_ref, wg_ref, bg_ref)
        vc = conv(v_ref, hv_ref, wv_ref, bv_ref)
        o_ref[...] = (_gelu(gc) * vc).astype(o_ref.dtype)
        gc_ref[...] = gc.astype(gc_ref.dtype)
        vc_ref[...] = vc.astype(vc_ref.dtype)

    blk = pl.BlockSpec((tm, cn), lambda j, i: (i, j))
    return pl.pallas_call(
        body, grid=(nf, t // tm),
        in_specs=[blk, blk, pl.BlockSpec((3, cn), lambda j, i: (0, j)), pl.BlockSpec((3, cn), lambda j, i: (0, j + nf)),
                  pl.BlockSpec((1, cn), lambda j, i: (0, j)), pl.BlockSpec((1, cn), lambda j, i: (0, j + nf))],
        out_specs=[blk, blk, blk], out_shape=[_sds((t, f), BF16)] * 3,
        scratch_shapes=[pltpu.VMEM((tm + 8, cn), F32), pltpu.VMEM((8, cn), F32), pltpu.VMEM((8, cn), F32)],
        name="ffn_act_fwd", compiler_params=_cp("parallel", "arbitrary"),
    )(hf_g, hf_v, conv_w, conv_w, conv_b, conv_b)


def _ffn_bwd(dx2, w_down, gc, vc, hf_g, hf_v, conv_w):
    t, f = hf_g.shape
    d = dx2.shape[1]
    tm, cn = _ffn_tiles(t, f)
    nf, nt = f // cn, t // tm

    def body(dx_ref, wd_ref, gc_ref, vc_ref, g_ref, v_ref, wg_ref, wv_ref, dg_ref, dv_ref, sg_ref, sv_ref,
             ext_ref, cg_ref, cv_ref):
        @pl.when(pl.program_id(1) == 0)
        def _():
            for ref in (cg_ref, cv_ref, sg_ref, sv_ref):
                ref[...] = jnp.zeros_like(ref)

        da = _dot_nt(dx_ref[...].astype(BF16), wd_ref[...])
        gel, dgel = _gelu_and_grad(gc_ref[...].astype(F32))
        dgc = da * vc_ref[...].astype(F32) * dgel
        dvc = da * gel

        def back(dc, carry_ref, w, x, out_ref, sums_ref):
            ext_ref[0:tm, :] = dc
            ext_ref[tm:tm + 8, :] = carry_ref[...]
            carry_ref[...] = dc[0:8, :]
            ext = ext_ref[...]
            up1, up2 = _shift_up(ext, 1, tm), _shift_up(ext, 2, tm)
            out_ref[...] = (w[2:3, :] * dc + w[1:2, :] * up1 + w[0:1, :] * up2).astype(out_ref.dtype)
            sums_ref[0:1, :] += jnp.sum(up2 * x, axis=0, keepdims=True)
            sums_ref[1:2, :] += jnp.sum(up1 * x, axis=0, keepdims=True)
            sums_ref[2:3, :] += jnp.sum(dc * x, axis=0, keepdims=True)
            sums_ref[3:4, :] += jnp.sum(dc, axis=0, keepdims=True)

        back(dgc, cg_ref, wg_ref[...], g_ref[...], dg_ref, sg_ref)
        back(dvc, cv_ref, wv_ref[...], v_ref[...], dv_ref, sv_ref)

    blk = pl.BlockSpec((tm, cn), lambda j, i: (nt - 1 - i, j))
    sums = pl.BlockSpec((8, cn), lambda j, i: (0, j))
    return pl.pallas_call(
        body, grid=(nf, nt),
        in_specs=[pl.BlockSpec((tm, d), lambda j, i: (nt - 1 - i, 0)), pl.BlockSpec((cn, d), lambda j, i: (j, 0)),
                  blk, blk, blk, blk, pl.BlockSpec((3, cn), lambda j, i: (0, j)), pl.BlockSpec((3, cn), lambda j, i: (0, j + nf))],
        out_specs=[blk, blk, sums, sums],
        out_shape=[_sds((t, f), BF16), _sds((t, f), BF16), _sds((8, f), F32), _sds((8, f), F32)],
        scratch_shapes=[pltpu.VMEM((tm + 8, cn), F32), pltpu.VMEM((8, cn), F32), pltpu.VMEM((8, cn), F32)],
        name="ffn_bwd", compiler_params=_cp("parallel", "arbitrary"),
    )(dx2, w_down, gc, vc, hf_g, hf_v, conv_w, conv_w)


def _ple_fwd(x2, pe, pre, next_g=None):
    t, d = x2.shape
    tm = _pick(t, ROW_TILE_PREFS)

    def body(x_ref, pe_ref, pre_ref, *rest):
        x3 = x_ref[...] + pe_ref[...] * _sigmoid(pre_ref[...])
        if next_g is None:
            rest[0][...] = x3
        else:
            g_ref, o_ref, h_ref = rest
            o_ref[...] = x3
            r = lax.rsqrt(jnp.mean(x3 * x3, axis=-1, keepdims=True) + EPS)
            h_ref[...] = (x3 * r * g_ref[...]).astype(h_ref.dtype)

    row = pl.BlockSpec((tm, d), lambda i: (i, 0))
    if next_g is None:
        return pl.pallas_call(body, grid=(t // tm,), in_specs=[row, row, row], out_specs=row,
                              out_shape=_sds((t, d), F32), name="ple_fwd", compiler_params=_cp("parallel"))(x2, pe, pre), None
    return pl.pallas_call(body, grid=(t // tm,), in_specs=[row, row, row, pl.BlockSpec((1, d), lambda i: (0, 0))],
                          out_specs=[row, row], out_shape=[_sds((t, d), F32), _sds((t, d), BF16)], name="ple_norm_fwd",
                          compiler_params=_cp("parallel"))(x2, pe, pre, next_g.reshape(1, d))


def _ple_bwd(dx3, pe, pre, after=None):
    t, d = dx3.shape
    tm = _pick(t, ROW_TILE_PREFS)

    def body(dx_ref, pe_ref, pre_ref, *rest):
        dpe_ref, dpre_ref = rest[-2:]
        gate = _sigmoid(pre_ref[...])
        dx = dx_ref[...]
        dpe_ref[...] = (dx * gate).astype(dpe_ref.dtype)
        dpre_ref[...] = (dx * pe_ref[...] * gate * (1.0 - gate)).astype(dpre_ref.dtype)

    row = pl.BlockSpec((tm, d), lambda i: (i, 0))
    extra = [] if after is None else [after]
    return pl.pallas_call(body, grid=(t // tm,), in_specs=[row, row, row] + [pl.BlockSpec(memory_space=pl.ANY)] * len(extra),
                          out_specs=[row, row], out_shape=[_sds((t, d), BF16), _sds((t, d), BF16)], name="ple_bwd",
                          compiler_params=_cp("parallel"))(dx3, pe, pre, *extra)


def _mix_tm(t):
    return _pick(t, (512, 256, 128))


def _zblk(tm, col, rev_nt=None):
    if rev_nt is None:
        return pl.BlockSpec((tm, W_GRP), lambda i: (i, col))
    return pl.BlockSpec((tm, W_GRP), lambda i: (rev_nt - 1 - i, col))


def _full(shape):
    nd = len(shape)
    return pl.BlockSpec(tuple(shape), lambda i: (0,) * nd)


def _gmlp_sv(wm_ref, vnc, bs, hm):
    sv = bs
    for h in range(N_HEADS):
        sv = sv + jnp.where(hm[h], _dot(wm_ref[h], vnc), 0.0)
    return sv


def _layernorm(v, g, b):
    mu = jnp.mean(v, axis=-1, keepdims=True)
    vc = v - mu
    rs = lax.rsqrt(jnp.mean(vc * vc, axis=-1, keepdims=True) + EPS)
    xhat = vc * rs
    return xhat, rs, xhat * g + b


def _mix_a_fwd(z, d_mix, ln_g, ln_b, wm, bs_t):
    t = z.shape[0]
    tm = _mix_tm(t)

    def body(u_ref, v_ref, g_ref, b_ref, wm_ref, bs_ref, o_ref):
        hm = _head_masks()
        ug = _gelu(u_ref[...])
        _, _, vn = _layernorm(_gelu(v_ref[...]), g_ref[...], b_ref[...])
        vnb = vn.astype(BF16)
        for n in range(tm // GMLP_CHUNK):
            sl = slice(n * GMLP_CHUNK, (n + 1) * GMLP_CHUNK)
            o_ref[sl, :] = ug[sl] * _gmlp_sv(wm_ref, vnb[sl], bs_ref[...], hm)

    return pl.pallas_call(
        body, grid=(t // tm,),
        in_specs=[_zblk(tm, 0), _zblk(tm, 1), _full((1, W_GRP)), _full((1, W_GRP)), _full(wm.shape), _full(bs_t.shape)],
        out_specs=_zblk(tm, 0), out_shape=_sds((t, d_mix), F32), name="mix_a_fwd", compiler_params=_cp("parallel"),
    )(z, z, ln_g, ln_b, wm, bs_t)


def _mix_a_bwd(z, dmix, ln_g, ln_b, wm, wm_t, bs_t):
    t, zc = z.shape
    tm = _mix_tm(t)

    def body(u_ref, v_ref, dy_ref, g_ref, b_ref, wm_ref, wmt_ref, bs_ref, dz_ref, dg_ref, db_ref, dws_ref, dbs_ref):
        i = pl.program_id(0)

        @pl.when(i == 0)
        def _():
            dg_ref[...] = jnp.zeros_like(dg_ref)
            db_ref[...] = jnp.zeros_like(db_ref)
            dws_ref[...] = jnp.zeros_like(dws_ref)
            dbs_ref[...] = jnp.zeros_like(dbs_ref)

        hm = _head_masks()
        ug, dug = _gelu_and_grad(u_ref[...])
        vg, dvg = _gelu_and_grad(v_ref[...])
        gv = g_ref[...]
        xhat, rs, vn = _layernorm(vg, gv, b_ref[...])
        vnb = vn.astype(BF16)
        dy = dy_ref[...]
        for n in range(tm // GMLP_CHUNK):
            sl = slice(n * GMLP_CHUNK, (n + 1) * GMLP_CHUNK)
            vnc = vnb[sl]
            sv = _gmlp_sv(wm_ref, vnc, bs_ref[...], hm)
            dsv = dy[sl] * ug[sl]
            dz_ref[sl, 0:W_GRP] = dy[sl] * sv * dug[sl]
            dbs_ref[...] += dsv
            dsvb = dsv.astype(BF16)
            dvn = jnp.zeros((GMLP_CHUNK, W_GRP), F32)
            for h in range(N_HEADS):
                dws_ref[h] += _dot_nt(jnp.where(hm[h], dsv, 0.0).astype(BF16), vnc)
                dvn = dvn + jnp.where(hm[h], _dot(wmt_ref[h], dsvb), 0.0)
            xh = xhat[sl]
            dg_ref[...] += jnp.sum(dvn * xh, axis=0, keepdims=True)
            db_ref[...] += jnp.sum(dvn, axis=0, keepdims=True)
            dxh = dvn * gv
            dvg_c = rs[sl] * (dxh - jnp.mean(dxh, axis=-1, keepdims=True) - xh * jnp.mean(dxh * xh, axis=-1, keepdims=True))
            dz_ref[sl, W_GRP:2 * W_GRP] = dvg_c * dvg[sl]

    return pl.pallas_call(
        body, grid=(t // tm,),
        in_specs=[_zblk(tm, 0), _zblk(tm, 1), _zblk(tm, 0), _full((1, W_GRP)), _full((1, W_GRP)), _full(wm.shape),
                  _full(wm_t.shape), _full(bs_t.shape)],
        out_specs=[pl.BlockSpec((tm, 2 * W_GRP), lambda i: (i, 0)), _full((1, W_GRP)), _full((1, W_GRP)),
                   _full(wm.shape), _full(bs_t.shape)],
        out_shape=[_sds((t, zc), F32), _sds((1, W_GRP), F32), _sds((1, W_GRP), F32), _sds(wm.shape, F32),
                   _sds(bs_t.shape, F32)],
        name="mix_a_bwd", compiler_params=_cp("arbitrary"),
    )(z, z, dmix, ln_g, ln_b, wm, wm_t, bs_t)


def _softplus(x):
    return jnp.maximum(x, 0.0) + jnp.log(1.0 + jnp.exp(-jnp.abs(x)))


def _neg_expm1(x):
    series = -x * (1.0 + x * 0.5 * (1.0 + x * (1.0 / 3.0) * (1.0 + x * 0.25 * (1.0 + x * 0.2))))
    return jnp.where(x > -0.1, series, 1.0 - jnp.exp(x))


def _rglru_gates(ext_ref, x_ref, halo, cw, cb, wa_ref, wx_ref, ba, bx, lam):
    ext_ref[0:8, :] = halo
    ext_ref[8:, :] = x_ref[...]
    ext = ext_ref[...]
    x0, x1, x2, x3 = ext[8:], _shift_down(ext, 1, 8), _shift_down(ext, 2, 8), _shift_down(ext, 3, 8)
    xc = cb + cw[3:4, :] * x0 + cw[2:3, :] * x1 + cw[1:2, :] * x2 + cw[0:1, :] * x3
    xcb = xc.astype(BF16)
    r = _sigmoid(_dot(xcb, wa_ref[...]) + ba)
    ig = _sigmoid(_dot(xcb, wx_ref[...]) + bx)
    sp = _softplus(-lam)
    la = -RGLRU_C * r * sp
    a = jnp.exp(la)
    mult = jnp.sqrt(_neg_expm1(2.0 * la))
    return (x0, x1, x2, x3), xc, r, ig, sp, a, mult


def _mix_b_fwd(z, mix, conv_w, conv_b, wa, wx, ba, bx, lam):
    t = z.shape[0]
    tm = _mix_tm(t)

    def body(x_ref, gb_ref, cw_ref, cb_ref, wa_ref, wx_ref, ba_ref, bx_ref, lam_ref, mix_in, o_ref, hs_ref,
             ext_ref, a_ref, b_ref, xh_ref, hc_ref):
        i = pl.program_id(0)

        @pl.when(i == 0)
        def _():
            xh_ref[...] = jnp.zeros_like(xh_ref)
            hc_ref[...] = jnp.zeros_like(hc_ref)

        _, xc, _, ig, _, a, mult = _rglru_gates(ext_ref, x_ref, xh_ref[...], cw_ref[...], cb_ref[...], wa_ref, wx_ref,
                                                ba_ref[...], bx_ref[...], lam_ref[...])
        xh_ref[...] = x_ref[tm - 8:tm, :]
        a_ref[...] = a
        b_ref[...] = mult * (ig * xc)
        rid = lax.broadcasted_iota(jnp.int32, (8, W_GRP), 0)

        def group(gi, hprev):
            base = pl.multiple_of(gi * 8, 8)
            ca = a_ref[pl.ds(base, 8), :]
            cb = b_ref[pl.ds(base, 8), :]
            for k in (1, 2, 4):
                m = rid >= k
                cb = jnp.where(m, ca * pltpu.roll(cb, k, 0) + cb, cb)
                ca = jnp.where(m, ca * pltpu.roll(ca, k, 0), ca)
            hh = cb + ca * hprev
            hs_ref[pl.ds(base, 8), :] = hh
            return hh[7:8, :]

        hlast = lax.fori_loop(0, tm // 8, group, hc_ref[0:1, :])
        hc_ref[...] = jnp.broadcast_to(hlast, hc_ref.shape)
        o_ref[...] = hs_ref[...] * _gelu(gb_ref[...])

    sq = _full((W_GRP, W_GRP))
    vec = _full((1, W_GRP))
    return pl.pallas_call(
        body, grid=(t // tm,),
        in_specs=[_zblk(tm, 2), _zblk(tm, 3), _full((4, W_GRP)), vec, sq, sq, vec, vec, vec, pl.BlockSpec(memory_space=pl.ANY)],
        out_specs=[_zblk(tm, 1), pl.BlockSpec((tm, W_GRP), lambda i: (i, 0))],
        out_shape=[_sds(mix.shape, F32), _sds((t, W_GRP), F32)],
        scratch_shapes=[pltpu.VMEM((tm + 8, W_GRP), F32), pltpu.VMEM((tm, W_GRP), F32), pltpu.VMEM((tm, W_GRP), F32),
                        pltpu.VMEM((8, W_GRP), F32), pltpu.VMEM((8, W_GRP), F32)],
        input_output_aliases={9: 0}, name="mix_b_fwd", compiler_params=_cp("arbitrary"),
    )(z, z, conv_w, conv_b, wa, wx, ba, bx, lam, mix)


def _mix_b_bwd(z, dz, dmix, hs, conv_w, conv_b, wa, wx, wa_t, wx_t, ba, bx, lam):
    t = z.shape[0]
    tm = _mix_tm(t)
    nt = t // tm
    hb = tm // 8

    def body(x_ref, gb_ref, xhalo_ref, hs_ref, hhalo_ref, dy_ref, cw_ref, cb_ref, wa_ref, wx_ref, wat_ref, wxt_ref,
             ba_ref, bx_ref, lam_ref, dz_in, dz_ref, dcw_ref, dcb_ref, dwa_ref, dwx_ref, dba_ref, dbx_ref, dlam_ref,
             ext_ref, c_ref, d_ref, g_ref, an_ref, gn_ref, dxn_ref):
        i = pl.program_id(0)
        first_tile = i == nt - 1

        @pl.when(i == 0)
        def _():
            for ref in (dcw_ref, dcb_ref, dwa_ref, dwx_ref, dba_ref, dbx_ref, dlam_ref, an_ref, gn_ref, dxn_ref):
                ref[...] = jnp.zeros_like(ref)

        cw, lam = cw_ref[...], lam_ref[...]
        xhalo = jnp.where(first_tile, 0.0, xhalo_ref[...])
        (x0, x1, x2, x3), xc, r, ig, sp, a, mult = _rglru_gates(
            ext_ref, x_ref, xhalo, cw, cb_ref[...], wa_ref, wx_ref, ba_ref[...], bx_ref[...], lam)
        hs = hs_ref[...]
        dy = dy_ref[...]
        gel, dgel = _gelu_and_grad(gb_ref[...])
        dz_ref[:, W_GRP:2 * W_GRP] = dy * hs * dgel

        ext_ref[0:tm, :] = a
        ext_ref[tm:tm + 8, :] = an_ref[...]
        an_ref[...] = a[0:8, :]
        c_ref[...] = _shift_up(ext_ref[...], 1, tm)
        d_ref[...] = dy * gel
        rid = lax.broadcasted_iota(jnp.int32, (8, W_GRP), 0)

        def group(j, gnext):
            base = pl.multiple_of((tm // 8 - 1 - j) * 8, 8)
            cc = c_ref[pl.ds(base, 8), :]
            cd = d_ref[pl.ds(base, 8), :]
            for k in (1, 2, 4):
                m = rid < 8 - k
                cd = jnp.where(m, cc * pltpu.roll(cd, 8 - k, 0) + cd, cd)
                cc = jnp.where(m, cc * pltpu.roll(cc, 8 - k, 0), cc)
            gg = cd + cc * gnext
            g_ref[pl.ds(base, 8), :] = gg
            return gg[0:1, :]

        gfirst = lax.fori_loop(0, tm // 8, group, gn_ref[0:1, :])
        gn_ref[...] = jnp.broadcast_to(gfirst, gn_ref.shape)
        g = g_ref[...]

        ext_ref[0:8, :] = jnp.where(first_tile, 0.0, hhalo_ref[...])
        ext_ref[8:, :] = hs
        hprev = _shift_down(ext_ref[...], 1, 8)
        da = g * hprev
        dmult = g * (ig * xc)
        di = g * mult * xc
        dxc = g * mult * ig
        dla = da * a - dmult * a * a / mult
        dr = dla * (-RGLRU_C * sp)
        dlam_ref[...] += jnp.sum(dla * (-RGLRU_C * r), axis=0, keepdims=True) * (-_sigmoid(-lam))
        dpr = dr * r * (1.0 - r)
        dpi = di * ig * (1.0 - ig)
        dprb, dpib, xcb = dpr.astype(BF16), dpi.astype(BF16), xc.astype(BF16)
        dba_ref[...] += jnp.sum(dpr, axis=0, keepdims=True)
        dbx_ref[...] += jnp.sum(dpi, axis=0, keepdims=True)
        dwa_ref[...] += _dot_tn(xcb, dprb)
        dwx_ref[...] += _dot_tn(xcb, dpib)
        dxc = dxc + _dot(dprb, wat_ref[...]) + _dot(dpib, wxt_ref[...])
        dcb_ref[...] += jnp.sum(dxc, axis=0, keepdims=True)
        dcw_ref[3:4, :] += jnp.sum(dxc * x0, axis=0, keepdims=True)
        dcw_ref[2:3, :] += jnp.sum(dxc * x1, axis=0, keepdims=True)
        dcw_ref[1:2, :] += jnp.sum(dxc * x2, axis=0, keepdims=True)
        dcw_ref[0:1, :] += jnp.sum(dxc * x3, axis=0, keepdims=True)
        ext_ref[0:tm, :] = dxc
        ext_ref[tm:tm + 8, :] = dxn_ref[...]
        dxn_ref[...] = dxc[0:8, :]
        ext = ext_ref[...]
        dz_ref[:, 0:W_GRP] = (cw[3:4, :] * dxc + cw[2:3, :] * _shift_up(ext, 1, tm) + cw[1:2, :] * _shift_up(ext, 2, tm)
                              + cw[0:1, :] * _shift_up(ext, 3, tm))

    sq = _full((W_GRP, W_GRP))
    vec = _full((1, W_GRP))
    halo = lambda col: pl.BlockSpec((8, W_GRP), lambda i: (jnp.maximum((nt - 1 - i) * hb - 1, 0), col))
    rev = lambda col: _zblk(tm, col, nt)
    return pl.pallas_call(
        body, grid=(nt,),
        in_specs=[rev(2), rev(3), halo(2), rev(0), halo(0), rev(1), _full((4, W_GRP)), vec, sq, sq, sq, sq, vec, vec, vec,
                  pl.BlockSpec(memory_space=pl.ANY)],
        out_specs=[pl.BlockSpec((tm, 2 * W_GRP), lambda i: (nt - 1 - i, 1)), _full((4, W_GRP)), vec, sq, sq, vec, vec, vec],
        out_shape=[_sds(dz.shape, F32), _sds((4, W_GRP), F32), _sds((1, W_GRP), F32), _sds((W_GRP, W_GRP), F32),
                   _sds((W_GRP, W_GRP), F32), _sds((1, W_GRP), F32), _sds((1, W_GRP), F32), _sds((1, W_GRP), F32)],
        scratch_shapes=[pltpu.VMEM((tm + 8, W_GRP), F32), pltpu.VMEM((tm, W_GRP), F32), pltpu.VMEM((tm, W_GRP), F32),
                        pltpu.VMEM((tm, W_GRP), F32), pltpu.VMEM((8, W_GRP), F32), pltpu.VMEM((8, W_GRP), F32),
                        pltpu.VMEM((8, W_GRP), F32)],
        input_output_aliases={15: 0}, name="mix_b_bwd", compiler_params=_cp("arbitrary"),
    )(z, z, z, hs, hs, dmix, conv_w, conv_b, wa, wx, wa_t, wx_t, ba, bx, lam, dz)


def _tri(n, lower):
    r = lax.broadcasted_iota(jnp.int32, (n, n), 0)
    c = lax.broadcasted_iota(jnp.int32, (n, n), 1)
    return jnp.where((r >= c) if lower else (r <= c), 1.0, 0.0).astype(BF16)


def _causal_stack():
    r = lax.broadcasted_iota(jnp.int32, (N_HEADS * HGRN_CHUNK, HGRN_CHUNK), 0)
    c = lax.broadcasted_iota(jnp.int32, (N_HEADS * HGRN_CHUNK, HGRN_CHUNK), 1)
    m = None
    for h in range(N_HEADS):
        mh = (r >= h * HGRN_CHUNK) & (r < (h + 1) * HGRN_CHUNK) & (r - h * HGRN_CHUNK >= c)
        m = mh if m is None else (m | mh)
    return m


def _stack_heads(x, hm):
    return jnp.concatenate([jnp.where(hm[h], x, 0.0) for h in range(N_HEADS)], axis=0)


def _unstack_heads(xs, hm):
    out = jnp.where(hm[0], xs[0:HGRN_CHUNK], 0.0)
    for h in range(1, N_HEADS):
        out = out + jnp.where(hm[h], xs[h * HGRN_CHUNK:(h + 1) * HGRN_CHUNK], 0.0)
    return out


def _hgrn_chunk(qv, fv, lb, tril):
    sq = _sigmoid(qv)
    qq = qv * sq
    sg = _sigmoid(fv)
    fg = lb + (1.0 - lb) * sg
    kk = 1.0 - fg
    bb = _dot_f32_lhs_exact(tril, jnp.log(fg))
    b_last = bb[HGRN_CHUNK - 1:HGRN_CHUNK, :]
    b_mid = bb[HGRN_CHUNK // 2 - 1:HGRN_CHUNK // 2, :]
    eq = jnp.exp(jnp.minimum(bb - b_mid, EXP_CLAMP))
    ek = jnp.exp(jnp.minimum(b_mid - bb, EXP_CLAMP))
    eb = jnp.exp(bb)
    el = jnp.exp(b_last - bb)
    return sq, qq, sg, fg, kk, b_last, eq, ek, eb, el


def _seg_mean(x, avg):
    return _dot_f32_rhs_exact(x, avg)


def _mix_c_fwd(z, mix, lb, ng):
    t = z.shape[0]
    tm = _mix_tm(t)
    nch = tm // HGRN_CHUNK

    def body(q_ref, f_ref, i_ref, g_ref, lb_ref, ng_ref, mix_in, y_ref, o_ref, ss_ref, s_ref):
        @pl.when(pl.program_id(0) == 0)
        def _():
            s_ref[...] = jnp.zeros_like(s_ref)

        hm = _head_masks()
        bmask = _block_mask()
        causal = _causal_stack()
        tril = _tri(HGRN_CHUNK, True)
        avg = jnp.where(bmask, 1.0 / HEAD_DIM, 0.0).astype(BF16)
        lb, ng = lb_ref[...], ng_ref[...]

        def chunk(c, carry):
            rows = pl.ds(pl.multiple_of(c * HGRN_CHUNK, HGRN_CHUNK), HGRN_CHUNK)
            vv = i_ref[rows, :]
            gv = g_ref[rows, :]
            _, qq, _, _, kk, b_last, eq, ek, eb, el = _hgrn_chunk(q_ref[rows, :], f_ref[rows, :], lb, tril)
            vb = vv.astype(BF16)
            qs = _stack_heads(qq * eq, hm).astype(BF16)
            att = jnp.where(causal, _dot_nt(qs, (kk * ek).astype(BF16)), 0.0)
            o = _unstack_heads(_dot(att.astype(BF16), vb), hm)
            s0 = s_ref[...]
            ss_ref[c] = s0
            o = o + _dot_nt((qq * eb).astype(BF16), s0.astype(BF16))
            s_ref[...] = s0 * jnp.exp(b_last) + jnp.where(bmask, _dot_tn(vb, (kk * el).astype(BF16)), 0.0)
            o_ref[rows, :] = o
            rstd = lax.rsqrt(_seg_mean(o * o, avg) + EPS)
            y_ref[rows, :] = o * rstd * ng * (gv * _sigmoid(gv))
            return carry

        lax.fori_loop(0, nch, chunk, 0, unroll=HGRN_UNROLL)

    vec = _full((1, W_GRP))
    return pl.pallas_call(
        body, grid=(t // tm,),
        in_specs=[_zblk(tm, 4), _zblk(tm, 5), _zblk(tm, 6), _zblk(tm, 7), vec, vec, pl.BlockSpec(memory_space=pl.ANY)],
        out_specs=[_zblk(tm, 2), pl.BlockSpec((tm, W_GRP), lambda i: (i, 0)),
                   pl.BlockSpec((nch, W_GRP, W_GRP), lambda i: (i, 0, 0))],
        out_shape=[_sds(mix.shape, F32), _sds((t, W_GRP), F32), _sds((t // HGRN_CHUNK, W_GRP, W_GRP), F32)],
        scratch_shapes=[pltpu.VMEM((W_GRP, W_GRP), F32)],
        input_output_aliases={6: 0}, name="mix_c_fwd", compiler_params=_cp("arbitrary"),
    )(z, z, z, z, lb, ng, mix)


def _mix_c_bwd(z, dz, dmix, o_pre, states, lb, ng):
    t = z.shape[0]
    tm = _mix_tm(t)
    nt = t // tm
    nch = tm // HGRN_CHUNK

    def body(q_ref, f_ref, i_ref, g_ref, o_ref, ss_ref, dy_ref, lb_ref, ng_ref, dz_in, dz_ref, dlb_ref, dng_ref, ds_ref):
        @pl.when(pl.program_id(0) == 0)
        def _():
            ds_ref[...] = jnp.zeros_like(ds_ref)
            dlb_ref[...] = jnp.zeros_like(dlb_ref)
            dng_ref[...] = jnp.zeros_like(dng_ref)

        hm = _head_masks()
        bmask = _block_mask()
        causal = _causal_stack()
        tril = _tri(HGRN_CHUNK, True)
        triu = _tri(HGRN_CHUNK, False)
        avg = jnp.where(bmask, 1.0 / HEAD_DIM, 0.0).astype(BF16)
        lb, ng = lb_ref[...], ng_ref[...]
        last_row = lax.broadcasted_iota(jnp.int32, (HGRN_CHUNK, W_GRP), 0) == HGRN_CHUNK - 1

        def chunk(j, carry):
            c = nch - 1 - j
            rows = pl.ds(pl.multiple_of(c * HGRN_CHUNK, HGRN_CHUNK), HGRN_CHUNK)
            qv, gv, vv = q_ref[rows, :], g_ref[rows, :], i_ref[rows, :]
            sq, qq, sg, fg, kk, b_last, eq, ek, eb, el = _hgrn_chunk(qv, f_ref[rows, :], lb, tril)
            s0 = ss_ref[c]
            ds1 = ds_ref[...]
            o = o_ref[rows, :]
            dy = dy_ref[rows, :]
            rstd = lax.rsqrt(_seg_mean(o * o, avg) + EPS)
            oh = o * rstd
            sgg = _sigmoid(gv)
            dz_ref[rows, 3 * W_GRP:4 * W_GRP] = dy * oh * ng * (sgg * (1.0 + gv * (1.0 - sgg)))
            don = dy * gv * sgg
            dng_ref[...] += jnp.sum(don * oh, axis=0, keepdims=True)
            doh = don * ng
            do = rstd * (doh - oh * _seg_mean(doh * oh, avg))
            qt, kt, qh, kh = qq * eq, kk * ek, qq * eb, kk * el
            vb, dob = vv.astype(BF16), do.astype(BF16)
            ktb, khb = kt.astype(BF16), kh.astype(BF16)
            ds1b = ds1.astype(BF16)
            qs = _stack_heads(qt, hm).astype(BF16)
            dos = _stack_heads(do, hm).astype(BF16)
            att = jnp.where(causal, _dot_nt(qs, ktb), 0.0).astype(BF16)
            datt = jnp.where(causal, _dot_nt(dos, vb), 0.0).astype(BF16)
            dv = _dot_tn(att, dos) + _dot_nt(khb, ds1b)
            dqt = _unstack_heads(_dot(datt, ktb), hm)
            dkt = _dot_tn(datt, qs)
            dqh = _dot(dob, s0.astype(BF16))
            dkh = _dot(vb, ds1b)
            e_last = jnp.exp(b_last)
            ds_ref[...] = ds1 * e_last + jnp.where(bmask, _dot_tn(dob, qh.astype(BF16)), 0.0)
            dq = dqt * eq + dqh * eb
            dk = dkt * ek + dkh * el
            db = qt * dqt - kt * dkt + qh * dqh - kh * dkh
            db_last = jnp.sum(kh * dkh, axis=0, keepdims=True) + e_last * jnp.sum(ds1 * s0, axis=0, keepdims=True)
            db = db + jnp.where(last_row, db_last, 0.0)
            dlogf = _dot_f32_lhs_exact(triu, db)
            dfg = dlogf / fg - dk
            dz_ref[rows, W_GRP:2 * W_GRP] = dfg * (1.0 - lb) * sg * (1.0 - sg)
            dlb_ref[...] += jnp.sum(dfg * (1.0 - sg), axis=0, keepdims=True)
            dz_ref[rows, 0:W_GRP] = dq * (sq * (1.0 + qv * (1.0 - sq)))
            dz_ref[rows, 2 * W_GRP:3 * W_GRP] = dv
            return carry

        lax.fori_loop(0, nch, chunk, 0, unroll=HGRN_UNROLL)

    vec = _full((1, W_GRP))
    rev = lambda col: _zblk(tm, col, nt)
    return pl.pallas_call(
        body, grid=(nt,),
        in_specs=[rev(4), rev(5), rev(6), rev(7), rev(0), pl.BlockSpec((nch, W_GRP, W_GRP), lambda i: (nt - 1 - i, 0, 0)),
                  rev(2), vec, vec, pl.BlockSpec(memory_space=pl.ANY)],
        out_specs=[pl.BlockSpec((tm, 4 * W_GRP), lambda i: (nt - 1 - i, 1)), vec, vec],
        out_shape=[_sds(dz.shape, F32), _sds((1, W_GRP), F32), _sds((1, W_GRP), F32)],
        scratch_shapes=[pltpu.VMEM((W_GRP, W_GRP), F32)],
        input_output_aliases={9: 0}, name="mix_c_bwd", compiler_params=_cp("arbitrary"),
    )(z, z, z, z, o_pre, states, dmix, lb, ng, dz)


def _pool_select(hm, s2, s4, s8, s16):
    return jnp.where(hm[0], s2, jnp.where(hm[1], s4, jnp.where(hm[2], s8, s16)))


def _pool_counts(hm, row0, tm):
    pos = (row0 + 1 + lax.broadcasted_iota(jnp.int32, (tm, W_GRP), 0)).astype(F32)
    win = _pool_select(hm, 2.0, 4.0, 8.0, 16.0)
    return jnp.minimum(pos, win)


def _pooled(ext_ref, x, halo, hm, cnt):
    ext_ref[0:POOL_HALO, :] = halo
    ext_ref[POOL_HALO:, :] = x
    e = ext_ref[...]
    s2 = e + pltpu.roll(e, 1, 0)
    s4 = s2 + pltpu.roll(s2, 2, 0)
    s8 = s4 + pltpu.roll(s4, 4, 0)
    s16 = s8 + pltpu.roll(s8, 8, 0)
    return _pool_select(hm, s2, s4, s8, s16)[POOL_HALO:] / cnt - x


def _mix_d_fwd(z, mix, wd, scale):
    t = z.shape[0]
    tm = _mix_tm(t)

    def body(x_ref, wd_ref, sc_ref, mix_in, o_ref, ext_ref, halo_ref):
        i = pl.program_id(0)

        @pl.when(i == 0)
        def _():
            halo_ref[...] = jnp.zeros_like(halo_ref)

        hm = _head_masks()
        x = x_ref[...]
        pooled = _pooled(ext_ref, x, halo_ref[...], hm, _pool_counts(hm, i * tm, tm))
        halo_ref[...] = x_ref[tm - POOL_HALO:tm, :]
        o_ref[...] = _dot(pooled.astype(BF16), wd_ref[...]) * sc_ref[...]

    return pl.pallas_call(
        body, grid=(t // tm,),
        in_specs=[_zblk(tm, 8), _full((W_GRP, W_GRP)), _full((1, W_GRP)), pl.BlockSpec(memory_space=pl.ANY)],
        out_specs=_zblk(tm, 3), out_shape=_sds(mix.shape, F32),
        scratch_shapes=[pltpu.VMEM((tm + POOL_HALO, W_GRP), F32), pltpu.VMEM((POOL_HALO, W_GRP), F32)],
        input_output_aliases={3: 0}, name="mix_d_fwd", compiler_params=_cp("arbitrary"),
    )(z, wd, scale, mix)


def _mix_d_bwd(z, dz, dmix, wd, wd_t, scale):
    t = z.shape[0]
    tm = _mix_tm(t)
    nt = t // tm
    hb = tm // POOL_HALO

    def body(x_ref, xhalo_ref, dy_ref, wd_ref, wdt_ref, sc_ref, dz_in, dz_ref, dwd_ref, dsc_ref, ext_ref, en_ref):
        i = pl.program_id(0)
        ri = nt - 1 - i

        @pl.when(i == 0)
        def _():
            en_ref[...] = jnp.zeros_like(en_ref)
            dwd_ref[...] = jnp.zeros_like(dwd_ref)
            dsc_ref[...] = jnp.zeros_like(dsc_ref)

        hm = _head_masks()
        cnt = _pool_counts(hm, ri * tm, tm)
        x = x_ref[...]
        pooled = _pooled(ext_ref, x, jnp.where(ri == 0, 0.0, xhalo_ref[...]), hm, cnt)
        pb = pooled.astype(BF16)
        dy = dy_ref[...]
        dsc_ref[...] += jnp.sum(dy * _dot(pb, wd_ref[...]), axis=0, keepdims=True)
        dyw = (dy * sc_ref[...]).astype(BF16)
        dwd_ref[...] += _dot_tn(pb, dyw)
        dpool = _dot(dyw, wdt_ref[...])
        e = dpool / cnt
        ext_ref[0:tm, :] = e
        ext_ref[tm:, :] = en_ref[...]
        en_ref[...] = e[0:POOL_HALO, :]
        ee = ext_ref[...]
        n = tm + POOL_HALO
        r2 = ee + pltpu.roll(ee, n - 1, 0)
        r4 = r2 + pltpu.roll(r2, n - 2, 0)
        r8 = r4 + pltpu.roll(r4, n - 4, 0)
        r16 = r8 + pltpu.roll(r8, n - 8, 0)
        dz_ref[...] = _pool_select(hm, r2, r4, r8, r16)[:tm] - dpool

    sq = _full((W_GRP, W_GRP))
    vec = _full((1, W_GRP))
    return pl.pallas_call(
        body, grid=(nt,),
        in_specs=[_zblk(tm, 8, nt), pl.BlockSpec((POOL_HALO, W_GRP), lambda i: (jnp.maximum((nt - 1 - i) * hb - 1, 0), 8)),
                  _zblk(tm, 3, nt), sq, sq, vec, pl.BlockSpec(memory_space=pl.ANY)],
        out_specs=[_zblk(tm, 8, nt), sq, vec],
        out_shape=[_sds(dz.shape, F32), _sds((W_GRP, W_GRP), F32), _sds((1, W_GRP), F32)],
        scratch_shapes=[pltpu.VMEM((tm + POOL_HALO, W_GRP), F32), pltpu.VMEM((POOL_HALO, W_GRP), F32)],
        input_output_aliases={6: 0}, name="mix_d_bwd", compiler_params=_cp("arbitrary"),
    )(z, z, dmix, wd, wd_t, scale, dz)


def _as2d(a):
    if a.ndim == 1:
        return a.reshape(1, a.shape[0])
    return a.reshape(-1, a.shape[-1])


def _adamw(w, g, m, v, name):
    shape = w.shape
    w2, g2, m2, v2 = _as2d(w), _as2d(g), _as2d(m), _as2d(v)
    rows, cols = w2.shape
    tr = _pick(rows, (1024, 512, 256, 128, 64, 32, 16, 8))
    if tr * cols * 4 * 14 > VMEM_LIMIT_BYTES:
        tr = _pick(rows, (256, 128, 64, 32, 16, 8))

    def body(w_ref, g_ref, m_ref, v_ref, d_ref, nm_ref, nv_ref):
        gv = g_ref[...]
        mn = ADAM_B1 * m_ref[...] + (1.0 - ADAM_B1) * gv
        vn = ADAM_B2 * v_ref[...] + (1.0 - ADAM_B2) * (gv * gv)
        m_hat = mn / (1.0 - ADAM_B1 ** ADAM_STEP)
        v_hat = vn / (1.0 - ADAM_B2 ** ADAM_STEP)
        d_ref[...] = -ADAM_LR * (m_hat / (jnp.sqrt(v_hat) + ADAM_EPS) + ADAM_WD * w_ref[...])
        nm_ref[...] = mn
        nv_ref[...] = vn

    blk = pl.BlockSpec((tr, cols), lambda i: (i, 0))
    outs = pl.pallas_call(
        body, grid=(rows // tr,), in_specs=[blk] * 4, out_specs=[blk] * 3, out_shape=[_sds((rows, cols), F32)] * 3,
        name=name, compiler_params=_cp("parallel"),
    )(w2, g2, m2, v2)
    return tuple(o.reshape(shape) for o in outs)


def _adamw_layer(w, g, m, v, layer, bufs, name):
    nl, r, cs = w.shape
    tr = _pick(r, (256, 128, 64, 32, 16, 8))

    def body(w_ref, g_ref, m_ref, v_ref, *rest):
        go_ref, d_ref, nm_ref, nv_ref = rest[-4:]
        gv = g_ref[...]
        mn = ADAM_B1 * m_ref[...] + (1.0 - ADAM_B1) * gv
        vn = ADAM_B2 * v_ref[...] + (1.0 - ADAM_B2) * (gv * gv)
        m_hat = mn / (1.0 - ADAM_B1 ** ADAM_STEP)
        v_hat = vn / (1.0 - ADAM_B2 ** ADAM_STEP)
        go_ref[...] = gv
        d_ref[...] = -ADAM_LR * (m_hat / (jnp.sqrt(v_hat) + ADAM_EPS) + ADAM_WD * w_ref[...])
        nm_ref[...] = mn
        nv_ref[...] = vn

    lay = pl.BlockSpec((None, tr, cs), lambda i: (layer, i, 0))
    in_specs = [lay, pl.BlockSpec((tr, cs), lambda i: (i, 0)), lay, lay]
    args = [w, g, m, v]
    aliases = {}
    if bufs is not None:
        in_specs += [pl.BlockSpec(memory_space=pl.ANY)] * 4
        args += list(bufs)
        aliases = {4 + i: i for i in range(4)}
    return pl.pallas_call(
        body, grid=(r // tr,), in_specs=in_specs, out_specs=[lay] * 4, out_shape=[_sds((nl, r, cs), F32)] * 4,
        input_output_aliases=aliases, name=name, compiler_params=_cp("parallel"),
    )(*args)


def _slot_sum(own, slots, name):
    n_slots, rows, cols = slots.shape
    whole_fits = rows * cols * 4 * (n_slots + 2) * 2 <= VMEM_LIMIT_BYTES // 2
    tr = rows if whole_fits else _pick(rows, (512, 352, 256, 128, 64, 32, 16, 8))

    def body(*refs):
        s_ref, o_ref = refs[-2], refs[-1]
        acc = s_ref[0].astype(F32) if own is None else refs[0][...].astype(F32) + s_ref[0].astype(F32)
        for k in range(1, n_slots):
            acc = acc + s_ref[k].astype(F32)
        o_ref[...] = acc

    row = pl.BlockSpec((tr, cols), lambda i: (i, 0))
    return pl.pallas_call(
        body, grid=(rows // tr,),
        in_specs=([] if own is None else [row]) + [pl.BlockSpec((n_slots, tr, cols), lambda i: (0, i, 0))],
        out_specs=row, out_shape=_sds((rows, cols), F32), name=name, compiler_params=_cp("parallel"),
    )(*(() if own is None else (own,)), slots)


def _me():
    return lax.axis_index("x"), lax.axis_index("y"), lax.axis_index("c")


def _other_chips(x, y):
    return [(1 - x, y), (x, 1 - y), (1 - x, 1 - y)]


ANY_SPEC = pl.BlockSpec(memory_space=pl.ANY)
HBM_SPEC = pl.BlockSpec(memory_space=pltpu.HBM)
SEM_SPEC = pl.BlockSpec(memory_space=pltpu.SEMAPHORE)
SPLIT_COPY_PARAMS = pltpu.CompilerParams(has_side_effects=pltpu.SideEffectType.DATAFLOW_SIDE_EFFECTING)
N_CHIPS = 4


def _aligned(v, m):
    return v if isinstance(v, int) else pl.multiple_of(v, m)


def _in_hbm(arr):
    return pltpu.with_memory_space_constraint(arr, pltpu.HBM)


def _peer(x, y, c, k):
    fx, fy, fc = (k >> 2) & 1, (k >> 1) & 1, k & 1
    px = 1 - x if fx else x
    py = 1 - y if fy else y
    pc = 1 - c if fc else c
    return px, py, pc


def _gather_start(shards, after, name):
    n = len(shards)

    def body(*refs):
        src, land = refs[:n], refs[n:2 * n]
        send_sems, recv_sems = refs[2 * n + 1], refs[2 * n + 2]
        token = refs[-1]
        x, y, c = _me()
        for w in range(n):
            for chip in _other_chips(x, y):
                pltpu.make_async_remote_copy(
                    src_ref=src[w], dst_ref=land[w].at[2 * x + y], send_sem=send_sems.at[w], recv_sem=recv_sems.at[w],
                    device_id=(*chip, c), device_id_type=MESH_ID).start()
        token[...] = jnp.zeros_like(token)

    lands = [lax.empty((N_CHIPS,) + s.shape, s.dtype) for s in shards]
    thru = [pltpu.HBM(s.shape, s.dtype) for s in shards] + [pltpu.HBM(z.shape, z.dtype) for z in lands]
    outs = pl.pallas_call(
        body, name=name,
        out_shape=(pltpu.SemaphoreType.DMA((n,)), pltpu.SemaphoreType.DMA((n,)), *thru, _sds((8, 128), F32)),
        in_specs=[HBM_SPEC] * (2 * n) + [ANY_SPEC],
        out_specs=(SEM_SPEC, SEM_SPEC, *[HBM_SPEC] * (2 * n), pl.BlockSpec(memory_space=pltpu.VMEM)),
        input_output_aliases={i: 2 + i for i in range(2 * n)}, compiler_params=SPLIT_COPY_PARAMS,
    )(*[_in_hbm(s) for s in shards], *[_in_hbm(z) for z in lands], after)
    return (outs[0], outs[1], outs[2:2 + n], outs[2 + n:2 + 2 * n]), outs[-1]


def _gather_wait(send_sems, recv_sems, srcs, lands, after, name):
    n = len(srcs)

    def body(*refs):
        land = refs[n:2 * n]
        send_sems, recv_sems = refs[2 * n], refs[2 * n + 1]
        x, y, c = _me()
        for w in range(n):
            three = land[w].at[pl.ds(0, N_CHIPS - 1)]
            cp = pltpu.make_async_remote_copy(src_ref=three, dst_ref=three, send_sem=send_sems.at[w], recv_sem=recv_sems.at[w],
                                              device_id=(x, y, c), device_id_type=MESH_ID)
            cp.wait_send()
            cp.wait_recv()

    both = list(srcs) + list(lands)
    outs = pl.pallas_call(
        body, name=name, out_shape=tuple(pltpu.HBM(b.shape, b.dtype) for b in both),
        in_specs=[HBM_SPEC] * (2 * n) + [SEM_SPEC, SEM_SPEC, ANY_SPEC], out_specs=[HBM_SPEC] * (2 * n),
        input_output_aliases={i: i for i in range(2 * n)}, compiler_params=SPLIT_COPY_PARAMS,
    )(*both, send_sems, recv_sems, after)
    return outs[n:2 * n]


def _push_start(grads, small, name):
    n = len(grads)
    srcs = list(grads) + ([] if small is None else [small])
    ns = len(srcs)

    def body(*refs):
        src, slots = refs[:ns], refs[ns:2 * ns]
        send_sems, recv_sems = refs[2 * ns], refs[2 * ns + 1]
        token = refs[-1]
        x, y, c = _me()
        for w in range(ns):
            for k in range(1, N_DEV):
                px, py, pc = _peer(x, y, c, k)
                if w < n:
                    hr = src[w].shape[1] // 2
                    piece = src[w].at[2 * px + py, pl.ds(_aligned(pc * hr, 16), hr), :]
                    slot = slots[w].at[k - 1]
                else:
                    piece = src[w]
                    slot = slots[w].at[4 * x + 2 * y + c]
                pltpu.make_async_remote_copy(
                    src_ref=piece, dst_ref=slot, send_sem=send_sems.at[w], recv_sem=recv_sems.at[w],
                    device_id=(px, py, pc), device_id_type=MESH_ID).start()
        token[...] = jnp.zeros_like(token)

    slots = [lax.empty((N_DEV - 1, g.shape[1] // 2, g.shape[2]), g.dtype) for g in grads]
    if small is not None:
        slots.append(lax.empty((N_DEV,) + small.shape, small.dtype))
    both = srcs + slots
    outs = pl.pallas_call(
        body, name=name,
        out_shape=(pltpu.SemaphoreType.DMA((ns,)), pltpu.SemaphoreType.DMA((ns,)),
                   *[pltpu.HBM(b.shape, b.dtype) for b in both], _sds((8, 128), F32)),
        in_specs=[HBM_SPEC] * len(both),
        out_specs=(SEM_SPEC, SEM_SPEC, *[HBM_SPEC] * len(both), pl.BlockSpec(memory_space=pltpu.VMEM)),
        input_output_aliases={i: 2 + i for i in range(len(both))}, compiler_params=SPLIT_COPY_PARAMS,
    )(*[_in_hbm(b) for b in both])
    return (outs[0], outs[1], outs[2:2 + ns], outs[2 + ns:2 + 2 * ns]), outs[-1]


def _push_wait(send_sems, recv_sems, srcs, slots, after, name):
    n = len(srcs)

    def body(*refs):
        slot = refs[n:2 * n]
        send_sems, recv_sems = refs[2 * n], refs[2 * n + 1]
        x, y, c = _me()
        for w in range(n):
            seven = slot[w].at[pl.ds(0, N_DEV - 1)]
            cp = pltpu.make_async_remote_copy(src_ref=seven, dst_ref=seven, send_sem=send_sems.at[w],
                                              recv_sem=recv_sems.at[w], device_id=(x, y, c), device_id_type=MESH_ID)
            cp.wait_send()
            cp.wait_recv()

    both = list(srcs) + list(slots)
    outs = pl.pallas_call(
        body, name=name, out_shape=tuple(pltpu.HBM(b.shape, b.dtype) for b in both),
        in_specs=[HBM_SPEC] * (2 * n) + [SEM_SPEC, SEM_SPEC, ANY_SPEC], out_specs=[HBM_SPEC] * (2 * n),
        input_output_aliases={i: i for i in range(2 * n)}, compiler_params=SPLIT_COPY_PARAMS,
    )(*both, send_sems, recv_sems, after)
    return outs[:n], outs[n:]


SWAP_CHUNK_BYTES = 2 * 1024 * 1024


def _swap_chunk_rows(hr, cs):
    ch = hr
    while ch * cs * 4 > SWAP_CHUNK_BYTES and ch % 16 == 0:
        ch //= 2
    return ch


def _swap_halves(halves, name):
    n = len(halves)
    chunk = [_swap_chunk_rows(*h.shape) for h in halves]
    rounds = max(h.shape[0] // ch for h, ch in zip(halves, chunk))

    def body(*refs):
        src, dst, buf = refs[:n], refs[n:2 * n], refs[2 * n:3 * n]
        load_sems, put_sems, send_sems, recv_sems = refs[3 * n:]
        x, y, c = _me()
        sibling = (x, y, 1 - c)
        for j in range(rounds):
            live = [w for w in range(n) if j < src[w].shape[0] // chunk[w]]
            loads = [pltpu.make_async_copy(src[w].at[pl.ds(j * chunk[w], chunk[w])], buf[w], load_sems.at[w]) for w in live]
            for ld in loads:
                ld.start()
            moves = []
            for ld, w in zip(loads, live):
                ld.wait()
                rows = pl.ds(_aligned(c * src[w].shape[0] + j * chunk[w], 8), chunk[w])
                put = pltpu.make_async_copy(buf[w], dst[w].at[rows], put_sems.at[w])
                send = pltpu.make_async_remote_copy(src_ref=buf[w], dst_ref=dst[w].at[rows], send_sem=send_sems.at[w],
                                                    recv_sem=recv_sems.at[w], device_id=sibling, device_id_type=MESH_ID)
                put.start()
                send.start()
                moves.append((put, send))
            for put, send in moves:
                put.wait()
                send.wait_send()
        for w in range(n):
            hr = src[w].shape[0]
            got = dst[w].at[pl.ds(_aligned((1 - c) * hr, 8), hr)]
            pltpu.make_async_remote_copy(src_ref=got, dst_ref=got, send_sem=send_sems.at[w], recv_sem=recv_sems.at[w],
                                         device_id=sibling, device_id_type=MESH_ID).wait_recv()

    return pl.pallas_call(
        body, in_specs=[ANY_SPEC] * n, out_specs=[ANY_SPEC] * n,
        out_shape=[_sds((2 * h.shape[0], h.shape[1]), F32) for h in halves],
        scratch_shapes=[pltpu.VMEM((ch, h.shape[1]), F32) for h, ch in zip(halves, chunk)]
        + [pltpu.SemaphoreType.DMA((n,))] * 4,
        name=name,
    )(*halves)


BIG = ("w_in", "w_out", "w_up", "w_down", "w_pe", "w_pg")
ROW_SHARDED = ("w_out", "w_down", "w_pg")
SMALL = ("norm1_g", "a_ln_g", "a_ln_b", "a_ws", "a_bs", "b_conv_w", "b_conv_b", "b_wa", "b_ba", "b_wx", "b_bx", "b_lam",
         "c_lb", "c_norm_g", "d_w", "d_scale", "norm2_g", "ffn_conv_w", "ffn_conv_b", "norm3_g", "final_g")
SMALL_SHARDED = ("b_conv_w", "ffn_conv_w")
WEIGHTS = ("norm1_g", "w_in", "a_ln_g", "a_ln_b", "a_ws", "a_bs", "b_conv_w", "b_conv_b", "b_wa", "b_ba", "b_wx", "b_bx",
           "b_lam", "c_lb", "c_norm_g", "d_w", "d_scale", "w_out", "norm2_g", "w_up", "ffn_conv_w", "ffn_conv_b", "w_down",
           "norm3_g", "w_pe", "w_pg", "final_g")
ARGS = ("x", "p") + WEIGHTS + ("loss_target",) + tuple("m_" + n for n in WEIGHTS) + tuple("v_" + n for n in WEIGHTS)


def _block_diag(w):
    eye = jnp.eye(N_HEADS, dtype=w.dtype)
    return (eye[None, :, None, :, None] * w[:, :, :, None, :]).reshape(w.shape[0], W_GRP, W_GRP)


def _diag_blocks(m):
    m4 = m.reshape(N_HEADS, HEAD_DIM, N_HEADS, HEAD_DIM)
    return jnp.stack([m4[h, :, h, :] for h in range(N_HEADS)])


def _lower_bounds(c_lb):
    lbs = jnp.cumsum(jax.nn.softmax(c_lb, axis=0), axis=0)
    return lbs - lbs[0:1]


def kernel(x, p, norm1_g, w_in, a_ln_g, a_ln_b, a_ws, a_bs, b_conv_w, b_conv_b, b_wa, b_ba, b_wx, b_bx, b_lam, c_lb, c_norm_g, d_w, d_scale, w_out, norm2_g, w_up, ffn_conv_w, ffn_conv_b, w_down, norm3_g, w_pe, w_pg, final_g, loss_target, m_norm1_g, m_w_in, m_a_ln_g, m_a_ln_b, m_a_ws, m_a_bs, m_b_conv_w, m_b_conv_b, m_b_wa, m_b_ba, m_b_wx, m_b_bx, m_b_lam, m_c_lb, m_c_norm_g, m_d_w, m_d_scale, m_w_out, m_norm2_g, m_w_up, m_ffn_conv_w, m_ffn_conv_b, m_w_down, m_norm3_g, m_w_pe, m_w_pg, m_final_g, v_norm1_g, v_w_in, v_a_ln_g, v_a_ln_b, v_a_ws, v_a_bs, v_b_conv_w, v_b_conv_b, v_b_wa, v_b_ba, v_b_wx, v_b_bx, v_b_lam, v_c_lb, v_c_norm_g, v_d_w, v_d_scale, v_w_out, v_norm2_g, v_w_up, v_ffn_conv_w, v_ffn_conv_b, v_w_down, v_norm3_g, v_w_pe, v_w_pg, v_final_g):
    return _step((x, p, norm1_g, w_in, a_ln_g, a_ln_b, a_ws, a_bs, b_conv_w, b_conv_b, b_wa, b_ba, b_wx, b_bx, b_lam, c_lb, c_norm_g, d_w, d_scale, w_out, norm2_g, w_up, ffn_conv_w, ffn_conv_b, w_down, norm3_g, w_pe, w_pg, final_g, loss_target, m_norm1_g, m_w_in, m_a_ln_g, m_a_ln_b, m_a_ws, m_a_bs, m_b_conv_w, m_b_conv_b, m_b_wa, m_b_ba, m_b_wx, m_b_bx, m_b_lam, m_c_lb, m_c_norm_g, m_d_w, m_d_scale, m_w_out, m_norm2_g, m_w_up, m_ffn_conv_w, m_ffn_conv_b, m_w_down, m_norm3_g, m_w_pe, m_w_pg, m_final_g, v_norm1_g, v_w_in, v_a_ln_g, v_a_ln_b, v_a_ws, v_a_bs, v_b_conv_w, v_b_conv_b, v_b_wa, v_b_ba, v_b_wx, v_b_bx, v_b_lam, v_c_lb, v_c_norm_g, v_d_w, v_d_scale, v_w_out, v_norm2_g, v_w_up, v_ffn_conv_w, v_ffn_conv_b, v_w_down, v_norm3_g, v_w_pe, v_w_pg, v_final_g))


SMALL_PER_LAYER = tuple(n for n in SMALL if n != "final_g")
GATHERED = BIG + SMALL_SHARDED
GATHER_LAYER0 = (("a", ("w_in", "b_conv_w")), ("b", ("w_out",)), ("c", ("w_up", "ffn_conv_w")), ("d", ("w_down", "w_pe", "w_pg")))
PUSH_EARLY = ("w_pe", "w_pg", "w_down", "w_up")
PUSH_MID = ("w_out",)
PUSH_LATE = ("w_in",)
SMALL_MID = tuple(n for n in SMALL_PER_LAYER if n != "norm1_g")


def _cols_to_slabs(m):
    r, c4 = m.shape
    return jnp.moveaxis(m.reshape(r, N_CHIPS, c4 // N_CHIPS), 1, 0)


def _slabs_to_cols(s):
    return jnp.moveaxis(s, 0, 1).reshape(s.shape[1], -1)


def _pack_small(parts):
    flat = jnp.concatenate([p.reshape(-1) for p in parts])
    return jnp.pad(flat, (0, (-flat.shape[0]) % 1024)).reshape(-1, 128)


def _step(args):
    a = dict(zip(ARGS, args, strict=True))
    x0 = a["x"][0]
    target = a["loss_target"][0]
    nl = a["norm1_g"].shape[0]
    t, d = x0.shape
    f = a["w_down"].shape[1] * N_CHIPS
    cx, cy, cc = _me()
    my_shard = 2 * cx + cy
    shards = {n: a[n].astype(BF16) for n in BIG}
    shards.update({n: a[n] for n in SMALL_SHARDED})

    def start_gather(l, names, after, tag):
        return _gather_start([shards[n][l] for n in names], after, f"gather_start_{l}{tag}")

    def finish_gather(l, names, handle, after, tag):
        send, recv, srcs, lands = handle
        lands = _gather_wait(send, recv, srcs, lands, after, f"gather_wait_{l}{tag}")
        w = {}
        for n, land in zip(names, lands):
            full = lax.dynamic_update_slice(land, shards[n][l][None], (my_shard, 0, 0))
            if n in ROW_SHARDED:
                w[n] = full.reshape(-1, full.shape[-1])
            elif n in ("w_up", "w_pe"):
                w[n] = full
            else:
                w[n] = _slabs_to_cols(full)
        return w

    lbs, lbs_vjp = jax.vjp(_lower_bounds, a["c_lb"])
    tril = jnp.tril(jnp.ones((GMLP_CHUNK, GMLP_CHUNK), F32))

    stacked_params = {"wm": (a["a_ws"] * tril).astype(BF16)}
    stacked_params["wm_t"] = jnp.swapaxes(stacked_params["wm"], 2, 3)
    stacked_params["bs_t"] = jnp.repeat(jnp.swapaxes(a["a_bs"], 1, 2), HEAD_DIM, axis=2)
    for nm in ("b_wa", "b_wx", "d_w"):
        bd = _block_diag(a[nm]).astype(BF16)
        stacked_params[nm], stacked_params[nm + "_t"] = bd, jnp.swapaxes(bd, 1, 2)
    for nm in ("a_ln_g", "a_ln_b", "b_conv_b", "b_ba", "b_bx", "b_lam", "d_scale"):
        stacked_params[nm] = a[nm].reshape(nl, 1, W_GRP)
    stacked_params["lb"] = lbs.reshape(nl, 1, W_GRP)
    stacked_params["ng"] = jnp.tile(a["c_norm_g"], (1, N_HEADS)).reshape(nl, 1, W_GRP)
    stacked_params["ffn_conv_b"] = a["ffn_conv_b"].reshape(nl, 1, 2 * f)

    def layer_params(l, w):
        q = {k: v[l] for k, v in stacked_params.items()}
        q.update(w)
        return q

    saved, weights, params = [], [], []
    first_groups = {tag: start_gather(0, names, x0, tag)[0] for tag, names in GATHER_LAYER0}
    xl = x0
    for l in range(nl):
        if l == 0:
            w = finish_gather(0, GATHER_LAYER0[0][1], first_groups["a"], xl, "a")
        else:
            w = finish_gather(l, GATHERED, next_handle, xl, "")
        s = {"x0": xl}
        s["h1"] = _rms_fwd(xl, a["norm1_g"][0], "rms1_fwd") if l == 0 else h_next
        token = None
        if 0 < l < nl - 1:
            next_handle, token = start_gather(l + 1, GATHERED, s["h1"], "")
        s["z"] = _mm(s["h1"], w["w_in"], "nn", out_dtype=F32, name="mm_z", after=token)
        q = layer_params(l, w)
        mix = _mix_a_fwd(s["z"], d, q["a_ln_g"], q["a_ln_b"], q["wm"], q["bs_t"])
        mix, s["hs"] = _mix_b_fwd(s["z"], mix, q["b_conv_w"], q["b_conv_b"], q["b_wa"], q["b_wx"], q["b_ba"], q["b_bx"],
                                  q["b_lam"])
        mix, s["o_pre"], s["states"] = _mix_c_fwd(s["z"], mix, q["lb"], q["ng"])
        s["mix"] = _mix_d_fwd(s["z"], mix, q["d_w"], q["d_scale"])
        def land(tag, after):
            if l == 0:
                w.update(finish_gather(0, dict(GATHER_LAYER0)[tag], first_groups[tag], after, tag))
                q.update(w)

        land("b", s["mix"])
        s["x1"], s["h2"] = _mm(s["mix"], w["w_out"], "nn", res=xl, out_dtype=F32, name="mm_out",
                               norm_g=a["norm2_g"][l], tm_max=512)
        land("c", s["h2"])
        s["hf_g"] = _mm(s["h2"], w["w_up"], "nn", b_slabs=True, n=f, out_dtype=F32, name="mm_up_g")
        s["hf_v"] = _mm(s["h2"], w["w_up"], "nn", b_slabs=True, n=f, b_noff=f, out_dtype=F32, name="mm_up_v")
        s["act"], s["gc"], s["vc"] = _ffn_act_fwd(s["hf_g"], s["hf_v"], q["ffn_conv_w"], q["ffn_conv_b"])
        land("d", s["act"])
        token = None
        if l == 0 and nl > 1:
            next_handle, token = start_gather(1, GATHERED, w["w_down"], "")
        s["x2"], s["h3"] = _mm(s["act"], w["w_down"], "nn", res=s["x1"], out_dtype=F32, name="mm_down",
                               norm_g=a["norm3_g"][l], tm_max=512, after=token)
        s["pre"] = _mm(s["h3"], w["w_pg"], "nn", out_dtype=F32, name="mm_pg")
        s["pe"] = _mm(a["p"][l, 0], w["w_pe"], "nn", b_slabs=True, out_dtype=F32, name="mm_pe")
        xl, h_next = _ple_fwd(s["x2"], s["pe"], s["pre"], a["norm1_g"][l + 1] if l + 1 < nl else None)
        saved.append(s)
        weights.append(w)
        params.append(q)

    dx, g_final, loss = _final_loss(xl, a["final_g"], target)
    loss = lax.psum(loss[0, 0], ("x", "y", "c"))

    stacked = {n: None for n in BIG}
    small_sums = {}

    def finish_push(l, names, handle, tag, after):
        send, recv, srcs, slots = handle
        srcs, slots = _push_wait(send, recv, srcs, slots, after, f"push_wait_{l}{tag}")
        halves = []
        for n, g, sl in zip(names, srcs, slots):
            hr = g.shape[1] // 2
            own = lax.dynamic_slice(g, (my_shard, cc * hr, 0), (1, hr, g.shape[2]))[0]
            halves.append(_slot_sum(own, sl, "sum_" + n))
        for n, g in zip(names, _swap_halves(halves, "swap_halves_" + tag)):
            stacked[n] = _adamw_layer(a[n], g, a["m_" + n], a["v_" + n], l, stacked[n], "adamw_" + n)
        if len(srcs) > len(names):
            by_sender = lax.dynamic_update_slice(slots[-1], srcs[-1][None], (2 * my_shard + cc, 0, 0))
            small_sums[l, tag] = _slot_sum(None, by_sender, "sum_small_" + tag)
        return stacked[names[-1]][1]

    pending = []
    token = None
    for l in reversed(range(nl)):
        q, s, w = params[l], saved[l], weights[l]
        gs = {}
        dpe, dpre = _ple_bwd(dx, s["pe"], s["pre"], after=token)
        g_pe = _mm(a["p"][l, 0], dpe, "tn", out_dtype=BF16, name="mm_dwpe", out_slabs=N_CHIPS)
        g_pg = _mm(s["h3"], dpre, "tn", out_dtype=BF16, name="mm_dwpg")
        dx2, gs["norm3_g"] = _mm(dpre, w["w_pg"], "nt", out_dtype=F32, name="mm_dh3", tm_max=512,
                                 rms_bwd=(s["x2"], a["norm3_g"][l], dx))
        g_down = _mm(s["act"], dx2, "tn", out_dtype=BF16, name="mm_dwdown")
        dhf_g, dhf_v, sums_g, sums_v = _ffn_bwd(dx2, w["w_down"], s["gc"], s["vc"], s["hf_g"], s["hf_v"], q["ffn_conv_w"])
        gs["ffn_conv_w"] = jnp.concatenate([sums_g[0:3], sums_v[0:3]], axis=1)
        gs["ffn_conv_b"] = jnp.concatenate([sums_g[3:4], sums_v[3:4]], axis=1)
        g_up = _mm(s["h2"], dhf_g, "tn", out_dtype=BF16, name="mm_dwup_g", out_slabs=N_CHIPS, out_n=2 * f)
        g_up = _mm(s["h2"], dhf_v, "tn", out_dtype=BF16, name="mm_dwup_v", out_slabs=N_CHIPS, out_n=2 * f, o_noff=f, out_buf=g_up)
        early = {"w_pe": g_pe, "w_up": g_up, "w_pg": g_pg.reshape(N_CHIPS, -1, g_pg.shape[-1]),
                 "w_down": g_down.reshape(N_CHIPS, -1, g_down.shape[-1])}
        early_handle, token = _push_start([early[n] for n in PUSH_EARLY], None, f"push_start_{l}a")
        dh2 = _mm(dhf_g, w["w_up"], "nt", b_slabs=True, out_dtype=F32, name="mm_dh2_g", after=token)
        dx1, gs["norm2_g"] = _mm(dhf_v, w["w_up"], "nt", b_slabs=True, b_koff=f, res=dh2, out_dtype=F32, name="mm_dh2_v",
                                 tm_max=512, rms_bwd=(s["x1"], a["norm2_g"][l], dx2))
        g_out = _mm(s["mix"], dx1, "tn", out_dtype=BF16, name="mm_dwout")
        dmix = _mm(dx1, w["w_out"], "nt", out_dtype=F32, name="mm_dmix")
        dz, gs["a_ln_g"], gs["a_ln_b"], dws, dbs_t = _mix_a_bwd(s["z"], dmix, q["a_ln_g"], q["a_ln_b"], q["wm"], q["wm_t"],
                                                               q["bs_t"])
        gs["a_ws"] = dws * tril
        gs["a_bs"] = dbs_t.reshape(GMLP_CHUNK, N_HEADS, HEAD_DIM).sum(-1).T
        dz, gs["b_conv_w"], gs["b_conv_b"], dwa, dwx, gs["b_ba"], gs["b_bx"], gs["b_lam"] = _mix_b_bwd(
            s["z"], dz, dmix, s["hs"], q["b_conv_w"], q["b_conv_b"], q["b_wa"], q["b_wx"], q["b_wa_t"], q["b_wx_t"],
            q["b_ba"], q["b_bx"], q["b_lam"])
        gs["b_wa"], gs["b_wx"] = _diag_blocks(dwa), _diag_blocks(dwx)
        dz, gs["c_lb"], dng = _mix_c_bwd(s["z"], dz, dmix, s["o_pre"], s["states"], q["lb"], q["ng"])
        gs["c_norm_g"] = dng.reshape(N_HEADS, HEAD_DIM).sum(0)
        dz, dwd, gs["d_scale"] = _mix_d_bwd(s["z"], dz, dmix, q["d_w"], q["d_w_t"], q["d_scale"])
        gs["d_w"] = _diag_blocks(dwd)
        small = [gs[n] for n in SMALL_MID] + ([g_final] if l == nl - 1 else [])
        mid_handle, token = _push_start([g_out.reshape(N_CHIPS, -1, g_out.shape[-1])], _pack_small(small), f"push_start_{l}b")
        g_in = _mm(s["h1"], dz, "tn", out_dtype=BF16, name="mm_dwin")
        dx, gs["norm1_g"] = _mm(dz, w["w_in"], "nt", out_dtype=F32, name="mm_dh1", after=(token, g_in), tm_max=512,
                                rms_bwd=(s["x0"], a["norm1_g"][l], dx1))

        late_handle, token = _push_start([_cols_to_slabs(g_in)], _pack_small([gs["norm1_g"]]), f"push_start_{l}c")
        dep = token
        for push in pending:
            dep = finish_push(*push, dep)
        pending = [(l, PUSH_EARLY, early_handle, "a"), (l, PUSH_MID, mid_handle, "b"), (l, PUSH_LATE, late_handle, "c")]
    for push in pending:
        dep = finish_push(*push, dep)
    grad_x = dx[None]

    def small_shape(n):
        return a[n].shape[1:-1] + (a[n].shape[-1] * N_CHIPS,) if n in SMALL_SHARDED else a[n].shape[1:]

    per_layer = {n: [] for n in SMALL_PER_LAYER}
    for l in range(nl):
        per_layer["norm1_g"].append(small_sums[l, "c"].reshape(-1)[:d])
        vec, off = small_sums[l, "b"].reshape(-1), 0
        for n in SMALL_MID:
            shape = small_shape(n)
            size = 1
            for dim in shape:
                size *= dim
            per_layer[n].append(vec[off:off + size].reshape(shape))
            off += size
        if l == nl - 1:
            grad_final = vec[off:off + d]
    grads = {n: jnp.stack(per_layer[n]) for n in SMALL_PER_LAYER}
    grads["c_lb"] = lbs_vjp(grads["c_lb"])[0]
    grads["final_g"] = grad_final
    for n in SMALL_SHARDED:
        cs = a[n].shape[-1]
        grads[n] = lax.dynamic_slice_in_dim(grads[n], my_shard * cs, cs, axis=2)

    outs = {}
    for n in WEIGHTS:
        if n in BIG:
            outs[n] = stacked[n]
        else:
            outs[n] = (grads[n],) + _adamw(a[n], grads[n], a["m_" + n], a["v_" + n], "adamw_" + n)
    return (loss, grad_x, *[outs[n][0] for n in WEIGHTS], *[outs[n][1] for n in WEIGHTS], *[outs[n][2] for n in WEIGHTS],
            *[outs[n][3] for n in WEIGHTS])
```

```python
import functools

import jax
import jax.numpy as jnp
from jax import lax
from jax.experimental import pallas as pl
from jax.experimental.pallas import tpu as pltpu

F32 = jnp.float32
BF16 = jnp.bfloat16
EPS = 1e-6
HEAD_DIM = 64
N_HEADS = 4
W_GRP = HEAD_DIM * N_HEADS
GMLP_CHUNK = 128
HGRN_CHUNK = 64
HGRN_UNROLL = 8
RGLRU_C = 8.0
POOL_HALO = 16
EXP_CLAMP = 80.0
ADAM_LR, ADAM_B1, ADAM_B2, ADAM_EPS, ADAM_WD, ADAM_STEP = 0.001, 0.9, 0.999, 1e-08, 0.01, 10
VMEM_LIMIT_BYTES = 56 * 1024 * 1024
TILE_PREFS = (1024, 1408, 768, 512, 256, 128)
MM_ROW_TILES = (2048, 1024, 512, 256, 128, 64, 32, 16, 8)
MM_VMEM_BUDGET = 42 * 1024 * 1024
ROW_TILE_PREFS = (512, 256, 128, 64, 32, 16, 8)
MESH_ID = pl.DeviceIdType.MESH
N_DEV = 8


def _pick(n, prefs=TILE_PREFS):
    for p in prefs:
        if n % p == 0:
            return p
    return n


def _cp(*sem):
    return pltpu.CompilerParams(dimension_semantics=sem if sem else None, vmem_limit_bytes=VMEM_LIMIT_BYTES)


def _sds(shape, dtype):
    return jax.ShapeDtypeStruct(tuple(shape), dtype)


_GELU_C = 0.7978845608028654
_GELU_A = 0.044715


def _gelu(x):
    return 0.5 * x * (1.0 + jnp.tanh(_GELU_C * (x + _GELU_A * x * x * x)))


def _gelu_and_grad(x):
    t = jnp.tanh(_GELU_C * (x + _GELU_A * x * x * x))
    g = 0.5 * x * (1.0 + t)
    dg = 0.5 * (1.0 + t) + 0.5 * x * (1.0 - t * t) * _GELU_C * (1.0 + 3.0 * _GELU_A * x * x)
    return g, dg


def _sigmoid(x):
    return 1.0 / (1.0 + jnp.exp(-x))


def _dot(a, b):
    return jnp.dot(a, b, preferred_element_type=F32)


def _dot_nt(a, b):
    return lax.dot_general(a, b, (((1,), (1,)), ((), ())), preferred_element_type=F32)


def _dot_tn(a, b):
    return lax.dot_general(a, b, (((0,), (0,)), ((), ())), preferred_element_type=F32)


def _split3(x):
    hi = x.astype(BF16)
    r1 = x - hi.astype(F32)
    mid = r1.astype(BF16)
    lo = (r1 - mid.astype(F32)).astype(BF16)
    return hi, mid, lo


def _dot_f32_rhs_exact(x, m_bf16):
    hi, mid, lo = _split3(x)
    return _dot(hi, m_bf16) + _dot(mid, m_bf16) + _dot(lo, m_bf16)


def _dot_f32_lhs_exact(m_bf16, x):
    hi, mid, lo = _split3(x)
    return _dot(m_bf16, hi) + _dot(m_bf16, mid) + _dot(m_bf16, lo)


def _head_masks(width=W_GRP):
    lane = lax.broadcasted_iota(jnp.int32, (1, width), 1)
    return [(lane >= h * HEAD_DIM) & (lane < (h + 1) * HEAD_DIM) for h in range(N_HEADS)]


def _block_mask(n=W_GRP):
    r = lax.broadcasted_iota(jnp.int32, (n, n), 0)
    c = lax.broadcasted_iota(jnp.int32, (n, n), 1)
    m = None
    for h in range(N_HEADS):
        mh = (r >= h * HEAD_DIM) & (r < (h + 1) * HEAD_DIM) & (c >= h * HEAD_DIM) & (c < (h + 1) * HEAD_DIM)
        m = mh if m is None else (m | mh)
    return m


def _mm(a, b, mode, *, out_dtype, name, res=None, b_slabs=False, n=None, b_noff=0, b_koff=0,
        out_slabs=0, out_buf=None, out_n=None, o_noff=0, after=(), norm_g=None, rms_bwd=None, tm_max=None):
    after = () if after is None else (tuple(after) if isinstance(after, (tuple, list)) else (after,))
    if mode == "tn":
        k_dim, m_dim = a.shape
    else:
        m_dim, k_dim = a.shape
    if mode == "nt":
        n_dim = b.shape[-2]
    else:
        n_dim = n if n is not None else (b.shape[0] * b.shape[2] if b_slabs else b.shape[1])
    n_total = out_n if out_n is not None else n_dim
    tm, tn, tk = _pick(m_dim), _pick(n_dim), _pick(k_dim)
    if tm_max is not None:
        tm = _pick(m_dim, tuple(p for p in TILE_PREFS if p <= tm_max))
    if b_slabs and mode == "nt":
        tk = _pick(b.shape[2])
    elif b_slabs:
        tn = _pick(b.shape[2])
    elif out_slabs:
        tn = _pick(n_total // out_slabs)
    def vmem_bytes(rows, kk):
        blocks = rows * kk * a.dtype.itemsize + kk * tn * b.dtype.itemsize + rows * tn * jnp.dtype(out_dtype).itemsize
        blocks += rows * tn * 4 * ((res is not None) + 2 * (rms_bwd is not None) + (norm_g is not None))
        return 2 * blocks + rows * tn * 4 * (1 if kk == k_dim else 2)

    row_cap = tm_max if tm_max is not None else (512 if mode == "tn" else MM_ROW_TILES[0])
    k_options = (tk,) if (b_slabs and mode == "nt") or tk == k_dim else (k_dim, tk)
    choice = next(((rows, kk) for kk in k_options for rows in MM_ROW_TILES
                   if rows <= row_cap and m_dim % rows == 0 and rows >= min(256, m_dim) and vmem_bytes(rows, kk) <= MM_VMEM_BUDGET),
                  None)
    if choice is not None:
        tm, tk = choice
    nk = k_dim // tk
    assert b_noff % tn == 0 and b_koff % tk == 0 and o_noff % tn == 0 and n_dim % tn == 0 and k_dim % tk == 0
    bn0, bk0, on0 = b_noff // tn, b_koff // tk, o_noff // tn
    dims = {"nn": (((1,), (0,)), ((), ())), "nt": (((1,), (1,)), ((), ())), "tn": (((0,), (0,)), ((), ()))}[mode]

    if mode == "tn":
        a_spec = pl.BlockSpec((tk, tm), lambda i, j, k: (k, i))
    else:
        a_spec = pl.BlockSpec((tm, tk), lambda i, j, k: (i, k))
    if not b_slabs:
        if mode == "nt":
            b_spec = pl.BlockSpec((tn, tk), lambda i, j, k: (j + bn0, k + bk0))
        else:
            b_spec = pl.BlockSpec((tk, tn), lambda i, j, k: (k + bk0, j + bn0))
    elif mode == "nt":
        bper = b.shape[2] // tk
        b_spec = pl.BlockSpec((None, tn, tk), lambda i, j, k: ((k + bk0) // bper, j, (k + bk0) % bper))
    else:
        bper = b.shape[2] // tn
        b_spec = pl.BlockSpec((None, tk, tn), lambda i, j, k: ((j + bn0) // bper, k, (j + bn0) % bper))
    in_specs = [a_spec, b_spec]
    args = [a, b]
    if res is not None:
        in_specs.append(pl.BlockSpec((tm, tn), lambda i, j, k: (i, j)))
        args.append(res)
    if out_slabs:
        oper = n_total // out_slabs // tn
        out_shape = _sds((out_slabs, m_dim, n_total // out_slabs), out_dtype)
        out_spec = pl.BlockSpec((None, tm, tn), lambda i, j, k: ((j + on0) // oper, i, (j + on0) % oper))
    else:
        out_shape = _sds((m_dim, n_total), out_dtype)
        out_spec = pl.BlockSpec((tm, tn), lambda i, j, k: (i, j + on0))
    out_specs, out_shapes = [out_spec], [out_shape]
    row_spec = pl.BlockSpec((tm, tn), lambda i, j, k: (i, 0))
    vec_spec = pl.BlockSpec((1, tn), lambda i, j, k: (0, 0))
    norm_at = rms_at = None
    if norm_g is not None:
        assert tn == n_dim and not out_slabs
        norm_at = len(args)
        in_specs.append(vec_spec)
        args.append(norm_g.reshape(1, n_dim))
        out_specs.append(row_spec)
        out_shapes.append(_sds((m_dim, n_dim), BF16))
    if rms_bwd is not None:
        assert tn == n_dim and not out_slabs and norm_g is None
        x_in, gain, dres = rms_bwd
        rms_at = len(args)
        in_specs += [row_spec, vec_spec, row_spec]
        args += [x_in, gain.reshape(1, n_dim), dres]
        out_specs, out_shapes = [row_spec, vec_spec], [_sds((m_dim, n_dim), F32), _sds((1, n_dim), F32)]
    aliases = {}
    if out_buf is not None:
        in_specs.append(pl.BlockSpec(memory_space=pl.ANY))
        args.append(out_buf)
        aliases = {len(args) - 1: 0}
    for dep in after:
        in_specs.append(pl.BlockSpec(memory_space=pl.ANY))
        args.append(dep)
    has_res = res is not None
    n_in = len(args)

    def body(*refs):
        a_ref, b_ref = refs[0], refs[1]
        res_ref = refs[2] if has_res else None
        o_ref = refs[n_in]
        acc_ref = refs[-1] if nk > 1 else None
        part = lax.dot_general(a_ref[...].astype(BF16), b_ref[...].astype(BF16), dims, preferred_element_type=F32)

        def finish(v):
            if has_res:
                v = v + res_ref[...]
            if rms_at is not None:
                xv, gv = refs[rms_at][...], refs[rms_at + 1][...]
                r = lax.rsqrt(jnp.mean(xv * xv, axis=-1, keepdims=True) + EPS)
                dyg = v * gv
                dot = jnp.mean(dyg * xv, axis=-1, keepdims=True)
                o_ref[...] = refs[rms_at + 2][...] + r * dyg - xv * (r * r * r) * dot
                dg_ref = refs[n_in + 1]
                gpart = jnp.sum(v * xv * r, axis=0, keepdims=True)
                first = pl.program_id(0) == 0

                @pl.when(first)
                def _():
                    dg_ref[...] = gpart

                @pl.when(jnp.logical_not(first))
                def _():
                    dg_ref[...] += gpart

                return
            o_ref[...] = v.astype(o_ref.dtype)
            if norm_at is not None:
                r = lax.rsqrt(jnp.mean(v * v, axis=-1, keepdims=True) + EPS)
                refs[n_in + 1][...] = (v * r * refs[norm_at][...]).astype(BF16)

        if nk == 1:
            finish(part)
        else:
            kk = pl.program_id(2)

            @pl.when(kk == 0)
            def _():
                acc_ref[...] = part

            @pl.when(kk > 0)
            def _():
                acc_ref[...] += part

            @pl.when(kk == nk - 1)
            def _():
                finish(acc_ref[...])

    outs = pl.pallas_call(
        body, grid=(m_dim // tm, n_dim // tn, nk), in_specs=in_specs, out_specs=out_specs, out_shape=out_shapes,
        scratch_shapes=[pltpu.VMEM((tm, tn), F32)] if nk > 1 else [],
        input_output_aliases=aliases, name=name,
        compiler_params=_cp(*(("arbitrary",) * 3 if rms_bwd is not None else ("parallel", "parallel", "arbitrary"))),
    )(*args)
    return outs[0] if len(outs) == 1 else tuple(outs)


def _rms_fwd(x, g, name):
    t, d = x.shape
    tm = _pick(t, ROW_TILE_PREFS)

    def body(x_ref, g_ref, o_ref):
        xv = x_ref[...]
        r = lax.rsqrt(jnp.mean(xv * xv, axis=-1, keepdims=True) + EPS)
        o_ref[...] = (xv * r * g_ref[...]).astype(o_ref.dtype)

    return pl.pallas_call(
        body, grid=(t // tm,),
        in_specs=[pl.BlockSpec((tm, d), lambda i: (i, 0)), pl.BlockSpec((1, d), lambda i: (0, 0))],
        out_specs=pl.BlockSpec((tm, d), lambda i: (i, 0)), out_shape=_sds((t, d), BF16),
        name=name, compiler_params=_cp("parallel"),
    )(x, g.reshape(1, d))


def _final_loss(x, g, target):
    t, d = x.shape
    tm = _pick(t, ROW_TILE_PREFS)

    def body(x_ref, g_ref, t_ref, dx_ref, dg_ref, loss_ref):
        i = pl.program_id(0)
        xv = x_ref[...]
        gv = g_ref[...]
        r = lax.rsqrt(jnp.mean(xv * xv, axis=-1, keepdims=True) + EPS)
        err = xv * r * gv - t_ref[...]
        lpart = (0.5 / d) * jnp.sum(jnp.sum(err * err, axis=1, keepdims=True), axis=0, keepdims=True)
        dy = err * (1.0 / d)
        dyg = dy * gv
        dot = jnp.mean(dyg * xv, axis=-1, keepdims=True)
        dx_ref[...] = r * dyg - xv * (r * r * r) * dot
        part = jnp.sum(dy * xv * r, axis=0, keepdims=True)

        @pl.when(i == 0)
        def _():
            dg_ref[...] = part
            loss_ref[...] = lpart

        @pl.when(i > 0)
        def _():
            dg_ref[...] += part
            loss_ref[...] += lpart

    row = pl.BlockSpec((tm, d), lambda i: (i, 0))
    vec = pl.BlockSpec((1, d), lambda i: (0, 0))
    return pl.pallas_call(
        body, grid=(t // tm,), in_specs=[row, vec, row], out_specs=[row, vec, pl.BlockSpec((1, 1), lambda i: (0, 0))],
        out_shape=[_sds((t, d), F32), _sds((1, d), F32), _sds((1, 1), F32)], name="final_loss",
        compiler_params=_cp("arbitrary"),
    )(x, g.reshape(1, d), target)


def _shift_down(ext, k, halo):
    return pltpu.roll(ext, k, 0)[halo:]


def _shift_up(ext, k, tm):
    return pltpu.roll(ext, ext.shape[0] - k, 0)[:tm]


def _ffn_tiles(t, f):
    return _pick(t, (256, 128, 64, 32, 16, 8)), _pick(f, (1408, 256, 128))


def _ffn_act_fwd(hf_g, hf_v, conv_w, conv_b):
    t, f = hf_g.shape
    tm, cn = _ffn_tiles(t, f)
    nf = f // cn

    def body(g_ref, v_ref, wg_ref, wv_ref, bg_ref, bv_ref, o_ref, gc_ref, vc_ref, ext_ref, hg_ref, hv_ref):
        i = pl.program_id(1)

        @pl.when(i == 0)
        def _():
            hg_ref[...] = jnp.zeros_like(hg_ref)
            hv_ref[...] = jnp.zeros_like(hv_ref)

        def conv(x_ref, halo_ref, w_ref, b_ref):
            ext_ref[0:8, :] = halo_ref[...]
            ext_ref[8:, :] = x_ref[...]
            halo_ref[...] = x_ref[tm - 8:tm, :]
            ext = ext_ref[...]
            w = w_ref[...]
            return b_ref[...] + w[2:3, :] * ext[8:] + w[1:2, :] * _shift_down(ext, 1, 8) + w[0:1, :] * _shift_down(ext, 2, 8)

        gc = conv(g_ref, hg_ref, wg_ref, bg_ref)
        vc = conv(v_ref, hv_ref, wv_ref, bv_ref)
        o_ref[...] = (_gelu(gc) * vc).astype(o_ref.dtype)
        gc_ref[...] = gc.astype(gc_ref.dtype)
        vc_ref[...] = vc.astype(vc_ref.dtype)

    blk = pl.BlockSpec((tm, cn), lambda j, i: (i, j))
    return pl.pallas_call(
        body, grid=(nf, t // tm),
        in_specs=[blk, blk, pl.BlockSpec((3, cn), lambda j, i: (0, j)), pl.BlockSpec((3, cn), lambda j, i: (0, j + nf)),
                  pl.BlockSpec((1, cn), lambda j, i: (0, j)), pl.BlockSpec((1, cn), lambda j, i: (0, j + nf))],
        out_specs=[blk, blk, blk], out_shape=[_sds((t, f), BF16)] * 3,
        scratch_shapes=[pltpu.VMEM((tm + 8, cn), F32), pltpu.VMEM((8, cn), F32), pltpu.VMEM((8, cn), F32)],
        name="ffn_act_fwd", compiler_params=_cp("parallel", "arbitrary"),
    )(hf_g, hf_v, conv_w, conv_w, conv_b, conv_b)


def _ffn_bwd(dx2, w_down, gc, vc, hf_g, hf_v, conv_w):
    t, f = hf_g.shape
    d = dx2.shape[1]
    tm, cn = _ffn_tiles(t, f)
    nf, nt = f // cn, t // tm

    def body(dx_ref, wd_ref, gc_ref, vc_ref, g_ref, v_ref, wg_ref, wv_ref, dg_ref, dv_ref, sg_ref, sv_ref,
             ext_ref, cg_ref, cv_ref):
        @pl.when(pl.program_id(1) == 0)
        def _():
            for ref in (cg_ref, cv_ref, sg_ref, sv_ref):
                ref[...] = jnp.zeros_like(ref)

        da = _dot_nt(dx_ref[...].astype(BF16), wd_ref[...])
        gel, dgel = _gelu_and_grad(gc_ref[...].astype(F32))
        dgc = da * vc_ref[...].astype(F32) * dgel
        dvc = da * gel

        def back(dc, carry_ref, w, x, out_ref, sums_ref):
            ext_ref[0:tm, :] = dc
            ext_ref[tm:tm + 8, :] = carry_ref[...]
            carry_ref[...] = dc[0:8, :]
            ext = ext_ref[...]
            up1, up2 = _shift_up(ext, 1, tm), _shift_up(ext, 2, tm)
            out_ref[...] = (w[2:3, :] * dc + w[1:2, :] * up1 + w[0:1, :] * up2).astype(out_ref.dtype)
            sums_ref[0:1, :] += jnp.sum(up2 * x, axis=0, keepdims=True)
            sums_ref[1:2, :] += jnp.sum(up1 * x, axis=0, keepdims=True)
            sums_ref[2:3, :] += jnp.sum(dc * x, axis=0, keepdims=True)
            sums_ref[3:4, :] += jnp.sum(dc, axis=0, keepdims=True)

        back(dgc, cg_ref, wg_ref[...], g_ref[...], dg_ref, sg_ref)
        back(dvc, cv_ref, wv_ref[...], v_ref[...], dv_ref, sv_ref)

    blk = pl.BlockSpec((tm, cn), lambda j, i: (nt - 1 - i, j))
    sums = pl.BlockSpec((8, cn), lambda j, i: (0, j))
    return pl.pallas_call(
        body, grid=(nf, nt),
        in_specs=[pl.BlockSpec((tm, d), lambda j, i: (nt - 1 - i, 0)), pl.BlockSpec((cn, d), lambda j, i: (j, 0)),
                  blk, blk, blk, blk, pl.BlockSpec((3, cn), lambda j, i: (0, j)), pl.BlockSpec((3, cn), lambda j, i: (0, j + nf))],
        out_specs=[blk, blk, sums, sums],
        out_shape=[_sds((t, f), BF16), _sds((t, f), BF16), _sds((8, f), F32), _sds((8, f), F32)],
        scratch_shapes=[pltpu.VMEM((tm + 8, cn), F32), pltpu.VMEM((8, cn), F32), pltpu.VMEM((8, cn), F32)],
        name="ffn_bwd", compiler_params=_cp("parallel", "arbitrary"),
    )(dx2, w_down, gc, vc, hf_g, hf_v, conv_w, conv_w)


def _ple_fwd(x2, pe, pre, next_g=None):
    t, d = x2.shape
    tm = _pick(t, ROW_TILE_PREFS)

    def body(x_ref, pe_ref, pre_ref, *rest):
        x3 = x_ref[...] + pe_ref[...] * _sigmoid(pre_ref[...])
        if next_g is None:
            rest[0][...] = x3
        else:
            g_ref, o_ref, h_ref = rest
            o_ref[...] = x3
            r = lax.rsqrt(jnp.mean(x3 * x3, axis=-1, keepdims=True) + EPS)
            h_ref[...] = (x3 * r * g_ref[...]).astype(h_ref.dtype)

    row = pl.BlockSpec((tm, d), lambda i: (i, 0))
    if next_g is None:
        return pl.pallas_call(body, grid=(t // tm,), in_specs=[row, row, row], out_specs=row,
                              out_shape=_sds((t, d), F32), name="ple_fwd", compiler_params=_cp("parallel"))(x2, pe, pre), None
    return pl.pallas_call(body, grid=(t // tm,), in_specs=[row, row, row, pl.BlockSpec((1, d), lambda i: (0, 0))],
                          out_specs=[row, row], out_shape=[_sds((t, d), F32), _sds((t, d), BF16)], name="ple_norm_fwd",
                          compiler_params=_cp("parallel"))(x2, pe, pre, next_g.reshape(1, d))


def _ple_bwd(dx3, pe, pre, after=None):
    t, d = dx3.shape
    tm = _pick(t, ROW_TILE_PREFS)

    def body(dx_ref, pe_ref, pre_ref, *rest):
        dpe_ref, dpre_ref = rest[-2:]
        gate = _sigmoid(pre_ref[...])
        dx = dx_ref[...]
        dpe_ref[...] = (dx * gate).astype(dpe_ref.dtype)
        dpre_ref[...] = (dx * pe_ref[...] * gate * (1.0 - gate)).astype(dpre_ref.dtype)

    row = pl.BlockSpec((tm, d), lambda i: (i, 0))
    extra = [] if after is None else [after]
    return pl.pallas_call(body, grid=(t // tm,), in_specs=[row, row, row] + [pl.BlockSpec(memory_space=pl.ANY)] * len(extra),
                          out_specs=[row, row], out_shape=[_sds((t, d), BF16), _sds((t, d), BF16)], name="ple_bwd",
                          compiler_params=_cp("parallel"))(dx3, pe, pre, *extra)


def _mix_tm(t):
    return _pick(t, (512, 256, 128))


def _zblk(tm, col, rev_nt=None):
    if rev_nt is None:
        return pl.BlockSpec((tm, W_GRP), lambda i: (i, col))
    return pl.BlockSpec((tm, W_GRP), lambda i: (rev_nt - 1 - i, col))


def _full(shape):
    nd = len(shape)
    return pl.BlockSpec(tuple(shape), lambda i: (0,) * nd)


def _gmlp_sv(wm_ref, vnc, bs, hm):
    sv = bs
    for h in range(N_HEADS):
        sv = sv + jnp.where(hm[h], _dot(wm_ref[h], vnc), 0.0)
    return sv


def _layernorm(v, g, b):
    mu = jnp.mean(v, axis=-1, keepdims=True)
    vc = v - mu
    rs = lax.rsqrt(jnp.mean(vc * vc, axis=-1, keepdims=True) + EPS)
    xhat = vc * rs
    return xhat, rs, xhat * g + b


def _mix_a_fwd(z, d_mix, ln_g, ln_b, wm, bs_t):
    t = z.shape[0]
    tm = _mix_tm(t)

    def body(u_ref, v_ref, g_ref, b_ref, wm_ref, bs_ref, o_ref):
        hm = _head_masks()
        ug = _gelu(u_ref[...])
        _, _, vn = _layernorm(_gelu(v_ref[...]), g_ref[...], b_ref[...])
        vnb = vn.astype(BF16)
        for n in range(tm // GMLP_CHUNK):
            sl = slice(n * GMLP_CHUNK, (n + 1) * GMLP_CHUNK)
            o_ref[sl, :] = ug[sl] * _gmlp_sv(wm_ref, vnb[sl], bs_ref[...], hm)

    return pl.pallas_call(
        body, grid=(t // tm,),
        in_specs=[_zblk(tm, 0), _zblk(tm, 1), _full((1, W_GRP)), _full((1, W_GRP)), _full(wm.shape), _full(bs_t.shape)],
        out_specs=_zblk(tm, 0), out_shape=_sds((t, d_mix), F32), name="mix_a_fwd", compiler_params=_cp("parallel"),
    )(z, z, ln_g, ln_b, wm, bs_t)


def _mix_a_bwd(z, dmix, ln_g, ln_b, wm, wm_t, bs_t):
    t, zc = z.shape
    tm = _mix_tm(t)

    def body(u_ref, v_ref, dy_ref, g_ref, b_ref, wm_ref, wmt_ref, bs_ref, dz_ref, dg_ref, db_ref, dws_ref, dbs_ref):
        i = pl.program_id(0)

        @pl.when(i == 0)
        def _():
            dg_ref[...] = jnp.zeros_like(dg_ref)
            db_ref[...] = jnp.zeros_like(db_ref)
            dws_ref[...] = jnp.zeros_like(dws_ref)
            dbs_ref[...] = jnp.zeros_like(dbs_ref)

        hm = _head_masks()
        ug, dug = _gelu_and_grad(u_ref[...])
        vg, dvg = _gelu_and_grad(v_ref[...])
        gv = g_ref[...]
        xhat, rs, vn = _layernorm(vg, gv, b_ref[...])
        vnb = vn.astype(BF16)
        dy = dy_ref[...]
        for n in range(tm // GMLP_CHUNK):
            sl = slice(n * GMLP_CHUNK, (n + 1) * GMLP_CHUNK)
            vnc = vnb[sl]
            sv = _gmlp_sv(wm_ref, vnc, bs_ref[...], hm)
            dsv = dy[sl] * ug[sl]
            dz_ref[sl, 0:W_GRP] = dy[sl] * sv * dug[sl]
            dbs_ref[...] += dsv
            dsvb = dsv.astype(BF16)
            dvn = jnp.zeros((GMLP_CHUNK, W_GRP), F32)
            for h in range(N_HEADS):
                dws_ref[h] += _dot_nt(jnp.where(hm[h], dsv, 0.0).astype(BF16), vnc)
                dvn = dvn + jnp.where(hm[h], _dot(wmt_ref[h], dsvb), 0.0)
            xh = xhat[sl]
            dg_ref[...] += jnp.sum(dvn * xh, axis=0, keepdims=True)
            db_ref[...] += jnp.sum(dvn, axis=0, keepdims=True)
            dxh = dvn * gv
            dvg_c = rs[sl] * (dxh - jnp.mean(dxh, axis=-1, keepdims=True) - xh * jnp.mean(dxh * xh, axis=-1, keepdims=True))
            dz_ref[sl, W_GRP:2 * W_GRP] = dvg_c * dvg[sl]

    return pl.pallas_call(
        body, grid=(t // tm,),
        in_specs=[_zblk(tm, 0), _zblk(tm, 1), _zblk(tm, 0), _full((1, W_GRP)), _full((1, W_GRP)), _full(wm.shape),
                  _full(wm_t.shape), _full(bs_t.shape)],
        out_specs=[pl.BlockSpec((tm, 2 * W_GRP), lambda i: (i, 0)), _full((1, W_GRP)), _full((1, W_GRP)),
                   _full(wm.shape), _full(bs_t.shape)],
        out_shape=[_sds((t, zc), F32), _sds((1, W_GRP), F32), _sds((1, W_GRP), F32), _sds(wm.shape, F32),
                   _sds(bs_t.shape, F32)],
        name="mix_a_bwd", compiler_params=_cp("arbitrary"),
    )(z, z, dmix, ln_g, ln_b, wm, wm_t, bs_t)


def _softplus(x):
    return jnp.maximum(x, 0.0) + jnp.log(1.0 + jnp.exp(-jnp.abs(x)))


def _neg_expm1(x):
    series = -x * (1.0 + x * 0.5 * (1.0 + x * (1.0 / 3.0) * (1.0 + x * 0.25 * (1.0 + x * 0.2))))
    return jnp.where(x > -0.1, series, 1.0 - jnp.exp(x))


def _rglru_gates(ext_ref, x_ref, halo, cw, cb, wa_ref, wx_ref, ba, bx, lam):
    ext_ref[0:8, :] = halo
    ext_ref[8:, :] = x_ref[...]
    ext = ext_ref[...]
    x0, x1, x2, x3 = ext[8:], _shift_down(ext, 1, 8), _shift_down(ext, 2, 8), _shift_down(ext, 3, 8)
    xc = cb + cw[3:4, :] * x0 + cw[2:3, :] * x1 + cw[1:2, :] * x2 + cw[0:1, :] * x3
    xcb = xc.astype(BF16)
    r = _sigmoid(_dot(xcb, wa_ref[...]) + ba)
    ig = _sigmoid(_dot(xcb, wx_ref[...]) + bx)
    sp = _softplus(-lam)
    la = -RGLRU_C * r * sp
    a = jnp.exp(la)
    mult = jnp.sqrt(_neg_expm1(2.0 * la))
    return (x0, x1, x2, x3), xc, r, ig, sp, a, mult


def _mix_b_fwd(z, mix, conv_w, conv_b, wa, wx, ba, bx, lam):
    t = z.shape[0]
    tm = _mix_tm(t)

    def body(x_ref, gb_ref, cw_ref, cb_ref, wa_ref, wx_ref, ba_ref, bx_ref, lam_ref, mix_in, o_ref, hs_ref,
             ext_ref, a_ref, b_ref, xh_ref, hc_ref):
        i = pl.program_id(0)

        @pl.when(i == 0)
        def _():
            xh_ref[...] = jnp.zeros_like(xh_ref)
            hc_ref[...] = jnp.zeros_like(hc_ref)

        _, xc, _, ig, _, a, mult = _rglru_gates(ext_ref, x_ref, xh_ref[...], cw_ref[...], cb_ref[...], wa_ref, wx_ref,
                                                ba_ref[...], bx_ref[...], lam_ref[...])
        xh_ref[...] = x_ref[tm - 8:tm, :]
        a_ref[...] = a
        b_ref[...] = mult * (ig * xc)
        rid = lax.broadcasted_iota(jnp.int32, (8, W_GRP), 0)

        def group(gi, hprev):
            base = pl.multiple_of(gi * 8, 8)
            ca = a_ref[pl.ds(base, 8), :]
            cb = b_ref[pl.ds(base, 8), :]
            for k in (1, 2, 4):
                m = rid >= k
                cb = jnp.where(m, ca * pltpu.roll(cb, k, 0) + cb, cb)
                ca = jnp.where(m, ca * pltpu.roll(ca, k, 0), ca)
            hh = cb + ca * hprev
            hs_ref[pl.ds(base, 8), :] = hh
            return hh[7:8, :]

        hlast = lax.fori_loop(0, tm // 8, group, hc_ref[0:1, :])
        hc_ref[...] = jnp.broadcast_to(hlast, hc_ref.shape)
        o_ref[...] = hs_ref[...] * _gelu(gb_ref[...])

    sq = _full((W_GRP, W_GRP))
    vec = _full((1, W_GRP))
    return pl.pallas_call(
        body, grid=(t // tm,),
        in_specs=[_zblk(tm, 2), _zblk(tm, 3), _full((4, W_GRP)), vec, sq, sq, vec, vec, vec, pl.BlockSpec(memory_space=pl.ANY)],
        out_specs=[_zblk(tm, 1), pl.BlockSpec((tm, W_GRP), lambda i: (i, 0))],
        out_shape=[_sds(mix.shape, F32), _sds((t, W_GRP), F32)],
        scratch_shapes=[pltpu.VMEM((tm + 8, W_GRP), F32), pltpu.VMEM((tm, W_GRP), F32), pltpu.VMEM((tm, W_GRP), F32),
                        pltpu.VMEM((8, W_GRP), F32), pltpu.VMEM((8, W_GRP), F32)],
        input_output_aliases={9: 0}, name="mix_b_fwd", compiler_params=_cp("arbitrary"),
    )(z, z, conv_w, conv_b, wa, wx, ba, bx, lam, mix)


def _mix_b_bwd(z, dz, dmix, hs, conv_w, conv_b, wa, wx, wa_t, wx_t, ba, bx, lam):
    t = z.shape[0]
    tm = _mix_tm(t)
    nt = t // tm
    hb = tm // 8

    def body(x_ref, gb_ref, xhalo_ref, hs_ref, hhalo_ref, dy_ref, cw_ref, cb_ref, wa_ref, wx_ref, wat_ref, wxt_ref,
             ba_ref, bx_ref, lam_ref, dz_in, dz_ref, dcw_ref, dcb_ref, dwa_ref, dwx_ref, dba_ref, dbx_ref, dlam_ref,
             ext_ref, c_ref, d_ref, g_ref, an_ref, gn_ref, dxn_ref):
        i = pl.program_id(0)
        first_tile = i == nt - 1

        @pl.when(i == 0)
        def _():
            for ref in (dcw_ref, dcb_ref, dwa_ref, dwx_ref, dba_ref, dbx_ref, dlam_ref, an_ref, gn_ref, dxn_ref):
                ref[...] = jnp.zeros_like(ref)

        cw, lam = cw_ref[...], lam_ref[...]
        xhalo = jnp.where(first_tile, 0.0, xhalo_ref[...])
        (x0, x1, x2, x3), xc, r, ig, sp, a, mult = _rglru_gates(
            ext_ref, x_ref, xhalo, cw, cb_ref[...], wa_ref, wx_ref, ba_ref[...], bx_ref[...], lam)
        hs = hs_ref[...]
        dy = dy_ref[...]
        gel, dgel = _gelu_and_grad(gb_ref[...])
        dz_ref[:, W_GRP:2 * W_GRP] = dy * hs * dgel

        ext_ref[0:tm, :] = a
        ext_ref[tm:tm + 8, :] = an_ref[...]
        an_ref[...] = a[0:8, :]
        c_ref[...] = _shift_up(ext_ref[...], 1, tm)
        d_ref[...] = dy * gel
        rid = lax.broadcasted_iota(jnp.int32, (8, W_GRP), 0)

        def group(j, gnext):
            base = pl.multiple_of((tm // 8 - 1 - j) * 8, 8)
            cc = c_ref[pl.ds(base, 8), :]
            cd = d_ref[pl.ds(base, 8), :]
            for k in (1, 2, 4):
                m = rid < 8 - k
                cd = jnp.where(m, cc * pltpu.roll(cd, 8 - k, 0) + cd, cd)
                cc = jnp.where(m, cc * pltpu.roll(cc, 8 - k, 0), cc)
            gg = cd + cc * gnext
            g_ref[pl.ds(base, 8), :] = gg
            return gg[0:1, :]

        gfirst = lax.fori_loop(0, tm // 8, group, gn_ref[0:1, :])
        gn_ref[...] = jnp.broadcast_to(gfirst, gn_ref.shape)
        g = g_ref[...]

        ext_ref[0:8, :] = jnp.where(first_tile, 0.0, hhalo_ref[...])
        ext_ref[8:, :] = hs
        hprev = _shift_down(ext_ref[...], 1, 8)
        da = g * hprev
        dmult = g * (ig * xc)
        di = g * mult * xc
        dxc = g * mult * ig
        dla = da * a - dmult * a * a / mult
        dr = dla * (-RGLRU_C * sp)
        dlam_ref[...] += jnp.sum(dla * (-RGLRU_C * r), axis=0, keepdims=True) * (-_sigmoid(-lam))
        dpr = dr * r * (1.0 - r)
        dpi = di * ig * (1.0 - ig)
        dprb, dpib, xcb = dpr.astype(BF16), dpi.astype(BF16), xc.astype(BF16)
        dba_ref[...] += jnp.sum(dpr, axis=0, keepdims=True)
        dbx_ref[...] += jnp.sum(dpi, axis=0, keepdims=True)
        dwa_ref[...] += _dot_tn(xcb, dprb)
        dwx_ref[...] += _dot_tn(xcb, dpib)
        dxc = dxc + _dot(dprb, wat_ref[...]) + _dot(dpib, wxt_ref[...])
        dcb_ref[...] += jnp.sum(dxc, axis=0, keepdims=True)
        dcw_ref[3:4, :] += jnp.sum(dxc * x0, axis=0, keepdims=True)
        dcw_ref[2:3, :] += jnp.sum(dxc * x1, axis=0, keepdims=True)
        dcw_ref[1:2, :] += jnp.sum(dxc * x2, axis=0, keepdims=True)
        dcw_ref[0:1, :] += jnp.sum(dxc * x3, axis=0, keepdims=True)
        ext_ref[0:tm, :] = dxc
        ext_ref[tm:tm + 8, :] = dxn_ref[...]
        dxn_ref[...] = dxc[0:8, :]
        ext = ext_ref[...]
        dz_ref[:, 0:W_GRP] = (cw[3:4, :] * dxc + cw[2:3, :] * _shift_up(ext, 1, tm) + cw[1:2, :] * _shift_up(ext, 2, tm)
                              + cw[0:1, :] * _shift_up(ext, 3, tm))

    sq = _full((W_GRP, W_GRP))
    vec = _full((1, W_GRP))
    halo = lambda col: pl.BlockSpec((8, W_GRP), lambda i: (jnp.maximum((nt - 1 - i) * hb - 1, 0), col))
    rev = lambda col: _zblk(tm, col, nt)
    return pl.pallas_call(
        body, grid=(nt,),
        in_specs=[rev(2), rev(3), halo(2), rev(0), halo(0), rev(1), _full((4, W_GRP)), vec, sq, sq, sq, sq, vec, vec, vec,
                  pl.BlockSpec(memory_space=pl.ANY)],
        out_specs=[pl.BlockSpec((tm, 2 * W_GRP), lambda i: (nt - 1 - i, 1)), _full((4, W_GRP)), vec, sq, sq, vec, vec, vec],
        out_shape=[_sds(dz.shape, F32), _sds((4, W_GRP), F32), _sds((1, W_GRP), F32), _sds((W_GRP, W_GRP), F32),
                   _sds((W_GRP, W_GRP), F32), _sds((1, W_GRP), F32), _sds((1, W_GRP), F32), _sds((1, W_GRP), F32)],
        scratch_shapes=[pltpu.VMEM((tm + 8, W_GRP), F32), pltpu.VMEM((tm, W_GRP), F32), pltpu.VMEM((tm, W_GRP), F32),
                        pltpu.VMEM((tm, W_GRP), F32), pltpu.VMEM((8, W_GRP), F32), pltpu.VMEM((8, W_GRP), F32),
                        pltpu.VMEM((8, W_GRP), F32)],
        input_output_aliases={15: 0}, name="mix_b_bwd", compiler_params=_cp("arbitrary"),
    )(z, z, z, hs, hs, dmix, conv_w, conv_b, wa, wx, wa_t, wx_t, ba, bx, lam, dz)


def _tri(n, lower):
    r = lax.broadcasted_iota(jnp.int32, (n, n), 0)
    c = lax.broadcasted_iota(jnp.int32, (n, n), 1)
    return jnp.where((r >= c) if lower else (r <= c), 1.0, 0.0).astype(BF16)


def _causal_stack():
    r = lax.broadcasted_iota(jnp.int32, (N_HEADS * HGRN_CHUNK, HGRN_CHUNK), 0)
    c = lax.broadcasted_iota(jnp.int32, (N_HEADS * HGRN_CHUNK, HGRN_CHUNK), 1)
    m = None
    for h in range(N_HEADS):
        mh = (r >= h * HGRN_CHUNK) & (r < (h + 1) * HGRN_CHUNK) & (r - h * HGRN_CHUNK >= c)
        m = mh if m is None else (m | mh)
    return m


def _stack_heads(x, hm):
    return jnp.concatenate([jnp.where(hm[h], x, 0.0) for h in range(N_HEADS)], axis=0)


def _unstack_heads(xs, hm):
    out = jnp.where(hm[0], xs[0:HGRN_CHUNK], 0.0)
    for h in range(1, N_HEADS):
        out = out + jnp.where(hm[h], xs[h * HGRN_CHUNK:(h + 1) * HGRN_CHUNK], 0.0)
    return out


def _hgrn_chunk(qv, fv, lb, tril):
    sq = _sigmoid(qv)
    qq = qv * sq
    sg = _sigmoid(fv)
    fg = lb + (1.0 - lb) * sg
    kk = 1.0 - fg
    bb = _dot_f32_lhs_exact(tril, jnp.log(fg))
    b_last = bb[HGRN_CHUNK - 1:HGRN_CHUNK, :]
    b_mid = bb[HGRN_CHUNK // 2 - 1:HGRN_CHUNK // 2, :]
    eq = jnp.exp(jnp.minimum(bb - b_mid, EXP_CLAMP))
    ek = jnp.exp(jnp.minimum(b_mid - bb, EXP_CLAMP))
    eb = jnp.exp(bb)
    el = jnp.exp(b_last - bb)
    return sq, qq, sg, fg, kk, b_last, eq, ek, eb, el


def _seg_mean(x, avg):
    return _dot_f32_rhs_exact(x, avg)


def _mix_c_fwd(z, mix, lb, ng):
    t = z.shape[0]
    tm = _mix_tm(t)
    nch = tm // HGRN_CHUNK

    def body(q_ref, f_ref, i_ref, g_ref, lb_ref, ng_ref, mix_in, y_ref, o_ref, ss_ref, s_ref):
        @pl.when(pl.program_id(0) == 0)
        def _():
            s_ref[...] = jnp.zeros_like(s_ref)

        hm = _head_masks()
        bmask = _block_mask()
        causal = _causal_stack()
        tril = _tri(HGRN_CHUNK, True)
        avg = jnp.where(bmask, 1.0 / HEAD_DIM, 0.0).astype(BF16)
        lb, ng = lb_ref[...], ng_ref[...]

        def chunk(c, carry):
            rows = pl.ds(pl.multiple_of(c * HGRN_CHUNK, HGRN_CHUNK), HGRN_CHUNK)
            vv = i_ref[rows, :]
            gv = g_ref[rows, :]
            _, qq, _, _, kk, b_last, eq, ek, eb, el = _hgrn_chunk(q_ref[rows, :], f_ref[rows, :], lb, tril)
            vb = vv.astype(BF16)
            qs = _stack_heads(qq * eq, hm).astype(BF16)
            att = jnp.where(causal, _dot_nt(qs, (kk * ek).astype(BF16)), 0.0)
            o = _unstack_heads(_dot(att.astype(BF16), vb), hm)
            s0 = s_ref[...]
            ss_ref[c] = s0
            o = o + _dot_nt((qq * eb).astype(BF16), s0.astype(BF16))
            s_ref[...] = s0 * jnp.exp(b_last) + jnp.where(bmask, _dot_tn(vb, (kk * el).astype(BF16)), 0.0)
            o_ref[rows, :] = o
            rstd = lax.rsqrt(_seg_mean(o * o, avg) + EPS)
            y_ref[rows, :] = o * rstd * ng * (gv * _sigmoid(gv))
            return carry

        lax.fori_loop(0, nch, chunk, 0, unroll=HGRN_UNROLL)

    vec = _full((1, W_GRP))
    return pl.pallas_call(
        body, grid=(t // tm,),
        in_specs=[_zblk(tm, 4), _zblk(tm, 5), _zblk(tm, 6), _zblk(tm, 7), vec, vec, pl.BlockSpec(memory_space=pl.ANY)],
        out_specs=[_zblk(tm, 2), pl.BlockSpec((tm, W_GRP), lambda i: (i, 0)),
                   pl.BlockSpec((nch, W_GRP, W_GRP), lambda i: (i, 0, 0))],
        out_shape=[_sds(mix.shape, F32), _sds((t, W_GRP), F32), _sds((t // HGRN_CHUNK, W_GRP, W_GRP), F32)],
        scratch_shapes=[pltpu.VMEM((W_GRP, W_GRP), F32)],
        input_output_aliases={6: 0}, name="mix_c_fwd", compiler_params=_cp("arbitrary"),
    )(z, z, z, z, lb, ng, mix)


def _mix_c_bwd(z, dz, dmix, o_pre, states, lb, ng):
    t = z.shape[0]
    tm = _mix_tm(t)
    nt = t // tm
    nch = tm // HGRN_CHUNK

    def body(q_ref, f_ref, i_ref, g_ref, o_ref, ss_ref, dy_ref, lb_ref, ng_ref, dz_in, dz_ref, dlb_ref, dng_ref, ds_ref):
        @pl.when(pl.program_id(0) == 0)
        def _():
            ds_ref[...] = jnp.zeros_like(ds_ref)
            dlb_ref[...] = jnp.zeros_like(dlb_ref)
            dng_ref[...] = jnp.zeros_like(dng_ref)

        hm = _head_masks()
        bmask = _block_mask()
        causal = _causal_stack()
        tril = _tri(HGRN_CHUNK, True)
        triu = _tri(HGRN_CHUNK, False)
        avg = jnp.where(bmask, 1.0 / HEAD_DIM, 0.0).astype(BF16)
        lb, ng = lb_ref[...], ng_ref[...]
        last_row = lax.broadcasted_iota(jnp.int32, (HGRN_CHUNK, W_GRP), 0) == HGRN_CHUNK - 1

        def chunk(j, carry):
            c = nch - 1 - j
            rows = pl.ds(pl.multiple_of(c * HGRN_CHUNK, HGRN_CHUNK), HGRN_CHUNK)
            qv, gv, vv = q_ref[rows, :], g_ref[rows, :], i_ref[rows, :]
            sq, qq, sg, fg, kk, b_last, eq, ek, eb, el = _hgrn_chunk(qv, f_ref[rows, :], lb, tril)
            s0 = ss_ref[c]
            ds1 = ds_ref[...]
            o = o_ref[rows, :]
            dy = dy_ref[rows, :]
            rstd = lax.rsqrt(_seg_mean(o * o, avg) + EPS)
            oh = o * rstd
            sgg = _sigmoid(gv)
            dz_ref[rows, 3 * W_GRP:4 * W_GRP] = dy * oh * ng * (sgg * (1.0 + gv * (1.0 - sgg)))
            don = dy * gv * sgg
            dng_ref[...] += jnp.sum(don * oh, axis=0, keepdims=True)
            doh = don * ng
            do = rstd * (doh - oh * _seg_mean(doh * oh, avg))
            qt, kt, qh, kh = qq * eq, kk * ek, qq * eb, kk * el
            vb, dob = vv.astype(BF16), do.astype(BF16)
            ktb, khb = kt.astype(BF16), kh.astype(BF16)
            ds1b = ds1.astype(BF16)
            qs = _stack_heads(qt, hm).astype(BF16)
            dos = _stack_heads(do, hm).astype(BF16)
            att = jnp.where(causal, _dot_nt(qs, ktb), 0.0).astype(BF16)
            datt = jnp.where(causal, _dot_nt(dos, vb), 0.0).astype(BF16)
            dv = _dot_tn(att, dos) + _dot_nt(khb, ds1b)
            dqt = _unstack_heads(_dot(datt, ktb), hm)
            dkt = _dot_tn(datt, qs)
            dqh = _dot(dob, s0.astype(BF16))
            dkh = _dot(vb, ds1b)
            e_last = jnp.exp(b_last)
            ds_ref[...] = ds1 * e_last + jnp.where(bmask, _dot_tn(dob, qh.astype(BF16)), 0.0)
            dq = dqt * eq + dqh * eb
            dk = dkt * ek + dkh * el
            db = qt * dqt - kt * dkt + qh * dqh - kh * dkh
            db_last = jnp.sum(kh * dkh, axis=0, keepdims=True) + e_last * jnp.sum(ds1 * s0, axis=0, keepdims=True)
            db = db + jnp.where(last_row, db_last, 0.0)
            dlogf = _dot_f32_lhs_exact(triu, db)
            dfg = dlogf / fg - dk
            dz_ref[rows, W_GRP:2 * W_GRP] = dfg * (1.0 - lb) * sg * (1.0 - sg)
            dlb_ref[...] += jnp.sum(dfg * (1.0 - sg), axis=0, keepdims=True)
            dz_ref[rows, 0:W_GRP] = dq * (sq * (1.0 + qv * (1.0 - sq)))
            dz_ref[rows, 2 * W_GRP:3 * W_GRP] = dv
            return carry

        lax.fori_loop(0, nch, chunk, 0, unroll=HGRN_UNROLL)

    vec = _full((1, W_GRP))
    rev = lambda col: _zblk(tm, col, nt)
    return pl.pallas_call(
        body, grid=(nt,),
        in_specs=[rev(4), rev(5), rev(6), rev(7), rev(0), pl.BlockSpec((nch, W_GRP, W_GRP), lambda i: (nt - 1 - i, 0, 0)),
                  rev(2), vec, vec, pl.BlockSpec(memory_space=pl.ANY)],
        out_specs=[pl.BlockSpec((tm, 4 * W_GRP), lambda i: (nt - 1 - i, 1)), vec, vec],
        out_shape=[_sds(dz.shape, F32), _sds((1, W_GRP), F32), _sds((1, W_GRP), F32)],
        scratch_shapes=[pltpu.VMEM((W_GRP, W_GRP), F32)],
        input_output_aliases={9: 0}, name="mix_c_bwd", compiler_params=_cp("arbitrary"),
    )(z, z, z, z, o_pre, states, dmix, lb, ng, dz)


def _pool_select(hm, s2, s4, s8, s16):
    return jnp.where(hm[0], s2, jnp.where(hm[1], s4, jnp.where(hm[2], s8, s16)))


def _pool_counts(hm, row0, tm):
    pos = (row0 + 1 + lax.broadcasted_iota(jnp.int32, (tm, W_GRP), 0)).astype(F32)
    win = _pool_select(hm, 2.0, 4.0, 8.0, 16.0)
    return jnp.minimum(pos, win)


def _pooled(ext_ref, x, halo, hm, cnt):
    ext_ref[0:POOL_HALO, :] = halo
    ext_ref[POOL_HALO:, :] = x
    e = ext_ref[...]
    s2 = e + pltpu.roll(e, 1, 0)
    s4 = s2 + pltpu.roll(s2, 2, 0)
    s8 = s4 + pltpu.roll(s4, 4, 0)
    s16 = s8 + pltpu.roll(s8, 8, 0)
    return _pool_select(hm, s2, s4, s8, s16)[POOL_HALO:] / cnt - x


def _mix_d_fwd(z, mix, wd, scale):
    t = z.shape[0]
    tm = _mix_tm(t)

    def body(x_ref, wd_ref, sc_ref, mix_in, o_ref, ext_ref, halo_ref):
        i = pl.program_id(0)

        @pl.when(i == 0)
        def _():
            halo_ref[...] = jnp.zeros_like(halo_ref)

        hm = _head_masks()
        x = x_ref[...]
        pooled = _pooled(ext_ref, x, halo_ref[...], hm, _pool_counts(hm, i * tm, tm))
        halo_ref[...] = x_ref[tm - POOL_HALO:tm, :]
        o_ref[...] = _dot(pooled.astype(BF16), wd_ref[...]) * sc_ref[...]

    return pl.pallas_call(
        body, grid=(t // tm,),
        in_specs=[_zblk(tm, 8), _full((W_GRP, W_GRP)), _full((1, W_GRP)), pl.BlockSpec(memory_space=pl.ANY)],
        out_specs=_zblk(tm, 3), out_shape=_sds(mix.shape, F32),
        scratch_shapes=[pltpu.VMEM((tm + POOL_HALO, W_GRP), F32), pltpu.VMEM((POOL_HALO, W_GRP), F32)],
        input_output_aliases={3: 0}, name="mix_d_fwd", compiler_params=_cp("arbitrary"),
    )(z, wd, scale, mix)


def _mix_d_bwd(z, dz, dmix, wd, wd_t, scale):
    t = z.shape[0]
    tm = _mix_tm(t)
    nt = t // tm
    hb = tm // POOL_HALO

    def body(x_ref, xhalo_ref, dy_ref, wd_ref, wdt_ref, sc_ref, dz_in, dz_ref, dwd_ref, dsc_ref, ext_ref, en_ref):
        i = pl.program_id(0)
        ri = nt - 1 - i

        @pl.when(i == 0)
        def _():
            en_ref[...] = jnp.zeros_like(en_ref)
            dwd_ref[...] = jnp.zeros_like(dwd_ref)
            dsc_ref[...] = jnp.zeros_like(dsc_ref)

        hm = _head_masks()
        cnt = _pool_counts(hm, ri * tm, tm)
        x = x_ref[...]
        pooled = _pooled(ext_ref, x, jnp.where(ri == 0, 0.0, xhalo_ref[...]), hm, cnt)
        pb = pooled.astype(BF16)
        dy = dy_ref[...]
        dsc_ref[...] += jnp.sum(dy * _dot(pb, wd_ref[...]), axis=0, keepdims=True)
        dyw = (dy * sc_ref[...]).astype(BF16)
        dwd_ref[...] += _dot_tn(pb, dyw)
        dpool = _dot(dyw, wdt_ref[...])
        e = dpool / cnt
        ext_ref[0:tm, :] = e
        ext_ref[tm:, :] = en_ref[...]
        en_ref[...] = e[0:POOL_HALO, :]
        ee = ext_ref[...]
        n = tm + POOL_HALO
        r2 = ee + pltpu.roll(ee, n - 1, 0)
        r4 = r2 + pltpu.roll(r2, n - 2, 0)
        r8 = r4 + pltpu.roll(r4, n - 4, 0)
        r16 = r8 + pltpu.roll(r8, n - 8, 0)
        dz_ref[...] = _pool_select(hm, r2, r4, r8, r16)[:tm] - dpool

    sq = _full((W_GRP, W_GRP))
    vec = _full((1, W_GRP))
    return pl.pallas_call(
        body, grid=(nt,),
        in_specs=[_zblk(tm, 8, nt), pl.BlockSpec((POOL_HALO, W_GRP), lambda i: (jnp.maximum((nt - 1 - i) * hb - 1, 0), 8)),
                  _zblk(tm, 3, nt), sq, sq, vec, pl.BlockSpec(memory_space=pl.ANY)],
        out_specs=[_zblk(tm, 8, nt), sq, vec],
        out_shape=[_sds(dz.shape, F32), _sds((W_GRP, W_GRP), F32), _sds((1, W_GRP), F32)],
        scratch_shapes=[pltpu.VMEM((tm + POOL_HALO, W_GRP), F32), pltpu.VMEM((POOL_HALO, W_GRP), F32)],
        input_output_aliases={6: 0}, name="mix_d_bwd", compiler_params=_cp("arbitrary"),
    )(z, z, dmix, wd, wd_t, scale, dz)


def _as2d(a):
    if a.ndim == 1:
        return a.reshape(1, a.shape[0])
    return a.reshape(-1, a.shape[-1])


def _adamw(w, g, m, v, name):
    shape = w.shape
    w2, g2, m2, v2 = _as2d(w), _as2d(g), _as2d(m), _as2d(v)
    rows, cols = w2.shape
    tr = _pick(rows, (1024, 512, 256, 128, 64, 32, 16, 8))
    if tr * cols * 4 * 14 > VMEM_LIMIT_BYTES:
        tr = _pick(rows, (256, 128, 64, 32, 16, 8))

    def body(w_ref, g_ref, m_ref, v_ref, d_ref, nm_ref, nv_ref):
        gv = g_ref[...]
        mn = ADAM_B1 * m_ref[...] + (1.0 - ADAM_B1) * gv
        vn = ADAM_B2 * v_ref[...] + (1.0 - ADAM_B2) * (gv * gv)
        m_hat = mn / (1.0 - ADAM_B1 ** ADAM_STEP)
        v_hat = vn / (1.0 - ADAM_B2 ** ADAM_STEP)
        d_ref[...] = -ADAM_LR * (m_hat / (jnp.sqrt(v_hat) + ADAM_EPS) + ADAM_WD * w_ref[...])
        nm_ref[...] = mn
        nv_ref[...] = vn

    blk = pl.BlockSpec((tr, cols), lambda i: (i, 0))
    outs = pl.pallas_call(
        body, grid=(rows // tr,), in_specs=[blk] * 4, out_specs=[blk] * 3, out_shape=[_sds((rows, cols), F32)] * 3,
        name=name, compiler_params=_cp("parallel"),
    )(w2, g2, m2, v2)
    return tuple(o.reshape(shape) for o in outs)


def _adamw_layer(w, g, m, v, layer, bufs, name):
    nl, r, cs = w.shape
    tr = _pick(r, (256, 128, 64, 32, 16, 8))

    def body(w_ref, g_ref, m_ref, v_ref, *rest):
        go_ref, d_ref, nm_ref, nv_ref = rest[-4:]
        gv = g_ref[...]
        mn = ADAM_B1 * m_ref[...] + (1.0 - ADAM_B1) * gv
        vn = ADAM_B2 * v_ref[...] + (1.0 - ADAM_B2) * (gv * gv)
        m_hat = mn / (1.0 - ADAM_B1 ** ADAM_STEP)
        v_hat = vn / (1.0 - ADAM_B2 ** ADAM_STEP)
        go_ref[...] = gv
        d_ref[...] = -ADAM_LR * (m_hat / (jnp.sqrt(v_hat) + ADAM_EPS) + ADAM_WD * w_ref[...])
        nm_ref[...] = mn
        nv_ref[...] = vn

    lay = pl.BlockSpec((None, tr, cs), lambda i: (layer, i, 0))
    in_specs = [lay, pl.BlockSpec((tr, cs), lambda i: (i, 0)), lay, lay]
    args = [w, g, m, v]
    aliases = {}
    if bufs is not None:
        in_specs += [pl.BlockSpec(memory_space=pl.ANY)] * 4
        args += list(bufs)
        aliases = {4 + i: i for i in range(4)}
    return pl.pallas_call(
        body, grid=(r // tr,), in_specs=in_specs, out_specs=[lay] * 4, out_shape=[_sds((nl, r, cs), F32)] * 4,
        input_output_aliases=aliases, name=name, compiler_params=_cp("parallel"),
    )(*args)


def _slot_sum(own, slots, name):
    n_slots, rows, cols = slots.shape
    whole_fits = rows * cols * 4 * (n_slots + 2) * 2 <= VMEM_LIMIT_BYTES // 2
    tr = rows if whole_fits else _pick(rows, (512, 352, 256, 128, 64, 32, 16, 8))

    def body(*refs):
        s_ref, o_ref = refs[-2], refs[-1]
        acc = s_ref[0].astype(F32) if own is None else refs[0][...].astype(F32) + s_ref[0].astype(F32)
        for k in range(1, n_slots):
            acc = acc + s_ref[k].astype(F32)
        o_ref[...] = acc

    row = pl.BlockSpec((tr, cols), lambda i: (i, 0))
    return pl.pallas_call(
        body, grid=(rows // tr,),
        in_specs=([] if own is None else [row]) + [pl.BlockSpec((n_slots, tr, cols), lambda i: (0, i, 0))],
        out_specs=row, out_shape=_sds((rows, cols), F32), name=name, compiler_params=_cp("parallel"),
    )(*(() if own is None else (own,)), slots)


def _me():
    return lax.axis_index("x"), lax.axis_index("y"), lax.axis_index("c")


def _other_chips(x, y):
    return [(1 - x, y), (x, 1 - y), (1 - x, 1 - y)]


ANY_SPEC = pl.BlockSpec(memory_space=pl.ANY)
HBM_SPEC = pl.BlockSpec(memory_space=pltpu.HBM)
SEM_SPEC = pl.BlockSpec(memory_space=pltpu.SEMAPHORE)
SPLIT_COPY_PARAMS = pltpu.CompilerParams(has_side_effects=pltpu.SideEffectType.DATAFLOW_SIDE_EFFECTING)
N_CHIPS = 4


def _aligned(v, m):
    return v if isinstance(v, int) else pl.multiple_of(v, m)


def _in_hbm(arr):
    return pltpu.with_memory_space_constraint(arr, pltpu.HBM)


def _peer(x, y, c, k):
    fx, fy, fc = (k >> 2) & 1, (k >> 1) & 1, k & 1
    px = 1 - x if fx else x
    py = 1 - y if fy else y
    pc = 1 - c if fc else c
    return px, py, pc


def _gather_start(shards, after, name):
    n = len(shards)

    def body(*refs):
        src, land = refs[:n], refs[n:2 * n]
        send_sems, recv_sems = refs[2 * n + 1], refs[2 * n + 2]
        token = refs[-1]
        x, y, c = _me()
        for w in range(n):
            for chip in _other_chips(x, y):
                pltpu.make_async_remote_copy(
                    src_ref=src[w], dst_ref=land[w].at[2 * x + y], send_sem=send_sems.at[w], recv_sem=recv_sems.at[w],
                    device_id=(*chip, c), device_id_type=MESH_ID).start()
        token[...] = jnp.zeros_like(token)

    lands = [lax.empty((N_CHIPS,) + s.shape, s.dtype) for s in shards]
    thru = [pltpu.HBM(s.shape, s.dtype) for s in shards] + [pltpu.HBM(z.shape, z.dtype) for z in lands]
    outs = pl.pallas_call(
        body, name=name,
        out_shape=(pltpu.SemaphoreType.DMA((n,)), pltpu.SemaphoreType.DMA((n,)), *thru, _sds((8, 128), F32)),
        in_specs=[HBM_SPEC] * (2 * n) + [ANY_SPEC],
        out_specs=(SEM_SPEC, SEM_SPEC, *[HBM_SPEC] * (2 * n), pl.BlockSpec(memory_space=pltpu.VMEM)),
        input_output_aliases={i: 2 + i for i in range(2 * n)}, compiler_params=SPLIT_COPY_PARAMS,
    )(*[_in_hbm(s) for s in shards], *[_in_hbm(z) for z in lands], after)
    return (outs[0], outs[1], outs[2:2 + n], outs[2 + n:2 + 2 * n]), outs[-1]


def _gather_wait(send_sems, recv_sems, srcs, lands, after, name):
    n = len(srcs)

    def body(*refs):
        land = refs[n:2 * n]
        send_sems, recv_sems = refs[2 * n], refs[2 * n + 1]
        x, y, c = _me()
        for w in range(n):
            three = land[w].at[pl.ds(0, N_CHIPS - 1)]
            cp = pltpu.make_async_remote_copy(src_ref=three, dst_ref=three, send_sem=send_sems.at[w], recv_sem=recv_sems.at[w],
                                              device_id=(x, y, c), device_id_type=MESH_ID)
            cp.wait_send()
            cp.wait_recv()

    both = list(srcs) + list(lands)
    outs = pl.pallas_call(
        body, name=name, out_shape=tuple(pltpu.HBM(b.shape, b.dtype) for b in both),
        in_specs=[HBM_SPEC] * (2 * n) + [SEM_SPEC, SEM_SPEC, ANY_SPEC], out_specs=[HBM_SPEC] * (2 * n),
        input_output_aliases={i: i for i in range(2 * n)}, compiler_params=SPLIT_COPY_PARAMS,
    )(*both, send_sems, recv_sems, after)
    return outs[n:2 * n]


def _push_start(grads, small, name):
    n = len(grads)
    srcs = list(grads) + ([] if small is None else [small])
    ns = len(srcs)

    def body(*refs):
        src, slots = refs[:ns], refs[ns:2 * ns]
        send_sems, recv_sems = refs[2 * ns], refs[2 * ns + 1]
        token = refs[-1]
        x, y, c = _me()
        for w in range(ns):
            for k in range(1, N_DEV):
                px, py, pc = _peer(x, y, c, k)
                if w < n:
                    hr = src[w].shape[1] // 2
                    piece = src[w].at[2 * px + py, pl.ds(_aligned(pc * hr, 16), hr), :]
                    slot = slots[w].at[k - 1]
                else:
                    piece = src[w]
                    slot = slots[w].at[4 * x + 2 * y + c]
                pltpu.make_async_remote_copy(
                    src_ref=piece, dst_ref=slot, send_sem=send_sems.at[w], recv_sem=recv_sems.at[w],
                    device_id=(px, py, pc), device_id_type=MESH_ID).start()
        token[...] = jnp.zeros_like(token)

    slots = [lax.empty((N_DEV - 1, g.shape[1] // 2, g.shape[2]), g.dtype) for g in grads]
    if small is not None:
        slots.append(lax.empty((N_DEV,) + small.shape, small.dtype))
    both = srcs + slots
    outs = pl.pallas_call(
        body, name=name,
        out_shape=(pltpu.SemaphoreType.DMA((ns,)), pltpu.SemaphoreType.DMA((ns,)),
                   *[pltpu.HBM(b.shape, b.dtype) for b in both], _sds((8, 128), F32)),
        in_specs=[HBM_SPEC] * len(both),
        out_specs=(SEM_SPEC, SEM_SPEC, *[HBM_SPEC] * len(both), pl.BlockSpec(memory_space=pltpu.VMEM)),
        input_output_aliases={i: 2 + i for i in range(len(both))}, compiler_params=SPLIT_COPY_PARAMS,
    )(*[_in_hbm(b) for b in both])
    return (outs[0], outs[1], outs[2:2 + ns], outs[2 + ns:2 + 2 * ns]), outs[-1]


def _push_wait(send_sems, recv_sems, srcs, slots, after, name):
    n = len(srcs)

    def body(*refs):
        slot = refs[n:2 * n]
        send_sems, recv_sems = refs[2 * n], refs[2 * n + 1]
        x, y, c = _me()
        for w in range(n):
            seven = slot[w].at[pl.ds(0, N_DEV - 1)]
            cp = pltpu.make_async_remote_copy(src_ref=seven, dst_ref=seven, send_sem=send_sems.at[w],
                                              recv_sem=recv_sems.at[w], device_id=(x, y, c), device_id_type=MESH_ID)
            cp.wait_send()
            cp.wait_recv()

    both = list(srcs) + list(slots)
    outs = pl.pallas_call(
        body, name=name, out_shape=tuple(pltpu.HBM(b.shape, b.dtype) for b in both),
        in_specs=[HBM_SPEC] * (2 * n) + [SEM_SPEC, SEM_SPEC, ANY_SPEC], out_specs=[HBM_SPEC] * (2 * n),
        input_output_aliases={i: i for i in range(2 * n)}, compiler_params=SPLIT_COPY_PARAMS,
    )(*both, send_sems, recv_sems, after)
    return outs[:n], outs[n:]


SWAP_CHUNK_BYTES = 2 * 1024 * 1024


def _swap_chunk_rows(hr, cs):
    ch = hr
    while ch * cs * 4 > SWAP_CHUNK_BYTES and ch % 16 == 0:
        ch //= 2
    return ch


def _swap_halves(halves, name):
    n = len(halves)
    chunk = [_swap_chunk_rows(*h.shape) for h in halves]
    rounds = max(h.shape[0] // ch for h, ch in zip(halves, chunk))

    def body(*refs):
        src, dst, buf = refs[:n], refs[n:2 * n], refs[2 * n:3 * n]
        load_sems, put_sems, send_sems, recv_sems = refs[3 * n:]
        x, y, c = _me()
        sibling = (x, y, 1 - c)
        for j in range(rounds):
            live = [w for w in range(n) if j < src[w].shape[0] // chunk[w]]
            loads = [pltpu.make_async_copy(src[w].at[pl.ds(j * chunk[w], chunk[w])], buf[w], load_sems.at[w]) for w in live]
            for ld in loads:
                ld.start()
            moves = []
            for ld, w in zip(loads, live):
                ld.wait()
                rows = pl.ds(_aligned(c * src[w].shape[0] + j * chunk[w], 8), chunk[w])
                put = pltpu.make_async_copy(buf[w], dst[w].at[rows], put_sems.at[w])
                send = pltpu.make_async_remote_copy(src_ref=buf[w], dst_ref=dst[w].at[rows], send_sem=send_sems.at[w],
                                                    recv_sem=recv_sems.at[w], device_id=sibling, device_id_type=MESH_ID)
                put.start()
                send.start()
                moves.append((put, send))
            for put, send in moves:
                put.wait()
                send.wait_send()
        for w in range(n):
            hr = src[w].shape[0]
            got = dst[w].at[pl.ds(_aligned((1 - c) * hr, 8), hr)]
            pltpu.make_async_remote_copy(src_ref=got, dst_ref=got, send_sem=send_sems.at[w], recv_sem=recv_sems.at[w],
                                         device_id=sibling, device_id_type=MESH_ID).wait_recv()

    return pl.pallas_call(
        body, in_specs=[ANY_SPEC] * n, out_specs=[ANY_SPEC] * n,
        out_shape=[_sds((2 * h.shape[0], h.shape[1]), F32) for h in halves],
        scratch_shapes=[pltpu.VMEM((ch, h.shape[1]), F32) for h, ch in zip(halves, chunk)]
        + [pltpu.SemaphoreType.DMA((n,))] * 4,
        name=name,
    )(*halves)


BIG = ("w_in", "w_out", "w_up", "w_down", "w_pe", "w_pg")
ROW_SHARDED = ("w_out", "w_down", "w_pg")
SMALL = ("norm1_g", "a_ln_g", "a_ln_b", "a_ws", "a_bs", "b_conv_w", "b_conv_b", "b_wa", "b_ba", "b_wx", "b_bx", "b_lam",
         "c_lb", "c_norm_g", "d_w", "d_scale", "norm2_g", "ffn_conv_w", "ffn_conv_b", "norm3_g", "final_g")
SMALL_SHARDED = ("b_conv_w", "ffn_conv_w")
WEIGHTS = ("norm1_g", "w_in", "a_ln_g", "a_ln_b", "a_ws", "a_bs", "b_conv_w", "b_conv_b", "b_wa", "b_ba", "b_wx", "b_bx",
           "b_lam", "c_lb", "c_norm_g", "d_w", "d_scale", "w_out", "norm2_g", "w_up", "ffn_conv_w", "ffn_conv_b", "w_down",
           "norm3_g", "w_pe", "w_pg", "final_g")
ARGS = ("x", "p") + WEIGHTS + ("loss_target",) + tuple("m_" + n for n in WEIGHTS) + tuple("v_" + n for n in WEIGHTS)


def _block_diag(w):
    eye = jnp.eye(N_HEADS, dtype=w.dtype)
    return (eye[None, :, None, :, None] * w[:, :, :, None, :]).reshape(w.shape[0], W_GRP, W_GRP)


def _diag_blocks(m):
    m4 = m.reshape(N_HEADS, HEAD_DIM, N_HEADS, HEAD_DIM)
    return jnp.stack([m4[h, :, h, :] for h in range(N_HEADS)])


def _lower_bounds(c_lb):
    lbs = jnp.cumsum(jax.nn.softmax(c_lb, axis=0), axis=0)
    return lbs - lbs[0:1]


def kernel(x, p, norm1_g, w_in, a_ln_g, a_ln_b, a_ws, a_bs, b_conv_w, b_conv_b, b_wa, b_ba, b_wx, b_bx, b_lam, c_lb, c_norm_g, d_w, d_scale, w_out, norm2_g, w_up, ffn_conv_w, ffn_conv_b, w_down, norm3_g, w_pe, w_pg, final_g, loss_target, m_norm1_g, m_w_in, m_a_ln_g, m_a_ln_b, m_a_ws, m_a_bs, m_b_conv_w, m_b_conv_b, m_b_wa, m_b_ba, m_b_wx, m_b_bx, m_b_lam, m_c_lb, m_c_norm_g, m_d_w, m_d_scale, m_w_out, m_norm2_g, m_w_up, m_ffn_conv_w, m_ffn_conv_b, m_w_down, m_norm3_g, m_w_pe, m_w_pg, m_final_g, v_norm1_g, v_w_in, v_a_ln_g, v_a_ln_b, v_a_ws, v_a_bs, v_b_conv_w, v_b_conv_b, v_b_wa, v_b_ba, v_b_wx, v_b_bx, v_b_lam, v_c_lb, v_c_norm_g, v_d_w, v_d_scale, v_w_out, v_norm2_g, v_w_up, v_ffn_conv_w, v_ffn_conv_b, v_w_down, v_norm3_g, v_w_pe, v_w_pg, v_final_g):
    return _step((x, p, norm1_g, w_in, a_ln_g, a_ln_b, a_ws, a_bs, b_conv_w, b_conv_b, b_wa, b_ba, b_wx, b_bx, b_lam, c_lb, c_norm_g, d_w, d_scale, w_out, norm2_g, w_up, ffn_conv_w, ffn_conv_b, w_down, norm3_g, w_pe, w_pg, final_g, loss_target, m_norm1_g, m_w_in, m_a_ln_g, m_a_ln_b, m_a_ws, m_a_bs, m_b_conv_w, m_b_conv_b, m_b_wa, m_b_ba, m_b_wx, m_b_bx, m_b_lam, m_c_lb, m_c_norm_g, m_d_w, m_d_scale, m_w_out, m_norm2_g, m_w_up, m_ffn_conv_w, m_ffn_conv_b, m_w_down, m_norm3_g, m_w_pe, m_w_pg, m_final_g, v_norm1_g, v_w_in, v_a_ln_g, v_a_ln_b, v_a_ws, v_a_bs, v_b_conv_w, v_b_conv_b, v_b_wa, v_b_ba, v_b_wx, v_b_bx, v_b_lam, v_c_lb, v_c_norm_g, v_d_w, v_d_scale, v_w_out, v_norm2_g, v_w_up, v_ffn_conv_w, v_ffn_conv_b, v_w_down, v_norm3_g, v_w_pe, v_w_pg, v_final_g))


SMALL_PER_LAYER = tuple(n for n in SMALL if n != "final_g")
GATHERED = BIG + SMALL_SHARDED
GATHER_LAYER0 = (("a", ("w_in", "b_conv_w")), ("b", ("w_out", "w_up", "ffn_conv_w", "w_down", "w_pe", "w_pg")))
PUSH_EARLY = ("w_pe", "w_pg", "w_down", "w_up")
PUSH_MID = ("w_out",)
PUSH_LATE = ("w_in",)
SMALL_MID = tuple(n for n in SMALL_PER_LAYER if n != "norm1_g")


def _cols_to_slabs(m):
    r, c4 = m.shape
    return jnp.moveaxis(m.reshape(r, N_CHIPS, c4 // N_CHIPS), 1, 0)


def _slabs_to_cols(s):
    return jnp.moveaxis(s, 0, 1).reshape(s.shape[1], -1)


def _pack_small(parts):
    flat = jnp.concatenate([p.reshape(-1) for p in parts])
    return jnp.pad(flat, (0, (-flat.shape[0]) % 1024)).reshape(-1, 128)


def _step(args):
    a = dict(zip(ARGS, args, strict=True))
    x0 = a["x"][0]
    target = a["loss_target"][0]
    nl = a["norm1_g"].shape[0]
    t, d = x0.shape
    f = a["w_down"].shape[1] * N_CHIPS
    cx, cy, cc = _me()
    my_shard = 2 * cx + cy
    shards = {n: a[n].astype(BF16) for n in BIG}
    shards.update({n: a[n] for n in SMALL_SHARDED})

    def start_gather(l, names, after, tag):
        return _gather_start([shards[n][l] for n in names], after, f"gather_start_{l}{tag}")

    def finish_gather(l, names, handle, after, tag):
        send, recv, srcs, lands = handle
        lands = _gather_wait(send, recv, srcs, lands, after, f"gather_wait_{l}{tag}")
        w = {}
        for n, land in zip(names, lands):
            full = lax.dynamic_update_slice(land, shards[n][l][None], (my_shard, 0, 0))
            if n in ROW_SHARDED:
                w[n] = full.reshape(-1, full.shape[-1])
            elif n in ("w_up", "w_pe"):
                w[n] = full
            else:
                w[n] = _slabs_to_cols(full)
        return w

    lbs, lbs_vjp = jax.vjp(_lower_bounds, a["c_lb"])
    tril = jnp.tril(jnp.ones((GMLP_CHUNK, GMLP_CHUNK), F32))

    stacked_params = {"wm": (a["a_ws"] * tril).astype(BF16)}
    stacked_params["wm_t"] = jnp.swapaxes(stacked_params["wm"], 2, 3)
    stacked_params["bs_t"] = jnp.repeat(jnp.swapaxes(a["a_bs"], 1, 2), HEAD_DIM, axis=2)
    for nm in ("b_wa", "b_wx", "d_w"):
        bd = _block_diag(a[nm]).astype(BF16)
        stacked_params[nm], stacked_params[nm + "_t"] = bd, jnp.swapaxes(bd, 1, 2)
    for nm in ("a_ln_g", "a_ln_b", "b_conv_b", "b_ba", "b_bx", "b_lam", "d_scale"):
        stacked_params[nm] = a[nm].reshape(nl, 1, W_GRP)
    stacked_params["lb"] = lbs.reshape(nl, 1, W_GRP)
    stacked_params["ng"] = jnp.tile(a["c_norm_g"], (1, N_HEADS)).reshape(nl, 1, W_GRP)
    stacked_params["ffn_conv_b"] = a["ffn_conv_b"].reshape(nl, 1, 2 * f)

    def layer_params(l, w):
        q = {k: v[l] for k, v in stacked_params.items()}
        q.update(w)
        return q

    saved, weights, params = [], [], []
    first_groups = {tag: start_gather(0, names, x0, tag)[0] for tag, names in GATHER_LAYER0}
    xl = x0
    for l in range(nl):
        if l == 0:
            w = finish_gather(0, GATHER_LAYER0[0][1], first_groups["a"], xl, "a")
        else:
            w = finish_gather(l, GATHERED, next_handle, xl, "")
        s = {"x0": xl}
        s["h1"] = _rms_fwd(xl, a["norm1_g"][0], "rms1_fwd") if l == 0 else h_next
        token = None
        if 0 < l < nl - 1:
            next_handle, token = start_gather(l + 1, GATHERED, s["h1"], "")
        s["z"] = _mm(s["h1"], w["w_in"], "nn", out_dtype=F32, name="mm_z", after=token)
        q = layer_params(l, w)
        mix = _mix_a_fwd(s["z"], d, q["a_ln_g"], q["a_ln_b"], q["wm"], q["bs_t"])
        mix, s["hs"] = _mix_b_fwd(s["z"], mix, q["b_conv_w"], q["b_conv_b"], q["b_wa"], q["b_wx"], q["b_ba"], q["b_bx"],
                                  q["b_lam"])
        mix, s["o_pre"], s["states"] = _mix_c_fwd(s["z"], mix, q["lb"], q["ng"])
        s["mix"] = _mix_d_fwd(s["z"], mix, q["d_w"], q["d_scale"])
        def land(tag, after):
            if l == 0:
                w.update(finish_gather(0, dict(GATHER_LAYER0)[tag], first_groups[tag], after, tag))
                q.update(w)

        land("b", s["mix"])
        token = None
        if l == 0 and nl > 1:
            next_handle, token = start_gather(1, GATHERED, w["w_out"], "")
        s["x1"], s["h2"] = _mm(s["mix"], w["w_out"], "nn", res=xl, out_dtype=F32, name="mm_out",
                               norm_g=a["norm2_g"][l], tm_max=512, after=token)
        s["hf_g"] = _mm(s["h2"], w["w_up"], "nn", b_slabs=True, n=f, out_dtype=F32, name="mm_up_g")
        s["hf_v"] = _mm(s["h2"], w["w_up"], "nn", b_slabs=True, n=f, b_noff=f, out_dtype=F32, name="mm_up_v")
        s["act"], s["gc"], s["vc"] = _ffn_act_fwd(s["hf_g"], s["hf_v"], q["ffn_conv_w"], q["ffn_conv_b"])
        s["x2"], s["h3"] = _mm(s["act"], w["w_down"], "nn", res=s["x1"], out_dtype=F32, name="mm_down",
                               norm_g=a["norm3_g"][l], tm_max=512)
        s["pre"] = _mm(s["h3"], w["w_pg"], "nn", out_dtype=F32, name="mm_pg")
        s["pe"] = _mm(a["p"][l, 0], w["w_pe"], "nn", b_slabs=True, out_dtype=F32, name="mm_pe")
        xl, h_next = _ple_fwd(s["x2"], s["pe"], s["pre"], a["norm1_g"][l + 1] if l + 1 < nl else None)
        saved.append(s)
        weights.append(w)
        params.append(q)

    dx, g_final, loss = _final_loss(xl, a["final_g"], target)
    loss = lax.psum(loss[0, 0], ("x", "y", "c"))

    stacked = {n: None for n in BIG}
    small_sums = {}

    def finish_push(l, names, handle, tag, after):
        send, recv, srcs, slots = handle
        srcs, slots = _push_wait(send, recv, srcs, slots, after, f"push_wait_{l}{tag}")
        halves = []
        for n, g, sl in zip(names, srcs, slots):
            hr = g.shape[1] // 2
            own = lax.dynamic_slice(g, (my_shard, cc * hr, 0), (1, hr, g.shape[2]))[0]
            halves.append(_slot_sum(own, sl, "sum_" + n))
        for n, g in zip(names, _swap_halves(halves, "swap_halves_" + tag)):
            stacked[n] = _adamw_layer(a[n], g, a["m_" + n], a["v_" + n], l, stacked[n], "adamw_" + n)
        if len(srcs) > len(names):
            by_sender = lax.dynamic_update_slice(slots[-1], srcs[-1][None], (2 * my_shard + cc, 0, 0))
            small_sums[l, tag] = _slot_sum(None, by_sender, "sum_small_" + tag)
        return stacked[names[-1]][1]

    pending = []
    token = None
    for l in reversed(range(nl)):
        q, s, w = params[l], saved[l], weights[l]
        gs = {}
        dpe, dpre = _ple_bwd(dx, s["pe"], s["pre"], after=token)
        g_pe = _mm(a["p"][l, 0], dpe, "tn", out_dtype=BF16, name="mm_dwpe", out_slabs=N_CHIPS)
        g_pg = _mm(s["h3"], dpre, "tn", out_dtype=BF16, name="mm_dwpg")
        dx2, gs["norm3_g"] = _mm(dpre, w["w_pg"], "nt", out_dtype=F32, name="mm_dh3", tm_max=512,
                                 rms_bwd=(s["x2"], a["norm3_g"][l], dx))
        g_down = _mm(s["act"], dx2, "tn", out_dtype=BF16, name="mm_dwdown")
        dhf_g, dhf_v, sums_g, sums_v = _ffn_bwd(dx2, w["w_down"], s["gc"], s["vc"], s["hf_g"], s["hf_v"], q["ffn_conv_w"])
        gs["ffn_conv_w"] = jnp.concatenate([sums_g[0:3], sums_v[0:3]], axis=1)
        gs["ffn_conv_b"] = jnp.concatenate([sums_g[3:4], sums_v[3:4]], axis=1)
        g_up = _mm(s["h2"], dhf_g, "tn", out_dtype=BF16, name="mm_dwup_g", out_slabs=N_CHIPS, out_n=2 * f)
        g_up = _mm(s["h2"], dhf_v, "tn", out_dtype=BF16, name="mm_dwup_v", out_slabs=N_CHIPS, out_n=2 * f, o_noff=f, out_buf=g_up)
        early = {"w_pe": g_pe, "w_up": g_up, "w_pg": g_pg.reshape(N_CHIPS, -1, g_pg.shape[-1]),
                 "w_down": g_down.reshape(N_CHIPS, -1, g_down.shape[-1])}
        early_handle, token = _push_start([early[n] for n in PUSH_EARLY], None, f"push_start_{l}a")
        dh2 = _mm(dhf_g, w["w_up"], "nt", b_slabs=True, out_dtype=F32, name="mm_dh2_g", after=token)
        dx1, gs["norm2_g"] = _mm(dhf_v, w["w_up"], "nt", b_slabs=True, b_koff=f, res=dh2, out_dtype=F32, name="mm_dh2_v",
                                 tm_max=512, rms_bwd=(s["x1"], a["norm2_g"][l], dx2))
        g_out = _mm(s["mix"], dx1, "tn", out_dtype=BF16, name="mm_dwout")
        dmix = _mm(dx1, w["w_out"], "nt", out_dtype=F32, name="mm_dmix")
        dz, gs["a_ln_g"], gs["a_ln_b"], dws, dbs_t = _mix_a_bwd(s["z"], dmix, q["a_ln_g"], q["a_ln_b"], q["wm"], q["wm_t"],
                                                               q["bs_t"])
        gs["a_ws"] = dws * tril
        gs["a_bs"] = dbs_t.reshape(GMLP_CHUNK, N_HEADS, HEAD_DIM).sum(-1).T
        dz, gs["b_conv_w"], gs["b_conv_b"], dwa, dwx, gs["b_ba"], gs["b_bx"], gs["b_lam"] = _mix_b_bwd(
            s["z"], dz, dmix, s["hs"], q["b_conv_w"], q["b_conv_b"], q["b_wa"], q["b_wx"], q["b_wa_t"], q["b_wx_t"],
            q["b_ba"], q["b_bx"], q["b_lam"])
        gs["b_wa"], gs["b_wx"] = _diag_blocks(dwa), _diag_blocks(dwx)
        dz, gs["c_lb"], dng = _mix_c_bwd(s["z"], dz, dmix, s["o_pre"], s["states"], q["lb"], q["ng"])
        gs["c_norm_g"] = dng.reshape(N_HEADS, HEAD_DIM).sum(0)
        dz, dwd, gs["d_scale"] = _mix_d_bwd(s["z"], dz, dmix, q["d_w"], q["d_w_t"], q["d_scale"])
        gs["d_w"] = _diag_blocks(dwd)
        small = [gs[n] for n in SMALL_MID] + ([g_final] if l == nl - 1 else [])
        mid_handle, token = _push_start([g_out.reshape(N_CHIPS, -1, g_out.shape[-1])], _pack_small(small), f"push_start_{l}b")
        g_in = _mm(s["h1"], dz, "tn", out_dtype=BF16, name="mm_dwin")
        dx, gs["norm1_g"] = _mm(dz, w["w_in"], "nt", out_dtype=F32, name="mm_dh1", after=(token, g_in), tm_max=512,
                                rms_bwd=(s["x0"], a["norm1_g"][l], dx1))

        late_handle, token = _push_start([_cols_to_slabs(g_in)], _pack_small([gs["norm1_g"]]), f"push_start_{l}c")
        dep = token
        for push in pending:
            dep = finish_push(*push, dep)
        pending = [(l, PUSH_EARLY, early_handle, "a"), (l, PUSH_MID, mid_handle, "b"), (l, PUSH_LATE, late_handle, "c")]
    for push in pending:
        dep = finish_push(*push, dep)
    grad_x = dx[None]

    def small_shape(n):
        return a[n].shape[1:-1] + (a[n].shape[-1] * N_CHIPS,) if n in SMALL_SHARDED else a[n].shape[1:]

    per_layer = {n: [] for n in SMALL_PER_LAYER}
    for l in range(nl):
        per_layer["norm1_g"].append(small_sums[l, "c"].reshape(-1)[:d])
        vec, off = small_sums[l, "b"].reshape(-1), 0
        for n in SMALL_MID:
            shape = small_shape(n)
            size = 1
            for dim in shape:
                size *= dim
            per_layer[n].append(vec[off:off + size].reshape(shape))
            off += size
        if l == nl - 1:
            grad_final = vec[off:off + d]
    grads = {n: jnp.stack(per_layer[n]) for n in SMALL_PER_LAYER}
    grads["c_lb"] = lbs_vjp(grads["c_lb"])[0]
    grads["final_g"] = grad_final
    for n in SMALL_SHARDED:
        cs = a[n].shape[-1]
        grads[n] = lax.dynamic_slice_in_dim(grads[n], my_shard * cs, cs, axis=2)

    outs = {}
    for n in WEIGHTS:
        if n in BIG:
            outs[n] = stacked[n]
        else:
            outs[n] = (grads[n],) + _adamw(a[n], grads[n], a["m_" + n], a["v_" + n], "adamw_" + n)
    return (loss, grad_x, *[outs[n][0] for n in WEIGHTS], *[outs[n][1] for n in WEIGHTS], *[outs[n][2] for n in WEIGHTS],
            *[outs[n][3] for n in WEIGHTS])
```

```python
import functools

import jax
import jax.numpy as jnp
from jax import lax
from jax.experimental import pallas as pl
from jax.experimental.pallas import tpu as pltpu

F32 = jnp.float32
BF16 = jnp.bfloat16
EPS = 1e-6
HEAD_DIM = 64
N_HEADS = 4
W_GRP = HEAD_DIM * N_HEADS
GMLP_CHUNK = 128
HGRN_CHUNK = 64
HGRN_UNROLL = 8
RGLRU_C = 8.0
POOL_HALO = 16
EXP_CLAMP = 80.0
ADAM_LR, ADAM_B1, ADAM_B2, ADAM_EPS, ADAM_WD, ADAM_STEP = 0.001, 0.9, 0.999, 1e-08, 0.01, 10
VMEM_LIMIT_BYTES = 56 * 1024 * 1024
TILE_PREFS = (1024, 1408, 768, 512, 256, 128)
MM_ROW_TILES = (2048, 1024, 512, 256, 128, 64, 32, 16, 8)
MM_VMEM_BUDGET = 42 * 1024 * 1024
ROW_TILE_PREFS = (512, 256, 128, 64, 32, 16, 8)
MESH_ID = pl.DeviceIdType.MESH
N_DEV = 8


def _pick(n, prefs=TILE_PREFS):
    for p in prefs:
        if n % p == 0:
            return p
    return n


def _cp(*sem):
    return pltpu.CompilerParams(dimension_semantics=sem if sem else None, vmem_limit_bytes=VMEM_LIMIT_BYTES)


def _sds(shape, dtype):
    return jax.ShapeDtypeStruct(tuple(shape), dtype)


_GELU_C = 0.7978845608028654
_GELU_A = 0.044715


def _gelu(x):
    return 0.5 * x * (1.0 + jnp.tanh(_GELU_C * (x + _GELU_A * x * x * x)))


def _gelu_and_grad(x):
    t = jnp.tanh(_GELU_C * (x + _GELU_A * x * x * x))
    g = 0.5 * x * (1.0 + t)
    dg = 0.5 * (1.0 + t) + 0.5 * x * (1.0 - t * t) * _GELU_C * (1.0 + 3.0 * _GELU_A * x * x)
    return g, dg


def _sigmoid(x):
    return 1.0 / (1.0 + jnp.exp(-x))


def _dot(a, b):
    return jnp.dot(a, b, preferred_element_type=F32)


def _dot_nt(a, b):
    return lax.dot_general(a, b, (((1,), (1,)), ((), ())), preferred_element_type=F32)


def _dot_tn(a, b):
    return lax.dot_general(a, b, (((0,), (0,)), ((), ())), preferred_element_type=F32)


def _split3(x):
    hi = x.astype(BF16)
    r1 = x - hi.astype(F32)
    mid = r1.astype(BF16)
    lo = (r1 - mid.astype(F32)).astype(BF16)
    return hi, mid, lo


def _dot_f32_rhs_exact(x, m_bf16):
    hi, mid, lo = _split3(x)
    return _dot(hi, m_bf16) + _dot(mid, m_bf16) + _dot(lo, m_bf16)


def _dot_f32_lhs_exact(m_bf16, x):
    hi, mid, lo = _split3(x)
    return _dot(m_bf16, hi) + _dot(m_bf16, mid) + _dot(m_bf16, lo)


def _head_masks(width=W_GRP):
    lane = lax.broadcasted_iota(jnp.int32, (1, width), 1)
    return [(lane >= h * HEAD_DIM) & (lane < (h + 1) * HEAD_DIM) for h in range(N_HEADS)]


def _block_mask(n=W_GRP):
    r = lax.broadcasted_iota(jnp.int32, (n, n), 0)
    c = lax.broadcasted_iota(jnp.int32, (n, n), 1)
    m = None
    for h in range(N_HEADS):
        mh = (r >= h * HEAD_DIM) & (r < (h + 1) * HEAD_DIM) & (c >= h * HEAD_DIM) & (c < (h + 1) * HEAD_DIM)
        m = mh if m is None else (m | mh)
    return m


def _mm(a, b, mode, *, out_dtype, name, res=None, b_slabs=False, n=None, b_noff=0, b_koff=0,
        out_slabs=0, out_buf=None, out_n=None, o_noff=0, after=(), norm_g=None, rms_bwd=None, tm_max=None):
    after = () if after is None else (tuple(after) if isinstance(after, (tuple, list)) else (after,))
    if mode == "tn":
        k_dim, m_dim = a.shape
    else:
        m_dim, k_dim = a.shape
    if mode == "nt":
        n_dim = b.shape[-2]
    else:
        n_dim = n if n is not None else (b.shape[0] * b.shape[2] if b_slabs else b.shape[1])
    n_total = out_n if out_n is not None else n_dim
    tm, tn, tk = _pick(m_dim), _pick(n_dim), _pick(k_dim)
    if tm_max is not None:
        tm = _pick(m_dim, tuple(p for p in TILE_PREFS if p <= tm_max))
    if b_slabs and mode == "nt":
        tk = _pick(b.shape[2])
    elif b_slabs:
        tn = _pick(b.shape[2])
    elif out_slabs:
        tn = _pick(n_total // out_slabs)
    def vmem_bytes(rows, kk):
        blocks = rows * kk * a.dtype.itemsize + kk * tn * b.dtype.itemsize + rows * tn * jnp.dtype(out_dtype).itemsize
        blocks += rows * tn * 4 * ((res is not None) + 2 * (rms_bwd is not None) + (norm_g is not None))
        return 2 * blocks + rows * tn * 4 * (1 if kk == k_dim else 2)

    row_cap = tm_max if tm_max is not None else (512 if mode == "tn" else MM_ROW_TILES[0])
    k_options = (tk,) if tk == k_dim else (k_dim, tk)
    choice = next(((rows, kk) for kk in k_options for rows in MM_ROW_TILES
                   if rows <= row_cap and m_dim % rows == 0 and rows >= min(256, m_dim) and vmem_bytes(rows, kk) <= MM_VMEM_BUDGET),
                  None)
    if choice is not None:
        tm, tk = choice
    nk = k_dim // tk
    assert b_noff % tn == 0 and b_koff % tk == 0 and o_noff % tn == 0 and n_dim % tn == 0 and k_dim % tk == 0
    bn0, bk0, on0 = b_noff // tn, b_koff // tk, o_noff // tn
    dims = {"nn": (((1,), (0,)), ((), ())), "nt": (((1,), (1,)), ((), ())), "tn": (((0,), (0,)), ((), ()))}[mode]

    if mode == "tn":
        a_spec = pl.BlockSpec((tk, tm), lambda i, j, k: (k, i))
    else:
        a_spec = pl.BlockSpec((tm, tk), lambda i, j, k: (i, k))
    slab_group = 0
    if not b_slabs:
        if mode == "nt":
            b_spec = pl.BlockSpec((tn, tk), lambda i, j, k: (j + bn0, k + bk0))
        else:
            b_spec = pl.BlockSpec((tk, tn), lambda i, j, k: (k + bk0, j + bn0))
    elif mode == "nt" and tk > b.shape[2]:
        slab_group = tk // b.shape[2]
        b_spec = pl.BlockSpec((slab_group, tn, b.shape[2]), lambda i, j, k: (bk0, j, 0))
    elif mode == "nt":
        bper = b.shape[2] // tk
        b_spec = pl.BlockSpec((None, tn, tk), lambda i, j, k: ((k + bk0) // bper, j, (k + bk0) % bper))
    else:
        bper = b.shape[2] // tn
        b_spec = pl.BlockSpec((None, tk, tn), lambda i, j, k: ((j + bn0) // bper, k, (j + bn0) % bper))
    in_specs = [a_spec, b_spec]
    args = [a, b]
    if res is not None:
        in_specs.append(pl.BlockSpec((tm, tn), lambda i, j, k: (i, j)))
        args.append(res)
    if out_slabs:
        oper = n_total // out_slabs // tn
        out_shape = _sds((out_slabs, m_dim, n_total // out_slabs), out_dtype)
        out_spec = pl.BlockSpec((None, tm, tn), lambda i, j, k: ((j + on0) // oper, i, (j + on0) % oper))
    else:
        out_shape = _sds((m_dim, n_total), out_dtype)
        out_spec = pl.BlockSpec((tm, tn), lambda i, j, k: (i, j + on0))
    out_specs, out_shapes = [out_spec], [out_shape]
    row_spec = pl.BlockSpec((tm, tn), lambda i, j, k: (i, 0))
    vec_spec = pl.BlockSpec((1, tn), lambda i, j, k: (0, 0))
    norm_at = rms_at = None
    if norm_g is not None:
        assert tn == n_dim and not out_slabs
        norm_at = len(args)
        in_specs.append(vec_spec)
        args.append(norm_g.reshape(1, n_dim))
        out_specs.append(row_spec)
        out_shapes.append(_sds((m_dim, n_dim), BF16))
    if rms_bwd is not None:
        assert tn == n_dim and not out_slabs and norm_g is None
        x_in, gain, dres = rms_bwd
        rms_at = len(args)
        in_specs += [row_spec, vec_spec, row_spec]
        args += [x_in, gain.reshape(1, n_dim), dres]
        out_specs, out_shapes = [row_spec, vec_spec], [_sds((m_dim, n_dim), F32), _sds((1, n_dim), F32)]
    aliases = {}
    if out_buf is not None:
        in_specs.append(pl.BlockSpec(memory_space=pl.ANY))
        args.append(out_buf)
        aliases = {len(args) - 1: 0}
    for dep in after:
        in_specs.append(pl.BlockSpec(memory_space=pl.ANY))
        args.append(dep)
    has_res = res is not None
    n_in = len(args)

    def body(*refs):
        a_ref, b_ref = refs[0], refs[1]
        res_ref = refs[2] if has_res else None
        o_ref = refs[n_in]
        acc_ref = refs[-1] if nk > 1 else None

        def product(rows):
            if slab_group:
                cs = b.shape[2]
                terms = [lax.dot_general(a_ref[rows, s * cs:(s + 1) * cs].astype(BF16), b_ref[s].astype(BF16), dims,
                                         preferred_element_type=F32) for s in range(slab_group)]
                return functools.reduce(lambda p, q: p + q, terms)
            lhs = a_ref[rows, :] if mode != "tn" else a_ref[:, rows]
            return lax.dot_general(lhs.astype(BF16), b_ref[...].astype(BF16), dims, preferred_element_type=F32)

        def finish(v, rows):
            if has_res:
                v = v + res_ref[rows, :]
            if rms_at is not None:
                xv, gv = refs[rms_at][rows, :], refs[rms_at + 1][...]
                r = lax.rsqrt(jnp.mean(xv * xv, axis=-1, keepdims=True) + EPS)
                dyg = v * gv
                dot = jnp.mean(dyg * xv, axis=-1, keepdims=True)
                o_ref[rows, :] = refs[rms_at + 2][rows, :] + r * dyg - xv * (r * r * r) * dot
                return jnp.sum(v * xv * r, axis=0, keepdims=True)
            o_ref[rows, :] = v.astype(o_ref.dtype)
            if norm_at is not None:
                r = lax.rsqrt(jnp.mean(v * v, axis=-1, keepdims=True) + EPS)
                refs[n_in + 1][rows, :] = (v * r * refs[norm_at][...]).astype(BF16)
            return None

        def add_gain_grad(gpart):
            dg_ref = refs[n_in + 1]
            first = pl.program_id(0) == 0

            @pl.when(first)
            def _():
                dg_ref[...] = gpart

            @pl.when(jnp.logical_not(first))
            def _():
                dg_ref[...] += gpart

        whole = slice(0, tm)
        if nk == 1:
            gpart = finish(product(whole), whole)
            if rms_at is not None:
                add_gain_grad(gpart)
        else:
            part = product(whole)
            kk = pl.program_id(2)

            @pl.when(kk == 0)
            def _():
                acc_ref[...] = part

            @pl.when(kk > 0)
            def _():
                acc_ref[...] += part

            @pl.when(kk == nk - 1)
            def _():
                gpart = finish(acc_ref[...], whole)
                if rms_at is not None:
                    add_gain_grad(gpart)

    outs = pl.pallas_call(
        body, grid=(m_dim // tm, n_dim // tn, nk), in_specs=in_specs, out_specs=out_specs, out_shape=out_shapes,
        scratch_shapes=[pltpu.VMEM((tm, tn), F32)] if nk > 1 else [],
        input_output_aliases=aliases, name=name,
        compiler_params=_cp(*(("arbitrary",) * 3 if rms_bwd is not None else ("parallel", "parallel", "arbitrary"))),
    )(*args)
    return outs[0] if len(outs) == 1 else tuple(outs)


def _rms_fwd(x, g, name):
    t, d = x.shape
    tm = _pick(t, ROW_TILE_PREFS)

    def body(x_ref, g_ref, o_ref):
        xv = x_ref[...]
        r = lax.rsqrt(jnp.mean(xv * xv, axis=-1, keepdims=True) + EPS)
        o_ref[...] = (xv * r * g_ref[...]).astype(o_ref.dtype)

    return pl.pallas_call(
        body, grid=(t // tm,),
        in_specs=[pl.BlockSpec((tm, d), lambda i: (i, 0)), pl.BlockSpec((1, d), lambda i: (0, 0))],
        out_specs=pl.BlockSpec((tm, d), lambda i: (i, 0)), out_shape=_sds((t, d), BF16),
        name=name, compiler_params=_cp("parallel"),
    )(x, g.reshape(1, d))


def _final_loss(x, g, target):
    t, d = x.shape
    tm = _pick(t, ROW_TILE_PREFS)

    def body(x_ref, g_ref, t_ref, dx_ref, dg_ref, loss_ref):
        i = pl.program_id(0)
        xv = x_ref[...]
        gv = g_ref[...]
        r = lax.rsqrt(jnp.mean(xv * xv, axis=-1, keepdims=True) + EPS)
        err = xv * r * gv - t_ref[...]
        lpart = (0.5 / d) * jnp.sum(jnp.sum(err * err, axis=1, keepdims=True), axis=0, keepdims=True)
        dy = err * (1.0 / d)
        dyg = dy * gv
        dot = jnp.mean(dyg * xv, axis=-1, keepdims=True)
        dx_ref[...] = r * dyg - xv * (r * r * r) * dot
        part = jnp.sum(dy * xv * r, axis=0, keepdims=True)

        @pl.when(i == 0)
        def _():
            dg_ref[...] = part
            loss_ref[...] = lpart

        @pl.when(i > 0)
        def _():
            dg_ref[...] += part
            loss_ref[...] += lpart

    row = pl.BlockSpec((tm, d), lambda i: (i, 0))
    vec = pl.BlockSpec((1, d), lambda i: (0, 0))
    return pl.pallas_call(
        body, grid=(t // tm,), in_specs=[row, vec, row], out_specs=[row, vec, pl.BlockSpec((1, 1), lambda i: (0, 0))],
        out_shape=[_sds((t, d), F32), _sds((1, d), F32), _sds((1, 1), F32)], name="final_loss",
        compiler_params=_cp("arbitrary"),
    )(x, g.reshape(1, d), target)


def _shift_down(ext, k, halo):
    return pltpu.roll(ext, k, 0)[halo:]


def _shift_up(ext, k, tm):
    return pltpu.roll(ext, ext.shape[0] - k, 0)[:tm]


def _ffn_tiles(t, f):
    return _pick(t, (256, 128, 64, 32, 16, 8)), _pick(f, (1408, 256, 128))


def _ffn_act_fwd(hf_g, hf_v, conv_w, conv_b):
    t, f = hf_g.shape
    tm, cn = _ffn_tiles(t, f)
    nf = f // cn

    def body(g_ref, v_ref, wg_ref, wv_ref, bg_ref, bv_ref, o_ref, gc_ref, vc_ref, ext_ref, hg_ref, hv_ref):
        i = pl.program_id(1)

        @pl.when(i == 0)
        def _():
            hg_ref[...] = jnp.zeros_like(hg_ref)
            hv_ref[...] = jnp.zeros_like(hv_ref)

        def conv(x_ref, halo_ref, w_ref, b_ref):
            ext_ref[0:8, :] = halo_ref[...]
            ext_ref[8:, :] = x_ref[...]
            halo_ref[...] = x_ref[tm - 8:tm, :]
            ext = ext_ref[...]
            w = w_ref[...]
            return b_ref[...] + w[2:3, :] * ext[8:] + w[1:2, :] * _shift_down(ext, 1, 8) + w[0:1, :] * _shift_down(ext, 2, 8)

        gc = conv(g_ref, hg_ref, wg_ref, bg_ref)
        vc = conv(v_ref, hv_ref, wv_ref, bv_ref)
        o_ref[...] = (_gelu(gc) * vc).astype(o_ref.dtype)
        gc_ref[...] = gc.astype(gc_ref.dtype)
        vc_ref[...] = vc.astype(vc_ref.dtype)

    blk = pl.BlockSpec((tm, cn), lambda j, i: (i, j))
    return pl.pallas_call(
        body, grid=(nf, t // tm),
        in_specs=[blk, blk, pl.BlockSpec((3, cn), lambda j, i: (0, j)), pl.BlockSpec((3, cn), lambda j, i: (0, j + nf)),
                  pl.BlockSpec((1, cn), lambda j, i: (0, j)), pl.BlockSpec((1, cn), lambda j, i: (0, j + nf))],
        out_specs=[blk, blk, blk], out_shape=[_sds((t, f), BF16)] * 3,
        scratch_shapes=[pltpu.VMEM((tm + 8, cn), F32), pltpu.VMEM((8, cn), F32), pltpu.VMEM((8, cn), F32)],
        name="ffn_act_fwd", compiler_params=_cp("parallel", "arbitrary"),
    )(hf_g, hf_v, conv_w, conv_w, conv_b, conv_b)


def _ffn_bwd(dx2, w_down, gc, vc, hf_g, hf_v, conv_w):
    t, f = hf_g.shape
    d = dx2.shape[1]
    tm, cn = _ffn_tiles(t, f)
    nf, nt = f // cn, t // tm

    def body(dx_ref, wd_ref, gc_ref, vc_ref, g_ref, v_ref, wg_ref, wv_ref, dg_ref, dv_ref, sg_ref, sv_ref,
             ext_ref, cg_ref, cv_ref):
        @pl.when(pl.program_id(1) == 0)
        def _():
            for ref in (cg_ref, cv_ref, sg_ref, sv_ref):
                ref[...] = jnp.zeros_like(ref)

        da = _dot_nt(dx_ref[...].astype(BF16), wd_ref[...])
        gel, dgel = _gelu_and_grad(gc_ref[...].astype(F32))
        dgc = da * vc_ref[...].astype(F32) * dgel
        dvc = da * gel

        def back(dc, carry_ref, w, x, out_ref, sums_ref):
            ext_ref[0:tm, :] = dc
            ext_ref[tm:tm + 8, :] = carry_ref[...]
            carry_ref[...] = dc[0:8, :]
            ext = ext_ref[...]
            up1, up2 = _shift_up(ext, 1, tm), _shift_up(ext, 2, tm)
            out_ref[...] = (w[2:3, :] * dc + w[1:2, :] * up1 + w[0:1, :] * up2).astype(out_ref.dtype)
            sums_ref[0:1, :] += jnp.sum(up2 * x, axis=0, keepdims=True)
            sums_ref[1:2, :] += jnp.sum(up1 * x, axis=0, keepdims=True)
            sums_ref[2:3, :] += jnp.sum(dc * x, axis=0, keepdims=True)
            sums_ref[3:4, :] += jnp.sum(dc, axis=0, keepdims=True)

        back(dgc, cg_ref, wg_ref[...], g_ref[...], dg_ref, sg_ref)
        back(dvc, cv_ref, wv_ref[...], v_ref[...], dv_ref, sv_ref)

    blk = pl.BlockSpec((tm, cn), lambda j, i: (nt - 1 - i, j))
    sums = pl.BlockSpec((8, cn), lambda j, i: (0, j))
    return pl.pallas_call(
        body, grid=(nf, nt),
        in_specs=[pl.BlockSpec((tm, d), lambda j, i: (nt - 1 - i, 0)), pl.BlockSpec((cn, d), lambda j, i: (j, 0)),
                  blk, blk, blk, blk, pl.BlockSpec((3, cn), lambda j, i: (0, j)), pl.BlockSpec((3, cn), lambda j, i: (0, j + nf))],
        out_specs=[blk, blk, sums, sums],
        out_shape=[_sds((t, f), BF16), _sds((t, f), BF16), _sds((8, f), F32), _sds((8, f), F32)],
        scratch_shapes=[pltpu.VMEM((tm + 8, cn), F32), pltpu.VMEM((8, cn), F32), pltpu.VMEM((8, cn), F32)],
        name="ffn_bwd", compiler_params=_cp("parallel", "arbitrary"),
    )(dx2, w_down, gc, vc, hf_g, hf_v, conv_w, conv_w)


def _ple_fwd(x2, pe, pre, next_g=None):
    t, d = x2.shape
    tm = _pick(t, ROW_TILE_PREFS)

    def body(x_ref, pe_ref, pre_ref, *rest):
        x3 = x_ref[...] + pe_ref[...] * _sigmoid(pre_ref[...])
        if next_g is None:
            rest[0][...] = x3
        else:
            g_ref, o_ref, h_ref = rest
            o_ref[...] = x3
            r = lax.rsqrt(jnp.mean(x3 * x3, axis=-1, keepdims=True) + EPS)
            h_ref[...] = (x3 * r * g_ref[...]).astype(h_ref.dtype)

    row = pl.BlockSpec((tm, d), lambda i: (i, 0))
    if next_g is None:
        return pl.pallas_call(body, grid=(t // tm,), in_specs=[row, row, row], out_specs=row,
                              out_shape=_sds((t, d), F32), name="ple_fwd", compiler_params=_cp("parallel"))(x2, pe, pre), None
    return pl.pallas_call(body, grid=(t // tm,), in_specs=[row, row, row, pl.BlockSpec((1, d), lambda i: (0, 0))],
                          out_specs=[row, row], out_shape=[_sds((t, d), F32), _sds((t, d), BF16)], name="ple_norm_fwd",
                          compiler_params=_cp("parallel"))(x2, pe, pre, next_g.reshape(1, d))


def _ple_bwd(dx3, pe, pre, after=None):
    t, d = dx3.shape
    tm = _pick(t, ROW_TILE_PREFS)

    def body(dx_ref, pe_ref, pre_ref, *rest):
        dpe_ref, dpre_ref = rest[-2:]
        gate = _sigmoid(pre_ref[...])
        dx = dx_ref[...]
        dpe_ref[...] = (dx * gate).astype(dpe_ref.dtype)
        dpre_ref[...] = (dx * pe_ref[...] * gate * (1.0 - gate)).astype(dpre_ref.dtype)

    row = pl.BlockSpec((tm, d), lambda i: (i, 0))
    extra = [] if after is None else [after]
    return pl.pallas_call(body, grid=(t // tm,), in_specs=[row, row, row] + [pl.BlockSpec(memory_space=pl.ANY)] * len(extra),
                          out_specs=[row, row], out_shape=[_sds((t, d), BF16), _sds((t, d), BF16)], name="ple_bwd",
                          compiler_params=_cp("parallel"))(dx3, pe, pre, *extra)


def _mix_tm(t):
    return _pick(t, (512, 256, 128))


def _zblk(tm, col, rev_nt=None):
    if rev_nt is None:
        return pl.BlockSpec((tm, W_GRP), lambda i: (i, col))
    return pl.BlockSpec((tm, W_GRP), lambda i: (rev_nt - 1 - i, col))


def _full(shape):
    nd = len(shape)
    return pl.BlockSpec(tuple(shape), lambda i: (0,) * nd)


def _gmlp_sv(wm_ref, vnc, bs, hm):
    sv = bs
    for h in range(N_HEADS):
        sv = sv + jnp.where(hm[h], _dot(wm_ref[h], vnc), 0.0)
    return sv


def _layernorm(v, g, b):
    mu = jnp.mean(v, axis=-1, keepdims=True)
    vc = v - mu
    rs = lax.rsqrt(jnp.mean(vc * vc, axis=-1, keepdims=True) + EPS)
    xhat = vc * rs
    return xhat, rs, xhat * g + b


def _mix_a_fwd(z, d_mix, ln_g, ln_b, wm, bs_t):
    t = z.shape[0]
    tm = _mix_tm(t)

    def body(u_ref, v_ref, g_ref, b_ref, wm_ref, bs_ref, o_ref):
        hm = _head_masks()
        ug = _gelu(u_ref[...])
        _, _, vn = _layernorm(_gelu(v_ref[...]), g_ref[...], b_ref[...])
        vnb = vn.astype(BF16)
        for n in range(tm // GMLP_CHUNK):
            sl = slice(n * GMLP_CHUNK, (n + 1) * GMLP_CHUNK)
            o_ref[sl, :] = ug[sl] * _gmlp_sv(wm_ref, vnb[sl], bs_ref[...], hm)

    return pl.pallas_call(
        body, grid=(t // tm,),
        in_specs=[_zblk(tm, 0), _zblk(tm, 1), _full((1, W_GRP)), _full((1, W_GRP)), _full(wm.shape), _full(bs_t.shape)],
        out_specs=_zblk(tm, 0), out_shape=_sds((t, d_mix), F32), name="mix_a_fwd", compiler_params=_cp("parallel"),
    )(z, z, ln_g, ln_b, wm, bs_t)


def _mix_a_bwd(z, dmix, ln_g, ln_b, wm, wm_t, bs_t):
    t, zc = z.shape
    tm = _mix_tm(t)

    def body(u_ref, v_ref, dy_ref, g_ref, b_ref, wm_ref, wmt_ref, bs_ref, dz_ref, dg_ref, db_ref, dws_ref, dbs_ref):
        i = pl.program_id(0)

        @pl.when(i == 0)
        def _():
            dg_ref[...] = jnp.zeros_like(dg_ref)
            db_ref[...] = jnp.zeros_like(db_ref)
            dws_ref[...] = jnp.zeros_like(dws_ref)
            dbs_ref[...] = jnp.zeros_like(dbs_ref)

        hm = _head_masks()
        ug, dug = _gelu_and_grad(u_ref[...])
        vg, dvg = _gelu_and_grad(v_ref[...])
        gv = g_ref[...]
        xhat, rs, vn = _layernorm(vg, gv, b_ref[...])
        vnb = vn.astype(BF16)
        dy = dy_ref[...]
        for n in range(tm // GMLP_CHUNK):
            sl = slice(n * GMLP_CHUNK, (n + 1) * GMLP_CHUNK)
            vnc = vnb[sl]
            sv = _gmlp_sv(wm_ref, vnc, bs_ref[...], hm)
            dsv = dy[sl] * ug[sl]
            dz_ref[sl, 0:W_GRP] = dy[sl] * sv * dug[sl]
            dbs_ref[...] += dsv
            dsvb = dsv.astype(BF16)
            dvn = jnp.zeros((GMLP_CHUNK, W_GRP), F32)
            for h in range(N_HEADS):
                dws_ref[h] += _dot_nt(jnp.where(hm[h], dsv, 0.0).astype(BF16), vnc)
                dvn = dvn + jnp.where(hm[h], _dot(wmt_ref[h], dsvb), 0.0)
            xh = xhat[sl]
            dg_ref[...] += jnp.sum(dvn * xh, axis=0, keepdims=True)
            db_ref[...] += jnp.sum(dvn, axis=0, keepdims=True)
            dxh = dvn * gv
            dvg_c = rs[sl] * (dxh - jnp.mean(dxh, axis=-1, keepdims=True) - xh * jnp.mean(dxh * xh, axis=-1, keepdims=True))
            dz_ref[sl, W_GRP:2 * W_GRP] = dvg_c * dvg[sl]

    return pl.pallas_call(
        body, grid=(t // tm,),
        in_specs=[_zblk(tm, 0), _zblk(tm, 1), _zblk(tm, 0), _full((1, W_GRP)), _full((1, W_GRP)), _full(wm.shape),
                  _full(wm_t.shape), _full(bs_t.shape)],
        out_specs=[pl.BlockSpec((tm, 2 * W_GRP), lambda i: (i, 0)), _full((1, W_GRP)), _full((1, W_GRP)),
                   _full(wm.shape), _full(bs_t.shape)],
        out_shape=[_sds((t, zc), F32), _sds((1, W_GRP), F32), _sds((1, W_GRP), F32), _sds(wm.shape, F32),
                   _sds(bs_t.shape, F32)],
        name="mix_a_bwd", compiler_params=_cp("arbitrary"),
    )(z, z, dmix, ln_g, ln_b, wm, wm_t, bs_t)


def _softplus(x):
    return jnp.maximum(x, 0.0) + jnp.log(1.0 + jnp.exp(-jnp.abs(x)))


def _neg_expm1(x):
    series = -x * (1.0 + x * 0.5 * (1.0 + x * (1.0 / 3.0) * (1.0 + x * 0.25 * (1.0 + x * 0.2))))
    return jnp.where(x > -0.1, series, 1.0 - jnp.exp(x))


def _rglru_gates(ext_ref, x_ref, halo, cw, cb, wa_ref, wx_ref, ba, bx, lam):
    ext_ref[0:8, :] = halo
    ext_ref[8:, :] = x_ref[...]
    ext = ext_ref[...]
    x0, x1, x2, x3 = ext[8:], _shift_down(ext, 1, 8), _shift_down(ext, 2, 8), _shift_down(ext, 3, 8)
    xc = cb + cw[3:4, :] * x0 + cw[2:3, :] * x1 + cw[1:2, :] * x2 + cw[0:1, :] * x3
    xcb = xc.astype(BF16)
    r = _sigmoid(_dot(xcb, wa_ref[...]) + ba)
    ig = _sigmoid(_dot(xcb, wx_ref[...]) + bx)
    sp = _softplus(-lam)
    la = -RGLRU_C * r * sp
    a = jnp.exp(la)
    mult = jnp.sqrt(_neg_expm1(2.0 * la))
    return (x0, x1, x2, x3), xc, r, ig, sp, a, mult


def _mix_b_fwd(z, mix, conv_w, conv_b, wa, wx, ba, bx, lam):
    t = z.shape[0]
    tm = _mix_tm(t)

    def body(x_ref, gb_ref, cw_ref, cb_ref, wa_ref, wx_ref, ba_ref, bx_ref, lam_ref, mix_in, o_ref, hs_ref,
             ext_ref, a_ref, b_ref, xh_ref, hc_ref):
        i = pl.program_id(0)

        @pl.when(i == 0)
        def _():
            xh_ref[...] = jnp.zeros_like(xh_ref)
            hc_ref[...] = jnp.zeros_like(hc_ref)

        _, xc, _, ig, _, a, mult = _rglru_gates(ext_ref, x_ref, xh_ref[...], cw_ref[...], cb_ref[...], wa_ref, wx_ref,
                                                ba_ref[...], bx_ref[...], lam_ref[...])
        xh_ref[...] = x_ref[tm - 8:tm, :]
        a_ref[...] = a
        b_ref[...] = mult * (ig * xc)
        rid = lax.broadcasted_iota(jnp.int32, (8, W_GRP), 0)

        def group(gi, hprev):
            base = pl.multiple_of(gi * 8, 8)
            ca = a_ref[pl.ds(base, 8), :]
            cb = b_ref[pl.ds(base, 8), :]
            for k in (1, 2, 4):
                m = rid >= k
                cb = jnp.where(m, ca * pltpu.roll(cb, k, 0) + cb, cb)
                ca = jnp.where(m, ca * pltpu.roll(ca, k, 0), ca)
            hh = cb + ca * hprev
            hs_ref[pl.ds(base, 8), :] = hh
            return hh[7:8, :]

        hlast = lax.fori_loop(0, tm // 8, group, hc_ref[0:1, :])
        hc_ref[...] = jnp.broadcast_to(hlast, hc_ref.shape)
        o_ref[...] = hs_ref[...] * _gelu(gb_ref[...])

    sq = _full((W_GRP, W_GRP))
    vec = _full((1, W_GRP))
    return pl.pallas_call(
        body, grid=(t // tm,),
        in_specs=[_zblk(tm, 2), _zblk(tm, 3), _full((4, W_GRP)), vec, sq, sq, vec, vec, vec, pl.BlockSpec(memory_space=pl.ANY)],
        out_specs=[_zblk(tm, 1), pl.BlockSpec((tm, W_GRP), lambda i: (i, 0))],
        out_shape=[_sds(mix.shape, F32), _sds((t, W_GRP), F32)],
        scratch_shapes=[pltpu.VMEM((tm + 8, W_GRP), F32), pltpu.VMEM((tm, W_GRP), F32), pltpu.VMEM((tm, W_GRP), F32),
                        pltpu.VMEM((8, W_GRP), F32), pltpu.VMEM((8, W_GRP), F32)],
        input_output_aliases={9: 0}, name="mix_b_fwd", compiler_params=_cp("arbitrary"),
    )(z, z, conv_w, conv_b, wa, wx, ba, bx, lam, mix)


def _mix_b_bwd(z, dz, dmix, hs, conv_w, conv_b, wa, wx, wa_t, wx_t, ba, bx, lam):
    t = z.shape[0]
    tm = _mix_tm(t)
    nt = t // tm
    hb = tm // 8

    def body(x_ref, gb_ref, xhalo_ref, hs_ref, hhalo_ref, dy_ref, cw_ref, cb_ref, wa_ref, wx_ref, wat_ref, wxt_ref,
             ba_ref, bx_ref, lam_ref, dz_in, dz_ref, dcw_ref, dcb_ref, dwa_ref, dwx_ref, dba_ref, dbx_ref, dlam_ref,
             ext_ref, c_ref, d_ref, g_ref, an_ref, gn_ref, dxn_ref):
        i = pl.program_id(0)
        first_tile = i == nt - 1

        @pl.when(i == 0)
        def _():
            for ref in (dcw_ref, dcb_ref, dwa_ref, dwx_ref, dba_ref, dbx_ref, dlam_ref, an_ref, gn_ref, dxn_ref):
                ref[...] = jnp.zeros_like(ref)

        cw, lam = cw_ref[...], lam_ref[...]
        xhalo = jnp.where(first_tile, 0.0, xhalo_ref[...])
        (x0, x1, x2, x3), xc, r, ig, sp, a, mult = _rglru_gates(
            ext_ref, x_ref, xhalo, cw, cb_ref[...], wa_ref, wx_ref, ba_ref[...], bx_ref[...], lam)
        hs = hs_ref[...]
        dy = dy_ref[...]
        gel, dgel = _gelu_and_grad(gb_ref[...])
        dz_ref[:, W_GRP:2 * W_GRP] = dy * hs * dgel

        ext_ref[0:tm, :] = a
        ext_ref[tm:tm + 8, :] = an_ref[...]
        an_ref[...] = a[0:8, :]
        c_ref[...] = _shift_up(ext_ref[...], 1, tm)
        d_ref[...] = dy * gel
        rid = lax.broadcasted_iota(jnp.int32, (8, W_GRP), 0)

        def group(j, gnext):
            base = pl.multiple_of((tm // 8 - 1 - j) * 8, 8)
            cc = c_ref[pl.ds(base, 8), :]
            cd = d_ref[pl.ds(base, 8), :]
            for k in (1, 2, 4):
                m = rid < 8 - k
                cd = jnp.where(m, cc * pltpu.roll(cd, 8 - k, 0) + cd, cd)
                cc = jnp.where(m, cc * pltpu.roll(cc, 8 - k, 0), cc)
            gg = cd + cc * gnext
            g_ref[pl.ds(base, 8), :] = gg
            return gg[0:1, :]

        gfirst = lax.fori_loop(0, tm // 8, group, gn_ref[0:1, :])
        gn_ref[...] = jnp.broadcast_to(gfirst, gn_ref.shape)
        g = g_ref[...]

        ext_ref[0:8, :] = jnp.where(first_tile, 0.0, hhalo_ref[...])
        ext_ref[8:, :] = hs
        hprev = _shift_down(ext_ref[...], 1, 8)
        da = g * hprev
        dmult = g * (ig * xc)
        di = g * mult * xc
        dxc = g * mult * ig
        dla = da * a - dmult * a * a / mult
        dr = dla * (-RGLRU_C * sp)
        dlam_ref[...] += jnp.sum(dla * (-RGLRU_C * r), axis=0, keepdims=True) * (-_sigmoid(-lam))
        dpr = dr * r * (1.0 - r)
        dpi = di * ig * (1.0 - ig)
        dprb, dpib, xcb = dpr.astype(BF16), dpi.astype(BF16), xc.astype(BF16)
        dba_ref[...] += jnp.sum(dpr, axis=0, keepdims=True)
        dbx_ref[...] += jnp.sum(dpi, axis=0, keepdims=True)
        dwa_ref[...] += _dot_tn(xcb, dprb)
        dwx_ref[...] += _dot_tn(xcb, dpib)
        dxc = dxc + _dot(dprb, wat_ref[...]) + _dot(dpib, wxt_ref[...])
        dcb_ref[...] += jnp.sum(dxc, axis=0, keepdims=True)
        dcw_ref[3:4, :] += jnp.sum(dxc * x0, axis=0, keepdims=True)
        dcw_ref[2:3, :] += jnp.sum(dxc * x1, axis=0, keepdims=True)
        dcw_ref[1:2, :] += jnp.sum(dxc * x2, axis=0, keepdims=True)
        dcw_ref[0:1, :] += jnp.sum(dxc * x3, axis=0, keepdims=True)
        ext_ref[0:tm, :] = dxc
        ext_ref[tm:tm + 8, :] = dxn_ref[...]
        dxn_ref[...] = dxc[0:8, :]
        ext = ext_ref[...]
        dz_ref[:, 0:W_GRP] = (cw[3:4, :] * dxc + cw[2:3, :] * _shift_up(ext, 1, tm) + cw[1:2, :] * _shift_up(ext, 2, tm)
                              + cw[0:1, :] * _shift_up(ext, 3, tm))

    sq = _full((W_GRP, W_GRP))
    vec = _full((1, W_GRP))
    halo = lambda col: pl.BlockSpec((8, W_GRP), lambda i: (jnp.maximum((nt - 1 - i) * hb - 1, 0), col))
    rev = lambda col: _zblk(tm, col, nt)
    return pl.pallas_call(
        body, grid=(nt,),
        in_specs=[rev(2), rev(3), halo(2), rev(0), halo(0), rev(1), _full((4, W_GRP)), vec, sq, sq, sq, sq, vec, vec, vec,
                  pl.BlockSpec(memory_space=pl.ANY)],
        out_specs=[pl.BlockSpec((tm, 2 * W_GRP), lambda i: (nt - 1 - i, 1)), _full((4, W_GRP)), vec, sq, sq, vec, vec, vec],
        out_shape=[_sds(dz.shape, F32), _sds((4, W_GRP), F32), _sds((1, W_GRP), F32), _sds((W_GRP, W_GRP), F32),
                   _sds((W_GRP, W_GRP), F32), _sds((1, W_GRP), F32), _sds((1, W_GRP), F32), _sds((1, W_GRP), F32)],
        scratch_shapes=[pltpu.VMEM((tm + 8, W_GRP), F32), pltpu.VMEM((tm, W_GRP), F32), pltpu.VMEM((tm, W_GRP), F32),
                        pltpu.VMEM((tm, W_GRP), F32), pltpu.VMEM((8, W_GRP), F32), pltpu.VMEM((8, W_GRP), F32),
                        pltpu.VMEM((8, W_GRP), F32)],
        input_output_aliases={15: 0}, name="mix_b_bwd", compiler_params=_cp("arbitrary"),
    )(z, z, z, hs, hs, dmix, conv_w, conv_b, wa, wx, wa_t, wx_t, ba, bx, lam, dz)


def _tri(n, lower):
    r = lax.broadcasted_iota(jnp.int32, (n, n), 0)
    c = lax.broadcasted_iota(jnp.int32, (n, n), 1)
    return jnp.where((r >= c) if lower else (r <= c), 1.0, 0.0).astype(BF16)


def _causal_stack():
    r = lax.broadcasted_iota(jnp.int32, (N_HEADS * HGRN_CHUNK, HGRN_CHUNK), 0)
    c = lax.broadcasted_iota(jnp.int32, (N_HEADS * HGRN_CHUNK, HGRN_CHUNK), 1)
    m = None
    for h in range(N_HEADS):
        mh = (r >= h * HGRN_CHUNK) & (r < (h + 1) * HGRN_CHUNK) & (r - h * HGRN_CHUNK >= c)
        m = mh if m is None else (m | mh)
    return m


def _stack_heads(x, hm):
    return jnp.concatenate([jnp.where(hm[h], x, 0.0) for h in range(N_HEADS)], axis=0)


def _unstack_heads(xs, hm):
    out = jnp.where(hm[0], xs[0:HGRN_CHUNK], 0.0)
    for h in range(1, N_HEADS):
        out = out + jnp.where(hm[h], xs[h * HGRN_CHUNK:(h + 1) * HGRN_CHUNK], 0.0)
    return out


def _hgrn_chunk(qv, fv, lb, tril):
    sq = _sigmoid(qv)
    qq = qv * sq
    sg = _sigmoid(fv)
    fg = lb + (1.0 - lb) * sg
    kk = 1.0 - fg
    bb = _dot_f32_lhs_exact(tril, jnp.log(fg))
    b_last = bb[HGRN_CHUNK - 1:HGRN_CHUNK, :]
    b_mid = bb[HGRN_CHUNK // 2 - 1:HGRN_CHUNK // 2, :]
    eq = jnp.exp(jnp.minimum(bb - b_mid, EXP_CLAMP))
    ek = jnp.exp(jnp.minimum(b_mid - bb, EXP_CLAMP))
    eb = jnp.exp(bb)
    el = jnp.exp(b_last - bb)
    return sq, qq, sg, fg, kk, b_last, eq, ek, eb, el


def _seg_mean(x, avg):
    return _dot_f32_rhs_exact(x, avg)


def _mix_c_fwd(z, mix, lb, ng):
    t = z.shape[0]
    tm = _mix_tm(t)
    nch = tm // HGRN_CHUNK

    def body(q_ref, f_ref, i_ref, g_ref, lb_ref, ng_ref, mix_in, y_ref, o_ref, ss_ref, s_ref):
        @pl.when(pl.program_id(0) == 0)
        def _():
            s_ref[...] = jnp.zeros_like(s_ref)

        hm = _head_masks()
        bmask = _block_mask()
        causal = _causal_stack()
        tril = _tri(HGRN_CHUNK, True)
        avg = jnp.where(bmask, 1.0 / HEAD_DIM, 0.0).astype(BF16)
        lb, ng = lb_ref[...], ng_ref[...]

        def chunk(c, carry):
            rows = pl.ds(pl.multiple_of(c * HGRN_CHUNK, HGRN_CHUNK), HGRN_CHUNK)
            vv = i_ref[rows, :]
            gv = g_ref[rows, :]
            _, qq, _, _, kk, b_last, eq, ek, eb, el = _hgrn_chunk(q_ref[rows, :], f_ref[rows, :], lb, tril)
            vb = vv.astype(BF16)
            qs = _stack_heads(qq * eq, hm).astype(BF16)
            att = jnp.where(causal, _dot_nt(qs, (kk * ek).astype(BF16)), 0.0)
            o = _unstack_heads(_dot(att.astype(BF16), vb), hm)
            s0 = s_ref[...]
            ss_ref[c] = s0
            o = o + _dot_nt((qq * eb).astype(BF16), s0.astype(BF16))
            s_ref[...] = s0 * jnp.exp(b_last) + jnp.where(bmask, _dot_tn(vb, (kk * el).astype(BF16)), 0.0)
            o_ref[rows, :] = o
            rstd = lax.rsqrt(_seg_mean(o * o, avg) + EPS)
            y_ref[rows, :] = o * rstd * ng * (gv * _sigmoid(gv))
            return carry

        lax.fori_loop(0, nch, chunk, 0, unroll=HGRN_UNROLL)

    vec = _full((1, W_GRP))
    return pl.pallas_call(
        body, grid=(t // tm,),
        in_specs=[_zblk(tm, 4), _zblk(tm, 5), _zblk(tm, 6), _zblk(tm, 7), vec, vec, pl.BlockSpec(memory_space=pl.ANY)],
        out_specs=[_zblk(tm, 2), pl.BlockSpec((tm, W_GRP), lambda i: (i, 0)),
                   pl.BlockSpec((nch, W_GRP, W_GRP), lambda i: (i, 0, 0))],
        out_shape=[_sds(mix.shape, F32), _sds((t, W_GRP), F32), _sds((t // HGRN_CHUNK, W_GRP, W_GRP), F32)],
        scratch_shapes=[pltpu.VMEM((W_GRP, W_GRP), F32)],
        input_output_aliases={6: 0}, name="mix_c_fwd", compiler_params=_cp("arbitrary"),
    )(z, z, z, z, lb, ng, mix)


def _mix_c_bwd(z, dz, dmix, o_pre, states, lb, ng):
    t = z.shape[0]
    tm = _mix_tm(t)
    nt = t // tm
    nch = tm // HGRN_CHUNK

    def body(q_ref, f_ref, i_ref, g_ref, o_ref, ss_ref, dy_ref, lb_ref, ng_ref, dz_in, dz_ref, dlb_ref, dng_ref, ds_ref):
        @pl.when(pl.program_id(0) == 0)
        def _():
            ds_ref[...] = jnp.zeros_like(ds_ref)
            dlb_ref[...] = jnp.zeros_like(dlb_ref)
            dng_ref[...] = jnp.zeros_like(dng_ref)

        hm = _head_masks()
        bmask = _block_mask()
        causal = _causal_stack()
        tril = _tri(HGRN_CHUNK, True)
        triu = _tri(HGRN_CHUNK, False)
        avg = jnp.where(bmask, 1.0 / HEAD_DIM, 0.0).astype(BF16)
        lb, ng = lb_ref[...], ng_ref[...]
        last_row = lax.broadcasted_iota(jnp.int32, (HGRN_CHUNK, W_GRP), 0) == HGRN_CHUNK - 1

        def chunk(j, carry):
            c = nch - 1 - j
            rows = pl.ds(pl.multiple_of(c * HGRN_CHUNK, HGRN_CHUNK), HGRN_CHUNK)
            qv, gv, vv = q_ref[rows, :], g_ref[rows, :], i_ref[rows, :]
            sq, qq, sg, fg, kk, b_last, eq, ek, eb, el = _hgrn_chunk(qv, f_ref[rows, :], lb, tril)
            s0 = ss_ref[c]
            ds1 = ds_ref[...]
            o = o_ref[rows, :]
            dy = dy_ref[rows, :]
            rstd = lax.rsqrt(_seg_mean(o * o, avg) + EPS)
            oh = o * rstd
            sgg = _sigmoid(gv)
            dz_ref[rows, 3 * W_GRP:4 * W_GRP] = dy * oh * ng * (sgg * (1.0 + gv * (1.0 - sgg)))
            don = dy * gv * sgg
            dng_ref[...] += jnp.sum(don * oh, axis=0, keepdims=True)
            doh = don * ng
            do = rstd * (doh - oh * _seg_mean(doh * oh, avg))
            qt, kt, qh, kh = qq * eq, kk * ek, qq * eb, kk * el
            vb, dob = vv.astype(BF16), do.astype(BF16)
            ktb, khb = kt.astype(BF16), kh.astype(BF16)
            ds1b = ds1.astype(BF16)
            qs = _stack_heads(qt, hm).astype(BF16)
            dos = _stack_heads(do, hm).astype(BF16)
            att = jnp.where(causal, _dot_nt(qs, ktb), 0.0).astype(BF16)
            datt = jnp.where(causal, _dot_nt(dos, vb), 0.0).astype(BF16)
            dv = _dot_tn(att, dos) + _dot_nt(khb, ds1b)
            dqt = _unstack_heads(_dot(datt, ktb), hm)
            dkt = _dot_tn(datt, qs)
            dqh = _dot(dob, s0.astype(BF16))
            dkh = _dot(vb, ds1b)
            e_last = jnp.exp(b_last)
            ds_ref[...] = ds1 * e_last + jnp.where(bmask, _dot_tn(dob, qh.astype(BF16)), 0.0)
            dq = dqt * eq + dqh * eb
            dk = dkt * ek + dkh * el
            db = qt * dqt - kt * dkt + qh * dqh - kh * dkh
            db_last = jnp.sum(kh * dkh, axis=0, keepdims=True) + e_last * jnp.sum(ds1 * s0, axis=0, keepdims=True)
            db = db + jnp.where(last_row, db_last, 0.0)
            dlogf = _dot_f32_lhs_exact(triu, db)
            dfg = dlogf / fg - dk
            dz_ref[rows, W_GRP:2 * W_GRP] = dfg * (1.0 - lb) * sg * (1.0 - sg)
            dlb_ref[...] += jnp.sum(dfg * (1.0 - sg), axis=0, keepdims=True)
            dz_ref[rows, 0:W_GRP] = dq * (sq * (1.0 + qv * (1.0 - sq)))
            dz_ref[rows, 2 * W_GRP:3 * W_GRP] = dv
            return carry

        lax.fori_loop(0, nch, chunk, 0, unroll=HGRN_UNROLL)

    vec = _full((1, W_GRP))
    rev = lambda col: _zblk(tm, col, nt)
    return pl.pallas_call(
        body, grid=(nt,),
        in_specs=[rev(4), rev(5), rev(6), rev(7), rev(0), pl.BlockSpec((nch, W_GRP, W_GRP), lambda i: (nt - 1 - i, 0, 0)),
                  rev(2), vec, vec, pl.BlockSpec(memory_space=pl.ANY)],
        out_specs=[pl.BlockSpec((tm, 4 * W_GRP), lambda i: (nt - 1 - i, 1)), vec, vec],
        out_shape=[_sds(dz.shape, F32), _sds((1, W_GRP), F32), _sds((1, W_GRP), F32)],
        scratch_shapes=[pltpu.VMEM((W_GRP, W_GRP), F32)],
        input_output_aliases={9: 0}, name="mix_c_bwd", compiler_params=_cp("arbitrary"),
    )(z, z, z, z, o_pre, states, dmix, lb, ng, dz)


def _pool_select(hm, s2, s4, s8, s16):
    return jnp.where(hm[0], s2, jnp.where(hm[1], s4, jnp.where(hm[2], s8, s16)))


def _pool_counts(hm, row0, tm):
    pos = (row0 + 1 + lax.broadcasted_iota(jnp.int32, (tm, W_GRP), 0)).astype(F32)
    win = _pool_select(hm, 2.0, 4.0, 8.0, 16.0)
    return jnp.minimum(pos, win)


def _pooled(ext_ref, x, halo, hm, cnt):
    ext_ref[0:POOL_HALO, :] = halo
    ext_ref[POOL_HALO:, :] = x
    e = ext_ref[...]
    s2 = e + pltpu.roll(e, 1, 0)
    s4 = s2 + pltpu.roll(s2, 2, 0)
    s8 = s4 + pltpu.roll(s4, 4, 0)
    s16 = s8 + pltpu.roll(s8, 8, 0)
    return _pool_select(hm, s2, s4, s8, s16)[POOL_HALO:] / cnt - x


def _mix_d_fwd(z, mix, wd, scale):
    t = z.shape[0]
    tm = _mix_tm(t)

    def body(x_ref, wd_ref, sc_ref, mix_in, o_ref, ext_ref, halo_ref):
        i = pl.program_id(0)

        @pl.when(i == 0)
        def _():
            halo_ref[...] = jnp.zeros_like(halo_ref)

        hm = _head_masks()
        x = x_ref[...]
        pooled = _pooled(ext_ref, x, halo_ref[...], hm, _pool_counts(hm, i * tm, tm))
        halo_ref[...] = x_ref[tm - POOL_HALO:tm, :]
        o_ref[...] = _dot(pooled.astype(BF16), wd_ref[...]) * sc_ref[...]

    return pl.pallas_call(
        body, grid=(t // tm,),
        in_specs=[_zblk(tm, 8), _full((W_GRP, W_GRP)), _full((1, W_GRP)), pl.BlockSpec(memory_space=pl.ANY)],
        out_specs=_zblk(tm, 3), out_shape=_sds(mix.shape, F32),
        scratch_shapes=[pltpu.VMEM((tm + POOL_HALO, W_GRP), F32), pltpu.VMEM((POOL_HALO, W_GRP), F32)],
        input_output_aliases={3: 0}, name="mix_d_fwd", compiler_params=_cp("arbitrary"),
    )(z, wd, scale, mix)


def _mix_d_bwd(z, dz, dmix, wd, wd_t, scale):
    t = z.shape[0]
    tm = _mix_tm(t)
    nt = t // tm
    hb = tm // POOL_HALO

    def body(x_ref, xhalo_ref, dy_ref, wd_ref, wdt_ref, sc_ref, dz_in, dz_ref, dwd_ref, dsc_ref, ext_ref, en_ref):
        i = pl.program_id(0)
        ri = nt - 1 - i

        @pl.when(i == 0)
        def _():
            en_ref[...] = jnp.zeros_like(en_ref)
            dwd_ref[...] = jnp.zeros_like(dwd_ref)
            dsc_ref[...] = jnp.zeros_like(dsc_ref)

        hm = _head_masks()
        cnt = _pool_counts(hm, ri * tm, tm)
        x = x_ref[...]
        pooled = _pooled(ext_ref, x, jnp.where(ri == 0, 0.0, xhalo_ref[...]), hm, cnt)
        pb = pooled.astype(BF16)
        dy = dy_ref[...]
        dsc_ref[...] += jnp.sum(dy * _dot(pb, wd_ref[...]), axis=0, keepdims=True)
        dyw = (dy * sc_ref[...]).astype(BF16)
        dwd_ref[...] += _dot_tn(pb, dyw)
        dpool = _dot(dyw, wdt_ref[...])
        e = dpool / cnt
        ext_ref[0:tm, :] = e
        ext_ref[tm:, :] = en_ref[...]
        en_ref[...] = e[0:POOL_HALO, :]
        ee = ext_ref[...]
        n = tm + POOL_HALO
        r2 = ee + pltpu.roll(ee, n - 1, 0)
        r4 = r2 + pltpu.roll(r2, n - 2, 0)
        r8 = r4 + pltpu.roll(r4, n - 4, 0)
        r16 = r8 + pltpu.roll(r8, n - 8, 0)
        dz_ref[...] = _pool_select(hm, r2, r4, r8, r16)[:tm] - dpool

    sq = _full((W_GRP, W_GRP))
    vec = _full((1, W_GRP))
    return pl.pallas_call(
        body, grid=(nt,),
        in_specs=[_zblk(tm, 8, nt), pl.BlockSpec((POOL_HALO, W_GRP), lambda i: (jnp.maximum((nt - 1 - i) * hb - 1, 0), 8)),
                  _zblk(tm, 3, nt), sq, sq, vec, pl.BlockSpec(memory_space=pl.ANY)],
        out_specs=[_zblk(tm, 8, nt), sq, vec],
        out_shape=[_sds(dz.shape, F32), _sds((W_GRP, W_GRP), F32), _sds((1, W_GRP), F32)],
        scratch_shapes=[pltpu.VMEM((tm + POOL_HALO, W_GRP), F32), pltpu.VMEM((POOL_HALO, W_GRP), F32)],
        input_output_aliases={6: 0}, name="mix_d_bwd", compiler_params=_cp("arbitrary"),
    )(z, z, dmix, wd, wd_t, scale, dz)


def _as2d(a):
    if a.ndim == 1:
        return a.reshape(1, a.shape[0])
    return a.reshape(-1, a.shape[-1])


def _adamw(w, g, m, v, name):
    shape = w.shape
    w2, g2, m2, v2 = _as2d(w), _as2d(g), _as2d(m), _as2d(v)
    rows, cols = w2.shape
    tr = _pick(rows, (1024, 512, 256, 128, 64, 32, 16, 8))
    if tr * cols * 4 * 14 > VMEM_LIMIT_BYTES:
        tr = _pick(rows, (256, 128, 64, 32, 16, 8))

    def body(w_ref, g_ref, m_ref, v_ref, d_ref, nm_ref, nv_ref):
        gv = g_ref[...]
        mn = ADAM_B1 * m_ref[...] + (1.0 - ADAM_B1) * gv
        vn = ADAM_B2 * v_ref[...] + (1.0 - ADAM_B2) * (gv * gv)
        m_hat = mn / (1.0 - ADAM_B1 ** ADAM_STEP)
        v_hat = vn / (1.0 - ADAM_B2 ** ADAM_STEP)
        d_ref[...] = -ADAM_LR * (m_hat / (jnp.sqrt(v_hat) + ADAM_EPS) + ADAM_WD * w_ref[...])
        nm_ref[...] = mn
        nv_ref[...] = vn

    blk = pl.BlockSpec((tr, cols), lambda i: (i, 0))
    outs = pl.pallas_call(
        body, grid=(rows // tr,), in_specs=[blk] * 4, out_specs=[blk] * 3, out_shape=[_sds((rows, cols), F32)] * 3,
        name=name, compiler_params=_cp("parallel"),
    )(w2, g2, m2, v2)
    return tuple(o.reshape(shape) for o in outs)


def _adamw_layer(w, g, m, v, layer, bufs, name):
    nl, r, cs = w.shape
    tr = _pick(r, (256, 128, 64, 32, 16, 8))

    def body(w_ref, g_ref, m_ref, v_ref, *rest):
        go_ref, d_ref, nm_ref, nv_ref = rest[-4:]
        gv = g_ref[...]
        mn = ADAM_B1 * m_ref[...] + (1.0 - ADAM_B1) * gv
        vn = ADAM_B2 * v_ref[...] + (1.0 - ADAM_B2) * (gv * gv)
        m_hat = mn / (1.0 - ADAM_B1 ** ADAM_STEP)
        v_hat = vn / (1.0 - ADAM_B2 ** ADAM_STEP)
        go_ref[...] = gv
        d_ref[...] = -ADAM_LR * (m_hat / (jnp.sqrt(v_hat) + ADAM_EPS) + ADAM_WD * w_ref[...])
        nm_ref[...] = mn
        nv_ref[...] = vn

    lay = pl.BlockSpec((None, tr, cs), lambda i: (layer, i, 0))
    in_specs = [lay, pl.BlockSpec((tr, cs), lambda i: (i, 0)), lay, lay]
    args = [w, g, m, v]
    aliases = {}
    if bufs is not None:
        in_specs += [pl.BlockSpec(memory_space=pl.ANY)] * 4
        args += list(bufs)
        aliases = {4 + i: i for i in range(4)}
    return pl.pallas_call(
        body, grid=(r // tr,), in_specs=in_specs, out_specs=[lay] * 4, out_shape=[_sds((nl, r, cs), F32)] * 4,
        input_output_aliases=aliases, name=name, compiler_params=_cp("parallel"),
    )(*args)


def _slot_sum(own, slots, name):
    n_slots, rows, cols = slots.shape
    whole_fits = rows * cols * 4 * (n_slots + 2) * 2 <= VMEM_LIMIT_BYTES // 2
    tr = rows if whole_fits else _pick(rows, (512, 352, 256, 128, 64, 32, 16, 8))

    def body(*refs):
        s_ref, o_ref = refs[-2], refs[-1]
        acc = s_ref[0].astype(F32) if own is None else refs[0][...].astype(F32) + s_ref[0].astype(F32)
        for k in range(1, n_slots):
            acc = acc + s_ref[k].astype(F32)
        o_ref[...] = acc

    row = pl.BlockSpec((tr, cols), lambda i: (i, 0))
    return pl.pallas_call(
        body, grid=(rows // tr,),
        in_specs=([] if own is None else [row]) + [pl.BlockSpec((n_slots, tr, cols), lambda i: (0, i, 0))],
        out_specs=row, out_shape=_sds((rows, cols), F32), name=name, compiler_params=_cp("parallel"),
    )(*(() if own is None else (own,)), slots)


def _me():
    return lax.axis_index("x"), lax.axis_index("y"), lax.axis_index("c")


def _other_chips(x, y):
    return [(1 - x, y), (x, 1 - y), (1 - x, 1 - y)]


ANY_SPEC = pl.BlockSpec(memory_space=pl.ANY)
HBM_SPEC = pl.BlockSpec(memory_space=pltpu.HBM)
SEM_SPEC = pl.BlockSpec(memory_space=pltpu.SEMAPHORE)
SPLIT_COPY_PARAMS = pltpu.CompilerParams(has_side_effects=pltpu.SideEffectType.DATAFLOW_SIDE_EFFECTING)
N_CHIPS = 4


def _aligned(v, m):
    return v if isinstance(v, int) else pl.multiple_of(v, m)


def _in_hbm(arr):
    return pltpu.with_memory_space_constraint(arr, pltpu.HBM)


def _peer(x, y, c, k):
    fx, fy, fc = (k >> 2) & 1, (k >> 1) & 1, k & 1
    px = 1 - x if fx else x
    py = 1 - y if fy else y
    pc = 1 - c if fc else c
    return px, py, pc


def _gather_start(shards, after, name):
    n = len(shards)

    def body(*refs):
        src, land = refs[:n], refs[n:2 * n]
        send_sems, recv_sems = refs[2 * n + 1], refs[2 * n + 2]
        token = refs[-1]
        x, y, c = _me()
        for w in range(n):
            for chip in _other_chips(x, y):
                pltpu.make_async_remote_copy(
                    src_ref=src[w], dst_ref=land[w].at[2 * x + y], send_sem=send_sems.at[w], recv_sem=recv_sems.at[w],
                    device_id=(*chip, c), device_id_type=MESH_ID).start()
        token[...] = jnp.zeros_like(token)

    lands = [lax.empty((N_CHIPS,) + s.shape, s.dtype) for s in shards]
    thru = [pltpu.HBM(s.shape, s.dtype) for s in shards] + [pltpu.HBM(z.shape, z.dtype) for z in lands]
    outs = pl.pallas_call(
        body, name=name,
        out_shape=(pltpu.SemaphoreType.DMA((n,)), pltpu.SemaphoreType.DMA((n,)), *thru, _sds((8, 128), F32)),
        in_specs=[HBM_SPEC] * (2 * n) + [ANY_SPEC],
        out_specs=(SEM_SPEC, SEM_SPEC, *[HBM_SPEC] * (2 * n), pl.BlockSpec(memory_space=pltpu.VMEM)),
        input_output_aliases={i: 2 + i for i in range(2 * n)}, compiler_params=SPLIT_COPY_PARAMS,
    )(*[_in_hbm(s) for s in shards], *[_in_hbm(z) for z in lands], after)
    return (outs[0], outs[1], outs[2:2 + n], outs[2 + n:2 + 2 * n]), outs[-1]


def _gather_wait(send_sems, recv_sems, srcs, lands, after, name):
    n = len(srcs)

    def body(*refs):
        land = refs[n:2 * n]
        send_sems, recv_sems = refs[2 * n], refs[2 * n + 1]
        x, y, c = _me()
        for w in range(n):
            three = land[w].at[pl.ds(0, N_CHIPS - 1)]
            cp = pltpu.make_async_remote_copy(src_ref=three, dst_ref=three, send_sem=send_sems.at[w], recv_sem=recv_sems.at[w],
                                              device_id=(x, y, c), device_id_type=MESH_ID)
            cp.wait_send()
            cp.wait_recv()

    both = list(srcs) + list(lands)
    outs = pl.pallas_call(
        body, name=name, out_shape=tuple(pltpu.HBM(b.shape, b.dtype) for b in both),
        in_specs=[HBM_SPEC] * (2 * n) + [SEM_SPEC, SEM_SPEC, ANY_SPEC], out_specs=[HBM_SPEC] * (2 * n),
        input_output_aliases={i: i for i in range(2 * n)}, compiler_params=SPLIT_COPY_PARAMS,
    )(*both, send_sems, recv_sems, after)
    return outs[n:2 * n]


def _push_start(grads, small, name):
    n = len(grads)
    srcs = list(grads) + ([] if small is None else [small])
    ns = len(srcs)

    def body(*refs):
        src, slots = refs[:ns], refs[ns:2 * ns]
        send_sems, recv_sems = refs[2 * ns], refs[2 * ns + 1]
        token = refs[-1]
        x, y, c = _me()
        for w in range(ns):
            for k in range(1, N_DEV):
                px, py, pc = _peer(x, y, c, k)
                if w < n:
                    hr = src[w].shape[1] // 2
                    piece = src[w].at[2 * px + py, pl.ds(_aligned(pc * hr, 16), hr), :]
                    slot = slots[w].at[k - 1]
                else:
                    piece = src[w]
                    slot = slots[w].at[4 * x + 2 * y + c]
                pltpu.make_async_remote_copy(
                    src_ref=piece, dst_ref=slot, send_sem=send_sems.at[w], recv_sem=recv_sems.at[w],
                    device_id=(px, py, pc), device_id_type=MESH_ID).start()
        token[...] = jnp.zeros_like(token)

    slots = [lax.empty((N_DEV - 1, g.shape[1] // 2, g.shape[2]), g.dtype) for g in grads]
    if small is not None:
        slots.append(lax.empty((N_DEV,) + small.shape, small.dtype))
    both = srcs + slots
    outs = pl.pallas_call(
        body, name=name,
        out_shape=(pltpu.SemaphoreType.DMA((ns,)), pltpu.SemaphoreType.DMA((ns,)),
                   *[pltpu.HBM(b.shape, b.dtype) for b in both], _sds((8, 128), F32)),
        in_specs=[HBM_SPEC] * len(both),
        out_specs=(SEM_SPEC, SEM_SPEC, *[HBM_SPEC] * len(both), pl.BlockSpec(memory_space=pltpu.VMEM)),
        input_output_aliases={i: 2 + i for i in range(len(both))}, compiler_params=SPLIT_COPY_PARAMS,
    )(*[_in_hbm(b) for b in both])
    return (outs[0], outs[1], outs[2:2 + ns], outs[2 + ns:2 + 2 * ns]), outs[-1]


def _push_wait(send_sems, recv_sems, srcs, slots, after, name):
    n = len(srcs)

    def body(*refs):
        slot = refs[n:2 * n]
        send_sems, recv_sems = refs[2 * n], refs[2 * n + 1]
        x, y, c = _me()
        for w in range(n):
            seven = slot[w].at[pl.ds(0, N_DEV - 1)]
            cp = pltpu.make_async_remote_copy(src_ref=seven, dst_ref=seven, send_sem=send_sems.at[w],
                                              recv_sem=recv_sems.at[w], device_id=(x, y, c), device_id_type=MESH_ID)
            cp.wait_send()
            cp.wait_recv()

    both = list(srcs) + list(slots)
    outs = pl.pallas_call(
        body, name=name, out_shape=tuple(pltpu.HBM(b.shape, b.dtype) for b in both),
        in_specs=[HBM_SPEC] * (2 * n) + [SEM_SPEC, SEM_SPEC, ANY_SPEC], out_specs=[HBM_SPEC] * (2 * n),
        input_output_aliases={i: i for i in range(2 * n)}, compiler_params=SPLIT_COPY_PARAMS,
    )(*both, send_sems, recv_sems, after)
    return outs[:n], outs[n:]


SWAP_CHUNK_BYTES = 2 * 1024 * 1024


def _swap_chunk_rows(hr, cs):
    ch = hr
    while ch * cs * 4 > SWAP_CHUNK_BYTES and ch % 16 == 0:
        ch //= 2
    return ch


def _swap_halves(halves, name):
    n = len(halves)
    chunk = [_swap_chunk_rows(*h.shape) for h in halves]
    rounds = max(h.shape[0] // ch for h, ch in zip(halves, chunk))

    def body(*refs):
        src, dst, buf = refs[:n], refs[n:2 * n], refs[2 * n:3 * n]
        load_sems, put_sems, send_sems, recv_sems = refs[3 * n:]
        x, y, c = _me()
        sibling = (x, y, 1 - c)
        for j in range(rounds):
            live = [w for w in range(n) if j < src[w].shape[0] // chunk[w]]
            loads = [pltpu.make_async_copy(src[w].at[pl.ds(j * chunk[w], chunk[w])], buf[w], load_sems.at[w]) for w in live]
            for ld in loads:
                ld.start()
            moves = []
            for ld, w in zip(loads, live):
                ld.wait()
                rows = pl.ds(_aligned(c * src[w].shape[0] + j * chunk[w], 8), chunk[w])
                put = pltpu.make_async_copy(buf[w], dst[w].at[rows], put_sems.at[w])
                send = pltpu.make_async_remote_copy(src_ref=buf[w], dst_ref=dst[w].at[rows], send_sem=send_sems.at[w],
                                                    recv_sem=recv_sems.at[w], device_id=sibling, device_id_type=MESH_ID)
                put.start()
                send.start()
                moves.append((put, send))
            for put, send in moves:
                put.wait()
                send.wait_send()
        for w in range(n):
            hr = src[w].shape[0]
            got = dst[w].at[pl.ds(_aligned((1 - c) * hr, 8), hr)]
            pltpu.make_async_remote_copy(src_ref=got, dst_ref=got, send_sem=send_sems.at[w], recv_sem=recv_sems.at[w],
                                         device_id=sibling, device_id_type=MESH_ID).wait_recv()

    return pl.pallas_call(
        body, in_specs=[ANY_SPEC] * n, out_specs=[ANY_SPEC] * n,
        out_shape=[_sds((2 * h.shape[0], h.shape[1]), F32) for h in halves],
        scratch_shapes=[pltpu.VMEM((ch, h.shape[1]), F32) for h, ch in zip(halves, chunk)]
        + [pltpu.SemaphoreType.DMA((n,))] * 4,
        name=name,
    )(*halves)


BIG = ("w_in", "w_out", "w_up", "w_down", "w_pe", "w_pg")
ROW_SHARDED = ("w_out", "w_down", "w_pg")
SMALL = ("norm1_g", "a_ln_g", "a_ln_b", "a_ws", "a_bs", "b_conv_w", "b_conv_b", "b_wa", "b_ba", "b_wx", "b_bx", "b_lam",
         "c_lb", "c_norm_g", "d_w", "d_scale", "norm2_g", "ffn_conv_w", "ffn_conv_b", "norm3_g", "final_g")
SMALL_SHARDED = ("b_conv_w", "ffn_conv_w")
WEIGHTS = ("norm1_g", "w_in", "a_ln_g", "a_ln_b", "a_ws", "a_bs", "b_conv_w", "b_conv_b", "b_wa", "b_ba", "b_wx", "b_bx",
           "b_lam", "c_lb", "c_norm_g", "d_w", "d_scale", "w_out", "norm2_g", "w_up", "ffn_conv_w", "ffn_conv_b", "w_down",
           "norm3_g", "w_pe", "w_pg", "final_g")
ARGS = ("x", "p") + WEIGHTS + ("loss_target",) + tuple("m_" + n for n in WEIGHTS) + tuple("v_" + n for n in WEIGHTS)


def _block_diag(w):
    eye = jnp.eye(N_HEADS, dtype=w.dtype)
    return (eye[None, :, None, :, None] * w[:, :, :, None, :]).reshape(w.shape[0], W_GRP, W_GRP)


def _diag_blocks(m):
    m4 = m.reshape(N_HEADS, HEAD_DIM, N_HEADS, HEAD_DIM)
    return jnp.stack([m4[h, :, h, :] for h in range(N_HEADS)])


def _lower_bounds(c_lb):
    lbs = jnp.cumsum(jax.nn.softmax(c_lb, axis=0), axis=0)
    return lbs - lbs[0:1]


def kernel(x, p, norm1_g, w_in, a_ln_g, a_ln_b, a_ws, a_bs, b_conv_w, b_conv_b, b_wa, b_ba, b_wx, b_bx, b_lam, c_lb, c_norm_g, d_w, d_scale, w_out, norm2_g, w_up, ffn_conv_w, ffn_conv_b, w_down, norm3_g, w_pe, w_pg, final_g, loss_target, m_norm1_g, m_w_in, m_a_ln_g, m_a_ln_b, m_a_ws, m_a_bs, m_b_conv_w, m_b_conv_b, m_b_wa, m_b_ba, m_b_wx, m_b_bx, m_b_lam, m_c_lb, m_c_norm_g, m_d_w, m_d_scale, m_w_out, m_norm2_g, m_w_up, m_ffn_conv_w, m_ffn_conv_b, m_w_down, m_norm3_g, m_w_pe, m_w_pg, m_final_g, v_norm1_g, v_w_in, v_a_ln_g, v_a_ln_b, v_a_ws, v_a_bs, v_b_conv_w, v_b_conv_b, v_b_wa, v_b_ba, v_b_wx, v_b_bx, v_b_lam, v_c_lb, v_c_norm_g, v_d_w, v_d_scale, v_w_out, v_norm2_g, v_w_up, v_ffn_conv_w, v_ffn_conv_b, v_w_down, v_norm3_g, v_w_pe, v_w_pg, v_final_g):
    return _step((x, p, norm1_g, w_in, a_ln_g, a_ln_b, a_ws, a_bs, b_conv_w, b_conv_b, b_wa, b_ba, b_wx, b_bx, b_lam, c_lb, c_norm_g, d_w, d_scale, w_out, norm2_g, w_up, ffn_conv_w, ffn_conv_b, w_down, norm3_g, w_pe, w_pg, final_g, loss_target, m_norm1_g, m_w_in, m_a_ln_g, m_a_ln_b, m_a_ws, m_a_bs, m_b_conv_w, m_b_conv_b, m_b_wa, m_b_ba, m_b_wx, m_b_bx, m_b_lam, m_c_lb, m_c_norm_g, m_d_w, m_d_scale, m_w_out, m_norm2_g, m_w_up, m_ffn_conv_w, m_ffn_conv_b, m_w_down, m_norm3_g, m_w_pe, m_w_pg, m_final_g, v_norm1_g, v_w_in, v_a_ln_g, v_a_ln_b, v_a_ws, v_a_bs, v_b_conv_w, v_b_conv_b, v_b_wa, v_b_ba, v_b_wx, v_b_bx, v_b_lam, v_c_lb, v_c_norm_g, v_d_w, v_d_scale, v_w_out, v_norm2_g, v_w_up, v_ffn_conv_w, v_ffn_conv_b, v_w_down, v_norm3_g, v_w_pe, v_w_pg, v_final_g))


SMALL_PER_LAYER = tuple(n for n in SMALL if n != "final_g")
GATHERED = BIG + SMALL_SHARDED
GATHER_LAYER0 = (("a", ("w_in", "b_conv_w")), ("b", ("w_out", "w_up", "ffn_conv_w", "w_down", "w_pe", "w_pg")))
PUSH_EARLY = ("w_pe", "w_pg", "w_down", "w_up")
PUSH_MID = ("w_out",)
PUSH_LATE = ("w_in",)
SMALL_MID = tuple(n for n in SMALL_PER_LAYER if n != "norm1_g")


def _cols_to_slabs(m):
    r, c4 = m.shape
    return jnp.moveaxis(m.reshape(r, N_CHIPS, c4 // N_CHIPS), 1, 0)


def _slabs_to_cols(s):
    return jnp.moveaxis(s, 0, 1).reshape(s.shape[1], -1)


def _pack_small(parts):
    flat = jnp.concatenate([p.reshape(-1) for p in parts])
    return jnp.pad(flat, (0, (-flat.shape[0]) % 1024)).reshape(-1, 128)


def _step(args):
    a = dict(zip(ARGS, args, strict=True))
    x0 = a["x"][0]
    target = a["loss_target"][0]
    nl = a["norm1_g"].shape[0]
    t, d = x0.shape
    f = a["w_down"].shape[1] * N_CHIPS
    cx, cy, cc = _me()
    my_shard = 2 * cx + cy
    shards = {n: a[n].astype(BF16) for n in BIG}
    shards.update({n: a[n] for n in SMALL_SHARDED})

    def start_gather(l, names, after, tag):
        return _gather_start([shards[n][l] for n in names], after, f"gather_start_{l}{tag}")

    def finish_gather(l, names, handle, after, tag):
        send, recv, srcs, lands = handle
        lands = _gather_wait(send, recv, srcs, lands, after, f"gather_wait_{l}{tag}")
        w = {}
        for n, land in zip(names, lands):
            full = lax.dynamic_update_slice(land, shards[n][l][None], (my_shard, 0, 0))
            if n in ROW_SHARDED:
                w[n] = full.reshape(-1, full.shape[-1])
            elif n in ("w_up", "w_pe"):
                w[n] = full
            else:
                w[n] = _slabs_to_cols(full)
        return w

    lbs, lbs_vjp = jax.vjp(_lower_bounds, a["c_lb"])
    tril = jnp.tril(jnp.ones((GMLP_CHUNK, GMLP_CHUNK), F32))

    stacked_params = {"wm": (a["a_ws"] * tril).astype(BF16)}
    stacked_params["wm_t"] = jnp.swapaxes(stacked_params["wm"], 2, 3)
    stacked_params["bs_t"] = jnp.repeat(jnp.swapaxes(a["a_bs"], 1, 2), HEAD_DIM, axis=2)
    for nm in ("b_wa", "b_wx", "d_w"):
        bd = _block_diag(a[nm]).astype(BF16)
        stacked_params[nm], stacked_params[nm + "_t"] = bd, jnp.swapaxes(bd, 1, 2)
    for nm in ("a_ln_g", "a_ln_b", "b_conv_b", "b_ba", "b_bx", "b_lam", "d_scale"):
        stacked_params[nm] = a[nm].reshape(nl, 1, W_GRP)
    stacked_params["lb"] = lbs.reshape(nl, 1, W_GRP)
    stacked_params["ng"] = jnp.tile(a["c_norm_g"], (1, N_HEADS)).reshape(nl, 1, W_GRP)
    stacked_params["ffn_conv_b"] = a["ffn_conv_b"].reshape(nl, 1, 2 * f)

    def layer_params(l, w):
        q = {k: v[l] for k, v in stacked_params.items()}
        q.update(w)
        return q

    saved, weights, params = [], [], []
    first_groups = {tag: start_gather(0, names, x0, tag)[0] for tag, names in GATHER_LAYER0}
    xl = x0
    for l in range(nl):
        if l == 0:
            w = finish_gather(0, GATHER_LAYER0[0][1], first_groups["a"], xl, "a")
        else:
            w = finish_gather(l, GATHERED, next_handle, xl, "")
        s = {"x0": xl}
        s["h1"] = _rms_fwd(xl, a["norm1_g"][0], "rms1_fwd") if l == 0 else h_next
        token = None
        if 0 < l < nl - 1:
            next_handle, token = start_gather(l + 1, GATHERED, s["h1"], "")
        s["z"] = _mm(s["h1"], w["w_in"], "nn", out_dtype=F32, name="mm_z", after=token)
        q = layer_params(l, w)
        mix = _mix_a_fwd(s["z"], d, q["a_ln_g"], q["a_ln_b"], q["wm"], q["bs_t"])
        mix, s["hs"] = _mix_b_fwd(s["z"], mix, q["b_conv_w"], q["b_conv_b"], q["b_wa"], q["b_wx"], q["b_ba"], q["b_bx"],
                                  q["b_lam"])
        mix, s["o_pre"], s["states"] = _mix_c_fwd(s["z"], mix, q["lb"], q["ng"])
        s["mix"] = _mix_d_fwd(s["z"], mix, q["d_w"], q["d_scale"])
        def land(tag, after):
            if l == 0:
                w.update(finish_gather(0, dict(GATHER_LAYER0)[tag], first_groups[tag], after, tag))
                q.update(w)

        land("b", s["mix"])
        token = None
        if l == 0 and nl > 1:
            next_handle, token = start_gather(1, GATHERED, w["w_out"], "")
        s["x1"], s["h2"] = _mm(s["mix"], w["w_out"], "nn", res=xl, out_dtype=F32, name="mm_out",
                               norm_g=a["norm2_g"][l], tm_max=1024, after=token)
        s["hf_g"] = _mm(s["h2"], w["w_up"], "nn", b_slabs=True, n=f, out_dtype=F32, name="mm_up_g")
        s["hf_v"] = _mm(s["h2"], w["w_up"], "nn", b_slabs=True, n=f, b_noff=f, out_dtype=F32, name="mm_up_v")
        s["act"], s["gc"], s["vc"] = _ffn_act_fwd(s["hf_g"], s["hf_v"], q["ffn_conv_w"], q["ffn_conv_b"])
        s["x2"], s["h3"] = _mm(s["act"], w["w_down"], "nn", res=s["x1"], out_dtype=F32, name="mm_down",
                               norm_g=a["norm3_g"][l], tm_max=1024)
        s["pre"] = _mm(s["h3"], w["w_pg"], "nn", out_dtype=F32, name="mm_pg")
        s["pe"] = _mm(a["p"][l, 0], w["w_pe"], "nn", b_slabs=True, out_dtype=F32, name="mm_pe")
        xl, h_next = _ple_fwd(s["x2"], s["pe"], s["pre"], a["norm1_g"][l + 1] if l + 1 < nl else None)
        saved.append(s)
        weights.append(w)
        params.append(q)

    dx, g_final, loss = _final_loss(xl, a["final_g"], target)
    loss = lax.psum(loss[0, 0], ("x", "y", "c"))

    stacked = {n: None for n in BIG}
    small_sums = {}

    def finish_push(l, names, handle, tag, after):
        send, recv, srcs, slots = handle
        srcs, slots = _push_wait(send, recv, srcs, slots, after, f"push_wait_{l}{tag}")
        halves = []
        for n, g, sl in zip(names, srcs, slots):
            hr = g.shape[1] // 2
            own = lax.dynamic_slice(g, (my_shard, cc * hr, 0), (1, hr, g.shape[2]))[0]
            halves.append(_slot_sum(own, sl, "sum_" + n))
        for n, g in zip(names, _swap_halves(halves, "swap_halves_" + tag)):
            stacked[n] = _adamw_layer(a[n], g, a["m_" + n], a["v_" + n], l, stacked[n], "adamw_" + n)
        if len(srcs) > len(names):
            by_sender = lax.dynamic_update_slice(slots[-1], srcs[-1][None], (2 * my_shard + cc, 0, 0))
            small_sums[l, tag] = _slot_sum(None, by_sender, "sum_small_" + tag)
        return stacked[names[-1]][1]

    pending = []
    token = None
    for l in reversed(range(nl)):
        q, s, w = params[l], saved[l], weights[l]
        gs = {}
        dpe, dpre = _ple_bwd(dx, s["pe"], s["pre"], after=token)
        g_pe = _mm(a["p"][l, 0], dpe, "tn", out_dtype=BF16, name="mm_dwpe", out_slabs=N_CHIPS)
        g_pg = _mm(s["h3"], dpre, "tn", out_dtype=BF16, name="mm_dwpg")
        dx2, gs["norm3_g"] = _mm(dpre, w["w_pg"], "nt", out_dtype=F32, name="mm_dh3", tm_max=1024,
                                 rms_bwd=(s["x2"], a["norm3_g"][l], dx))
        g_down = _mm(s["act"], dx2, "tn", out_dtype=BF16, name="mm_dwdown")
        dhf_g, dhf_v, sums_g, sums_v = _ffn_bwd(dx2, w["w_down"], s["gc"], s["vc"], s["hf_g"], s["hf_v"], q["ffn_conv_w"])
        gs["ffn_conv_w"] = jnp.concatenate([sums_g[0:3], sums_v[0:3]], axis=1)
        gs["ffn_conv_b"] = jnp.concatenate([sums_g[3:4], sums_v[3:4]], axis=1)
        g_up = _mm(s["h2"], dhf_g, "tn", out_dtype=BF16, name="mm_dwup_g", out_slabs=N_CHIPS, out_n=2 * f)
        g_up = _mm(s["h2"], dhf_v, "tn", out_dtype=BF16, name="mm_dwup_v", out_slabs=N_CHIPS, out_n=2 * f, o_noff=f, out_buf=g_up)
        early = {"w_pe": g_pe, "w_up": g_up, "w_pg": g_pg.reshape(N_CHIPS, -1, g_pg.shape[-1]),
                 "w_down": g_down.reshape(N_CHIPS, -1, g_down.shape[-1])}
        early_handle, token = _push_start([early[n] for n in PUSH_EARLY], None, f"push_start_{l}a")
        dh2 = _mm(dhf_g, w["w_up"], "nt", b_slabs=True, out_dtype=F32, name="mm_dh2_g", after=token)
        dx1, gs["norm2_g"] = _mm(dhf_v, w["w_up"], "nt", b_slabs=True, b_koff=f, res=dh2, out_dtype=F32, name="mm_dh2_v",
                                 tm_max=1024, rms_bwd=(s["x1"], a["norm2_g"][l], dx2))
        g_out = _mm(s["mix"], dx1, "tn", out_dtype=BF16, name="mm_dwout")
        dmix = _mm(dx1, w["w_out"], "nt", out_dtype=F32, name="mm_dmix")
        dz, gs["a_ln_g"], gs["a_ln_b"], dws, dbs_t = _mix_a_bwd(s["z"], dmix, q["a_ln_g"], q["a_ln_b"], q["wm"], q["wm_t"],
                                                               q["bs_t"])
        gs["a_ws"] = dws * tril
        gs["a_bs"] = dbs_t.reshape(GMLP_CHUNK, N_HEADS, HEAD_DIM).sum(-1).T
        dz, gs["b_conv_w"], gs["b_conv_b"], dwa, dwx, gs["b_ba"], gs["b_bx"], gs["b_lam"] = _mix_b_bwd(
            s["z"], dz, dmix, s["hs"], q["b_conv_w"], q["b_conv_b"], q["b_wa"], q["b_wx"], q["b_wa_t"], q["b_wx_t"],
            q["b_ba"], q["b_bx"], q["b_lam"])
        gs["b_wa"], gs["b_wx"] = _diag_blocks(dwa), _diag_blocks(dwx)
        dz, gs["c_lb"], dng = _mix_c_bwd(s["z"], dz, dmix, s["o_pre"], s["states"], q["lb"], q["ng"])
        gs["c_norm_g"] = dng.reshape(N_HEADS, HEAD_DIM).sum(0)
        dz, dwd, gs["d_scale"] = _mix_d_bwd(s["z"], dz, dmix, q["d_w"], q["d_w_t"], q["d_scale"])
        gs["d_w"] = _diag_blocks(dwd)
        small = [gs[n] for n in SMALL_MID] + ([g_final] if l == nl - 1 else [])
        mid_handle, token = _push_start([g_out.reshape(N_CHIPS, -1, g_out.shape[-1])], _pack_small(small), f"push_start_{l}b")
        g_in = _mm(s["h1"], dz, "tn", out_dtype=BF16, name="mm_dwin")
        dx, gs["norm1_g"] = _mm(dz, w["w_in"], "nt", out_dtype=F32, name="mm_dh1", after=(token, g_in), tm_max=1024,
                                rms_bwd=(s["x0"], a["norm1_g"][l], dx1))

        late_handle, token = _push_start([_cols_to_slabs(g_in)], _pack_small([gs["norm1_g"]]), f"push_start_{l}c")
        dep = token
        for push in pending:
            dep = finish_push(*push, dep)
        pending = [(l, PUSH_EARLY, early_handle, "a"), (l, PUSH_MID, mid_handle, "b"), (l, PUSH_LATE, late_handle, "c")]
    for push in pending:
        dep = finish_push(*push, dep)
    grad_x = dx[None]

    def small_shape(n):
        return a[n].shape[1:-1] + (a[n].shape[-1] * N_CHIPS,) if n in SMALL_SHARDED else a[n].shape[1:]

    per_layer = {n: [] for n in SMALL_PER_LAYER}
    for l in range(nl):
        per_layer["norm1_g"].append(small_sums[l, "c"].reshape(-1)[:d])
        vec, off = small_sums[l, "b"].reshape(-1), 0
        for n in SMALL_MID:
            shape = small_shape(n)
            size = 1
            for dim in shape:
                size *= dim
            per_layer[n].append(vec[off:off + size].reshape(shape))
            off += size
        if l == nl - 1:
            grad_final = vec[off:off + d]
    grads = {n: jnp.stack(per_layer[n]) for n in SMALL_PER_LAYER}
    grads["c_lb"] = lbs_vjp(grads["c_lb"])[0]
    grads["final_g"] = grad_final
    for n in SMALL_SHARDED:
        cs = a[n].shape[-1]
        grads[n] = lax.dynamic_slice_in_dim(grads[n], my_shard * cs, cs, axis=2)

    outs = {}
    for n in WEIGHTS:
        if n in BIG:
            outs[n] = stacked[n]
        else:
            outs[n] = (grads[n],) + _adamw(a[n], grads[n], a["m_" + n], a["v_" + n], "adamw_" + n)
    return (loss, grad_x, *[outs[n][0] for n in WEIGHTS], *[outs[n][1] for n in WEIGHTS], *[outs[n][2] for n in WEIGHTS],
            *[outs[n][3] for n in WEIGHTS])
```

```python
import functools

import jax
import jax.numpy as jnp
from jax import lax
from jax.experimental import pallas as pl
from jax.experimental.pallas import tpu as pltpu

F32 = jnp.float32
BF16 = jnp.bfloat16
EPS = 1e-6
HEAD_DIM = 64
N_HEADS = 4
W_GRP = HEAD_DIM * N_HEADS
GMLP_CHUNK = 128
HGRN_CHUNK = 64
HGRN_UNROLL = 8
RGLRU_C = 8.0
POOL_HALO = 16
EXP_CLAMP = 80.0
ADAM_LR, ADAM_B1, ADAM_B2, ADAM_EPS, ADAM_WD, ADAM_STEP = 0.001, 0.9, 0.999, 1e-08, 0.01, 10
VMEM_LIMIT_BYTES = 60 * 1024 * 1024
TILE_PREFS = (1024, 1408, 768, 512, 256, 128)
MM_ROW_TILES = (2048, 1024, 512, 256, 128, 64, 32, 16, 8)
MM_VMEM_BUDGET = 49 * 1024 * 1024
ROW_TILE_PREFS = (512, 256, 128, 64, 32, 16, 8)
MESH_ID = pl.DeviceIdType.MESH
N_DEV = 8


def _pick(n, prefs=TILE_PREFS):
    for p in prefs:
        if n % p == 0:
            return p
    return n


def _cp(*sem):
    return pltpu.CompilerParams(dimension_semantics=sem if sem else None, vmem_limit_bytes=VMEM_LIMIT_BYTES)


def _sds(shape, dtype):
    return jax.ShapeDtypeStruct(tuple(shape), dtype)


_GELU_C = 0.7978845608028654
_GELU_A = 0.044715


def _gelu(x):
    hx = 0.5 * x
    return hx + hx * jnp.tanh(x * (_GELU_C + (_GELU_C * _GELU_A) * (x * x)))


def _gelu_and_grad(x):
    x2 = x * x
    t = jnp.tanh(x * (_GELU_C + (_GELU_C * _GELU_A) * x2))
    hx = 0.5 * x
    g = hx + hx * t
    dg = (0.5 + 0.5 * t) + (hx * (1.0 - t * t)) * (_GELU_C + (3.0 * _GELU_C * _GELU_A) * x2)
    return g, dg


def _sigmoid(x):
    return 1.0 / (1.0 + jnp.exp(-x))


def _dot(a, b):
    return jnp.dot(a, b, preferred_element_type=F32)


def _dot_nt(a, b):
    return lax.dot_general(a, b, (((1,), (1,)), ((), ())), preferred_element_type=F32)


def _dot_tn(a, b):
    return lax.dot_general(a, b, (((0,), (0,)), ((), ())), preferred_element_type=F32)


def _split3(x):
    hi = x.astype(BF16)
    r1 = x - hi.astype(F32)
    mid = r1.astype(BF16)
    lo = (r1 - mid.astype(F32)).astype(BF16)
    return hi, mid, lo


def _dot_f32_rhs_exact(x, m_bf16):
    hi, mid, lo = _split3(x)
    return _dot(hi, m_bf16) + _dot(mid, m_bf16) + _dot(lo, m_bf16)


def _dot_f32_lhs_exact(m_bf16, x):
    hi, mid, lo = _split3(x)
    return _dot(m_bf16, hi) + _dot(m_bf16, mid) + _dot(m_bf16, lo)


def _head_masks(width=W_GRP):
    lane = lax.broadcasted_iota(jnp.int32, (1, width), 1)
    return [(lane >= h * HEAD_DIM) & (lane < (h + 1) * HEAD_DIM) for h in range(N_HEADS)]


def _block_mask(n=W_GRP):
    r = lax.broadcasted_iota(jnp.int32, (n, n), 0)
    c = lax.broadcasted_iota(jnp.int32, (n, n), 1)
    m = None
    for h in range(N_HEADS):
        mh = (r >= h * HEAD_DIM) & (r < (h + 1) * HEAD_DIM) & (c >= h * HEAD_DIM) & (c < (h + 1) * HEAD_DIM)
        m = mh if m is None else (m | mh)
    return m


def _mm(a, b, mode, *, out_dtype, name, res=None, b_slabs=False, n=None, b_noff=0, b_koff=0,
        out_slabs=0, out_buf=None, out_n=None, o_noff=0, after=(), norm_g=None, rms_bwd=None, tm_max=None):
    after = () if after is None else (tuple(after) if isinstance(after, (tuple, list)) else (after,))
    if mode == "tn":
        k_dim, m_dim = a.shape
    else:
        m_dim, k_dim = a.shape
    if mode == "nt":
        n_dim = b.shape[-2]
    else:
        n_dim = n if n is not None else (b.shape[0] * b.shape[2] if b_slabs else b.shape[1])
    n_total = out_n if out_n is not None else n_dim
    tm, tn, tk = _pick(m_dim), _pick(n_dim), _pick(k_dim)
    if tm_max is not None:
        tm = _pick(m_dim, tuple(p for p in TILE_PREFS if p <= tm_max))
    if b_slabs and mode == "nt":
        tk = _pick(b.shape[2])
    elif b_slabs:
        tn = _pick(b.shape[2])
    elif out_slabs:
        tn = _pick(n_total // out_slabs)
    def vmem_bytes(rows, kk):
        blocks = rows * kk * a.dtype.itemsize + kk * tn * b.dtype.itemsize + rows * tn * jnp.dtype(out_dtype).itemsize
        blocks += rows * tn * 4 * ((res is not None) + 2 * (rms_bwd is not None) + (norm_g is not None))
        return 2 * blocks + rows * tn * 4 * (1 if kk == k_dim else 2)

    row_cap = tm_max if tm_max is not None else (512 if mode == "tn" else MM_ROW_TILES[0])
    k_options = (tk,) if tk == k_dim else (k_dim, tk)
    choice = next(((rows, kk) for kk in k_options for rows in MM_ROW_TILES
                   if rows <= row_cap and m_dim % rows == 0 and rows >= min(256, m_dim) and vmem_bytes(rows, kk) <= MM_VMEM_BUDGET),
                  None)
    if choice is not None:
        tm, tk = choice
    nk = k_dim // tk
    assert b_noff % tn == 0 and b_koff % tk == 0 and o_noff % tn == 0 and n_dim % tn == 0 and k_dim % tk == 0
    bn0, bk0, on0 = b_noff // tn, b_koff // tk, o_noff // tn
    dims = {"nn": (((1,), (0,)), ((), ())), "nt": (((1,), (1,)), ((), ())), "tn": (((0,), (0,)), ((), ()))}[mode]

    if mode == "tn":
        a_spec = pl.BlockSpec((tk, tm), lambda i, j, k: (k, i))
    else:
        a_spec = pl.BlockSpec((tm, tk), lambda i, j, k: (i, k))
    slab_group = 0
    if not b_slabs:
        if mode == "nt":
            b_spec = pl.BlockSpec((tn, tk), lambda i, j, k: (j + bn0, k + bk0))
        else:
            b_spec = pl.BlockSpec((tk, tn), lambda i, j, k: (k + bk0, j + bn0))
    elif mode == "nt" and tk > b.shape[2]:
        slab_group = tk // b.shape[2]
        b_spec = pl.BlockSpec((slab_group, tn, b.shape[2]), lambda i, j, k: (bk0, j, 0))
    elif mode == "nt":
        bper = b.shape[2] // tk
        b_spec = pl.BlockSpec((None, tn, tk), lambda i, j, k: ((k + bk0) // bper, j, (k + bk0) % bper))
    else:
        bper = b.shape[2] // tn
        b_spec = pl.BlockSpec((None, tk, tn), lambda i, j, k: ((j + bn0) // bper, k, (j + bn0) % bper))
    in_specs = [a_spec, b_spec]
    args = [a, b]
    if res is not None:
        in_specs.append(pl.BlockSpec((tm, tn), lambda i, j, k: (i, j)))
        args.append(res)
    if out_slabs:
        oper = n_total // out_slabs // tn
        out_shape = _sds((out_slabs, m_dim, n_total // out_slabs), out_dtype)
        out_spec = pl.BlockSpec((None, tm, tn), lambda i, j, k: ((j + on0) // oper, i, (j + on0) % oper))
    else:
        out_shape = _sds((m_dim, n_total), out_dtype)
        out_spec = pl.BlockSpec((tm, tn), lambda i, j, k: (i, j + on0))
    out_specs, out_shapes = [out_spec], [out_shape]
    row_spec = pl.BlockSpec((tm, tn), lambda i, j, k: (i, 0))
    vec_spec = pl.BlockSpec((1, tn), lambda i, j, k: (0, 0))
    norm_at = rms_at = None
    if norm_g is not None:
        assert tn == n_dim and not out_slabs
        norm_at = len(args)
        in_specs.append(vec_spec)
        args.append(norm_g.reshape(1, n_dim))
        out_specs.append(row_spec)
        out_shapes.append(_sds((m_dim, n_dim), BF16))
    if rms_bwd is not None:
        assert tn == n_dim and not out_slabs and norm_g is None
        x_in, gain, dres = rms_bwd
        rms_at = len(args)
        in_specs += [row_spec, vec_spec, row_spec]
        args += [x_in, gain.reshape(1, n_dim), dres]
        out_specs, out_shapes = [row_spec, vec_spec], [_sds((m_dim, n_dim), F32), _sds((1, n_dim), F32)]
    aliases = {}
    if out_buf is not None:
        in_specs.append(pl.BlockSpec(memory_space=pl.ANY))
        args.append(out_buf)
        aliases = {len(args) - 1: 0}
    for dep in after:
        in_specs.append(pl.BlockSpec(memory_space=pl.ANY))
        args.append(dep)
    has_res = res is not None
    n_in = len(args)

    def body(*refs):
        a_ref, b_ref = refs[0], refs[1]
        res_ref = refs[2] if has_res else None
        o_ref = refs[n_in]
        acc_ref = refs[-1] if nk > 1 else None

        def product(rows):
            if slab_group:
                cs = b.shape[2]
                terms = [lax.dot_general(a_ref[rows, s * cs:(s + 1) * cs].astype(BF16), b_ref[s].astype(BF16), dims,
                                         preferred_element_type=F32) for s in range(slab_group)]
                return functools.reduce(lambda p, q: p + q, terms)
            lhs = a_ref[rows, :] if mode != "tn" else a_ref[:, rows]
            return lax.dot_general(lhs.astype(BF16), b_ref[...].astype(BF16), dims, preferred_element_type=F32)

        def finish(v, rows):
            if has_res:
                v = v + res_ref[rows, :]
            if rms_at is not None:
                xv, gv = refs[rms_at][rows, :], refs[rms_at + 1][...]
                r = lax.rsqrt(jnp.mean(xv * xv, axis=-1, keepdims=True) + EPS)
                dyg = v * gv
                dot = jnp.mean(dyg * xv, axis=-1, keepdims=True)
                o_ref[rows, :] = refs[rms_at + 2][rows, :] + r * dyg - xv * (r * r * r) * dot
                return jnp.sum(v * xv * r, axis=0, keepdims=True)
            o_ref[rows, :] = v.astype(o_ref.dtype)
            if norm_at is not None:
                r = lax.rsqrt(jnp.mean(v * v, axis=-1, keepdims=True) + EPS)
                refs[n_in + 1][rows, :] = (v * r * refs[norm_at][...]).astype(BF16)
            return None

        def add_gain_grad(gpart):
            dg_ref = refs[n_in + 1]
            first = pl.program_id(0) == 0

            @pl.when(first)
            def _():
                dg_ref[...] = gpart

            @pl.when(jnp.logical_not(first))
            def _():
                dg_ref[...] += gpart

        whole = slice(0, tm)
        if nk == 1:
            gpart = finish(product(whole), whole)
            if rms_at is not None:
                add_gain_grad(gpart)
        else:
            part = product(whole)
            kk = pl.program_id(2)

            @pl.when(kk == 0)
            def _():
                acc_ref[...] = part

            @pl.when(kk > 0)
            def _():
                acc_ref[...] += part

            @pl.when(kk == nk - 1)
            def _():
                gpart = finish(acc_ref[...], whole)
                if rms_at is not None:
                    add_gain_grad(gpart)

    outs = pl.pallas_call(
        body, grid=(m_dim // tm, n_dim // tn, nk), in_specs=in_specs, out_specs=out_specs, out_shape=out_shapes,
        scratch_shapes=[pltpu.VMEM((tm, tn), F32)] if nk > 1 else [],
        input_output_aliases=aliases, name=name,
        compiler_params=_cp(*(("arbitrary",) * 3 if rms_bwd is not None else ("parallel", "parallel", "arbitrary"))),
    )(*args)
    return outs[0] if len(outs) == 1 else tuple(outs)


def _rms_fwd(x, g, name):
    t, d = x.shape
    tm = _pick(t, ROW_TILE_PREFS)

    def body(x_ref, g_ref, o_ref):
        xv = x_ref[...]
        r = lax.rsqrt(jnp.mean(xv * xv, axis=-1, keepdims=True) + EPS)
        o_ref[...] = (xv * r * g_ref[...]).astype(o_ref.dtype)

    return pl.pallas_call(
        body, grid=(t // tm,),
        in_specs=[pl.BlockSpec((tm, d), lambda i: (i, 0)), pl.BlockSpec((1, d), lambda i: (0, 0))],
        out_specs=pl.BlockSpec((tm, d), lambda i: (i, 0)), out_shape=_sds((t, d), BF16),
        name=name, compiler_params=_cp("parallel"),
    )(x, g.reshape(1, d))


def _final_loss(x, g, target):
    t, d = x.shape
    tm = _pick(t, ROW_TILE_PREFS)

    def body(x_ref, g_ref, t_ref, dx_ref, dg_ref, loss_ref):
        i = pl.program_id(0)
        xv = x_ref[...]
        gv = g_ref[...]
        r = lax.rsqrt(jnp.mean(xv * xv, axis=-1, keepdims=True) + EPS)
        err = xv * r * gv - t_ref[...]
        lpart = (0.5 / d) * jnp.sum(jnp.sum(err * err, axis=1, keepdims=True), axis=0, keepdims=True)
        dy = err * (1.0 / d)
        dyg = dy * gv
        dot = jnp.mean(dyg * xv, axis=-1, keepdims=True)
        dx_ref[...] = r * dyg - xv * (r * r * r) * dot
        part = jnp.sum(dy * xv * r, axis=0, keepdims=True)

        @pl.when(i == 0)
        def _():
            dg_ref[...] = part
            loss_ref[...] = lpart

        @pl.when(i > 0)
        def _():
            dg_ref[...] += part
            loss_ref[...] += lpart

    row = pl.BlockSpec((tm, d), lambda i: (i, 0))
    vec = pl.BlockSpec((1, d), lambda i: (0, 0))
    return pl.pallas_call(
        body, grid=(t // tm,), in_specs=[row, vec, row], out_specs=[row, vec, pl.BlockSpec((1, 1), lambda i: (0, 0))],
        out_shape=[_sds((t, d), F32), _sds((1, d), F32), _sds((1, 1), F32)], name="final_loss",
        compiler_params=_cp("arbitrary"),
    )(x, g.reshape(1, d), target)


def _shift_down(ext, k, halo):
    return pltpu.roll(ext, k, 0)[halo:]


def _shift_up(ext, k, tm):
    return pltpu.roll(ext, ext.shape[0] - k, 0)[:tm]


def _ffn_tiles(t, f):
    return _pick(t, (256, 128, 64, 32, 16, 8)), _pick(f, (1408, 256, 128))


def _ffn_act_fwd(hf_g, hf_v, conv_w, conv_b):
    t, f = hf_g.shape
    tm, cn = _ffn_tiles(t, f)
    nf = f // cn

    def body(g_ref, v_ref, wg_ref, wv_ref, bg_ref, bv_ref, o_ref, gc_ref, vc_ref, ext_ref, hg_ref, hv_ref):
        i = pl.program_id(1)

        @pl.when(i == 0)
        def _():
            hg_ref[...] = jnp.zeros_like(hg_ref)
            hv_ref[...] = jnp.zeros_like(hv_ref)

        def conv(x_ref, halo_ref, w_ref, b_ref):
            ext_ref[0:8, :] = halo_ref[...]
            ext_ref[8:, :] = x_ref[...]
            halo_ref[...] = x_ref[tm - 8:tm, :]
            ext = ext_ref[...]
            w = w_ref[...]
            return b_ref[...] + w[2:3, :] * ext[8:] + w[1:2, :] * _shift_down(ext, 1, 8) + w[0:1, :] * _shift_down(ext, 2, 8)

        gc = conv(g_ref, hg_ref, wg_ref, bg_ref)
        vc = conv(v_ref, hv_ref, wv_ref, bv_ref)
        o_ref[...] = (_gelu(gc) * vc).astype(o_ref.dtype)
        gc_ref[...] = gc.astype(gc_ref.dtype)
        vc_ref[...] = vc.astype(vc_ref.dtype)

    blk = pl.BlockSpec((tm, cn), lambda j, i: (i, j))
    return pl.pallas_call(
        body, grid=(nf, t // tm),
        in_specs=[blk, blk, pl.BlockSpec((3, cn), lambda j, i: (0, j)), pl.BlockSpec((3, cn), lambda j, i: (0, j + nf)),
                  pl.BlockSpec((1, cn), lambda j, i: (0, j)), pl.BlockSpec((1, cn), lambda j, i: (0, j + nf))],
        out_specs=[blk, blk, blk], out_shape=[_sds((t, f), BF16)] * 3,
        scratch_shapes=[pltpu.VMEM((tm + 8, cn), F32), pltpu.VMEM((8, cn), F32), pltpu.VMEM((8, cn), F32)],
        name="ffn_act_fwd", compiler_params=_cp("parallel", "arbitrary"),
    )(hf_g, hf_v, conv_w, conv_w, conv_b, conv_b)


def _ffn_bwd(dx2, w_down, gc, vc, hf_g, hf_v, conv_w):
    t, f = hf_g.shape
    d = dx2.shape[1]
    tm, cn = _ffn_tiles(t, f)
    nf, nt = f // cn, t // tm

    def body(dx_ref, wd_ref, gc_ref, vc_ref, g_ref, v_ref, wg_ref, wv_ref, dg_ref, dv_ref, sg_ref, sv_ref,
             ext_ref, cg_ref, cv_ref):
        @pl.when(pl.program_id(1) == 0)
        def _():
            for ref in (cg_ref, cv_ref, sg_ref, sv_ref):
                ref[...] = jnp.zeros_like(ref)

        da = _dot_nt(dx_ref[...].astype(BF16), wd_ref[...])
        gel, dgel = _gelu_and_grad(gc_ref[...].astype(F32))
        dgc = da * vc_ref[...].astype(F32) * dgel
        dvc = da * gel

        def back(dc, carry_ref, w, x, out_ref, sums_ref):
            ext_ref[0:tm, :] = dc
            ext_ref[tm:tm + 8, :] = carry_ref[...]
            carry_ref[...] = dc[0:8, :]
            ext = ext_ref[...]
            up1, up2 = _shift_up(ext, 1, tm), _shift_up(ext, 2, tm)
            out_ref[...] = (w[2:3, :] * dc + w[1:2, :] * up1 + w[0:1, :] * up2).astype(out_ref.dtype)
            sums_ref[0:1, :] += jnp.sum(up2 * x, axis=0, keepdims=True)
            sums_ref[1:2, :] += jnp.sum(up1 * x, axis=0, keepdims=True)
            sums_ref[2:3, :] += jnp.sum(dc * x, axis=0, keepdims=True)
            sums_ref[3:4, :] += jnp.sum(dc, axis=0, keepdims=True)

        back(dgc, cg_ref, wg_ref[...], g_ref[...], dg_ref, sg_ref)
        back(dvc, cv_ref, wv_ref[...], v_ref[...], dv_ref, sv_ref)

    blk = pl.BlockSpec((tm, cn), lambda j, i: (nt - 1 - i, j))
    sums = pl.BlockSpec((8, cn), lambda j, i: (0, j))
    return pl.pallas_call(
        body, grid=(nf, nt),
        in_specs=[pl.BlockSpec((tm, d), lambda j, i: (nt - 1 - i, 0)), pl.BlockSpec((cn, d), lambda j, i: (j, 0)),
                  blk, blk, blk, blk, pl.BlockSpec((3, cn), lambda j, i: (0, j)), pl.BlockSpec((3, cn), lambda j, i: (0, j + nf))],
        out_specs=[blk, blk, sums, sums],
        out_shape=[_sds((t, f), BF16), _sds((t, f), BF16), _sds((8, f), F32), _sds((8, f), F32)],
        scratch_shapes=[pltpu.VMEM((tm + 8, cn), F32), pltpu.VMEM((8, cn), F32), pltpu.VMEM((8, cn), F32)],
        name="ffn_bwd", compiler_params=_cp("parallel", "arbitrary"),
    )(dx2, w_down, gc, vc, hf_g, hf_v, conv_w, conv_w)


def _ple_fwd(x2, pe, pre, next_g=None):
    t, d = x2.shape
    tm = _pick(t, ROW_TILE_PREFS)

    def body(x_ref, pe_ref, pre_ref, *rest):
        x3 = x_ref[...] + pe_ref[...] * _sigmoid(pre_ref[...])
        if next_g is None:
            rest[0][...] = x3
        else:
            g_ref, o_ref, h_ref = rest
            o_ref[...] = x3
            r = lax.rsqrt(jnp.mean(x3 * x3, axis=-1, keepdims=True) + EPS)
            h_ref[...] = (x3 * r * g_ref[...]).astype(h_ref.dtype)

    row = pl.BlockSpec((tm, d), lambda i: (i, 0))
    if next_g is None:
        return pl.pallas_call(body, grid=(t // tm,), in_specs=[row, row, row], out_specs=row,
                              out_shape=_sds((t, d), F32), name="ple_fwd", compiler_params=_cp("parallel"))(x2, pe, pre), None
    return pl.pallas_call(body, grid=(t // tm,), in_specs=[row, row, row, pl.BlockSpec((1, d), lambda i: (0, 0))],
                          out_specs=[row, row], out_shape=[_sds((t, d), F32), _sds((t, d), BF16)], name="ple_norm_fwd",
                          compiler_params=_cp("parallel"))(x2, pe, pre, next_g.reshape(1, d))


def _ple_bwd(dx3, pe, pre, after=None):
    t, d = dx3.shape
    tm = _pick(t, ROW_TILE_PREFS)

    def body(dx_ref, pe_ref, pre_ref, *rest):
        dpe_ref, dpre_ref = rest[-2:]
        gate = _sigmoid(pre_ref[...])
        dx = dx_ref[...]
        dpe_ref[...] = (dx * gate).astype(dpe_ref.dtype)
        dpre_ref[...] = (dx * pe_ref[...] * gate * (1.0 - gate)).astype(dpre_ref.dtype)

    row = pl.BlockSpec((tm, d), lambda i: (i, 0))
    extra = [] if after is None else [after]
    return pl.pallas_call(body, grid=(t // tm,), in_specs=[row, row, row] + [pl.BlockSpec(memory_space=pl.ANY)] * len(extra),
                          out_specs=[row, row], out_shape=[_sds((t, d), BF16), _sds((t, d), BF16)], name="ple_bwd",
                          compiler_params=_cp("parallel"))(dx3, pe, pre, *extra)


def _mix_tm(t):
    return _pick(t, (512, 256, 128))


def _zblk(tm, col, rev_nt=None):
    if rev_nt is None:
        return pl.BlockSpec((tm, W_GRP), lambda i: (i, col))
    return pl.BlockSpec((tm, W_GRP), lambda i: (rev_nt - 1 - i, col))


def _full(shape):
    nd = len(shape)
    return pl.BlockSpec(tuple(shape), lambda i: (0,) * nd)


def _gmlp_sv(wm_ref, vnc, bs, hm):
    sv = bs
    for h in range(N_HEADS):
        sv = sv + jnp.where(hm[h], _dot(wm_ref[h], vnc), 0.0)
    return sv


def _layernorm(v, g, b):
    mu = jnp.mean(v, axis=-1, keepdims=True)
    vc = v - mu
    rs = lax.rsqrt(jnp.mean(vc * vc, axis=-1, keepdims=True) + EPS)
    xhat = vc * rs
    return xhat, rs, xhat * g + b


def _mix_a_fwd(z, d_mix, ln_g, ln_b, wm, bs_t):
    t = z.shape[0]
    tm = _mix_tm(t)

    def body(u_ref, v_ref, g_ref, b_ref, wm_ref, bs_ref, o_ref):
        hm = _head_masks()
        ug = _gelu(u_ref[...])
        _, _, vn = _layernorm(_gelu(v_ref[...]), g_ref[...], b_ref[...])
        vnb = vn.astype(BF16)
        for n in range(tm // GMLP_CHUNK):
            sl = slice(n * GMLP_CHUNK, (n + 1) * GMLP_CHUNK)
            o_ref[sl, :] = ug[sl] * _gmlp_sv(wm_ref, vnb[sl], bs_ref[...], hm)

    return pl.pallas_call(
        body, grid=(t // tm,),
        in_specs=[_zblk(tm, 0), _zblk(tm, 1), _full((1, W_GRP)), _full((1, W_GRP)), _full(wm.shape), _full(bs_t.shape)],
        out_specs=_zblk(tm, 0), out_shape=_sds((t, d_mix), F32), name="mix_a_fwd", compiler_params=_cp("parallel"),
    )(z, z, ln_g, ln_b, wm, bs_t)


def _mix_a_bwd(z, dmix, ln_g, ln_b, wm, wm_t, bs_t):
    t, zc = z.shape
    tm = _mix_tm(t)

    def body(u_ref, v_ref, dy_ref, g_ref, b_ref, wm_ref, wmt_ref, bs_ref, dz_ref, dg_ref, db_ref, dws_ref, dbs_ref):
        i = pl.program_id(0)

        @pl.when(i == 0)
        def _():
            dg_ref[...] = jnp.zeros_like(dg_ref)
            db_ref[...] = jnp.zeros_like(db_ref)
            dws_ref[...] = jnp.zeros_like(dws_ref)
            dbs_ref[...] = jnp.zeros_like(dbs_ref)

        hm = _head_masks()
        ug, dug = _gelu_and_grad(u_ref[...])
        vg, dvg = _gelu_and_grad(v_ref[...])
        gv = g_ref[...]
        xhat, rs, vn = _layernorm(vg, gv, b_ref[...])
        vnb = vn.astype(BF16)
        dy = dy_ref[...]
        for n in range(tm // GMLP_CHUNK):
            sl = slice(n * GMLP_CHUNK, (n + 1) * GMLP_CHUNK)
            vnc = vnb[sl]
            sv = _gmlp_sv(wm_ref, vnc, bs_ref[...], hm)
            dsv = dy[sl] * ug[sl]
            dz_ref[sl, 0:W_GRP] = dy[sl] * sv * dug[sl]
            dbs_ref[...] += dsv
            dsvb = dsv.astype(BF16)
            dvn = jnp.zeros((GMLP_CHUNK, W_GRP), F32)
            for h in range(N_HEADS):
                dws_ref[h] += _dot_nt(jnp.where(hm[h], dsv, 0.0).astype(BF16), vnc)
                dvn = dvn + jnp.where(hm[h], _dot(wmt_ref[h], dsvb), 0.0)
            xh = xhat[sl]
            dg_ref[...] += jnp.sum(dvn * xh, axis=0, keepdims=True)
            db_ref[...] += jnp.sum(dvn, axis=0, keepdims=True)
            dxh = dvn * gv
            dvg_c = rs[sl] * (dxh - jnp.mean(dxh, axis=-1, keepdims=True) - xh * jnp.mean(dxh * xh, axis=-1, keepdims=True))
            dz_ref[sl, W_GRP:2 * W_GRP] = dvg_c * dvg[sl]

    return pl.pallas_call(
        body, grid=(t // tm,),
        in_specs=[_zblk(tm, 0), _zblk(tm, 1), _zblk(tm, 0), _full((1, W_GRP)), _full((1, W_GRP)), _full(wm.shape),
                  _full(wm_t.shape), _full(bs_t.shape)],
        out_specs=[pl.BlockSpec((tm, 2 * W_GRP), lambda i: (i, 0)), _full((1, W_GRP)), _full((1, W_GRP)),
                   _full(wm.shape), _full(bs_t.shape)],
        out_shape=[_sds((t, zc), F32), _sds((1, W_GRP), F32), _sds((1, W_GRP), F32), _sds(wm.shape, F32),
                   _sds(bs_t.shape, F32)],
        name="mix_a_bwd", compiler_params=_cp("arbitrary"),
    )(z, z, dmix, ln_g, ln_b, wm, wm_t, bs_t)


def _softplus(x):
    return jnp.maximum(x, 0.0) + jnp.log(1.0 + jnp.exp(-jnp.abs(x)))


def _neg_expm1(x):
    series = -x * (1.0 + x * 0.5 * (1.0 + x * (1.0 / 3.0) * (1.0 + x * 0.25 * (1.0 + x * 0.2))))
    return jnp.where(x > -0.1, series, 1.0 - jnp.exp(x))


def _rglru_gates(ext_ref, x_ref, halo, cw, cb, wa_ref, wx_ref, ba, bx, lam):
    ext_ref[0:8, :] = halo
    ext_ref[8:, :] = x_ref[...]
    ext = ext_ref[...]
    x0, x1, x2, x3 = ext[8:], _shift_down(ext, 1, 8), _shift_down(ext, 2, 8), _shift_down(ext, 3, 8)
    xc = cb + cw[3:4, :] * x0 + cw[2:3, :] * x1 + cw[1:2, :] * x2 + cw[0:1, :] * x3
    xcb = xc.astype(BF16)
    r = _sigmoid(_dot(xcb, wa_ref[...]) + ba)
    ig = _sigmoid(_dot(xcb, wx_ref[...]) + bx)
    sp = _softplus(-lam)
    la = -RGLRU_C * r * sp
    a = jnp.exp(la)
    mult = jnp.sqrt(_neg_expm1(2.0 * la))
    return (x0, x1, x2, x3), xc, r, ig, sp, a, mult


def _mix_b_fwd(z, mix, conv_w, conv_b, wa, wx, ba, bx, lam):
    t = z.shape[0]
    tm = _mix_tm(t)

    def body(x_ref, gb_ref, cw_ref, cb_ref, wa_ref, wx_ref, ba_ref, bx_ref, lam_ref, mix_in, o_ref, hs_ref,
             ext_ref, a_ref, b_ref, xh_ref, hc_ref):
        i = pl.program_id(0)

        @pl.when(i == 0)
        def _():
            xh_ref[...] = jnp.zeros_like(xh_ref)
            hc_ref[...] = jnp.zeros_like(hc_ref)

        _, xc, _, ig, _, a, mult = _rglru_gates(ext_ref, x_ref, xh_ref[...], cw_ref[...], cb_ref[...], wa_ref, wx_ref,
                                                ba_ref[...], bx_ref[...], lam_ref[...])
        xh_ref[...] = x_ref[tm - 8:tm, :]
        a_ref[...] = a
        b_ref[...] = mult * (ig * xc)
        rid = lax.broadcasted_iota(jnp.int32, (8, W_GRP), 0)

        def group(gi, hprev):
            base = pl.multiple_of(gi * 8, 8)
            ca = a_ref[pl.ds(base, 8), :]
            cb = b_ref[pl.ds(base, 8), :]
            for k in (1, 2, 4):
                m = rid >= k
                cb = jnp.where(m, ca * pltpu.roll(cb, k, 0) + cb, cb)
                ca = jnp.where(m, ca * pltpu.roll(ca, k, 0), ca)
            hh = cb + ca * hprev
            hs_ref[pl.ds(base, 8), :] = hh
            return hh[7:8, :]

        hlast = lax.fori_loop(0, tm // 8, group, hc_ref[0:1, :])
        hc_ref[...] = jnp.broadcast_to(hlast, hc_ref.shape)
        o_ref[...] = hs_ref[...] * _gelu(gb_ref[...])

    sq = _full((W_GRP, W_GRP))
    vec = _full((1, W_GRP))
    return pl.pallas_call(
        body, grid=(t // tm,),
        in_specs=[_zblk(tm, 2), _zblk(tm, 3), _full((4, W_GRP)), vec, sq, sq, vec, vec, vec, pl.BlockSpec(memory_space=pl.ANY)],
        out_specs=[_zblk(tm, 1), pl.BlockSpec((tm, W_GRP), lambda i: (i, 0))],
        out_shape=[_sds(mix.shape, F32), _sds((t, W_GRP), F32)],
        scratch_shapes=[pltpu.VMEM((tm + 8, W_GRP), F32), pltpu.VMEM((tm, W_GRP), F32), pltpu.VMEM((tm, W_GRP), F32),
                        pltpu.VMEM((8, W_GRP), F32), pltpu.VMEM((8, W_GRP), F32)],
        input_output_aliases={9: 0}, name="mix_b_fwd", compiler_params=_cp("arbitrary"),
    )(z, z, conv_w, conv_b, wa, wx, ba, bx, lam, mix)


def _mix_b_bwd(z, dz, dmix, hs, conv_w, conv_b, wa, wx, wa_t, wx_t, ba, bx, lam):
    t = z.shape[0]
    tm = _mix_tm(t)
    nt = t // tm
    hb = tm // 8

    def body(x_ref, gb_ref, xhalo_ref, hs_ref, hhalo_ref, dy_ref, cw_ref, cb_ref, wa_ref, wx_ref, wat_ref, wxt_ref,
             ba_ref, bx_ref, lam_ref, dz_in, dz_ref, dcw_ref, dcb_ref, dwa_ref, dwx_ref, dba_ref, dbx_ref, dlam_ref,
             ext_ref, c_ref, d_ref, g_ref, an_ref, gn_ref, dxn_ref):
        i = pl.program_id(0)
        first_tile = i == nt - 1

        @pl.when(i == 0)
        def _():
            for ref in (dcw_ref, dcb_ref, dwa_ref, dwx_ref, dba_ref, dbx_ref, dlam_ref, an_ref, gn_ref, dxn_ref):
                ref[...] = jnp.zeros_like(ref)

        cw, lam = cw_ref[...], lam_ref[...]
        xhalo = jnp.where(first_tile, 0.0, xhalo_ref[...])
        (x0, x1, x2, x3), xc, r, ig, sp, a, mult = _rglru_gates(
            ext_ref, x_ref, xhalo, cw, cb_ref[...], wa_ref, wx_ref, ba_ref[...], bx_ref[...], lam)
        hs = hs_ref[...]
        dy = dy_ref[...]
        gel, dgel = _gelu_and_grad(gb_ref[...])
        dz_ref[:, W_GRP:2 * W_GRP] = dy * hs * dgel

        ext_ref[0:tm, :] = a
        ext_ref[tm:tm + 8, :] = an_ref[...]
        an_ref[...] = a[0:8, :]
        c_ref[...] = _shift_up(ext_ref[...], 1, tm)
        d_ref[...] = dy * gel
        rid = lax.broadcasted_iota(jnp.int32, (8, W_GRP), 0)

        def group(j, gnext):
            base = pl.multiple_of((tm // 8 - 1 - j) * 8, 8)
            cc = c_ref[pl.ds(base, 8), :]
            cd = d_ref[pl.ds(base, 8), :]
            for k in (1, 2, 4):
                m = rid < 8 - k
                cd = jnp.where(m, cc * pltpu.roll(cd, 8 - k, 0) + cd, cd)
                cc = jnp.where(m, cc * pltpu.roll(cc, 8 - k, 0), cc)
            gg = cd + cc * gnext
            g_ref[pl.ds(base, 8), :] = gg
            return gg[0:1, :]

        gfirst = lax.fori_loop(0, tm // 8, group, gn_ref[0:1, :])
        gn_ref[...] = jnp.broadcast_to(gfirst, gn_ref.shape)
        g = g_ref[...]

        ext_ref[0:8, :] = jnp.where(first_tile, 0.0, hhalo_ref[...])
        ext_ref[8:, :] = hs
        hprev = _shift_down(ext_ref[...], 1, 8)
        da = g * hprev
        dmult = g * (ig * xc)
        di = g * mult * xc
        dxc = g * mult * ig
        dla = da * a - dmult * a * a / mult
        dr = dla * (-RGLRU_C * sp)
        dlam_ref[...] += jnp.sum(dla * (-RGLRU_C * r), axis=0, keepdims=True) * (-_sigmoid(-lam))
        dpr = dr * r * (1.0 - r)
        dpi = di * ig * (1.0 - ig)
        dprb, dpib, xcb = dpr.astype(BF16), dpi.astype(BF16), xc.astype(BF16)
        dba_ref[...] += jnp.sum(dpr, axis=0, keepdims=True)
        dbx_ref[...] += jnp.sum(dpi, axis=0, keepdims=True)
        dwa_ref[...] += _dot_tn(xcb, dprb)
        dwx_ref[...] += _dot_tn(xcb, dpib)
        dxc = dxc + _dot(dprb, wat_ref[...]) + _dot(dpib, wxt_ref[...])
        dcb_ref[...] += jnp.sum(dxc, axis=0, keepdims=True)
        dcw_ref[3:4, :] += jnp.sum(dxc * x0, axis=0, keepdims=True)
        dcw_ref[2:3, :] += jnp.sum(dxc * x1, axis=0, keepdims=True)
        dcw_ref[1:2, :] += jnp.sum(dxc * x2, axis=0, keepdims=True)
        dcw_ref[0:1, :] += jnp.sum(dxc * x3, axis=0, keepdims=True)
        ext_ref[0:tm, :] = dxc
        ext_ref[tm:tm + 8, :] = dxn_ref[...]
        dxn_ref[...] = dxc[0:8, :]
        ext = ext_ref[...]
        dz_ref[:, 0:W_GRP] = (cw[3:4, :] * dxc + cw[2:3, :] * _shift_up(ext, 1, tm) + cw[1:2, :] * _shift_up(ext, 2, tm)
                              + cw[0:1, :] * _shift_up(ext, 3, tm))

    sq = _full((W_GRP, W_GRP))
    vec = _full((1, W_GRP))
    halo = lambda col: pl.BlockSpec((8, W_GRP), lambda i: (jnp.maximum((nt - 1 - i) * hb - 1, 0), col))
    rev = lambda col: _zblk(tm, col, nt)
    return pl.pallas_call(
        body, grid=(nt,),
        in_specs=[rev(2), rev(3), halo(2), rev(0), halo(0), rev(1), _full((4, W_GRP)), vec, sq, sq, sq, sq, vec, vec, vec,
                  pl.BlockSpec(memory_space=pl.ANY)],
        out_specs=[pl.BlockSpec((tm, 2 * W_GRP), lambda i: (nt - 1 - i, 1)), _full((4, W_GRP)), vec, sq, sq, vec, vec, vec],
        out_shape=[_sds(dz.shape, F32), _sds((4, W_GRP), F32), _sds((1, W_GRP), F32), _sds((W_GRP, W_GRP), F32),
                   _sds((W_GRP, W_GRP), F32), _sds((1, W_GRP), F32), _sds((1, W_GRP), F32), _sds((1, W_GRP), F32)],
        scratch_shapes=[pltpu.VMEM((tm + 8, W_GRP), F32), pltpu.VMEM((tm, W_GRP), F32), pltpu.VMEM((tm, W_GRP), F32),
                        pltpu.VMEM((tm, W_GRP), F32), pltpu.VMEM((8, W_GRP), F32), pltpu.VMEM((8, W_GRP), F32),
                        pltpu.VMEM((8, W_GRP), F32)],
        input_output_aliases={15: 0}, name="mix_b_bwd", compiler_params=_cp("arbitrary"),
    )(z, z, z, hs, hs, dmix, conv_w, conv_b, wa, wx, wa_t, wx_t, ba, bx, lam, dz)


def _tri(n, lower):
    r = lax.broadcasted_iota(jnp.int32, (n, n), 0)
    c = lax.broadcasted_iota(jnp.int32, (n, n), 1)
    return jnp.where((r >= c) if lower else (r <= c), 1.0, 0.0).astype(BF16)


def _causal_stack():
    r = lax.broadcasted_iota(jnp.int32, (N_HEADS * HGRN_CHUNK, HGRN_CHUNK), 0)
    c = lax.broadcasted_iota(jnp.int32, (N_HEADS * HGRN_CHUNK, HGRN_CHUNK), 1)
    m = None
    for h in range(N_HEADS):
        mh = (r >= h * HGRN_CHUNK) & (r < (h + 1) * HGRN_CHUNK) & (r - h * HGRN_CHUNK >= c)
        m = mh if m is None else (m | mh)
    return m


def _stack_heads(x, hm):
    return jnp.concatenate([jnp.where(hm[h], x, 0.0) for h in range(N_HEADS)], axis=0)


def _unstack_heads(xs, hm):
    out = jnp.where(hm[0], xs[0:HGRN_CHUNK], 0.0)
    for h in range(1, N_HEADS):
        out = out + jnp.where(hm[h], xs[h * HGRN_CHUNK:(h + 1) * HGRN_CHUNK], 0.0)
    return out


def _hgrn_chunk(qv, fv, lb, tril):
    sq = _sigmoid(qv)
    qq = qv * sq
    sg = _sigmoid(fv)
    fg = lb + (1.0 - lb) * sg
    kk = 1.0 - fg
    bb = _dot_f32_lhs_exact(tril, jnp.log(fg))
    b_last = bb[HGRN_CHUNK - 1:HGRN_CHUNK, :]
    b_mid = bb[HGRN_CHUNK // 2 - 1:HGRN_CHUNK // 2, :]
    eq = jnp.exp(jnp.minimum(bb - b_mid, EXP_CLAMP))
    ek = jnp.exp(jnp.minimum(b_mid - bb, EXP_CLAMP))
    eb = jnp.exp(bb)
    el = jnp.exp(b_last - bb)
    return sq, qq, sg, fg, kk, b_last, eq, ek, eb, el


def _seg_mean(x, avg):
    return _dot_f32_rhs_exact(x, avg)


def _mix_c_fwd(z, mix, lb, ng):
    t = z.shape[0]
    tm = _mix_tm(t)
    nch = tm // HGRN_CHUNK

    def body(q_ref, f_ref, i_ref, g_ref, lb_ref, ng_ref, mix_in, y_ref, o_ref, ss_ref, s_ref):
        @pl.when(pl.program_id(0) == 0)
        def _():
            s_ref[...] = jnp.zeros_like(s_ref)

        hm = _head_masks()
        bmask = _block_mask()
        causal = _causal_stack()
        tril = _tri(HGRN_CHUNK, True)
        avg = jnp.where(bmask, 1.0 / HEAD_DIM, 0.0).astype(BF16)
        lb, ng = lb_ref[...], ng_ref[...]

        def chunk(c, carry):
            rows = pl.ds(pl.multiple_of(c * HGRN_CHUNK, HGRN_CHUNK), HGRN_CHUNK)
            vv = i_ref[rows, :]
            gv = g_ref[rows, :]
            _, qq, _, _, kk, b_last, eq, ek, eb, el = _hgrn_chunk(q_ref[rows, :], f_ref[rows, :], lb, tril)
            vb = vv.astype(BF16)
            qs = _stack_heads(qq * eq, hm).astype(BF16)
            att = jnp.where(causal, _dot_nt(qs, (kk * ek).astype(BF16)), 0.0)
            o = _unstack_heads(_dot(att.astype(BF16), vb), hm)
            s0 = s_ref[...]
            ss_ref[c] = s0
            o = o + _dot_nt((qq * eb).astype(BF16), s0.astype(BF16))
            s_ref[...] = s0 * jnp.exp(b_last) + jnp.where(bmask, _dot_tn(vb, (kk * el).astype(BF16)), 0.0)
            o_ref[rows, :] = o
            rstd = lax.rsqrt(_seg_mean(o * o, avg) + EPS)
            y_ref[rows, :] = o * rstd * ng * (gv * _sigmoid(gv))
            return carry

        lax.fori_loop(0, nch, chunk, 0, unroll=HGRN_UNROLL)

    vec = _full((1, W_GRP))
    return pl.pallas_call(
        body, grid=(t // tm,),
        in_specs=[_zblk(tm, 4), _zblk(tm, 5), _zblk(tm, 6), _zblk(tm, 7), vec, vec, pl.BlockSpec(memory_space=pl.ANY)],
        out_specs=[_zblk(tm, 2), pl.BlockSpec((tm, W_GRP), lambda i: (i, 0)),
                   pl.BlockSpec((nch, W_GRP, W_GRP), lambda i: (i, 0, 0))],
        out_shape=[_sds(mix.shape, F32), _sds((t, W_GRP), F32), _sds((t // HGRN_CHUNK, W_GRP, W_GRP), F32)],
        scratch_shapes=[pltpu.VMEM((W_GRP, W_GRP), F32)],
        input_output_aliases={6: 0}, name="mix_c_fwd", compiler_params=_cp("arbitrary"),
    )(z, z, z, z, lb, ng, mix)


def _mix_c_bwd(z, dz, dmix, o_pre, states, lb, ng):
    t = z.shape[0]
    tm = _mix_tm(t)
    nt = t // tm
    nch = tm // HGRN_CHUNK

    def body(q_ref, f_ref, i_ref, g_ref, o_ref, ss_ref, dy_ref, lb_ref, ng_ref, dz_in, dz_ref, dlb_ref, dng_ref, ds_ref):
        @pl.when(pl.program_id(0) == 0)
        def _():
            ds_ref[...] = jnp.zeros_like(ds_ref)
            dlb_ref[...] = jnp.zeros_like(dlb_ref)
            dng_ref[...] = jnp.zeros_like(dng_ref)

        hm = _head_masks()
        bmask = _block_mask()
        causal = _causal_stack()
        tril = _tri(HGRN_CHUNK, True)
        triu = _tri(HGRN_CHUNK, False)
        avg = jnp.where(bmask, 1.0 / HEAD_DIM, 0.0).astype(BF16)
        lb, ng = lb_ref[...], ng_ref[...]
        last_row = lax.broadcasted_iota(jnp.int32, (HGRN_CHUNK, W_GRP), 0) == HGRN_CHUNK - 1

        def chunk(j, carry):
            c = nch - 1 - j
            rows = pl.ds(pl.multiple_of(c * HGRN_CHUNK, HGRN_CHUNK), HGRN_CHUNK)
            qv, gv, vv = q_ref[rows, :], g_ref[rows, :], i_ref[rows, :]
            sq, qq, sg, fg, kk, b_last, eq, ek, eb, el = _hgrn_chunk(qv, f_ref[rows, :], lb, tril)
            s0 = ss_ref[c]
            ds1 = ds_ref[...]
            o = o_ref[rows, :]
            dy = dy_ref[rows, :]
            rstd = lax.rsqrt(_seg_mean(o * o, avg) + EPS)
            oh = o * rstd
            sgg = _sigmoid(gv)
            dz_ref[rows, 3 * W_GRP:4 * W_GRP] = dy * oh * ng * (sgg * (1.0 + gv * (1.0 - sgg)))
            don = dy * gv * sgg
            dng_ref[...] += jnp.sum(don * oh, axis=0, keepdims=True)
            doh = don * ng
            do = rstd * (doh - oh * _seg_mean(doh * oh, avg))
            qt, kt, qh, kh = qq * eq, kk * ek, qq * eb, kk * el
            vb, dob = vv.astype(BF16), do.astype(BF16)
            ktb, khb = kt.astype(BF16), kh.astype(BF16)
            ds1b = ds1.astype(BF16)
            qs = _stack_heads(qt, hm).astype(BF16)
            dos = _stack_heads(do, hm).astype(BF16)
            att = jnp.where(causal, _dot_nt(qs, ktb), 0.0).astype(BF16)
            datt = jnp.where(causal, _dot_nt(dos, vb), 0.0).astype(BF16)
            dv = _dot_tn(att, dos) + _dot_nt(khb, ds1b)
            dqt = _unstack_heads(_dot(datt, ktb), hm)
            dkt = _dot_tn(datt, qs)
            dqh = _dot(dob, s0.astype(BF16))
            dkh = _dot(vb, ds1b)
            e_last = jnp.exp(b_last)
            ds_ref[...] = ds1 * e_last + jnp.where(bmask, _dot_tn(dob, qh.astype(BF16)), 0.0)
            dq = dqt * eq + dqh * eb
            dk = dkt * ek + dkh * el
            db = qt * dqt - kt * dkt + qh * dqh - kh * dkh
            db_last = jnp.sum(kh * dkh, axis=0, keepdims=True) + e_last * jnp.sum(ds1 * s0, axis=0, keepdims=True)
            db = db + jnp.where(last_row, db_last, 0.0)
            dlogf = _dot_f32_lhs_exact(triu, db)
            dfg = dlogf / fg - dk
            dz_ref[rows, W_GRP:2 * W_GRP] = dfg * (1.0 - lb) * sg * (1.0 - sg)
            dlb_ref[...] += jnp.sum(dfg * (1.0 - sg), axis=0, keepdims=True)
            dz_ref[rows, 0:W_GRP] = dq * (sq * (1.0 + qv * (1.0 - sq)))
            dz_ref[rows, 2 * W_GRP:3 * W_GRP] = dv
            return carry

        lax.fori_loop(0, nch, chunk, 0, unroll=HGRN_UNROLL)

    vec = _full((1, W_GRP))
    rev = lambda col: _zblk(tm, col, nt)
    return pl.pallas_call(
        body, grid=(nt,),
        in_specs=[rev(4), rev(5), rev(6), rev(7), rev(0), pl.BlockSpec((nch, W_GRP, W_GRP), lambda i: (nt - 1 - i, 0, 0)),
                  rev(2), vec, vec, pl.BlockSpec(memory_space=pl.ANY)],
        out_specs=[pl.BlockSpec((tm, 4 * W_GRP), lambda i: (nt - 1 - i, 1)), vec, vec],
        out_shape=[_sds(dz.shape, F32), _sds((1, W_GRP), F32), _sds((1, W_GRP), F32)],
        scratch_shapes=[pltpu.VMEM((W_GRP, W_GRP), F32)],
        input_output_aliases={9: 0}, name="mix_c_bwd", compiler_params=_cp("arbitrary"),
    )(z, z, z, z, o_pre, states, dmix, lb, ng, dz)


def _pool_select(hm, s2, s4, s8, s16):
    return jnp.where(hm[0], s2, jnp.where(hm[1], s4, jnp.where(hm[2], s8, s16)))


def _pool_counts(hm, row0, tm):
    pos = (row0 + 1 + lax.broadcasted_iota(jnp.int32, (tm, W_GRP), 0)).astype(F32)
    win = _pool_select(hm, 2.0, 4.0, 8.0, 16.0)
    return jnp.minimum(pos, win)


def _pooled(ext_ref, x, halo, hm, cnt):
    ext_ref[0:POOL_HALO, :] = halo
    ext_ref[POOL_HALO:, :] = x
    e = ext_ref[...]
    s2 = e + pltpu.roll(e, 1, 0)
    s4 = s2 + pltpu.roll(s2, 2, 0)
    s8 = s4 + pltpu.roll(s4, 4, 0)
    s16 = s8 + pltpu.roll(s8, 8, 0)
    return _pool_select(hm, s2, s4, s8, s16)[POOL_HALO:] / cnt - x


def _mix_d_fwd(z, mix, wd, scale):
    t = z.shape[0]
    tm = _mix_tm(t)

    def body(x_ref, wd_ref, sc_ref, mix_in, o_ref, ext_ref, halo_ref):
        i = pl.program_id(0)

        @pl.when(i == 0)
        def _():
            halo_ref[...] = jnp.zeros_like(halo_ref)

        hm = _head_masks()
        x = x_ref[...]
        pooled = _pooled(ext_ref, x, halo_ref[...], hm, _pool_counts(hm, i * tm, tm))
        halo_ref[...] = x_ref[tm - POOL_HALO:tm, :]
        o_ref[...] = _dot(pooled.astype(BF16), wd_ref[...]) * sc_ref[...]

    return pl.pallas_call(
        body, grid=(t // tm,),
        in_specs=[_zblk(tm, 8), _full((W_GRP, W_GRP)), _full((1, W_GRP)), pl.BlockSpec(memory_space=pl.ANY)],
        out_specs=_zblk(tm, 3), out_shape=_sds(mix.shape, F32),
        scratch_shapes=[pltpu.VMEM((tm + POOL_HALO, W_GRP), F32), pltpu.VMEM((POOL_HALO, W_GRP), F32)],
        input_output_aliases={3: 0}, name="mix_d_fwd", compiler_params=_cp("arbitrary"),
    )(z, wd, scale, mix)


def _mix_d_bwd(z, dz, dmix, wd, wd_t, scale):
    t = z.shape[0]
    tm = _mix_tm(t)
    nt = t // tm
    hb = tm // POOL_HALO

    def body(x_ref, xhalo_ref, dy_ref, wd_ref, wdt_ref, sc_ref, dz_in, dz_ref, dwd_ref, dsc_ref, ext_ref, en_ref):
        i = pl.program_id(0)
        ri = nt - 1 - i

        @pl.when(i == 0)
        def _():
            en_ref[...] = jnp.zeros_like(en_ref)
            dwd_ref[...] = jnp.zeros_like(dwd_ref)
            dsc_ref[...] = jnp.zeros_like(dsc_ref)

        hm = _head_masks()
        cnt = _pool_counts(hm, ri * tm, tm)
        x = x_ref[...]
        pooled = _pooled(ext_ref, x, jnp.where(ri == 0, 0.0, xhalo_ref[...]), hm, cnt)
        pb = pooled.astype(BF16)
        dy = dy_ref[...]
        dsc_ref[...] += jnp.sum(dy * _dot(pb, wd_ref[...]), axis=0, keepdims=True)
        dyw = (dy * sc_ref[...]).astype(BF16)
        dwd_ref[...] += _dot_tn(pb, dyw)
        dpool = _dot(dyw, wdt_ref[...])
        e = dpool / cnt
        ext_ref[0:tm, :] = e
        ext_ref[tm:, :] = en_ref[...]
        en_ref[...] = e[0:POOL_HALO, :]
        ee = ext_ref[...]
        n = tm + POOL_HALO
        r2 = ee + pltpu.roll(ee, n - 1, 0)
        r4 = r2 + pltpu.roll(r2, n - 2, 0)
        r8 = r4 + pltpu.roll(r4, n - 4, 0)
        r16 = r8 + pltpu.roll(r8, n - 8, 0)
        dz_ref[...] = _pool_select(hm, r2, r4, r8, r16)[:tm] - dpool

    sq = _full((W_GRP, W_GRP))
    vec = _full((1, W_GRP))
    return pl.pallas_call(
        body, grid=(nt,),
        in_specs=[_zblk(tm, 8, nt), pl.BlockSpec((POOL_HALO, W_GRP), lambda i: (jnp.maximum((nt - 1 - i) * hb - 1, 0), 8)),
                  _zblk(tm, 3, nt), sq, sq, vec, pl.BlockSpec(memory_space=pl.ANY)],
        out_specs=[_zblk(tm, 8, nt), sq, vec],
        out_shape=[_sds(dz.shape, F32), _sds((W_GRP, W_GRP), F32), _sds((1, W_GRP), F32)],
        scratch_shapes=[pltpu.VMEM((tm + POOL_HALO, W_GRP), F32), pltpu.VMEM((POOL_HALO, W_GRP), F32)],
        input_output_aliases={6: 0}, name="mix_d_bwd", compiler_params=_cp("arbitrary"),
    )(z, z, dmix, wd, wd_t, scale, dz)


def _as2d(a):
    if a.ndim == 1:
        return a.reshape(1, a.shape[0])
    return a.reshape(-1, a.shape[-1])


def _adamw(w, g, m, v, name):
    shape = w.shape
    w2, g2, m2, v2 = _as2d(w), _as2d(g), _as2d(m), _as2d(v)
    rows, cols = w2.shape
    tr = _pick(rows, (1024, 512, 256, 128, 64, 32, 16, 8))
    if tr * cols * 4 * 14 > VMEM_LIMIT_BYTES:
        tr = _pick(rows, (256, 128, 64, 32, 16, 8))

    def body(w_ref, g_ref, m_ref, v_ref, d_ref, nm_ref, nv_ref):
        gv = g_ref[...]
        mn = ADAM_B1 * m_ref[...] + (1.0 - ADAM_B1) * gv
        vn = ADAM_B2 * v_ref[...] + (1.0 - ADAM_B2) * (gv * gv)
        m_hat = mn / (1.0 - ADAM_B1 ** ADAM_STEP)
        v_hat = vn / (1.0 - ADAM_B2 ** ADAM_STEP)
        d_ref[...] = -ADAM_LR * (m_hat / (jnp.sqrt(v_hat) + ADAM_EPS) + ADAM_WD * w_ref[...])
        nm_ref[...] = mn
        nv_ref[...] = vn

    blk = pl.BlockSpec((tr, cols), lambda i: (i, 0))
    outs = pl.pallas_call(
        body, grid=(rows // tr,), in_specs=[blk] * 4, out_specs=[blk] * 3, out_shape=[_sds((rows, cols), F32)] * 3,
        name=name, compiler_params=_cp("parallel"),
    )(w2, g2, m2, v2)
    return tuple(o.reshape(shape) for o in outs)


def _adamw_layer(w, g, m, v, layer, bufs, name):
    nl, r, cs = w.shape
    tr = _pick(r, (256, 128, 64, 32, 16, 8))

    def body(w_ref, g_ref, m_ref, v_ref, *rest):
        go_ref, d_ref, nm_ref, nv_ref = rest[-4:]
        gv = g_ref[...]
        mn = ADAM_B1 * m_ref[...] + (1.0 - ADAM_B1) * gv
        vn = ADAM_B2 * v_ref[...] + (1.0 - ADAM_B2) * (gv * gv)
        m_hat = mn / (1.0 - ADAM_B1 ** ADAM_STEP)
        v_hat = vn / (1.0 - ADAM_B2 ** ADAM_STEP)
        go_ref[...] = gv
        d_ref[...] = -ADAM_LR * (m_hat / (jnp.sqrt(v_hat) + ADAM_EPS) + ADAM_WD * w_ref[...])
        nm_ref[...] = mn
        nv_ref[...] = vn

    lay = pl.BlockSpec((None, tr, cs), lambda i: (layer, i, 0))
    in_specs = [lay, pl.BlockSpec((tr, cs), lambda i: (i, 0)), lay, lay]
    args = [w, g, m, v]
    aliases = {}
    if bufs is not None:
        in_specs += [pl.BlockSpec(memory_space=pl.ANY)] * 4
        args += list(bufs)
        aliases = {4 + i: i for i in range(4)}
    return pl.pallas_call(
        body, grid=(r // tr,), in_specs=in_specs, out_specs=[lay] * 4, out_shape=[_sds((nl, r, cs), F32)] * 4,
        input_output_aliases=aliases, name=name, compiler_params=_cp("parallel"),
    )(*args)


def _slot_sum(own, slots, name):
    n_slots, rows, cols = slots.shape
    whole_fits = rows * cols * 4 * (n_slots + 2) * 2 <= VMEM_LIMIT_BYTES // 2
    tr = rows if whole_fits else _pick(rows, (512, 352, 256, 128, 64, 32, 16, 8))

    def body(*refs):
        s_ref, o_ref = refs[-2], refs[-1]
        acc = s_ref[0].astype(F32) if own is None else refs[0][...].astype(F32) + s_ref[0].astype(F32)
        for k in range(1, n_slots):
            acc = acc + s_ref[k].astype(F32)
        o_ref[...] = acc

    row = pl.BlockSpec((tr, cols), lambda i: (i, 0))
    return pl.pallas_call(
        body, grid=(rows // tr,),
        in_specs=([] if own is None else [row]) + [pl.BlockSpec((n_slots, tr, cols), lambda i: (0, i, 0))],
        out_specs=row, out_shape=_sds((rows, cols), F32), name=name, compiler_params=_cp("parallel"),
    )(*(() if own is None else (own,)), slots)


def _me():
    return lax.axis_index("x"), lax.axis_index("y"), lax.axis_index("c")


def _other_chips(x, y):
    return [(1 - x, y), (x, 1 - y), (1 - x, 1 - y)]


ANY_SPEC = pl.BlockSpec(memory_space=pl.ANY)
HBM_SPEC = pl.BlockSpec(memory_space=pltpu.HBM)
SEM_SPEC = pl.BlockSpec(memory_space=pltpu.SEMAPHORE)
SPLIT_COPY_PARAMS = pltpu.CompilerParams(has_side_effects=pltpu.SideEffectType.DATAFLOW_SIDE_EFFECTING)
N_CHIPS = 4


def _aligned(v, m):
    return v if isinstance(v, int) else pl.multiple_of(v, m)


def _in_hbm(arr):
    return pltpu.with_memory_space_constraint(arr, pltpu.HBM)


def _peer(x, y, c, k):
    fx, fy, fc = (k >> 2) & 1, (k >> 1) & 1, k & 1
    px = 1 - x if fx else x
    py = 1 - y if fy else y
    pc = 1 - c if fc else c
    return px, py, pc


def _gather_start(shards, after, name):
    n = len(shards)

    def body(*refs):
        src, land = refs[:n], refs[n:2 * n]
        send_sems, recv_sems = refs[2 * n + 1], refs[2 * n + 2]
        token = refs[-1]
        x, y, c = _me()
        for w in range(n):
            for chip in _other_chips(x, y):
                pltpu.make_async_remote_copy(
                    src_ref=src[w], dst_ref=land[w].at[2 * x + y], send_sem=send_sems.at[w], recv_sem=recv_sems.at[w],
                    device_id=(*chip, c), device_id_type=MESH_ID).start()
        token[...] = jnp.zeros_like(token)

    lands = [lax.empty((N_CHIPS,) + s.shape, s.dtype) for s in shards]
    thru = [pltpu.HBM(s.shape, s.dtype) for s in shards] + [pltpu.HBM(z.shape, z.dtype) for z in lands]
    outs = pl.pallas_call(
        body, name=name,
        out_shape=(pltpu.SemaphoreType.DMA((n,)), pltpu.SemaphoreType.DMA((n,)), *thru, _sds((8, 128), F32)),
        in_specs=[HBM_SPEC] * (2 * n) + [ANY_SPEC],
        out_specs=(SEM_SPEC, SEM_SPEC, *[HBM_SPEC] * (2 * n), pl.BlockSpec(memory_space=pltpu.VMEM)),
        input_output_aliases={i: 2 + i for i in range(2 * n)}, compiler_params=SPLIT_COPY_PARAMS,
    )(*[_in_hbm(s) for s in shards], *[_in_hbm(z) for z in lands], after)
    return (outs[0], outs[1], outs[2:2 + n], outs[2 + n:2 + 2 * n]), outs[-1]


def _gather_wait(send_sems, recv_sems, srcs, lands, after, name):
    n = len(srcs)

    def body(*refs):
        land = refs[n:2 * n]
        send_sems, recv_sems = refs[2 * n], refs[2 * n + 1]
        x, y, c = _me()
        for w in range(n):
            three = land[w].at[pl.ds(0, N_CHIPS - 1)]
            cp = pltpu.make_async_remote_copy(src_ref=three, dst_ref=three, send_sem=send_sems.at[w], recv_sem=recv_sems.at[w],
                                              device_id=(x, y, c), device_id_type=MESH_ID)
            cp.wait_send()
            cp.wait_recv()

    both = list(srcs) + list(lands)
    outs = pl.pallas_call(
        body, name=name, out_shape=tuple(pltpu.HBM(b.shape, b.dtype) for b in both),
        in_specs=[HBM_SPEC] * (2 * n) + [SEM_SPEC, SEM_SPEC, ANY_SPEC], out_specs=[HBM_SPEC] * (2 * n),
        input_output_aliases={i: i for i in range(2 * n)}, compiler_params=SPLIT_COPY_PARAMS,
    )(*both, send_sems, recv_sems, after)
    return outs[n:2 * n]


def _push_start(grads, small, name):
    n = len(grads)
    srcs = list(grads) + ([] if small is None else [small])
    ns = len(srcs)

    def body(*refs):
        src, slots = refs[:ns], refs[ns:2 * ns]
        send_sems, recv_sems = refs[2 * ns], refs[2 * ns + 1]
        token = refs[-1]
        x, y, c = _me()
        for w in range(ns):
            for k in range(1, N_DEV):
                px, py, pc = _peer(x, y, c, k)
                if w < n:
                    hr = src[w].shape[1] // 2
                    piece = src[w].at[2 * px + py, pl.ds(_aligned(pc * hr, 16), hr), :]
                    slot = slots[w].at[k - 1]
                else:
                    piece = src[w]
                    slot = slots[w].at[4 * x + 2 * y + c]
                pltpu.make_async_remote_copy(
                    src_ref=piece, dst_ref=slot, send_sem=send_sems.at[w], recv_sem=recv_sems.at[w],
                    device_id=(px, py, pc), device_id_type=MESH_ID).start()
        token[...] = jnp.zeros_like(token)

    slots = [lax.empty((N_DEV - 1, g.shape[1] // 2, g.shape[2]), g.dtype) for g in grads]
    if small is not None:
        slots.append(lax.empty((N_DEV,) + small.shape, small.dtype))
    both = srcs + slots
    outs = pl.pallas_call(
        body, name=name,
        out_shape=(pltpu.SemaphoreType.DMA((ns,)), pltpu.SemaphoreType.DMA((ns,)),
                   *[pltpu.HBM(b.shape, b.dtype) for b in both], _sds((8, 128), F32)),
        in_specs=[HBM_SPEC] * len(both),
        out_specs=(SEM_SPEC, SEM_SPEC, *[HBM_SPEC] * len(both), pl.BlockSpec(memory_space=pltpu.VMEM)),
        input_output_aliases={i: 2 + i for i in range(len(both))}, compiler_params=SPLIT_COPY_PARAMS,
    )(*[_in_hbm(b) for b in both])
    return (outs[0], outs[1], outs[2:2 + ns], outs[2 + ns:2 + 2 * ns]), outs[-1]


def _push_wait(send_sems, recv_sems, srcs, slots, after, name):
    n = len(srcs)

    def body(*refs):
        slot = refs[n:2 * n]
        send_sems, recv_sems = refs[2 * n], refs[2 * n + 1]
        x, y, c = _me()
        for w in range(n):
            seven = slot[w].at[pl.ds(0, N_DEV - 1)]
            cp = pltpu.make_async_remote_copy(src_ref=seven, dst_ref=seven, send_sem=send_sems.at[w],
                                              recv_sem=recv_sems.at[w], device_id=(x, y, c), device_id_type=MESH_ID)
            cp.wait_send()
            cp.wait_recv()

    both = list(srcs) + list(slots)
    outs = pl.pallas_call(
        body, name=name, out_shape=tuple(pltpu.HBM(b.shape, b.dtype) for b in both),
        in_specs=[HBM_SPEC] * (2 * n) + [SEM_SPEC, SEM_SPEC, ANY_SPEC], out_specs=[HBM_SPEC] * (2 * n),
        input_output_aliases={i: i for i in range(2 * n)}, compiler_params=SPLIT_COPY_PARAMS,
    )(*both, send_sems, recv_sems, after)
    return outs[:n], outs[n:]


SWAP_CHUNK_BYTES = 2 * 1024 * 1024


def _swap_chunk_rows(hr, cs):
    ch = hr
    while ch * cs * 4 > SWAP_CHUNK_BYTES and ch % 16 == 0:
        ch //= 2
    return ch


def _swap_halves(halves, name):
    n = len(halves)
    chunk = [_swap_chunk_rows(*h.shape) for h in halves]
    rounds = max(h.shape[0] // ch for h, ch in zip(halves, chunk))

    def body(*refs):
        src, dst, buf = refs[:n], refs[n:2 * n], refs[2 * n:3 * n]
        load_sems, put_sems, send_sems, recv_sems = refs[3 * n:]
        x, y, c = _me()
        sibling = (x, y, 1 - c)
        for j in range(rounds):
            live = [w for w in range(n) if j < src[w].shape[0] // chunk[w]]
            loads = [pltpu.make_async_copy(src[w].at[pl.ds(j * chunk[w], chunk[w])], buf[w], load_sems.at[w]) for w in live]
            for ld in loads:
                ld.start()
            moves = []
            for ld, w in zip(loads, live):
                ld.wait()
                rows = pl.ds(_aligned(c * src[w].shape[0] + j * chunk[w], 8), chunk[w])
                put = pltpu.make_async_copy(buf[w], dst[w].at[rows], put_sems.at[w])
                send = pltpu.make_async_remote_copy(src_ref=buf[w], dst_ref=dst[w].at[rows], send_sem=send_sems.at[w],
                                                    recv_sem=recv_sems.at[w], device_id=sibling, device_id_type=MESH_ID)
                put.start()
                send.start()
                moves.append((put, send))
            for put, send in moves:
                put.wait()
                send.wait_send()
        for w in range(n):
            hr = src[w].shape[0]
            got = dst[w].at[pl.ds(_aligned((1 - c) * hr, 8), hr)]
            pltpu.make_async_remote_copy(src_ref=got, dst_ref=got, send_sem=send_sems.at[w], recv_sem=recv_sems.at[w],
                                         device_id=sibling, device_id_type=MESH_ID).wait_recv()

    return pl.pallas_call(
        body, in_specs=[ANY_SPEC] * n, out_specs=[ANY_SPEC] * n,
        out_shape=[_sds((2 * h.shape[0], h.shape[1]), F32) for h in halves],
        scratch_shapes=[pltpu.VMEM((ch, h.shape[1]), F32) for h, ch in zip(halves, chunk)]
        + [pltpu.SemaphoreType.DMA((n,))] * 4,
        name=name,
    )(*halves)


BIG = ("w_in", "w_out", "w_up", "w_down", "w_pe", "w_pg")
ROW_SHARDED = ("w_out", "w_down", "w_pg")
SMALL = ("norm1_g", "a_ln_g", "a_ln_b", "a_ws", "a_bs", "b_conv_w", "b_conv_b", "b_wa", "b_ba", "b_wx", "b_bx", "b_lam",
         "c_lb", "c_norm_g", "d_w", "d_scale", "norm2_g", "ffn_conv_w", "ffn_conv_b", "norm3_g", "final_g")
SMALL_SHARDED = ("b_conv_w", "ffn_conv_w")
WEIGHTS = ("norm1_g", "w_in", "a_ln_g", "a_ln_b", "a_ws", "a_bs", "b_conv_w", "b_conv_b", "b_wa", "b_ba", "b_wx", "b_bx",
           "b_lam", "c_lb", "c_norm_g", "d_w", "d_scale", "w_out", "norm2_g", "w_up", "ffn_conv_w", "ffn_conv_b", "w_down",
           "norm3_g", "w_pe", "w_pg", "final_g")
ARGS = ("x", "p") + WEIGHTS + ("loss_target",) + tuple("m_" + n for n in WEIGHTS) + tuple("v_" + n for n in WEIGHTS)


def _block_diag(w):
    eye = jnp.eye(N_HEADS, dtype=w.dtype)
    return (eye[None, :, None, :, None] * w[:, :, :, None, :]).reshape(w.shape[0], W_GRP, W_GRP)


def _diag_blocks(m):
    m4 = m.reshape(N_HEADS, HEAD_DIM, N_HEADS, HEAD_DIM)
    return jnp.stack([m4[h, :, h, :] for h in range(N_HEADS)])


def _lower_bounds(c_lb):
    lbs = jnp.cumsum(jax.nn.softmax(c_lb, axis=0), axis=0)
    return lbs - lbs[0:1]


def kernel(x, p, norm1_g, w_in, a_ln_g, a_ln_b, a_ws, a_bs, b_conv_w, b_conv_b, b_wa, b_ba, b_wx, b_bx, b_lam, c_lb, c_norm_g, d_w, d_scale, w_out, norm2_g, w_up, ffn_conv_w, ffn_conv_b, w_down, norm3_g, w_pe, w_pg, final_g, loss_target, m_norm1_g, m_w_in, m_a_ln_g, m_a_ln_b, m_a_ws, m_a_bs, m_b_conv_w, m_b_conv_b, m_b_wa, m_b_ba, m_b_wx, m_b_bx, m_b_lam, m_c_lb, m_c_norm_g, m_d_w, m_d_scale, m_w_out, m_norm2_g, m_w_up, m_ffn_conv_w, m_ffn_conv_b, m_w_down, m_norm3_g, m_w_pe, m_w_pg, m_final_g, v_norm1_g, v_w_in, v_a_ln_g, v_a_ln_b, v_a_ws, v_a_bs, v_b_conv_w, v_b_conv_b, v_b_wa, v_b_ba, v_b_wx, v_b_bx, v_b_lam, v_c_lb, v_c_norm_g, v_d_w, v_d_scale, v_w_out, v_norm2_g, v_w_up, v_ffn_conv_w, v_ffn_conv_b, v_w_down, v_norm3_g, v_w_pe, v_w_pg, v_final_g):
    return _step((x, p, norm1_g, w_in, a_ln_g, a_ln_b, a_ws, a_bs, b_conv_w, b_conv_b, b_wa, b_ba, b_wx, b_bx, b_lam, c_lb, c_norm_g, d_w, d_scale, w_out, norm2_g, w_up, ffn_conv_w, ffn_conv_b, w_down, norm3_g, w_pe, w_pg, final_g, loss_target, m_norm1_g, m_w_in, m_a_ln_g, m_a_ln_b, m_a_ws, m_a_bs, m_b_conv_w, m_b_conv_b, m_b_wa, m_b_ba, m_b_wx, m_b_bx, m_b_lam, m_c_lb, m_c_norm_g, m_d_w, m_d_scale, m_w_out, m_norm2_g, m_w_up, m_ffn_conv_w, m_ffn_conv_b, m_w_down, m_norm3_g, m_w_pe, m_w_pg, m_final_g, v_norm1_g, v_w_in, v_a_ln_g, v_a_ln_b, v_a_ws, v_a_bs, v_b_conv_w, v_b_conv_b, v_b_wa, v_b_ba, v_b_wx, v_b_bx, v_b_lam, v_c_lb, v_c_norm_g, v_d_w, v_d_scale, v_w_out, v_norm2_g, v_w_up, v_ffn_conv_w, v_ffn_conv_b, v_w_down, v_norm3_g, v_w_pe, v_w_pg, v_final_g))


SMALL_PER_LAYER = tuple(n for n in SMALL if n != "final_g")
GATHERED = BIG + SMALL_SHARDED
GATHER_LAYER0 = (("a", ("w_in", "b_conv_w")), ("b", ("w_out", "w_up", "ffn_conv_w", "w_down", "w_pe", "w_pg")))
PUSH_EARLY = ("w_pe", "w_pg", "w_down", "w_up")
PUSH_MID = ("w_out",)
PUSH_LATE = ("w_in",)
SMALL_MID = tuple(n for n in SMALL_PER_LAYER if n != "norm1_g")


def _cols_to_slabs(m):
    r, c4 = m.shape
    return jnp.moveaxis(m.reshape(r, N_CHIPS, c4 // N_CHIPS), 1, 0)


def _slabs_to_cols(s):
    return jnp.moveaxis(s, 0, 1).reshape(s.shape[1], -1)


def _pack_small(parts):
    flat = jnp.concatenate([p.reshape(-1) for p in parts])
    return jnp.pad(flat, (0, (-flat.shape[0]) % 1024)).reshape(-1, 128)


def _step(args):
    a = dict(zip(ARGS, args, strict=True))
    x0 = a["x"][0]
    target = a["loss_target"][0]
    nl = a["norm1_g"].shape[0]
    t, d = x0.shape
    f = a["w_down"].shape[1] * N_CHIPS
    cx, cy, cc = _me()
    my_shard = 2 * cx + cy
    shards = {n: a[n].astype(BF16) for n in BIG}
    shards.update({n: a[n] for n in SMALL_SHARDED})

    def start_gather(l, names, after, tag):
        return _gather_start([shards[n][l] for n in names], after, f"gather_start_{l}{tag}")

    def finish_gather(l, names, handle, after, tag):
        send, recv, srcs, lands = handle
        lands = _gather_wait(send, recv, srcs, lands, after, f"gather_wait_{l}{tag}")
        w = {}
        for n, land in zip(names, lands):
            full = lax.dynamic_update_slice(land, shards[n][l][None], (my_shard, 0, 0))
            if n in ROW_SHARDED:
                w[n] = full.reshape(-1, full.shape[-1])
            elif n in ("w_up", "w_pe"):
                w[n] = full
            else:
                w[n] = _slabs_to_cols(full)
        return w

    lbs, lbs_vjp = jax.vjp(_lower_bounds, a["c_lb"])
    tril = jnp.tril(jnp.ones((GMLP_CHUNK, GMLP_CHUNK), F32))

    stacked_params = {"wm": (a["a_ws"] * tril).astype(BF16)}
    stacked_params["wm_t"] = jnp.swapaxes(stacked_params["wm"], 2, 3)
    stacked_params["bs_t"] = jnp.repeat(jnp.swapaxes(a["a_bs"], 1, 2), HEAD_DIM, axis=2)
    for nm in ("b_wa", "b_wx", "d_w"):
        bd = _block_diag(a[nm]).astype(BF16)
        stacked_params[nm], stacked_params[nm + "_t"] = bd, jnp.swapaxes(bd, 1, 2)
    for nm in ("a_ln_g", "a_ln_b", "b_conv_b", "b_ba", "b_bx", "b_lam", "d_scale"):
        stacked_params[nm] = a[nm].reshape(nl, 1, W_GRP)
    stacked_params["lb"] = lbs.reshape(nl, 1, W_GRP)
    stacked_params["ng"] = jnp.tile(a["c_norm_g"], (1, N_HEADS)).reshape(nl, 1, W_GRP)
    stacked_params["ffn_conv_b"] = a["ffn_conv_b"].reshape(nl, 1, 2 * f)

    def layer_params(l, w):
        q = {k: v[l] for k, v in stacked_params.items()}
        q.update(w)
        return q

    saved, weights, params = [], [], []
    first_groups = {tag: start_gather(0, names, x0, tag)[0] for tag, names in GATHER_LAYER0}
    xl = x0
    for l in range(nl):
        if l == 0:
            w = finish_gather(0, GATHER_LAYER0[0][1], first_groups["a"], xl, "a")
        else:
            w = finish_gather(l, GATHERED, next_handle, xl, "")
        s = {"x0": xl}
        s["h1"] = _rms_fwd(xl, a["norm1_g"][0], "rms1_fwd") if l == 0 else h_next
        token = None
        if 0 < l < nl - 1:
            next_handle, token = start_gather(l + 1, GATHERED, s["h1"], "")
        s["z"] = _mm(s["h1"], w["w_in"], "nn", out_dtype=F32, name="mm_z", after=token)
        q = layer_params(l, w)
        mix = _mix_a_fwd(s["z"], d, q["a_ln_g"], q["a_ln_b"], q["wm"], q["bs_t"])
        mix, s["hs"] = _mix_b_fwd(s["z"], mix, q["b_conv_w"], q["b_conv_b"], q["b_wa"], q["b_wx"], q["b_ba"], q["b_bx"],
                                  q["b_lam"])
        mix, s["o_pre"], s["states"] = _mix_c_fwd(s["z"], mix, q["lb"], q["ng"])
        s["mix"] = _mix_d_fwd(s["z"], mix, q["d_w"], q["d_scale"])
        def land(tag, after):
            if l == 0:
                w.update(finish_gather(0, dict(GATHER_LAYER0)[tag], first_groups[tag], after, tag))
                q.update(w)

        land("b", s["mix"])
        token = None
        if l == 0 and nl > 1:
            next_handle, token = start_gather(1, GATHERED, w["w_out"], "")
        s["x1"], s["h2"] = _mm(s["mix"], w["w_out"], "nn", res=xl, out_dtype=F32, name="mm_out",
                               norm_g=a["norm2_g"][l], tm_max=1024, after=token)
        s["hf_g"] = _mm(s["h2"], w["w_up"], "nn", b_slabs=True, n=f, out_dtype=F32, name="mm_up_g")
        s["hf_v"] = _mm(s["h2"], w["w_up"], "nn", b_slabs=True, n=f, b_noff=f, out_dtype=F32, name="mm_up_v")
        s["act"], s["gc"], s["vc"] = _ffn_act_fwd(s["hf_g"], s["hf_v"], q["ffn_conv_w"], q["ffn_conv_b"])
        s["x2"], s["h3"] = _mm(s["act"], w["w_down"], "nn", res=s["x1"], out_dtype=F32, name="mm_down",
                               norm_g=a["norm3_g"][l], tm_max=1024)
        s["pre"] = _mm(s["h3"], w["w_pg"], "nn", out_dtype=F32, name="mm_pg")
        s["pe"] = _mm(a["p"][l, 0], w["w_pe"], "nn", b_slabs=True, out_dtype=F32, name="mm_pe")
        xl, h_next = _ple_fwd(s["x2"], s["pe"], s["pre"], a["norm1_g"][l + 1] if l + 1 < nl else None)
        saved.append(s)
        weights.append(w)
        params.append(q)

    dx, g_final, loss = _final_loss(xl, a["final_g"], target)
    loss = lax.psum(loss[0, 0], ("x", "y", "c"))

    stacked = {n: None for n in BIG}
    small_sums = {}

    def finish_push(l, names, handle, tag, after):
        send, recv, srcs, slots = handle
        srcs, slots = _push_wait(send, recv, srcs, slots, after, f"push_wait_{l}{tag}")
        halves = []
        for n, g, sl in zip(names, srcs, slots):
            hr = g.shape[1] // 2
            own = lax.dynamic_slice(g, (my_shard, cc * hr, 0), (1, hr, g.shape[2]))[0]
            halves.append(_slot_sum(own, sl, "sum_" + n))
        for n, g in zip(names, _swap_halves(halves, "swap_halves_" + tag)):
            stacked[n] = _adamw_layer(a[n], g, a["m_" + n], a["v_" + n], l, stacked[n], "adamw_" + n)
        if len(srcs) > len(names):
            by_sender = lax.dynamic_update_slice(slots[-1], srcs[-1][None], (2 * my_shard + cc, 0, 0))
            small_sums[l, tag] = _slot_sum(None, by_sender, "sum_small_" + tag)
        return stacked[names[-1]][1]

    pending = []
    token = None
    for l in reversed(range(nl)):
        q, s, w = params[l], saved[l], weights[l]
        gs = {}
        dpe, dpre = _ple_bwd(dx, s["pe"], s["pre"], after=token)
        g_pe = _mm(a["p"][l, 0], dpe, "tn", out_dtype=BF16, name="mm_dwpe", out_slabs=N_CHIPS)
        g_pg = _mm(s["h3"], dpre, "tn", out_dtype=BF16, name="mm_dwpg")
        dx2, gs["norm3_g"] = _mm(dpre, w["w_pg"], "nt", out_dtype=F32, name="mm_dh3", tm_max=1024,
                                 rms_bwd=(s["x2"], a["norm3_g"][l], dx))
        g_down = _mm(s["act"], dx2, "tn", out_dtype=BF16, name="mm_dwdown")
        dhf_g, dhf_v, sums_g, sums_v = _ffn_bwd(dx2, w["w_down"], s["gc"], s["vc"], s["hf_g"], s["hf_v"], q["ffn_conv_w"])
        gs["ffn_conv_w"] = jnp.concatenate([sums_g[0:3], sums_v[0:3]], axis=1)
        gs["ffn_conv_b"] = jnp.concatenate([sums_g[3:4], sums_v[3:4]], axis=1)
        g_up = _mm(s["h2"], dhf_g, "tn", out_dtype=BF16, name="mm_dwup_g", out_slabs=N_CHIPS, out_n=2 * f)
        g_up = _mm(s["h2"], dhf_v, "tn", out_dtype=BF16, name="mm_dwup_v", out_slabs=N_CHIPS, out_n=2 * f, o_noff=f, out_buf=g_up)
        early = {"w_pe": g_pe, "w_up": g_up, "w_pg": g_pg.reshape(N_CHIPS, -1, g_pg.shape[-1]),
                 "w_down": g_down.reshape(N_CHIPS, -1, g_down.shape[-1])}
        early_handle, token = _push_start([early[n] for n in PUSH_EARLY], None, f"push_start_{l}a")
        dh2 = _mm(dhf_g, w["w_up"], "nt", b_slabs=True, out_dtype=F32, name="mm_dh2_g", after=token)
        dx1, gs["norm2_g"] = _mm(dhf_v, w["w_up"], "nt", b_slabs=True, b_koff=f, res=dh2, out_dtype=F32, name="mm_dh2_v",
                                 tm_max=1024, rms_bwd=(s["x1"], a["norm2_g"][l], dx2))
        g_out = _mm(s["mix"], dx1, "tn", out_dtype=BF16, name="mm_dwout")
        dmix = _mm(dx1, w["w_out"], "nt", out_dtype=F32, name="mm_dmix")
        dz, gs["a_ln_g"], gs["a_ln_b"], dws, dbs_t = _mix_a_bwd(s["z"], dmix, q["a_ln_g"], q["a_ln_b"], q["wm"], q["wm_t"],
                                                               q["bs_t"])
        gs["a_ws"] = dws * tril
        gs["a_bs"] = dbs_t.reshape(GMLP_CHUNK, N_HEADS, HEAD_DIM).sum(-1).T
        dz, gs["b_conv_w"], gs["b_conv_b"], dwa, dwx, gs["b_ba"], gs["b_bx"], gs["b_lam"] = _mix_b_bwd(
            s["z"], dz, dmix, s["hs"], q["b_conv_w"], q["b_conv_b"], q["b_wa"], q["b_wx"], q["b_wa_t"], q["b_wx_t"],
            q["b_ba"], q["b_bx"], q["b_lam"])
        gs["b_wa"], gs["b_wx"] = _diag_blocks(dwa), _diag_blocks(dwx)
        dz, gs["c_lb"], dng = _mix_c_bwd(s["z"], dz, dmix, s["o_pre"], s["states"], q["lb"], q["ng"])
        gs["c_norm_g"] = dng.reshape(N_HEADS, HEAD_DIM).sum(0)
        dz, dwd, gs["d_scale"] = _mix_d_bwd(s["z"], dz, dmix, q["d_w"], q["d_w_t"], q["d_scale"])
        gs["d_w"] = _diag_blocks(dwd)
        small = [gs[n] for n in SMALL_MID] + ([g_final] if l == nl - 1 else [])
        mid_handle, token = _push_start([g_out.reshape(N_CHIPS, -1, g_out.shape[-1])], _pack_small(small), f"push_start_{l}b")
        g_in = _mm(s["h1"], dz, "tn", out_dtype=BF16, name="mm_dwin")
        dx, gs["norm1_g"] = _mm(dz, w["w_in"], "nt", out_dtype=F32, name="mm_dh1", after=(token, g_in), tm_max=1024,
                                rms_bwd=(s["x0"], a["norm1_g"][l], dx1))

        late_handle, token = _push_start([_cols_to_slabs(g_in)], _pack_small([gs["norm1_g"]]), f"push_start_{l}c")
        dep = token
        for push in pending:
            dep = finish_push(*push, dep)
        pending = [(l, PUSH_EARLY, early_handle, "a"), (l, PUSH_MID, mid_handle, "b"), (l, PUSH_LATE, late_handle, "c")]
    for push in pending:
        dep = finish_push(*push, dep)
    grad_x = dx[None]

    def small_shape(n):
        return a[n].shape[1:-1] + (a[n].shape[-1] * N_CHIPS,) if n in SMALL_SHARDED else a[n].shape[1:]

    per_layer = {n: [] for n in SMALL_PER_LAYER}
    for l in range(nl):
        per_layer["norm1_g"].append(small_sums[l, "c"].reshape(-1)[:d])
        vec, off = small_sums[l, "b"].reshape(-1), 0
        for n in SMALL_MID:
            shape = small_shape(n)
            size = 1
            for dim in shape:
                size *= dim
            per_layer[n].append(vec[off:off + size].reshape(shape))
            off += size
        if l == nl - 1:
            grad_final = vec[off:off + d]
    grads = {n: jnp.stack(per_layer[n]) for n in SMALL_PER_LAYER}
    grads["c_lb"] = lbs_vjp(grads["c_lb"])[0]
    grads["final_g"] = grad_final
    for n in SMALL_SHARDED:
        cs = a[n].shape[-1]
        grads[n] = lax.dynamic_slice_in_dim(grads[n], my_shard * cs, cs, axis=2)

    outs = {}
    for n in WEIGHTS:
        if n in BIG:
            outs[n] = stacked[n]
        else:
            outs[n] = (grads[n],) + _adamw(a[n], grads[n], a["m_" + n], a["v_" + n], "adamw_" + n)
    return (loss, grad_x, *[outs[n][0] for n in WEIGHTS], *[outs[n][1] for n in WEIGHTS], *[outs[n][2] for n in WEIGHTS],
            *[outs[n][3] for n in WEIGHTS])
```

```python
import functools

import jax
import jax.numpy as jnp
from jax import lax
from jax.experimental import pallas as pl
from jax.experimental.pallas import tpu as pltpu

F32 = jnp.float32
BF16 = jnp.bfloat16
EPS = 1e-6
HEAD_DIM = 64
N_HEADS = 4
W_GRP = HEAD_DIM * N_HEADS
GMLP_CHUNK = 128
HGRN_CHUNK = 64
HGRN_UNROLL = 8
RGLRU_C = 8.0
POOL_HALO = 16
EXP_CLAMP = 80.0
ADAM_LR, ADAM_B1, ADAM_B2, ADAM_EPS, ADAM_WD, ADAM_STEP = 0.001, 0.9, 0.999, 1e-08, 0.01, 10
VMEM_LIMIT_BYTES = 60 * 1024 * 1024
TILE_PREFS = (1024, 1408, 768, 512, 256, 128)
MM_ROW_TILES = (2048, 1024, 512, 256, 128, 64, 32, 16, 8)
MM_VMEM_BUDGET = 49 * 1024 * 1024
ROW_TILE_PREFS = (512, 256, 128, 64, 32, 16, 8)
MESH_ID = pl.DeviceIdType.MESH
N_DEV = 8


def _pick(n, prefs=TILE_PREFS):
    for p in prefs:
        if n % p == 0:
            return p
    return n


def _cp(*sem):
    return pltpu.CompilerParams(dimension_semantics=sem if sem else None, vmem_limit_bytes=VMEM_LIMIT_BYTES)


def _sds(shape, dtype):
    return jax.ShapeDtypeStruct(tuple(shape), dtype)


_GELU_C = 0.7978845608028654
_GELU_A = 0.044715


def _gelu(x):
    hx = 0.5 * x
    return hx + hx * jnp.tanh(x * (_GELU_C + (_GELU_C * _GELU_A) * (x * x)))


def _gelu_and_grad(x):
    x2 = x * x
    t = jnp.tanh(x * (_GELU_C + (_GELU_C * _GELU_A) * x2))
    hx = 0.5 * x
    g = hx + hx * t
    dg = (0.5 + 0.5 * t) + (hx * (1.0 - t * t)) * (_GELU_C + (3.0 * _GELU_C * _GELU_A) * x2)
    return g, dg


def _sigmoid(x):
    return 1.0 / (1.0 + jnp.exp(-x))


def _dot(a, b):
    return jnp.dot(a, b, preferred_element_type=F32)


def _dot_nt(a, b):
    return lax.dot_general(a, b, (((1,), (1,)), ((), ())), preferred_element_type=F32)


def _dot_tn(a, b):
    return lax.dot_general(a, b, (((0,), (0,)), ((), ())), preferred_element_type=F32)


def _split3(x):
    hi = x.astype(BF16)
    r1 = x - hi.astype(F32)
    mid = r1.astype(BF16)
    lo = (r1 - mid.astype(F32)).astype(BF16)
    return hi, mid, lo


def _dot_f32_rhs_exact(x, m_bf16):
    hi, mid, lo = _split3(x)
    return _dot(hi, m_bf16) + _dot(mid, m_bf16) + _dot(lo, m_bf16)


def _dot_f32_lhs_exact(m_bf16, x):
    hi, mid, lo = _split3(x)
    return _dot(m_bf16, hi) + _dot(m_bf16, mid) + _dot(m_bf16, lo)


def _head_masks(width=W_GRP):
    lane = lax.broadcasted_iota(jnp.int32, (1, width), 1)
    return [(lane >= h * HEAD_DIM) & (lane < (h + 1) * HEAD_DIM) for h in range(N_HEADS)]


def _block_mask(n=W_GRP):
    r = lax.broadcasted_iota(jnp.int32, (n, n), 0)
    c = lax.broadcasted_iota(jnp.int32, (n, n), 1)
    m = None
    for h in range(N_HEADS):
        mh = (r >= h * HEAD_DIM) & (r < (h + 1) * HEAD_DIM) & (c >= h * HEAD_DIM) & (c < (h + 1) * HEAD_DIM)
        m = mh if m is None else (m | mh)
    return m


def _mm(a, b, mode, *, out_dtype, name, res=None, b_slabs=False, n=None, b_noff=0, b_koff=0,
        out_slabs=0, out_buf=None, out_n=None, o_noff=0, after=(), norm_g=None, rms_bwd=None, tm_max=None):
    after = () if after is None else (tuple(after) if isinstance(after, (tuple, list)) else (after,))
    if mode == "tn":
        k_dim, m_dim = a.shape
    else:
        m_dim, k_dim = a.shape
    if mode == "nt":
        n_dim = b.shape[-2]
    else:
        n_dim = n if n is not None else (b.shape[0] * b.shape[2] if b_slabs else b.shape[1])
    n_total = out_n if out_n is not None else n_dim
    tm, tn, tk = _pick(m_dim), _pick(n_dim), _pick(k_dim)
    if tm_max is not None:
        tm = _pick(m_dim, tuple(p for p in TILE_PREFS if p <= tm_max))
    if b_slabs and mode == "nt":
        tk = _pick(b.shape[2])
    elif b_slabs:
        tn = _pick(b.shape[2])
    elif out_slabs:
        tn = _pick(n_total // out_slabs)
    def vmem_bytes(rows, kk):
        blocks = rows * kk * a.dtype.itemsize + kk * tn * b.dtype.itemsize + rows * tn * jnp.dtype(out_dtype).itemsize
        blocks += rows * tn * 4 * ((res is not None) + 2 * (rms_bwd is not None) + (norm_g is not None))
        return 2 * blocks + rows * tn * 4 * (1 if kk == k_dim else 2)

    row_cap = tm_max if tm_max is not None else ((1024 if n_dim // tn > 1 else 512) if mode == "tn" else MM_ROW_TILES[0])
    k_options = (tk,) if tk == k_dim else (k_dim, tk)
    choice = next(((rows, kk) for kk in k_options for rows in MM_ROW_TILES
                   if rows <= row_cap and m_dim % rows == 0 and rows >= min(256, m_dim) and vmem_bytes(rows, kk) <= MM_VMEM_BUDGET),
                  None)
    if choice is not None:
        tm, tk = choice
    nk = k_dim // tk
    assert b_noff % tn == 0 and b_koff % tk == 0 and o_noff % tn == 0 and n_dim % tn == 0 and k_dim % tk == 0
    bn0, bk0, on0 = b_noff // tn, b_koff // tk, o_noff // tn
    dims = {"nn": (((1,), (0,)), ((), ())), "nt": (((1,), (1,)), ((), ())), "tn": (((0,), (0,)), ((), ()))}[mode]

    if mode == "tn":
        a_spec = pl.BlockSpec((tk, tm), lambda i, j, k: (k, i))
    else:
        a_spec = pl.BlockSpec((tm, tk), lambda i, j, k: (i, k))
    slab_group = 0
    if not b_slabs:
        if mode == "nt":
            b_spec = pl.BlockSpec((tn, tk), lambda i, j, k: (j + bn0, k + bk0))
        else:
            b_spec = pl.BlockSpec((tk, tn), lambda i, j, k: (k + bk0, j + bn0))
    elif mode == "nt" and tk > b.shape[2]:
        slab_group = tk // b.shape[2]
        b_spec = pl.BlockSpec((slab_group, tn, b.shape[2]), lambda i, j, k: (bk0, j, 0))
    elif mode == "nt":
        bper = b.shape[2] // tk
        b_spec = pl.BlockSpec((None, tn, tk), lambda i, j, k: ((k + bk0) // bper, j, (k + bk0) % bper))
    else:
        bper = b.shape[2] // tn
        b_spec = pl.BlockSpec((None, tk, tn), lambda i, j, k: ((j + bn0) // bper, k, (j + bn0) % bper))
    in_specs = [a_spec, b_spec]
    args = [a, b]
    if res is not None:
        in_specs.append(pl.BlockSpec((tm, tn), lambda i, j, k: (i, j)))
        args.append(res)
    if out_slabs:
        oper = n_total // out_slabs // tn
        out_shape = _sds((out_slabs, m_dim, n_total // out_slabs), out_dtype)
        out_spec = pl.BlockSpec((None, tm, tn), lambda i, j, k: ((j + on0) // oper, i, (j + on0) % oper))
    else:
        out_shape = _sds((m_dim, n_total), out_dtype)
        out_spec = pl.BlockSpec((tm, tn), lambda i, j, k: (i, j + on0))
    out_specs, out_shapes = [out_spec], [out_shape]
    row_spec = pl.BlockSpec((tm, tn), lambda i, j, k: (i, 0))
    vec_spec = pl.BlockSpec((1, tn), lambda i, j, k: (0, 0))
    norm_at = rms_at = None
    if norm_g is not None:
        assert tn == n_dim and not out_slabs
        norm_at = len(args)
        in_specs.append(vec_spec)
        args.append(norm_g.reshape(1, n_dim))
        out_specs.append(row_spec)
        out_shapes.append(_sds((m_dim, n_dim), BF16))
    if rms_bwd is not None:
        assert tn == n_dim and not out_slabs and norm_g is None
        x_in, gain, dres = rms_bwd
        rms_at = len(args)
        in_specs += [row_spec, vec_spec, row_spec]
        args += [x_in, gain.reshape(1, n_dim), dres]
        out_specs, out_shapes = [row_spec, vec_spec], [_sds((m_dim, n_dim), F32), _sds((1, n_dim), F32)]
    aliases = {}
    if out_buf is not None:
        in_specs.append(pl.BlockSpec(memory_space=pl.ANY))
        args.append(out_buf)
        aliases = {len(args) - 1: 0}
    for dep in after:
        in_specs.append(pl.BlockSpec(memory_space=pl.ANY))
        args.append(dep)
    has_res = res is not None
    n_in = len(args)

    def body(*refs):
        a_ref, b_ref = refs[0], refs[1]
        res_ref = refs[2] if has_res else None
        o_ref = refs[n_in]
        acc_ref = refs[-1] if nk > 1 else None

        def product(rows):
            if slab_group:
                cs = b.shape[2]
                terms = [lax.dot_general(a_ref[rows, s * cs:(s + 1) * cs].astype(BF16), b_ref[s].astype(BF16), dims,
                                         preferred_element_type=F32) for s in range(slab_group)]
                return functools.reduce(lambda p, q: p + q, terms)
            lhs = a_ref[rows, :] if mode != "tn" else a_ref[:, rows]
            return lax.dot_general(lhs.astype(BF16), b_ref[...].astype(BF16), dims, preferred_element_type=F32)

        def finish(v, rows):
            if has_res:
                v = v + res_ref[rows, :]
            if rms_at is not None:
                xv, gv = refs[rms_at][rows, :], refs[rms_at + 1][...]
                r = lax.rsqrt(jnp.mean(xv * xv, axis=-1, keepdims=True) + EPS)
                dyg = v * gv
                dot = jnp.mean(dyg * xv, axis=-1, keepdims=True)
                o_ref[rows, :] = refs[rms_at + 2][rows, :] + r * dyg - xv * (r * r * r) * dot
                return jnp.sum(v * xv * r, axis=0, keepdims=True)
            o_ref[rows, :] = v.astype(o_ref.dtype)
            if norm_at is not None:
                r = lax.rsqrt(jnp.mean(v * v, axis=-1, keepdims=True) + EPS)
                refs[n_in + 1][rows, :] = (v * r * refs[norm_at][...]).astype(BF16)
            return None

        def add_gain_grad(gpart):
            dg_ref = refs[n_in + 1]
            first = pl.program_id(0) == 0

            @pl.when(first)
            def _():
                dg_ref[...] = gpart

            @pl.when(jnp.logical_not(first))
            def _():
                dg_ref[...] += gpart

        whole = slice(0, tm)
        if nk == 1:
            gpart = finish(product(whole), whole)
            if rms_at is not None:
                add_gain_grad(gpart)
        else:
            part = product(whole)
            kk = pl.program_id(2)

            @pl.when(kk == 0)
            def _():
                acc_ref[...] = part

            @pl.when(kk > 0)
            def _():
                acc_ref[...] += part

            @pl.when(kk == nk - 1)
            def _():
                gpart = finish(acc_ref[...], whole)
                if rms_at is not None:
                    add_gain_grad(gpart)

    outs = pl.pallas_call(
        body, grid=(m_dim // tm, n_dim // tn, nk), in_specs=in_specs, out_specs=out_specs, out_shape=out_shapes,
        scratch_shapes=[pltpu.VMEM((tm, tn), F32)] if nk > 1 else [],
        input_output_aliases=aliases, name=name,
        compiler_params=_cp(*(("arbitrary",) * 3 if rms_bwd is not None else ("parallel", "parallel", "arbitrary"))),
    )(*args)
    return outs[0] if len(outs) == 1 else tuple(outs)


def _rms_fwd(x, g, name):
    t, d = x.shape
    tm = _pick(t, ROW_TILE_PREFS)

    def body(x_ref, g_ref, o_ref):
        xv = x_ref[...]
        r = lax.rsqrt(jnp.mean(xv * xv, axis=-1, keepdims=True) + EPS)
        o_ref[...] = (xv * r * g_ref[...]).astype(o_ref.dtype)

    return pl.pallas_call(
        body, grid=(t // tm,),
        in_specs=[pl.BlockSpec((tm, d), lambda i: (i, 0)), pl.BlockSpec((1, d), lambda i: (0, 0))],
        out_specs=pl.BlockSpec((tm, d), lambda i: (i, 0)), out_shape=_sds((t, d), BF16),
        name=name, compiler_params=_cp("parallel"),
    )(x, g.reshape(1, d))


def _final_loss(x, g, target):
    t, d = x.shape
    tm = _pick(t, ROW_TILE_PREFS)

    def body(x_ref, g_ref, t_ref, dx_ref, dg_ref, loss_ref):
        i = pl.program_id(0)
        xv = x_ref[...]
        gv = g_ref[...]
        r = lax.rsqrt(jnp.mean(xv * xv, axis=-1, keepdims=True) + EPS)
        err = xv * r * gv - t_ref[...]
        lpart = (0.5 / d) * jnp.sum(jnp.sum(err * err, axis=1, keepdims=True), axis=0, keepdims=True)
        dy = err * (1.0 / d)
        dyg = dy * gv
        dot = jnp.mean(dyg * xv, axis=-1, keepdims=True)
        dx_ref[...] = r * dyg - xv * (r * r * r) * dot
        part = jnp.sum(dy * xv * r, axis=0, keepdims=True)

        @pl.when(i == 0)
        def _():
            dg_ref[...] = part
            loss_ref[...] = lpart

        @pl.when(i > 0)
        def _():
            dg_ref[...] += part
            loss_ref[...] += lpart

    row = pl.BlockSpec((tm, d), lambda i: (i, 0))
    vec = pl.BlockSpec((1, d), lambda i: (0, 0))
    return pl.pallas_call(
        body, grid=(t // tm,), in_specs=[row, vec, row], out_specs=[row, vec, pl.BlockSpec((1, 1), lambda i: (0, 0))],
        out_shape=[_sds((t, d), F32), _sds((1, d), F32), _sds((1, 1), F32)], name="final_loss",
        compiler_params=_cp("arbitrary"),
    )(x, g.reshape(1, d), target)


def _shift_down(ext, k, halo):
    return pltpu.roll(ext, k, 0)[halo:]


def _shift_up(ext, k, tm):
    return pltpu.roll(ext, ext.shape[0] - k, 0)[:tm]


def _ffn_tiles(t, f, max_rows=256):
    return _pick(t, tuple(r for r in (512, 256, 128, 64, 32, 16, 8) if r <= max_rows)), _pick(f, (1408, 256, 128))


def _ffn_act_fwd(hf_g, hf_v, conv_w, conv_b):
    t, f = hf_g.shape
    tm, cn = _ffn_tiles(t, f, max_rows=512)
    nf = f // cn

    def body(g_ref, v_ref, wg_ref, wv_ref, bg_ref, bv_ref, o_ref, gc_ref, vc_ref, ext_ref, hg_ref, hv_ref):
        i = pl.program_id(1)

        @pl.when(i == 0)
        def _():
            hg_ref[...] = jnp.zeros_like(hg_ref)
            hv_ref[...] = jnp.zeros_like(hv_ref)

        def conv(x_ref, halo_ref, w_ref, b_ref):
            ext_ref[0:8, :] = halo_ref[...]
            ext_ref[8:, :] = x_ref[...]
            halo_ref[...] = x_ref[tm - 8:tm, :]
            ext = ext_ref[...]
            w = w_ref[...]
            return b_ref[...] + w[2:3, :] * ext[8:] + w[1:2, :] * _shift_down(ext, 1, 8) + w[0:1, :] * _shift_down(ext, 2, 8)

        gc = conv(g_ref, hg_ref, wg_ref, bg_ref)
        vc = conv(v_ref, hv_ref, wv_ref, bv_ref)
        o_ref[...] = (_gelu(gc) * vc).astype(o_ref.dtype)
        gc_ref[...] = gc.astype(gc_ref.dtype)
        vc_ref[...] = vc.astype(vc_ref.dtype)

    blk = pl.BlockSpec((tm, cn), lambda j, i: (i, j))
    return pl.pallas_call(
        body, grid=(nf, t // tm),
        in_specs=[blk, blk, pl.BlockSpec((3, cn), lambda j, i: (0, j)), pl.BlockSpec((3, cn), lambda j, i: (0, j + nf)),
                  pl.BlockSpec((1, cn), lambda j, i: (0, j)), pl.BlockSpec((1, cn), lambda j, i: (0, j + nf))],
        out_specs=[blk, blk, blk], out_shape=[_sds((t, f), BF16)] * 3,
        scratch_shapes=[pltpu.VMEM((tm + 8, cn), F32), pltpu.VMEM((8, cn), F32), pltpu.VMEM((8, cn), F32)],
        name="ffn_act_fwd", compiler_params=_cp("parallel", "arbitrary"),
    )(hf_g, hf_v, conv_w, conv_w, conv_b, conv_b)


def _ffn_bwd(dx2, w_down, gc, vc, hf_g, hf_v, conv_w):
    t, f = hf_g.shape
    d = dx2.shape[1]
    tm, cn = _ffn_tiles(t, f)
    nf, nt = f // cn, t // tm

    def body(dx_ref, wd_ref, gc_ref, vc_ref, g_ref, v_ref, wg_ref, wv_ref, dg_ref, dv_ref, sg_ref, sv_ref,
             ext_ref, cg_ref, cv_ref):
        @pl.when(pl.program_id(1) == 0)
        def _():
            for ref in (cg_ref, cv_ref, sg_ref, sv_ref):
                ref[...] = jnp.zeros_like(ref)

        da = _dot_nt(dx_ref[...].astype(BF16), wd_ref[...])
        gel, dgel = _gelu_and_grad(gc_ref[...].astype(F32))
        dgc = da * vc_ref[...].astype(F32) * dgel
        dvc = da * gel

        def back(dc, carry_ref, w, x, out_ref, sums_ref):
            ext_ref[0:tm, :] = dc
            ext_ref[tm:tm + 8, :] = carry_ref[...]
            carry_ref[...] = dc[0:8, :]
            ext = ext_ref[...]
            up1, up2 = _shift_up(ext, 1, tm), _shift_up(ext, 2, tm)
            out_ref[...] = (w[2:3, :] * dc + w[1:2, :] * up1 + w[0:1, :] * up2).astype(out_ref.dtype)
            sums_ref[0:1, :] += jnp.sum(up2 * x, axis=0, keepdims=True)
            sums_ref[1:2, :] += jnp.sum(up1 * x, axis=0, keepdims=True)
            sums_ref[2:3, :] += jnp.sum(dc * x, axis=0, keepdims=True)
            sums_ref[3:4, :] += jnp.sum(dc, axis=0, keepdims=True)

        back(dgc, cg_ref, wg_ref[...], g_ref[...], dg_ref, sg_ref)
        back(dvc, cv_ref, wv_ref[...], v_ref[...], dv_ref, sv_ref)

    blk = pl.BlockSpec((tm, cn), lambda j, i: (nt - 1 - i, j))
    sums = pl.BlockSpec((8, cn), lambda j, i: (0, j))
    return pl.pallas_call(
        body, grid=(nf, nt),
        in_specs=[pl.BlockSpec((tm, d), lambda j, i: (nt - 1 - i, 0)), pl.BlockSpec((cn, d), lambda j, i: (j, 0)),
                  blk, blk, blk, blk, pl.BlockSpec((3, cn), lambda j, i: (0, j)), pl.BlockSpec((3, cn), lambda j, i: (0, j + nf))],
        out_specs=[blk, blk, sums, sums],
        out_shape=[_sds((t, f), BF16), _sds((t, f), BF16), _sds((8, f), F32), _sds((8, f), F32)],
        scratch_shapes=[pltpu.VMEM((tm + 8, cn), F32), pltpu.VMEM((8, cn), F32), pltpu.VMEM((8, cn), F32)],
        name="ffn_bwd", compiler_params=_cp("parallel", "arbitrary"),
    )(dx2, w_down, gc, vc, hf_g, hf_v, conv_w, conv_w)


def _ple_fwd(x2, pe, pre, next_g=None):
    t, d = x2.shape
    tm = _pick(t, ROW_TILE_PREFS)

    def body(x_ref, pe_ref, pre_ref, *rest):
        x3 = x_ref[...] + pe_ref[...] * _sigmoid(pre_ref[...])
        if next_g is None:
            rest[0][...] = x3
        else:
            g_ref, o_ref, h_ref = rest
            o_ref[...] = x3
            r = lax.rsqrt(jnp.mean(x3 * x3, axis=-1, keepdims=True) + EPS)
            h_ref[...] = (x3 * r * g_ref[...]).astype(h_ref.dtype)

    row = pl.BlockSpec((tm, d), lambda i: (i, 0))
    if next_g is None:
        return pl.pallas_call(body, grid=(t // tm,), in_specs=[row, row, row], out_specs=row,
                              out_shape=_sds((t, d), F32), name="ple_fwd", compiler_params=_cp("parallel"))(x2, pe, pre), None
    return pl.pallas_call(body, grid=(t // tm,), in_specs=[row, row, row, pl.BlockSpec((1, d), lambda i: (0, 0))],
                          out_specs=[row, row], out_shape=[_sds((t, d), F32), _sds((t, d), BF16)], name="ple_norm_fwd",
                          compiler_params=_cp("parallel"))(x2, pe, pre, next_g.reshape(1, d))


def _ple_bwd(dx3, pe, pre, after=None):
    t, d = dx3.shape
    tm = _pick(t, ROW_TILE_PREFS)

    def body(dx_ref, pe_ref, pre_ref, *rest):
        dpe_ref, dpre_ref = rest[-2:]
        gate = _sigmoid(pre_ref[...])
        dx = dx_ref[...]
        dpe_ref[...] = (dx * gate).astype(dpe_ref.dtype)
        dpre_ref[...] = (dx * pe_ref[...] * gate * (1.0 - gate)).astype(dpre_ref.dtype)

    row = pl.BlockSpec((tm, d), lambda i: (i, 0))
    extra = [] if after is None else [after]
    return pl.pallas_call(body, grid=(t // tm,), in_specs=[row, row, row] + [pl.BlockSpec(memory_space=pl.ANY)] * len(extra),
                          out_specs=[row, row], out_shape=[_sds((t, d), BF16), _sds((t, d), BF16)], name="ple_bwd",
                          compiler_params=_cp("parallel"))(dx3, pe, pre, *extra)


def _mix_tm(t):
    return _pick(t, (512, 256, 128))


def _zblk(tm, col, rev_nt=None):
    if rev_nt is None:
        return pl.BlockSpec((tm, W_GRP), lambda i: (i, col))
    return pl.BlockSpec((tm, W_GRP), lambda i: (rev_nt - 1 - i, col))


def _full(shape):
    nd = len(shape)
    return pl.BlockSpec(tuple(shape), lambda i: (0,) * nd)


def _gmlp_sv(wm_ref, vnc, bs, hm):
    sv = bs
    for h in range(N_HEADS):
        sv = sv + jnp.where(hm[h], _dot(wm_ref[h], vnc), 0.0)
    return sv


def _layernorm(v, g, b):
    mu = jnp.mean(v, axis=-1, keepdims=True)
    vc = v - mu
    rs = lax.rsqrt(jnp.mean(vc * vc, axis=-1, keepdims=True) + EPS)
    xhat = vc * rs
    return xhat, rs, xhat * g + b


def _mix_a_fwd(z, d_mix, ln_g, ln_b, wm, bs_t):
    t = z.shape[0]
    tm = _mix_tm(t)

    def body(u_ref, v_ref, g_ref, b_ref, wm_ref, bs_ref, o_ref):
        hm = _head_masks()
        ug = _gelu(u_ref[...])
        _, _, vn = _layernorm(_gelu(v_ref[...]), g_ref[...], b_ref[...])
        vnb = vn.astype(BF16)
        for n in range(tm // GMLP_CHUNK):
            sl = slice(n * GMLP_CHUNK, (n + 1) * GMLP_CHUNK)
            o_ref[sl, :] = ug[sl] * _gmlp_sv(wm_ref, vnb[sl], bs_ref[...], hm)

    return pl.pallas_call(
        body, grid=(t // tm,),
        in_specs=[_zblk(tm, 0), _zblk(tm, 1), _full((1, W_GRP)), _full((1, W_GRP)), _full(wm.shape), _full(bs_t.shape)],
        out_specs=_zblk(tm, 0), out_shape=_sds((t, d_mix), F32), name="mix_a_fwd", compiler_params=_cp("parallel"),
    )(z, z, ln_g, ln_b, wm, bs_t)


def _mix_a_bwd(z, dmix, ln_g, ln_b, wm, wm_t, bs_t):
    t, zc = z.shape
    tm = _mix_tm(t)

    def body(u_ref, v_ref, dy_ref, g_ref, b_ref, wm_ref, wmt_ref, bs_ref, dz_ref, dg_ref, db_ref, dws_ref, dbs_ref):
        i = pl.program_id(0)

        @pl.when(i == 0)
        def _():
            dg_ref[...] = jnp.zeros_like(dg_ref)
            db_ref[...] = jnp.zeros_like(db_ref)
            dws_ref[...] = jnp.zeros_like(dws_ref)
            dbs_ref[...] = jnp.zeros_like(dbs_ref)

        hm = _head_masks()
        ug, dug = _gelu_and_grad(u_ref[...])
        vg, dvg = _gelu_and_grad(v_ref[...])
        gv = g_ref[...]
        xhat, rs, vn = _layernorm(vg, gv, b_ref[...])
        vnb = vn.astype(BF16)
        dy = dy_ref[...]
        for n in range(tm // GMLP_CHUNK):
            sl = slice(n * GMLP_CHUNK, (n + 1) * GMLP_CHUNK)
            vnc = vnb[sl]
            sv = _gmlp_sv(wm_ref, vnc, bs_ref[...], hm)
            dsv = dy[sl] * ug[sl]
            dz_ref[sl, 0:W_GRP] = dy[sl] * sv * dug[sl]
            dbs_ref[...] += dsv
            dsvb = dsv.astype(BF16)
            dvn = jnp.zeros((GMLP_CHUNK, W_GRP), F32)
            for h in range(N_HEADS):
                dws_ref[h] += _dot_nt(jnp.where(hm[h], dsv, 0.0).astype(BF16), vnc)
                dvn = dvn + jnp.where(hm[h], _dot(wmt_ref[h], dsvb), 0.0)
            xh = xhat[sl]
            dg_ref[...] += jnp.sum(dvn * xh, axis=0, keepdims=True)
            db_ref[...] += jnp.sum(dvn, axis=0, keepdims=True)
            dxh = dvn * gv
            dvg_c = rs[sl] * (dxh - jnp.mean(dxh, axis=-1, keepdims=True) - xh * jnp.mean(dxh * xh, axis=-1, keepdims=True))
            dz_ref[sl, W_GRP:2 * W_GRP] = dvg_c * dvg[sl]

    return pl.pallas_call(
        body, grid=(t // tm,),
        in_specs=[_zblk(tm, 0), _zblk(tm, 1), _zblk(tm, 0), _full((1, W_GRP)), _full((1, W_GRP)), _full(wm.shape),
                  _full(wm_t.shape), _full(bs_t.shape)],
        out_specs=[pl.BlockSpec((tm, 2 * W_GRP), lambda i: (i, 0)), _full((1, W_GRP)), _full((1, W_GRP)),
                   _full(wm.shape), _full(bs_t.shape)],
        out_shape=[_sds((t, zc), F32), _sds((1, W_GRP), F32), _sds((1, W_GRP), F32), _sds(wm.shape, F32),
                   _sds(bs_t.shape, F32)],
        name="mix_a_bwd", compiler_params=_cp("arbitrary"),
    )(z, z, dmix, ln_g, ln_b, wm, wm_t, bs_t)


def _softplus(x):
    return jnp.maximum(x, 0.0) + jnp.log(1.0 + jnp.exp(-jnp.abs(x)))


def _neg_expm1(x):
    series = -x * (1.0 + x * 0.5 * (1.0 + x * (1.0 / 3.0) * (1.0 + x * 0.25 * (1.0 + x * 0.2))))
    return jnp.where(x > -0.1, series, 1.0 - jnp.exp(x))


def _rglru_gates(ext_ref, x_ref, halo, cw, cb, wa_ref, wx_ref, ba, bx, lam):
    ext_ref[0:8, :] = halo
    ext_ref[8:, :] = x_ref[...]
    ext = ext_ref[...]
    x0, x1, x2, x3 = ext[8:], _shift_down(ext, 1, 8), _shift_down(ext, 2, 8), _shift_down(ext, 3, 8)
    xc = cb + cw[3:4, :] * x0 + cw[2:3, :] * x1 + cw[1:2, :] * x2 + cw[0:1, :] * x3
    xcb = xc.astype(BF16)
    r = _sigmoid(_dot(xcb, wa_ref[...]) + ba)
    ig = _sigmoid(_dot(xcb, wx_ref[...]) + bx)
    sp = _softplus(-lam)
    la = -RGLRU_C * r * sp
    a = jnp.exp(la)
    mult = jnp.sqrt(_neg_expm1(2.0 * la))
    return (x0, x1, x2, x3), xc, r, ig, sp, a, mult


def _mix_b_fwd(z, mix, conv_w, conv_b, wa, wx, ba, bx, lam):
    t = z.shape[0]
    tm = _mix_tm(t)

    def body(x_ref, gb_ref, cw_ref, cb_ref, wa_ref, wx_ref, ba_ref, bx_ref, lam_ref, mix_in, o_ref, hs_ref,
             ext_ref, a_ref, b_ref, xh_ref, hc_ref):
        i = pl.program_id(0)

        @pl.when(i == 0)
        def _():
            xh_ref[...] = jnp.zeros_like(xh_ref)
            hc_ref[...] = jnp.zeros_like(hc_ref)

        _, xc, _, ig, _, a, mult = _rglru_gates(ext_ref, x_ref, xh_ref[...], cw_ref[...], cb_ref[...], wa_ref, wx_ref,
                                                ba_ref[...], bx_ref[...], lam_ref[...])
        xh_ref[...] = x_ref[tm - 8:tm, :]
        a_ref[...] = a
        b_ref[...] = mult * (ig * xc)
        rid = lax.broadcasted_iota(jnp.int32, (8, W_GRP), 0)

        def group(gi, hprev):
            base = pl.multiple_of(gi * 8, 8)
            ca = a_ref[pl.ds(base, 8), :]
            cb = b_ref[pl.ds(base, 8), :]
            for k in (1, 2, 4):
                m = rid >= k
                cb = jnp.where(m, ca * pltpu.roll(cb, k, 0) + cb, cb)
                ca = jnp.where(m, ca * pltpu.roll(ca, k, 0), ca)
            hh = cb + ca * hprev
            hs_ref[pl.ds(base, 8), :] = hh
            return hh[7:8, :]

        hlast = lax.fori_loop(0, tm // 8, group, hc_ref[0:1, :])
        hc_ref[...] = jnp.broadcast_to(hlast, hc_ref.shape)
        o_ref[...] = hs_ref[...] * _gelu(gb_ref[...])

    sq = _full((W_GRP, W_GRP))
    vec = _full((1, W_GRP))
    return pl.pallas_call(
        body, grid=(t // tm,),
        in_specs=[_zblk(tm, 2), _zblk(tm, 3), _full((4, W_GRP)), vec, sq, sq, vec, vec, vec, pl.BlockSpec(memory_space=pl.ANY)],
        out_specs=[_zblk(tm, 1), pl.BlockSpec((tm, W_GRP), lambda i: (i, 0))],
        out_shape=[_sds(mix.shape, F32), _sds((t, W_GRP), F32)],
        scratch_shapes=[pltpu.VMEM((tm + 8, W_GRP), F32), pltpu.VMEM((tm, W_GRP), F32), pltpu.VMEM((tm, W_GRP), F32),
                        pltpu.VMEM((8, W_GRP), F32), pltpu.VMEM((8, W_GRP), F32)],
        input_output_aliases={9: 0}, name="mix_b_fwd", compiler_params=_cp("arbitrary"),
    )(z, z, conv_w, conv_b, wa, wx, ba, bx, lam, mix)


def _mix_b_bwd(z, dz, dmix, hs, conv_w, conv_b, wa, wx, wa_t, wx_t, ba, bx, lam):
    t = z.shape[0]
    tm = _mix_tm(t)
    nt = t // tm
    hb = tm // 8

    def body(x_ref, gb_ref, xhalo_ref, hs_ref, hhalo_ref, dy_ref, cw_ref, cb_ref, wa_ref, wx_ref, wat_ref, wxt_ref,
             ba_ref, bx_ref, lam_ref, dz_in, dz_ref, dcw_ref, dcb_ref, dwa_ref, dwx_ref, dba_ref, dbx_ref, dlam_ref,
             ext_ref, c_ref, d_ref, g_ref, an_ref, gn_ref, dxn_ref):
        i = pl.program_id(0)
        first_tile = i == nt - 1

        @pl.when(i == 0)
        def _():
            for ref in (dcw_ref, dcb_ref, dwa_ref, dwx_ref, dba_ref, dbx_ref, dlam_ref, an_ref, gn_ref, dxn_ref):
                ref[...] = jnp.zeros_like(ref)

        cw, lam = cw_ref[...], lam_ref[...]
        xhalo = jnp.where(first_tile, 0.0, xhalo_ref[...])
        (x0, x1, x2, x3), xc, r, ig, sp, a, mult = _rglru_gates(
            ext_ref, x_ref, xhalo, cw, cb_ref[...], wa_ref, wx_ref, ba_ref[...], bx_ref[...], lam)
        hs = hs_ref[...]
        dy = dy_ref[...]
        gel, dgel = _gelu_and_grad(gb_ref[...])
        dz_ref[:, W_GRP:2 * W_GRP] = dy * hs * dgel

        ext_ref[0:tm, :] = a
        ext_ref[tm:tm + 8, :] = an_ref[...]
        an_ref[...] = a[0:8, :]
        c_ref[...] = _shift_up(ext_ref[...], 1, tm)
        d_ref[...] = dy * gel
        rid = lax.broadcasted_iota(jnp.int32, (8, W_GRP), 0)

        def group(j, gnext):
            base = pl.multiple_of((tm // 8 - 1 - j) * 8, 8)
            cc = c_ref[pl.ds(base, 8), :]
            cd = d_ref[pl.ds(base, 8), :]
            for k in (1, 2, 4):
                m = rid < 8 - k
                cd = jnp.where(m, cc * pltpu.roll(cd, 8 - k, 0) + cd, cd)
                cc = jnp.where(m, cc * pltpu.roll(cc, 8 - k, 0), cc)
            gg = cd + cc * gnext
            g_ref[pl.ds(base, 8), :] = gg
            return gg[0:1, :]

        gfirst = lax.fori_loop(0, tm // 8, group, gn_ref[0:1, :])
        gn_ref[...] = jnp.broadcast_to(gfirst, gn_ref.shape)
        g = g_ref[...]

        ext_ref[0:8, :] = jnp.where(first_tile, 0.0, hhalo_ref[...])
        ext_ref[8:, :] = hs
        hprev = _shift_down(ext_ref[...], 1, 8)
        da = g * hprev
        dmult = g * (ig * xc)
        di = g * mult * xc
        dxc = g * mult * ig
        dla = da * a - dmult * a * a / mult
        dr = dla * (-RGLRU_C * sp)
        dlam_ref[...] += jnp.sum(dla * (-RGLRU_C * r), axis=0, keepdims=True) * (-_sigmoid(-lam))
        dpr = dr * r * (1.0 - r)
        dpi = di * ig * (1.0 - ig)
        dprb, dpib, xcb = dpr.astype(BF16), dpi.astype(BF16), xc.astype(BF16)
        dba_ref[...] += jnp.sum(dpr, axis=0, keepdims=True)
        dbx_ref[...] += jnp.sum(dpi, axis=0, keepdims=True)
        dwa_ref[...] += _dot_tn(xcb, dprb)
        dwx_ref[...] += _dot_tn(xcb, dpib)
        dxc = dxc + _dot(dprb, wat_ref[...]) + _dot(dpib, wxt_ref[...])
        dcb_ref[...] += jnp.sum(dxc, axis=0, keepdims=True)
        dcw_ref[3:4, :] += jnp.sum(dxc * x0, axis=0, keepdims=True)
        dcw_ref[2:3, :] += jnp.sum(dxc * x1, axis=0, keepdims=True)
        dcw_ref[1:2, :] += jnp.sum(dxc * x2, axis=0, keepdims=True)
        dcw_ref[0:1, :] += jnp.sum(dxc * x3, axis=0, keepdims=True)
        ext_ref[0:tm, :] = dxc
        ext_ref[tm:tm + 8, :] = dxn_ref[...]
        dxn_ref[...] = dxc[0:8, :]
        ext = ext_ref[...]
        dz_ref[:, 0:W_GRP] = (cw[3:4, :] * dxc + cw[2:3, :] * _shift_up(ext, 1, tm) + cw[1:2, :] * _shift_up(ext, 2, tm)
                              + cw[0:1, :] * _shift_up(ext, 3, tm))

    sq = _full((W_GRP, W_GRP))
    vec = _full((1, W_GRP))
    halo = lambda col: pl.BlockSpec((8, W_GRP), lambda i: (jnp.maximum((nt - 1 - i) * hb - 1, 0), col))
    rev = lambda col: _zblk(tm, col, nt)
    return pl.pallas_call(
        body, grid=(nt,),
        in_specs=[rev(2), rev(3), halo(2), rev(0), halo(0), rev(1), _full((4, W_GRP)), vec, sq, sq, sq, sq, vec, vec, vec,
                  pl.BlockSpec(memory_space=pl.ANY)],
        out_specs=[pl.BlockSpec((tm, 2 * W_GRP), lambda i: (nt - 1 - i, 1)), _full((4, W_GRP)), vec, sq, sq, vec, vec, vec],
        out_shape=[_sds(dz.shape, F32), _sds((4, W_GRP), F32), _sds((1, W_GRP), F32), _sds((W_GRP, W_GRP), F32),
                   _sds((W_GRP, W_GRP), F32), _sds((1, W_GRP), F32), _sds((1, W_GRP), F32), _sds((1, W_GRP), F32)],
        scratch_shapes=[pltpu.VMEM((tm + 8, W_GRP), F32), pltpu.VMEM((tm, W_GRP), F32), pltpu.VMEM((tm, W_GRP), F32),
                        pltpu.VMEM((tm, W_GRP), F32), pltpu.VMEM((8, W_GRP), F32), pltpu.VMEM((8, W_GRP), F32),
                        pltpu.VMEM((8, W_GRP), F32)],
        input_output_aliases={15: 0}, name="mix_b_bwd", compiler_params=_cp("arbitrary"),
    )(z, z, z, hs, hs, dmix, conv_w, conv_b, wa, wx, wa_t, wx_t, ba, bx, lam, dz)


def _tri(n, lower):
    r = lax.broadcasted_iota(jnp.int32, (n, n), 0)
    c = lax.broadcasted_iota(jnp.int32, (n, n), 1)
    return jnp.where((r >= c) if lower else (r <= c), 1.0, 0.0).astype(BF16)


def _causal_stack():
    r = lax.broadcasted_iota(jnp.int32, (N_HEADS * HGRN_CHUNK, HGRN_CHUNK), 0)
    c = lax.broadcasted_iota(jnp.int32, (N_HEADS * HGRN_CHUNK, HGRN_CHUNK), 1)
    m = None
    for h in range(N_HEADS):
        mh = (r >= h * HGRN_CHUNK) & (r < (h + 1) * HGRN_CHUNK) & (r - h * HGRN_CHUNK >= c)
        m = mh if m is None else (m | mh)
    return m


def _stack_heads(x, hm):
    return jnp.concatenate([jnp.where(hm[h], x, 0.0) for h in range(N_HEADS)], axis=0)


def _unstack_heads(xs, hm):
    out = jnp.where(hm[0], xs[0:HGRN_CHUNK], 0.0)
    for h in range(1, N_HEADS):
        out = out + jnp.where(hm[h], xs[h * HGRN_CHUNK:(h + 1) * HGRN_CHUNK], 0.0)
    return out


def _hgrn_chunk(qv, fv, lb, tril):
    sq = _sigmoid(qv)
    qq = qv * sq
    sg = _sigmoid(fv)
    fg = lb + (1.0 - lb) * sg
    kk = 1.0 - fg
    bb = _dot_f32_lhs_exact(tril, jnp.log(fg))
    b_last = bb[HGRN_CHUNK - 1:HGRN_CHUNK, :]
    b_mid = bb[HGRN_CHUNK // 2 - 1:HGRN_CHUNK // 2, :]
    eq = jnp.exp(jnp.minimum(bb - b_mid, EXP_CLAMP))
    ek = jnp.exp(jnp.minimum(b_mid - bb, EXP_CLAMP))
    eb = jnp.exp(bb)
    el = jnp.exp(b_last - bb)
    return sq, qq, sg, fg, kk, b_last, eq, ek, eb, el


def _seg_mean(x, avg):
    return _dot_f32_rhs_exact(x, avg)


def _mix_c_fwd(z, mix, lb, ng):
    t = z.shape[0]
    tm = _mix_tm(t)
    nch = tm // HGRN_CHUNK

    def body(q_ref, f_ref, i_ref, g_ref, lb_ref, ng_ref, mix_in, y_ref, o_ref, ss_ref, s_ref):
        @pl.when(pl.program_id(0) == 0)
        def _():
            s_ref[...] = jnp.zeros_like(s_ref)

        hm = _head_masks()
        bmask = _block_mask()
        causal = _causal_stack()
        tril = _tri(HGRN_CHUNK, True)
        avg = jnp.where(bmask, 1.0 / HEAD_DIM, 0.0).astype(BF16)
        lb, ng = lb_ref[...], ng_ref[...]

        def chunk(c, carry):
            rows = pl.ds(pl.multiple_of(c * HGRN_CHUNK, HGRN_CHUNK), HGRN_CHUNK)
            vv = i_ref[rows, :]
            gv = g_ref[rows, :]
            _, qq, _, _, kk, b_last, eq, ek, eb, el = _hgrn_chunk(q_ref[rows, :], f_ref[rows, :], lb, tril)
            vb = vv.astype(BF16)
            qs = _stack_heads(qq * eq, hm).astype(BF16)
            att = jnp.where(causal, _dot_nt(qs, (kk * ek).astype(BF16)), 0.0)
            o = _unstack_heads(_dot(att.astype(BF16), vb), hm)
            s0 = s_ref[...]
            ss_ref[c] = s0
            o = o + _dot_nt((qq * eb).astype(BF16), s0.astype(BF16))
            s_ref[...] = s0 * jnp.exp(b_last) + jnp.where(bmask, _dot_tn(vb, (kk * el).astype(BF16)), 0.0)
            o_ref[rows, :] = o
            rstd = lax.rsqrt(_seg_mean(o * o, avg) + EPS)
            y_ref[rows, :] = o * rstd * ng * (gv * _sigmoid(gv))
            return carry

        lax.fori_loop(0, nch, chunk, 0, unroll=HGRN_UNROLL)

    vec = _full((1, W_GRP))
    return pl.pallas_call(
        body, grid=(t // tm,),
        in_specs=[_zblk(tm, 4), _zblk(tm, 5), _zblk(tm, 6), _zblk(tm, 7), vec, vec, pl.BlockSpec(memory_space=pl.ANY)],
        out_specs=[_zblk(tm, 2), pl.BlockSpec((tm, W_GRP), lambda i: (i, 0)),
                   pl.BlockSpec((nch, W_GRP, W_GRP), lambda i: (i, 0, 0))],
        out_shape=[_sds(mix.shape, F32), _sds((t, W_GRP), F32), _sds((t // HGRN_CHUNK, W_GRP, W_GRP), F32)],
        scratch_shapes=[pltpu.VMEM((W_GRP, W_GRP), F32)],
        input_output_aliases={6: 0}, name="mix_c_fwd", compiler_params=_cp("arbitrary"),
    )(z, z, z, z, lb, ng, mix)


def _mix_c_bwd(z, dz, dmix, o_pre, states, lb, ng):
    t = z.shape[0]
    tm = _mix_tm(t)
    nt = t // tm
    nch = tm // HGRN_CHUNK

    def body(q_ref, f_ref, i_ref, g_ref, o_ref, ss_ref, dy_ref, lb_ref, ng_ref, dz_in, dz_ref, dlb_ref, dng_ref, ds_ref):
        @pl.when(pl.program_id(0) == 0)
        def _():
            ds_ref[...] = jnp.zeros_like(ds_ref)
            dlb_ref[...] = jnp.zeros_like(dlb_ref)
            dng_ref[...] = jnp.zeros_like(dng_ref)

        hm = _head_masks()
        bmask = _block_mask()
        causal = _causal_stack()
        tril = _tri(HGRN_CHUNK, True)
        triu = _tri(HGRN_CHUNK, False)
        avg = jnp.where(bmask, 1.0 / HEAD_DIM, 0.0).astype(BF16)
        lb, ng = lb_ref[...], ng_ref[...]
        last_row = lax.broadcasted_iota(jnp.int32, (HGRN_CHUNK, W_GRP), 0) == HGRN_CHUNK - 1

        def chunk(j, carry):
            c = nch - 1 - j
            rows = pl.ds(pl.multiple_of(c * HGRN_CHUNK, HGRN_CHUNK), HGRN_CHUNK)
            qv, gv, vv = q_ref[rows, :], g_ref[rows, :], i_ref[rows, :]
            sq, qq, sg, fg, kk, b_last, eq, ek, eb, el = _hgrn_chunk(qv, f_ref[rows, :], lb, tril)
            s0 = ss_ref[c]
            ds1 = ds_ref[...]
            o = o_ref[rows, :]
            dy = dy_ref[rows, :]
            rstd = lax.rsqrt(_seg_mean(o * o, avg) + EPS)
            oh = o * rstd
            sgg = _sigmoid(gv)
            dz_ref[rows, 3 * W_GRP:4 * W_GRP] = dy * oh * ng * (sgg * (1.0 + gv * (1.0 - sgg)))
            don = dy * gv * sgg
            dng_ref[...] += jnp.sum(don * oh, axis=0, keepdims=True)
            doh = don * ng
            do = rstd * (doh - oh * _seg_mean(doh * oh, avg))
            qt, kt, qh, kh = qq * eq, kk * ek, qq * eb, kk * el
            vb, dob = vv.astype(BF16), do.astype(BF16)
            ktb, khb = kt.astype(BF16), kh.astype(BF16)
            ds1b = ds1.astype(BF16)
            qs = _stack_heads(qt, hm).astype(BF16)
            dos = _stack_heads(do, hm).astype(BF16)
            att = jnp.where(causal, _dot_nt(qs, ktb), 0.0).astype(BF16)
            datt = jnp.where(causal, _dot_nt(dos, vb), 0.0).astype(BF16)
            dv = _dot_tn(att, dos) + _dot_nt(khb, ds1b)
            dqt = _unstack_heads(_dot(datt, ktb), hm)
            dkt = _dot_tn(datt, qs)
            dqh = _dot(dob, s0.astype(BF16))
            dkh = _dot(vb, ds1b)
            e_last = jnp.exp(b_last)
            ds_ref[...] = ds1 * e_last + jnp.where(bmask, _dot_tn(dob, qh.astype(BF16)), 0.0)
            dq = dqt * eq + dqh * eb
            dk = dkt * ek + dkh * el
            db = qt * dqt - kt * dkt + qh * dqh - kh * dkh
            db_last = jnp.sum(kh * dkh, axis=0, keepdims=True) + e_last * jnp.sum(ds1 * s0, axis=0, keepdims=True)
            db = db + jnp.where(last_row, db_last, 0.0)
            dlogf = _dot_f32_lhs_exact(triu, db)
            dfg = dlogf / fg - dk
            dz_ref[rows, W_GRP:2 * W_GRP] = dfg * (1.0 - lb) * sg * (1.0 - sg)
            dlb_ref[...] += jnp.sum(dfg * (1.0 - sg), axis=0, keepdims=True)
            dz_ref[rows, 0:W_GRP] = dq * (sq * (1.0 + qv * (1.0 - sq)))
            dz_ref[rows, 2 * W_GRP:3 * W_GRP] = dv
            return carry

        lax.fori_loop(0, nch, chunk, 0, unroll=HGRN_UNROLL)

    vec = _full((1, W_GRP))
    rev = lambda col: _zblk(tm, col, nt)
    return pl.pallas_call(
        body, grid=(nt,),
        in_specs=[rev(4), rev(5), rev(6), rev(7), rev(0), pl.BlockSpec((nch, W_GRP, W_GRP), lambda i: (nt - 1 - i, 0, 0)),
                  rev(2), vec, vec, pl.BlockSpec(memory_space=pl.ANY)],
        out_specs=[pl.BlockSpec((tm, 4 * W_GRP), lambda i: (nt - 1 - i, 1)), vec, vec],
        out_shape=[_sds(dz.shape, F32), _sds((1, W_GRP), F32), _sds((1, W_GRP), F32)],
        scratch_shapes=[pltpu.VMEM((W_GRP, W_GRP), F32)],
        input_output_aliases={9: 0}, name="mix_c_bwd", compiler_params=_cp("arbitrary"),
    )(z, z, z, z, o_pre, states, dmix, lb, ng, dz)


def _pool_select(hm, s2, s4, s8, s16):
    return jnp.where(hm[0], s2, jnp.where(hm[1], s4, jnp.where(hm[2], s8, s16)))


def _pool_counts(hm, row0, tm):
    pos = (row0 + 1 + lax.broadcasted_iota(jnp.int32, (tm, W_GRP), 0)).astype(F32)
    win = _pool_select(hm, 2.0, 4.0, 8.0, 16.0)
    return jnp.minimum(pos, win)


def _pooled(ext_ref, x, halo, hm, cnt):
    ext_ref[0:POOL_HALO, :] = halo
    ext_ref[POOL_HALO:, :] = x
    e = ext_ref[...]
    s2 = e + pltpu.roll(e, 1, 0)
    s4 = s2 + pltpu.roll(s2, 2, 0)
    s8 = s4 + pltpu.roll(s4, 4, 0)
    s16 = s8 + pltpu.roll(s8, 8, 0)
    return _pool_select(hm, s2, s4, s8, s16)[POOL_HALO:] / cnt - x


def _mix_d_fwd(z, mix, wd, scale):
    t = z.shape[0]
    tm = _mix_tm(t)

    def body(x_ref, wd_ref, sc_ref, mix_in, o_ref, ext_ref, halo_ref):
        i = pl.program_id(0)

        @pl.when(i == 0)
        def _():
            halo_ref[...] = jnp.zeros_like(halo_ref)

        hm = _head_masks()
        x = x_ref[...]
        pooled = _pooled(ext_ref, x, halo_ref[...], hm, _pool_counts(hm, i * tm, tm))
        halo_ref[...] = x_ref[tm - POOL_HALO:tm, :]
        o_ref[...] = _dot(pooled.astype(BF16), wd_ref[...]) * sc_ref[...]

    return pl.pallas_call(
        body, grid=(t // tm,),
        in_specs=[_zblk(tm, 8), _full((W_GRP, W_GRP)), _full((1, W_GRP)), pl.BlockSpec(memory_space=pl.ANY)],
        out_specs=_zblk(tm, 3), out_shape=_sds(mix.shape, F32),
        scratch_shapes=[pltpu.VMEM((tm + POOL_HALO, W_GRP), F32), pltpu.VMEM((POOL_HALO, W_GRP), F32)],
        input_output_aliases={3: 0}, name="mix_d_fwd", compiler_params=_cp("arbitrary"),
    )(z, wd, scale, mix)


def _mix_d_bwd(z, dz, dmix, wd, wd_t, scale):
    t = z.shape[0]
    tm = _mix_tm(t)
    nt = t // tm
    hb = tm // POOL_HALO

    def body(x_ref, xhalo_ref, dy_ref, wd_ref, wdt_ref, sc_ref, dz_in, dz_ref, dwd_ref, dsc_ref, ext_ref, en_ref):
        i = pl.program_id(0)
        ri = nt - 1 - i

        @pl.when(i == 0)
        def _():
            en_ref[...] = jnp.zeros_like(en_ref)
            dwd_ref[...] = jnp.zeros_like(dwd_ref)
            dsc_ref[...] = jnp.zeros_like(dsc_ref)

        hm = _head_masks()
        cnt = _pool_counts(hm, ri * tm, tm)
        x = x_ref[...]
        pooled = _pooled(ext_ref, x, jnp.where(ri == 0, 0.0, xhalo_ref[...]), hm, cnt)
        pb = pooled.astype(BF16)
        dy = dy_ref[...]
        dsc_ref[...] += jnp.sum(dy * _dot(pb, wd_ref[...]), axis=0, keepdims=True)
        dyw = (dy * sc_ref[...]).astype(BF16)
        dwd_ref[...] += _dot_tn(pb, dyw)
        dpool = _dot(dyw, wdt_ref[...])
        e = dpool / cnt
        ext_ref[0:tm, :] = e
        ext_ref[tm:, :] = en_ref[...]
        en_ref[...] = e[0:POOL_HALO, :]
        ee = ext_ref[...]
        n = tm + POOL_HALO
        r2 = ee + pltpu.roll(ee, n - 1, 0)
        r4 = r2 + pltpu.roll(r2, n - 2, 0)
        r8 = r4 + pltpu.roll(r4, n - 4, 0)
        r16 = r8 + pltpu.roll(r8, n - 8, 0)
        dz_ref[...] = _pool_select(hm, r2, r4, r8, r16)[:tm] - dpool

    sq = _full((W_GRP, W_GRP))
    vec = _full((1, W_GRP))
    return pl.pallas_call(
        body, grid=(nt,),
        in_specs=[_zblk(tm, 8, nt), pl.BlockSpec((POOL_HALO, W_GRP), lambda i: (jnp.maximum((nt - 1 - i) * hb - 1, 0), 8)),
                  _zblk(tm, 3, nt), sq, sq, vec, pl.BlockSpec(memory_space=pl.ANY)],
        out_specs=[_zblk(tm, 8, nt), sq, vec],
        out_shape=[_sds(dz.shape, F32), _sds((W_GRP, W_GRP), F32), _sds((1, W_GRP), F32)],
        scratch_shapes=[pltpu.VMEM((tm + POOL_HALO, W_GRP), F32), pltpu.VMEM((POOL_HALO, W_GRP), F32)],
        input_output_aliases={6: 0}, name="mix_d_bwd", compiler_params=_cp("arbitrary"),
    )(z, z, dmix, wd, wd_t, scale, dz)


def _as2d(a):
    if a.ndim == 1:
        return a.reshape(1, a.shape[0])
    return a.reshape(-1, a.shape[-1])


def _adamw(w, g, m, v, name):
    shape = w.shape
    w2, g2, m2, v2 = _as2d(w), _as2d(g), _as2d(m), _as2d(v)
    rows, cols = w2.shape
    tr = _pick(rows, (1024, 512, 256, 128, 64, 32, 16, 8))
    if tr * cols * 4 * 14 > VMEM_LIMIT_BYTES:
        tr = _pick(rows, (256, 128, 64, 32, 16, 8))

    def body(w_ref, g_ref, m_ref, v_ref, d_ref, nm_ref, nv_ref):
        gv = g_ref[...]
        mn = ADAM_B1 * m_ref[...] + (1.0 - ADAM_B1) * gv
        vn = ADAM_B2 * v_ref[...] + (1.0 - ADAM_B2) * (gv * gv)
        m_hat = mn / (1.0 - ADAM_B1 ** ADAM_STEP)
        v_hat = vn / (1.0 - ADAM_B2 ** ADAM_STEP)
        d_ref[...] = -ADAM_LR * (m_hat / (jnp.sqrt(v_hat) + ADAM_EPS) + ADAM_WD * w_ref[...])
        nm_ref[...] = mn
        nv_ref[...] = vn

    blk = pl.BlockSpec((tr, cols), lambda i: (i, 0))
    outs = pl.pallas_call(
        body, grid=(rows // tr,), in_specs=[blk] * 4, out_specs=[blk] * 3, out_shape=[_sds((rows, cols), F32)] * 3,
        name=name, compiler_params=_cp("parallel"),
    )(w2, g2, m2, v2)
    return tuple(o.reshape(shape) for o in outs)


def _adamw_layer(w, g, m, v, layer, bufs, name):
    nl, r, cs = w.shape
    tr = _pick(r, (256, 128, 64, 32, 16, 8))

    def body(w_ref, g_ref, m_ref, v_ref, *rest):
        go_ref, d_ref, nm_ref, nv_ref = rest[-4:]
        gv = g_ref[...]
        mn = ADAM_B1 * m_ref[...] + (1.0 - ADAM_B1) * gv
        vn = ADAM_B2 * v_ref[...] + (1.0 - ADAM_B2) * (gv * gv)
        m_hat = mn / (1.0 - ADAM_B1 ** ADAM_STEP)
        v_hat = vn / (1.0 - ADAM_B2 ** ADAM_STEP)
        go_ref[...] = gv
        d_ref[...] = -ADAM_LR * (m_hat / (jnp.sqrt(v_hat) + ADAM_EPS) + ADAM_WD * w_ref[...])
        nm_ref[...] = mn
        nv_ref[...] = vn

    lay = pl.BlockSpec((None, tr, cs), lambda i: (layer, i, 0))
    in_specs = [lay, pl.BlockSpec((tr, cs), lambda i: (i, 0)), lay, lay]
    args = [w, g, m, v]
    aliases = {}
    if bufs is not None:
        in_specs += [pl.BlockSpec(memory_space=pl.ANY)] * 4
        args += list(bufs)
        aliases = {4 + i: i for i in range(4)}
    return pl.pallas_call(
        body, grid=(r // tr,), in_specs=in_specs, out_specs=[lay] * 4, out_shape=[_sds((nl, r, cs), F32)] * 4,
        input_output_aliases=aliases, name=name, compiler_params=_cp("parallel"),
    )(*args)


def _slot_sum(own, slots, name):
    n_slots, rows, cols = slots.shape
    whole_fits = rows * cols * 4 * (n_slots + 2) * 2 <= VMEM_LIMIT_BYTES // 2
    tr = rows if whole_fits else _pick(rows, (512, 352, 256, 128, 64, 32, 16, 8))

    def body(*refs):
        s_ref, o_ref = refs[-2], refs[-1]
        acc = s_ref[0].astype(F32) if own is None else refs[0][...].astype(F32) + s_ref[0].astype(F32)
        for k in range(1, n_slots):
            acc = acc + s_ref[k].astype(F32)
        o_ref[...] = acc

    row = pl.BlockSpec((tr, cols), lambda i: (i, 0))
    return pl.pallas_call(
        body, grid=(rows // tr,),
        in_specs=([] if own is None else [row]) + [pl.BlockSpec((n_slots, tr, cols), lambda i: (0, i, 0))],
        out_specs=row, out_shape=_sds((rows, cols), F32), name=name, compiler_params=_cp("parallel"),
    )(*(() if own is None else (own,)), slots)


def _me():
    return lax.axis_index("x"), lax.axis_index("y"), lax.axis_index("c")


def _other_chips(x, y):
    return [(1 - x, y), (x, 1 - y), (1 - x, 1 - y)]


ANY_SPEC = pl.BlockSpec(memory_space=pl.ANY)
HBM_SPEC = pl.BlockSpec(memory_space=pltpu.HBM)
SEM_SPEC = pl.BlockSpec(memory_space=pltpu.SEMAPHORE)
SPLIT_COPY_PARAMS = pltpu.CompilerParams(has_side_effects=pltpu.SideEffectType.DATAFLOW_SIDE_EFFECTING)
N_CHIPS = 4


def _aligned(v, m):
    return v if isinstance(v, int) else pl.multiple_of(v, m)


def _in_hbm(arr):
    return pltpu.with_memory_space_constraint(arr, pltpu.HBM)


def _peer(x, y, c, k):
    fx, fy, fc = (k >> 2) & 1, (k >> 1) & 1, k & 1
    px = 1 - x if fx else x
    py = 1 - y if fy else y
    pc = 1 - c if fc else c
    return px, py, pc


def _gather_start(shards, after, name):
    n = len(shards)

    def body(*refs):
        src, land = refs[:n], refs[n:2 * n]
        send_sems, recv_sems = refs[2 * n + 1], refs[2 * n + 2]
        token = refs[-1]
        x, y, c = _me()
        for w in range(n):
            for chip in _other_chips(x, y):
                pltpu.make_async_remote_copy(
                    src_ref=src[w], dst_ref=land[w].at[2 * x + y], send_sem=send_sems.at[w], recv_sem=recv_sems.at[w],
                    device_id=(*chip, c), device_id_type=MESH_ID).start()
        token[...] = jnp.zeros_like(token)

    lands = [lax.empty((N_CHIPS,) + s.shape, s.dtype) for s in shards]
    thru = [pltpu.HBM(s.shape, s.dtype) for s in shards] + [pltpu.HBM(z.shape, z.dtype) for z in lands]
    outs = pl.pallas_call(
        body, name=name,
        out_shape=(pltpu.SemaphoreType.DMA((n,)), pltpu.SemaphoreType.DMA((n,)), *thru, _sds((8, 128), F32)),
        in_specs=[HBM_SPEC] * (2 * n) + [ANY_SPEC],
        out_specs=(SEM_SPEC, SEM_SPEC, *[HBM_SPEC] * (2 * n), pl.BlockSpec(memory_space=pltpu.VMEM)),
        input_output_aliases={i: 2 + i for i in range(2 * n)}, compiler_params=SPLIT_COPY_PARAMS,
    )(*[_in_hbm(s) for s in shards], *[_in_hbm(z) for z in lands], after)
    return (outs[0], outs[1], outs[2:2 + n], outs[2 + n:2 + 2 * n]), outs[-1]


def _gather_wait(send_sems, recv_sems, srcs, lands, after, name):
    n = len(srcs)

    def body(*refs):
        land = refs[n:2 * n]
        send_sems, recv_sems = refs[2 * n], refs[2 * n + 1]
        x, y, c = _me()
        for w in range(n):
            three = land[w].at[pl.ds(0, N_CHIPS - 1)]
            cp = pltpu.make_async_remote_copy(src_ref=three, dst_ref=three, send_sem=send_sems.at[w], recv_sem=recv_sems.at[w],
                                              device_id=(x, y, c), device_id_type=MESH_ID)
            cp.wait_send()
            cp.wait_recv()

    both = list(srcs) + list(lands)
    outs = pl.pallas_call(
        body, name=name, out_shape=tuple(pltpu.HBM(b.shape, b.dtype) for b in both),
        in_specs=[HBM_SPEC] * (2 * n) + [SEM_SPEC, SEM_SPEC, ANY_SPEC], out_specs=[HBM_SPEC] * (2 * n),
        input_output_aliases={i: i for i in range(2 * n)}, compiler_params=SPLIT_COPY_PARAMS,
    )(*both, send_sems, recv_sems, after)
    return outs[n:2 * n]


def _push_start(grads, small, name):
    n = len(grads)
    srcs = list(grads) + ([] if small is None else [small])
    ns = len(srcs)

    def body(*refs):
        src, slots = refs[:ns], refs[ns:2 * ns]
        send_sems, recv_sems = refs[2 * ns], refs[2 * ns + 1]
        token = refs[-1]
        x, y, c = _me()
        for w in range(ns):
            for k in range(1, N_DEV):
                px, py, pc = _peer(x, y, c, k)
                if w < n:
                    hr = src[w].shape[1] // 2
                    piece = src[w].at[2 * px + py, pl.ds(_aligned(pc * hr, 16), hr), :]
                    slot = slots[w].at[k - 1]
                else:
                    piece = src[w]
                    slot = slots[w].at[4 * x + 2 * y + c]
                pltpu.make_async_remote_copy(
                    src_ref=piece, dst_ref=slot, send_sem=send_sems.at[w], recv_sem=recv_sems.at[w],
                    device_id=(px, py, pc), device_id_type=MESH_ID).start()
        token[...] = jnp.zeros_like(token)

    slots = [lax.empty((N_DEV - 1, g.shape[1] // 2, g.shape[2]), g.dtype) for g in grads]
    if small is not None:
        slots.append(lax.empty((N_DEV,) + small.shape, small.dtype))
    both = srcs + slots
    outs = pl.pallas_call(
        body, name=name,
        out_shape=(pltpu.SemaphoreType.DMA((ns,)), pltpu.SemaphoreType.DMA((ns,)),
                   *[pltpu.HBM(b.shape, b.dtype) for b in both], _sds((8, 128), F32)),
        in_specs=[HBM_SPEC] * len(both),
        out_specs=(SEM_SPEC, SEM_SPEC, *[HBM_SPEC] * len(both), pl.BlockSpec(memory_space=pltpu.VMEM)),
        input_output_aliases={i: 2 + i for i in range(len(both))}, compiler_params=SPLIT_COPY_PARAMS,
    )(*[_in_hbm(b) for b in both])
    return (outs[0], outs[1], outs[2:2 + ns], outs[2 + ns:2 + 2 * ns]), outs[-1]


def _push_wait(send_sems, recv_sems, srcs, slots, after, name):
    n = len(srcs)

    def body(*refs):
        slot = refs[n:2 * n]
        send_sems, recv_sems = refs[2 * n], refs[2 * n + 1]
        x, y, c = _me()
        for w in range(n):
            seven = slot[w].at[pl.ds(0, N_DEV - 1)]
            cp = pltpu.make_async_remote_copy(src_ref=seven, dst_ref=seven, send_sem=send_sems.at[w],
                                              recv_sem=recv_sems.at[w], device_id=(x, y, c), device_id_type=MESH_ID)
            cp.wait_send()
            cp.wait_recv()

    both = list(srcs) + list(slots)
    outs = pl.pallas_call(
        body, name=name, out_shape=tuple(pltpu.HBM(b.shape, b.dtype) for b in both),
        in_specs=[HBM_SPEC] * (2 * n) + [SEM_SPEC, SEM_SPEC, ANY_SPEC], out_specs=[HBM_SPEC] * (2 * n),
        input_output_aliases={i: i for i in range(2 * n)}, compiler_params=SPLIT_COPY_PARAMS,
    )(*both, send_sems, recv_sems, after)
    return outs[:n], outs[n:]


SWAP_CHUNK_BYTES = 2 * 1024 * 1024


def _swap_chunk_rows(hr, cs):
    ch = hr
    while ch * cs * 4 > SWAP_CHUNK_BYTES and ch % 16 == 0:
        ch //= 2
    return ch


def _swap_halves(halves, name):
    n = len(halves)
    chunk = [_swap_chunk_rows(*h.shape) for h in halves]
    rounds = max(h.shape[0] // ch for h, ch in zip(halves, chunk))

    def body(*refs):
        src, dst, buf = refs[:n], refs[n:2 * n], refs[2 * n:3 * n]
        load_sems, put_sems, send_sems, recv_sems = refs[3 * n:]
        x, y, c = _me()
        sibling = (x, y, 1 - c)
        for j in range(rounds):
            live = [w for w in range(n) if j < src[w].shape[0] // chunk[w]]
            loads = [pltpu.make_async_copy(src[w].at[pl.ds(j * chunk[w], chunk[w])], buf[w], load_sems.at[w]) for w in live]
            for ld in loads:
                ld.start()
            moves = []
            for ld, w in zip(loads, live):
                ld.wait()
                rows = pl.ds(_aligned(c * src[w].shape[0] + j * chunk[w], 8), chunk[w])
                put = pltpu.make_async_copy(buf[w], dst[w].at[rows], put_sems.at[w])
                send = pltpu.make_async_remote_copy(src_ref=buf[w], dst_ref=dst[w].at[rows], send_sem=send_sems.at[w],
                                                    recv_sem=recv_sems.at[w], device_id=sibling, device_id_type=MESH_ID)
                put.start()
                send.start()
                moves.append((put, send))
            for put, send in moves:
                put.wait()
                send.wait_send()
        for w in range(n):
            hr = src[w].shape[0]
            got = dst[w].at[pl.ds(_aligned((1 - c) * hr, 8), hr)]
            pltpu.make_async_remote_copy(src_ref=got, dst_ref=got, send_sem=send_sems.at[w], recv_sem=recv_sems.at[w],
                                         device_id=sibling, device_id_type=MESH_ID).wait_recv()

    return pl.pallas_call(
        body, in_specs=[ANY_SPEC] * n, out_specs=[ANY_SPEC] * n,
        out_shape=[_sds((2 * h.shape[0], h.shape[1]), F32) for h in halves],
        scratch_shapes=[pltpu.VMEM((ch, h.shape[1]), F32) for h, ch in zip(halves, chunk)]
        + [pltpu.SemaphoreType.DMA((n,))] * 4,
        name=name,
    )(*halves)


BIG = ("w_in", "w_out", "w_up", "w_down", "w_pe", "w_pg")
ROW_SHARDED = ("w_out", "w_down", "w_pg")
SMALL = ("norm1_g", "a_ln_g", "a_ln_b", "a_ws", "a_bs", "b_conv_w", "b_conv_b", "b_wa", "b_ba", "b_wx", "b_bx", "b_lam",
         "c_lb", "c_norm_g", "d_w", "d_scale", "norm2_g", "ffn_conv_w", "ffn_conv_b", "norm3_g", "final_g")
SMALL_SHARDED = ("b_conv_w", "ffn_conv_w")
WEIGHTS = ("norm1_g", "w_in", "a_ln_g", "a_ln_b", "a_ws", "a_bs", "b_conv_w", "b_conv_b", "b_wa", "b_ba", "b_wx", "b_bx",
           "b_lam", "c_lb", "c_norm_g", "d_w", "d_scale", "w_out", "norm2_g", "w_up", "ffn_conv_w", "ffn_conv_b", "w_down",
           "norm3_g", "w_pe", "w_pg", "final_g")
ARGS = ("x", "p") + WEIGHTS + ("loss_target",) + tuple("m_" + n for n in WEIGHTS) + tuple("v_" + n for n in WEIGHTS)


def _block_diag(w):
    eye = jnp.eye(N_HEADS, dtype=w.dtype)
    return (eye[None, :, None, :, None] * w[:, :, :, None, :]).reshape(w.shape[0], W_GRP, W_GRP)


def _diag_blocks(m):
    m4 = m.reshape(N_HEADS, HEAD_DIM, N_HEADS, HEAD_DIM)
    return jnp.stack([m4[h, :, h, :] for h in range(N_HEADS)])


def _lower_bounds(c_lb):
    lbs = jnp.cumsum(jax.nn.softmax(c_lb, axis=0), axis=0)
    return lbs - lbs[0:1]


def kernel(x, p, norm1_g, w_in, a_ln_g, a_ln_b, a_ws, a_bs, b_conv_w, b_conv_b, b_wa, b_ba, b_wx, b_bx, b_lam, c_lb, c_norm_g, d_w, d_scale, w_out, norm2_g, w_up, ffn_conv_w, ffn_conv_b, w_down, norm3_g, w_pe, w_pg, final_g, loss_target, m_norm1_g, m_w_in, m_a_ln_g, m_a_ln_b, m_a_ws, m_a_bs, m_b_conv_w, m_b_conv_b, m_b_wa, m_b_ba, m_b_wx, m_b_bx, m_b_lam, m_c_lb, m_c_norm_g, m_d_w, m_d_scale, m_w_out, m_norm2_g, m_w_up, m_ffn_conv_w, m_ffn_conv_b, m_w_down, m_norm3_g, m_w_pe, m_w_pg, m_final_g, v_norm1_g, v_w_in, v_a_ln_g, v_a_ln_b, v_a_ws, v_a_bs, v_b_conv_w, v_b_conv_b, v_b_wa, v_b_ba, v_b_wx, v_b_bx, v_b_lam, v_c_lb, v_c_norm_g, v_d_w, v_d_scale, v_w_out, v_norm2_g, v_w_up, v_ffn_conv_w, v_ffn_conv_b, v_w_down, v_norm3_g, v_w_pe, v_w_pg, v_final_g):
    return _step((x, p, norm1_g, w_in, a_ln_g, a_ln_b, a_ws, a_bs, b_conv_w, b_conv_b, b_wa, b_ba, b_wx, b_bx, b_lam, c_lb, c_norm_g, d_w, d_scale, w_out, norm2_g, w_up, ffn_conv_w, ffn_conv_b, w_down, norm3_g, w_pe, w_pg, final_g, loss_target, m_norm1_g, m_w_in, m_a_ln_g, m_a_ln_b, m_a_ws, m_a_bs, m_b_conv_w, m_b_conv_b, m_b_wa, m_b_ba, m_b_wx, m_b_bx, m_b_lam, m_c_lb, m_c_norm_g, m_d_w, m_d_scale, m_w_out, m_norm2_g, m_w_up, m_ffn_conv_w, m_ffn_conv_b, m_w_down, m_norm3_g, m_w_pe, m_w_pg, m_final_g, v_norm1_g, v_w_in, v_a_ln_g, v_a_ln_b, v_a_ws, v_a_bs, v_b_conv_w, v_b_conv_b, v_b_wa, v_b_ba, v_b_wx, v_b_bx, v_b_lam, v_c_lb, v_c_norm_g, v_d_w, v_d_scale, v_w_out, v_norm2_g, v_w_up, v_ffn_conv_w, v_ffn_conv_b, v_w_down, v_norm3_g, v_w_pe, v_w_pg, v_final_g))


SMALL_PER_LAYER = tuple(n for n in SMALL if n != "final_g")
GATHERED = BIG + SMALL_SHARDED
GATHER_LAYER0 = (("a", ("w_in", "b_conv_w")), ("b", ("w_out", "w_up", "ffn_conv_w", "w_down", "w_pe", "w_pg")))
PUSH_EARLY = ("w_pe", "w_pg", "w_down", "w_up")
PUSH_MID = ("w_out",)
PUSH_LATE = ("w_in",)
SMALL_MID = tuple(n for n in SMALL_PER_LAYER if n != "norm1_g")


def _cols_to_slabs(m):
    r, c4 = m.shape
    return jnp.moveaxis(m.reshape(r, N_CHIPS, c4 // N_CHIPS), 1, 0)


def _slabs_to_cols(s):
    return jnp.moveaxis(s, 0, 1).reshape(s.shape[1], -1)


def _pack_small(parts):
    flat = jnp.concatenate([p.reshape(-1) for p in parts])
    return jnp.pad(flat, (0, (-flat.shape[0]) % 1024)).reshape(-1, 128)


def _step(args):
    a = dict(zip(ARGS, args, strict=True))
    x0 = a["x"][0]
    target = a["loss_target"][0]
    nl = a["norm1_g"].shape[0]
    t, d = x0.shape
    f = a["w_down"].shape[1] * N_CHIPS
    cx, cy, cc = _me()
    my_shard = 2 * cx + cy
    shards = {n: a[n].astype(BF16) for n in BIG}
    shards.update({n: a[n] for n in SMALL_SHARDED})

    def start_gather(l, names, after, tag):
        return _gather_start([shards[n][l] for n in names], after, f"gather_start_{l}{tag}")

    def finish_gather(l, names, handle, after, tag):
        send, recv, srcs, lands = handle
        lands = _gather_wait(send, recv, srcs, lands, after, f"gather_wait_{l}{tag}")
        w = {}
        for n, land in zip(names, lands):
            full = lax.dynamic_update_slice(land, shards[n][l][None], (my_shard, 0, 0))
            if n in ROW_SHARDED:
                w[n] = full.reshape(-1, full.shape[-1])
            elif n in ("w_up", "w_pe"):
                w[n] = full
            else:
                w[n] = _slabs_to_cols(full)
        return w

    lbs, lbs_vjp = jax.vjp(_lower_bounds, a["c_lb"])
    tril = jnp.tril(jnp.ones((GMLP_CHUNK, GMLP_CHUNK), F32))

    stacked_params = {"wm": (a["a_ws"] * tril).astype(BF16)}
    stacked_params["wm_t"] = jnp.swapaxes(stacked_params["wm"], 2, 3)
    stacked_params["bs_t"] = jnp.repeat(jnp.swapaxes(a["a_bs"], 1, 2), HEAD_DIM, axis=2)
    for nm in ("b_wa", "b_wx", "d_w"):
        bd = _block_diag(a[nm]).astype(BF16)
        stacked_params[nm], stacked_params[nm + "_t"] = bd, jnp.swapaxes(bd, 1, 2)
    for nm in ("a_ln_g", "a_ln_b", "b_conv_b", "b_ba", "b_bx", "b_lam", "d_scale"):
        stacked_params[nm] = a[nm].reshape(nl, 1, W_GRP)
    stacked_params["lb"] = lbs.reshape(nl, 1, W_GRP)
    stacked_params["ng"] = jnp.tile(a["c_norm_g"], (1, N_HEADS)).reshape(nl, 1, W_GRP)
    stacked_params["ffn_conv_b"] = a["ffn_conv_b"].reshape(nl, 1, 2 * f)

    def layer_params(l, w):
        q = {k: v[l] for k, v in stacked_params.items()}
        q.update(w)
        return q

    saved, weights, params = [], [], []
    first_groups = {tag: start_gather(0, names, x0, tag)[0] for tag, names in GATHER_LAYER0}
    xl = x0
    for l in range(nl):
        if l == 0:
            w = finish_gather(0, GATHER_LAYER0[0][1], first_groups["a"], xl, "a")
        else:
            w = finish_gather(l, GATHERED, next_handle, xl, "")
        s = {"x0": xl}
        s["h1"] = _rms_fwd(xl, a["norm1_g"][0], "rms1_fwd") if l == 0 else h_next
        token = None
        if 0 < l < nl - 1:
            next_handle, token = start_gather(l + 1, GATHERED, s["h1"], "")
        s["z"] = _mm(s["h1"], w["w_in"], "nn", out_dtype=F32, name="mm_z", after=token)
        q = layer_params(l, w)
        mix = _mix_a_fwd(s["z"], d, q["a_ln_g"], q["a_ln_b"], q["wm"], q["bs_t"])
        mix, s["hs"] = _mix_b_fwd(s["z"], mix, q["b_conv_w"], q["b_conv_b"], q["b_wa"], q["b_wx"], q["b_ba"], q["b_bx"],
                                  q["b_lam"])
        mix, s["o_pre"], s["states"] = _mix_c_fwd(s["z"], mix, q["lb"], q["ng"])
        s["mix"] = _mix_d_fwd(s["z"], mix, q["d_w"], q["d_scale"])
        def land(tag, after):
            if l == 0:
                w.update(finish_gather(0, dict(GATHER_LAYER0)[tag], first_groups[tag], after, tag))
                q.update(w)

        land("b", s["mix"])
        token = None
        if l == 0 and nl > 1:
            next_handle, token = start_gather(1, GATHERED, w["w_out"], "")
        s["x1"], s["h2"] = _mm(s["mix"], w["w_out"], "nn", res=xl, out_dtype=F32, name="mm_out",
                               norm_g=a["norm2_g"][l], tm_max=1024, after=token)
        s["hf_g"] = _mm(s["h2"], w["w_up"], "nn", b_slabs=True, n=f, out_dtype=F32, name="mm_up_g")
        s["hf_v"] = _mm(s["h2"], w["w_up"], "nn", b_slabs=True, n=f, b_noff=f, out_dtype=F32, name="mm_up_v")
        s["act"], s["gc"], s["vc"] = _ffn_act_fwd(s["hf_g"], s["hf_v"], q["ffn_conv_w"], q["ffn_conv_b"])
        s["x2"], s["h3"] = _mm(s["act"], w["w_down"], "nn", res=s["x1"], out_dtype=F32, name="mm_down",
                               norm_g=a["norm3_g"][l], tm_max=1024)
        s["pre"] = _mm(s["h3"], w["w_pg"], "nn", out_dtype=F32, name="mm_pg")
        s["pe"] = _mm(a["p"][l, 0], w["w_pe"], "nn", b_slabs=True, out_dtype=F32, name="mm_pe")
        xl, h_next = _ple_fwd(s["x2"], s["pe"], s["pre"], a["norm1_g"][l + 1] if l + 1 < nl else None)
        saved.append(s)
        weights.append(w)
        params.append(q)

    dx, g_final, loss = _final_loss(xl, a["final_g"], target)
    loss = lax.psum(loss[0, 0], ("x", "y", "c"))

    stacked = {n: None for n in BIG}
    small_sums = {}

    def finish_push(l, names, handle, tag, after):
        send, recv, srcs, slots = handle
        srcs, slots = _push_wait(send, recv, srcs, slots, after, f"push_wait_{l}{tag}")
        halves = []
        for n, g, sl in zip(names, srcs, slots):
            hr = g.shape[1] // 2
            own = lax.dynamic_slice(g, (my_shard, cc * hr, 0), (1, hr, g.shape[2]))[0]
            halves.append(_slot_sum(own, sl, "sum_" + n))
        for n, g in zip(names, _swap_halves(halves, "swap_halves_" + tag)):
            stacked[n] = _adamw_layer(a[n], g, a["m_" + n], a["v_" + n], l, stacked[n], "adamw_" + n)
        if len(srcs) > len(names):
            by_sender = lax.dynamic_update_slice(slots[-1], srcs[-1][None], (2 * my_shard + cc, 0, 0))
            small_sums[l, tag] = _slot_sum(None, by_sender, "sum_small_" + tag)
        return stacked[names[-1]][1]

    pending = []
    token = None
    for l in reversed(range(nl)):
        q, s, w = params[l], saved[l], weights[l]
        gs = {}
        dpe, dpre = _ple_bwd(dx, s["pe"], s["pre"], after=token)
        g_pe = _mm(a["p"][l, 0], dpe, "tn", out_dtype=BF16, name="mm_dwpe", out_slabs=N_CHIPS)
        g_pg = _mm(s["h3"], dpre, "tn", out_dtype=BF16, name="mm_dwpg")
        dx2, gs["norm3_g"] = _mm(dpre, w["w_pg"], "nt", out_dtype=F32, name="mm_dh3", tm_max=1024,
                                 rms_bwd=(s["x2"], a["norm3_g"][l], dx))
        g_down = _mm(s["act"], dx2, "tn", out_dtype=BF16, name="mm_dwdown")
        dhf_g, dhf_v, sums_g, sums_v = _ffn_bwd(dx2, w["w_down"], s["gc"], s["vc"], s["hf_g"], s["hf_v"], q["ffn_conv_w"])
        gs["ffn_conv_w"] = jnp.concatenate([sums_g[0:3], sums_v[0:3]], axis=1)
        gs["ffn_conv_b"] = jnp.concatenate([sums_g[3:4], sums_v[3:4]], axis=1)
        g_up = _mm(s["h2"], dhf_g, "tn", out_dtype=BF16, name="mm_dwup_g", out_slabs=N_CHIPS, out_n=2 * f)
        g_up = _mm(s["h2"], dhf_v, "tn", out_dtype=BF16, name="mm_dwup_v", out_slabs=N_CHIPS, out_n=2 * f, o_noff=f, out_buf=g_up)
        early = {"w_pe": g_pe, "w_up": g_up, "w_pg": g_pg.reshape(N_CHIPS, -1, g_pg.shape[-1]),
                 "w_down": g_down.reshape(N_CHIPS, -1, g_down.shape[-1])}
        early_handle, token = _push_start([early[n] for n in PUSH_EARLY], None, f"push_start_{l}a")
        dh2 = _mm(dhf_g, w["w_up"], "nt", b_slabs=True, out_dtype=F32, name="mm_dh2_g", after=token)
        dx1, gs["norm2_g"] = _mm(dhf_v, w["w_up"], "nt", b_slabs=True, b_koff=f, res=dh2, out_dtype=F32, name="mm_dh2_v",
                                 tm_max=1024, rms_bwd=(s["x1"], a["norm2_g"][l], dx2))
        g_out = _mm(s["mix"], dx1, "tn", out_dtype=BF16, name="mm_dwout")
        dmix = _mm(dx1, w["w_out"], "nt", out_dtype=F32, name="mm_dmix")
        dz, gs["a_ln_g"], gs["a_ln_b"], dws, dbs_t = _mix_a_bwd(s["z"], dmix, q["a_ln_g"], q["a_ln_b"], q["wm"], q["wm_t"],
                                                               q["bs_t"])
        gs["a_ws"] = dws * tril
        gs["a_bs"] = dbs_t.reshape(GMLP_CHUNK, N_HEADS, HEAD_DIM).sum(-1).T
        dz, gs["b_conv_w"], gs["b_conv_b"], dwa, dwx, gs["b_ba"], gs["b_bx"], gs["b_lam"] = _mix_b_bwd(
            s["z"], dz, dmix, s["hs"], q["b_conv_w"], q["b_conv_b"], q["b_wa"], q["b_wx"], q["b_wa_t"], q["b_wx_t"],
            q["b_ba"], q["b_bx"], q["b_lam"])
        gs["b_wa"], gs["b_wx"] = _diag_blocks(dwa), _diag_blocks(dwx)
        dz, gs["c_lb"], dng = _mix_c_bwd(s["z"], dz, dmix, s["o_pre"], s["states"], q["lb"], q["ng"])
        gs["c_norm_g"] = dng.reshape(N_HEADS, HEAD_DIM).sum(0)
        dz, dwd, gs["d_scale"] = _mix_d_bwd(s["z"], dz, dmix, q["d_w"], q["d_w_t"], q["d_scale"])
        gs["d_w"] = _diag_blocks(dwd)
        small = [gs[n] for n in SMALL_MID] + ([g_final] if l == nl - 1 else [])
        mid_handle, token = _push_start([g_out.reshape(N_CHIPS, -1, g_out.shape[-1])], _pack_small(small), f"push_start_{l}b")
        g_in = _mm(s["h1"], dz, "tn", out_dtype=BF16, name="mm_dwin")
        dx, gs["norm1_g"] = _mm(dz, w["w_in"], "nt", out_dtype=F32, name="mm_dh1", after=(token, g_in), tm_max=1024,
                                rms_bwd=(s["x0"], a["norm1_g"][l], dx1))

        late_handle, token = _push_start([_cols_to_slabs(g_in)], _pack_small([gs["norm1_g"]]), f"push_start_{l}c")
        dep = token
        for push in pending:
            dep = finish_push(*push, dep)
        pending = [(l, PUSH_EARLY, early_handle, "a"), (l, PUSH_MID, mid_handle, "b"), (l, PUSH_LATE, late_handle, "c")]
    for push in pending:
        dep = finish_push(*push, dep)
    grad_x = dx[None]

    def small_shape(n):
        return a[n].shape[1:-1] + (a[n].shape[-1] * N_CHIPS,) if n in SMALL_SHARDED else a[n].shape[1:]

    per_layer = {n: [] for n in SMALL_PER_LAYER}
    for l in range(nl):
        per_layer["norm1_g"].append(small_sums[l, "c"].reshape(-1)[:d])
        vec, off = small_sums[l, "b"].reshape(-1), 0
        for n in SMALL_MID:
            shape = small_shape(n)
            size = 1
            for dim in shape:
                size *= dim
            per_layer[n].append(vec[off:off + size].reshape(shape))
            off += size
        if l == nl - 1:
            grad_final = vec[off:off + d]
    grads = {n: jnp.stack(per_layer[n]) for n in SMALL_PER_LAYER}
    grads["c_lb"] = lbs_vjp(grads["c_lb"])[0]
    grads["final_g"] = grad_final
    for n in SMALL_SHARDED:
        cs = a[n].shape[-1]
        grads[n] = lax.dynamic_slice_in_dim(grads[n], my_shard * cs, cs, axis=2)

    outs = {}
    for n in WEIGHTS:
        if n in BIG:
            outs[n] = stacked[n]
        else:
            outs[n] = (grads[n],) + _adamw(a[n], grads[n], a["m_" + n], a["v_" + n], "adamw_" + n)
    return (loss, grad_x, *[outs[n][0] for n in WEIGHTS], *[outs[n][1] for n in WEIGHTS], *[outs[n][2] for n in WEIGHTS],
            *[outs[n][3] for n in WEIGHTS])
```

```python
import functools

import jax
import jax.numpy as jnp
from jax import lax
from jax.experimental import pallas as pl
from jax.experimental.pallas import tpu as pltpu

F32 = jnp.float32
BF16 = jnp.bfloat16
EPS = 1e-6
HEAD_DIM = 64
N_HEADS = 4
W_GRP = HEAD_DIM * N_HEADS
GMLP_CHUNK = 128
HGRN_CHUNK = 64
HGRN_UNROLL = 8
RGLRU_C = 8.0
POOL_HALO = 16
EXP_CLAMP = 80.0
ADAM_LR, ADAM_B1, ADAM_B2, ADAM_EPS, ADAM_WD, ADAM_STEP = 0.001, 0.9, 0.999, 1e-08, 0.01, 10
VMEM_LIMIT_BYTES = 60 * 1024 * 1024
TILE_PREFS = (1024, 1408, 768, 512, 256, 128)
MM_ROW_TILES = (2048, 1024, 512, 256, 128, 64, 32, 16, 8)
MM_VMEM_BUDGET = 49 * 1024 * 1024
ROW_TILE_PREFS = (512, 256, 128, 64, 32, 16, 8)
MESH_ID = pl.DeviceIdType.MESH
N_DEV = 8


def _pick(n, prefs=TILE_PREFS):
    for p in prefs:
        if n % p == 0:
            return p
    return n


def _cp(*sem):
    return pltpu.CompilerParams(dimension_semantics=sem if sem else None, vmem_limit_bytes=VMEM_LIMIT_BYTES)


def _sds(shape, dtype):
    return jax.ShapeDtypeStruct(tuple(shape), dtype)


_GELU_C = 0.7978845608028654
_GELU_A = 0.044715


def _gelu(x):
    hx = 0.5 * x
    return hx + hx * jnp.tanh(x * (_GELU_C + (_GELU_C * _GELU_A) * (x * x)))


def _gelu_and_grad(x):
    x2 = x * x
    t = jnp.tanh(x * (_GELU_C + (_GELU_C * _GELU_A) * x2))
    hx = 0.5 * x
    g = hx + hx * t
    dg = (0.5 + 0.5 * t) + (hx * (1.0 - t * t)) * (_GELU_C + (3.0 * _GELU_C * _GELU_A) * x2)
    return g, dg


def _sigmoid(x):
    return 1.0 / (1.0 + jnp.exp(-x))


def _dot(a, b):
    return jnp.dot(a, b, preferred_element_type=F32)


def _dot_nt(a, b):
    return lax.dot_general(a, b, (((1,), (1,)), ((), ())), preferred_element_type=F32)


def _dot_tn(a, b):
    return lax.dot_general(a, b, (((0,), (0,)), ((), ())), preferred_element_type=F32)


def _split3(x):
    hi = x.astype(BF16)
    r1 = x - hi.astype(F32)
    mid = r1.astype(BF16)
    lo = (r1 - mid.astype(F32)).astype(BF16)
    return hi, mid, lo


def _dot_f32_rhs_exact(x, m_bf16):
    hi, mid, lo = _split3(x)
    return _dot(hi, m_bf16) + _dot(mid, m_bf16) + _dot(lo, m_bf16)


def _dot_f32_lhs_exact(m_bf16, x):
    hi, mid, lo = _split3(x)
    return _dot(m_bf16, hi) + _dot(m_bf16, mid) + _dot(m_bf16, lo)


def _head_masks(width=W_GRP):
    lane = lax.broadcasted_iota(jnp.int32, (1, width), 1)
    return [(lane >= h * HEAD_DIM) & (lane < (h + 1) * HEAD_DIM) for h in range(N_HEADS)]


def _block_mask(n=W_GRP):
    r = lax.broadcasted_iota(jnp.int32, (n, n), 0)
    c = lax.broadcasted_iota(jnp.int32, (n, n), 1)
    m = None
    for h in range(N_HEADS):
        mh = (r >= h * HEAD_DIM) & (r < (h + 1) * HEAD_DIM) & (c >= h * HEAD_DIM) & (c < (h + 1) * HEAD_DIM)
        m = mh if m is None else (m | mh)
    return m


def _mm(a, b, mode, *, out_dtype, name, res=None, b_slabs=False, n=None, b_noff=0, b_koff=0,
        out_slabs=0, out_buf=None, out_n=None, o_noff=0, after=(), norm_g=None, rms_bwd=None, tm_max=None):
    after = () if after is None else (tuple(after) if isinstance(after, (tuple, list)) else (after,))
    if mode == "tn":
        k_dim, m_dim = a.shape
    else:
        m_dim, k_dim = a.shape
    if mode == "nt":
        n_dim = b.shape[-2]
    else:
        n_dim = n if n is not None else (b.shape[0] * b.shape[2] if b_slabs else b.shape[1])
    n_total = out_n if out_n is not None else n_dim
    tm, tn, tk = _pick(m_dim), _pick(n_dim), _pick(k_dim)
    if tm_max is not None:
        tm = _pick(m_dim, tuple(p for p in TILE_PREFS if p <= tm_max))
    if b_slabs and mode == "nt":
        tk = _pick(b.shape[2])
    elif b_slabs:
        tn = _pick(b.shape[2])
    elif out_slabs:
        tn = _pick(n_total // out_slabs)
    def vmem_bytes(rows, kk):
        blocks = rows * kk * a.dtype.itemsize + kk * tn * b.dtype.itemsize + rows * tn * jnp.dtype(out_dtype).itemsize
        blocks += rows * tn * 4 * ((res is not None) + 2 * (rms_bwd is not None) + (norm_g is not None))
        return 2 * blocks + rows * tn * 4 * (1 if kk == k_dim else 2)

    row_cap = tm_max if tm_max is not None else ((1024 if n_dim // tn > 1 else 512) if mode == "tn" else MM_ROW_TILES[0])
    k_options = (tk,) if tk == k_dim else (k_dim, tk)
    choice = next(((rows, kk) for kk in k_options for rows in MM_ROW_TILES
                   if rows <= row_cap and m_dim % rows == 0 and rows >= min(256, m_dim) and vmem_bytes(rows, kk) <= MM_VMEM_BUDGET),
                  None)
    if choice is not None:
        tm, tk = choice
    nk = k_dim // tk
    assert b_noff % tn == 0 and b_koff % tk == 0 and o_noff % tn == 0 and n_dim % tn == 0 and k_dim % tk == 0
    bn0, bk0, on0 = b_noff // tn, b_koff // tk, o_noff // tn
    dims = {"nn": (((1,), (0,)), ((), ())), "nt": (((1,), (1,)), ((), ())), "tn": (((0,), (0,)), ((), ()))}[mode]

    if mode == "tn":
        a_spec = pl.BlockSpec((tk, tm), lambda i, j, k: (k, i))
    else:
        a_spec = pl.BlockSpec((tm, tk), lambda i, j, k: (i, k))
    slab_group = 0
    if not b_slabs:
        if mode == "nt":
            b_spec = pl.BlockSpec((tn, tk), lambda i, j, k: (j + bn0, k + bk0))
        else:
            b_spec = pl.BlockSpec((tk, tn), lambda i, j, k: (k + bk0, j + bn0))
    elif mode == "nt" and tk > b.shape[2]:
        slab_group = tk // b.shape[2]
        b_spec = pl.BlockSpec((slab_group, tn, b.shape[2]), lambda i, j, k: (bk0, j, 0))
    elif mode == "nt":
        bper = b.shape[2] // tk
        b_spec = pl.BlockSpec((None, tn, tk), lambda i, j, k: ((k + bk0) // bper, j, (k + bk0) % bper))
    else:
        bper = b.shape[2] // tn
        b_spec = pl.BlockSpec((None, tk, tn), lambda i, j, k: ((j + bn0) // bper, k, (j + bn0) % bper))
    in_specs = [a_spec, b_spec]
    args = [a, b]
    if res is not None:
        in_specs.append(pl.BlockSpec((tm, tn), lambda i, j, k: (i, j)))
        args.append(res)
    if out_slabs:
        oper = n_total // out_slabs // tn
        out_shape = _sds((out_slabs, m_dim, n_total // out_slabs), out_dtype)
        out_spec = pl.BlockSpec((None, tm, tn), lambda i, j, k: ((j + on0) // oper, i, (j + on0) % oper))
    else:
        out_shape = _sds((m_dim, n_total), out_dtype)
        out_spec = pl.BlockSpec((tm, tn), lambda i, j, k: (i, j + on0))
    out_specs, out_shapes = [out_spec], [out_shape]
    row_spec = pl.BlockSpec((tm, tn), lambda i, j, k: (i, 0))
    vec_spec = pl.BlockSpec((1, tn), lambda i, j, k: (0, 0))
    norm_at = rms_at = None
    if norm_g is not None:
        assert tn == n_dim and not out_slabs
        norm_at = len(args)
        in_specs.append(vec_spec)
        args.append(norm_g.reshape(1, n_dim))
        out_specs.append(row_spec)
        out_shapes.append(_sds((m_dim, n_dim), BF16))
    if rms_bwd is not None:
        assert tn == n_dim and not out_slabs and norm_g is None
        x_in, gain, dres = rms_bwd
        rms_at = len(args)
        in_specs += [row_spec, vec_spec, row_spec]
        args += [x_in, gain.reshape(1, n_dim), dres]
        out_specs, out_shapes = [row_spec, vec_spec], [_sds((m_dim, n_dim), F32), _sds((1, n_dim), F32)]
    aliases = {}
    if out_buf is not None:
        in_specs.append(pl.BlockSpec(memory_space=pl.ANY))
        args.append(out_buf)
        aliases = {len(args) - 1: 0}
    for dep in after:
        in_specs.append(pl.BlockSpec(memory_space=pl.ANY))
        args.append(dep)
    has_res = res is not None
    n_in = len(args)

    def body(*refs):
        a_ref, b_ref = refs[0], refs[1]
        res_ref = refs[2] if has_res else None
        o_ref = refs[n_in]
        acc_ref = refs[-1] if nk > 1 else None

        def product(rows):
            if slab_group:
                cs = b.shape[2]
                terms = [lax.dot_general(a_ref[rows, s * cs:(s + 1) * cs].astype(BF16), b_ref[s].astype(BF16), dims,
                                         preferred_element_type=F32) for s in range(slab_group)]
                return functools.reduce(lambda p, q: p + q, terms)
            lhs = a_ref[rows, :] if mode != "tn" else a_ref[:, rows]
            return lax.dot_general(lhs.astype(BF16), b_ref[...].astype(BF16), dims, preferred_element_type=F32)

        def finish(v, rows):
            if has_res:
                v = v + res_ref[rows, :]
            if rms_at is not None:
                xv, gv = refs[rms_at][rows, :], refs[rms_at + 1][...]
                r = lax.rsqrt(jnp.mean(xv * xv, axis=-1, keepdims=True) + EPS)
                dyg = v * gv
                dot = jnp.mean(dyg * xv, axis=-1, keepdims=True)
                o_ref[rows, :] = refs[rms_at + 2][rows, :] + r * dyg - xv * (r * r * r) * dot
                return jnp.sum(v * xv * r, axis=0, keepdims=True)
            o_ref[rows, :] = v.astype(o_ref.dtype)
            if norm_at is not None:
                r = lax.rsqrt(jnp.mean(v * v, axis=-1, keepdims=True) + EPS)
                refs[n_in + 1][rows, :] = (v * r * refs[norm_at][...]).astype(BF16)
            return None

        def add_gain_grad(gpart):
            dg_ref = refs[n_in + 1]
            first = pl.program_id(0) == 0

            @pl.when(first)
            def _():
                dg_ref[...] = gpart

            @pl.when(jnp.logical_not(first))
            def _():
                dg_ref[...] += gpart

        whole = slice(0, tm)
        if nk == 1:
            gpart = finish(product(whole), whole)
            if rms_at is not None:
                add_gain_grad(gpart)
        else:
            part = product(whole)
            kk = pl.program_id(2)

            @pl.when(kk == 0)
            def _():
                acc_ref[...] = part

            @pl.when(kk > 0)
            def _():
                acc_ref[...] += part

            @pl.when(kk == nk - 1)
            def _():
                gpart = finish(acc_ref[...], whole)
                if rms_at is not None:
                    add_gain_grad(gpart)

    outs = pl.pallas_call(
        body, grid=(m_dim // tm, n_dim // tn, nk), in_specs=in_specs, out_specs=out_specs, out_shape=out_shapes,
        scratch_shapes=[pltpu.VMEM((tm, tn), F32)] if nk > 1 else [],
        input_output_aliases=aliases, name=name,
        compiler_params=_cp(*(("arbitrary",) * 3 if rms_bwd is not None else ("parallel", "parallel", "arbitrary"))),
    )(*args)
    return outs[0] if len(outs) == 1 else tuple(outs)


def _rms_fwd(x, g, name):
    t, d = x.shape
    tm = _pick(t, ROW_TILE_PREFS)

    def body(x_ref, g_ref, o_ref):
        xv = x_ref[...]
        r = lax.rsqrt(jnp.mean(xv * xv, axis=-1, keepdims=True) + EPS)
        o_ref[...] = (xv * r * g_ref[...]).astype(o_ref.dtype)

    return pl.pallas_call(
        body, grid=(t // tm,),
        in_specs=[pl.BlockSpec((tm, d), lambda i: (i, 0)), pl.BlockSpec((1, d), lambda i: (0, 0))],
        out_specs=pl.BlockSpec((tm, d), lambda i: (i, 0)), out_shape=_sds((t, d), BF16),
        name=name, compiler_params=_cp("parallel"),
    )(x, g.reshape(1, d))


def _final_loss(x, g, target):
    t, d = x.shape
    tm = _pick(t, ROW_TILE_PREFS)

    def body(x_ref, g_ref, t_ref, dx_ref, dg_ref, loss_ref):
        i = pl.program_id(0)
        xv = x_ref[...]
        gv = g_ref[...]
        r = lax.rsqrt(jnp.mean(xv * xv, axis=-1, keepdims=True) + EPS)
        err = xv * r * gv - t_ref[...]
        lpart = (0.5 / d) * jnp.sum(jnp.sum(err * err, axis=1, keepdims=True), axis=0, keepdims=True)
        dy = err * (1.0 / d)
        dyg = dy * gv
        dot = jnp.mean(dyg * xv, axis=-1, keepdims=True)
        dx_ref[...] = r * dyg - xv * (r * r * r) * dot
        part = jnp.sum(dy * xv * r, axis=0, keepdims=True)

        @pl.when(i == 0)
        def _():
            dg_ref[...] = part
            loss_ref[...] = lpart

        @pl.when(i > 0)
        def _():
            dg_ref[...] += part
            loss_ref[...] += lpart

    row = pl.BlockSpec((tm, d), lambda i: (i, 0))
    vec = pl.BlockSpec((1, d), lambda i: (0, 0))
    return pl.pallas_call(
        body, grid=(t // tm,), in_specs=[row, vec, row], out_specs=[row, vec, pl.BlockSpec((1, 1), lambda i: (0, 0))],
        out_shape=[_sds((t, d), F32), _sds((1, d), F32), _sds((1, 1), F32)], name="final_loss",
        compiler_params=_cp("arbitrary"),
    )(x, g.reshape(1, d), target)


def _shift_down(ext, k, halo):
    return pltpu.roll(ext, k, 0)[halo:]


def _shift_up(ext, k, tm):
    return pltpu.roll(ext, ext.shape[0] - k, 0)[:tm]


def _ffn_tiles(t, f, max_rows=256):
    return _pick(t, tuple(r for r in (512, 256, 128, 64, 32, 16, 8) if r <= max_rows)), _pick(f, (1408, 256, 128))


def _ffn_act_fwd(hf_g, hf_v, conv_w, conv_b):
    t, f = hf_g.shape
    tm, cn = _ffn_tiles(t, f, max_rows=512)
    nf = f // cn

    def body(g_ref, v_ref, wg_ref, wv_ref, bg_ref, bv_ref, o_ref, gc_ref, vc_ref, ext_ref, hg_ref, hv_ref):
        i = pl.program_id(1)

        @pl.when(i == 0)
        def _():
            hg_ref[...] = jnp.zeros_like(hg_ref)
            hv_ref[...] = jnp.zeros_like(hv_ref)

        def conv(x_ref, halo_ref, w_ref, b_ref):
            ext_ref[0:8, :] = halo_ref[...]
            ext_ref[8:, :] = x_ref[...]
            halo_ref[...] = x_ref[tm - 8:tm, :]
            ext = ext_ref[...]
            w = w_ref[...]
            return b_ref[...] + w[2:3, :] * ext[8:] + w[1:2, :] * _shift_down(ext, 1, 8) + w[0:1, :] * _shift_down(ext, 2, 8)

        gc = conv(g_ref, hg_ref, wg_ref, bg_ref)
        vc = conv(v_ref, hv_ref, wv_ref, bv_ref)
        o_ref[...] = (_gelu(gc) * vc).astype(o_ref.dtype)
        gc_ref[...] = gc.astype(gc_ref.dtype)
        vc_ref[...] = vc.astype(vc_ref.dtype)

    blk = pl.BlockSpec((tm, cn), lambda j, i: (i, j))
    return pl.pallas_call(
        body, grid=(nf, t // tm),
        in_specs=[blk, blk, pl.BlockSpec((3, cn), lambda j, i: (0, j)), pl.BlockSpec((3, cn), lambda j, i: (0, j + nf)),
                  pl.BlockSpec((1, cn), lambda j, i: (0, j)), pl.BlockSpec((1, cn), lambda j, i: (0, j + nf))],
        out_specs=[blk, blk, blk], out_shape=[_sds((t, f), BF16)] * 3,
        scratch_shapes=[pltpu.VMEM((tm + 8, cn), F32), pltpu.VMEM((8, cn), F32), pltpu.VMEM((8, cn), F32)],
        name="ffn_act_fwd", compiler_params=_cp("parallel", "arbitrary"),
    )(hf_g, hf_v, conv_w, conv_w, conv_b, conv_b)


def _ffn_bwd(dx2, w_down, gc, vc, hf_g, hf_v, conv_w):
    t, f = hf_g.shape
    d = dx2.shape[1]
    tm, cn = _ffn_tiles(t, f)
    nf, nt = f // cn, t // tm

    def body(dx_ref, wd_ref, gc_ref, vc_ref, g_ref, v_ref, wg_ref, wv_ref, dg_ref, dv_ref, sg_ref, sv_ref,
             ext_ref, cg_ref, cv_ref):
        @pl.when(pl.program_id(1) == 0)
        def _():
            for ref in (cg_ref, cv_ref, sg_ref, sv_ref):
                ref[...] = jnp.zeros_like(ref)

        da = _dot_nt(dx_ref[...].astype(BF16), wd_ref[...])
        gel, dgel = _gelu_and_grad(gc_ref[...].astype(F32))
        dgc = da * vc_ref[...].astype(F32) * dgel
        dvc = da * gel

        def back(dc, carry_ref, w, x, out_ref, sums_ref):
            ext_ref[0:tm, :] = dc
            ext_ref[tm:tm + 8, :] = carry_ref[...]
            carry_ref[...] = dc[0:8, :]
            ext = ext_ref[...]
            up1, up2 = _shift_up(ext, 1, tm), _shift_up(ext, 2, tm)
            out_ref[...] = (w[2:3, :] * dc + w[1:2, :] * up1 + w[0:1, :] * up2).astype(out_ref.dtype)
            sums_ref[0:1, :] += jnp.sum(up2 * x, axis=0, keepdims=True)
            sums_ref[1:2, :] += jnp.sum(up1 * x, axis=0, keepdims=True)
            sums_ref[2:3, :] += jnp.sum(dc * x, axis=0, keepdims=True)
            sums_ref[3:4, :] += jnp.sum(dc, axis=0, keepdims=True)

        back(dgc, cg_ref, wg_ref[...], g_ref[...], dg_ref, sg_ref)
        back(dvc, cv_ref, wv_ref[...], v_ref[...], dv_ref, sv_ref)

    blk = pl.BlockSpec((tm, cn), lambda j, i: (nt - 1 - i, j))
    sums = pl.BlockSpec((8, cn), lambda j, i: (0, j))
    return pl.pallas_call(
        body, grid=(nf, nt),
        in_specs=[pl.BlockSpec((tm, d), lambda j, i: (nt - 1 - i, 0)), pl.BlockSpec((cn, d), lambda j, i: (j, 0)),
                  blk, blk, blk, blk, pl.BlockSpec((3, cn), lambda j, i: (0, j)), pl.BlockSpec((3, cn), lambda j, i: (0, j + nf))],
        out_specs=[blk, blk, sums, sums],
        out_shape=[_sds((t, f), BF16), _sds((t, f), BF16), _sds((8, f), F32), _sds((8, f), F32)],
        scratch_shapes=[pltpu.VMEM((tm + 8, cn), F32), pltpu.VMEM((8, cn), F32), pltpu.VMEM((8, cn), F32)],
        name="ffn_bwd", compiler_params=_cp("parallel", "arbitrary"),
    )(dx2, w_down, gc, vc, hf_g, hf_v, conv_w, conv_w)


def _ple_fwd(x2, pe, pre, next_g=None):
    t, d = x2.shape
    tm = _pick(t, ROW_TILE_PREFS)

    def body(x_ref, pe_ref, pre_ref, *rest):
        x3 = x_ref[...] + pe_ref[...] * _sigmoid(pre_ref[...])
        if next_g is None:
            rest[0][...] = x3
        else:
            g_ref, o_ref, h_ref = rest
            o_ref[...] = x3
            r = lax.rsqrt(jnp.mean(x3 * x3, axis=-1, keepdims=True) + EPS)
            h_ref[...] = (x3 * r * g_ref[...]).astype(h_ref.dtype)

    row = pl.BlockSpec((tm, d), lambda i: (i, 0))
    if next_g is None:
        return pl.pallas_call(body, grid=(t // tm,), in_specs=[row, row, row], out_specs=row,
                              out_shape=_sds((t, d), F32), name="ple_fwd", compiler_params=_cp("parallel"))(x2, pe, pre), None
    return pl.pallas_call(body, grid=(t // tm,), in_specs=[row, row, row, pl.BlockSpec((1, d), lambda i: (0, 0))],
                          out_specs=[row, row], out_shape=[_sds((t, d), F32), _sds((t, d), BF16)], name="ple_norm_fwd",
                          compiler_params=_cp("parallel"))(x2, pe, pre, next_g.reshape(1, d))


def _ple_bwd(dx3, pe, pre, after=None):
    t, d = dx3.shape
    tm = _pick(t, ROW_TILE_PREFS)

    def body(dx_ref, pe_ref, pre_ref, *rest):
        dpe_ref, dpre_ref = rest[-2:]
        gate = _sigmoid(pre_ref[...])
        dx = dx_ref[...]
        dpe_ref[...] = (dx * gate).astype(dpe_ref.dtype)
        dpre_ref[...] = (dx * pe_ref[...] * gate * (1.0 - gate)).astype(dpre_ref.dtype)

    row = pl.BlockSpec((tm, d), lambda i: (i, 0))
    extra = [] if after is None else [after]
    return pl.pallas_call(body, grid=(t // tm,), in_specs=[row, row, row] + [pl.BlockSpec(memory_space=pl.ANY)] * len(extra),
                          out_specs=[row, row], out_shape=[_sds((t, d), BF16), _sds((t, d), BF16)], name="ple_bwd",
                          compiler_params=_cp("parallel"))(dx3, pe, pre, *extra)


def _mix_tm(t):
    return _pick(t, (512, 256, 128))


def _zblk(tm, col, rev_nt=None):
    if rev_nt is None:
        return pl.BlockSpec((tm, W_GRP), lambda i: (i, col))
    return pl.BlockSpec((tm, W_GRP), lambda i: (rev_nt - 1 - i, col))


def _full(shape):
    nd = len(shape)
    return pl.BlockSpec(tuple(shape), lambda i: (0,) * nd)


def _gmlp_sv(wm_ref, vnc, bs, hm):
    sv = bs
    for h in range(N_HEADS):
        sv = sv + jnp.where(hm[h], _dot(wm_ref[h], vnc), 0.0)
    return sv


def _layernorm(v, g, b):
    mu = jnp.mean(v, axis=-1, keepdims=True)
    vc = v - mu
    rs = lax.rsqrt(jnp.mean(vc * vc, axis=-1, keepdims=True) + EPS)
    xhat = vc * rs
    return xhat, rs, xhat * g + b


def _mix_a_fwd(z, d_mix, ln_g, ln_b, wm, bs_t):
    t = z.shape[0]
    tm = _mix_tm(t)

    def body(u_ref, v_ref, g_ref, b_ref, wm_ref, bs_ref, o_ref):
        hm = _head_masks()
        ug = _gelu(u_ref[...])
        _, _, vn = _layernorm(_gelu(v_ref[...]), g_ref[...], b_ref[...])
        vnb = vn.astype(BF16)
        for n in range(tm // GMLP_CHUNK):
            sl = slice(n * GMLP_CHUNK, (n + 1) * GMLP_CHUNK)
            o_ref[sl, :] = ug[sl] * _gmlp_sv(wm_ref, vnb[sl], bs_ref[...], hm)

    return pl.pallas_call(
        body, grid=(t // tm,),
        in_specs=[_zblk(tm, 0), _zblk(tm, 1), _full((1, W_GRP)), _full((1, W_GRP)), _full(wm.shape), _full(bs_t.shape)],
        out_specs=_zblk(tm, 0), out_shape=_sds((t, d_mix), F32), name="mix_a_fwd", compiler_params=_cp("parallel"),
    )(z, z, ln_g, ln_b, wm, bs_t)


def _mix_a_bwd(z, dmix, ln_g, ln_b, wm, wm_t, bs_t):
    t, zc = z.shape
    tm = _mix_tm(t)

    def body(u_ref, v_ref, dy_ref, g_ref, b_ref, wm_ref, wmt_ref, bs_ref, dz_ref, dg_ref, db_ref, dws_ref, dbs_ref):
        i = pl.program_id(0)

        @pl.when(i == 0)
        def _():
            dg_ref[...] = jnp.zeros_like(dg_ref)
            db_ref[...] = jnp.zeros_like(db_ref)
            dws_ref[...] = jnp.zeros_like(dws_ref)
            dbs_ref[...] = jnp.zeros_like(dbs_ref)

        hm = _head_masks()
        ug, dug = _gelu_and_grad(u_ref[...])
        vg, dvg = _gelu_and_grad(v_ref[...])
        gv = g_ref[...]
        xhat, rs, vn = _layernorm(vg, gv, b_ref[...])
        vnb = vn.astype(BF16)
        dy = dy_ref[...]
        for n in range(tm // GMLP_CHUNK):
            sl = slice(n * GMLP_CHUNK, (n + 1) * GMLP_CHUNK)
            vnc = vnb[sl]
            sv = _gmlp_sv(wm_ref, vnc, bs_ref[...], hm)
            dsv = dy[sl] * ug[sl]
            dz_ref[sl, 0:W_GRP] = dy[sl] * sv * dug[sl]
            dbs_ref[...] += dsv
            dsvb = dsv.astype(BF16)
            dvn = jnp.zeros((GMLP_CHUNK, W_GRP), F32)
            for h in range(N_HEADS):
                dws_ref[h] += _dot_nt(jnp.where(hm[h], dsv, 0.0).astype(BF16), vnc)
                dvn = dvn + jnp.where(hm[h], _dot(wmt_ref[h], dsvb), 0.0)
            xh = xhat[sl]
            dg_ref[...] += jnp.sum(dvn * xh, axis=0, keepdims=True)
            db_ref[...] += jnp.sum(dvn, axis=0, keepdims=True)
            dxh = dvn * gv
            dvg_c = rs[sl] * (dxh - jnp.mean(dxh, axis=-1, keepdims=True) - xh * jnp.mean(dxh * xh, axis=-1, keepdims=True))
            dz_ref[sl, W_GRP:2 * W_GRP] = dvg_c * dvg[sl]

    return pl.pallas_call(
        body, grid=(t // tm,),
        in_specs=[_zblk(tm, 0), _zblk(tm, 1), _zblk(tm, 0), _full((1, W_GRP)), _full((1, W_GRP)), _full(wm.shape),
                  _full(wm_t.shape), _full(bs_t.shape)],
        out_specs=[pl.BlockSpec((tm, 2 * W_GRP), lambda i: (i, 0)), _full((1, W_GRP)), _full((1, W_GRP)),
                   _full(wm.shape), _full(bs_t.shape)],
        out_shape=[_sds((t, zc), F32), _sds((1, W_GRP), F32), _sds((1, W_GRP), F32), _sds(wm.shape, F32),
                   _sds(bs_t.shape, F32)],
        name="mix_a_bwd", compiler_params=_cp("arbitrary"),
    )(z, z, dmix, ln_g, ln_b, wm, wm_t, bs_t)


def _softplus(x):
    return jnp.maximum(x, 0.0) + jnp.log(1.0 + jnp.exp(-jnp.abs(x)))


def _neg_expm1(x):
    series = -x * (1.0 + x * 0.5 * (1.0 + x * (1.0 / 3.0) * (1.0 + x * 0.25 * (1.0 + x * 0.2))))
    return jnp.where(x > -0.1, series, 1.0 - jnp.exp(x))


def _rglru_gates(ext_ref, x_ref, halo, cw, cb, wa_ref, wx_ref, ba, bx, lam):
    ext_ref[0:8, :] = halo
    ext_ref[8:, :] = x_ref[...]
    ext = ext_ref[...]
    x0, x1, x2, x3 = ext[8:], _shift_down(ext, 1, 8), _shift_down(ext, 2, 8), _shift_down(ext, 3, 8)
    xc = cb + cw[3:4, :] * x0 + cw[2:3, :] * x1 + cw[1:2, :] * x2 + cw[0:1, :] * x3
    xcb = xc.astype(BF16)
    r = _sigmoid(_dot(xcb, wa_ref[...]) + ba)
    ig = _sigmoid(_dot(xcb, wx_ref[...]) + bx)
    sp = _softplus(-lam)
    la = -RGLRU_C * r * sp
    a = jnp.exp(la)
    mult = jnp.sqrt(_neg_expm1(2.0 * la))
    return (x0, x1, x2, x3), xc, r, ig, sp, a, mult


def _mix_b_fwd(z, mix, conv_w, conv_b, wa, wx, ba, bx, lam):
    t = z.shape[0]
    tm = _mix_tm(t)

    def body(x_ref, gb_ref, cw_ref, cb_ref, wa_ref, wx_ref, ba_ref, bx_ref, lam_ref, mix_in, o_ref, hs_ref,
             ext_ref, a_ref, b_ref, xh_ref, hc_ref):
        i = pl.program_id(0)

        @pl.when(i == 0)
        def _():
            xh_ref[...] = jnp.zeros_like(xh_ref)
            hc_ref[...] = jnp.zeros_like(hc_ref)

        _, xc, _, ig, _, a, mult = _rglru_gates(ext_ref, x_ref, xh_ref[...], cw_ref[...], cb_ref[...], wa_ref, wx_ref,
                                                ba_ref[...], bx_ref[...], lam_ref[...])
        xh_ref[...] = x_ref[tm - 8:tm, :]
        a_ref[...] = a
        b_ref[...] = mult * (ig * xc)
        rid = lax.broadcasted_iota(jnp.int32, (8, W_GRP), 0)

        def group(gi, hprev):
            base = pl.multiple_of(gi * 8, 8)
            ca = a_ref[pl.ds(base, 8), :]
            cb = b_ref[pl.ds(base, 8), :]
            for k in (1, 2, 4):
                m = rid >= k
                cb = jnp.where(m, ca * pltpu.roll(cb, k, 0) + cb, cb)
                ca = jnp.where(m, ca * pltpu.roll(ca, k, 0), ca)
            hh = cb + ca * hprev
            hs_ref[pl.ds(base, 8), :] = hh
            return hh[7:8, :]

        hlast = lax.fori_loop(0, tm // 8, group, hc_ref[0:1, :])
        hc_ref[...] = jnp.broadcast_to(hlast, hc_ref.shape)
        o_ref[...] = hs_ref[...] * _gelu(gb_ref[...])

    sq = _full((W_GRP, W_GRP))
    vec = _full((1, W_GRP))
    return pl.pallas_call(
        body, grid=(t // tm,),
        in_specs=[_zblk(tm, 2), _zblk(tm, 3), _full((4, W_GRP)), vec, sq, sq, vec, vec, vec, pl.BlockSpec(memory_space=pl.ANY)],
        out_specs=[_zblk(tm, 1), pl.BlockSpec((tm, W_GRP), lambda i: (i, 0))],
        out_shape=[_sds(mix.shape, F32), _sds((t, W_GRP), F32)],
        scratch_shapes=[pltpu.VMEM((tm + 8, W_GRP), F32), pltpu.VMEM((tm, W_GRP), F32), pltpu.VMEM((tm, W_GRP), F32),
                        pltpu.VMEM((8, W_GRP), F32), pltpu.VMEM((8, W_GRP), F32)],
        input_output_aliases={9: 0}, name="mix_b_fwd", compiler_params=_cp("arbitrary"),
    )(z, z, conv_w, conv_b, wa, wx, ba, bx, lam, mix)


def _mix_b_bwd(z, dz, dmix, hs, conv_w, conv_b, wa, wx, wa_t, wx_t, ba, bx, lam):
    t = z.shape[0]
    tm = _mix_tm(t)
    nt = t // tm
    hb = tm // 8

    def body(x_ref, gb_ref, xhalo_ref, hs_ref, hhalo_ref, dy_ref, cw_ref, cb_ref, wa_ref, wx_ref, wat_ref, wxt_ref,
             ba_ref, bx_ref, lam_ref, dz_in, dz_ref, dcw_ref, dcb_ref, dwa_ref, dwx_ref, dba_ref, dbx_ref, dlam_ref,
             ext_ref, c_ref, d_ref, g_ref, an_ref, gn_ref, dxn_ref):
        i = pl.program_id(0)
        first_tile = i == nt - 1

        @pl.when(i == 0)
        def _():
            for ref in (dcw_ref, dcb_ref, dwa_ref, dwx_ref, dba_ref, dbx_ref, dlam_ref, an_ref, gn_ref, dxn_ref):
                ref[...] = jnp.zeros_like(ref)

        cw, lam = cw_ref[...], lam_ref[...]
        xhalo = jnp.where(first_tile, 0.0, xhalo_ref[...])
        (x0, x1, x2, x3), xc, r, ig, sp, a, mult = _rglru_gates(
            ext_ref, x_ref, xhalo, cw, cb_ref[...], wa_ref, wx_ref, ba_ref[...], bx_ref[...], lam)
        hs = hs_ref[...]
        dy = dy_ref[...]
        gel, dgel = _gelu_and_grad(gb_ref[...])
        dz_ref[:, W_GRP:2 * W_GRP] = dy * hs * dgel

        ext_ref[0:tm, :] = a
        ext_ref[tm:tm + 8, :] = an_ref[...]
        an_ref[...] = a[0:8, :]
        c_ref[...] = _shift_up(ext_ref[...], 1, tm)
        d_ref[...] = dy * gel
        rid = lax.broadcasted_iota(jnp.int32, (8, W_GRP), 0)

        def group(j, gnext):
            base = pl.multiple_of((tm // 8 - 1 - j) * 8, 8)
            cc = c_ref[pl.ds(base, 8), :]
            cd = d_ref[pl.ds(base, 8), :]
            for k in (1, 2, 4):
                m = rid < 8 - k
                cd = jnp.where(m, cc * pltpu.roll(cd, 8 - k, 0) + cd, cd)
                cc = jnp.where(m, cc * pltpu.roll(cc, 8 - k, 0), cc)
            gg = cd + cc * gnext
            g_ref[pl.ds(base, 8), :] = gg
            return gg[0:1, :]

        gfirst = lax.fori_loop(0, tm // 8, group, gn_ref[0:1, :])
        gn_ref[...] = jnp.broadcast_to(gfirst, gn_ref.shape)
        g = g_ref[...]

        ext_ref[0:8, :] = jnp.where(first_tile, 0.0, hhalo_ref[...])
        ext_ref[8:, :] = hs
        hprev = _shift_down(ext_ref[...], 1, 8)
        da = g * hprev
        dmult = g * (ig * xc)
        di = g * mult * xc
        dxc = g * mult * ig
        dla = da * a - dmult * a * a / mult
        dr = dla * (-RGLRU_C * sp)
        dlam_ref[...] += jnp.sum(dla * (-RGLRU_C * r), axis=0, keepdims=True) * (-_sigmoid(-lam))
        dpr = dr * r * (1.0 - r)
        dpi = di * ig * (1.0 - ig)
        dprb, dpib, xcb = dpr.astype(BF16), dpi.astype(BF16), xc.astype(BF16)
        dba_ref[...] += jnp.sum(dpr, axis=0, keepdims=True)
        dbx_ref[...] += jnp.sum(dpi, axis=0, keepdims=True)
        dwa_ref[...] += _dot_tn(xcb, dprb)
        dwx_ref[...] += _dot_tn(xcb, dpib)
        dxc = dxc + _dot(dprb, wat_ref[...]) + _dot(dpib, wxt_ref[...])
        dcb_ref[...] += jnp.sum(dxc, axis=0, keepdims=True)
        dcw_ref[3:4, :] += jnp.sum(dxc * x0, axis=0, keepdims=True)
        dcw_ref[2:3, :] += jnp.sum(dxc * x1, axis=0, keepdims=True)
        dcw_ref[1:2, :] += jnp.sum(dxc * x2, axis=0, keepdims=True)
        dcw_ref[0:1, :] += jnp.sum(dxc * x3, axis=0, keepdims=True)
        ext_ref[0:tm, :] = dxc
        ext_ref[tm:tm + 8, :] = dxn_ref[...]
        dxn_ref[...] = dxc[0:8, :]
        ext = ext_ref[...]
        dz_ref[:, 0:W_GRP] = (cw[3:4, :] * dxc + cw[2:3, :] * _shift_up(ext, 1, tm) + cw[1:2, :] * _shift_up(ext, 2, tm)
                              + cw[0:1, :] * _shift_up(ext, 3, tm))

    sq = _full((W_GRP, W_GRP))
    vec = _full((1, W_GRP))
    halo = lambda col: pl.BlockSpec((8, W_GRP), lambda i: (jnp.maximum((nt - 1 - i) * hb - 1, 0), col))
    rev = lambda col: _zblk(tm, col, nt)
    return pl.pallas_call(
        body, grid=(nt,),
        in_specs=[rev(2), rev(3), halo(2), rev(0), halo(0), rev(1), _full((4, W_GRP)), vec, sq, sq, sq, sq, vec, vec, vec,
                  pl.BlockSpec(memory_space=pl.ANY)],
        out_specs=[pl.BlockSpec((tm, 2 * W_GRP), lambda i: (nt - 1 - i, 1)), _full((4, W_GRP)), vec, sq, sq, vec, vec, vec],
        out_shape=[_sds(dz.shape, F32), _sds((4, W_GRP), F32), _sds((1, W_GRP), F32), _sds((W_GRP, W_GRP), F32),
                   _sds((W_GRP, W_GRP), F32), _sds((1, W_GRP), F32), _sds((1, W_GRP), F32), _sds((1, W_GRP), F32)],
        scratch_shapes=[pltpu.VMEM((tm + 8, W_GRP), F32), pltpu.VMEM((tm, W_GRP), F32), pltpu.VMEM((tm, W_GRP), F32),
                        pltpu.VMEM((tm, W_GRP), F32), pltpu.VMEM((8, W_GRP), F32), pltpu.VMEM((8, W_GRP), F32),
                        pltpu.VMEM((8, W_GRP), F32)],
        input_output_aliases={15: 0}, name="mix_b_bwd", compiler_params=_cp("arbitrary"),
    )(z, z, z, hs, hs, dmix, conv_w, conv_b, wa, wx, wa_t, wx_t, ba, bx, lam, dz)


def _tri(n, lower):
    r = lax.broadcasted_iota(jnp.int32, (n, n), 0)
    c = lax.broadcasted_iota(jnp.int32, (n, n), 1)
    return jnp.where((r >= c) if lower else (r <= c), 1.0, 0.0).astype(BF16)


def _causal_stack():
    r = lax.broadcasted_iota(jnp.int32, (N_HEADS * HGRN_CHUNK, HGRN_CHUNK), 0)
    c = lax.broadcasted_iota(jnp.int32, (N_HEADS * HGRN_CHUNK, HGRN_CHUNK), 1)
    m = None
    for h in range(N_HEADS):
        mh = (r >= h * HGRN_CHUNK) & (r < (h + 1) * HGRN_CHUNK) & (r - h * HGRN_CHUNK >= c)
        m = mh if m is None else (m | mh)
    return m


def _stack_heads(x, hm):
    return jnp.concatenate([jnp.where(hm[h], x, 0.0) for h in range(N_HEADS)], axis=0)


def _unstack_heads(xs, hm):
    out = jnp.where(hm[0], xs[0:HGRN_CHUNK], 0.0)
    for h in range(1, N_HEADS):
        out = out + jnp.where(hm[h], xs[h * HGRN_CHUNK:(h + 1) * HGRN_CHUNK], 0.0)
    return out


def _hgrn_chunk(qv, fv, lb, tril):
    sq = _sigmoid(qv)
    qq = qv * sq
    sg = _sigmoid(fv)
    fg = lb + (1.0 - lb) * sg
    kk = 1.0 - fg
    bb = _dot_f32_lhs_exact(tril, jnp.log(fg))
    b_last = bb[HGRN_CHUNK - 1:HGRN_CHUNK, :]
    b_mid = bb[HGRN_CHUNK // 2 - 1:HGRN_CHUNK // 2, :]
    eq = jnp.exp(jnp.minimum(bb - b_mid, EXP_CLAMP))
    ek = jnp.exp(jnp.minimum(b_mid - bb, EXP_CLAMP))
    eb = jnp.exp(bb)
    el = jnp.exp(b_last - bb)
    return sq, qq, sg, fg, kk, b_last, eq, ek, eb, el


def _seg_mean(x, avg):
    return _dot_f32_rhs_exact(x, avg)


def _mix_c_fwd(z, mix, lb, ng):
    t = z.shape[0]
    tm = _mix_tm(t)
    nch = tm // HGRN_CHUNK

    def body(q_ref, f_ref, i_ref, g_ref, lb_ref, ng_ref, mix_in, y_ref, o_ref, ss_ref, s_ref):
        @pl.when(pl.program_id(0) == 0)
        def _():
            s_ref[...] = jnp.zeros_like(s_ref)

        hm = _head_masks()
        bmask = _block_mask()
        causal = _causal_stack()
        tril = _tri(HGRN_CHUNK, True)
        avg = jnp.where(bmask, 1.0 / HEAD_DIM, 0.0).astype(BF16)
        lb, ng = lb_ref[...], ng_ref[...]

        def chunk(c, carry):
            rows = pl.ds(pl.multiple_of(c * HGRN_CHUNK, HGRN_CHUNK), HGRN_CHUNK)
            vv = i_ref[rows, :]
            gv = g_ref[rows, :]
            _, qq, _, _, kk, b_last, eq, ek, eb, el = _hgrn_chunk(q_ref[rows, :], f_ref[rows, :], lb, tril)
            vb = vv.astype(BF16)
            qs = _stack_heads(qq * eq, hm).astype(BF16)
            att = jnp.where(causal, _dot_nt(qs, (kk * ek).astype(BF16)), 0.0)
            o = _unstack_heads(_dot(att.astype(BF16), vb), hm)
            s0 = s_ref[...]
            ss_ref[c] = s0
            o = o + _dot_nt((qq * eb).astype(BF16), s0.astype(BF16))
            s_ref[...] = s0 * jnp.exp(b_last) + jnp.where(bmask, _dot_tn(vb, (kk * el).astype(BF16)), 0.0)
            o_ref[rows, :] = o
            rstd = lax.rsqrt(_seg_mean(o * o, avg) + EPS)
            y_ref[rows, :] = o * rstd * ng * (gv * _sigmoid(gv))
            return carry

        lax.fori_loop(0, nch, chunk, 0, unroll=HGRN_UNROLL)

    vec = _full((1, W_GRP))
    return pl.pallas_call(
        body, grid=(t // tm,),
        in_specs=[_zblk(tm, 4), _zblk(tm, 5), _zblk(tm, 6), _zblk(tm, 7), vec, vec, pl.BlockSpec(memory_space=pl.ANY)],
        out_specs=[_zblk(tm, 2), pl.BlockSpec((tm, W_GRP), lambda i: (i, 0)),
                   pl.BlockSpec((nch, W_GRP, W_GRP), lambda i: (i, 0, 0))],
        out_shape=[_sds(mix.shape, F32), _sds((t, W_GRP), F32), _sds((t // HGRN_CHUNK, W_GRP, W_GRP), F32)],
        scratch_shapes=[pltpu.VMEM((W_GRP, W_GRP), F32)],
        input_output_aliases={6: 0}, name="mix_c_fwd", compiler_params=_cp("arbitrary"),
    )(z, z, z, z, lb, ng, mix)


def _mix_c_bwd(z, dz, dmix, o_pre, states, lb, ng):
    t = z.shape[0]
    tm = _mix_tm(t)
    nt = t // tm
    nch = tm // HGRN_CHUNK

    def body(q_ref, f_ref, i_ref, g_ref, o_ref, ss_ref, dy_ref, lb_ref, ng_ref, dz_in, dz_ref, dlb_ref, dng_ref, ds_ref):
        @pl.when(pl.program_id(0) == 0)
        def _():
            ds_ref[...] = jnp.zeros_like(ds_ref)
            dlb_ref[...] = jnp.zeros_like(dlb_ref)
            dng_ref[...] = jnp.zeros_like(dng_ref)

        hm = _head_masks()
        bmask = _block_mask()
        causal = _causal_stack()
        tril = _tri(HGRN_CHUNK, True)
        triu = _tri(HGRN_CHUNK, False)
        avg = jnp.where(bmask, 1.0 / HEAD_DIM, 0.0).astype(BF16)
        lb, ng = lb_ref[...], ng_ref[...]
        last_row = lax.broadcasted_iota(jnp.int32, (HGRN_CHUNK, W_GRP), 0) == HGRN_CHUNK - 1

        def chunk(j, carry):
            c = nch - 1 - j
            rows = pl.ds(pl.multiple_of(c * HGRN_CHUNK, HGRN_CHUNK), HGRN_CHUNK)
            qv, gv, vv = q_ref[rows, :], g_ref[rows, :], i_ref[rows, :]
            sq, qq, sg, fg, kk, b_last, eq, ek, eb, el = _hgrn_chunk(qv, f_ref[rows, :], lb, tril)
            s0 = ss_ref[c]
            ds1 = ds_ref[...]
            o = o_ref[rows, :]
            dy = dy_ref[rows, :]
            rstd = lax.rsqrt(_seg_mean(o * o, avg) + EPS)
            oh = o * rstd
            sgg = _sigmoid(gv)
            dz_ref[rows, 3 * W_GRP:4 * W_GRP] = dy * oh * ng * (sgg * (1.0 + gv * (1.0 - sgg)))
            don = dy * gv * sgg
            dng_ref[...] += jnp.sum(don * oh, axis=0, keepdims=True)
            doh = don * ng
            do = rstd * (doh - oh * _seg_mean(doh * oh, avg))
            qt, kt, qh, kh = qq * eq, kk * ek, qq * eb, kk * el
            vb, dob = vv.astype(BF16), do.astype(BF16)
            ktb, khb = kt.astype(BF16), kh.astype(BF16)
            ds1b = ds1.astype(BF16)
            qs = _stack_heads(qt, hm).astype(BF16)
            dos = _stack_heads(do, hm).astype(BF16)
            att = jnp.where(causal, _dot_nt(qs, ktb), 0.0).astype(BF16)
            datt = jnp.where(causal, _dot_nt(dos, vb), 0.0).astype(BF16)
            dv = _dot_tn(att, dos) + _dot_nt(khb, ds1b)
            dqt = _unstack_heads(_dot(datt, ktb), hm)
            dkt = _dot_tn(datt, qs)
            dqh = _dot(dob, s0.astype(BF16))
            dkh = _dot(vb, ds1b)
            e_last = jnp.exp(b_last)
            ds_ref[...] = ds1 * e_last + jnp.where(bmask, _dot_tn(dob, qh.astype(BF16)), 0.0)
            dq = dqt * eq + dqh * eb
            dk = dkt * ek + dkh * el
            db = qt * dqt - kt * dkt + qh * dqh - kh * dkh
            db_last = jnp.sum(kh * dkh, axis=0, keepdims=True) + e_last * jnp.sum(ds1 * s0, axis=0, keepdims=True)
            db = db + jnp.where(last_row, db_last, 0.0)
            dlogf = _dot_f32_lhs_exact(triu, db)
            dfg = dlogf / fg - dk
            dz_ref[rows, W_GRP:2 * W_GRP] = dfg * (1.0 - lb) * sg * (1.0 - sg)
            dlb_ref[...] += jnp.sum(dfg * (1.0 - sg), axis=0, keepdims=True)
            dz_ref[rows, 0:W_GRP] = dq * (sq * (1.0 + qv * (1.0 - sq)))
            dz_ref[rows, 2 * W_GRP:3 * W_GRP] = dv
            return carry

        lax.fori_loop(0, nch, chunk, 0, unroll=HGRN_UNROLL)

    vec = _full((1, W_GRP))
    rev = lambda col: _zblk(tm, col, nt)
    return pl.pallas_call(
        body, grid=(nt,),
        in_specs=[rev(4), rev(5), rev(6), rev(7), rev(0), pl.BlockSpec((nch, W_GRP, W_GRP), lambda i: (nt - 1 - i, 0, 0)),
                  rev(2), vec, vec, pl.BlockSpec(memory_space=pl.ANY)],
        out_specs=[pl.BlockSpec((tm, 4 * W_GRP), lambda i: (nt - 1 - i, 1)), vec, vec],
        out_shape=[_sds(dz.shape, F32), _sds((1, W_GRP), F32), _sds((1, W_GRP), F32)],
        scratch_shapes=[pltpu.VMEM((W_GRP, W_GRP), F32)],
        input_output_aliases={9: 0}, name="mix_c_bwd", compiler_params=_cp("arbitrary"),
    )(z, z, z, z, o_pre, states, dmix, lb, ng, dz)


def _pool_select(hm, s2, s4, s8, s16):
    return jnp.where(hm[0], s2, jnp.where(hm[1], s4, jnp.where(hm[2], s8, s16)))


def _pool_counts(hm, row0, tm):
    pos = (row0 + 1 + lax.broadcasted_iota(jnp.int32, (tm, W_GRP), 0)).astype(F32)
    win = _pool_select(hm, 2.0, 4.0, 8.0, 16.0)
    return jnp.minimum(pos, win)


def _pooled(ext_ref, x, halo, hm, cnt):
    ext_ref[0:POOL_HALO, :] = halo
    ext_ref[POOL_HALO:, :] = x
    e = ext_ref[...]
    s2 = e + pltpu.roll(e, 1, 0)
    s4 = s2 + pltpu.roll(s2, 2, 0)
    s8 = s4 + pltpu.roll(s4, 4, 0)
    s16 = s8 + pltpu.roll(s8, 8, 0)
    return _pool_select(hm, s2, s4, s8, s16)[POOL_HALO:] / cnt - x


def _mix_d_fwd(z, mix, wd, scale):
    t = z.shape[0]
    tm = _mix_tm(t)

    def body(x_ref, wd_ref, sc_ref, mix_in, o_ref, ext_ref, halo_ref):
        i = pl.program_id(0)

        @pl.when(i == 0)
        def _():
            halo_ref[...] = jnp.zeros_like(halo_ref)

        hm = _head_masks()
        x = x_ref[...]
        pooled = _pooled(ext_ref, x, halo_ref[...], hm, _pool_counts(hm, i * tm, tm))
        halo_ref[...] = x_ref[tm - POOL_HALO:tm, :]
        o_ref[...] = _dot(pooled.astype(BF16), wd_ref[...]) * sc_ref[...]

    return pl.pallas_call(
        body, grid=(t // tm,),
        in_specs=[_zblk(tm, 8), _full((W_GRP, W_GRP)), _full((1, W_GRP)), pl.BlockSpec(memory_space=pl.ANY)],
        out_specs=_zblk(tm, 3), out_shape=_sds(mix.shape, F32),
        scratch_shapes=[pltpu.VMEM((tm + POOL_HALO, W_GRP), F32), pltpu.VMEM((POOL_HALO, W_GRP), F32)],
        input_output_aliases={3: 0}, name="mix_d_fwd", compiler_params=_cp("arbitrary"),
    )(z, wd, scale, mix)


def _mix_d_bwd(z, dz, dmix, wd, wd_t, scale):
    t = z.shape[0]
    tm = _mix_tm(t)
    nt = t // tm
    hb = tm // POOL_HALO

    def body(x_ref, xhalo_ref, dy_ref, wd_ref, wdt_ref, sc_ref, dz_in, dz_ref, dwd_ref, dsc_ref, ext_ref, en_ref):
        i = pl.program_id(0)
        ri = nt - 1 - i

        @pl.when(i == 0)
        def _():
            en_ref[...] = jnp.zeros_like(en_ref)
            dwd_ref[...] = jnp.zeros_like(dwd_ref)
            dsc_ref[...] = jnp.zeros_like(dsc_ref)

        hm = _head_masks()
        cnt = _pool_counts(hm, ri * tm, tm)
        x = x_ref[...]
        pooled = _pooled(ext_ref, x, jnp.where(ri == 0, 0.0, xhalo_ref[...]), hm, cnt)
        pb = pooled.astype(BF16)
        dy = dy_ref[...]
        dsc_ref[...] += jnp.sum(dy * _dot(pb, wd_ref[...]), axis=0, keepdims=True)
        dyw = (dy * sc_ref[...]).astype(BF16)
        dwd_ref[...] += _dot_tn(pb, dyw)
        dpool = _dot(dyw, wdt_ref[...])
        e = dpool / cnt
        ext_ref[0:tm, :] = e
        ext_ref[tm:, :] = en_ref[...]
        en_ref[...] = e[0:POOL_HALO, :]
        ee = ext_ref[...]
        n = tm + POOL_HALO
        r2 = ee + pltpu.roll(ee, n - 1, 0)
        r4 = r2 + pltpu.roll(r2, n - 2, 0)
        r8 = r4 + pltpu.roll(r4, n - 4, 0)
        r16 = r8 + pltpu.roll(r8, n - 8, 0)
        dz_ref[...] = _pool_select(hm, r2, r4, r8, r16)[:tm] - dpool

    sq = _full((W_GRP, W_GRP))
    vec = _full((1, W_GRP))
    return pl.pallas_call(
        body, grid=(nt,),
        in_specs=[_zblk(tm, 8, nt), pl.BlockSpec((POOL_HALO, W_GRP), lambda i: (jnp.maximum((nt - 1 - i) * hb - 1, 0), 8)),
                  _zblk(tm, 3, nt), sq, sq, vec, pl.BlockSpec(memory_space=pl.ANY)],
        out_specs=[_zblk(tm, 8, nt), sq, vec],
        out_shape=[_sds(dz.shape, F32), _sds((W_GRP, W_GRP), F32), _sds((1, W_GRP), F32)],
        scratch_shapes=[pltpu.VMEM((tm + POOL_HALO, W_GRP), F32), pltpu.VMEM((POOL_HALO, W_GRP), F32)],
        input_output_aliases={6: 0}, name="mix_d_bwd", compiler_params=_cp("arbitrary"),
    )(z, z, dmix, wd, wd_t, scale, dz)


def _as2d(a):
    if a.ndim == 1:
        return a.reshape(1, a.shape[0])
    return a.reshape(-1, a.shape[-1])


def _adamw(w, g, m, v, name):
    shape = w.shape
    w2, g2, m2, v2 = _as2d(w), _as2d(g), _as2d(m), _as2d(v)
    rows, cols = w2.shape
    tr = _pick(rows, (1024, 512, 256, 128, 64, 32, 16, 8))
    if tr * cols * 4 * 14 > VMEM_LIMIT_BYTES:
        tr = _pick(rows, (256, 128, 64, 32, 16, 8))

    def body(w_ref, g_ref, m_ref, v_ref, d_ref, nm_ref, nv_ref):
        gv = g_ref[...]
        mn = ADAM_B1 * m_ref[...] + (1.0 - ADAM_B1) * gv
        vn = ADAM_B2 * v_ref[...] + (1.0 - ADAM_B2) * (gv * gv)
        m_hat = mn / (1.0 - ADAM_B1 ** ADAM_STEP)
        v_hat = vn / (1.0 - ADAM_B2 ** ADAM_STEP)
        d_ref[...] = -ADAM_LR * (m_hat / (jnp.sqrt(v_hat) + ADAM_EPS) + ADAM_WD * w_ref[...])
        nm_ref[...] = mn
        nv_ref[...] = vn

    blk = pl.BlockSpec((tr, cols), lambda i: (i, 0))
    outs = pl.pallas_call(
        body, grid=(rows // tr,), in_specs=[blk] * 4, out_specs=[blk] * 3, out_shape=[_sds((rows, cols), F32)] * 3,
        name=name, compiler_params=_cp("parallel"),
    )(w2, g2, m2, v2)
    return tuple(o.reshape(shape) for o in outs)


def _adamw_layer(w, g, m, v, layer, bufs, name):
    nl, r, cs = w.shape
    tr = _pick(r, (256, 128, 64, 32, 16, 8))

    def body(w_ref, g_ref, m_ref, v_ref, *rest):
        go_ref, d_ref, nm_ref, nv_ref = rest[-4:]
        gv = g_ref[...]
        mn = ADAM_B1 * m_ref[...] + (1.0 - ADAM_B1) * gv
        vn = ADAM_B2 * v_ref[...] + (1.0 - ADAM_B2) * (gv * gv)
        m_hat = mn / (1.0 - ADAM_B1 ** ADAM_STEP)
        v_hat = vn / (1.0 - ADAM_B2 ** ADAM_STEP)
        go_ref[...] = gv
        d_ref[...] = -ADAM_LR * (m_hat / (jnp.sqrt(v_hat) + ADAM_EPS) + ADAM_WD * w_ref[...])
        nm_ref[...] = mn
        nv_ref[...] = vn

    lay = pl.BlockSpec((None, tr, cs), lambda i: (layer, i, 0))
    in_specs = [lay, pl.BlockSpec((tr, cs), lambda i: (i, 0)), lay, lay]
    args = [w, g, m, v]
    aliases = {}
    if bufs is not None:
        in_specs += [pl.BlockSpec(memory_space=pl.ANY)] * 4
        args += list(bufs)
        aliases = {4 + i: i for i in range(4)}
    return pl.pallas_call(
        body, grid=(r // tr,), in_specs=in_specs, out_specs=[lay] * 4, out_shape=[_sds((nl, r, cs), F32)] * 4,
        input_output_aliases=aliases, name=name, compiler_params=_cp("parallel"),
    )(*args)


def _slot_sum(own, slots, name):
    n_slots, rows, cols = slots.shape
    whole_fits = rows * cols * 4 * (n_slots + 2) * 2 <= VMEM_LIMIT_BYTES // 2
    tr = rows if whole_fits else _pick(rows, (512, 352, 256, 128, 64, 32, 16, 8))

    def body(*refs):
        s_ref, o_ref = refs[-2], refs[-1]
        acc = s_ref[0].astype(F32) if own is None else refs[0][...].astype(F32) + s_ref[0].astype(F32)
        for k in range(1, n_slots):
            acc = acc + s_ref[k].astype(F32)
        o_ref[...] = acc

    row = pl.BlockSpec((tr, cols), lambda i: (i, 0))
    return pl.pallas_call(
        body, grid=(rows // tr,),
        in_specs=([] if own is None else [row]) + [pl.BlockSpec((n_slots, tr, cols), lambda i: (0, i, 0))],
        out_specs=row, out_shape=_sds((rows, cols), F32), name=name, compiler_params=_cp("parallel"),
    )(*(() if own is None else (own,)), slots)


def _me():
    return lax.axis_index("x"), lax.axis_index("y"), lax.axis_index("c")


def _other_chips(x, y):
    return [(1 - x, y), (x, 1 - y), (1 - x, 1 - y)]


ANY_SPEC = pl.BlockSpec(memory_space=pl.ANY)
HBM_SPEC = pl.BlockSpec(memory_space=pltpu.HBM)
SEM_SPEC = pl.BlockSpec(memory_space=pltpu.SEMAPHORE)
SPLIT_COPY_PARAMS = pltpu.CompilerParams(has_side_effects=pltpu.SideEffectType.DATAFLOW_SIDE_EFFECTING)
N_CHIPS = 4


def _aligned(v, m):
    return v if isinstance(v, int) else pl.multiple_of(v, m)


def _in_hbm(arr):
    return pltpu.with_memory_space_constraint(arr, pltpu.HBM)


def _peer(x, y, c, k):
    fx, fy, fc = (k >> 2) & 1, (k >> 1) & 1, k & 1
    px = 1 - x if fx else x
    py = 1 - y if fy else y
    pc = 1 - c if fc else c
    return px, py, pc


def _gather_start(shards, after, name):
    n = len(shards)

    def body(*refs):
        src, land = refs[:n], refs[n:2 * n]
        send_sems, recv_sems = refs[2 * n + 1], refs[2 * n + 2]
        token = refs[-1]
        x, y, c = _me()
        for w in range(n):
            for chip in _other_chips(x, y):
                pltpu.make_async_remote_copy(
                    src_ref=src[w], dst_ref=land[w].at[2 * x + y], send_sem=send_sems.at[w], recv_sem=recv_sems.at[w],
                    device_id=(*chip, c), device_id_type=MESH_ID).start()
        token[...] = jnp.zeros_like(token)

    lands = [lax.empty((N_CHIPS,) + s.shape, s.dtype) for s in shards]
    thru = [pltpu.HBM(s.shape, s.dtype) for s in shards] + [pltpu.HBM(z.shape, z.dtype) for z in lands]
    outs = pl.pallas_call(
        body, name=name,
        out_shape=(pltpu.SemaphoreType.DMA((n,)), pltpu.SemaphoreType.DMA((n,)), *thru, _sds((8, 128), F32)),
        in_specs=[HBM_SPEC] * (2 * n) + [ANY_SPEC],
        out_specs=(SEM_SPEC, SEM_SPEC, *[HBM_SPEC] * (2 * n), pl.BlockSpec(memory_space=pltpu.VMEM)),
        input_output_aliases={i: 2 + i for i in range(2 * n)}, compiler_params=SPLIT_COPY_PARAMS,
    )(*[_in_hbm(s) for s in shards], *[_in_hbm(z) for z in lands], after)
    return (outs[0], outs[1], outs[2:2 + n], outs[2 + n:2 + 2 * n]), outs[-1]


def _gather_wait(send_sems, recv_sems, srcs, lands, after, name):
    n = len(srcs)

    def body(*refs):
        land = refs[n:2 * n]
        send_sems, recv_sems = refs[2 * n], refs[2 * n + 1]
        x, y, c = _me()
        for w in range(n):
            three = land[w].at[pl.ds(0, N_CHIPS - 1)]
            cp = pltpu.make_async_remote_copy(src_ref=three, dst_ref=three, send_sem=send_sems.at[w], recv_sem=recv_sems.at[w],
                                              device_id=(x, y, c), device_id_type=MESH_ID)
            cp.wait_send()
            cp.wait_recv()

    both = list(srcs) + list(lands)
    outs = pl.pallas_call(
        body, name=name, out_shape=tuple(pltpu.HBM(b.shape, b.dtype) for b in both),
        in_specs=[HBM_SPEC] * (2 * n) + [SEM_SPEC, SEM_SPEC, ANY_SPEC], out_specs=[HBM_SPEC] * (2 * n),
        input_output_aliases={i: i for i in range(2 * n)}, compiler_params=SPLIT_COPY_PARAMS,
    )(*both, send_sems, recv_sems, after)
    return outs[n:2 * n]


def _push_start(grads, small, name):
    n = len(grads)
    srcs = list(grads) + ([] if small is None else [small])
    ns = len(srcs)

    def body(*refs):
        src, slots = refs[:ns], refs[ns:2 * ns]
        send_sems, recv_sems = refs[2 * ns], refs[2 * ns + 1]
        token = refs[-1]
        x, y, c = _me()
        for w in range(ns):
            for k in range(1, N_DEV):
                px, py, pc = _peer(x, y, c, k)
                if w < n:
                    hr = src[w].shape[1] // 2
                    piece = src[w].at[2 * px + py, pl.ds(_aligned(pc * hr, 16), hr), :]
                    slot = slots[w].at[k - 1]
                else:
                    piece = src[w]
                    slot = slots[w].at[4 * x + 2 * y + c]
                pltpu.make_async_remote_copy(
                    src_ref=piece, dst_ref=slot, send_sem=send_sems.at[w], recv_sem=recv_sems.at[w],
                    device_id=(px, py, pc), device_id_type=MESH_ID).start()
        token[...] = jnp.zeros_like(token)

    slots = [lax.empty((N_DEV - 1, g.shape[1] // 2, g.shape[2]), g.dtype) for g in grads]
    if small is not None:
        slots.append(lax.empty((N_DEV,) + small.shape, small.dtype))
    both = srcs + slots
    outs = pl.pallas_call(
        body, name=name,
        out_shape=(pltpu.SemaphoreType.DMA((ns,)), pltpu.SemaphoreType.DMA((ns,)),
                   *[pltpu.HBM(b.shape, b.dtype) for b in both], _sds((8, 128), F32)),
        in_specs=[HBM_SPEC] * len(both),
        out_specs=(SEM_SPEC, SEM_SPEC, *[HBM_SPEC] * len(both), pl.BlockSpec(memory_space=pltpu.VMEM)),
        input_output_aliases={i: 2 + i for i in range(len(both))}, compiler_params=SPLIT_COPY_PARAMS,
    )(*[_in_hbm(b) for b in both])
    return (outs[0], outs[1], outs[2:2 + ns], outs[2 + ns:2 + 2 * ns]), outs[-1]


def _push_wait(send_sems, recv_sems, srcs, slots, after, name):
    n = len(srcs)

    def body(*refs):
        slot = refs[n:2 * n]
        send_sems, recv_sems = refs[2 * n], refs[2 * n + 1]
        x, y, c = _me()
        for w in range(n):
            seven = slot[w].at[pl.ds(0, N_DEV - 1)]
            cp = pltpu.make_async_remote_copy(src_ref=seven, dst_ref=seven, send_sem=send_sems.at[w],
                                              recv_sem=recv_sems.at[w], device_id=(x, y, c), device_id_type=MESH_ID)
            cp.wait_send()
            cp.wait_recv()

    both = list(srcs) + list(slots)
    outs = pl.pallas_call(
        body, name=name, out_shape=tuple(pltpu.HBM(b.shape, b.dtype) for b in both),
        in_specs=[HBM_SPEC] * (2 * n) + [SEM_SPEC, SEM_SPEC, ANY_SPEC], out_specs=[HBM_SPEC] * (2 * n),
        input_output_aliases={i: i for i in range(2 * n)}, compiler_params=SPLIT_COPY_PARAMS,
    )(*both, send_sems, recv_sems, after)
    return outs[:n], outs[n:]


SWAP_CHUNK_BYTES = 2 * 1024 * 1024


def _swap_chunk_rows(hr, cs):
    ch = hr
    while ch * cs * 4 > SWAP_CHUNK_BYTES and ch % 16 == 0:
        ch //= 2
    return ch


def _swap_halves(halves, name):
    n = len(halves)
    chunk = [_swap_chunk_rows(*h.shape) for h in halves]
    rounds = max(h.shape[0] // ch for h, ch in zip(halves, chunk))

    def body(*refs):
        src, dst, buf = refs[:n], refs[n:2 * n], refs[2 * n:3 * n]
        load_sems, put_sems, send_sems, recv_sems = refs[3 * n:]
        x, y, c = _me()
        sibling = (x, y, 1 - c)
        for j in range(rounds):
            live = [w for w in range(n) if j < src[w].shape[0] // chunk[w]]
            loads = [pltpu.make_async_copy(src[w].at[pl.ds(j * chunk[w], chunk[w])], buf[w], load_sems.at[w]) for w in live]
            for ld in loads:
                ld.start()
            moves = []
            for ld, w in zip(loads, live):
                ld.wait()
                rows = pl.ds(_aligned(c * src[w].shape[0] + j * chunk[w], 8), chunk[w])
                put = pltpu.make_async_copy(buf[w], dst[w].at[rows], put_sems.at[w])
                send = pltpu.make_async_remote_copy(src_ref=buf[w], dst_ref=dst[w].at[rows], send_sem=send_sems.at[w],
                                                    recv_sem=recv_sems.at[w], device_id=sibling, device_id_type=MESH_ID)
                put.start()
                send.start()
                moves.append((put, send))
            for put, send in moves:
                put.wait()
                send.wait_send()
        for w in range(n):
            hr = src[w].shape[0]
            got = dst[w].at[pl.ds(_aligned((1 - c) * hr, 8), hr)]
            pltpu.make_async_remote_copy(src_ref=got, dst_ref=got, send_sem=send_sems.at[w], recv_sem=recv_sems.at[w],
                                         device_id=sibling, device_id_type=MESH_ID).wait_recv()

    return pl.pallas_call(
        body, in_specs=[ANY_SPEC] * n, out_specs=[ANY_SPEC] * n,
        out_shape=[_sds((2 * h.shape[0], h.shape[1]), F32) for h in halves],
        scratch_shapes=[pltpu.VMEM((ch, h.shape[1]), F32) for h, ch in zip(halves, chunk)]
        + [pltpu.SemaphoreType.DMA((n,))] * 4,
        name=name,
    )(*halves)


BIG = ("w_in", "w_out", "w_up", "w_down", "w_pe", "w_pg")
ROW_SHARDED = ("w_out", "w_down", "w_pg")
SMALL = ("norm1_g", "a_ln_g", "a_ln_b", "a_ws", "a_bs", "b_conv_w", "b_conv_b", "b_wa", "b_ba", "b_wx", "b_bx", "b_lam",
         "c_lb", "c_norm_g", "d_w", "d_scale", "norm2_g", "ffn_conv_w", "ffn_conv_b", "norm3_g", "final_g")
SMALL_SHARDED = ("b_conv_w", "ffn_conv_w")
WEIGHTS = ("norm1_g", "w_in", "a_ln_g", "a_ln_b", "a_ws", "a_bs", "b_conv_w", "b_conv_b", "b_wa", "b_ba", "b_wx", "b_bx",
           "b_lam", "c_lb", "c_norm_g", "d_w", "d_scale", "w_out", "norm2_g", "w_up", "ffn_conv_w", "ffn_conv_b", "w_down",
           "norm3_g", "w_pe", "w_pg", "final_g")
ARGS = ("x", "p") + WEIGHTS + ("loss_target",) + tuple("m_" + n for n in WEIGHTS) + tuple("v_" + n for n in WEIGHTS)


def _block_diag(w):
    eye = jnp.eye(N_HEADS, dtype=w.dtype)
    return (eye[None, :, None, :, None] * w[:, :, :, None, :]).reshape(w.shape[0], W_GRP, W_GRP)


def _diag_blocks(m):
    m4 = m.reshape(N_HEADS, HEAD_DIM, N_HEADS, HEAD_DIM)
    return jnp.stack([m4[h, :, h, :] for h in range(N_HEADS)])


def _lower_bounds(c_lb):
    lbs = jnp.cumsum(jax.nn.softmax(c_lb, axis=0), axis=0)
    return lbs - lbs[0:1]


def kernel(x, p, norm1_g, w_in, a_ln_g, a_ln_b, a_ws, a_bs, b_conv_w, b_conv_b, b_wa, b_ba, b_wx, b_bx, b_lam, c_lb, c_norm_g, d_w, d_scale, w_out, norm2_g, w_up, ffn_conv_w, ffn_conv_b, w_down, norm3_g, w_pe, w_pg, final_g, loss_target, m_norm1_g, m_w_in, m_a_ln_g, m_a_ln_b, m_a_ws, m_a_bs, m_b_conv_w, m_b_conv_b, m_b_wa, m_b_ba, m_b_wx, m_b_bx, m_b_lam, m_c_lb, m_c_norm_g, m_d_w, m_d_scale, m_w_out, m_norm2_g, m_w_up, m_ffn_conv_w, m_ffn_conv_b, m_w_down, m_norm3_g, m_w_pe, m_w_pg, m_final_g, v_norm1_g, v_w_in, v_a_ln_g, v_a_ln_b, v_a_ws, v_a_bs, v_b_conv_w, v_b_conv_b, v_b_wa, v_b_ba, v_b_wx, v_b_bx, v_b_lam, v_c_lb, v_c_norm_g, v_d_w, v_d_scale, v_w_out, v_norm2_g, v_w_up, v_ffn_conv_w, v_ffn_conv_b, v_w_down, v_norm3_g, v_w_pe, v_w_pg, v_final_g):
    return _step((x, p, norm1_g, w_in, a_ln_g, a_ln_b, a_ws, a_bs, b_conv_w, b_conv_b, b_wa, b_ba, b_wx, b_bx, b_lam, c_lb, c_norm_g, d_w, d_scale, w_out, norm2_g, w_up, ffn_conv_w, ffn_conv_b, w_down, norm3_g, w_pe, w_pg, final_g, loss_target, m_norm1_g, m_w_in, m_a_ln_g, m_a_ln_b, m_a_ws, m_a_bs, m_b_conv_w, m_b_conv_b, m_b_wa, m_b_ba, m_b_wx, m_b_bx, m_b_lam, m_c_lb, m_c_norm_g, m_d_w, m_d_scale, m_w_out, m_norm2_g, m_w_up, m_ffn_conv_w, m_ffn_conv_b, m_w_down, m_norm3_g, m_w_pe, m_w_pg, m_final_g, v_norm1_g, v_w_in, v_a_ln_g, v_a_ln_b, v_a_ws, v_a_bs, v_b_conv_w, v_b_conv_b, v_b_wa, v_b_ba, v_b_wx, v_b_bx, v_b_lam, v_c_lb, v_c_norm_g, v_d_w, v_d_scale, v_w_out, v_norm2_g, v_w_up, v_ffn_conv_w, v_ffn_conv_b, v_w_down, v_norm3_g, v_w_pe, v_w_pg, v_final_g))


SMALL_PER_LAYER = tuple(n for n in SMALL if n != "final_g")
GATHERED = BIG + SMALL_SHARDED
GATHER_LAYER0 = (("a", ("w_in", "b_conv_w")), ("b", ("w_out", "w_up", "ffn_conv_w", "w_down", "w_pe", "w_pg")))
PUSH_EARLY = ("w_pe", "w_pg", "w_down", "w_up")
PUSH_MID = ("w_out",)
PUSH_LATE = ("w_in",)
SMALL_MID = tuple(n for n in SMALL_PER_LAYER if n != "norm1_g")


def _cols_to_slabs(m):
    r, c4 = m.shape
    return jnp.moveaxis(m.reshape(r, N_CHIPS, c4 // N_CHIPS), 1, 0)


def _slabs_to_cols(s):
    return jnp.moveaxis(s, 0, 1).reshape(s.shape[1], -1)


def _pack_small(parts):
    flat = jnp.concatenate([p.reshape(-1) for p in parts])
    return jnp.pad(flat, (0, (-flat.shape[0]) % 1024)).reshape(-1, 128)


def _step(args):
    a = dict(zip(ARGS, args, strict=True))
    x0 = a["x"][0]
    target = a["loss_target"][0]
    nl = a["norm1_g"].shape[0]
    t, d = x0.shape
    f = a["w_down"].shape[1] * N_CHIPS
    cx, cy, cc = _me()
    my_shard = 2 * cx + cy
    shards = {n: a[n].astype(BF16) for n in BIG}
    shards.update({n: a[n] for n in SMALL_SHARDED})

    def start_gather(l, names, after, tag):
        return _gather_start([shards[n][l] for n in names], after, f"gather_start_{l}{tag}")

    def finish_gather(l, names, handle, after, tag):
        send, recv, srcs, lands = handle
        lands = _gather_wait(send, recv, srcs, lands, after, f"gather_wait_{l}{tag}")
        w = {}
        for n, land in zip(names, lands):
            full = lax.dynamic_update_slice(land, shards[n][l][None], (my_shard, 0, 0))
            if n in ROW_SHARDED:
                w[n] = full.reshape(-1, full.shape[-1])
            elif n in ("w_up", "w_pe"):
                w[n] = full
            else:
                w[n] = _slabs_to_cols(full)
        return w

    lbs, lbs_vjp = jax.vjp(_lower_bounds, a["c_lb"])
    tril = jnp.tril(jnp.ones((GMLP_CHUNK, GMLP_CHUNK), F32))

    stacked_params = {"wm": (a["a_ws"] * tril).astype(BF16)}
    stacked_params["wm_t"] = jnp.swapaxes(stacked_params["wm"], 2, 3)
    stacked_params["bs_t"] = jnp.repeat(jnp.swapaxes(a["a_bs"], 1, 2), HEAD_DIM, axis=2)
    for nm in ("b_wa", "b_wx", "d_w"):
        bd = _block_diag(a[nm]).astype(BF16)
        stacked_params[nm], stacked_params[nm + "_t"] = bd, jnp.swapaxes(bd, 1, 2)
    for nm in ("a_ln_g", "a_ln_b", "b_conv_b", "b_ba", "b_bx", "b_lam", "d_scale"):
        stacked_params[nm] = a[nm].reshape(nl, 1, W_GRP)
    stacked_params["lb"] = lbs.reshape(nl, 1, W_GRP)
    stacked_params["ng"] = jnp.tile(a["c_norm_g"], (1, N_HEADS)).reshape(nl, 1, W_GRP)
    stacked_params["ffn_conv_b"] = a["ffn_conv_b"].reshape(nl, 1, 2 * f)

    def layer_params(l, w):
        q = {k: v[l] for k, v in stacked_params.items()}
        q.update(w)
        return q

    saved, weights, params = [], [], []
    first_groups = {tag: start_gather(0, names, x0, tag)[0] for tag, names in GATHER_LAYER0}
    xl = x0
    for l in range(nl):
        if l == 0:
            w = finish_gather(0, GATHER_LAYER0[0][1], first_groups["a"], xl, "a")
        else:
            w = finish_gather(l, GATHERED, next_handle, xl, "")
        s = {"x0": xl}
        s["h1"] = _rms_fwd(xl, a["norm1_g"][0], "rms1_fwd") if l == 0 else h_next
        token = None
        if 0 < l < nl - 1:
            next_handle, token = start_gather(l + 1, GATHERED, s["h1"], "")
        s["z"] = _mm(s["h1"], w["w_in"], "nn", out_dtype=F32, name="mm_z", after=token)
        q = layer_params(l, w)
        mix = _mix_a_fwd(s["z"], d, q["a_ln_g"], q["a_ln_b"], q["wm"], q["bs_t"])
        mix, s["hs"] = _mix_b_fwd(s["z"], mix, q["b_conv_w"], q["b_conv_b"], q["b_wa"], q["b_wx"], q["b_ba"], q["b_bx"],
                                  q["b_lam"])
        mix, s["o_pre"], s["states"] = _mix_c_fwd(s["z"], mix, q["lb"], q["ng"])
        s["mix"] = _mix_d_fwd(s["z"], mix, q["d_w"], q["d_scale"])
        def land(tag, after):
            if l == 0:
                w.update(finish_gather(0, dict(GATHER_LAYER0)[tag], first_groups[tag], after, tag))
                q.update(w)

        land("b", s["mix"])
        token = None
        if l == 0 and nl > 1:
            next_handle, token = start_gather(1, GATHERED, w["w_out"], "")
        s["x1"], s["h2"] = _mm(s["mix"], w["w_out"], "nn", res=xl, out_dtype=F32, name="mm_out",
                               norm_g=a["norm2_g"][l], tm_max=1024, after=token)
        s["hf_g"] = _mm(s["h2"], w["w_up"], "nn", b_slabs=True, n=f, out_dtype=F32, name="mm_up_g")
        s["hf_v"] = _mm(s["h2"], w["w_up"], "nn", b_slabs=True, n=f, b_noff=f, out_dtype=F32, name="mm_up_v")
        s["act"], s["gc"], s["vc"] = _ffn_act_fwd(s["hf_g"], s["hf_v"], q["ffn_conv_w"], q["ffn_conv_b"])
        s["x2"], s["h3"] = _mm(s["act"], w["w_down"], "nn", res=s["x1"], out_dtype=F32, name="mm_down",
                               norm_g=a["norm3_g"][l], tm_max=1024)
        s["pre"] = _mm(s["h3"], w["w_pg"], "nn", out_dtype=F32, name="mm_pg")
        s["pe"] = _mm(a["p"][l, 0], w["w_pe"], "nn", b_slabs=True, out_dtype=F32, name="mm_pe")
        xl, h_next = _ple_fwd(s["x2"], s["pe"], s["pre"], a["norm1_g"][l + 1] if l + 1 < nl else None)
        saved.append(s)
        weights.append(w)
        params.append(q)

    dx, g_final, loss = _final_loss(xl, a["final_g"], target)
    loss = lax.psum(loss[0, 0], ("x", "y", "c"))

    stacked = {n: None for n in BIG}
    small_sums = {}

    def finish_push(l, names, handle, tag, after):
        send, recv, srcs, slots = handle
        srcs, slots = _push_wait(send, recv, srcs, slots, after, f"push_wait_{l}{tag}")
        halves = []
        for n, g, sl in zip(names, srcs, slots):
            hr = g.shape[1] // 2
            own = lax.dynamic_slice(g, (my_shard, cc * hr, 0), (1, hr, g.shape[2]))[0]
            halves.append(_slot_sum(own, sl, "sum_" + n))
        for n, g in zip(names, _swap_halves(halves, "swap_halves_" + tag)):
            stacked[n] = _adamw_layer(a[n], g, a["m_" + n], a["v_" + n], l, stacked[n], "adamw_" + n)
        if len(srcs) > len(names):
            by_sender = lax.dynamic_update_slice(slots[-1], srcs[-1][None], (2 * my_shard + cc, 0, 0))
            small_sums[l, tag] = _slot_sum(None, by_sender, "sum_small_" + tag)
        return stacked[names[-1]][1]

    pending = []
    token = None
    for l in reversed(range(nl)):
        q, s, w = params[l], saved[l], weights[l]
        gs = {}
        dpe, dpre = _ple_bwd(dx, s["pe"], s["pre"], after=token)
        g_pe = _mm(a["p"][l, 0], dpe, "tn", out_dtype=BF16, name="mm_dwpe", out_slabs=N_CHIPS)
        g_pg = _mm(s["h3"], dpre, "tn", out_dtype=BF16, name="mm_dwpg")
        dx2, gs["norm3_g"] = _mm(dpre, w["w_pg"], "nt", out_dtype=F32, name="mm_dh3", tm_max=1024,
                                 rms_bwd=(s["x2"], a["norm3_g"][l], dx))
        g_down = _mm(s["act"], dx2, "tn", out_dtype=BF16, name="mm_dwdown")
        dhf_g, dhf_v, sums_g, sums_v = _ffn_bwd(dx2, w["w_down"], s["gc"], s["vc"], s["hf_g"], s["hf_v"], q["ffn_conv_w"])
        gs["ffn_conv_w"] = jnp.concatenate([sums_g[0:3], sums_v[0:3]], axis=1)
        gs["ffn_conv_b"] = jnp.concatenate([sums_g[3:4], sums_v[3:4]], axis=1)
        g_up = _mm(s["h2"], dhf_g, "tn", out_dtype=BF16, name="mm_dwup_g", out_slabs=N_CHIPS, out_n=2 * f)
        g_up = _mm(s["h2"], dhf_v, "tn", out_dtype=BF16, name="mm_dwup_v", out_slabs=N_CHIPS, out_n=2 * f, o_noff=f, out_buf=g_up)
        early = {"w_pe": g_pe, "w_up": g_up, "w_pg": g_pg.reshape(N_CHIPS, -1, g_pg.shape[-1]),
                 "w_down": g_down.reshape(N_CHIPS, -1, g_down.shape[-1])}
        early_handle, token = _push_start([early[n] for n in PUSH_EARLY], None, f"push_start_{l}a")
        dh2 = _mm(dhf_g, w["w_up"], "nt", b_slabs=True, out_dtype=F32, name="mm_dh2_g", after=token)
        dx1, gs["norm2_g"] = _mm(dhf_v, w["w_up"], "nt", b_slabs=True, b_koff=f, res=dh2, out_dtype=F32, name="mm_dh2_v",
                                 tm_max=1024, rms_bwd=(s["x1"], a["norm2_g"][l], dx2))
        g_out = _mm(s["mix"], dx1, "tn", out_dtype=BF16, name="mm_dwout")
        dmix = _mm(dx1, w["w_out"], "nt", out_dtype=F32, name="mm_dmix")
        dz, gs["a_ln_g"], gs["a_ln_b"], dws, dbs_t = _mix_a_bwd(s["z"], dmix, q["a_ln_g"], q["a_ln_b"], q["wm"], q["wm_t"],
                                                               q["bs_t"])
        gs["a_ws"] = dws * tril
        gs["a_bs"] = dbs_t.reshape(GMLP_CHUNK, N_HEADS, HEAD_DIM).sum(-1).T
        dz, gs["b_conv_w"], gs["b_conv_b"], dwa, dwx, gs["b_ba"], gs["b_bx"], gs["b_lam"] = _mix_b_bwd(
            s["z"], dz, dmix, s["hs"], q["b_conv_w"], q["b_conv_b"], q["b_wa"], q["b_wx"], q["b_wa_t"], q["b_wx_t"],
            q["b_ba"], q["b_bx"], q["b_lam"])
        gs["b_wa"], gs["b_wx"] = _diag_blocks(dwa), _diag_blocks(dwx)
        dz, gs["c_lb"], dng = _mix_c_bwd(s["z"], dz, dmix, s["o_pre"], s["states"], q["lb"], q["ng"])
        gs["c_norm_g"] = dng.reshape(N_HEADS, HEAD_DIM).sum(0)
        dz, dwd, gs["d_scale"] = _mix_d_bwd(s["z"], dz, dmix, q["d_w"], q["d_w_t"], q["d_scale"])
        gs["d_w"] = _diag_blocks(dwd)
        small = [gs[n] for n in SMALL_MID] + ([g_final] if l == nl - 1 else [])
        out_slabs = g_out.reshape(N_CHIPS, -1, g_out.shape[-1])
        if l == 0:
            mid_handle, token = _push_start([out_slabs], _pack_small(small), "push_start_0b")
        g_in = _mm(s["h1"], dz, "tn", out_dtype=BF16, name="mm_dwin")
        dx, gs["norm1_g"] = _mm(dz, w["w_in"], "nt", out_dtype=F32, name="mm_dh1", after=(token, g_in), tm_max=1024,
                                rms_bwd=(s["x0"], a["norm1_g"][l], dx1))

        if l == 0:
            late_names, late_slabs, late_small = PUSH_LATE, [_cols_to_slabs(g_in)], [gs["norm1_g"]]
        else:
            late_names, late_slabs, late_small = PUSH_MID + PUSH_LATE, [out_slabs, _cols_to_slabs(g_in)], [gs["norm1_g"]] + small
        late_handle, token = _push_start(late_slabs, _pack_small(late_small), f"push_start_{l}c")
        dep = token
        for push in pending:
            dep = finish_push(*push, dep)
        pending = [(l, PUSH_EARLY, early_handle, "a")] + ([(l, PUSH_MID, mid_handle, "b")] if l == 0 else [])
        pending.append((l, late_names, late_handle, "c"))
    for push in pending:
        dep = finish_push(*push, dep)
    grad_x = dx[None]

    def small_shape(n):
        return a[n].shape[1:-1] + (a[n].shape[-1] * N_CHIPS,) if n in SMALL_SHARDED else a[n].shape[1:]

    per_layer = {n: [] for n in SMALL_PER_LAYER}
    for l in range(nl):
        late_vec = small_sums[l, "c"].reshape(-1)
        per_layer["norm1_g"].append(late_vec[:d])
        vec, off = (small_sums[l, "b"].reshape(-1), 0) if l == 0 else (late_vec, d)
        for n in SMALL_MID:
            shape = small_shape(n)
            size = 1
            for dim in shape:
                size *= dim
            per_layer[n].append(vec[off:off + size].reshape(shape))
            off += size
        if l == nl - 1:
            grad_final = vec[off:off + d]
    grads = {n: jnp.stack(per_layer[n]) for n in SMALL_PER_LAYER}
    grads["c_lb"] = lbs_vjp(grads["c_lb"])[0]
    grads["final_g"] = grad_final
    for n in SMALL_SHARDED:
        cs = a[n].shape[-1]
        grads[n] = lax.dynamic_slice_in_dim(grads[n], my_shard * cs, cs, axis=2)

    outs = {}
    for n in WEIGHTS:
        if n in BIG:
            outs[n] = stacked[n]
        else:
            outs[n] = (grads[n],) + _adamw(a[n], grads[n], a["m_" + n], a["v_" + n], "adamw_" + n)
    return (loss, grad_x, *[outs[n][0] for n in WEIGHTS], *[outs[n][1] for n in WEIGHTS], *[outs[n][2] for n in WEIGHTS],
            *[outs[n][3] for n in WEIGHTS])
```

```python
import functools

import jax
import jax.numpy as jnp
from jax import lax
from jax.experimental import pallas as pl
from jax.experimental.pallas import tpu as pltpu

F32 = jnp.float32
BF16 = jnp.bfloat16
EPS = 1e-6
HEAD_DIM = 64
N_HEADS = 4
W_GRP = HEAD_DIM * N_HEADS
GMLP_CHUNK = 128
HGRN_CHUNK = 64
HGRN_UNROLL = 8
RGLRU_C = 8.0
POOL_HALO = 16
EXP_CLAMP = 80.0
ADAM_LR, ADAM_B1, ADAM_B2, ADAM_EPS, ADAM_WD, ADAM_STEP = 0.001, 0.9, 0.999, 1e-08, 0.01, 10
VMEM_LIMIT_BYTES = 60 * 1024 * 1024
TILE_PREFS = (1024, 1408, 768, 512, 256, 128)
MM_ROW_TILES = (2048, 1024, 512, 256, 128, 64, 32, 16, 8)
MM_VMEM_BUDGET = 49 * 1024 * 1024
ROW_TILE_PREFS = (512, 256, 128, 64, 32, 16, 8)
MESH_ID = pl.DeviceIdType.MESH
N_DEV = 8


def _pick(n, prefs=TILE_PREFS):
    for p in prefs:
        if n % p == 0:
            return p
    return n


def _cp(*sem):
    return pltpu.CompilerParams(dimension_semantics=sem if sem else None, vmem_limit_bytes=VMEM_LIMIT_BYTES)


def _sds(shape, dtype):
    return jax.ShapeDtypeStruct(tuple(shape), dtype)


_GELU_C = 0.7978845608028654
_GELU_A = 0.044715


def _gelu(x):
    hx = 0.5 * x
    return hx + hx * jnp.tanh(x * (_GELU_C + (_GELU_C * _GELU_A) * (x * x)))


def _gelu_and_grad(x):
    x2 = x * x
    t = jnp.tanh(x * (_GELU_C + (_GELU_C * _GELU_A) * x2))
    hx = 0.5 * x
    g = hx + hx * t
    dg = (0.5 + 0.5 * t) + (hx * (1.0 - t * t)) * (_GELU_C + (3.0 * _GELU_C * _GELU_A) * x2)
    return g, dg


def _sigmoid(x):
    return 1.0 / (1.0 + jnp.exp(-x))


def _dot(a, b):
    return jnp.dot(a, b, preferred_element_type=F32)


def _dot_nt(a, b):
    return lax.dot_general(a, b, (((1,), (1,)), ((), ())), preferred_element_type=F32)


def _dot_tn(a, b):
    return lax.dot_general(a, b, (((0,), (0,)), ((), ())), preferred_element_type=F32)


def _split3(x):
    hi = x.astype(BF16)
    r1 = x - hi.astype(F32)
    mid = r1.astype(BF16)
    lo = (r1 - mid.astype(F32)).astype(BF16)
    return hi, mid, lo


def _dot_f32_rhs_exact(x, m_bf16):
    hi, mid, lo = _split3(x)
    return _dot(hi, m_bf16) + _dot(mid, m_bf16) + _dot(lo, m_bf16)


def _dot_f32_lhs_exact(m_bf16, x):
    hi, mid, lo = _split3(x)
    return _dot(m_bf16, hi) + _dot(m_bf16, mid) + _dot(m_bf16, lo)


def _head_masks(width=W_GRP):
    lane = lax.broadcasted_iota(jnp.int32, (1, width), 1)
    return [(lane >= h * HEAD_DIM) & (lane < (h + 1) * HEAD_DIM) for h in range(N_HEADS)]


def _block_mask(n=W_GRP):
    r = lax.broadcasted_iota(jnp.int32, (n, n), 0)
    c = lax.broadcasted_iota(jnp.int32, (n, n), 1)
    m = None
    for h in range(N_HEADS):
        mh = (r >= h * HEAD_DIM) & (r < (h + 1) * HEAD_DIM) & (c >= h * HEAD_DIM) & (c < (h + 1) * HEAD_DIM)
        m = mh if m is None else (m | mh)
    return m


def _mm(a, b, mode, *, out_dtype, name, res=None, b_slabs=False, n=None, b_noff=0, b_koff=0,
        out_slabs=0, out_buf=None, out_n=None, o_noff=0, after=(), norm_g=None, rms_bwd=None, tm_max=None):
    after = () if after is None else (tuple(after) if isinstance(after, (tuple, list)) else (after,))
    if mode == "tn":
        k_dim, m_dim = a.shape
    else:
        m_dim, k_dim = a.shape
    if mode == "nt":
        n_dim = b.shape[-2]
    else:
        n_dim = n if n is not None else (b.shape[0] * b.shape[2] if b_slabs else b.shape[1])
    n_total = out_n if out_n is not None else n_dim
    tm, tn, tk = _pick(m_dim), _pick(n_dim), _pick(k_dim)
    if tm_max is not None:
        tm = _pick(m_dim, tuple(p for p in TILE_PREFS if p <= tm_max))
    if b_slabs and mode == "nt":
        tk = _pick(b.shape[2])
    elif b_slabs:
        tn = _pick(b.shape[2])
    elif out_slabs:
        tn = _pick(n_total // out_slabs)
    def vmem_bytes(rows, kk):
        blocks = rows * kk * a.dtype.itemsize + kk * tn * b.dtype.itemsize + rows * tn * jnp.dtype(out_dtype).itemsize
        blocks += rows * tn * 4 * ((res is not None) + 2 * (rms_bwd is not None) + (norm_g is not None))
        return 2 * blocks + rows * tn * 4 * (1 if kk == k_dim else 2)

    row_cap = tm_max if tm_max is not None else ((1024 if n_dim // tn > 1 else 512) if mode == "tn" else MM_ROW_TILES[0])
    k_options = (tk,) if tk == k_dim else (k_dim, tk)
    choice = next(((rows, kk) for kk in k_options for rows in MM_ROW_TILES
                   if rows <= row_cap and m_dim % rows == 0 and rows >= min(256, m_dim) and vmem_bytes(rows, kk) <= MM_VMEM_BUDGET),
                  None)
    if choice is not None:
        tm, tk = choice
    nk = k_dim // tk
    assert b_noff % tn == 0 and b_koff % tk == 0 and o_noff % tn == 0 and n_dim % tn == 0 and k_dim % tk == 0
    bn0, bk0, on0 = b_noff // tn, b_koff // tk, o_noff // tn
    dims = {"nn": (((1,), (0,)), ((), ())), "nt": (((1,), (1,)), ((), ())), "tn": (((0,), (0,)), ((), ()))}[mode]

    if mode == "tn":
        a_spec = pl.BlockSpec((tk, tm), lambda i, j, k: (k, i))
    else:
        a_spec = pl.BlockSpec((tm, tk), lambda i, j, k: (i, k))
    slab_group = 0
    if not b_slabs:
        if mode == "nt":
            b_spec = pl.BlockSpec((tn, tk), lambda i, j, k: (j + bn0, k + bk0))
        else:
            b_spec = pl.BlockSpec((tk, tn), lambda i, j, k: (k + bk0, j + bn0))
    elif mode == "nt" and tk > b.shape[2]:
        slab_group = tk // b.shape[2]
        b_spec = pl.BlockSpec((slab_group, tn, b.shape[2]), lambda i, j, k: (bk0, j, 0))
    elif mode == "nt":
        bper = b.shape[2] // tk
        b_spec = pl.BlockSpec((None, tn, tk), lambda i, j, k: ((k + bk0) // bper, j, (k + bk0) % bper))
    else:
        bper = b.shape[2] // tn
        b_spec = pl.BlockSpec((None, tk, tn), lambda i, j, k: ((j + bn0) // bper, k, (j + bn0) % bper))
    in_specs = [a_spec, b_spec]
    args = [a, b]
    if res is not None:
        in_specs.append(pl.BlockSpec((tm, tn), lambda i, j, k: (i, j)))
        args.append(res)
    if out_slabs:
        oper = n_total // out_slabs // tn
        out_shape = _sds((out_slabs, m_dim, n_total // out_slabs), out_dtype)
        out_spec = pl.BlockSpec((None, tm, tn), lambda i, j, k: ((j + on0) // oper, i, (j + on0) % oper))
    else:
        out_shape = _sds((m_dim, n_total), out_dtype)
        out_spec = pl.BlockSpec((tm, tn), lambda i, j, k: (i, j + on0))
    out_specs, out_shapes = [out_spec], [out_shape]
    row_spec = pl.BlockSpec((tm, tn), lambda i, j, k: (i, 0))
    vec_spec = pl.BlockSpec((1, tn), lambda i, j, k: (0, 0))
    norm_at = rms_at = None
    if norm_g is not None:
        assert tn == n_dim and not out_slabs
        norm_at = len(args)
        in_specs.append(vec_spec)
        args.append(norm_g.reshape(1, n_dim))
        out_specs.append(row_spec)
        out_shapes.append(_sds((m_dim, n_dim), BF16))
    if rms_bwd is not None:
        assert tn == n_dim and not out_slabs and norm_g is None
        x_in, gain, dres = rms_bwd
        rms_at = len(args)
        in_specs += [row_spec, vec_spec, row_spec]
        args += [x_in, gain.reshape(1, n_dim), dres]
        out_specs, out_shapes = [row_spec, vec_spec], [_sds((m_dim, n_dim), F32), _sds((1, n_dim), F32)]
    aliases = {}
    if out_buf is not None:
        in_specs.append(pl.BlockSpec(memory_space=pl.ANY))
        args.append(out_buf)
        aliases = {len(args) - 1: 0}
    for dep in after:
        in_specs.append(pl.BlockSpec(memory_space=pl.ANY))
        args.append(dep)
    has_res = res is not None
    n_in = len(args)

    def body(*refs):
        a_ref, b_ref = refs[0], refs[1]
        res_ref = refs[2] if has_res else None
        o_ref = refs[n_in]
        acc_ref = refs[-1] if nk > 1 else None

        def product(rows):
            if slab_group:
                cs = b.shape[2]
                terms = [lax.dot_general(a_ref[rows, s * cs:(s + 1) * cs].astype(BF16), b_ref[s].astype(BF16), dims,
                                         preferred_element_type=F32) for s in range(slab_group)]
                return functools.reduce(lambda p, q: p + q, terms)
            lhs = a_ref[rows, :] if mode != "tn" else a_ref[:, rows]
            return lax.dot_general(lhs.astype(BF16), b_ref[...].astype(BF16), dims, preferred_element_type=F32)

        def finish(v, rows):
            if has_res:
                v = v + res_ref[rows, :]
            if rms_at is not None:
                xv, gv = refs[rms_at][rows, :], refs[rms_at + 1][...]
                r = lax.rsqrt(jnp.mean(xv * xv, axis=-1, keepdims=True) + EPS)
                dyg = v * gv
                dot = jnp.mean(dyg * xv, axis=-1, keepdims=True)
                o_ref[rows, :] = refs[rms_at + 2][rows, :] + r * dyg - xv * (r * r * r) * dot
                return jnp.sum(v * xv * r, axis=0, keepdims=True)
            o_ref[rows, :] = v.astype(o_ref.dtype)
            if norm_at is not None:
                r = lax.rsqrt(jnp.mean(v * v, axis=-1, keepdims=True) + EPS)
                refs[n_in + 1][rows, :] = (v * r * refs[norm_at][...]).astype(BF16)
            return None

        def add_gain_grad(gpart):
            dg_ref = refs[n_in + 1]
            first = pl.program_id(0) == 0

            @pl.when(first)
            def _():
                dg_ref[...] = gpart

            @pl.when(jnp.logical_not(first))
            def _():
                dg_ref[...] += gpart

        whole = slice(0, tm)
        if nk == 1:
            gpart = finish(product(whole), whole)
            if rms_at is not None:
                add_gain_grad(gpart)
        else:
            part = product(whole)
            kk = pl.program_id(2)

            @pl.when(kk == 0)
            def _():
                acc_ref[...] = part

            @pl.when(kk > 0)
            def _():
                acc_ref[...] += part

            @pl.when(kk == nk - 1)
            def _():
                gpart = finish(acc_ref[...], whole)
                if rms_at is not None:
                    add_gain_grad(gpart)

    outs = pl.pallas_call(
        body, grid=(m_dim // tm, n_dim // tn, nk), in_specs=in_specs, out_specs=out_specs, out_shape=out_shapes,
        scratch_shapes=[pltpu.VMEM((tm, tn), F32)] if nk > 1 else [],
        input_output_aliases=aliases, name=name,
        compiler_params=_cp(*(("arbitrary",) * 3 if rms_bwd is not None else ("parallel", "parallel", "arbitrary"))),
    )(*args)
    return outs[0] if len(outs) == 1 else tuple(outs)


def _rms_fwd(x, g, name):
    t, d = x.shape
    tm = _pick(t, ROW_TILE_PREFS)

    def body(x_ref, g_ref, o_ref):
        xv = x_ref[...]
        r = lax.rsqrt(jnp.mean(xv * xv, axis=-1, keepdims=True) + EPS)
        o_ref[...] = (xv * r * g_ref[...]).astype(o_ref.dtype)

    return pl.pallas_call(
        body, grid=(t // tm,),
        in_specs=[pl.BlockSpec((tm, d), lambda i: (i, 0)), pl.BlockSpec((1, d), lambda i: (0, 0))],
        out_specs=pl.BlockSpec((tm, d), lambda i: (i, 0)), out_shape=_sds((t, d), BF16),
        name=name, compiler_params=_cp("parallel"),
    )(x, g.reshape(1, d))


def _final_loss(x, g, target):
    t, d = x.shape
    tm = _pick(t, ROW_TILE_PREFS)

    def body(x_ref, g_ref, t_ref, dx_ref, dg_ref, loss_ref):
        i = pl.program_id(0)
        xv = x_ref[...]
        gv = g_ref[...]
        r = lax.rsqrt(jnp.mean(xv * xv, axis=-1, keepdims=True) + EPS)
        err = xv * r * gv - t_ref[...]
        lpart = (0.5 / d) * jnp.sum(jnp.sum(err * err, axis=1, keepdims=True), axis=0, keepdims=True)
        dy = err * (1.0 / d)
        dyg = dy * gv
        dot = jnp.mean(dyg * xv, axis=-1, keepdims=True)
        dx_ref[...] = r * dyg - xv * (r * r * r) * dot
        part = jnp.sum(dy * xv * r, axis=0, keepdims=True)

        @pl.when(i == 0)
        def _():
            dg_ref[...] = part
            loss_ref[...] = lpart

        @pl.when(i > 0)
        def _():
            dg_ref[...] += part
            loss_ref[...] += lpart

    row = pl.BlockSpec((tm, d), lambda i: (i, 0))
    vec = pl.BlockSpec((1, d), lambda i: (0, 0))
    return pl.pallas_call(
        body, grid=(t // tm,), in_specs=[row, vec, row], out_specs=[row, vec, pl.BlockSpec((1, 1), lambda i: (0, 0))],
        out_shape=[_sds((t, d), F32), _sds((1, d), F32), _sds((1, 1), F32)], name="final_loss",
        compiler_params=_cp("arbitrary"),
    )(x, g.reshape(1, d), target)


def _shift_down(ext, k, halo):
    return pltpu.roll(ext, k, 0)[halo:]


def _shift_up(ext, k, tm):
    return pltpu.roll(ext, ext.shape[0] - k, 0)[:tm]


def _ffn_tiles(t, f, max_rows=256):
    return _pick(t, tuple(r for r in (512, 256, 128, 64, 32, 16, 8) if r <= max_rows)), _pick(f, (1408, 256, 128))


def _ffn_act_fwd(hf_g, hf_v, conv_w, conv_b):
    t, f = hf_g.shape
    tm, cn = _ffn_tiles(t, f, max_rows=512)
    nf = f // cn

    def body(g_ref, v_ref, wg_ref, wv_ref, bg_ref, bv_ref, o_ref, gc_ref, vc_ref, ext_ref, hg_ref, hv_ref):
        i = pl.program_id(1)

        @pl.when(i == 0)
        def _():
            hg_ref[...] = jnp.zeros_like(hg_ref)
            hv_ref[...] = jnp.zeros_like(hv_ref)

        def conv(x_ref, halo_ref, w_ref, b_ref):
            ext_ref[0:8, :] = halo_ref[...]
            ext_ref[8:, :] = x_ref[...]
            halo_ref[...] = x_ref[tm - 8:tm, :]
            ext = ext_ref[...]
            w = w_ref[...]
            return b_ref[...] + w[2:3, :] * ext[8:] + w[1:2, :] * _shift_down(ext, 1, 8) + w[0:1, :] * _shift_down(ext, 2, 8)

        gc = conv(g_ref, hg_ref, wg_ref, bg_ref)
        vc = conv(v_ref, hv_ref, wv_ref, bv_ref)
        o_ref[...] = (_gelu(gc) * vc).astype(o_ref.dtype)
        gc_ref[...] = gc.astype(gc_ref.dtype)
        vc_ref[...] = vc.astype(vc_ref.dtype)

    blk = pl.BlockSpec((tm, cn), lambda j, i: (i, j))
    return pl.pallas_call(
        body, grid=(nf, t // tm),
        in_specs=[blk, blk, pl.BlockSpec((3, cn), lambda j, i: (0, j)), pl.BlockSpec((3, cn), lambda j, i: (0, j + nf)),
                  pl.BlockSpec((1, cn), lambda j, i: (0, j)), pl.BlockSpec((1, cn), lambda j, i: (0, j + nf))],
        out_specs=[blk, blk, blk], out_shape=[_sds((t, f), BF16)] * 3,
        scratch_shapes=[pltpu.VMEM((tm + 8, cn), F32), pltpu.VMEM((8, cn), F32), pltpu.VMEM((8, cn), F32)],
        name="ffn_act_fwd", compiler_params=_cp("parallel", "arbitrary"),
    )(hf_g, hf_v, conv_w, conv_w, conv_b, conv_b)


def _ffn_bwd(dx2, w_down, gc, vc, hf_g, hf_v, conv_w):
    t, f = hf_g.shape
    d = dx2.shape[1]
    tm, cn = _ffn_tiles(t, f)
    nf, nt = f // cn, t // tm

    def body(dx_ref, wd_ref, gc_ref, vc_ref, g_ref, v_ref, wg_ref, wv_ref, dg_ref, dv_ref, sg_ref, sv_ref,
             ext_ref, cg_ref, cv_ref):
        @pl.when(pl.program_id(1) == 0)
        def _():
            for ref in (cg_ref, cv_ref, sg_ref, sv_ref):
                ref[...] = jnp.zeros_like(ref)

        da = _dot_nt(dx_ref[...].astype(BF16), wd_ref[...])
        gel, dgel = _gelu_and_grad(gc_ref[...].astype(F32))
        dgc = da * vc_ref[...].astype(F32) * dgel
        dvc = da * gel

        def back(dc, carry_ref, w, x, out_ref, sums_ref):
            ext_ref[0:tm, :] = dc
            ext_ref[tm:tm + 8, :] = carry_ref[...]
            carry_ref[...] = dc[0:8, :]
            ext = ext_ref[...]
            up1, up2 = _shift_up(ext, 1, tm), _shift_up(ext, 2, tm)
            out_ref[...] = (w[2:3, :] * dc + w[1:2, :] * up1 + w[0:1, :] * up2).astype(out_ref.dtype)
            sums_ref[0:1, :] += jnp.sum(up2 * x, axis=0, keepdims=True)
            sums_ref[1:2, :] += jnp.sum(up1 * x, axis=0, keepdims=True)
            sums_ref[2:3, :] += jnp.sum(dc * x, axis=0, keepdims=True)
            sums_ref[3:4, :] += jnp.sum(dc, axis=0, keepdims=True)

        back(dgc, cg_ref, wg_ref[...], g_ref[...], dg_ref, sg_ref)
        back(dvc, cv_ref, wv_ref[...], v_ref[...], dv_ref, sv_ref)

    blk = pl.BlockSpec((tm, cn), lambda j, i: (nt - 1 - i, j))
    sums = pl.BlockSpec((8, cn), lambda j, i: (0, j))
    return pl.pallas_call(
        body, grid=(nf, nt),
        in_specs=[pl.BlockSpec((tm, d), lambda j, i: (nt - 1 - i, 0)), pl.BlockSpec((cn, d), lambda j, i: (j, 0)),
                  blk, blk, blk, blk, pl.BlockSpec((3, cn), lambda j, i: (0, j)), pl.BlockSpec((3, cn), lambda j, i: (0, j + nf))],
        out_specs=[blk, blk, sums, sums],
        out_shape=[_sds((t, f), BF16), _sds((t, f), BF16), _sds((8, f), F32), _sds((8, f), F32)],
        scratch_shapes=[pltpu.VMEM((tm + 8, cn), F32), pltpu.VMEM((8, cn), F32), pltpu.VMEM((8, cn), F32)],
        name="ffn_bwd", compiler_params=_cp("parallel", "arbitrary"),
    )(dx2, w_down, gc, vc, hf_g, hf_v, conv_w, conv_w)


def _ple_fwd(x2, pe, pre, next_g=None):
    t, d = x2.shape
    tm = _pick(t, ROW_TILE_PREFS)

    def body(x_ref, pe_ref, pre_ref, *rest):
        x3 = x_ref[...] + pe_ref[...] * _sigmoid(pre_ref[...])
        if next_g is None:
            rest[0][...] = x3
        else:
            g_ref, o_ref, h_ref = rest
            o_ref[...] = x3
            r = lax.rsqrt(jnp.mean(x3 * x3, axis=-1, keepdims=True) + EPS)
            h_ref[...] = (x3 * r * g_ref[...]).astype(h_ref.dtype)

    row = pl.BlockSpec((tm, d), lambda i: (i, 0))
    if next_g is None:
        return pl.pallas_call(body, grid=(t // tm,), in_specs=[row, row, row], out_specs=row,
                              out_shape=_sds((t, d), F32), name="ple_fwd", compiler_params=_cp("parallel"))(x2, pe, pre), None
    return pl.pallas_call(body, grid=(t // tm,), in_specs=[row, row, row, pl.BlockSpec((1, d), lambda i: (0, 0))],
                          out_specs=[row, row], out_shape=[_sds((t, d), F32), _sds((t, d), BF16)], name="ple_norm_fwd",
                          compiler_params=_cp("parallel"))(x2, pe, pre, next_g.reshape(1, d))


def _ple_bwd(dx3, pe, pre, after=None):
    t, d = dx3.shape
    tm = _pick(t, ROW_TILE_PREFS)

    def body(dx_ref, pe_ref, pre_ref, *rest):
        dpe_ref, dpre_ref = rest[-2:]
        gate = _sigmoid(pre_ref[...])
        dx = dx_ref[...]
        dpe_ref[...] = (dx * gate).astype(dpe_ref.dtype)
        dpre_ref[...] = (dx * pe_ref[...] * gate * (1.0 - gate)).astype(dpre_ref.dtype)

    row = pl.BlockSpec((tm, d), lambda i: (i, 0))
    extra = [] if after is None else [after]
    return pl.pallas_call(body, grid=(t // tm,), in_specs=[row, row, row] + [pl.BlockSpec(memory_space=pl.ANY)] * len(extra),
                          out_specs=[row, row], out_shape=[_sds((t, d), BF16), _sds((t, d), BF16)], name="ple_bwd",
                          compiler_params=_cp("parallel"))(dx3, pe, pre, *extra)


def _mix_tm(t):
    return _pick(t, (512, 256, 128))


def _zblk(tm, col, rev_nt=None):
    if rev_nt is None:
        return pl.BlockSpec((tm, W_GRP), lambda i: (i, col))
    return pl.BlockSpec((tm, W_GRP), lambda i: (rev_nt - 1 - i, col))


def _full(shape):
    nd = len(shape)
    return pl.BlockSpec(tuple(shape), lambda i: (0,) * nd)


def _gmlp_sv(wm_ref, vnc, bs, hm):
    sv = bs
    for h in range(N_HEADS):
        sv = sv + jnp.where(hm[h], _dot(wm_ref[h], vnc), 0.0)
    return sv


def _layernorm(v, g, b):
    mu = jnp.mean(v, axis=-1, keepdims=True)
    vc = v - mu
    rs = lax.rsqrt(jnp.mean(vc * vc, axis=-1, keepdims=True) + EPS)
    xhat = vc * rs
    return xhat, rs, xhat * g + b


def _mix_a_fwd(z, d_mix, ln_g, ln_b, wm, bs_t):
    t = z.shape[0]
    tm = _mix_tm(t)

    def body(u_ref, v_ref, g_ref, b_ref, wm_ref, bs_ref, o_ref):
        hm = _head_masks()
        ug = _gelu(u_ref[...])
        _, _, vn = _layernorm(_gelu(v_ref[...]), g_ref[...], b_ref[...])
        vnb = vn.astype(BF16)
        for n in range(tm // GMLP_CHUNK):
            sl = slice(n * GMLP_CHUNK, (n + 1) * GMLP_CHUNK)
            o_ref[sl, :] = ug[sl] * _gmlp_sv(wm_ref, vnb[sl], bs_ref[...], hm)

    return pl.pallas_call(
        body, grid=(t // tm,),
        in_specs=[_zblk(tm, 0), _zblk(tm, 1), _full((1, W_GRP)), _full((1, W_GRP)), _full(wm.shape), _full(bs_t.shape)],
        out_specs=_zblk(tm, 0), out_shape=_sds((t, d_mix), F32), name="mix_a_fwd", compiler_params=_cp("parallel"),
    )(z, z, ln_g, ln_b, wm, bs_t)


def _mix_a_bwd(z, dmix, ln_g, ln_b, wm, wm_t, bs_t):
    t, zc = z.shape
    tm = _mix_tm(t)

    def body(u_ref, v_ref, dy_ref, g_ref, b_ref, wm_ref, wmt_ref, bs_ref, dz_ref, dg_ref, db_ref, dws_ref, dbs_ref):
        i = pl.program_id(0)

        @pl.when(i == 0)
        def _():
            dg_ref[...] = jnp.zeros_like(dg_ref)
            db_ref[...] = jnp.zeros_like(db_ref)
            dws_ref[...] = jnp.zeros_like(dws_ref)
            dbs_ref[...] = jnp.zeros_like(dbs_ref)

        hm = _head_masks()
        ug, dug = _gelu_and_grad(u_ref[...])
        vg, dvg = _gelu_and_grad(v_ref[...])
        gv = g_ref[...]
        xhat, rs, vn = _layernorm(vg, gv, b_ref[...])
        vnb = vn.astype(BF16)
        dy = dy_ref[...]
        for n in range(tm // GMLP_CHUNK):
            sl = slice(n * GMLP_CHUNK, (n + 1) * GMLP_CHUNK)
            vnc = vnb[sl]
            sv = _gmlp_sv(wm_ref, vnc, bs_ref[...], hm)
            dsv = dy[sl] * ug[sl]
            dz_ref[sl, 0:W_GRP] = (dy[sl] * sv * dug[sl]).astype(BF16)
            dbs_ref[...] += dsv
            dsvb = dsv.astype(BF16)
            dvn = jnp.zeros((GMLP_CHUNK, W_GRP), F32)
            for h in range(N_HEADS):
                dws_ref[h] += _dot_nt(jnp.where(hm[h], dsv, 0.0).astype(BF16), vnc)
                dvn = dvn + jnp.where(hm[h], _dot(wmt_ref[h], dsvb), 0.0)
            xh = xhat[sl]
            dg_ref[...] += jnp.sum(dvn * xh, axis=0, keepdims=True)
            db_ref[...] += jnp.sum(dvn, axis=0, keepdims=True)
            dxh = dvn * gv
            dvg_c = rs[sl] * (dxh - jnp.mean(dxh, axis=-1, keepdims=True) - xh * jnp.mean(dxh * xh, axis=-1, keepdims=True))
            dz_ref[sl, W_GRP:2 * W_GRP] = (dvg_c * dvg[sl]).astype(BF16)

    return pl.pallas_call(
        body, grid=(t // tm,),
        in_specs=[_zblk(tm, 0), _zblk(tm, 1), _zblk(tm, 0), _full((1, W_GRP)), _full((1, W_GRP)), _full(wm.shape),
                  _full(wm_t.shape), _full(bs_t.shape)],
        out_specs=[pl.BlockSpec((tm, 2 * W_GRP), lambda i: (i, 0)), _full((1, W_GRP)), _full((1, W_GRP)),
                   _full(wm.shape), _full(bs_t.shape)],
        out_shape=[_sds((t, zc), BF16), _sds((1, W_GRP), F32), _sds((1, W_GRP), F32), _sds(wm.shape, F32),
                   _sds(bs_t.shape, F32)],
        name="mix_a_bwd", compiler_params=_cp("arbitrary"),
    )(z, z, dmix, ln_g, ln_b, wm, wm_t, bs_t)


def _softplus(x):
    return jnp.maximum(x, 0.0) + jnp.log(1.0 + jnp.exp(-jnp.abs(x)))


def _neg_expm1(x):
    series = -x * (1.0 + x * 0.5 * (1.0 + x * (1.0 / 3.0) * (1.0 + x * 0.25 * (1.0 + x * 0.2))))
    return jnp.where(x > -0.1, series, 1.0 - jnp.exp(x))


def _rglru_gates(ext_ref, x_ref, halo, cw, cb, wa_ref, wx_ref, ba, bx, lam):
    ext_ref[0:8, :] = halo
    ext_ref[8:, :] = x_ref[...]
    ext = ext_ref[...]
    x0, x1, x2, x3 = ext[8:], _shift_down(ext, 1, 8), _shift_down(ext, 2, 8), _shift_down(ext, 3, 8)
    xc = cb + cw[3:4, :] * x0 + cw[2:3, :] * x1 + cw[1:2, :] * x2 + cw[0:1, :] * x3
    xcb = xc.astype(BF16)
    r = _sigmoid(_dot(xcb, wa_ref[...]) + ba)
    ig = _sigmoid(_dot(xcb, wx_ref[...]) + bx)
    sp = _softplus(-lam)
    la = -RGLRU_C * r * sp
    a = jnp.exp(la)
    mult = jnp.sqrt(_neg_expm1(2.0 * la))
    return (x0, x1, x2, x3), xc, r, ig, sp, a, mult


def _mix_b_fwd(z, mix, conv_w, conv_b, wa, wx, ba, bx, lam):
    t = z.shape[0]
    tm = _mix_tm(t)

    def body(x_ref, gb_ref, cw_ref, cb_ref, wa_ref, wx_ref, ba_ref, bx_ref, lam_ref, mix_in, o_ref, hs_ref,
             ext_ref, a_ref, b_ref, xh_ref, hc_ref):
        i = pl.program_id(0)

        @pl.when(i == 0)
        def _():
            xh_ref[...] = jnp.zeros_like(xh_ref)
            hc_ref[...] = jnp.zeros_like(hc_ref)

        _, xc, _, ig, _, a, mult = _rglru_gates(ext_ref, x_ref, xh_ref[...], cw_ref[...], cb_ref[...], wa_ref, wx_ref,
                                                ba_ref[...], bx_ref[...], lam_ref[...])
        xh_ref[...] = x_ref[tm - 8:tm, :]
        a_ref[...] = a
        b_ref[...] = mult * (ig * xc)
        rid = lax.broadcasted_iota(jnp.int32, (8, W_GRP), 0)

        def group(gi, hprev):
            base = pl.multiple_of(gi * 8, 8)
            ca = a_ref[pl.ds(base, 8), :]
            cb = b_ref[pl.ds(base, 8), :]
            for k in (1, 2, 4):
                m = rid >= k
                cb = jnp.where(m, ca * pltpu.roll(cb, k, 0) + cb, cb)
                ca = jnp.where(m, ca * pltpu.roll(ca, k, 0), ca)
            hh = cb + ca * hprev
            hs_ref[pl.ds(base, 8), :] = hh
            return hh[7:8, :]

        hlast = lax.fori_loop(0, tm // 8, group, hc_ref[0:1, :])
        hc_ref[...] = jnp.broadcast_to(hlast, hc_ref.shape)
        o_ref[...] = hs_ref[...] * _gelu(gb_ref[...])

    sq = _full((W_GRP, W_GRP))
    vec = _full((1, W_GRP))
    return pl.pallas_call(
        body, grid=(t // tm,),
        in_specs=[_zblk(tm, 2), _zblk(tm, 3), _full((4, W_GRP)), vec, sq, sq, vec, vec, vec, pl.BlockSpec(memory_space=pl.ANY)],
        out_specs=[_zblk(tm, 1), pl.BlockSpec((tm, W_GRP), lambda i: (i, 0))],
        out_shape=[_sds(mix.shape, F32), _sds((t, W_GRP), F32)],
        scratch_shapes=[pltpu.VMEM((tm + 8, W_GRP), F32), pltpu.VMEM((tm, W_GRP), F32), pltpu.VMEM((tm, W_GRP), F32),
                        pltpu.VMEM((8, W_GRP), F32), pltpu.VMEM((8, W_GRP), F32)],
        input_output_aliases={9: 0}, name="mix_b_fwd", compiler_params=_cp("arbitrary"),
    )(z, z, conv_w, conv_b, wa, wx, ba, bx, lam, mix)


def _mix_b_bwd(z, dz, dmix, hs, conv_w, conv_b, wa, wx, wa_t, wx_t, ba, bx, lam):
    t = z.shape[0]
    tm = _mix_tm(t)
    nt = t // tm
    hb = tm // 8

    def body(x_ref, gb_ref, xhalo_ref, hs_ref, hhalo_ref, dy_ref, cw_ref, cb_ref, wa_ref, wx_ref, wat_ref, wxt_ref,
             ba_ref, bx_ref, lam_ref, dz_in, dz_ref, dcw_ref, dcb_ref, dwa_ref, dwx_ref, dba_ref, dbx_ref, dlam_ref,
             ext_ref, c_ref, d_ref, g_ref, an_ref, gn_ref, dxn_ref):
        i = pl.program_id(0)
        first_tile = i == nt - 1

        @pl.when(i == 0)
        def _():
            for ref in (dcw_ref, dcb_ref, dwa_ref, dwx_ref, dba_ref, dbx_ref, dlam_ref, an_ref, gn_ref, dxn_ref):
                ref[...] = jnp.zeros_like(ref)

        cw, lam = cw_ref[...], lam_ref[...]
        xhalo = jnp.where(first_tile, 0.0, xhalo_ref[...])
        (x0, x1, x2, x3), xc, r, ig, sp, a, mult = _rglru_gates(
            ext_ref, x_ref, xhalo, cw, cb_ref[...], wa_ref, wx_ref, ba_ref[...], bx_ref[...], lam)
        hs = hs_ref[...]
        dy = dy_ref[...]
        gel, dgel = _gelu_and_grad(gb_ref[...])
        dz_ref[:, W_GRP:2 * W_GRP] = (dy * hs * dgel).astype(BF16)

        ext_ref[0:tm, :] = a
        ext_ref[tm:tm + 8, :] = an_ref[...]
        an_ref[...] = a[0:8, :]
        c_ref[...] = _shift_up(ext_ref[...], 1, tm)
        d_ref[...] = dy * gel
        rid = lax.broadcasted_iota(jnp.int32, (8, W_GRP), 0)

        def group(j, gnext):
            base = pl.multiple_of((tm // 8 - 1 - j) * 8, 8)
            cc = c_ref[pl.ds(base, 8), :]
            cd = d_ref[pl.ds(base, 8), :]
            for k in (1, 2, 4):
                m = rid < 8 - k
                cd = jnp.where(m, cc * pltpu.roll(cd, 8 - k, 0) + cd, cd)
                cc = jnp.where(m, cc * pltpu.roll(cc, 8 - k, 0), cc)
            gg = cd + cc * gnext
            g_ref[pl.ds(base, 8), :] = gg
            return gg[0:1, :]

        gfirst = lax.fori_loop(0, tm // 8, group, gn_ref[0:1, :])
        gn_ref[...] = jnp.broadcast_to(gfirst, gn_ref.shape)
        g = g_ref[...]

        ext_ref[0:8, :] = jnp.where(first_tile, 0.0, hhalo_ref[...])
        ext_ref[8:, :] = hs
        hprev = _shift_down(ext_ref[...], 1, 8)
        da = g * hprev
        dmult = g * (ig * xc)
        di = g * mult * xc
        dxc = g * mult * ig
        dla = da * a - dmult * a * a / mult
        dr = dla * (-RGLRU_C * sp)
        dlam_ref[...] += jnp.sum(dla * (-RGLRU_C * r), axis=0, keepdims=True) * (-_sigmoid(-lam))
        dpr = dr * r * (1.0 - r)
        dpi = di * ig * (1.0 - ig)
        dprb, dpib, xcb = dpr.astype(BF16), dpi.astype(BF16), xc.astype(BF16)
        dba_ref[...] += jnp.sum(dpr, axis=0, keepdims=True)
        dbx_ref[...] += jnp.sum(dpi, axis=0, keepdims=True)
        dwa_ref[...] += _dot_tn(xcb, dprb)
        dwx_ref[...] += _dot_tn(xcb, dpib)
        dxc = dxc + _dot(dprb, wat_ref[...]) + _dot(dpib, wxt_ref[...])
        dcb_ref[...] += jnp.sum(dxc, axis=0, keepdims=True)
        dcw_ref[3:4, :] += jnp.sum(dxc * x0, axis=0, keepdims=True)
        dcw_ref[2:3, :] += jnp.sum(dxc * x1, axis=0, keepdims=True)
        dcw_ref[1:2, :] += jnp.sum(dxc * x2, axis=0, keepdims=True)
        dcw_ref[0:1, :] += jnp.sum(dxc * x3, axis=0, keepdims=True)
        ext_ref[0:tm, :] = dxc
        ext_ref[tm:tm + 8, :] = dxn_ref[...]
        dxn_ref[...] = dxc[0:8, :]
        ext = ext_ref[...]
        dz_ref[:, 0:W_GRP] = (cw[3:4, :] * dxc + cw[2:3, :] * _shift_up(ext, 1, tm) + cw[1:2, :] * _shift_up(ext, 2, tm)
                              + cw[0:1, :] * _shift_up(ext, 3, tm)).astype(BF16)

    sq = _full((W_GRP, W_GRP))
    vec = _full((1, W_GRP))
    halo = lambda col: pl.BlockSpec((8, W_GRP), lambda i: (jnp.maximum((nt - 1 - i) * hb - 1, 0), col))
    rev = lambda col: _zblk(tm, col, nt)
    return pl.pallas_call(
        body, grid=(nt,),
        in_specs=[rev(2), rev(3), halo(2), rev(0), halo(0), rev(1), _full((4, W_GRP)), vec, sq, sq, sq, sq, vec, vec, vec,
                  pl.BlockSpec(memory_space=pl.ANY)],
        out_specs=[pl.BlockSpec((tm, 2 * W_GRP), lambda i: (nt - 1 - i, 1)), _full((4, W_GRP)), vec, sq, sq, vec, vec, vec],
        out_shape=[_sds(dz.shape, dz.dtype), _sds((4, W_GRP), F32), _sds((1, W_GRP), F32), _sds((W_GRP, W_GRP), F32),
                   _sds((W_GRP, W_GRP), F32), _sds((1, W_GRP), F32), _sds((1, W_GRP), F32), _sds((1, W_GRP), F32)],
        scratch_shapes=[pltpu.VMEM((tm + 8, W_GRP), F32), pltpu.VMEM((tm, W_GRP), F32), pltpu.VMEM((tm, W_GRP), F32),
                        pltpu.VMEM((tm, W_GRP), F32), pltpu.VMEM((8, W_GRP), F32), pltpu.VMEM((8, W_GRP), F32),
                        pltpu.VMEM((8, W_GRP), F32)],
        input_output_aliases={15: 0}, name="mix_b_bwd", compiler_params=_cp("arbitrary"),
    )(z, z, z, hs, hs, dmix, conv_w, conv_b, wa, wx, wa_t, wx_t, ba, bx, lam, dz)


def _tri(n, lower):
    r = lax.broadcasted_iota(jnp.int32, (n, n), 0)
    c = lax.broadcasted_iota(jnp.int32, (n, n), 1)
    return jnp.where((r >= c) if lower else (r <= c), 1.0, 0.0).astype(BF16)


def _causal_stack():
    r = lax.broadcasted_iota(jnp.int32, (N_HEADS * HGRN_CHUNK, HGRN_CHUNK), 0)
    c = lax.broadcasted_iota(jnp.int32, (N_HEADS * HGRN_CHUNK, HGRN_CHUNK), 1)
    m = None
    for h in range(N_HEADS):
        mh = (r >= h * HGRN_CHUNK) & (r < (h + 1) * HGRN_CHUNK) & (r - h * HGRN_CHUNK >= c)
        m = mh if m is None else (m | mh)
    return m


def _stack_heads(x, hm):
    return jnp.concatenate([jnp.where(hm[h], x, 0.0) for h in range(N_HEADS)], axis=0)


def _unstack_heads(xs, hm):
    out = jnp.where(hm[0], xs[0:HGRN_CHUNK], 0.0)
    for h in range(1, N_HEADS):
        out = out + jnp.where(hm[h], xs[h * HGRN_CHUNK:(h + 1) * HGRN_CHUNK], 0.0)
    return out


def _hgrn_chunk(qv, fv, lb, tril):
    sq = _sigmoid(qv)
    qq = qv * sq
    sg = _sigmoid(fv)
    fg = lb + (1.0 - lb) * sg
    kk = 1.0 - fg
    bb = _dot_f32_lhs_exact(tril, jnp.log(fg))
    b_last = bb[HGRN_CHUNK - 1:HGRN_CHUNK, :]
    b_mid = bb[HGRN_CHUNK // 2 - 1:HGRN_CHUNK // 2, :]
    eq = jnp.exp(jnp.minimum(bb - b_mid, EXP_CLAMP))
    ek = jnp.exp(jnp.minimum(b_mid - bb, EXP_CLAMP))
    eb = jnp.exp(bb)
    el = jnp.exp(b_last - bb)
    return sq, qq, sg, fg, kk, b_last, eq, ek, eb, el


def _seg_mean(x, avg):
    return _dot_f32_rhs_exact(x, avg)


def _mix_c_fwd(z, mix, lb, ng):
    t = z.shape[0]
    tm = _mix_tm(t)
    nch = tm // HGRN_CHUNK

    def body(q_ref, f_ref, i_ref, g_ref, lb_ref, ng_ref, mix_in, y_ref, o_ref, ss_ref, s_ref):
        @pl.when(pl.program_id(0) == 0)
        def _():
            s_ref[...] = jnp.zeros_like(s_ref)

        hm = _head_masks()
        bmask = _block_mask()
        causal = _causal_stack()
        tril = _tri(HGRN_CHUNK, True)
        avg = jnp.where(bmask, 1.0 / HEAD_DIM, 0.0).astype(BF16)
        lb, ng = lb_ref[...], ng_ref[...]

        def chunk(c, carry):
            rows = pl.ds(pl.multiple_of(c * HGRN_CHUNK, HGRN_CHUNK), HGRN_CHUNK)
            vv = i_ref[rows, :]
            gv = g_ref[rows, :]
            _, qq, _, _, kk, b_last, eq, ek, eb, el = _hgrn_chunk(q_ref[rows, :], f_ref[rows, :], lb, tril)
            vb = vv.astype(BF16)
            qs = _stack_heads(qq * eq, hm).astype(BF16)
            att = jnp.where(causal, _dot_nt(qs, (kk * ek).astype(BF16)), 0.0)
            o = _unstack_heads(_dot(att.astype(BF16), vb), hm)
            s0 = s_ref[...]
            ss_ref[c] = s0
            o = o + _dot_nt((qq * eb).astype(BF16), s0.astype(BF16))
            s_ref[...] = s0 * jnp.exp(b_last) + jnp.where(bmask, _dot_tn(vb, (kk * el).astype(BF16)), 0.0)
            o_ref[rows, :] = o
            rstd = lax.rsqrt(_seg_mean(o * o, avg) + EPS)
            y_ref[rows, :] = o * rstd * ng * (gv * _sigmoid(gv))
            return carry

        lax.fori_loop(0, nch, chunk, 0, unroll=HGRN_UNROLL)

    vec = _full((1, W_GRP))
    return pl.pallas_call(
        body, grid=(t // tm,),
        in_specs=[_zblk(tm, 4), _zblk(tm, 5), _zblk(tm, 6), _zblk(tm, 7), vec, vec, pl.BlockSpec(memory_space=pl.ANY)],
        out_specs=[_zblk(tm, 2), pl.BlockSpec((tm, W_GRP), lambda i: (i, 0)),
                   pl.BlockSpec((nch, W_GRP, W_GRP), lambda i: (i, 0, 0))],
        out_shape=[_sds(mix.shape, F32), _sds((t, W_GRP), F32), _sds((t // HGRN_CHUNK, W_GRP, W_GRP), F32)],
        scratch_shapes=[pltpu.VMEM((W_GRP, W_GRP), F32)],
        input_output_aliases={6: 0}, name="mix_c_fwd", compiler_params=_cp("arbitrary"),
    )(z, z, z, z, lb, ng, mix)


def _mix_c_bwd(z, dz, dmix, o_pre, states, lb, ng):
    t = z.shape[0]
    tm = _mix_tm(t)
    nt = t // tm
    nch = tm // HGRN_CHUNK

    def body(q_ref, f_ref, i_ref, g_ref, o_ref, ss_ref, dy_ref, lb_ref, ng_ref, dz_in, dz_ref, dlb_ref, dng_ref, ds_ref):
        @pl.when(pl.program_id(0) == 0)
        def _():
            ds_ref[...] = jnp.zeros_like(ds_ref)
            dlb_ref[...] = jnp.zeros_like(dlb_ref)
            dng_ref[...] = jnp.zeros_like(dng_ref)

        hm = _head_masks()
        bmask = _block_mask()
        causal = _causal_stack()
        tril = _tri(HGRN_CHUNK, True)
        triu = _tri(HGRN_CHUNK, False)
        avg = jnp.where(bmask, 1.0 / HEAD_DIM, 0.0).astype(BF16)
        lb, ng = lb_ref[...], ng_ref[...]
        last_row = lax.broadcasted_iota(jnp.int32, (HGRN_CHUNK, W_GRP), 0) == HGRN_CHUNK - 1

        def chunk(j, carry):
            c = nch - 1 - j
            rows = pl.ds(pl.multiple_of(c * HGRN_CHUNK, HGRN_CHUNK), HGRN_CHUNK)
            qv, gv, vv = q_ref[rows, :], g_ref[rows, :], i_ref[rows, :]
            sq, qq, sg, fg, kk, b_last, eq, ek, eb, el = _hgrn_chunk(qv, f_ref[rows, :], lb, tril)
            s0 = ss_ref[c]
            ds1 = ds_ref[...]
            o = o_ref[rows, :]
            dy = dy_ref[rows, :]
            rstd = lax.rsqrt(_seg_mean(o * o, avg) + EPS)
            oh = o * rstd
            sgg = _sigmoid(gv)
            dz_ref[rows, 3 * W_GRP:4 * W_GRP] = (dy * oh * ng * (sgg * (1.0 + gv * (1.0 - sgg)))).astype(BF16)
            don = dy * gv * sgg
            dng_ref[...] += jnp.sum(don * oh, axis=0, keepdims=True)
            doh = don * ng
            do = rstd * (doh - oh * _seg_mean(doh * oh, avg))
            qt, kt, qh, kh = qq * eq, kk * ek, qq * eb, kk * el
            vb, dob = vv.astype(BF16), do.astype(BF16)
            ktb, khb = kt.astype(BF16), kh.astype(BF16)
            ds1b = ds1.astype(BF16)
            qs = _stack_heads(qt, hm).astype(BF16)
            dos = _stack_heads(do, hm).astype(BF16)
            att = jnp.where(causal, _dot_nt(qs, ktb), 0.0).astype(BF16)
            datt = jnp.where(causal, _dot_nt(dos, vb), 0.0).astype(BF16)
            dv = _dot_tn(att, dos) + _dot_nt(khb, ds1b)
            dqt = _unstack_heads(_dot(datt, ktb), hm)
            dkt = _dot_tn(datt, qs)
            dqh = _dot(dob, s0.astype(BF16))
            dkh = _dot(vb, ds1b)
            e_last = jnp.exp(b_last)
            ds_ref[...] = ds1 * e_last + jnp.where(bmask, _dot_tn(dob, qh.astype(BF16)), 0.0)
            dq = dqt * eq + dqh * eb
            dk = dkt * ek + dkh * el
            db = qt * dqt - kt * dkt + qh * dqh - kh * dkh
            db_last = jnp.sum(kh * dkh, axis=0, keepdims=True) + e_last * jnp.sum(ds1 * s0, axis=0, keepdims=True)
            db = db + jnp.where(last_row, db_last, 0.0)
            dlogf = _dot_f32_lhs_exact(triu, db)
            dfg = dlogf / fg - dk
            dz_ref[rows, W_GRP:2 * W_GRP] = (dfg * (1.0 - lb) * sg * (1.0 - sg)).astype(BF16)
            dlb_ref[...] += jnp.sum(dfg * (1.0 - sg), axis=0, keepdims=True)
            dz_ref[rows, 0:W_GRP] = (dq * (sq * (1.0 + qv * (1.0 - sq)))).astype(BF16)
            dz_ref[rows, 2 * W_GRP:3 * W_GRP] = dv.astype(BF16)
            return carry

        lax.fori_loop(0, nch, chunk, 0, unroll=HGRN_UNROLL)

    vec = _full((1, W_GRP))
    rev = lambda col: _zblk(tm, col, nt)
    return pl.pallas_call(
        body, grid=(nt,),
        in_specs=[rev(4), rev(5), rev(6), rev(7), rev(0), pl.BlockSpec((nch, W_GRP, W_GRP), lambda i: (nt - 1 - i, 0, 0)),
                  rev(2), vec, vec, pl.BlockSpec(memory_space=pl.ANY)],
        out_specs=[pl.BlockSpec((tm, 4 * W_GRP), lambda i: (nt - 1 - i, 1)), vec, vec],
        out_shape=[_sds(dz.shape, dz.dtype), _sds((1, W_GRP), F32), _sds((1, W_GRP), F32)],
        scratch_shapes=[pltpu.VMEM((W_GRP, W_GRP), F32)],
        input_output_aliases={9: 0}, name="mix_c_bwd", compiler_params=_cp("arbitrary"),
    )(z, z, z, z, o_pre, states, dmix, lb, ng, dz)


def _pool_select(hm, s2, s4, s8, s16):
    return jnp.where(hm[0], s2, jnp.where(hm[1], s4, jnp.where(hm[2], s8, s16)))


def _pool_counts(hm, row0, tm):
    pos = (row0 + 1 + lax.broadcasted_iota(jnp.int32, (tm, W_GRP), 0)).astype(F32)
    win = _pool_select(hm, 2.0, 4.0, 8.0, 16.0)
    return jnp.minimum(pos, win)


def _pooled(ext_ref, x, halo, hm, cnt):
    ext_ref[0:POOL_HALO, :] = halo
    ext_ref[POOL_HALO:, :] = x
    e = ext_ref[...]
    s2 = e + pltpu.roll(e, 1, 0)
    s4 = s2 + pltpu.roll(s2, 2, 0)
    s8 = s4 + pltpu.roll(s4, 4, 0)
    s16 = s8 + pltpu.roll(s8, 8, 0)
    return _pool_select(hm, s2, s4, s8, s16)[POOL_HALO:] / cnt - x


def _mix_d_fwd(z, mix, wd, scale):
    t = z.shape[0]
    tm = _mix_tm(t)

    def body(x_ref, wd_ref, sc_ref, mix_in, o_ref, ext_ref, halo_ref):
        i = pl.program_id(0)

        @pl.when(i == 0)
        def _():
            halo_ref[...] = jnp.zeros_like(halo_ref)

        hm = _head_masks()
        x = x_ref[...]
        pooled = _pooled(ext_ref, x, halo_ref[...], hm, _pool_counts(hm, i * tm, tm))
        halo_ref[...] = x_ref[tm - POOL_HALO:tm, :]
        o_ref[...] = _dot(pooled.astype(BF16), wd_ref[...]) * sc_ref[...]

    return pl.pallas_call(
        body, grid=(t // tm,),
        in_specs=[_zblk(tm, 8), _full((W_GRP, W_GRP)), _full((1, W_GRP)), pl.BlockSpec(memory_space=pl.ANY)],
        out_specs=_zblk(tm, 3), out_shape=_sds(mix.shape, F32),
        scratch_shapes=[pltpu.VMEM((tm + POOL_HALO, W_GRP), F32), pltpu.VMEM((POOL_HALO, W_GRP), F32)],
        input_output_aliases={3: 0}, name="mix_d_fwd", compiler_params=_cp("arbitrary"),
    )(z, wd, scale, mix)


def _mix_d_bwd(z, dz, dmix, wd, wd_t, scale):
    t = z.shape[0]
    tm = _mix_tm(t)
    nt = t // tm
    hb = tm // POOL_HALO

    def body(x_ref, xhalo_ref, dy_ref, wd_ref, wdt_ref, sc_ref, dz_in, dz_ref, dwd_ref, dsc_ref, ext_ref, en_ref):
        i = pl.program_id(0)
        ri = nt - 1 - i

        @pl.when(i == 0)
        def _():
            en_ref[...] = jnp.zeros_like(en_ref)
            dwd_ref[...] = jnp.zeros_like(dwd_ref)
            dsc_ref[...] = jnp.zeros_like(dsc_ref)

        hm = _head_masks()
        cnt = _pool_counts(hm, ri * tm, tm)
        x = x_ref[...]
        pooled = _pooled(ext_ref, x, jnp.where(ri == 0, 0.0, xhalo_ref[...]), hm, cnt)
        pb = pooled.astype(BF16)
        dy = dy_ref[...]
        dsc_ref[...] += jnp.sum(dy * _dot(pb, wd_ref[...]), axis=0, keepdims=True)
        dyw = (dy * sc_ref[...]).astype(BF16)
        dwd_ref[...] += _dot_tn(pb, dyw)
        dpool = _dot(dyw, wdt_ref[...])
        e = dpool / cnt
        ext_ref[0:tm, :] = e
        ext_ref[tm:, :] = en_ref[...]
        en_ref[...] = e[0:POOL_HALO, :]
        ee = ext_ref[...]
        n = tm + POOL_HALO
        r2 = ee + pltpu.roll(ee, n - 1, 0)
        r4 = r2 + pltpu.roll(r2, n - 2, 0)
        r8 = r4 + pltpu.roll(r4, n - 4, 0)
        r16 = r8 + pltpu.roll(r8, n - 8, 0)
        dz_ref[...] = (_pool_select(hm, r2, r4, r8, r16)[:tm] - dpool).astype(BF16)

    sq = _full((W_GRP, W_GRP))
    vec = _full((1, W_GRP))
    return pl.pallas_call(
        body, grid=(nt,),
        in_specs=[_zblk(tm, 8, nt), pl.BlockSpec((POOL_HALO, W_GRP), lambda i: (jnp.maximum((nt - 1 - i) * hb - 1, 0), 8)),
                  _zblk(tm, 3, nt), sq, sq, vec, pl.BlockSpec(memory_space=pl.ANY)],
        out_specs=[_zblk(tm, 8, nt), sq, vec],
        out_shape=[_sds(dz.shape, dz.dtype), _sds((W_GRP, W_GRP), F32), _sds((1, W_GRP), F32)],
        scratch_shapes=[pltpu.VMEM((tm + POOL_HALO, W_GRP), F32), pltpu.VMEM((POOL_HALO, W_GRP), F32)],
        input_output_aliases={6: 0}, name="mix_d_bwd", compiler_params=_cp("arbitrary"),
    )(z, z, dmix, wd, wd_t, scale, dz)


def _as2d(a):
    if a.ndim == 1:
        return a.reshape(1, a.shape[0])
    return a.reshape(-1, a.shape[-1])


def _adamw(w, g, m, v, name):
    shape = w.shape
    w2, g2, m2, v2 = _as2d(w), _as2d(g), _as2d(m), _as2d(v)
    rows, cols = w2.shape
    tr = _pick(rows, (1024, 512, 256, 128, 64, 32, 16, 8))
    if tr * cols * 4 * 14 > VMEM_LIMIT_BYTES:
        tr = _pick(rows, (256, 128, 64, 32, 16, 8))

    def body(w_ref, g_ref, m_ref, v_ref, d_ref, nm_ref, nv_ref):
        gv = g_ref[...]
        mn = ADAM_B1 * m_ref[...] + (1.0 - ADAM_B1) * gv
        vn = ADAM_B2 * v_ref[...] + (1.0 - ADAM_B2) * (gv * gv)
        m_hat = mn / (1.0 - ADAM_B1 ** ADAM_STEP)
        v_hat = vn / (1.0 - ADAM_B2 ** ADAM_STEP)
        d_ref[...] = -ADAM_LR * (m_hat / (jnp.sqrt(v_hat) + ADAM_EPS) + ADAM_WD * w_ref[...])
        nm_ref[...] = mn
        nv_ref[...] = vn

    blk = pl.BlockSpec((tr, cols), lambda i: (i, 0))
    outs = pl.pallas_call(
        body, grid=(rows // tr,), in_specs=[blk] * 4, out_specs=[blk] * 3, out_shape=[_sds((rows, cols), F32)] * 3,
        name=name, compiler_params=_cp("parallel"),
    )(w2, g2, m2, v2)
    return tuple(o.reshape(shape) for o in outs)


def _adamw_layer(w, g, m, v, layer, bufs, name):
    nl, r, cs = w.shape
    tr = _pick(r, (256, 128, 64, 32, 16, 8))

    def body(w_ref, g_ref, m_ref, v_ref, *rest):
        go_ref, d_ref, nm_ref, nv_ref = rest[-4:]
        gv = g_ref[...]
        mn = ADAM_B1 * m_ref[...] + (1.0 - ADAM_B1) * gv
        vn = ADAM_B2 * v_ref[...] + (1.0 - ADAM_B2) * (gv * gv)
        m_hat = mn / (1.0 - ADAM_B1 ** ADAM_STEP)
        v_hat = vn / (1.0 - ADAM_B2 ** ADAM_STEP)
        go_ref[...] = gv
        d_ref[...] = -ADAM_LR * (m_hat / (jnp.sqrt(v_hat) + ADAM_EPS) + ADAM_WD * w_ref[...])
        nm_ref[...] = mn
        nv_ref[...] = vn

    lay = pl.BlockSpec((None, tr, cs), lambda i: (layer, i, 0))
    in_specs = [lay, pl.BlockSpec((tr, cs), lambda i: (i, 0)), lay, lay]
    args = [w, g, m, v]
    aliases = {}
    if bufs is not None:
        in_specs += [pl.BlockSpec(memory_space=pl.ANY)] * 4
        args += list(bufs)
        aliases = {4 + i: i for i in range(4)}
    return pl.pallas_call(
        body, grid=(r // tr,), in_specs=in_specs, out_specs=[lay] * 4, out_shape=[_sds((nl, r, cs), F32)] * 4,
        input_output_aliases=aliases, name=name, compiler_params=_cp("parallel"),
    )(*args)


def _slot_sum(own, slots, name):
    n_slots, rows, cols = slots.shape
    whole_fits = rows * cols * 4 * (n_slots + 2) * 2 <= VMEM_LIMIT_BYTES // 2
    tr = rows if whole_fits else _pick(rows, (512, 352, 256, 128, 64, 32, 16, 8))

    def body(*refs):
        s_ref, o_ref = refs[-2], refs[-1]
        acc = s_ref[0].astype(F32) if own is None else refs[0][...].astype(F32) + s_ref[0].astype(F32)
        for k in range(1, n_slots):
            acc = acc + s_ref[k].astype(F32)
        o_ref[...] = acc

    row = pl.BlockSpec((tr, cols), lambda i: (i, 0))
    return pl.pallas_call(
        body, grid=(rows // tr,),
        in_specs=([] if own is None else [row]) + [pl.BlockSpec((n_slots, tr, cols), lambda i: (0, i, 0))],
        out_specs=row, out_shape=_sds((rows, cols), F32), name=name, compiler_params=_cp("parallel"),
    )(*(() if own is None else (own,)), slots)


def _me():
    return lax.axis_index("x"), lax.axis_index("y"), lax.axis_index("c")


def _other_chips(x, y):
    return [(1 - x, y), (x, 1 - y), (1 - x, 1 - y)]


ANY_SPEC = pl.BlockSpec(memory_space=pl.ANY)
HBM_SPEC = pl.BlockSpec(memory_space=pltpu.HBM)
SEM_SPEC = pl.BlockSpec(memory_space=pltpu.SEMAPHORE)
SPLIT_COPY_PARAMS = pltpu.CompilerParams(has_side_effects=pltpu.SideEffectType.DATAFLOW_SIDE_EFFECTING)
N_CHIPS = 4


def _aligned(v, m):
    return v if isinstance(v, int) else pl.multiple_of(v, m)


def _in_hbm(arr):
    return pltpu.with_memory_space_constraint(arr, pltpu.HBM)


def _peer(x, y, c, k):
    fx, fy, fc = (k >> 2) & 1, (k >> 1) & 1, k & 1
    px = 1 - x if fx else x
    py = 1 - y if fy else y
    pc = 1 - c if fc else c
    return px, py, pc


def _gather_start(shards, after, name):
    n = len(shards)

    def body(*refs):
        src, land = refs[:n], refs[n:2 * n]
        send_sems, recv_sems = refs[2 * n + 1], refs[2 * n + 2]
        token = refs[-1]
        x, y, c = _me()
        for w in range(n):
            for chip in _other_chips(x, y):
                pltpu.make_async_remote_copy(
                    src_ref=src[w], dst_ref=land[w].at[2 * x + y], send_sem=send_sems.at[w], recv_sem=recv_sems.at[w],
                    device_id=(*chip, c), device_id_type=MESH_ID).start()
        token[...] = jnp.zeros_like(token)

    lands = [lax.empty((N_CHIPS,) + s.shape, s.dtype) for s in shards]
    thru = [pltpu.HBM(s.shape, s.dtype) for s in shards] + [pltpu.HBM(z.shape, z.dtype) for z in lands]
    outs = pl.pallas_call(
        body, name=name,
        out_shape=(pltpu.SemaphoreType.DMA((n,)), pltpu.SemaphoreType.DMA((n,)), *thru, _sds((8, 128), F32)),
        in_specs=[HBM_SPEC] * (2 * n) + [ANY_SPEC],
        out_specs=(SEM_SPEC, SEM_SPEC, *[HBM_SPEC] * (2 * n), pl.BlockSpec(memory_space=pltpu.VMEM)),
        input_output_aliases={i: 2 + i for i in range(2 * n)}, compiler_params=SPLIT_COPY_PARAMS,
    )(*[_in_hbm(s) for s in shards], *[_in_hbm(z) for z in lands], after)
    return (outs[0], outs[1], outs[2:2 + n], outs[2 + n:2 + 2 * n]), outs[-1]


def _gather_wait(send_sems, recv_sems, srcs, lands, after, name):
    n = len(srcs)

    def body(*refs):
        land = refs[n:2 * n]
        send_sems, recv_sems = refs[2 * n], refs[2 * n + 1]
        x, y, c = _me()
        for w in range(n):
            three = land[w].at[pl.ds(0, N_CHIPS - 1)]
            cp = pltpu.make_async_remote_copy(src_ref=three, dst_ref=three, send_sem=send_sems.at[w], recv_sem=recv_sems.at[w],
                                              device_id=(x, y, c), device_id_type=MESH_ID)
            cp.wait_send()
            cp.wait_recv()

    both = list(srcs) + list(lands)
    outs = pl.pallas_call(
        body, name=name, out_shape=tuple(pltpu.HBM(b.shape, b.dtype) for b in both),
        in_specs=[HBM_SPEC] * (2 * n) + [SEM_SPEC, SEM_SPEC, ANY_SPEC], out_specs=[HBM_SPEC] * (2 * n),
        input_output_aliases={i: i for i in range(2 * n)}, compiler_params=SPLIT_COPY_PARAMS,
    )(*both, send_sems, recv_sems, after)
    return outs[n:2 * n]


def _push_start(grads, small, name):
    n = len(grads)
    srcs = list(grads) + ([] if small is None else [small])
    ns = len(srcs)

    def body(*refs):
        src, slots = refs[:ns], refs[ns:2 * ns]
        send_sems, recv_sems = refs[2 * ns], refs[2 * ns + 1]
        token = refs[-1]
        x, y, c = _me()
        for w in range(ns):
            for k in range(1, N_DEV):
                px, py, pc = _peer(x, y, c, k)
                if w < n:
                    hr = src[w].shape[1] // 2
                    piece = src[w].at[2 * px + py, pl.ds(_aligned(pc * hr, 16), hr), :]
                    slot = slots[w].at[k - 1]
                else:
                    piece = src[w]
                    slot = slots[w].at[4 * x + 2 * y + c]
                pltpu.make_async_remote_copy(
                    src_ref=piece, dst_ref=slot, send_sem=send_sems.at[w], recv_sem=recv_sems.at[w],
                    device_id=(px, py, pc), device_id_type=MESH_ID).start()
        token[...] = jnp.zeros_like(token)

    slots = [lax.empty((N_DEV - 1, g.shape[1] // 2, g.shape[2]), g.dtype) for g in grads]
    if small is not None:
        slots.append(lax.empty((N_DEV,) + small.shape, small.dtype))
    both = srcs + slots
    outs = pl.pallas_call(
        body, name=name,
        out_shape=(pltpu.SemaphoreType.DMA((ns,)), pltpu.SemaphoreType.DMA((ns,)),
                   *[pltpu.HBM(b.shape, b.dtype) for b in both], _sds((8, 128), F32)),
        in_specs=[HBM_SPEC] * len(both),
        out_specs=(SEM_SPEC, SEM_SPEC, *[HBM_SPEC] * len(both), pl.BlockSpec(memory_space=pltpu.VMEM)),
        input_output_aliases={i: 2 + i for i in range(len(both))}, compiler_params=SPLIT_COPY_PARAMS,
    )(*[_in_hbm(b) for b in both])
    return (outs[0], outs[1], outs[2:2 + ns], outs[2 + ns:2 + 2 * ns]), outs[-1]


def _push_wait(send_sems, recv_sems, srcs, slots, after, name):
    n = len(srcs)

    def body(*refs):
        slot = refs[n:2 * n]
        send_sems, recv_sems = refs[2 * n], refs[2 * n + 1]
        x, y, c = _me()
        for w in range(n):
            seven = slot[w].at[pl.ds(0, N_DEV - 1)]
            cp = pltpu.make_async_remote_copy(src_ref=seven, dst_ref=seven, send_sem=send_sems.at[w],
                                              recv_sem=recv_sems.at[w], device_id=(x, y, c), device_id_type=MESH_ID)
            cp.wait_send()
            cp.wait_recv()

    both = list(srcs) + list(slots)
    outs = pl.pallas_call(
        body, name=name, out_shape=tuple(pltpu.HBM(b.shape, b.dtype) for b in both),
        in_specs=[HBM_SPEC] * (2 * n) + [SEM_SPEC, SEM_SPEC, ANY_SPEC], out_specs=[HBM_SPEC] * (2 * n),
        input_output_aliases={i: i for i in range(2 * n)}, compiler_params=SPLIT_COPY_PARAMS,
    )(*both, send_sems, recv_sems, after)
    return outs[:n], outs[n:]


SWAP_CHUNK_BYTES = 2 * 1024 * 1024


def _swap_chunk_rows(hr, cs):
    ch = hr
    while ch * cs * 4 > SWAP_CHUNK_BYTES and ch % 16 == 0:
        ch //= 2
    return ch


def _swap_halves(halves, name):
    n = len(halves)
    chunk = [_swap_chunk_rows(*h.shape) for h in halves]
    rounds = max(h.shape[0] // ch for h, ch in zip(halves, chunk))

    def body(*refs):
        src, dst, buf = refs[:n], refs[n:2 * n], refs[2 * n:3 * n]
        load_sems, put_sems, send_sems, recv_sems = refs[3 * n:]
        x, y, c = _me()
        sibling = (x, y, 1 - c)
        for j in range(rounds):
            live = [w for w in range(n) if j < src[w].shape[0] // chunk[w]]
            loads = [pltpu.make_async_copy(src[w].at[pl.ds(j * chunk[w], chunk[w])], buf[w], load_sems.at[w]) for w in live]
            for ld in loads:
                ld.start()
            moves = []
            for ld, w in zip(loads, live):
                ld.wait()
                rows = pl.ds(_aligned(c * src[w].shape[0] + j * chunk[w], 8), chunk[w])
                put = pltpu.make_async_copy(buf[w], dst[w].at[rows], put_sems.at[w])
                send = pltpu.make_async_remote_copy(src_ref=buf[w], dst_ref=dst[w].at[rows], send_sem=send_sems.at[w],
                                                    recv_sem=recv_sems.at[w], device_id=sibling, device_id_type=MESH_ID)
                put.start()
                send.start()
                moves.append((put, send))
            for put, send in moves:
                put.wait()
                send.wait_send()
        for w in range(n):
            hr = src[w].shape[0]
            got = dst[w].at[pl.ds(_aligned((1 - c) * hr, 8), hr)]
            pltpu.make_async_remote_copy(src_ref=got, dst_ref=got, send_sem=send_sems.at[w], recv_sem=recv_sems.at[w],
                                         device_id=sibling, device_id_type=MESH_ID).wait_recv()

    return pl.pallas_call(
        body, in_specs=[ANY_SPEC] * n, out_specs=[ANY_SPEC] * n,
        out_shape=[_sds((2 * h.shape[0], h.shape[1]), F32) for h in halves],
        scratch_shapes=[pltpu.VMEM((ch, h.shape[1]), F32) for h, ch in zip(halves, chunk)]
        + [pltpu.SemaphoreType.DMA((n,))] * 4,
        name=name,
    )(*halves)


BIG = ("w_in", "w_out", "w_up", "w_down", "w_pe", "w_pg")
ROW_SHARDED = ("w_out", "w_down", "w_pg")
SMALL = ("norm1_g", "a_ln_g", "a_ln_b", "a_ws", "a_bs", "b_conv_w", "b_conv_b", "b_wa", "b_ba", "b_wx", "b_bx", "b_lam",
         "c_lb", "c_norm_g", "d_w", "d_scale", "norm2_g", "ffn_conv_w", "ffn_conv_b", "norm3_g", "final_g")
SMALL_SHARDED = ("b_conv_w", "ffn_conv_w")
WEIGHTS = ("norm1_g", "w_in", "a_ln_g", "a_ln_b", "a_ws", "a_bs", "b_conv_w", "b_conv_b", "b_wa", "b_ba", "b_wx", "b_bx",
           "b_lam", "c_lb", "c_norm_g", "d_w", "d_scale", "w_out", "norm2_g", "w_up", "ffn_conv_w", "ffn_conv_b", "w_down",
           "norm3_g", "w_pe", "w_pg", "final_g")
ARGS = ("x", "p") + WEIGHTS + ("loss_target",) + tuple("m_" + n for n in WEIGHTS) + tuple("v_" + n for n in WEIGHTS)


def _block_diag(w):
    eye = jnp.eye(N_HEADS, dtype=w.dtype)
    return (eye[None, :, None, :, None] * w[:, :, :, None, :]).reshape(w.shape[0], W_GRP, W_GRP)


def _diag_blocks(m):
    m4 = m.reshape(N_HEADS, HEAD_DIM, N_HEADS, HEAD_DIM)
    return jnp.stack([m4[h, :, h, :] for h in range(N_HEADS)])


def _lower_bounds(c_lb):
    lbs = jnp.cumsum(jax.nn.softmax(c_lb, axis=0), axis=0)
    return lbs - lbs[0:1]


def kernel(x, p, norm1_g, w_in, a_ln_g, a_ln_b, a_ws, a_bs, b_conv_w, b_conv_b, b_wa, b_ba, b_wx, b_bx, b_lam, c_lb, c_norm_g, d_w, d_scale, w_out, norm2_g, w_up, ffn_conv_w, ffn_conv_b, w_down, norm3_g, w_pe, w_pg, final_g, loss_target, m_norm1_g, m_w_in, m_a_ln_g, m_a_ln_b, m_a_ws, m_a_bs, m_b_conv_w, m_b_conv_b, m_b_wa, m_b_ba, m_b_wx, m_b_bx, m_b_lam, m_c_lb, m_c_norm_g, m_d_w, m_d_scale, m_w_out, m_norm2_g, m_w_up, m_ffn_conv_w, m_ffn_conv_b, m_w_down, m_norm3_g, m_w_pe, m_w_pg, m_final_g, v_norm1_g, v_w_in, v_a_ln_g, v_a_ln_b, v_a_ws, v_a_bs, v_b_conv_w, v_b_conv_b, v_b_wa, v_b_ba, v_b_wx, v_b_bx, v_b_lam, v_c_lb, v_c_norm_g, v_d_w, v_d_scale, v_w_out, v_norm2_g, v_w_up, v_ffn_conv_w, v_ffn_conv_b, v_w_down, v_norm3_g, v_w_pe, v_w_pg, v_final_g):
    return _step((x, p, norm1_g, w_in, a_ln_g, a_ln_b, a_ws, a_bs, b_conv_w, b_conv_b, b_wa, b_ba, b_wx, b_bx, b_lam, c_lb, c_norm_g, d_w, d_scale, w_out, norm2_g, w_up, ffn_conv_w, ffn_conv_b, w_down, norm3_g, w_pe, w_pg, final_g, loss_target, m_norm1_g, m_w_in, m_a_ln_g, m_a_ln_b, m_a_ws, m_a_bs, m_b_conv_w, m_b_conv_b, m_b_wa, m_b_ba, m_b_wx, m_b_bx, m_b_lam, m_c_lb, m_c_norm_g, m_d_w, m_d_scale, m_w_out, m_norm2_g, m_w_up, m_ffn_conv_w, m_ffn_conv_b, m_w_down, m_norm3_g, m_w_pe, m_w_pg, m_final_g, v_norm1_g, v_w_in, v_a_ln_g, v_a_ln_b, v_a_ws, v_a_bs, v_b_conv_w, v_b_conv_b, v_b_wa, v_b_ba, v_b_wx, v_b_bx, v_b_lam, v_c_lb, v_c_norm_g, v_d_w, v_d_scale, v_w_out, v_norm2_g, v_w_up, v_ffn_conv_w, v_ffn_conv_b, v_w_down, v_norm3_g, v_w_pe, v_w_pg, v_final_g))


SMALL_PER_LAYER = tuple(n for n in SMALL if n != "final_g")
GATHERED = BIG + SMALL_SHARDED
GATHER_LAYER0 = (("a", ("w_in", "b_conv_w")), ("b", ("w_out", "w_up", "ffn_conv_w", "w_down", "w_pe", "w_pg")))
PUSH_EARLY = ("w_pe", "w_pg", "w_down", "w_up")
PUSH_MID = ("w_out",)
PUSH_LATE = ("w_in",)
SMALL_MID = tuple(n for n in SMALL_PER_LAYER if n != "norm1_g")


def _cols_to_slabs(m):
    r, c4 = m.shape
    return jnp.moveaxis(m.reshape(r, N_CHIPS, c4 // N_CHIPS), 1, 0)


def _slabs_to_cols(s):
    return jnp.moveaxis(s, 0, 1).reshape(s.shape[1], -1)


def _pack_small(parts):
    flat = jnp.concatenate([p.reshape(-1) for p in parts])
    return jnp.pad(flat, (0, (-flat.shape[0]) % 1024)).reshape(-1, 128)


def _step(args):
    a = dict(zip(ARGS, args, strict=True))
    x0 = a["x"][0]
    target = a["loss_target"][0]
    nl = a["norm1_g"].shape[0]
    t, d = x0.shape
    f = a["w_down"].shape[1] * N_CHIPS
    cx, cy, cc = _me()
    my_shard = 2 * cx + cy
    shards = {n: a[n].astype(BF16) for n in BIG}
    shards.update({n: a[n] for n in SMALL_SHARDED})

    def start_gather(l, names, after, tag):
        return _gather_start([shards[n][l] for n in names], after, f"gather_start_{l}{tag}")

    def finish_gather(l, names, handle, after, tag):
        send, recv, srcs, lands = handle
        lands = _gather_wait(send, recv, srcs, lands, after, f"gather_wait_{l}{tag}")
        w = {}
        for n, land in zip(names, lands):
            full = lax.dynamic_update_slice(land, shards[n][l][None], (my_shard, 0, 0))
            if n in ROW_SHARDED:
                w[n] = full.reshape(-1, full.shape[-1])
            elif n in ("w_up", "w_pe"):
                w[n] = full
            else:
                w[n] = _slabs_to_cols(full)
        return w

    lbs, lbs_vjp = jax.vjp(_lower_bounds, a["c_lb"])
    tril = jnp.tril(jnp.ones((GMLP_CHUNK, GMLP_CHUNK), F32))

    stacked_params = {"wm": (a["a_ws"] * tril).astype(BF16)}
    stacked_params["wm_t"] = jnp.swapaxes(stacked_params["wm"], 2, 3)
    stacked_params["bs_t"] = jnp.repeat(jnp.swapaxes(a["a_bs"], 1, 2), HEAD_DIM, axis=2)
    for nm in ("b_wa", "b_wx", "d_w"):
        bd = _block_diag(a[nm]).astype(BF16)
        stacked_params[nm], stacked_params[nm + "_t"] = bd, jnp.swapaxes(bd, 1, 2)
    for nm in ("a_ln_g", "a_ln_b", "b_conv_b", "b_ba", "b_bx", "b_lam", "d_scale"):
        stacked_params[nm] = a[nm].reshape(nl, 1, W_GRP)
    stacked_params["lb"] = lbs.reshape(nl, 1, W_GRP)
    stacked_params["ng"] = jnp.tile(a["c_norm_g"], (1, N_HEADS)).reshape(nl, 1, W_GRP)
    stacked_params["ffn_conv_b"] = a["ffn_conv_b"].reshape(nl, 1, 2 * f)

    def layer_params(l, w):
        q = {k: v[l] for k, v in stacked_params.items()}
        q.update(w)
        return q

    saved, weights, params = [], [], []
    first_groups = {tag: start_gather(0, names, x0, tag)[0] for tag, names in GATHER_LAYER0}
    xl = x0
    for l in range(nl):
        if l == 0:
            w = finish_gather(0, GATHER_LAYER0[0][1], first_groups["a"], xl, "a")
        else:
            w = finish_gather(l, GATHERED, next_handle, xl, "")
        s = {"x0": xl}
        s["h1"] = _rms_fwd(xl, a["norm1_g"][0], "rms1_fwd") if l == 0 else h_next
        token = None
        if 0 < l < nl - 1:
            next_handle, token = start_gather(l + 1, GATHERED, s["h1"], "")
        s["z"] = _mm(s["h1"], w["w_in"], "nn", out_dtype=F32, name="mm_z", after=token)
        q = layer_params(l, w)
        mix = _mix_a_fwd(s["z"], d, q["a_ln_g"], q["a_ln_b"], q["wm"], q["bs_t"])
        mix, s["hs"] = _mix_b_fwd(s["z"], mix, q["b_conv_w"], q["b_conv_b"], q["b_wa"], q["b_wx"], q["b_ba"], q["b_bx"],
                                  q["b_lam"])
        mix, s["o_pre"], s["states"] = _mix_c_fwd(s["z"], mix, q["lb"], q["ng"])
        s["mix"] = _mix_d_fwd(s["z"], mix, q["d_w"], q["d_scale"])
        def land(tag, after):
            if l == 0:
                w.update(finish_gather(0, dict(GATHER_LAYER0)[tag], first_groups[tag], after, tag))
                q.update(w)

        land("b", s["mix"])
        token = None
        if l == 0 and nl > 1:
            next_handle, token = start_gather(1, GATHERED, w["w_out"], "")
        s["x1"], s["h2"] = _mm(s["mix"], w["w_out"], "nn", res=xl, out_dtype=F32, name="mm_out",
                               norm_g=a["norm2_g"][l], tm_max=1024, after=token)
        s["hf_g"] = _mm(s["h2"], w["w_up"], "nn", b_slabs=True, n=f, out_dtype=F32, name="mm_up_g")
        s["hf_v"] = _mm(s["h2"], w["w_up"], "nn", b_slabs=True, n=f, b_noff=f, out_dtype=F32, name="mm_up_v")
        s["act"], s["gc"], s["vc"] = _ffn_act_fwd(s["hf_g"], s["hf_v"], q["ffn_conv_w"], q["ffn_conv_b"])
        s["x2"], s["h3"] = _mm(s["act"], w["w_down"], "nn", res=s["x1"], out_dtype=F32, name="mm_down",
                               norm_g=a["norm3_g"][l], tm_max=1024)
        s["pre"] = _mm(s["h3"], w["w_pg"], "nn", out_dtype=F32, name="mm_pg")
        s["pe"] = _mm(a["p"][l, 0], w["w_pe"], "nn", b_slabs=True, out_dtype=F32, name="mm_pe")
        xl, h_next = _ple_fwd(s["x2"], s["pe"], s["pre"], a["norm1_g"][l + 1] if l + 1 < nl else None)
        saved.append(s)
        weights.append(w)
        params.append(q)

    dx, g_final, loss = _final_loss(xl, a["final_g"], target)
    loss = lax.psum(loss[0, 0], ("x", "y", "c"))

    stacked = {n: None for n in BIG}
    small_sums = {}

    def finish_push(l, names, handle, tag, after):
        send, recv, srcs, slots = handle
        srcs, slots = _push_wait(send, recv, srcs, slots, after, f"push_wait_{l}{tag}")
        halves = []
        for n, g, sl in zip(names, srcs, slots):
            hr = g.shape[1] // 2
            own = lax.dynamic_slice(g, (my_shard, cc * hr, 0), (1, hr, g.shape[2]))[0]
            halves.append(_slot_sum(own, sl, "sum_" + n))
        for n, g in zip(names, _swap_halves(halves, "swap_halves_" + tag)):
            stacked[n] = _adamw_layer(a[n], g, a["m_" + n], a["v_" + n], l, stacked[n], "adamw_" + n)
        if len(srcs) > len(names):
            by_sender = lax.dynamic_update_slice(slots[-1], srcs[-1][None], (2 * my_shard + cc, 0, 0))
            small_sums[l, tag] = _slot_sum(None, by_sender, "sum_small_" + tag)
        return stacked[names[-1]][1]

    pending = []
    token = None
    for l in reversed(range(nl)):
        q, s, w = params[l], saved[l], weights[l]
        gs = {}
        dpe, dpre = _ple_bwd(dx, s["pe"], s["pre"], after=token)
        g_pe = _mm(a["p"][l, 0], dpe, "tn", out_dtype=BF16, name="mm_dwpe", out_slabs=N_CHIPS)
        g_pg = _mm(s["h3"], dpre, "tn", out_dtype=BF16, name="mm_dwpg")
        dx2, gs["norm3_g"] = _mm(dpre, w["w_pg"], "nt", out_dtype=F32, name="mm_dh3", tm_max=1024,
                                 rms_bwd=(s["x2"], a["norm3_g"][l], dx))
        g_down = _mm(s["act"], dx2, "tn", out_dtype=BF16, name="mm_dwdown")
        dhf_g, dhf_v, sums_g, sums_v = _ffn_bwd(dx2, w["w_down"], s["gc"], s["vc"], s["hf_g"], s["hf_v"], q["ffn_conv_w"])
        gs["ffn_conv_w"] = jnp.concatenate([sums_g[0:3], sums_v[0:3]], axis=1)
        gs["ffn_conv_b"] = jnp.concatenate([sums_g[3:4], sums_v[3:4]], axis=1)
        g_up = _mm(s["h2"], dhf_g, "tn", out_dtype=BF16, name="mm_dwup_g", out_slabs=N_CHIPS, out_n=2 * f)
        g_up = _mm(s["h2"], dhf_v, "tn", out_dtype=BF16, name="mm_dwup_v", out_slabs=N_CHIPS, out_n=2 * f, o_noff=f, out_buf=g_up)
        early = {"w_pe": g_pe, "w_up": g_up, "w_pg": g_pg.reshape(N_CHIPS, -1, g_pg.shape[-1]),
                 "w_down": g_down.reshape(N_CHIPS, -1, g_down.shape[-1])}
        early_handle, token = _push_start([early[n] for n in PUSH_EARLY], None, f"push_start_{l}a")
        dh2 = _mm(dhf_g, w["w_up"], "nt", b_slabs=True, out_dtype=F32, name="mm_dh2_g", after=token)
        dx1, gs["norm2_g"] = _mm(dhf_v, w["w_up"], "nt", b_slabs=True, b_koff=f, res=dh2, out_dtype=F32, name="mm_dh2_v",
                                 tm_max=1024, rms_bwd=(s["x1"], a["norm2_g"][l], dx2))
        g_out = _mm(s["mix"], dx1, "tn", out_dtype=BF16, name="mm_dwout")
        dmix = _mm(dx1, w["w_out"], "nt", out_dtype=F32, name="mm_dmix")
        dz, gs["a_ln_g"], gs["a_ln_b"], dws, dbs_t = _mix_a_bwd(s["z"], dmix, q["a_ln_g"], q["a_ln_b"], q["wm"], q["wm_t"],
                                                               q["bs_t"])
        gs["a_ws"] = dws * tril
        gs["a_bs"] = dbs_t.reshape(GMLP_CHUNK, N_HEADS, HEAD_DIM).sum(-1).T
        dz, gs["b_conv_w"], gs["b_conv_b"], dwa, dwx, gs["b_ba"], gs["b_bx"], gs["b_lam"] = _mix_b_bwd(
            s["z"], dz, dmix, s["hs"], q["b_conv_w"], q["b_conv_b"], q["b_wa"], q["b_wx"], q["b_wa_t"], q["b_wx_t"],
            q["b_ba"], q["b_bx"], q["b_lam"])
        gs["b_wa"], gs["b_wx"] = _diag_blocks(dwa), _diag_blocks(dwx)
        dz, gs["c_lb"], dng = _mix_c_bwd(s["z"], dz, dmix, s["o_pre"], s["states"], q["lb"], q["ng"])
        gs["c_norm_g"] = dng.reshape(N_HEADS, HEAD_DIM).sum(0)
        dz, dwd, gs["d_scale"] = _mix_d_bwd(s["z"], dz, dmix, q["d_w"], q["d_w_t"], q["d_scale"])
        gs["d_w"] = _diag_blocks(dwd)
        small = [gs[n] for n in SMALL_MID] + ([g_final] if l == nl - 1 else [])
        out_slabs = g_out.reshape(N_CHIPS, -1, g_out.shape[-1])
        if l == 0:
            mid_handle, token = _push_start([out_slabs], _pack_small(small), "push_start_0b")
        g_in = _mm(s["h1"], dz, "tn", out_dtype=BF16, name="mm_dwin")
        dx, gs["norm1_g"] = _mm(dz, w["w_in"], "nt", out_dtype=F32, name="mm_dh1", after=(token, g_in), tm_max=1024,
                                rms_bwd=(s["x0"], a["norm1_g"][l], dx1))

        if l == 0:
            late_names, late_slabs, late_small = PUSH_LATE, [_cols_to_slabs(g_in)], [gs["norm1_g"]]
        else:
            late_names, late_slabs, late_small = PUSH_MID + PUSH_LATE, [out_slabs, _cols_to_slabs(g_in)], [gs["norm1_g"]] + small
        late_handle, token = _push_start(late_slabs, _pack_small(late_small), f"push_start_{l}c")
        dep = token
        for push in pending:
            dep = finish_push(*push, dep)
        pending = [(l, PUSH_EARLY, early_handle, "a")] + ([(l, PUSH_MID, mid_handle, "b")] if l == 0 else [])
        pending.append((l, late_names, late_handle, "c"))
    for push in pending:
        dep = finish_push(*push, dep)
    grad_x = dx[None]

    def small_shape(n):
        return a[n].shape[1:-1] + (a[n].shape[-1] * N_CHIPS,) if n in SMALL_SHARDED else a[n].shape[1:]

    per_layer = {n: [] for n in SMALL_PER_LAYER}
    for l in range(nl):
        late_vec = small_sums[l, "c"].reshape(-1)
        per_layer["norm1_g"].append(late_vec[:d])
        vec, off = (small_sums[l, "b"].reshape(-1), 0) if l == 0 else (late_vec, d)
        for n in SMALL_MID:
            shape = small_shape(n)
            size = 1
            for dim in shape:
                size *= dim
            per_layer[n].append(vec[off:off + size].reshape(shape))
            off += size
        if l == nl - 1:
            grad_final = vec[off:off + d]
    grads = {n: jnp.stack(per_layer[n]) for n in SMALL_PER_LAYER}
    grads["c_lb"] = lbs_vjp(grads["c_lb"])[0]
    grads["final_g"] = grad_final
    for n in SMALL_SHARDED:
        cs = a[n].shape[-1]
        grads[n] = lax.dynamic_slice_in_dim(grads[n], my_shard * cs, cs, axis=2)

    outs = {}
    for n in WEIGHTS:
        if n in BIG:
            outs[n] = stacked[n]
        else:
            outs[n] = (grads[n],) + _adamw(a[n], grads[n], a["m_" + n], a["v_" + n], "adamw_" + n)
    return (loss, grad_x, *[outs[n][0] for n in WEIGHTS], *[outs[n][1] for n in WEIGHTS], *[outs[n][2] for n in WEIGHTS],
            *[outs[n][3] for n in WEIGHTS])
```
